```python
import jax, jax.numpy as jnp
from jax import lax
import numpy as np

D_MODEL = 1024
BATCH = 8
SEQ = 8192
DEPTH = 2

HEAD_DIM = 128
HEADS_PER_GROUP = 4
DILATION_GROUPS = ((128, 1), (512, 4), (2048, 16))
N_GROUPS = 3
ATT_WIDTH = HEADS_PER_GROUP * HEAD_DIM
QKV_WIDTH = N_GROUPS * ATT_WIDTH
BAND_BLOCK = 128
LRU_WIDTH = D_MODEL
LRU_BLOCKS = 16
LRU_BLOCK_DIM = LRU_WIDTH // LRU_BLOCKS
CONV_WIDTH = 4
LRU_C = 8.0
N_BRANCHES = 2
NORM_EPS = 1e-6
NEG_INF = -1e30
SPLIT_SIZES = (QKV_WIDTH, QKV_WIDTH, QKV_WIDTH, ATT_WIDTH, LRU_WIDTH, LRU_WIDTH, N_BRANCHES * D_MODEL)
IN_WIDTH = 3 * QKV_WIDTH + ATT_WIDTH + 2 * LRU_WIDTH + N_BRANCHES * D_MODEL

kernel_name = "hybrid_dilated_attn_rglru_block"


def rms_norm(x, gain):
    xf = x.astype(jnp.float32)
    y = xf * lax.rsqrt(jnp.mean(xf * xf, axis=-1, keepdims=True) + NORM_EPS)
    return (y * gain.astype(jnp.float32)).astype(x.dtype)


def dilated_window_group(q, k, v, window, dilation):
    B, S, H, Dh = q.shape
    span = window // dilation
    L = S // dilation
    n_blk = -(-L // BAND_BLOCK)
    Lp = n_blk * BAND_BLOCK

    def to_sub(t):
        t = t.astype(jnp.float32).reshape(B, L, dilation, H, Dh).transpose(0, 2, 1, 3, 4)
        return jnp.pad(t, ((0, 0), (0, 0), (0, Lp - L), (0, 0), (0, 0)))

    def band(t):
        t = jnp.pad(to_sub(t), ((0, 0), (0, 0), (BAND_BLOCK, 0), (0, 0), (0, 0)))
        t = t.reshape(B, dilation, n_blk + 1, BAND_BLOCK, H, Dh)
        return jnp.concatenate([t[:, :, :-1], t[:, :, 1:]], axis=3)

    qs = to_sub(q).reshape(B, dilation, n_blk, BAND_BLOCK, H, Dh)
    kb = band(k)
    vb = band(v)
    s = jnp.einsum('brnqhd,brnkhd->brnhqk', qs, kb) * (Dh ** -0.5)
    qi = jnp.arange(BAND_BLOCK)[:, None] + BAND_BLOCK
    ki = jnp.arange(2 * BAND_BLOCK)[None, :]
    dist = qi - ki
    key_pos = jnp.arange(n_blk)[:, None, None] * BAND_BLOCK + ki[None] - BAND_BLOCK
    valid = (dist >= 0) & (dist <= span) & (key_pos >= 0)
    s = jnp.where(valid[None, None, :, None], s, NEG_INF)
    m = jnp.max(s, axis=-1)
    p = jnp.exp(s - m[..., None])
    l = jnp.sum(p, axis=-1)
    m = jnp.swapaxes(m, -1, -2)
    l = jnp.swapaxes(l, -1, -2)
    o = jnp.einsum('brnhqk,brnkhd->brnqhd', p, vb) / l[..., None]

    def from_sub(t):
        t = t.reshape((B, dilation, Lp) + t.shape[4:])[:, :, :L]
        t = jnp.swapaxes(t, 1, 2)
        return t.reshape((B, S) + t.shape[3:])

    return from_sub(o), from_sub(m), from_sub(l)


def dilated_attention(q, k, v):
    B, S, _ = q.shape
    shp = (B, S, N_GROUPS, HEADS_PER_GROUP, HEAD_DIM)
    qg, kg, vg = q.reshape(shp), k.reshape(shp), v.reshape(shp)
    outs, maxes, dens = [], [], []
    for g, (window, dilation) in enumerate(DILATION_GROUPS):
        o, m, l = dilated_window_group(qg[:, :, g], kg[:, :, g], vg[:, :, g], window, dilation)
        outs.append(o)
        maxes.append(m)
        dens.append(l)
    o = jnp.stack(outs, 0)
    m = jnp.stack(maxes, 0)
    l = jnp.stack(dens, 0)
    wgt = l * jnp.exp(m - jnp.max(m, axis=0, keepdims=True))
    o = jnp.sum(wgt[..., None] * o, axis=0) / jnp.sum(wgt, axis=0)[..., None]
    return o.reshape(B, S, ATT_WIDTH).astype(q.dtype)


def causal_depthwise_conv(u, w, b):
    S = u.shape[1]
    up = jnp.pad(u, ((0, 0), (CONV_WIDTH - 1, 0), (0, 0)))
    y = b
    for j in range(CONV_WIDTH):
        y = y + up[:, CONV_WIDTH - 1 - j: CONV_WIDTH - 1 - j + S] * w[j]
    return y


def rg_lru(u, w_rg, b_rg, w_ig, b_ig, lru_lambda):
    B, S, _ = u.shape
    uf = u.astype(jnp.float32)
    ub = uf.reshape(B, S, LRU_BLOCKS, LRU_BLOCK_DIM)
    r = jax.nn.sigmoid(jnp.einsum('bshi,hij->bshj', ub, w_rg.astype(jnp.float32)).reshape(B, S, LRU_WIDTH) + b_rg.astype(jnp.float32))
    i = jax.nn.sigmoid(jnp.einsum('bshi,hij->bshj', ub, w_ig.astype(jnp.float32)).reshape(B, S, LRU_WIDTH) + b_ig.astype(jnp.float32))
    log_a = -LRU_C * r * jax.nn.softplus(-lru_lambda.astype(jnp.float32))
    a = jnp.exp(log_a)
    xin = jnp.sqrt(-jnp.expm1(2.0 * log_a)) * (i * uf)

    def combine(left, right):
        a1, b1 = left
        a2, b2 = right
        return a1 * a2, a2 * b1 + b2

    _, h = lax.associative_scan(combine, (a, xin), axis=1)
    return h.astype(u.dtype)


def hybrid_layer(x, c, w_mod, b_mod, g_pre, w_in, conv_w, conv_b, w_rg, b_rg, w_ig, b_ig,
                 lru_lambda, w_pa, w_pb, w_o, g_post):
    mod = jax.nn.silu(c) @ w_mod + b_mod
    shift, scale, gate = jnp.split(mod[:, None, :], 3, axis=-1)
    h = rms_norm(x, g_pre) * (1 + scale) + shift
    proj = h @ w_in
    split_at = [int(v) for v in np.cumsum(SPLIT_SIZES)[:-1]]
    q, k, v, g_att, u, g_lru, merge = jnp.split(proj, split_at, axis=-1)
    y_a = (dilated_attention(q, k, v) * jax.nn.silu(g_att)) @ w_pa
    u = causal_depthwise_conv(u, conv_w, conv_b)
    y_b = (rg_lru(u, w_rg, b_rg, w_ig, b_ig, lru_lambda) * jax.nn.silu(g_lru)) @ w_pb
    gate_a, gate_b = jnp.split(jax.nn.sigmoid(merge), N_BRANCHES, axis=-1)
    out = (gate_a * y_a + gate_b * y_b) @ w_o
    return x + gate * rms_norm(out, g_post)


def _fwd_setup_inputs(seed: int = 0) -> dict:
    key = jax.random.key(seed)
    ks = jax.random.split(key, 20)
    f32 = jnp.float32
    nrm = lambda k, shape, s: jax.random.normal(k, shape, f32) * s
    u = jax.random.uniform(ks[13], (DEPTH, LRU_WIDTH), f32, 0.9, 0.999)
    a_base = u ** (1.0 / LRU_C)
    lru_lambda = jnp.log(a_base) - jnp.log1p(-a_base)
    return {
        "x": nrm(ks[0], (BATCH, SEQ, D_MODEL), 1.0),
        "c": nrm(ks[1], (BATCH, D_MODEL), 1.0),
        "w_mod": nrm(ks[2], (DEPTH, D_MODEL, 3 * D_MODEL), 0.5 * D_MODEL ** -0.5),
        "b_mod": nrm(ks[3], (DEPTH, 3 * D_MODEL), 0.01),
        "g_pre": 1.0 + nrm(ks[4], (DEPTH, D_MODEL), 0.05),
        "w_in": nrm(ks[5], (DEPTH, D_MODEL, IN_WIDTH), D_MODEL ** -0.5),
        "conv_w": nrm(ks[6], (DEPTH, CONV_WIDTH, LRU_WIDTH), CONV_WIDTH ** -0.5),
        "conv_b": nrm(ks[7], (DEPTH, LRU_WIDTH), 0.01),
        "w_rg": nrm(ks[8], (DEPTH, LRU_BLOCKS, LRU_BLOCK_DIM, LRU_BLOCK_DIM), LRU_BLOCK_DIM ** -0.5),
        "b_rg": nrm(ks[9], (DEPTH, LRU_WIDTH), 0.01),
        "w_ig": nrm(ks[10], (DEPTH, LRU_BLOCKS, LRU_BLOCK_DIM, LRU_BLOCK_DIM), LRU_BLOCK_DIM ** -0.5),
        "b_ig": nrm(ks[11], (DEPTH, LRU_WIDTH), 0.01),
        "lru_lambda": lru_lambda,
        "w_pa": nrm(ks[14], (DEPTH, ATT_WIDTH, D_MODEL), ATT_WIDTH ** -0.5),
        "w_pb": nrm(ks[15], (DEPTH, LRU_WIDTH, D_MODEL), LRU_WIDTH ** -0.5),
        "w_o": nrm(ks[16], (DEPTH, D_MODEL, D_MODEL), D_MODEL ** -0.5),
        "g_post": 1.0 + nrm(ks[17], (DEPTH, D_MODEL), 0.05),
    }


def _fwd_reference(x, c, w_mod, b_mod, g_pre, w_in, conv_w, conv_b, w_rg, b_rg, w_ig, b_ig,
              lru_lambda, w_pa, w_pb, w_o, g_post):
    for layer in range(DEPTH):
        x = hybrid_layer(x, c, w_mod[layer], b_mod[layer], g_pre[layer], w_in[layer],
                         conv_w[layer], conv_b[layer], w_rg[layer], b_rg[layer],
                         w_ig[layer], b_ig[layer], lru_lambda[layer], w_pa[layer],
                         w_pb[layer], w_o[layer], g_post[layer])
    return x


import jax as _jax
import jax.numpy as _jnp

TWIN_FORMAT = 'train_step'
FWD_PARAMS = ['x', 'c', 'w_mod', 'b_mod', 'g_pre', 'w_in', 'conv_w', 'conv_b', 'w_rg', 'b_rg', 'w_ig', 'b_ig', 'lru_lambda', 'w_pa', 'w_pb', 'w_o', 'g_post']
TWIN_WEIGHTS = ['w_mod', 'b_mod', 'g_pre', 'w_in', 'conv_w', 'conv_b', 'w_rg', 'b_rg', 'w_ig', 'b_ig', 'lru_lambda', 'w_pa', 'w_pb', 'w_o', 'g_post']
TWIN_DIFF_INPUT = 'x'
TWIN_INPUTS = ['x', 'c', 'w_mod', 'b_mod', 'g_pre', 'w_in', 'conv_w', 'conv_b', 'w_rg', 'b_rg', 'w_ig', 'b_ig', 'lru_lambda', 'w_pa', 'w_pb', 'w_o', 'g_post', 'loss_target', 'm_w_mod', 'm_b_mod', 'm_g_pre', 'm_w_in', 'm_conv_w', 'm_conv_b', 'm_w_rg', 'm_b_rg', 'm_w_ig', 'm_b_ig', 'm_lru_lambda', 'm_w_pa', 'm_w_pb', 'm_w_o', 'm_g_post', 'v_w_mod', 'v_b_mod', 'v_g_pre', 'v_w_in', 'v_conv_w', 'v_conv_b', 'v_w_rg', 'v_b_rg', 'v_w_ig', 'v_b_ig', 'v_lru_lambda', 'v_w_pa', 'v_w_pb', 'v_w_o', 'v_g_post']
TWIN_OUTPUTS = ['loss', 'grad_x', 'grad_w_mod', 'grad_b_mod', 'grad_g_pre', 'grad_w_in', 'grad_conv_w', 'grad_conv_b', 'grad_w_rg', 'grad_b_rg', 'grad_w_ig', 'grad_b_ig', 'grad_lru_lambda', 'grad_w_pa', 'grad_w_pb', 'grad_w_o', 'grad_g_post', 'delta_w_mod', 'delta_b_mod', 'delta_g_pre', 'delta_w_in', 'delta_conv_w', 'delta_conv_b', 'delta_w_rg', 'delta_b_rg', 'delta_w_ig', 'delta_b_ig', 'delta_lru_lambda', 'delta_w_pa', 'delta_w_pb', 'delta_w_o', 'delta_g_post', 'new_m_w_mod', 'new_m_b_mod', 'new_m_g_pre', 'new_m_w_in', 'new_m_conv_w', 'new_m_conv_b', 'new_m_w_rg', 'new_m_b_rg', 'new_m_w_ig', 'new_m_b_ig', 'new_m_lru_lambda', 'new_m_w_pa', 'new_m_w_pb', 'new_m_w_o', 'new_m_g_post', 'new_v_w_mod', 'new_v_b_mod', 'new_v_g_pre', 'new_v_w_in', 'new_v_conv_w', 'new_v_conv_b', 'new_v_w_rg', 'new_v_b_rg', 'new_v_w_ig', 'new_v_b_ig', 'new_v_lru_lambda', 'new_v_w_pa', 'new_v_w_pb', 'new_v_w_o', 'new_v_g_post']
TWIN_LEAF_KINDS = {'loss': 'loss', 'grad_x': 'grad_x', 'grad_w_mod': 'grad_w', 'grad_b_mod': 'grad_w', 'grad_g_pre': 'grad_w', 'grad_w_in': 'grad_w', 'grad_conv_w': 'grad_w', 'grad_conv_b': 'grad_w', 'grad_w_rg': 'grad_w', 'grad_b_rg': 'grad_w', 'grad_w_ig': 'grad_w', 'grad_b_ig': 'grad_w', 'grad_lru_lambda': 'grad_w', 'grad_w_pa': 'grad_w', 'grad_w_pb': 'grad_w', 'grad_w_o': 'grad_w', 'grad_g_post': 'grad_w', 'delta_w_mod': 'delta_w', 'delta_b_mod': 'delta_w', 'delta_g_pre': 'delta_w', 'delta_w_in': 'delta_w', 'delta_conv_w': 'delta_w', 'delta_conv_b': 'delta_w', 'delta_w_rg': 'delta_w', 'delta_b_rg': 'delta_w', 'delta_w_ig': 'delta_w', 'delta_b_ig': 'delta_w', 'delta_lru_lambda': 'delta_w', 'delta_w_pa': 'delta_w', 'delta_w_pb': 'delta_w', 'delta_w_o': 'delta_w', 'delta_g_post': 'delta_w', 'new_m_w_mod': 'new_m', 'new_m_b_mod': 'new_m', 'new_m_g_pre': 'new_m', 'new_m_w_in': 'new_m', 'new_m_conv_w': 'new_m', 'new_m_conv_b': 'new_m', 'new_m_w_rg': 'new_m', 'new_m_b_rg': 'new_m', 'new_m_w_ig': 'new_m', 'new_m_b_ig': 'new_m', 'new_m_lru_lambda': 'new_m', 'new_m_w_pa': 'new_m', 'new_m_w_pb': 'new_m', 'new_m_w_o': 'new_m', 'new_m_g_post': 'new_m', 'new_v_w_mod': 'new_v', 'new_v_b_mod': 'new_v', 'new_v_g_pre': 'new_v', 'new_v_w_in': 'new_v', 'new_v_conv_w': 'new_v', 'new_v_conv_b': 'new_v', 'new_v_w_rg': 'new_v', 'new_v_b_rg': 'new_v', 'new_v_w_ig': 'new_v', 'new_v_b_ig': 'new_v', 'new_v_lru_lambda': 'new_v', 'new_v_w_pa': 'new_v', 'new_v_w_pb': 'new_v', 'new_v_w_o': 'new_v', 'new_v_g_post': 'new_v'}


def _forward(args):
    return _fwd_reference(*[args[k] for k in FWD_PARAMS])


def _output_shape():
    def fwd():
        inp = _fwd_setup_inputs(0)
        return _fwd_reference(*[inp[k] for k in FWD_PARAMS])
    out = _jax.eval_shape(fwd)
    return out.shape, out.dtype

N_MICROBATCH = 1
ADAM_LR = 0.001
ADAM_B1 = 0.9
ADAM_B2 = 0.999
ADAM_EPS = 1e-08
ADAM_WD = 0.01
ADAM_STEP = 10
PER_EXAMPLE_BATCH_AXIS = {'x': 0, 'c': 0, 'loss_target': 0}
SHARED_INPUTS = []
_WEIGHT_DTYPES = {'w_mod': _jnp.float32, 'b_mod': _jnp.float32, 'g_pre': _jnp.float32, 'w_in': _jnp.float32, 'conv_w': _jnp.float32, 'conv_b': _jnp.float32, 'w_rg': _jnp.float32, 'b_rg': _jnp.float32, 'w_ig': _jnp.float32, 'b_ig': _jnp.float32, 'lru_lambda': _jnp.float32, 'w_pa': _jnp.float32, 'w_pb': _jnp.float32, 'w_o': _jnp.float32, 'g_post': _jnp.float32}
MOMENT_SCALE = {'w_mod': 2.656929e+00, 'b_mod': 5.519504e+00, 'g_pre': 1.976406e-01, 'w_in': 1.730019e-01, 'conv_w': 4.822728e-01, 'conv_b': 1.622689e+00, 'w_rg': 5.975715e-02, 'b_rg': 9.537356e-02, 'w_ig': 1.333091e-01, 'b_ig': 2.123152e-01, 'lru_lambda': 2.602018e-01, 'w_pa': 4.341376e-02, 'w_pb': 5.350676e-01, 'w_o': 5.180427e-01, 'g_post': 6.758964e+00}


def _to_microbatches(a, axis):
    t = _jnp.moveaxis(a, axis, 0)
    t = t.reshape((N_MICROBATCH, t.shape[0] // N_MICROBATCH) + t.shape[1:])
    return _jnp.moveaxis(t, 1, axis + 1)


def setup_inputs(seed: int = 0) -> dict:
    inp = _fwd_setup_inputs(seed)
    key = _jax.random.fold_in(_jax.random.key(seed), 7919)
    shape, _ = _output_shape()
    out = dict(inp)
    out["loss_target"] = _jax.random.normal(_jax.random.fold_in(key, 0), shape, _jnp.float32)
    for i, name in enumerate(TWIN_WEIGHTS):
        w = inp[name].astype(_jnp.float32)
        if MOMENT_SCALE is None:
            s = _jnp.sqrt(_jnp.mean(_jnp.square(w)) + 1e-30)
        else:
            s = MOMENT_SCALE[name]
        km, kv = _jax.random.split(_jax.random.fold_in(key, i + 1))
        out[name] = w
        out["m_" + name] = s * _jax.random.normal(km, w.shape, _jnp.float32)
        out["v_" + name] = (s * s) * _jax.random.uniform(kv, w.shape, _jnp.float32, 0.5, 1.5)
    if N_MICROBATCH > 1:
        for name, axis in PER_EXAMPLE_BATCH_AXIS.items():
            out[name] = _to_microbatches(out[name], axis)
    return {'x': out['x'], 'c': out['c'], 'w_mod': out['w_mod'], 'b_mod': out['b_mod'], 'g_pre': out['g_pre'], 'w_in': out['w_in'], 'conv_w': out['conv_w'], 'conv_b': out['conv_b'], 'w_rg': out['w_rg'], 'b_rg': out['b_rg'], 'w_ig': out['w_ig'], 'b_ig': out['b_ig'], 'lru_lambda': out['lru_lambda'], 'w_pa': out['w_pa'], 'w_pb': out['w_pb'], 'w_o': out['w_o'], 'g_post': out['g_post'], 'loss_target': out['loss_target'], 'm_w_mod': out['m_w_mod'], 'm_b_mod': out['m_b_mod'], 'm_g_pre': out['m_g_pre'], 'm_w_in': out['m_w_in'], 'm_conv_w': out['m_conv_w'], 'm_conv_b': out['m_conv_b'], 'm_w_rg': out['m_w_rg'], 'm_b_rg': out['m_b_rg'], 'm_w_ig': out['m_w_ig'], 'm_b_ig': out['m_b_ig'], 'm_lru_lambda': out['m_lru_lambda'], 'm_w_pa': out['m_w_pa'], 'm_w_pb': out['m_w_pb'], 'm_w_o': out['m_w_o'], 'm_g_post': out['m_g_post'], 'v_w_mod': out['v_w_mod'], 'v_b_mod': out['v_b_mod'], 'v_g_pre': out['v_g_pre'], 'v_w_in': out['v_w_in'], 'v_conv_w': out['v_conv_w'], 'v_conv_b': out['v_conv_b'], 'v_w_rg': out['v_w_rg'], 'v_b_rg': out['v_b_rg'], 'v_w_ig': out['v_w_ig'], 'v_b_ig': out['v_b_ig'], 'v_lru_lambda': out['v_lru_lambda'], 'v_w_pa': out['v_w_pa'], 'v_w_pb': out['v_w_pb'], 'v_w_o': out['v_w_o'], 'v_g_post': out['v_g_post']}


def _loss(weights, diff, rest, loss_target):
    with _jax.named_scope("forward"):
        args = {**rest, TWIN_DIFF_INPUT: diff, **{k: w.astype(_WEIGHT_DTYPES[k]) for k, w in weights.items()}}
        y = _forward(args)
    with _jax.named_scope("loss_head"):
        err = _jnp.square(y.astype(_jnp.float32) - loss_target)
        return 0.5 * _jnp.sum(_jnp.mean(err, axis=-1)) if err.ndim else 0.5 * err


def _adamw(w, g, m, v):
    m = ADAM_B1 * m + (1.0 - ADAM_B1) * g
    v = ADAM_B2 * v + (1.0 - ADAM_B2) * _jnp.square(g)
    m_hat = m / (1.0 - ADAM_B1 ** ADAM_STEP)
    v_hat = v / (1.0 - ADAM_B2 ** ADAM_STEP)
    delta = -ADAM_LR * (m_hat / (_jnp.sqrt(v_hat) + ADAM_EPS) + ADAM_WD * w)
    return delta, m, v


def reference(x, c, w_mod, b_mod, g_pre, w_in, conv_w, conv_b, w_rg, b_rg, w_ig, b_ig, lru_lambda, w_pa, w_pb, w_o, g_post, loss_target, m_w_mod, m_b_mod, m_g_pre, m_w_in, m_conv_w, m_conv_b, m_w_rg, m_b_rg, m_w_ig, m_b_ig, m_lru_lambda, m_w_pa, m_w_pb, m_w_o, m_g_post, v_w_mod, v_b_mod, v_g_pre, v_w_in, v_conv_w, v_conv_b, v_w_rg, v_b_rg, v_w_ig, v_b_ig, v_lru_lambda, v_w_pa, v_w_pb, v_w_o, v_g_post):
    given = dict(x=x, c=c, w_mod=w_mod, b_mod=b_mod, g_pre=g_pre, w_in=w_in, conv_w=conv_w, conv_b=conv_b, w_rg=w_rg, b_rg=b_rg, w_ig=w_ig, b_ig=b_ig, lru_lambda=lru_lambda, w_pa=w_pa, w_pb=w_pb, w_o=w_o, g_post=g_post, loss_target=loss_target, m_w_mod=m_w_mod, m_b_mod=m_b_mod, m_g_pre=m_g_pre, m_w_in=m_w_in, m_conv_w=m_conv_w, m_conv_b=m_conv_b, m_w_rg=m_w_rg, m_b_rg=m_b_rg, m_w_ig=m_w_ig, m_b_ig=m_b_ig, m_lru_lambda=m_lru_lambda, m_w_pa=m_w_pa, m_w_pb=m_w_pb, m_w_o=m_w_o, m_g_post=m_g_post, v_w_mod=v_w_mod, v_b_mod=v_b_mod, v_g_pre=v_g_pre, v_w_in=v_w_in, v_conv_w=v_conv_w, v_conv_b=v_conv_b, v_w_rg=v_w_rg, v_b_rg=v_b_rg, v_w_ig=v_w_ig, v_b_ig=v_b_ig, v_lru_lambda=v_lru_lambda, v_w_pa=v_w_pa, v_w_pb=v_w_pb, v_w_o=v_w_o, v_g_post=v_g_post)
    weights = {n: given[n] for n in TWIN_WEIGHTS}
    shared = {n: given[n] for n in SHARED_INPUTS}
    per_example = {n: given[n] for n in ['x', 'c']}
    grad_fn = _jax.value_and_grad(_loss, argnums=(0, 1))

    def one_microbatch(ex, loss_target):
        ex = dict(ex)
        diff = ex.pop(TWIN_DIFF_INPUT)
        return grad_fn(weights, diff, {**shared, **ex}, loss_target)

    if N_MICROBATCH == 1:
        loss, (grad_w, grad_x) = one_microbatch(per_example, given["loss_target"])
    else:
        def body(carry, xs):
            loss_sum, grad_sum = carry
            l_k, (gw_k, gx_k) = one_microbatch(xs[0], xs[1])
            with _jax.named_scope("update"):
                return (loss_sum + l_k, _jax.tree.map(_jnp.add, grad_sum, gw_k)), gx_k

        init = (_jnp.zeros((), _jnp.float32), _jax.tree.map(_jnp.zeros_like, weights))
        (loss, grad_w), grad_x = _jax.lax.scan(body, init, (per_example, given["loss_target"]))
    with _jax.named_scope("update"):
        delta_w, new_m, new_v = {}, {}, {}
        for n in TWIN_WEIGHTS:
            delta_w[n], new_m[n], new_v[n] = _adamw(weights[n], grad_w[n], given["m_" + n], given["v_" + n])
    return (loss, grad_x, *[grad_w[n] for n in TWIN_WEIGHTS], *[delta_w[n] for n in TWIN_WEIGHTS],
            *[new_m[n] for n in TWIN_WEIGHTS], *[new_v[n] for n in TWIN_WEIGHTS])
```

```python
import functools

import jax
import jax.numpy as jnp
from jax import lax
from jax.experimental import pallas as pl
from jax.experimental.pallas import tpu as pltpu

F32 = jnp.float32
BF16 = jnp.bfloat16

N_DEV = 8
D_MODEL = 1024
HEAD_DIM = 128
HEADS = 4
ATT_WIDTH = HEADS * HEAD_DIM
DILATIONS = (1, 4, 16)
BAND = 128
N_CHUNKS = 18
QKV_CHUNKS = 9
CONV_WIDTH = 4
LRU_C = 8.0
NORM_EPS = 1e-6
NEG_INF = -1e30
ADAM_LR = 0.001
ADAM_B1 = 0.9
ADAM_B2 = 0.999
ADAM_EPS = 1e-08
ADAM_WD = 0.01
ADAM_STEP = 10
CHUNK_PERM = (0, 3, 6, 1, 4, 7, 2, 5, 8, 10, 11, 12, 13, 14, 15, 16, 17, 9)
DPROJ_CHUNKS = 21
VMEM_LIMIT = 56 * 1024 * 1024
MESH_AXES = ("x", "y", "c")


def _params(sem=None):
    return pltpu.CompilerParams(dimension_semantics=sem, vmem_limit_bytes=VMEM_LIMIT)


def _dproj_chunk(k):
    return k + jnp.where(k >= 9, 1, 0) + jnp.where(k >= 11, 2, 0)


def _orig_chunk(k):
    return jnp.where(k < 9, (k % 3) * 3 + k // 3, jnp.where(k == 17, 9, k + 1))


def _silu(x):
    return x * jax.nn.sigmoid(x)


def _dsilu(x):
    s = jax.nn.sigmoid(x)
    return s * (1.0 + x * (1.0 - s))


def _neg_expm1(x):
    series = -x * (1.0 + x * (0.5 + x * (1.0 / 6.0 + x * (1.0 / 24.0 + x * (1.0 / 120.0)))))
    return jnp.where(x > -0.05, series, 1.0 - jnp.exp(x))


def _softplus_neg(lam):
    z = jnp.exp(-jnp.abs(lam))
    small = z * (1.0 - z * (0.5 - z * (1.0 / 3.0 - z * 0.25)))
    log1p_z = jnp.where(z < 1e-2, small, jnp.log(1.0 + z))
    return jnp.maximum(-lam, 0.0) + log1p_z


def _dot(a, b):
    return jnp.dot(a, b, preferred_element_type=F32)


def _dot_nt(a, b):
    return lax.dot_general(a, b, (((1,), (1,)), ((), ())), preferred_element_type=F32)


def _dot_tn(a, b):
    return lax.dot_general(a, b, (((0,), (0,)), ((), ())), preferred_element_type=F32)


def _exchange(arrs, modes, name):
    n = len(arrs)
    out_shapes = []
    for a, mode in zip(arrs, modes):
        shape = ((N_DEV,) + a.shape) if mode == "ag" else a.shape
        out_shapes.append(jax.ShapeDtypeStruct(shape, a.dtype))

    def body(*refs):
        ins, outs = refs[:n], refs[n:2 * n]
        send_sems, recv_sems, local_sems = refs[2 * n:]
        x, y, c = lax.axis_index("x"), lax.axis_index("y"), lax.axis_index("c")
        me = 4 * x + 2 * y + c

        def src_for(a, dev):
            return ins[a] if modes[a] == "ag" else ins[a].at[dev]

        local = []
        for a in range(n):
            cp = pltpu.make_async_copy(src_for(a, me), outs[a].at[me], local_sems.at[a])
            cp.start()
            local.append(cp)
        started, arrivals = [], []
        for k in range(1, N_DEV):
            px = 1 - x if (k >> 2) & 1 else x
            py = 1 - y if (k >> 1) & 1 else y
            pc = 1 - c if k & 1 else c
            peer = 4 * px + 2 * py + pc
            for a in range(n):
                s = (k - 1) * n + a
                cp = pltpu.make_async_remote_copy(
                    src_ref=src_for(a, peer), dst_ref=outs[a].at[me],
                    send_sem=send_sems.at[s], recv_sem=recv_sems.at[s],
                    device_id=(px, py, pc), device_id_type=pl.DeviceIdType.MESH)
                cp.start()
                started.append(cp)
                arrivals.append(pltpu.make_async_remote_copy(
                    src_ref=src_for(a, peer), dst_ref=outs[a].at[peer],
                    send_sem=send_sems.at[s], recv_sem=recv_sems.at[s],
                    device_id=(px, py, pc), device_id_type=pl.DeviceIdType.MESH))
        for cp in arrivals:
            cp.wait_recv()
        for cp in started:
            cp.wait_send()
        for cp in local:
            cp.wait()

    any_spec = pl.BlockSpec(memory_space=pl.ANY)
    outs = pl.pallas_call(
        body, name=name,
        out_shape=tuple(out_shapes),
        in_specs=[any_spec] * n, out_specs=tuple([any_spec] * n),
        scratch_shapes=[pltpu.SemaphoreType.DMA((7 * n,)), pltpu.SemaphoreType.DMA((7 * n,)),
                        pltpu.SemaphoreType.DMA((n,))],
    )(*arrs)
    return list(outs)


def _mod_fwd(c_pad, w_mod):
    def body(c_ref, w_ref, o_ref):
        sc = _silu(c_ref[...]).astype(BF16)
        for layer in range(2):
            o_ref[layer] = _dot(sc, w_ref[layer].astype(BF16))

    return pl.pallas_call(
        body, name="mod_fwd", out_shape=jax.ShapeDtypeStruct((2, 16, w_mod.shape[2]), F32),
        compiler_params=_params())(c_pad, w_mod)


def _mod_wgrad(c_t, dmod):
    n_cols = dmod.shape[2]

    def body(c_ref, d_ref, o_ref):
        sc = _silu(c_ref[...]).astype(BF16).astype(F32)
        for layer in range(2):
            dm = d_ref[layer].astype(BF16).astype(F32)
            acc = sc[:, 0:1] * dm[0:1, :]
            for b in range(1, N_DEV):
                acc = acc + sc[:, b:b + 1] * dm[b:b + 1, :]
            o_ref[layer] = acc

    return pl.pallas_call(
        body, name="mod_wgrad", out_shape=jax.ShapeDtypeStruct((2, D_MODEL, n_cols), F32),
        compiler_params=_params())(c_t, dmod)


def _norm_mod_fwd(x, vec):
    S = x.shape[0]
    tm = 512

    def body(x_ref, v_ref, h_ref):
        xv = x_ref[...]
        rstd = lax.rsqrt(jnp.mean(xv * xv, axis=-1, keepdims=True) + NORM_EPS)
        h = (xv * rstd) * v_ref[0:1, :]
        h_ref[...] = (h * (1.0 + v_ref[1:2, :]) + v_ref[2:3, :]).astype(BF16)

    return pl.pallas_call(
        body, name="norm_mod_fwd", grid=(S // tm,),
        in_specs=[pl.BlockSpec((tm, D_MODEL), lambda i: (i, 0)),
                  pl.BlockSpec((8, D_MODEL), lambda i: (0, 0))],
        out_specs=pl.BlockSpec((tm, D_MODEL), lambda i: (i, 0)),
        out_shape=jax.ShapeDtypeStruct((S, D_MODEL), BF16),
        compiler_params=_params(("parallel",)))(x, vec)


def _norm_mod_bwd(dh, x, dxn, vec):
    S = x.shape[0]
    tm = 512

    def body(dh_ref, x_ref, dxn_ref, v_ref, dx_ref, acc_ref):
        i = pl.program_id(0)

        @pl.when(i == 0)
        def _():
            acc_ref[...] = jnp.zeros_like(acc_ref)

        xv = x_ref[...]
        dhv = dh_ref[...]
        g = v_ref[0:1, :]
        sc1 = 1.0 + v_ref[1:2, :]
        rstd = lax.rsqrt(jnp.mean(xv * xv, axis=-1, keepdims=True) + NORM_EPS)
        xhat = xv * rstd
        acc_ref[0:1, :] += jnp.sum(dhv, axis=0, keepdims=True)
        acc_ref[1:2, :] += jnp.sum(dhv * (xhat * g), axis=0, keepdims=True)
        acc_ref[2:3, :] += jnp.sum(dhv * xhat * sc1, axis=0, keepdims=True)
        dxhat = dhv * (g * sc1)
        dx = rstd * (dxhat - xhat * jnp.mean(dxhat * xhat, axis=-1, keepdims=True))
        dx_ref[...] = dx + dxn_ref[...]

    row = pl.BlockSpec((tm, D_MODEL), lambda i: (i, 0))
    vec_spec = pl.BlockSpec((8, D_MODEL), lambda i: (0, 0))
    return pl.pallas_call(
        body, name="norm_mod_bwd", grid=(S // tm,),
        in_specs=[row, row, row, vec_spec],
        out_specs=(row, vec_spec),
        out_shape=(jax.ShapeDtypeStruct((S, D_MODEL), F32), jax.ShapeDtypeStruct((8, D_MODEL), F32)),
        compiler_params=_params(("arbitrary",)))(dh, x, dxn, vec)


def _proj_fwd(h, w_int, first_chunk, out_dtype, name):
    S = h.shape[0]
    tm, tn = 1024, ATT_WIDTH

    def body(h_ref, w_ref, o_ref):
        o_ref[...] = _dot(h_ref[...], w_ref[...]).astype(out_dtype)

    return pl.pallas_call(
        body, name=name, grid=(S // tm, QKV_CHUNKS),
        in_specs=[pl.BlockSpec((tm, D_MODEL), lambda i, j: (i, 0)),
                  pl.BlockSpec((D_MODEL, tn), lambda i, j: (0, first_chunk + j))],
        out_specs=pl.BlockSpec((tm, tn), lambda i, j: (i, j)),
        out_shape=jax.ShapeDtypeStruct((S, QKV_CHUNKS * tn), out_dtype),
        compiler_params=_params(("parallel", "parallel")))(h, w_int)


def _proj_dgrad(dproj, w_int):
    S = dproj.shape[1]
    tm = 512

    def body(d_ref, w_ref, o_ref):
        k = pl.program_id(1)
        part = _dot_nt(d_ref[...], w_ref[...])

        @pl.when(k == 0)
        def _():
            o_ref[...] = part

        @pl.when(k > 0)
        def _():
            o_ref[...] += part

    return pl.pallas_call(
        body, name="proj_dgrad", grid=(S // tm, N_CHUNKS),
        in_specs=[pl.BlockSpec((None, tm, ATT_WIDTH), lambda i, k: (_dproj_chunk(k), i, 0)),
                  pl.BlockSpec((D_MODEL, ATT_WIDTH), lambda i, k: (0, k))],
        out_specs=pl.BlockSpec((tm, D_MODEL), lambda i, k: (i, 0)),
        out_shape=jax.ShapeDtypeStruct((S, D_MODEL), F32),
        compiler_params=_params(("parallel", "arbitrary")))(dproj, w_int)


def _proj_wgrad(h, dproj):
    S = h.shape[0]
    ts = 1024

    def body(h_ref, d_ref, o_ref):
        i = pl.program_id(1)
        part = _dot_tn(h_ref[...], d_ref[...])

        @pl.when(i == 0)
        def _():
            o_ref[...] = part

        @pl.when(i > 0)
        def _():
            o_ref[...] += part

    return pl.pallas_call(
        body, name="proj_wgrad", grid=(N_CHUNKS, S // ts),
        in_specs=[pl.BlockSpec((ts, D_MODEL), lambda k, i: (i, 0)),
                  pl.BlockSpec((None, ts, ATT_WIDTH), lambda k, i: (_dproj_chunk(k), i, 0))],
        out_specs=pl.BlockSpec((D_MODEL, ATT_WIDTH), lambda k, i: (0, _orig_chunk(k))),
        out_shape=jax.ShapeDtypeStruct((D_MODEL, N_CHUNKS * ATT_WIDTH), F32),
        compiler_params=_params(("parallel", "arbitrary")))(h, dproj)


def _wgrad_tn(a, b, name):
    S, M = a.shape
    N = b.shape[1]
    ts = 1024

    def body(a_ref, b_ref, o_ref):
        i = pl.program_id(0)
        part = _dot_tn(a_ref[...], b_ref[...])

        @pl.when(i == 0)
        def _():
            o_ref[...] = part

        @pl.when(i > 0)
        def _():
            o_ref[...] += part

    return pl.pallas_call(
        body, name=name, grid=(S // ts,),
        in_specs=[pl.BlockSpec((ts, M), lambda i: (i, 0)), pl.BlockSpec((ts, N), lambda i: (i, 0))],
        out_specs=pl.BlockSpec((M, N), lambda i: (0, 0)),
        out_shape=jax.ShapeDtypeStruct((M, N), F32),
        compiler_params=_params(("arbitrary",)))(a, b)


def _attn_fwd(qkv, g, r):
    S = qkv.shape[0]
    L = S // r
    nb = L // BAND
    scale = HEAD_DIM ** -0.5
    qv = qkv.reshape(L, r * QKV_CHUNKS * ATT_WIDTH)

    def body(q_ref, kp_ref, kc_ref, vp_ref, vc_ref, o_ref, st_ref):
        n = pl.program_id(1)
        ii = lax.broadcasted_iota(jnp.int32, (BAND, BAND), 0)
        kk = lax.broadcasted_iota(jnp.int32, (BAND, BAND), 1)
        mask_prev = jnp.logical_and(kk >= ii, n > 0)
        mask_cur = kk <= ii
        lane = lax.broadcasted_iota(jnp.int32, (BAND, 128), 1)
        stats = jnp.zeros((BAND, 128), F32)
        for h in range(HEADS):
            sl = slice(h * HEAD_DIM, (h + 1) * HEAD_DIM)
            q = q_ref[:, sl]
            s_p = jnp.where(mask_prev, _dot_nt(q, kp_ref[:, sl]) * scale, NEG_INF)
            s_c = jnp.where(mask_cur, _dot_nt(q, kc_ref[:, sl]) * scale, NEG_INF)
            m = jnp.maximum(jnp.max(s_p, axis=1, keepdims=True), jnp.max(s_c, axis=1, keepdims=True))
            p_p = jnp.exp(s_p - m)
            p_c = jnp.exp(s_c - m)
            l = jnp.sum(p_p, axis=1, keepdims=True) + jnp.sum(p_c, axis=1, keepdims=True)
            o = _dot(p_p.astype(BF16), vp_ref[:, sl]) + _dot(p_c.astype(BF16), vc_ref[:, sl])
            o_ref[:, sl] = o / l
            stats = jnp.where(lane == h, m, stats)
            stats = jnp.where(lane == HEADS + h, l, stats)
        st_ref[...] = stats

    base = 3 * g
    blk = (BAND, ATT_WIDTH)
    return pl.pallas_call(
        body, name=f"attn_fwd_g{g}", grid=(r, nb),
        in_specs=[
            pl.BlockSpec(blk, lambda rho, n: (n, rho * QKV_CHUNKS + base)),
            pl.BlockSpec(blk, lambda rho, n: (jnp.maximum(n - 1, 0), rho * QKV_CHUNKS + base + 1)),
            pl.BlockSpec(blk, lambda rho, n: (n, rho * QKV_CHUNKS + base + 1)),
            pl.BlockSpec(blk, lambda rho, n: (jnp.maximum(n - 1, 0), rho * QKV_CHUNKS + base + 2)),
            pl.BlockSpec(blk, lambda rho, n: (n, rho * QKV_CHUNKS + base + 2)),
        ],
        out_specs=(pl.BlockSpec(blk, lambda rho, n: (n, rho)),
                   pl.BlockSpec((BAND, 128), lambda rho, n: (n, rho))),
        out_shape=(jax.ShapeDtypeStruct((L, r * ATT_WIDTH), F32),
                   jax.ShapeDtypeStruct((L, r * 128), F32)),
        compiler_params=_params(("parallel", "parallel")))(qv, qv, qv, qv, qv)


def _attn_combine(outs, stats, rest):
    S = rest.shape[0]
    tm = 512
    gatt_blk = 8

    def body(o0_ref, o1_ref, o2_ref, s0_ref, s1_ref, s2_ref, g_ref, o_ref, a_ref, lse_ref):
        o_refs = (o0_ref, o1_ref, o2_ref)
        st = (s0_ref[...], s1_ref[...], s2_ref[...])
        lane = lax.broadcasted_iota(jnp.int32, (tm, 128), 1)
        lse_out = jnp.zeros((tm, 128), F32)
        for h in range(HEADS):
            sl = slice(h * HEAD_DIM, (h + 1) * HEAD_DIM)
            ms = [s[:, h:h + 1] for s in st]
            ls = [s[:, HEADS + h:HEADS + h + 1] for s in st]
            m_all = jnp.maximum(jnp.maximum(ms[0], ms[1]), ms[2])
            ws = [l * jnp.exp(m - m_all) for m, l in zip(ms, ls)]
            den = ws[0] + ws[1] + ws[2]
            o = (ws[0] * o_refs[0][:, sl] + ws[1] * o_refs[1][:, sl] + ws[2] * o_refs[2][:, sl]) / den
            o_ref[:, sl] = o
            a_ref[:, sl] = (o * _silu(g_ref[:, sl])).astype(BF16)
            lse_out = jnp.where(lane == h, m_all + jnp.log(den), lse_out)
        lse_ref[...] = lse_out

    o_spec = pl.BlockSpec((tm, ATT_WIDTH), lambda i: (i, 0))
    s_spec = pl.BlockSpec((tm, 128), lambda i: (i, 0))
    return pl.pallas_call(
        body, name="attn_combine", grid=(S // tm,),
        in_specs=[o_spec, o_spec, o_spec, s_spec, s_spec, s_spec,
                  pl.BlockSpec((tm, ATT_WIDTH), lambda i: (i, gatt_blk))],
        out_specs=(o_spec, o_spec, s_spec),
        out_shape=(jax.ShapeDtypeStruct((S, ATT_WIDTH), F32), jax.ShapeDtypeStruct((S, ATT_WIDTH), BF16),
                   jax.ShapeDtypeStruct((S, 128), F32)),
        compiler_params=_params(("parallel",)))(*outs, *stats, rest)


def _attn_bwd(qkv, do, lse, dvec, dproj, g, r):
    S = qkv.shape[0]
    L = S // r
    nb = L // BAND
    scale = HEAD_DIM ** -0.5
    qv = qkv.reshape(L, r * QKV_CHUNKS * ATT_WIDTH)
    dov = do.reshape(L, r * ATT_WIDTH)
    lsev = lse.reshape(L, r * 128)
    dvv = dvec.reshape(L, r * 128)
    dpv = dproj.reshape(DPROJ_CHUNKS, L, r * ATT_WIDTH)

    def body(qc_ref, qn_ref, k_ref, v_ref, doc_ref, don_ref, lc_ref, ln_ref, dc_ref, dn_ref, _alias,
             out_ref, carry_ref):
        j = pl.program_id(1)

        @pl.when(j == 0)
        def _():
            carry_ref[...] = jnp.zeros_like(carry_ref)

        ii = lax.broadcasted_iota(jnp.int32, (BAND, BAND), 0)
        kk = lax.broadcasted_iota(jnp.int32, (BAND, BAND), 1)
        mask_diag = kk <= ii
        mask_off = jnp.logical_and(kk >= ii, j < nb - 1)
        lc, ln, dc, dn = lc_ref[...], ln_ref[...], dc_ref[...], dn_ref[...]
        for h in range(HEADS):
            sl = slice(h * HEAD_DIM, (h + 1) * HEAD_DIM)
            k = k_ref[:, sl]
            v = v_ref[:, sl]
            q1, do1 = qc_ref[:, sl], doc_ref[:, sl]
            p1 = jnp.where(mask_diag, jnp.exp(_dot_nt(q1, k) * scale - lc[:, h:h + 1]), 0.0)
            ds1 = (p1 * (_dot_nt(do1, v) - dc[:, h:h + 1])).astype(BF16)
            q2, do2 = qn_ref[:, sl], don_ref[:, sl]
            p2 = jnp.where(mask_off, jnp.exp(_dot_nt(q2, k) * scale - ln[:, h:h + 1]), 0.0)
            ds2 = (p2 * (_dot_nt(do2, v) - dn[:, h:h + 1])).astype(BF16)
            dq = carry_ref[:, sl] + _dot(ds1, k) * scale
            dk = (_dot_tn(ds1, q1) + _dot_tn(ds2, q2)) * scale
            dv = _dot_tn(p1.astype(BF16), do1) + _dot_tn(p2.astype(BF16), do2)
            out_ref[0, :, sl] = dq.astype(BF16)
            out_ref[1, :, sl] = dk.astype(BF16)
            out_ref[2, :, sl] = dv.astype(BF16)
            carry_ref[:, sl] = _dot(ds2, k) * scale

    base = 3 * g
    blk = (BAND, ATT_WIDTH)
    sblk = (BAND, 128)

    def nxt(j):
        return jnp.minimum(j + 1, nb - 1)

    out = pl.pallas_call(
        body, name=f"attn_bwd_g{g}", grid=(r, nb),
        in_specs=[
            pl.BlockSpec(blk, lambda rho, j: (j, rho * QKV_CHUNKS + base)),
            pl.BlockSpec(blk, lambda rho, j: (nxt(j), rho * QKV_CHUNKS + base)),
            pl.BlockSpec(blk, lambda rho, j: (j, rho * QKV_CHUNKS + base + 1)),
            pl.BlockSpec(blk, lambda rho, j: (j, rho * QKV_CHUNKS + base + 2)),
            pl.BlockSpec(blk, lambda rho, j: (j, rho)),
            pl.BlockSpec(blk, lambda rho, j: (nxt(j), rho)),
            pl.BlockSpec(sblk, lambda rho, j: (j, rho)),
            pl.BlockSpec(sblk, lambda rho, j: (nxt(j), rho)),
            pl.BlockSpec(sblk, lambda rho, j: (j, rho)),
            pl.BlockSpec(sblk, lambda rho, j: (nxt(j), rho)),
            pl.BlockSpec(memory_space=pl.ANY),
        ],
        out_specs=pl.BlockSpec((3, BAND, ATT_WIDTH), lambda rho, j: (g, j, rho)),
        out_shape=jax.ShapeDtypeStruct(dpv.shape, BF16),
        scratch_shapes=[pltpu.VMEM((BAND, ATT_WIDTH), F32)],
        input_output_aliases={10: 0},
        compiler_params=_params(("parallel", "arbitrary")),
    )(qv, qv, qv, qv, dov, dov, lsev, lsev, dvv, dvv, dpv)
    return out.reshape(DPROJ_CHUNKS, S, ATT_WIDTH)


LRU_T = 256


def _shift_down(x, s, rows):
    return pltpu.roll(x, s, 0)


def _gate_matmuls(ucb, w_ref, bias):
    parts = [_dot(ucb[:, j * 128:(j + 1) * 128], w_ref[j]) for j in range(8)]
    return jnp.concatenate(parts, axis=1) + bias


def _conv_fwd(u, u_prev8, cw_ref):
    T = u.shape[0]
    ue = jnp.concatenate([u_prev8, u], axis=0)
    uc = cw_ref[4:5, :] + cw_ref[0:1, :] * u
    for j in range(1, CONV_WIDTH):
        uc = uc + cw_ref[j:j + 1, :] * pltpu.roll(ue, j, 0)[8:8 + T]
    return uc


def _lru_fwd(rest, cw, wr2, wi2):
    S = rest.shape[0]
    T = LRU_T

    def body(u_ref, cw_ref, wr_ref, wi_ref, h_ref, a_ref, ucar_ref, hcar_ref):
        c = pl.program_id(0)

        @pl.when(c == 0)
        def _():
            ucar_ref[...] = jnp.zeros_like(ucar_ref)
            hcar_ref[...] = jnp.zeros_like(hcar_ref)

        u = u_ref[...]
        uc = _conv_fwd(u, ucar_ref[...], cw_ref)
        ucar_ref[...] = u[T - 8:, :]
        ucb = uc.astype(BF16)
        r = jax.nn.sigmoid(_gate_matmuls(ucb, wr_ref, cw_ref[5:6, :]))
        ig = jax.nn.sigmoid(_gate_matmuls(ucb, wi_ref, cw_ref[6:7, :]))
        log_a = -LRU_C * r * _softplus_neg(cw_ref[7:8, :])
        a = jnp.exp(log_a)
        b = jnp.sqrt(_neg_expm1(2.0 * log_a)) * (ig * uc)
        a_ref[...] = a
        row = lax.broadcasted_iota(jnp.int32, (T, D_MODEL), 0)
        acc_a, acc_b = a, b
        s = 1
        while s < T:
            keep = row >= s
            b_sh = jnp.where(keep, pltpu.roll(acc_b, s, 0), 0.0)
            a_sh = jnp.where(keep, pltpu.roll(acc_a, s, 0), 1.0)
            acc_b = acc_a * b_sh + acc_b
            acc_a = acc_a * a_sh
            s *= 2
        h = acc_b + acc_a * hcar_ref[0:1, :]
        h_ref[...] = h
        hcar_ref[...] = jnp.broadcast_to(h[T - 1:T, :], (8, D_MODEL))

    row_spec = pl.BlockSpec((T, D_MODEL), lambda c: (c, 0))
    return pl.pallas_call(
        body, name="lru_fwd", grid=(S // T,),
        in_specs=[row_spec, pl.BlockSpec((8, D_MODEL), lambda c: (0, 0)),
                  pl.BlockSpec((8, 128, 128), lambda c: (0, 0, 0)),
                  pl.BlockSpec((8, 128, 128), lambda c: (0, 0, 0))],
        out_specs=(row_spec, row_spec),
        out_shape=(jax.ShapeDtypeStruct((S, D_MODEL), F32), jax.ShapeDtypeStruct((S, D_MODEL), F32)),
        scratch_shapes=[pltpu.VMEM((8, D_MODEL), F32), pltpu.VMEM((8, D_MODEL), F32)],
        compiler_params=_params(("arbitrary",)))(rest, cw, wr2, wi2)


def _lru_bwd(dhl, hl, a_all, rest, cw, wr2, wi2, dproj):
    S = rest.shape[0]
    T = LRU_T
    nc = S // T

    def body(dh_ref, h_ref, hp_ref, a_ref, u_ref, up_ref, cw_ref, wr_ref, wi_ref, _alias,
             du_ref, acc_ref, gwr_ref, gwi_ref, gcar_ref, acar_ref, dcar_ref):
        step = pl.program_id(0)
        c = nc - 1 - step

        @pl.when(step == 0)
        def _():
            acc_ref[...] = jnp.zeros_like(acc_ref)
            gwr_ref[...] = jnp.zeros_like(gwr_ref)
            gwi_ref[...] = jnp.zeros_like(gwi_ref)
            gcar_ref[...] = jnp.zeros_like(gcar_ref)
            acar_ref[...] = jnp.zeros_like(acar_ref)
            dcar_ref[...] = jnp.zeros_like(dcar_ref)

        row = lax.broadcasted_iota(jnp.int32, (T, D_MODEL), 0)
        u = u_ref[...]
        u_prev = jnp.where(c > 0, up_ref[...], 0.0)
        h_prev8 = jnp.where(c > 0, hp_ref[...], 0.0)
        a = a_ref[...]
        h = h_ref[...]
        uc = _conv_fwd(u, u_prev, cw_ref)
        ucb = uc.astype(BF16)
        r = jax.nn.sigmoid(_gate_matmuls(ucb, wr_ref, cw_ref[5:6, :]))
        ig = jax.nn.sigmoid(_gate_matmuls(ucb, wi_ref, cw_ref[6:7, :]))
        sp = _softplus_neg(cw_ref[7:8, :])
        log_a = -LRU_C * r * sp
        mult = jnp.sqrt(_neg_expm1(2.0 * log_a))
        a_next = jnp.where(row < T - 1, pltpu.roll(a, T - 1, 0), acar_ref[0:1, :])
        acc_a, acc_b = a_next, dh_ref[...]
        s = 1
        while s < T:
            keep = row < T - s
            b_sh = jnp.where(keep, pltpu.roll(acc_b, T - s, 0), 0.0)
            a_sh = jnp.where(keep, pltpu.roll(acc_a, T - s, 0), 1.0)
            acc_b = acc_a * b_sh + acc_b
            acc_a = acc_a * a_sh
            s *= 2
        G = acc_b + acc_a * gcar_ref[0:1, :]
        gcar_ref[...] = jnp.broadcast_to(G[0:1, :], (8, D_MODEL))
        acar_ref[...] = jnp.broadcast_to(a[0:1, :], (8, D_MODEL))
        he = jnp.concatenate([h_prev8, h], axis=0)
        h_before = pltpu.roll(he, 1, 0)[8:8 + T]
        d_a = G * h_before
        d_mult = G * (ig * uc)
        d_ig = G * (mult * uc)
        duc = G * (mult * ig)
        d_log_a = d_a * a - d_mult * (a * a) / mult
        d_r = d_log_a * (-LRU_C * sp)
        d_sp = jnp.sum(d_log_a * (-LRU_C * r), axis=0, keepdims=True)
        dpre_r = d_r * r * (1.0 - r)
        dpre_i = d_ig * ig * (1.0 - ig)
        dprb = dpre_r.astype(BF16)
        dpib = dpre_i.astype(BF16)
        back = []
        for j in range(8):
            sl = slice(j * 128, (j + 1) * 128)
            back.append(_dot_nt(dprb[:, sl], wr_ref[j]) + _dot_nt(dpib[:, sl], wi_ref[j]))
            gwr_ref[j] += _dot_tn(ucb[:, sl], dprb[:, sl])
            gwi_ref[j] += _dot_tn(ucb[:, sl], dpib[:, sl])
        duc = duc + jnp.concatenate(back, axis=1)
        de = jnp.concatenate([duc, dcar_ref[...]], axis=0)
        du = cw_ref[0:1, :] * duc
        ue = jnp.concatenate([u_prev, u], axis=0)
        acc_ref[0:1, :] += jnp.sum(duc * u, axis=0, keepdims=True)
        for j in range(1, CONV_WIDTH):
            du = du + cw_ref[j:j + 1, :] * pltpu.roll(de, T + 8 - j, 0)[0:T]
            acc_ref[j:j + 1, :] += jnp.sum(duc * pltpu.roll(ue, j, 0)[8:8 + T], axis=0, keepdims=True)
        dcar_ref[...] = duc[0:8, :]
        acc_ref[4:5, :] += jnp.sum(duc, axis=0, keepdims=True)
        acc_ref[5:6, :] += jnp.sum(dpre_r, axis=0, keepdims=True)
        acc_ref[6:7, :] += jnp.sum(dpre_i, axis=0, keepdims=True)
        acc_ref[7:8, :] += d_sp
        du_ref[0] = du[:, 0:ATT_WIDTH].astype(BF16)
        du_ref[1] = du[:, ATT_WIDTH:].astype(BF16)

    def rev(step):
        return nc - 1 - step

    def prev8(step):
        return jnp.maximum(rev(step) * (T // 8) - 1, 0)

    row_spec = pl.BlockSpec((T, D_MODEL), lambda s: (rev(s), 0))
    prev_spec = pl.BlockSpec((8, D_MODEL), lambda s: (prev8(s), 0))
    vec_spec = pl.BlockSpec((8, D_MODEL), lambda s: (0, 0))
    w_spec = pl.BlockSpec((8, 128, 128), lambda s: (0, 0, 0))
    outs = pl.pallas_call(
        body, name="lru_bwd", grid=(nc,),
        in_specs=[row_spec, row_spec, prev_spec, row_spec, row_spec, prev_spec, vec_spec, w_spec, w_spec,
                  pl.BlockSpec(memory_space=pl.ANY)],
        out_specs=(pl.BlockSpec((2, T, ATT_WIDTH), lambda s: (5, rev(s), 0)), vec_spec, w_spec, w_spec),
        out_shape=(jax.ShapeDtypeStruct(dproj.shape, BF16), jax.ShapeDtypeStruct((8, D_MODEL), F32),
                   jax.ShapeDtypeStruct((8, 128, 128), F32), jax.ShapeDtypeStruct((8, 128, 128), F32)),
        scratch_shapes=[pltpu.VMEM((8, D_MODEL), F32), pltpu.VMEM((8, D_MODEL), F32),
                        pltpu.VMEM((8, D_MODEL), F32)],
        input_output_aliases={9: 0},
        compiler_params=_params(("arbitrary",)),
    )(dhl, hl, hl, a_all, rest, rest, cw, wr2, wi2, dproj)
    return outs


OUT_TM = 256


def _out_fwd(a_gated, hl, rest, x, vec, w_pa, w_pb, w_o):
    S = x.shape[0]
    tm = OUT_TM

    def body(ag_ref, hl_ref, gl_ref, ma_ref, mb_ref, x_ref, v_ref, wpa_ref, wpb_ref, wo_ref,
             xn_ref, mix_ref, ya_ref, yb_ref, bg_ref):
        bg = (hl_ref[...] * _silu(gl_ref[...])).astype(BF16)
        ya = _dot(ag_ref[...], wpa_ref[...])
        yb = _dot(bg, wpb_ref[...])
        mix = (jax.nn.sigmoid(ma_ref[...]) * ya + jax.nn.sigmoid(mb_ref[...]) * yb).astype(BF16)
        out = _dot(mix, wo_ref[...])
        rstd = lax.rsqrt(jnp.mean(out * out, axis=-1, keepdims=True) + NORM_EPS)
        xn_ref[...] = x_ref[...] + v_ref[1:2, :] * ((out * rstd) * v_ref[0:1, :])
        mix_ref[...] = mix
        ya_ref[...] = ya.astype(BF16)
        yb_ref[...] = yb.astype(BF16)
        bg_ref[...] = bg

    def col(j):
        return pl.BlockSpec((tm, D_MODEL), lambda i: (i, j))

    def whole(shape):
        return pl.BlockSpec(shape, lambda i: (0, 0))

    row = col(0)
    bf = jax.ShapeDtypeStruct((S, D_MODEL), BF16)
    return pl.pallas_call(
        body, name="out_fwd", grid=(S // tm,),
        in_specs=[pl.BlockSpec((tm, ATT_WIDTH), lambda i: (i, 0)), row, col(1), col(2), col(3), row,
                  whole((8, D_MODEL)), whole((ATT_WIDTH, D_MODEL)), whole((D_MODEL, D_MODEL)),
                  whole((D_MODEL, D_MODEL))],
        out_specs=(row, row, row, row, row),
        out_shape=(jax.ShapeDtypeStruct((S, D_MODEL), F32), bf, bf, bf, bf),
        compiler_params=_params(("parallel",)))(a_gated, hl, rest, rest, rest, x, vec, w_pa, w_pb, w_o)


def _out_bwd(dxn, mix, ya, yb, hl, rest, o_att, vec, w_pa, w_pb, w_o):
    S = dxn.shape[0]
    tm = OUT_TM

    def body(dxn_ref, mix_ref, ya_ref, yb_ref, hl_ref, gl_ref, ma_ref, mb_ref, ga_ref, o_ref, v_ref,
             wpa_ref, wpb_ref, wo_ref,
             dout_ref, dya_ref, dyb_ref, do_ref, dv_ref, dhl_ref, dp_ref, acc_ref):
        i = pl.program_id(0)

        @pl.when(i == 0)
        def _():
            acc_ref[...] = jnp.zeros_like(acc_ref)

        g_post = v_ref[0:1, :]
        gate = v_ref[1:2, :]
        dxn_v = dxn_ref[...]
        out = _dot(mix_ref[...], wo_ref[...])
        rstd = lax.rsqrt(jnp.mean(out * out, axis=-1, keepdims=True) + NORM_EPS)
        nrm = out * rstd
        acc_ref[0:1, :] += jnp.sum(dxn_v * (nrm * g_post), axis=0, keepdims=True)
        acc_ref[1:2, :] += jnp.sum(dxn_v * gate * nrm, axis=0, keepdims=True)
        dn = dxn_v * (gate * g_post)
        dout = (rstd * (dn - nrm * jnp.mean(dn * nrm, axis=-1, keepdims=True))).astype(BF16)
        dout_ref[...] = dout
        dmix = _dot_nt(dout, wo_ref[...])
        sa = jax.nn.sigmoid(ma_ref[...])
        sb = jax.nn.sigmoid(mb_ref[...])
        dya = (dmix * sa).astype(BF16)
        dyb = (dmix * sb).astype(BF16)
        dya_ref[...] = dya
        dyb_ref[...] = dyb
        dma = dmix * ya_ref[...].astype(F32) * (sa * (1.0 - sa))
        dmb = dmix * yb_ref[...].astype(F32) * (sb * (1.0 - sb))
        d_ag = _dot_nt(dya, wpa_ref[...])
        d_bg = _dot_nt(dyb, wpb_ref[...])
        gl = gl_ref[...]
        hl_v = hl_ref[...]
        dhl_ref[...] = d_bg * _silu(gl)
        dgl = d_bg * hl_v * _dsilu(gl)
        ga = ga_ref[...]
        o = o_ref[...]
        do = d_ag * _silu(ga)
        dga = d_ag * o * _dsilu(ga)
        do_ref[...] = do.astype(BF16)
        lane = lax.broadcasted_iota(jnp.int32, (tm, 128), 1)
        dvec = jnp.zeros((tm, 128), F32)
        for h in range(HEADS):
            sl = slice(h * HEAD_DIM, (h + 1) * HEAD_DIM)
            dvec = jnp.where(lane == h, jnp.sum(do[:, sl] * o[:, sl], axis=1, keepdims=True), dvec)
        dv_ref[...] = dvec
        dp_ref[0] = dgl[:, 0:ATT_WIDTH].astype(BF16)
        dp_ref[1] = dgl[:, ATT_WIDTH:].astype(BF16)
        dp_ref[2] = dma[:, 0:ATT_WIDTH].astype(BF16)
        dp_ref[3] = dma[:, ATT_WIDTH:].astype(BF16)
        dp_ref[4] = dmb[:, 0:ATT_WIDTH].astype(BF16)
        dp_ref[5] = dmb[:, ATT_WIDTH:].astype(BF16)
        dp_ref[6] = dga.astype(BF16)

    def col(j):
        return pl.BlockSpec((tm, D_MODEL), lambda i: (i, j))

    def whole(shape):
        return pl.BlockSpec(shape, lambda i: (0, 0))

    row = col(0)
    att = pl.BlockSpec((tm, ATT_WIDTH), lambda i: (i, 0))
    bf = jax.ShapeDtypeStruct((S, D_MODEL), BF16)
    return pl.pallas_call(
        body, name="out_bwd", grid=(S // tm,),
        in_specs=[row, row, row, row, row, col(1), col(2), col(3),
                  pl.BlockSpec((tm, ATT_WIDTH), lambda i: (i, 8)), att,
                  whole((8, D_MODEL)), whole((ATT_WIDTH, D_MODEL)), whole((D_MODEL, D_MODEL)),
                  whole((D_MODEL, D_MODEL))],
        out_specs=(row, row, row, att, pl.BlockSpec((tm, 128), lambda i: (i, 0)), row,
                   pl.BlockSpec((7, tm, ATT_WIDTH), lambda i: (2, i, 0)), whole((8, D_MODEL))),
        out_shape=(bf, bf, bf, jax.ShapeDtypeStruct((S, ATT_WIDTH), BF16),
                   jax.ShapeDtypeStruct((S, 128), F32), jax.ShapeDtypeStruct((S, D_MODEL), F32),
                   jax.ShapeDtypeStruct((DPROJ_CHUNKS, S, ATT_WIDTH), BF16),
                   jax.ShapeDtypeStruct((8, D_MODEL), F32)),
        compiler_params=_params(("arbitrary",)),
    )(dxn, mix, ya, yb, hl, rest, rest, rest, rest, o_att, vec, w_pa, w_pb, w_o)


def _loss_head(y, target):
    S = y.shape[0]
    tm = 512

    def body(y_ref, t_ref, dy_ref, acc_ref):
        i = pl.program_id(0)

        @pl.when(i == 0)
        def _():
            acc_ref[...] = jnp.zeros_like(acc_ref)

        err = y_ref[...] - t_ref[...]
        dy_ref[...] = err * (1.0 / D_MODEL)
        part = jnp.sum(jnp.sum(err * err, axis=1, keepdims=True), axis=0, keepdims=True)
        acc_ref[...] += jnp.broadcast_to(part, acc_ref.shape)

    row = pl.BlockSpec((tm, D_MODEL), lambda i: (i, 0))
    return pl.pallas_call(
        body, name="loss_head", grid=(S // tm,),
        in_specs=[row, row],
        out_specs=(row, pl.BlockSpec((8, 128), lambda i: (0, 0))),
        out_shape=(jax.ShapeDtypeStruct((S, D_MODEL), F32), jax.ShapeDtypeStruct((8, 128), F32)),
        compiler_params=_params(("arbitrary",)))(y, target)


def _adamw(gsrc, w, m, v, n_stack, name, rows=None):
    R, C = w.shape
    budget = 96 * 1024
    tr = rows
    if tr is None:
        tr = R
        while tr * C > budget and tr % 16 == 0:
            tr //= 2
    assert R % tr == 0 and (tr % 8 == 0 or tr == R)
    c1 = 1.0 - ADAM_B1 ** ADAM_STEP
    c2 = 1.0 - ADAM_B2 ** ADAM_STEP

    def body(g_ref, w_ref, m_ref, v_ref, go_ref, d_ref, mo_ref, vo_ref):
        if n_stack:
            g = g_ref[0]
            for s in range(1, n_stack):
                g = g + g_ref[s]
        else:
            g = g_ref[...]
        m_new = ADAM_B1 * m_ref[...] + (1.0 - ADAM_B1) * g
        v_new = ADAM_B2 * v_ref[...] + (1.0 - ADAM_B2) * (g * g)
        m_hat = m_new / c1
        v_hat = v_new / c2
        go_ref[...] = g
        d_ref[...] = -ADAM_LR * (m_hat / (jnp.sqrt(v_hat) + ADAM_EPS) + ADAM_WD * w_ref[...])
        mo_ref[...] = m_new
        vo_ref[...] = v_new

    blk = pl.BlockSpec((tr, C), lambda i: (i, 0))
    g_spec = pl.BlockSpec((n_stack, tr, C), lambda i: (0, i, 0)) if n_stack else blk
    shp = jax.ShapeDtypeStruct((R, C), F32)
    return pl.pallas_call(
        body, name=name, grid=(R // tr,),
        in_specs=[g_spec, blk, blk, blk], out_specs=(blk, blk, blk, blk),
        out_shape=(shp, shp, shp, shp),
        compiler_params=_params(("parallel",)))(gsrc, w, m, v)


def _pair_blocks(w):
    w = w.reshape(8, 2, 64, 64)
    z = jnp.zeros((8, 64, 64), w.dtype)
    top = jnp.concatenate([w[:, 0], z], axis=2)
    bot = jnp.concatenate([z, w[:, 1]], axis=2)
    return jnp.concatenate([top, bot], axis=1).astype(BF16)


def _unpair_blocks(g):
    return jnp.stack([g[:, :64, :64], g[:, 64:, 64:]], axis=1).reshape(16, 64, 64)


def _layer_fwd(x, mod, p):
    zeros = jnp.zeros((5, D_MODEL), F32)
    vec_pre = jnp.concatenate([p["g_pre"][None], mod[1:2], mod[0:1], zeros], axis=0)
    h = _norm_mod_fwd(x, vec_pre)
    qkv = _proj_fwd(h, p["w_int"], 0, BF16, "proj_fwd_qkv")
    rest = _proj_fwd(h, p["w_int"], QKV_CHUNKS, F32, "proj_fwd_rest")
    outs, stats = [], []
    for g, r in enumerate(DILATIONS):
        o_g, st_g = _attn_fwd(qkv, g, r)
        S = x.shape[0]
        outs.append(o_g.reshape(S, ATT_WIDTH))
        stats.append(st_g.reshape(S, 128))
    o_att, a_gated, lse = _attn_combine(outs, stats, rest)
    hl, a_dec = _lru_fwd(rest, p["cw"], p["wr2"], p["wi2"])
    vec_post = jnp.concatenate([p["g_post"][None], mod[2:3], jnp.zeros((6, D_MODEL), F32)], axis=0)
    x_new, mix, ya, yb, b_gated = _out_fwd(a_gated, hl, rest, x, vec_post, p["w_pa"], p["w_pb"], p["w_o"])
    saved = dict(x=x, h=h, qkv=qkv, rest=rest, o_att=o_att, a_gated=a_gated, lse=lse, hl=hl, a_dec=a_dec,
                 mix=mix, ya=ya, yb=yb, b_gated=b_gated, vec_pre=vec_pre, vec_post=vec_post)
    return x_new, saved


def _layer_bwd(dxn, sv, p):
    dout, dya, dyb, do, dvec, dhl, dproj, acc_out = _out_bwd(
        dxn, sv["mix"], sv["ya"], sv["yb"], sv["hl"], sv["rest"], sv["o_att"], sv["vec_post"],
        p["w_pa"], p["w_pb"], p["w_o"])
    dproj, acc_lru, gwr2, gwi2 = _lru_bwd(dhl, sv["hl"], sv["a_dec"], sv["rest"], p["cw"], p["wr2"],
                                          p["wi2"], dproj)
    for g, r in enumerate(DILATIONS):
        dproj = _attn_bwd(sv["qkv"], do, sv["lse"], dvec, dproj, g, r)
    dh = _proj_dgrad(dproj, p["w_int"])
    dx, acc_pre = _norm_mod_bwd(dh, sv["x"], dxn, sv["vec_pre"])
    grads = dict(
        w_in=_proj_wgrad(sv["h"], dproj),
        w_o=_wgrad_tn(sv["mix"], dout, "wgrad_o"),
        w_pa=_wgrad_tn(sv["a_gated"], dya, "wgrad_pa"),
        w_pb=_wgrad_tn(sv["b_gated"], dyb, "wgrad_pb"),
        dmod=jnp.concatenate([acc_pre[0], acc_pre[1], acc_out[0]]),
        g_pre=acc_pre[2], g_post=acc_out[1],
        conv_w=acc_lru[0:4], conv_b=acc_lru[4], b_rg=acc_lru[5], b_ig=acc_lru[6],
        lru_lambda=acc_lru[7] * (-jax.nn.sigmoid(-p["lru_lambda"])),
        w_rg=_unpair_blocks(gwr2), w_ig=_unpair_blocks(gwi2))
    return dx, grads


SMALL_NAMES = ("b_mod", "g_pre", "conv_b", "w_rg", "b_rg", "w_ig", "b_ig", "lru_lambda", "g_post")


def kernel(x, c, w_mod, b_mod, g_pre, w_in, conv_w, conv_b, w_rg, b_rg, w_ig, b_ig, lru_lambda, w_pa, w_pb, w_o, g_post, loss_target, m_w_mod, m_b_mod, m_g_pre, m_w_in, m_conv_w, m_conv_b, m_w_rg, m_b_rg, m_w_ig, m_b_ig, m_lru_lambda, m_w_pa, m_w_pb, m_w_o, m_g_post, v_w_mod, v_b_mod, v_g_pre, v_w_in, v_conv_w, v_conv_b, v_w_rg, v_b_rg, v_w_ig, v_b_ig, v_lru_lambda, v_w_pa, v_w_pb, v_w_o, v_g_post):
    W = dict(w_mod=w_mod, b_mod=b_mod, g_pre=g_pre, w_in=w_in, conv_w=conv_w, conv_b=conv_b, w_rg=w_rg,
             b_rg=b_rg, w_ig=w_ig, b_ig=b_ig, lru_lambda=lru_lambda, w_pa=w_pa, w_pb=w_pb, w_o=w_o,
             g_post=g_post)
    M = dict(w_mod=m_w_mod, b_mod=m_b_mod, g_pre=m_g_pre, w_in=m_w_in, conv_w=m_conv_w, conv_b=m_conv_b,
             w_rg=m_w_rg, b_rg=m_b_rg, w_ig=m_w_ig, b_ig=m_b_ig, lru_lambda=m_lru_lambda, w_pa=m_w_pa,
             w_pb=m_w_pb, w_o=m_w_o, g_post=m_g_post)
    V = dict(w_mod=v_w_mod, b_mod=v_b_mod, g_pre=v_g_pre, w_in=v_w_in, conv_w=v_conv_w, conv_b=v_conv_b,
             w_rg=v_w_rg, b_rg=v_b_rg, w_ig=v_w_ig, b_ig=v_b_ig, lru_lambda=v_lru_lambda, w_pa=v_w_pa,
             w_pb=v_w_pb, w_o=v_w_o, g_post=v_g_post)
    S = x.shape[1]
    me = 4 * lax.axis_index("x") + 2 * lax.axis_index("y") + lax.axis_index("c")
    n_mod = w_mod.shape[2]
    n_in = w_in.shape[2]
    n_conv = conv_w.shape[2]

    c_rows = jnp.broadcast_to(c, (8, D_MODEL))
    g_c, g_win, g_wpa, g_wpb, g_wo, g_cw = _exchange(
        [c_rows, w_in.astype(BF16), w_pa.astype(BF16), w_pb.astype(BF16), w_o.astype(BF16), conv_w],
        ["ag"] * 6, "gather_weights")
    c_all = g_c[:, 0, :]
    c_pad = jnp.concatenate([c_all, jnp.zeros((8, D_MODEL), F32)], axis=0)
    conv_w_full = jnp.transpose(g_cw, (1, 2, 0, 3)).reshape(2, CONV_WIDTH, D_MODEL)

    mod_cols = _mod_fwd(c_pad, w_mod)
    mod_src = jnp.transpose(mod_cols[:, :8], (1, 0, 2))
    mod_src = jnp.concatenate([mod_src, jnp.zeros((8, 6, n_mod), F32)], axis=1)
    (mod_got,) = _exchange([mod_src], ["a2a"], "scatter_mod")
    mod = jnp.transpose(mod_got[:, :2], (1, 0, 2)).reshape(2, 3 * D_MODEL) + b_mod

    layers = []
    for layer in range(2):
        w_full = jnp.transpose(g_win[:, layer], (1, 0, 2)).reshape(D_MODEL, N_CHUNKS, ATT_WIDTH)
        w_int = w_full[:, jnp.array(CHUNK_PERM)].reshape(D_MODEL, N_CHUNKS * ATT_WIDTH)
        cw = jnp.concatenate([conv_w_full[layer], conv_b[layer][None], b_rg[layer][None], b_ig[layer][None],
                              lru_lambda[layer][None]], axis=0)
        layers.append(dict(
            w_int=w_int,
            w_pa=jnp.transpose(g_wpa[:, layer], (1, 0, 2)).reshape(ATT_WIDTH, D_MODEL),
            w_pb=g_wpb[:, layer].reshape(D_MODEL, D_MODEL),
            w_o=g_wo[:, layer].reshape(D_MODEL, D_MODEL),
            cw=cw, wr2=_pair_blocks(w_rg[layer]), wi2=_pair_blocks(w_ig[layer]),
            g_pre=g_pre[layer], g_post=g_post[layer], lru_lambda=lru_lambda[layer]))

    act = x[0]
    saved = []
    for layer in range(2):
        act, sv = _layer_fwd(act, mod[layer].reshape(3, D_MODEL), layers[layer])
        saved.append(sv)
    dy, loss_acc = _loss_head(act, loss_target[0])
    loss = lax.psum(0.5 * loss_acc[0, 0] / D_MODEL, MESH_AXES)
    grads = [None, None]
    for layer in (1, 0):
        dy, grads[layer] = _layer_bwd(dy, saved[layer], layers[layer])
    grad_x = dy[None]

    def stack2(name):
        return jnp.stack([grads[0][name], grads[1][name]], axis=0)

    gw_in = stack2("w_in").reshape(2, D_MODEL, N_DEV, n_in)
    gw_in = jnp.transpose(gw_in, (2, 0, 1, 3))
    gw_pa = jnp.transpose(stack2("w_pa").reshape(2, ATT_WIDTH, N_DEV, 128), (2, 0, 1, 3))
    gw_pb = jnp.transpose(stack2("w_pb").reshape(2, N_DEV, 128, D_MODEL), (1, 0, 2, 3))
    gw_o = jnp.transpose(stack2("w_o").reshape(2, N_DEV, 128, D_MODEL), (1, 0, 2, 3))
    dmod = stack2("dmod")
    dmod_src = jnp.transpose(dmod.reshape(2, N_DEV, n_mod), (1, 0, 2))
    dmod_src = jnp.concatenate([dmod_src, jnp.zeros((8, 6, n_mod), F32)], axis=1)
    small = {"b_mod": dmod}
    for name in SMALL_NAMES[1:]:
        small[name] = stack2(name)
    small_vec = jnp.concatenate([small[n].reshape(-1) for n in SMALL_NAMES] + [stack2("conv_w").reshape(-1)])
    n_small = small_vec.shape[0]
    small_vec = small_vec.reshape(n_small // 128, 128)
    r_in, r_pa, r_pb, r_o, r_dmod, r_small = _exchange(
        [gw_in, gw_pa, gw_pb, gw_o, dmod_src, small_vec], ["a2a", "a2a", "a2a", "a2a", "a2a", "ag"],
        "exchange_grads")

    res = {}
    res["w_in"] = _adamw(r_in.reshape(8, 2 * D_MODEL, n_in), w_in.reshape(2 * D_MODEL, n_in),
                         m_w_in.reshape(2 * D_MODEL, n_in), v_w_in.reshape(2 * D_MODEL, n_in), 8, "adamw_w_in")
    res["w_pa"] = _adamw(r_pa.reshape(8, 2 * ATT_WIDTH, 128), w_pa.reshape(2 * ATT_WIDTH, 128),
                         m_w_pa.reshape(2 * ATT_WIDTH, 128), v_w_pa.reshape(2 * ATT_WIDTH, 128), 8, "adamw_w_pa")
    res["w_pb"] = _adamw(r_pb.reshape(8, 256, D_MODEL), w_pb.reshape(256, D_MODEL),
                         m_w_pb.reshape(256, D_MODEL), v_w_pb.reshape(256, D_MODEL), 8, "adamw_w_pb")
    res["w_o"] = _adamw(r_o.reshape(8, 256, D_MODEL), w_o.reshape(256, D_MODEL),
                        m_w_o.reshape(256, D_MODEL), v_w_o.reshape(256, D_MODEL), 8, "adamw_w_o")
    dmod_all = jnp.transpose(r_dmod[:, :2], (1, 0, 2))
    gw_mod = _mod_wgrad(jnp.transpose(c_all), dmod_all)
    res["w_mod"] = _adamw(gw_mod.reshape(2 * D_MODEL, n_mod), w_mod.reshape(2 * D_MODEL, n_mod),
                          m_w_mod.reshape(2 * D_MODEL, n_mod), v_w_mod.reshape(2 * D_MODEL, n_mod), 0,
                          "adamw_w_mod")
    n_rep = sum(W[n].size for n in SMALL_NAMES)
    rep_rows = n_rep // 128

    def flat_rep(src):
        return jnp.concatenate([src[n].reshape(-1) for n in SMALL_NAMES]).reshape(rep_rows, 128)

    rep = _adamw(r_small, flat_rep(W), flat_rep(M), flat_rep(V), 8, "adamw_small", rows=rep_rows // 2)
    off = 0
    for name in SMALL_NAMES:
        size = W[name].size
        res[name] = tuple(t.reshape(-1)[off:off + size].reshape(W[name].shape) for t in rep)
        off += size
    conv_stack = r_small[:, rep_rows:].reshape(8, 2, CONV_WIDTH, D_MODEL)
    conv_stack = lax.dynamic_slice_in_dim(conv_stack, me * n_conv, n_conv, axis=3).reshape(8, 8, n_conv)
    res["conv_w"] = _adamw(conv_stack, conv_w.reshape(8, n_conv), m_conv_w.reshape(8, n_conv),
                           v_conv_w.reshape(8, n_conv), 8, "adamw_conv_w")

    names = ("w_mod", "b_mod", "g_pre", "w_in", "conv_w", "conv_b", "w_rg", "b_rg", "w_ig", "b_ig",
             "lru_lambda", "w_pa", "w_pb", "w_o", "g_post")
    outs = [loss, grad_x]
    for k in range(4):
        outs.extend(res[n][k].reshape(W[n].shape) for n in names)
    return tuple(outs)
```

```python
import functools

import jax
import jax.numpy as jnp
from jax import lax
from jax.experimental import pallas as pl
from jax.experimental.pallas import tpu as pltpu

F32 = jnp.float32
BF16 = jnp.bfloat16

N_DEV = 8
D_MODEL = 1024
HEAD_DIM = 128
HEADS = 4
ATT_WIDTH = HEADS * HEAD_DIM
DILATIONS = (1, 4, 16)
BAND = 128
N_CHUNKS = 18
QKV_CHUNKS = 9
CONV_WIDTH = 4
LRU_C = 8.0
NORM_EPS = 1e-6
NEG_INF = -1e30
ADAM_LR = 0.001
ADAM_B1 = 0.9
ADAM_B2 = 0.999
ADAM_EPS = 1e-08
ADAM_WD = 0.01
ADAM_STEP = 10
CHUNK_PERM = (0, 3, 6, 1, 4, 7, 2, 5, 8, 10, 11, 12, 13, 14, 15, 16, 17, 9)
DREST_CHUNKS = 14
VMEM_LIMIT = 56 * 1024 * 1024
MESH_AXES = ("x", "y", "c")


def _params(sem=None):
    return pltpu.CompilerParams(dimension_semantics=sem, vmem_limit_bytes=VMEM_LIMIT)


def _drest_chunk(k):
    return jnp.where(k >= 11, k - 4, jnp.maximum(k - 9, 0))


def _group_chunk(k, g):
    return jnp.clip(k - 3 * g, 0, 2)


def _orig_chunk(k):
    return jnp.where(k < 9, (k % 3) * 3 + k // 3, jnp.where(k == 17, 9, k + 1))


def _silu(x):
    return x * jax.nn.sigmoid(x)


def _dsilu(x):
    s = jax.nn.sigmoid(x)
    return s * (1.0 + x * (1.0 - s))


def _neg_expm1(x):
    series = -x * (1.0 + x * (0.5 + x * (1.0 / 6.0 + x * (1.0 / 24.0 + x * (1.0 / 120.0)))))
    return jnp.where(x > -0.05, series, 1.0 - jnp.exp(x))


def _softplus_neg(lam):
    z = jnp.exp(-jnp.abs(lam))
    small = z * (1.0 - z * (0.5 - z * (1.0 / 3.0 - z * 0.25)))
    log1p_z = jnp.where(z < 1e-2, small, jnp.log(1.0 + z))
    return jnp.maximum(-lam, 0.0) + log1p_z


def _dot(a, b):
    return jnp.dot(a, b, preferred_element_type=F32)


def _dot_nt(a, b):
    return lax.dot_general(a, b, (((1,), (1,)), ((), ())), preferred_element_type=F32)


def _dot_tn(a, b):
    return lax.dot_general(a, b, (((0,), (0,)), ((), ())), preferred_element_type=F32)


def _exchange(arrs, modes, name):
    n = len(arrs)
    out_shapes = []
    for a, mode in zip(arrs, modes):
        shape = ((N_DEV,) + a.shape) if mode == "ag" else a.shape
        out_shapes.append(jax.ShapeDtypeStruct(shape, a.dtype))

    def body(*refs):
        ins, outs = refs[:n], refs[n:2 * n]
        send_sems, recv_sems, local_sems = refs[2 * n:]
        x, y, c = lax.axis_index("x"), lax.axis_index("y"), lax.axis_index("c")
        me = 4 * x + 2 * y + c

        def src_for(a, dev):
            return ins[a] if modes[a] == "ag" else ins[a].at[dev]

        local = []
        for a in range(n):
            cp = pltpu.make_async_copy(src_for(a, me), outs[a].at[me], local_sems.at[a])
            cp.start()
            local.append(cp)
        started, arrivals = [], []
        for k in range(1, N_DEV):
            px = 1 - x if (k >> 2) & 1 else x
            py = 1 - y if (k >> 1) & 1 else y
            pc = 1 - c if k & 1 else c
            peer = 4 * px + 2 * py + pc
            for a in range(n):
                s = (k - 1) * n + a
                cp = pltpu.make_async_remote_copy(
                    src_ref=src_for(a, peer), dst_ref=outs[a].at[me],
                    send_sem=send_sems.at[s], recv_sem=recv_sems.at[s],
                    device_id=(px, py, pc), device_id_type=pl.DeviceIdType.MESH)
                cp.start()
                started.append(cp)
                arrivals.append(pltpu.make_async_remote_copy(
                    src_ref=src_for(a, peer), dst_ref=outs[a].at[peer],
                    send_sem=send_sems.at[s], recv_sem=recv_sems.at[s],
                    device_id=(px, py, pc), device_id_type=pl.DeviceIdType.MESH))
        for cp in arrivals:
            cp.wait_recv()
        for cp in started:
            cp.wait_send()
        for cp in local:
            cp.wait()

    any_spec = pl.BlockSpec(memory_space=pl.ANY)
    outs = pl.pallas_call(
        body, name=name,
        out_shape=tuple(out_shapes),
        in_specs=[any_spec] * n, out_specs=tuple([any_spec] * n),
        scratch_shapes=[pltpu.SemaphoreType.DMA((7 * n,)), pltpu.SemaphoreType.DMA((7 * n,)),
                        pltpu.SemaphoreType.DMA((n,))],
    )(*arrs)
    return list(outs)


def _mod_fwd(c_pad, w_mod):
    def body(c_ref, w_ref, o_ref):
        sc = _silu(c_ref[...]).astype(BF16)
        for layer in range(2):
            o_ref[layer] = _dot(sc, w_ref[layer].astype(BF16))

    return pl.pallas_call(
        body, name="mod_fwd", out_shape=jax.ShapeDtypeStruct((2, 16, w_mod.shape[2]), F32),
        compiler_params=_params())(c_pad, w_mod)


def _mod_wgrad(c_t, dmod):
    n_cols = dmod.shape[2]

    def body(c_ref, d_ref, o_ref):
        sc = _silu(c_ref[...]).astype(BF16).astype(F32)
        for layer in range(2):
            dm = d_ref[layer].astype(BF16).astype(F32)
            acc = sc[:, 0:1] * dm[0:1, :]
            for b in range(1, N_DEV):
                acc = acc + sc[:, b:b + 1] * dm[b:b + 1, :]
            o_ref[layer] = acc

    return pl.pallas_call(
        body, name="mod_wgrad", out_shape=jax.ShapeDtypeStruct((2, D_MODEL, n_cols), F32),
        compiler_params=_params())(c_t, dmod)


def _norm_mod_fwd(x, vec):
    S = x.shape[0]
    tm = 512

    def body(x_ref, v_ref, h_ref):
        xv = x_ref[...]
        rstd = lax.rsqrt(jnp.mean(xv * xv, axis=-1, keepdims=True) + NORM_EPS)
        h = (xv * rstd) * v_ref[0:1, :]
        h_ref[...] = (h * (1.0 + v_ref[1:2, :]) + v_ref[2:3, :]).astype(BF16)

    return pl.pallas_call(
        body, name="norm_mod_fwd", grid=(S // tm,),
        in_specs=[pl.BlockSpec((tm, D_MODEL), lambda i: (i, 0)),
                  pl.BlockSpec((8, D_MODEL), lambda i: (0, 0))],
        out_specs=pl.BlockSpec((tm, D_MODEL), lambda i: (i, 0)),
        out_shape=jax.ShapeDtypeStruct((S, D_MODEL), BF16),
        compiler_params=_params(("parallel",)))(x, vec)


def _norm_mod_bwd(dh, x, dxn, vec):
    S = x.shape[0]
    tm = 512

    def body(dh_ref, x_ref, dxn_ref, v_ref, dx_ref, acc_ref):
        i = pl.program_id(0)

        @pl.when(i == 0)
        def _():
            acc_ref[...] = jnp.zeros_like(acc_ref)

        xv = x_ref[...]
        dhv = dh_ref[...]
        g = v_ref[0:1, :]
        sc1 = 1.0 + v_ref[1:2, :]
        rstd = lax.rsqrt(jnp.mean(xv * xv, axis=-1, keepdims=True) + NORM_EPS)
        xhat = xv * rstd
        acc_ref[0:1, :] += jnp.sum(dhv, axis=0, keepdims=True)
        acc_ref[1:2, :] += jnp.sum(dhv * (xhat * g), axis=0, keepdims=True)
        acc_ref[2:3, :] += jnp.sum(dhv * xhat * sc1, axis=0, keepdims=True)
        dxhat = dhv * (g * sc1)
        dx = rstd * (dxhat - xhat * jnp.mean(dxhat * xhat, axis=-1, keepdims=True))
        dx_ref[...] = dx + dxn_ref[...]

    row = pl.BlockSpec((tm, D_MODEL), lambda i: (i, 0))
    vec_spec = pl.BlockSpec((8, D_MODEL), lambda i: (0, 0))
    return pl.pallas_call(
        body, name="norm_mod_bwd", grid=(S // tm,),
        in_specs=[row, row, row, vec_spec],
        out_specs=(row, vec_spec),
        out_shape=(jax.ShapeDtypeStruct((S, D_MODEL), F32), jax.ShapeDtypeStruct((8, D_MODEL), F32)),
        compiler_params=_params(("arbitrary",)))(dh, x, dxn, vec)


def _proj_fwd(h, w_int, first_chunk, n_chunks, out_dtype, name):
    S = h.shape[0]
    tm, tn = 1024, ATT_WIDTH

    def body(h_ref, w_ref, o_ref):
        o_ref[...] = _dot(h_ref[...], w_ref[...]).astype(out_dtype)

    return pl.pallas_call(
        body, name=name, grid=(S // tm, n_chunks),
        in_specs=[pl.BlockSpec((tm, D_MODEL), lambda i, j: (i, 0)),
                  pl.BlockSpec((D_MODEL, tn), lambda i, j: (0, first_chunk + j))],
        out_specs=pl.BlockSpec((tm, tn), lambda i, j: (i, j)),
        out_shape=jax.ShapeDtypeStruct((S, n_chunks * tn), out_dtype),
        compiler_params=_params(("parallel", "parallel")))(h, w_int)


def _chunk_sources(d_refs, k):
    def dispatch(fn):
        for g in range(3):
            pl.when(jnp.logical_and(k >= 3 * g, k < 3 * g + 3))(functools.partial(fn, d_refs[g]))
        pl.when(k >= QKV_CHUNKS)(functools.partial(fn, d_refs[3]))
    return dispatch


def _chunk_specs(rows, grid_to_ki):
    def spec(chunk_of):
        def index(*grid):
            k, i = grid_to_ki(*grid)
            return (chunk_of(k), i, 0)
        return pl.BlockSpec((None, rows, ATT_WIDTH), index)
    return [spec(functools.partial(_group_chunk, g=g)) for g in range(3)] + [spec(_drest_chunk)]


def _proj_dgrad(dqkv, drest, w_int):
    S = drest.shape[1]
    tm = 1024

    def body(d0_ref, d1_ref, d2_ref, d3_ref, w_ref, o_ref):
        k = pl.program_id(1)

        @pl.when(k == 0)
        def _():
            o_ref[...] = jnp.zeros_like(o_ref)

        def add(d_ref):
            o_ref[...] += _dot_nt(d_ref[...], w_ref[...])

        _chunk_sources((d0_ref, d1_ref, d2_ref, d3_ref), k)(add)

    return pl.pallas_call(
        body, name="proj_dgrad", grid=(S // tm, N_CHUNKS),
        in_specs=_chunk_specs(tm, lambda i, k: (k, i)) + [pl.BlockSpec((D_MODEL, ATT_WIDTH), lambda i, k: (0, k))],
        out_specs=pl.BlockSpec((tm, D_MODEL), lambda i, k: (i, 0)),
        out_shape=jax.ShapeDtypeStruct((S, D_MODEL), F32),
        compiler_params=_params(("parallel", "arbitrary")))(*dqkv, drest, w_int)


def _proj_wgrad(h, dqkv, drest):
    S = h.shape[0]
    ts = 1024

    def body(h_ref, d0_ref, d1_ref, d2_ref, d3_ref, o_ref):
        k = pl.program_id(0)
        i = pl.program_id(1)

        @pl.when(i == 0)
        def _():
            o_ref[...] = jnp.zeros_like(o_ref)

        def add(d_ref):
            o_ref[...] += _dot_tn(h_ref[...], d_ref[...])

        _chunk_sources((d0_ref, d1_ref, d2_ref, d3_ref), k)(add)

    return pl.pallas_call(
        body, name="proj_wgrad", grid=(N_CHUNKS, S // ts),
        in_specs=[pl.BlockSpec((ts, D_MODEL), lambda k, i: (i, 0))] + _chunk_specs(ts, lambda k, i: (k, i)),
        out_specs=pl.BlockSpec((D_MODEL, ATT_WIDTH), lambda k, i: (0, _orig_chunk(k))),
        out_shape=jax.ShapeDtypeStruct((D_MODEL, N_CHUNKS * ATT_WIDTH), F32),
        compiler_params=_params(("parallel", "arbitrary")))(h, *dqkv, drest)


def _wgrad_tn(a, b, name):
    S, M = a.shape
    N = b.shape[1]
    ts = 1024

    def body(a_ref, b_ref, o_ref):
        i = pl.program_id(0)
        part = _dot_tn(a_ref[...], b_ref[...])

        @pl.when(i == 0)
        def _():
            o_ref[...] = part

        @pl.when(i > 0)
        def _():
            o_ref[...] += part

    return pl.pallas_call(
        body, name=name, grid=(S // ts,),
        in_specs=[pl.BlockSpec((ts, M), lambda i: (i, 0)), pl.BlockSpec((ts, N), lambda i: (i, 0))],
        out_specs=pl.BlockSpec((M, N), lambda i: (0, 0)),
        out_shape=jax.ShapeDtypeStruct((M, N), F32),
        compiler_params=_params(("arbitrary",)))(a, b)


def _attn_fwd(qkv, g, r):
    S = qkv.shape[0]
    L = S // r
    nb = L // BAND
    scale = HEAD_DIM ** -0.5
    qv = qkv.reshape(L, r * 3 * ATT_WIDTH)

    def body(q_ref, kp_ref, kc_ref, vp_ref, vc_ref, o_ref, st_ref):
        n = pl.program_id(1)
        ii = lax.broadcasted_iota(jnp.int32, (BAND, BAND), 0)
        kk = lax.broadcasted_iota(jnp.int32, (BAND, BAND), 1)
        mask_prev = jnp.logical_and(kk >= ii, n > 0)
        mask_cur = kk <= ii
        lane = lax.broadcasted_iota(jnp.int32, (BAND, 128), 1)
        stats = jnp.zeros((BAND, 128), F32)
        for h in range(HEADS):
            sl = slice(h * HEAD_DIM, (h + 1) * HEAD_DIM)
            q = q_ref[:, sl]
            s_p = jnp.where(mask_prev, _dot_nt(q, kp_ref[:, sl]) * scale, NEG_INF)
            s_c = jnp.where(mask_cur, _dot_nt(q, kc_ref[:, sl]) * scale, NEG_INF)
            m = jnp.maximum(jnp.max(s_p, axis=1, keepdims=True), jnp.max(s_c, axis=1, keepdims=True))
            p_p = jnp.exp(s_p - m)
            p_c = jnp.exp(s_c - m)
            l = jnp.sum(p_p, axis=1, keepdims=True) + jnp.sum(p_c, axis=1, keepdims=True)
            o = _dot(p_p.astype(BF16), vp_ref[:, sl]) + _dot(p_c.astype(BF16), vc_ref[:, sl])
            o_ref[:, sl] = o / l
            stats = jnp.where(lane == h, m, stats)
            stats = jnp.where(lane == HEADS + h, l, stats)
        st_ref[...] = stats

    blk = (BAND, ATT_WIDTH)
    return pl.pallas_call(
        body, name=f"attn_fwd_g{g}", grid=(r, nb),
        in_specs=[
            pl.BlockSpec(blk, lambda rho, n: (n, rho * 3)),
            pl.BlockSpec(blk, lambda rho, n: (jnp.maximum(n - 1, 0), rho * 3 + 1)),
            pl.BlockSpec(blk, lambda rho, n: (n, rho * 3 + 1)),
            pl.BlockSpec(blk, lambda rho, n: (jnp.maximum(n - 1, 0), rho * 3 + 2)),
            pl.BlockSpec(blk, lambda rho, n: (n, rho * 3 + 2)),
        ],
        out_specs=(pl.BlockSpec(blk, lambda rho, n: (n, rho)),
                   pl.BlockSpec((BAND, 128), lambda rho, n: (n, rho))),
        out_shape=(jax.ShapeDtypeStruct((L, r * ATT_WIDTH), F32),
                   jax.ShapeDtypeStruct((L, r * 128), F32)),
        compiler_params=_params(("parallel", "parallel")))(qv, qv, qv, qv, qv)


def _attn_combine(outs, stats, rest):
    S = rest.shape[0]
    tm = 512
    gatt_blk = 8

    def body(o0_ref, o1_ref, o2_ref, s0_ref, s1_ref, s2_ref, g_ref, o_ref, a_ref, lse_ref):
        o_refs = (o0_ref, o1_ref, o2_ref)
        st = (s0_ref[...], s1_ref[...], s2_ref[...])
        lane = lax.broadcasted_iota(jnp.int32, (tm, 128), 1)
        lse_out = jnp.zeros((tm, 128), F32)
        for h in range(HEADS):
            sl = slice(h * HEAD_DIM, (h + 1) * HEAD_DIM)
            ms = [s[:, h:h + 1] for s in st]
            ls = [s[:, HEADS + h:HEADS + h + 1] for s in st]
            m_all = jnp.maximum(jnp.maximum(ms[0], ms[1]), ms[2])
            ws = [l * jnp.exp(m - m_all) for m, l in zip(ms, ls)]
            den = ws[0] + ws[1] + ws[2]
            o = (ws[0] * o_refs[0][:, sl] + ws[1] * o_refs[1][:, sl] + ws[2] * o_refs[2][:, sl]) / den
            o_ref[:, sl] = o
            a_ref[:, sl] = (o * _silu(g_ref[:, sl])).astype(BF16)
            lse_out = jnp.where(lane == h, m_all + jnp.log(den), lse_out)
        lse_ref[...] = lse_out

    o_spec = pl.BlockSpec((tm, ATT_WIDTH), lambda i: (i, 0))
    s_spec = pl.BlockSpec((tm, 128), lambda i: (i, 0))
    return pl.pallas_call(
        body, name="attn_combine", grid=(S // tm,),
        in_specs=[o_spec, o_spec, o_spec, s_spec, s_spec, s_spec,
                  pl.BlockSpec((tm, ATT_WIDTH), lambda i: (i, gatt_blk))],
        out_specs=(o_spec, o_spec, s_spec),
        out_shape=(jax.ShapeDtypeStruct((S, ATT_WIDTH), F32), jax.ShapeDtypeStruct((S, ATT_WIDTH), BF16),
                   jax.ShapeDtypeStruct((S, 128), F32)),
        compiler_params=_params(("parallel",)))(*outs, *stats, rest)


def _attn_bwd(qkv, do, lse, dvec, g, r):
    S = qkv.shape[0]
    L = S // r
    nb = L // BAND
    scale = HEAD_DIM ** -0.5
    qv = qkv.reshape(L, r * 3 * ATT_WIDTH)
    dov = do.reshape(L, r * ATT_WIDTH)
    lsev = lse.reshape(L, r * 128)
    dvv = dvec.reshape(L, r * 128)

    def body(qc_ref, qn_ref, k_ref, v_ref, doc_ref, don_ref, lc_ref, ln_ref, dc_ref, dn_ref,
             out_ref, carry_ref):
        j = pl.program_id(1)

        @pl.when(j == 0)
        def _():
            carry_ref[...] = jnp.zeros_like(carry_ref)

        ii = lax.broadcasted_iota(jnp.int32, (BAND, BAND), 0)
        kk = lax.broadcasted_iota(jnp.int32, (BAND, BAND), 1)
        mask_diag = kk <= ii
        mask_off = jnp.logical_and(kk >= ii, j < nb - 1)
        lc, ln, dc, dn = lc_ref[...], ln_ref[...], dc_ref[...], dn_ref[...]
        for h in range(HEADS):
            sl = slice(h * HEAD_DIM, (h + 1) * HEAD_DIM)
            k = k_ref[:, sl]
            v = v_ref[:, sl]
            q1, do1 = qc_ref[:, sl], doc_ref[:, sl]
            p1 = jnp.where(mask_diag, jnp.exp(_dot_nt(q1, k) * scale - lc[:, h:h + 1]), 0.0)
            ds1 = (p1 * (_dot_nt(do1, v) - dc[:, h:h + 1])).astype(BF16)
            q2, do2 = qn_ref[:, sl], don_ref[:, sl]
            p2 = jnp.where(mask_off, jnp.exp(_dot_nt(q2, k) * scale - ln[:, h:h + 1]), 0.0)
            ds2 = (p2 * (_dot_nt(do2, v) - dn[:, h:h + 1])).astype(BF16)
            dq = carry_ref[:, sl] + _dot(ds1, k) * scale
            dk = (_dot_tn(ds1, q1) + _dot_tn(ds2, q2)) * scale
            dv = _dot_tn(p1.astype(BF16), do1) + _dot_tn(p2.astype(BF16), do2)
            out_ref[0, :, sl] = dq.astype(BF16)
            out_ref[1, :, sl] = dk.astype(BF16)
            out_ref[2, :, sl] = dv.astype(BF16)
            carry_ref[:, sl] = _dot(ds2, k) * scale

    blk = (BAND, ATT_WIDTH)
    sblk = (BAND, 128)

    def nxt(j):
        return jnp.minimum(j + 1, nb - 1)

    out = pl.pallas_call(
        body, name=f"attn_bwd_g{g}", grid=(r, nb),
        in_specs=[
            pl.BlockSpec(blk, lambda rho, j: (j, rho * 3)),
            pl.BlockSpec(blk, lambda rho, j: (nxt(j), rho * 3)),
            pl.BlockSpec(blk, lambda rho, j: (j, rho * 3 + 1)),
            pl.BlockSpec(blk, lambda rho, j: (j, rho * 3 + 2)),
            pl.BlockSpec(blk, lambda rho, j: (j, rho)),
            pl.BlockSpec(blk, lambda rho, j: (nxt(j), rho)),
            pl.BlockSpec(sblk, lambda rho, j: (j, rho)),
            pl.BlockSpec(sblk, lambda rho, j: (nxt(j), rho)),
            pl.BlockSpec(sblk, lambda rho, j: (j, rho)),
            pl.BlockSpec(sblk, lambda rho, j: (nxt(j), rho)),
        ],
        out_specs=pl.BlockSpec((3, BAND, ATT_WIDTH), lambda rho, j: (0, j, rho)),
        out_shape=jax.ShapeDtypeStruct((3, L, r * ATT_WIDTH), BF16),
        scratch_shapes=[pltpu.VMEM((BAND, ATT_WIDTH), F32)],
        compiler_params=_params(("parallel", "arbitrary")),
    )(qv, qv, qv, qv, dov, dov, lsev, lsev, dvv, dvv)
    return out.reshape(3, S, ATT_WIDTH)


LRU_T = 256


def _shift_down(x, s, rows):
    return pltpu.roll(x, s, 0)


def _gate_matmuls(ucb, w_ref, bias):
    parts = [_dot(ucb[:, j * 128:(j + 1) * 128], w_ref[j]) for j in range(8)]
    return jnp.concatenate(parts, axis=1) + bias


def _conv_fwd(u, u_prev8, cw_ref):
    T = u.shape[0]
    ue = jnp.concatenate([u_prev8, u], axis=0)
    uc = cw_ref[4:5, :] + cw_ref[0:1, :] * u
    for j in range(1, CONV_WIDTH):
        uc = uc + cw_ref[j:j + 1, :] * pltpu.roll(ue, j, 0)[8:8 + T]
    return uc


def _lru_fwd(rest, cw, wr2, wi2):
    S = rest.shape[0]
    T = LRU_T

    def body(u_ref, cw_ref, wr_ref, wi_ref, h_ref, a_ref, ucar_ref, hcar_ref):
        c = pl.program_id(0)

        @pl.when(c == 0)
        def _():
            ucar_ref[...] = jnp.zeros_like(ucar_ref)
            hcar_ref[...] = jnp.zeros_like(hcar_ref)

        u = u_ref[...]
        uc = _conv_fwd(u, ucar_ref[...], cw_ref)
        ucar_ref[...] = u[T - 8:, :]
        ucb = uc.astype(BF16)
        r = jax.nn.sigmoid(_gate_matmuls(ucb, wr_ref, cw_ref[5:6, :]))
        ig = jax.nn.sigmoid(_gate_matmuls(ucb, wi_ref, cw_ref[6:7, :]))
        log_a = -LRU_C * r * _softplus_neg(cw_ref[7:8, :])
        a = jnp.exp(log_a)
        b = jnp.sqrt(_neg_expm1(2.0 * log_a)) * (ig * uc)
        a_ref[...] = a
        row = lax.broadcasted_iota(jnp.int32, (T, D_MODEL), 0)
        acc_a, acc_b = a, b
        s = 1
        while s < T:
            keep = row >= s
            b_sh = jnp.where(keep, pltpu.roll(acc_b, s, 0), 0.0)
            a_sh = jnp.where(keep, pltpu.roll(acc_a, s, 0), 1.0)
            acc_b = acc_a * b_sh + acc_b
            acc_a = acc_a * a_sh
            s *= 2
        h = acc_b + acc_a * hcar_ref[0:1, :]
        h_ref[...] = h
        hcar_ref[...] = jnp.broadcast_to(h[T - 1:T, :], (8, D_MODEL))

    row_spec = pl.BlockSpec((T, D_MODEL), lambda c: (c, 0))
    return pl.pallas_call(
        body, name="lru_fwd", grid=(S // T,),
        in_specs=[row_spec, pl.BlockSpec((8, D_MODEL), lambda c: (0, 0)),
                  pl.BlockSpec((8, 128, 128), lambda c: (0, 0, 0)),
                  pl.BlockSpec((8, 128, 128), lambda c: (0, 0, 0))],
        out_specs=(row_spec, row_spec),
        out_shape=(jax.ShapeDtypeStruct((S, D_MODEL), F32), jax.ShapeDtypeStruct((S, D_MODEL), F32)),
        scratch_shapes=[pltpu.VMEM((8, D_MODEL), F32), pltpu.VMEM((8, D_MODEL), F32)],
        compiler_params=_params(("arbitrary",)))(rest, cw, wr2, wi2)


def _lru_bwd(dhl, hl, a_all, rest, cw, wr2, wi2, dproj):
    S = rest.shape[0]
    T = LRU_T
    nc = S // T

    def body(dh_ref, h_ref, hp_ref, a_ref, u_ref, up_ref, cw_ref, wr_ref, wi_ref, _alias,
             du_ref, acc_ref, gwr_ref, gwi_ref, gcar_ref, acar_ref, dcar_ref):
        step = pl.program_id(0)
        c = nc - 1 - step

        @pl.when(step == 0)
        def _():
            acc_ref[...] = jnp.zeros_like(acc_ref)
            gwr_ref[...] = jnp.zeros_like(gwr_ref)
            gwi_ref[...] = jnp.zeros_like(gwi_ref)
            gcar_ref[...] = jnp.zeros_like(gcar_ref)
            acar_ref[...] = jnp.zeros_like(acar_ref)
            dcar_ref[...] = jnp.zeros_like(dcar_ref)

        row = lax.broadcasted_iota(jnp.int32, (T, D_MODEL), 0)
        u = u_ref[...]
        u_prev = jnp.where(c > 0, up_ref[...], 0.0)
        h_prev8 = jnp.where(c > 0, hp_ref[...], 0.0)
        a = a_ref[...]
        h = h_ref[...]
        uc = _conv_fwd(u, u_prev, cw_ref)
        ucb = uc.astype(BF16)
        r = jax.nn.sigmoid(_gate_matmuls(ucb, wr_ref, cw_ref[5:6, :]))
        ig = jax.nn.sigmoid(_gate_matmuls(ucb, wi_ref, cw_ref[6:7, :]))
        sp = _softplus_neg(cw_ref[7:8, :])
        log_a = -LRU_C * r * sp
        mult = jnp.sqrt(_neg_expm1(2.0 * log_a))
        a_next = jnp.where(row < T - 1, pltpu.roll(a, T - 1, 0), acar_ref[0:1, :])
        acc_a, acc_b = a_next, dh_ref[...]
        s = 1
        while s < T:
            keep = row < T - s
            b_sh = jnp.where(keep, pltpu.roll(acc_b, T - s, 0), 0.0)
            a_sh = jnp.where(keep, pltpu.roll(acc_a, T - s, 0), 1.0)
            acc_b = acc_a * b_sh + acc_b
            acc_a = acc_a * a_sh
            s *= 2
        G = acc_b + acc_a * gcar_ref[0:1, :]
        gcar_ref[...] = jnp.broadcast_to(G[0:1, :], (8, D_MODEL))
        acar_ref[...] = jnp.broadcast_to(a[0:1, :], (8, D_MODEL))
        he = jnp.concatenate([h_prev8, h], axis=0)
        h_before = pltpu.roll(he, 1, 0)[8:8 + T]
        d_a = G * h_before
        d_mult = G * (ig * uc)
        d_ig = G * (mult * uc)
        duc = G * (mult * ig)
        d_log_a = d_a * a - d_mult * (a * a) / mult
        d_r = d_log_a * (-LRU_C * sp)
        d_sp = jnp.sum(d_log_a * (-LRU_C * r), axis=0, keepdims=True)
        dpre_r = d_r * r * (1.0 - r)
        dpre_i = d_ig * ig * (1.0 - ig)
        dprb = dpre_r.astype(BF16)
        dpib = dpre_i.astype(BF16)
        back = []
        for j in range(8):
            sl = slice(j * 128, (j + 1) * 128)
            back.append(_dot_nt(dprb[:, sl], wr_ref[j]) + _dot_nt(dpib[:, sl], wi_ref[j]))
            gwr_ref[j] += _dot_tn(ucb[:, sl], dprb[:, sl])
            gwi_ref[j] += _dot_tn(ucb[:, sl], dpib[:, sl])
        duc = duc + jnp.concatenate(back, axis=1)
        de = jnp.concatenate([duc, dcar_ref[...]], axis=0)
        du = cw_ref[0:1, :] * duc
        ue = jnp.concatenate([u_prev, u], axis=0)
        acc_ref[0:1, :] += jnp.sum(duc * u, axis=0, keepdims=True)
        for j in range(1, CONV_WIDTH):
            du = du + cw_ref[j:j + 1, :] * pltpu.roll(de, T + 8 - j, 0)[0:T]
            acc_ref[j:j + 1, :] += jnp.sum(duc * pltpu.roll(ue, j, 0)[8:8 + T], axis=0, keepdims=True)
        dcar_ref[...] = duc[0:8, :]
        acc_ref[4:5, :] += jnp.sum(duc, axis=0, keepdims=True)
        acc_ref[5:6, :] += jnp.sum(dpre_r, axis=0, keepdims=True)
        acc_ref[6:7, :] += jnp.sum(dpre_i, axis=0, keepdims=True)
        acc_ref[7:8, :] += d_sp
        du_ref[0] = du[:, 0:ATT_WIDTH].astype(BF16)
        du_ref[1] = du[:, ATT_WIDTH:].astype(BF16)

    def rev(step):
        return nc - 1 - step

    def prev8(step):
        return jnp.maximum(rev(step) * (T // 8) - 1, 0)

    row_spec = pl.BlockSpec((T, D_MODEL), lambda s: (rev(s), 0))
    prev_spec = pl.BlockSpec((8, D_MODEL), lambda s: (prev8(s), 0))
    vec_spec = pl.BlockSpec((8, D_MODEL), lambda s: (0, 0))
    w_spec = pl.BlockSpec((8, 128, 128), lambda s: (0, 0, 0))
    outs = pl.pallas_call(
        body, name="lru_bwd", grid=(nc,),
        in_specs=[row_spec, row_spec, prev_spec, row_spec, row_spec, prev_spec, vec_spec, w_spec, w_spec,
                  pl.BlockSpec(memory_space=pl.ANY)],
        out_specs=(pl.BlockSpec((2, T, ATT_WIDTH), lambda s: (0, rev(s), 0)), vec_spec, w_spec, w_spec),
        out_shape=(jax.ShapeDtypeStruct(dproj.shape, BF16), jax.ShapeDtypeStruct((8, D_MODEL), F32),
                   jax.ShapeDtypeStruct((8, 128, 128), F32), jax.ShapeDtypeStruct((8, 128, 128), F32)),
        scratch_shapes=[pltpu.VMEM((8, D_MODEL), F32), pltpu.VMEM((8, D_MODEL), F32),
                        pltpu.VMEM((8, D_MODEL), F32)],
        input_output_aliases={9: 0},
        compiler_params=_params(("arbitrary",)),
    )(dhl, hl, hl, a_all, rest, rest, cw, wr2, wi2, dproj)
    return outs


OUT_TM = 256


def _out_fwd(a_gated, hl, rest, x, vec, w_pa, w_pb, w_o):
    S = x.shape[0]
    tm = OUT_TM

    def body(ag_ref, hl_ref, gl_ref, ma_ref, mb_ref, x_ref, v_ref, wpa_ref, wpb_ref, wo_ref,
             xn_ref, mix_ref, ya_ref, yb_ref, bg_ref):
        bg = (hl_ref[...] * _silu(gl_ref[...])).astype(BF16)
        ya = _dot(ag_ref[...], wpa_ref[...])
        yb = _dot(bg, wpb_ref[...])
        mix = (jax.nn.sigmoid(ma_ref[...]) * ya + jax.nn.sigmoid(mb_ref[...]) * yb).astype(BF16)
        out = _dot(mix, wo_ref[...])
        rstd = lax.rsqrt(jnp.mean(out * out, axis=-1, keepdims=True) + NORM_EPS)
        xn_ref[...] = x_ref[...] + v_ref[1:2, :] * ((out * rstd) * v_ref[0:1, :])
        mix_ref[...] = mix
        ya_ref[...] = ya.astype(BF16)
        yb_ref[...] = yb.astype(BF16)
        bg_ref[...] = bg

    def col(j):
        return pl.BlockSpec((tm, D_MODEL), lambda i: (i, j))

    def whole(shape):
        return pl.BlockSpec(shape, lambda i: (0, 0))

    row = col(0)
    bf = jax.ShapeDtypeStruct((S, D_MODEL), BF16)
    return pl.pallas_call(
        body, name="out_fwd", grid=(S // tm,),
        in_specs=[pl.BlockSpec((tm, ATT_WIDTH), lambda i: (i, 0)), row, col(1), col(2), col(3), row,
                  whole((8, D_MODEL)), whole((ATT_WIDTH, D_MODEL)), whole((D_MODEL, D_MODEL)),
                  whole((D_MODEL, D_MODEL))],
        out_specs=(row, row, row, row, row),
        out_shape=(jax.ShapeDtypeStruct((S, D_MODEL), F32), bf, bf, bf, bf),
        compiler_params=_params(("parallel",)))(a_gated, hl, rest, rest, rest, x, vec, w_pa, w_pb, w_o)


def _out_bwd(dxn, mix, ya, yb, hl, rest, o_att, vec, w_pa, w_pb, w_o):
    S = dxn.shape[0]
    tm = OUT_TM

    def body(dxn_ref, mix_ref, ya_ref, yb_ref, hl_ref, gl_ref, ma_ref, mb_ref, ga_ref, o_ref, v_ref,
             wpa_ref, wpb_ref, wo_ref,
             dout_ref, dya_ref, dyb_ref, do_ref, dv_ref, dhl_ref, dp_ref, acc_ref):
        i = pl.program_id(0)

        @pl.when(i == 0)
        def _():
            acc_ref[...] = jnp.zeros_like(acc_ref)

        g_post = v_ref[0:1, :]
        gate = v_ref[1:2, :]
        dxn_v = dxn_ref[...]
        out = _dot(mix_ref[...], wo_ref[...])
        rstd = lax.rsqrt(jnp.mean(out * out, axis=-1, keepdims=True) + NORM_EPS)
        nrm = out * rstd
        acc_ref[0:1, :] += jnp.sum(dxn_v * (nrm * g_post), axis=0, keepdims=True)
        acc_ref[1:2, :] += jnp.sum(dxn_v * gate * nrm, axis=0, keepdims=True)
        dn = dxn_v * (gate * g_post)
        dout = (rstd * (dn - nrm * jnp.mean(dn * nrm, axis=-1, keepdims=True))).astype(BF16)
        dout_ref[...] = dout
        dmix = _dot_nt(dout, wo_ref[...])
        sa = jax.nn.sigmoid(ma_ref[...])
        sb = jax.nn.sigmoid(mb_ref[...])
        dya = (dmix * sa).astype(BF16)
        dyb = (dmix * sb).astype(BF16)
        dya_ref[...] = dya
        dyb_ref[...] = dyb
        dma = dmix * ya_ref[...].astype(F32) * (sa * (1.0 - sa))
        dmb = dmix * yb_ref[...].astype(F32) * (sb * (1.0 - sb))
        d_ag = _dot_nt(dya, wpa_ref[...])
        d_bg = _dot_nt(dyb, wpb_ref[...])
        gl = gl_ref[...]
        hl_v = hl_ref[...]
        dhl_ref[...] = d_bg * _silu(gl)
        dgl = d_bg * hl_v * _dsilu(gl)
        ga = ga_ref[...]
        o = o_ref[...]
        do = d_ag * _silu(ga)
        dga = d_ag * o * _dsilu(ga)
        do_ref[...] = do.astype(BF16)
        lane = lax.broadcasted_iota(jnp.int32, (tm, 128), 1)
        dvec = jnp.zeros((tm, 128), F32)
        for h in range(HEADS):
            sl = slice(h * HEAD_DIM, (h + 1) * HEAD_DIM)
            dvec = jnp.where(lane == h, jnp.sum(do[:, sl] * o[:, sl], axis=1, keepdims=True), dvec)
        dv_ref[...] = dvec
        dp_ref[0] = dgl[:, 0:ATT_WIDTH].astype(BF16)
        dp_ref[1] = dgl[:, ATT_WIDTH:].astype(BF16)
        dp_ref[2] = dma[:, 0:ATT_WIDTH].astype(BF16)
        dp_ref[3] = dma[:, ATT_WIDTH:].astype(BF16)
        dp_ref[4] = dmb[:, 0:ATT_WIDTH].astype(BF16)
        dp_ref[5] = dmb[:, ATT_WIDTH:].astype(BF16)
        dp_ref[6] = dga.astype(BF16)

    def col(j):
        return pl.BlockSpec((tm, D_MODEL), lambda i: (i, j))

    def whole(shape):
        return pl.BlockSpec(shape, lambda i: (0, 0))

    row = col(0)
    att = pl.BlockSpec((tm, ATT_WIDTH), lambda i: (i, 0))
    bf = jax.ShapeDtypeStruct((S, D_MODEL), BF16)
    return pl.pallas_call(
        body, name="out_bwd", grid=(S // tm,),
        in_specs=[row, row, row, row, row, col(1), col(2), col(3),
                  pl.BlockSpec((tm, ATT_WIDTH), lambda i: (i, 8)), att,
                  whole((8, D_MODEL)), whole((ATT_WIDTH, D_MODEL)), whole((D_MODEL, D_MODEL)),
                  whole((D_MODEL, D_MODEL))],
        out_specs=(row, row, row, att, pl.BlockSpec((tm, 128), lambda i: (i, 0)), row,
                   pl.BlockSpec((7, tm, ATT_WIDTH), lambda i: (1, i, 0)), whole((8, D_MODEL))),
        out_shape=(bf, bf, bf, jax.ShapeDtypeStruct((S, ATT_WIDTH), BF16),
                   jax.ShapeDtypeStruct((S, 128), F32), jax.ShapeDtypeStruct((S, D_MODEL), F32),
                   jax.ShapeDtypeStruct((DREST_CHUNKS, S, ATT_WIDTH), BF16),
                   jax.ShapeDtypeStruct((8, D_MODEL), F32)),
        compiler_params=_params(("arbitrary",)),
    )(dxn, mix, ya, yb, hl, rest, rest, rest, rest, o_att, vec, w_pa, w_pb, w_o)


def _loss_head(y, target):
    S = y.shape[0]
    tm = 512

    def body(y_ref, t_ref, dy_ref, acc_ref):
        i = pl.program_id(0)

        @pl.when(i == 0)
        def _():
            acc_ref[...] = jnp.zeros_like(acc_ref)

        err = y_ref[...] - t_ref[...]
        dy_ref[...] = err * (1.0 / D_MODEL)
        part = jnp.sum(jnp.sum(err * err, axis=1, keepdims=True), axis=0, keepdims=True)
        acc_ref[...] += jnp.broadcast_to(part, acc_ref.shape)

    row = pl.BlockSpec((tm, D_MODEL), lambda i: (i, 0))
    return pl.pallas_call(
        body, name="loss_head", grid=(S // tm,),
        in_specs=[row, row],
        out_specs=(row, pl.BlockSpec((8, 128), lambda i: (0, 0))),
        out_shape=(jax.ShapeDtypeStruct((S, D_MODEL), F32), jax.ShapeDtypeStruct((8, 128), F32)),
        compiler_params=_params(("arbitrary",)))(y, target)


def _adamw(gsrc, w, m, v, n_stack, name, rows=None):
    R, C = w.shape
    budget = 96 * 1024
    tr = rows
    if tr is None:
        tr = R
        while tr * C > budget and tr % 16 == 0:
            tr //= 2
    assert R % tr == 0 and (tr % 8 == 0 or tr == R)
    c1 = 1.0 - ADAM_B1 ** ADAM_STEP
    c2 = 1.0 - ADAM_B2 ** ADAM_STEP

    def body(g_ref, w_ref, m_ref, v_ref, go_ref, d_ref, mo_ref, vo_ref):
        if n_stack:
            g = g_ref[0].astype(F32)
            for s in range(1, n_stack):
                g = g + g_ref[s].astype(F32)
        else:
            g = g_ref[...]
        m_new = ADAM_B1 * m_ref[...] + (1.0 - ADAM_B1) * g
        v_new = ADAM_B2 * v_ref[...] + (1.0 - ADAM_B2) * (g * g)
        m_hat = m_new / c1
        v_hat = v_new / c2
        go_ref[...] = g
        d_ref[...] = -ADAM_LR * (m_hat / (jnp.sqrt(v_hat) + ADAM_EPS) + ADAM_WD * w_ref[...])
        mo_ref[...] = m_new
        vo_ref[...] = v_new

    blk = pl.BlockSpec((tr, C), lambda i: (i, 0))
    g_spec = pl.BlockSpec((n_stack, tr, C), lambda i: (0, i, 0)) if n_stack else blk
    shp = jax.ShapeDtypeStruct((R, C), F32)
    return pl.pallas_call(
        body, name=name, grid=(R // tr,),
        in_specs=[g_spec, blk, blk, blk], out_specs=(blk, blk, blk, blk),
        out_shape=(shp, shp, shp, shp),
        compiler_params=_params(("parallel",)))(gsrc, w, m, v)


def _pair_blocks(w):
    w = w.reshape(8, 2, 64, 64)
    z = jnp.zeros((8, 64, 64), w.dtype)
    top = jnp.concatenate([w[:, 0], z], axis=2)
    bot = jnp.concatenate([z, w[:, 1]], axis=2)
    return jnp.concatenate([top, bot], axis=1).astype(BF16)


def _unpair_blocks(g):
    return jnp.stack([g[:, :64, :64], g[:, 64:, 64:]], axis=1).reshape(16, 64, 64)


def _layer_fwd(x, mod, p):
    zeros = jnp.zeros((5, D_MODEL), F32)
    vec_pre = jnp.concatenate([p["g_pre"][None], mod[1:2], mod[0:1], zeros], axis=0)
    h = _norm_mod_fwd(x, vec_pre)
    rest = _proj_fwd(h, p["w_int"], QKV_CHUNKS, QKV_CHUNKS, F32, "proj_fwd_rest")
    S = x.shape[0]
    qkv, outs, stats = [], [], []
    for g, r in enumerate(DILATIONS):
        qkv.append(_proj_fwd(h, p["w_int"], 3 * g, 3, BF16, f"proj_fwd_qkv{g}"))
        o_g, st_g = _attn_fwd(qkv[g], g, r)
        outs.append(o_g.reshape(S, ATT_WIDTH))
        stats.append(st_g.reshape(S, 128))
    o_att, a_gated, lse = _attn_combine(outs, stats, rest)
    hl, a_dec = _lru_fwd(rest, p["cw"], p["wr2"], p["wi2"])
    vec_post = jnp.concatenate([p["g_post"][None], mod[2:3], jnp.zeros((6, D_MODEL), F32)], axis=0)
    x_new, mix, ya, yb, b_gated = _out_fwd(a_gated, hl, rest, x, vec_post, p["w_pa"], p["w_pb"], p["w_o"])
    saved = dict(x=x, h=h, qkv=qkv, rest=rest, o_att=o_att, a_gated=a_gated, lse=lse, hl=hl, a_dec=a_dec,
                 mix=mix, ya=ya, yb=yb, b_gated=b_gated, vec_pre=vec_pre, vec_post=vec_post)
    return x_new, saved


def _layer_bwd(dxn, sv, p):
    dout, dya, dyb, do, dvec, dhl, drest, acc_out = _out_bwd(
        dxn, sv["mix"], sv["ya"], sv["yb"], sv["hl"], sv["rest"], sv["o_att"], sv["vec_post"],
        p["w_pa"], p["w_pb"], p["w_o"])
    drest, acc_lru, gwr2, gwi2 = _lru_bwd(dhl, sv["hl"], sv["a_dec"], sv["rest"], p["cw"], p["wr2"],
                                          p["wi2"], drest)
    dqkv = [_attn_bwd(sv["qkv"][g], do, sv["lse"], dvec, g, r) for g, r in enumerate(DILATIONS)]
    dh = _proj_dgrad(dqkv, drest, p["w_int"])
    dx, acc_pre = _norm_mod_bwd(dh, sv["x"], dxn, sv["vec_pre"])
    grads = dict(
        w_in=_proj_wgrad(sv["h"], dqkv, drest),
        w_o=_wgrad_tn(sv["mix"], dout, "wgrad_o"),
        w_pa=_wgrad_tn(sv["a_gated"], dya, "wgrad_pa"),
        w_pb=_wgrad_tn(sv["b_gated"], dyb, "wgrad_pb"),
        dmod=jnp.concatenate([acc_pre[0], acc_pre[1], acc_out[0]]),
        g_pre=acc_pre[2], g_post=acc_out[1],
        conv_w=acc_lru[0:4], conv_b=acc_lru[4], b_rg=acc_lru[5], b_ig=acc_lru[6],
        lru_lambda=acc_lru[7] * (-jax.nn.sigmoid(-p["lru_lambda"])),
        w_rg=_unpair_blocks(gwr2), w_ig=_unpair_blocks(gwi2))
    return dx, grads


SMALL_NAMES = ("b_mod", "g_pre", "conv_b", "w_rg", "b_rg", "w_ig", "b_ig", "lru_lambda", "g_post")


def kernel(x, c, w_mod, b_mod, g_pre, w_in, conv_w, conv_b, w_rg, b_rg, w_ig, b_ig, lru_lambda, w_pa, w_pb, w_o, g_post, loss_target, m_w_mod, m_b_mod, m_g_pre, m_w_in, m_conv_w, m_conv_b, m_w_rg, m_b_rg, m_w_ig, m_b_ig, m_lru_lambda, m_w_pa, m_w_pb, m_w_o, m_g_post, v_w_mod, v_b_mod, v_g_pre, v_w_in, v_conv_w, v_conv_b, v_w_rg, v_b_rg, v_w_ig, v_b_ig, v_lru_lambda, v_w_pa, v_w_pb, v_w_o, v_g_post):
    W = dict(w_mod=w_mod, b_mod=b_mod, g_pre=g_pre, w_in=w_in, conv_w=conv_w, conv_b=conv_b, w_rg=w_rg,
             b_rg=b_rg, w_ig=w_ig, b_ig=b_ig, lru_lambda=lru_lambda, w_pa=w_pa, w_pb=w_pb, w_o=w_o,
             g_post=g_post)
    M = dict(w_mod=m_w_mod, b_mod=m_b_mod, g_pre=m_g_pre, w_in=m_w_in, conv_w=m_conv_w, conv_b=m_conv_b,
             w_rg=m_w_rg, b_rg=m_b_rg, w_ig=m_w_ig, b_ig=m_b_ig, lru_lambda=m_lru_lambda, w_pa=m_w_pa,
             w_pb=m_w_pb, w_o=m_w_o, g_post=m_g_post)
    V = dict(w_mod=v_w_mod, b_mod=v_b_mod, g_pre=v_g_pre, w_in=v_w_in, conv_w=v_conv_w, conv_b=v_conv_b,
             w_rg=v_w_rg, b_rg=v_b_rg, w_ig=v_w_ig, b_ig=v_b_ig, lru_lambda=v_lru_lambda, w_pa=v_w_pa,
             w_pb=v_w_pb, w_o=v_w_o, g_post=v_g_post)
    S = x.shape[1]
    me = 4 * lax.axis_index("x") + 2 * lax.axis_index("y") + lax.axis_index("c")
    n_mod = w_mod.shape[2]
    n_in = w_in.shape[2]
    n_conv = conv_w.shape[2]

    c_rows = jnp.broadcast_to(c, (8, D_MODEL))
    g_c, g_win, g_wpa, g_wpb, g_wo, g_cw = _exchange(
        [c_rows, w_in.astype(BF16), w_pa.astype(BF16), w_pb.astype(BF16), w_o.astype(BF16), conv_w],
        ["ag"] * 6, "gather_weights")
    c_all = g_c[:, 0, :]
    c_pad = jnp.concatenate([c_all, jnp.zeros((8, D_MODEL), F32)], axis=0)
    conv_w_full = jnp.transpose(g_cw, (1, 2, 0, 3)).reshape(2, CONV_WIDTH, D_MODEL)

    mod_cols = _mod_fwd(c_pad, w_mod)
    mod_src = jnp.transpose(mod_cols[:, :8], (1, 0, 2))
    mod_src = jnp.concatenate([mod_src, jnp.zeros((8, 6, n_mod), F32)], axis=1)
    (mod_got,) = _exchange([mod_src], ["a2a"], "scatter_mod")
    mod = jnp.transpose(mod_got[:, :2], (1, 0, 2)).reshape(2, 3 * D_MODEL) + b_mod

    layers = []
    for layer in range(2):
        w_full = jnp.transpose(g_win[:, layer], (1, 0, 2)).reshape(D_MODEL, N_CHUNKS, ATT_WIDTH)
        w_int = w_full[:, jnp.array(CHUNK_PERM)].reshape(D_MODEL, N_CHUNKS * ATT_WIDTH)
        cw = jnp.concatenate([conv_w_full[layer], conv_b[layer][None], b_rg[layer][None], b_ig[layer][None],
                              lru_lambda[layer][None]], axis=0)
        layers.append(dict(
            w_int=w_int,
            w_pa=jnp.transpose(g_wpa[:, layer], (1, 0, 2)).reshape(ATT_WIDTH, D_MODEL),
            w_pb=g_wpb[:, layer].reshape(D_MODEL, D_MODEL),
            w_o=g_wo[:, layer].reshape(D_MODEL, D_MODEL),
            cw=cw, wr2=_pair_blocks(w_rg[layer]), wi2=_pair_blocks(w_ig[layer]),
            g_pre=g_pre[layer], g_post=g_post[layer], lru_lambda=lru_lambda[layer]))

    act = x[0]
    saved = []
    for layer in range(2):
        act, sv = _layer_fwd(act, mod[layer].reshape(3, D_MODEL), layers[layer])
        saved.append(sv)
    dy, loss_acc = _loss_head(act, loss_target[0])
    loss = lax.psum(0.5 * loss_acc[0, 0] / D_MODEL, MESH_AXES)
    grads = [None, None]
    for layer in (1, 0):
        dy, grads[layer] = _layer_bwd(dy, saved[layer], layers[layer])
    grad_x = dy[None]

    def stack2(name):
        return jnp.stack([grads[0][name], grads[1][name]], axis=0)

    gw_in = stack2("w_in").astype(BF16).reshape(2, D_MODEL, N_DEV, n_in)
    gw_in = jnp.transpose(gw_in, (2, 0, 1, 3))
    gw_pa = jnp.transpose(stack2("w_pa").astype(BF16).reshape(2, ATT_WIDTH, N_DEV, 128), (2, 0, 1, 3))
    gw_pb = jnp.transpose(stack2("w_pb").astype(BF16).reshape(2, N_DEV, 128, D_MODEL), (1, 0, 2, 3))
    gw_o = jnp.transpose(stack2("w_o").astype(BF16).reshape(2, N_DEV, 128, D_MODEL), (1, 0, 2, 3))
    dmod = stack2("dmod")
    dmod_src = jnp.transpose(dmod.reshape(2, N_DEV, n_mod), (1, 0, 2))
    dmod_src = jnp.concatenate([dmod_src, jnp.zeros((8, 6, n_mod), F32)], axis=1)
    small = {"b_mod": dmod}
    for name in SMALL_NAMES[1:]:
        small[name] = stack2(name)
    small_vec = jnp.concatenate([small[n].reshape(-1) for n in SMALL_NAMES] + [stack2("conv_w").reshape(-1)])
    n_small = small_vec.shape[0]
    small_vec = small_vec.reshape(n_small // 128, 128)
    r_in, r_pa, r_pb, r_o, r_dmod, r_small = _exchange(
        [gw_in, gw_pa, gw_pb, gw_o, dmod_src, small_vec], ["a2a", "a2a", "a2a", "a2a", "a2a", "ag"],
        "exchange_grads")

    res = {}
    res["w_in"] = _adamw(r_in.reshape(8, 2 * D_MODEL, n_in), w_in.reshape(2 * D_MODEL, n_in),
                         m_w_in.reshape(2 * D_MODEL, n_in), v_w_in.reshape(2 * D_MODEL, n_in), 8, "adamw_w_in")
    res["w_pa"] = _adamw(r_pa.reshape(8, 2 * ATT_WIDTH, 128), w_pa.reshape(2 * ATT_WIDTH, 128),
                         m_w_pa.reshape(2 * ATT_WIDTH, 128), v_w_pa.reshape(2 * ATT_WIDTH, 128), 8, "adamw_w_pa")
    res["w_pb"] = _adamw(r_pb.reshape(8, 256, D_MODEL), w_pb.reshape(256, D_MODEL),
                         m_w_pb.reshape(256, D_MODEL), v_w_pb.reshape(256, D_MODEL), 8, "adamw_w_pb")
    res["w_o"] = _adamw(r_o.reshape(8, 256, D_MODEL), w_o.reshape(256, D_MODEL),
                        m_w_o.reshape(256, D_MODEL), v_w_o.reshape(256, D_MODEL), 8, "adamw_w_o")
    dmod_all = jnp.transpose(r_dmod[:, :2], (1, 0, 2))
    gw_mod = _mod_wgrad(jnp.transpose(c_all), dmod_all)
    res["w_mod"] = _adamw(gw_mod.reshape(2 * D_MODEL, n_mod), w_mod.reshape(2 * D_MODEL, n_mod),
                          m_w_mod.reshape(2 * D_MODEL, n_mod), v_w_mod.reshape(2 * D_MODEL, n_mod), 0,
                          "adamw_w_mod")
    n_rep = sum(W[n].size for n in SMALL_NAMES)
    rep_rows = n_rep // 128

    def flat_rep(src):
        return jnp.concatenate([src[n].reshape(-1) for n in SMALL_NAMES]).reshape(rep_rows, 128)

    rep = _adamw(r_small, flat_rep(W), flat_rep(M), flat_rep(V), 8, "adamw_small", rows=rep_rows // 2)
    off = 0
    for name in SMALL_NAMES:
        size = W[name].size
        res[name] = tuple(t.reshape(-1)[off:off + size].reshape(W[name].shape) for t in rep)
        off += size
    conv_stack = r_small[:, rep_rows:].reshape(8, 2, CONV_WIDTH, D_MODEL)
    conv_stack = lax.dynamic_slice_in_dim(conv_stack, me * n_conv, n_conv, axis=3).reshape(8, 8, n_conv)
    res["conv_w"] = _adamw(conv_stack, conv_w.reshape(8, n_conv), m_conv_w.reshape(8, n_conv),
                           v_conv_w.reshape(8, n_conv), 8, "adamw_conv_w")

    names = ("w_mod", "b_mod", "g_pre", "w_in", "conv_w", "conv_b", "w_rg", "b_rg", "w_ig", "b_ig",
             "lru_lambda", "w_pa", "w_pb", "w_o", "g_post")
    outs = [loss, grad_x]
    for k in range(4):
        outs.extend(res[n][k].reshape(W[n].shape) for n in names)
    return tuple(outs)
```

```python
import functools

import jax
import jax.numpy as jnp
from jax import lax
from jax.experimental import pallas as pl
from jax.experimental.pallas import tpu as pltpu

F32 = jnp.float32
BF16 = jnp.bfloat16

N_DEV = 8
D_MODEL = 1024
HEAD_DIM = 128
HEADS = 4
ATT_WIDTH = HEADS * HEAD_DIM
DILATIONS = (1, 4, 16)
BAND = 128
N_CHUNKS = 18
QKV_CHUNKS = 9
CONV_WIDTH = 4
LRU_C = 8.0
NORM_EPS = 1e-6
NEG_INF = -1e30
ADAM_LR = 0.001
ADAM_B1 = 0.9
ADAM_B2 = 0.999
ADAM_EPS = 1e-08
ADAM_WD = 0.01
ADAM_STEP = 10
CHUNK_PERM = (0, 3, 6, 1, 4, 7, 2, 5, 8, 10, 11, 12, 13, 14, 15, 16, 17, 9)
DREST_CHUNKS = 10
VMEM_LIMIT = 56 * 1024 * 1024
MESH_AXES = ("x", "y", "c")


def _params(sem=None):
    return pltpu.CompilerParams(dimension_semantics=sem, vmem_limit_bytes=VMEM_LIMIT)


def _silu(x):
    return x * jax.nn.sigmoid(x)


def _dsilu(x):
    s = jax.nn.sigmoid(x)
    return s * (1.0 + x * (1.0 - s))


def _neg_expm1(x):
    series = -x * (1.0 + x * (0.5 + x * (1.0 / 6.0 + x * (1.0 / 24.0 + x * (1.0 / 120.0)))))
    return jnp.where(x > -0.05, series, 1.0 - jnp.exp(x))


def _softplus_neg(lam):
    z = jnp.exp(-jnp.abs(lam))
    small = z * (1.0 - z * (0.5 - z * (1.0 / 3.0 - z * 0.25)))
    log1p_z = jnp.where(z < 1e-2, small, jnp.log(1.0 + z))
    return jnp.maximum(-lam, 0.0) + log1p_z


def _dot(a, b):
    return jnp.dot(a, b, preferred_element_type=F32)


def _dot_nt(a, b):
    return lax.dot_general(a, b, (((1,), (1,)), ((), ())), preferred_element_type=F32)


def _dot_tn(a, b):
    return lax.dot_general(a, b, (((0,), (0,)), ((), ())), preferred_element_type=F32)


def _exchange(arrs, modes, name):
    n = len(arrs)
    out_shapes = []
    for a, mode in zip(arrs, modes):
        shape = ((N_DEV,) + a.shape) if mode == "ag" else a.shape
        out_shapes.append(jax.ShapeDtypeStruct(shape, a.dtype))

    def body(*refs):
        ins, outs = refs[:n], refs[n:2 * n]
        send_sems, recv_sems, local_sems = refs[2 * n:]
        x, y, c = lax.axis_index("x"), lax.axis_index("y"), lax.axis_index("c")
        me = 4 * x + 2 * y + c

        def src_for(a, dev):
            return ins[a] if modes[a] == "ag" else ins[a].at[dev]

        local = []
        for a in range(n):
            cp = pltpu.make_async_copy(src_for(a, me), outs[a].at[me], local_sems.at[a])
            cp.start()
            local.append(cp)
        started, arrivals = [], []
        for k in range(1, N_DEV):
            px = 1 - x if (k >> 2) & 1 else x
            py = 1 - y if (k >> 1) & 1 else y
            pc = 1 - c if k & 1 else c
            peer = 4 * px + 2 * py + pc
            for a in range(n):
                s = (k - 1) * n + a
                cp = pltpu.make_async_remote_copy(
                    src_ref=src_for(a, peer), dst_ref=outs[a].at[me],
                    send_sem=send_sems.at[s], recv_sem=recv_sems.at[s],
                    device_id=(px, py, pc), device_id_type=pl.DeviceIdType.MESH)
                cp.start()
                started.append(cp)
                arrivals.append(pltpu.make_async_remote_copy(
                    src_ref=src_for(a, peer), dst_ref=outs[a].at[peer],
                    send_sem=send_sems.at[s], recv_sem=recv_sems.at[s],
                    device_id=(px, py, pc), device_id_type=pl.DeviceIdType.MESH))
        for cp in arrivals:
            cp.wait_recv()
        for cp in started:
            cp.wait_send()
        for cp in local:
            cp.wait()

    any_spec = pl.BlockSpec(memory_space=pl.ANY)
    outs = pl.pallas_call(
        body, name=name,
        out_shape=tuple(out_shapes),
        in_specs=[any_spec] * n, out_specs=tuple([any_spec] * n),
        scratch_shapes=[pltpu.SemaphoreType.DMA((7 * n,)), pltpu.SemaphoreType.DMA((7 * n,)),
                        pltpu.SemaphoreType.DMA((n,))],
    )(*arrs)
    return list(outs)


def _mod_fwd(c_pad, w_mod):
    def body(c_ref, w_ref, o_ref):
        sc = _silu(c_ref[...]).astype(BF16)
        for layer in range(2):
            o_ref[layer] = _dot(sc, w_ref[layer].astype(BF16))

    return pl.pallas_call(
        body, name="mod_fwd", out_shape=jax.ShapeDtypeStruct((2, 16, w_mod.shape[2]), F32),
        compiler_params=_params())(c_pad, w_mod)


def _mod_wgrad(c_t, dmod):
    n_cols = dmod.shape[2]

    def body(c_ref, d_ref, o_ref):
        sc = _silu(c_ref[...]).astype(BF16).astype(F32)
        for layer in range(2):
            dm = d_ref[layer].astype(BF16).astype(F32)
            acc = sc[:, 0:1] * dm[0:1, :]
            for b in range(1, N_DEV):
                acc = acc + sc[:, b:b + 1] * dm[b:b + 1, :]
            o_ref[layer] = acc

    return pl.pallas_call(
        body, name="mod_wgrad", out_shape=jax.ShapeDtypeStruct((2, D_MODEL, n_cols), F32),
        compiler_params=_params())(c_t, dmod)


GROUP_COLS = 3 * ATT_WIDTH
PROJ_TM = 256


def _norm_proj_fwd(x, vec, w_int):
    S = x.shape[0]
    tm = PROJ_TM

    def body(x_ref, v_ref, w_ref, ht_ref, q0_ref, q1_ref, q2_ref, rest_ref):
        xv = x_ref[...]
        rstd = lax.rsqrt(jnp.mean(xv * xv, axis=-1, keepdims=True) + NORM_EPS)
        hf = ((xv * rstd) * v_ref[0:1, :]) * (1.0 + v_ref[1:2, :]) + v_ref[2:3, :]
        ht_ref[...] = hf.T.astype(BF16)
        h = hf.astype(BF16)
        for g, q_ref in enumerate((q0_ref, q1_ref, q2_ref)):
            q_ref[...] = _dot(h, w_ref[:, g * GROUP_COLS:(g + 1) * GROUP_COLS]).astype(BF16)
        rest_ref[...] = _dot(h, w_ref[:, 3 * GROUP_COLS:])

    n_rest = QKV_CHUNKS * ATT_WIDTH
    qkv_shape = jax.ShapeDtypeStruct((S, GROUP_COLS), BF16)
    qkv_spec = pl.BlockSpec((tm, GROUP_COLS), lambda i: (i, 0))
    return pl.pallas_call(
        body, name="norm_proj_fwd", grid=(S // tm,),
        in_specs=[pl.BlockSpec((tm, D_MODEL), lambda i: (i, 0)),
                  pl.BlockSpec((8, D_MODEL), lambda i: (0, 0)),
                  pl.BlockSpec(memory_space=pltpu.VMEM)],
        out_specs=(pl.BlockSpec((D_MODEL, tm), lambda i: (0, i)), qkv_spec, qkv_spec, qkv_spec,
                   pl.BlockSpec((tm, n_rest), lambda i: (i, 0))),
        out_shape=(jax.ShapeDtypeStruct((D_MODEL, S), BF16), qkv_shape, qkv_shape, qkv_shape,
                   jax.ShapeDtypeStruct((S, n_rest), F32)),
        compiler_params=_params(("parallel",)))(x, vec, w_int)


def _proj_dgrad_norm_bwd(dqkv, drest, w_int, x, dxn, vec):
    S = x.shape[0]
    tm = PROJ_TM

    def body(d0_ref, d1_ref, d2_ref, dr_ref, w_ref, x_ref, dxn_ref, v_ref, dx_ref, acc_ref):
        i = pl.program_id(0)

        @pl.when(i == 0)
        def _():
            acc_ref[...] = jnp.zeros_like(acc_ref)

        def wcols(k):
            return w_ref[:, k * ATT_WIDTH:(k + 1) * ATT_WIDTH]

        dhv = None
        for g, d_ref in enumerate((d0_ref, d1_ref, d2_ref)):
            for t in range(3):
                part = _dot_nt(d_ref[t], wcols(3 * g + t))
                dhv = part if dhv is None else dhv + part
        for t in range(7):
            dhv = dhv + _dot_nt(dr_ref[t], wcols(11 + t))
        for t in range(2):
            dhv = dhv + _dot_nt(dr_ref[8 + t], wcols(9 + t))

        xv = x_ref[...]
        g = v_ref[0:1, :]
        sc1 = 1.0 + v_ref[1:2, :]
        rstd = lax.rsqrt(jnp.mean(xv * xv, axis=-1, keepdims=True) + NORM_EPS)
        xhat = xv * rstd
        acc_ref[0:1, :] += jnp.sum(dhv, axis=0, keepdims=True)
        acc_ref[1:2, :] += jnp.sum(dhv * (xhat * g), axis=0, keepdims=True)
        acc_ref[2:3, :] += jnp.sum(dhv * xhat * sc1, axis=0, keepdims=True)
        dxhat = dhv * (g * sc1)
        dx = rstd * (dxhat - xhat * jnp.mean(dxhat * xhat, axis=-1, keepdims=True))
        dx_ref[...] = dx + dxn_ref[...]

    row = pl.BlockSpec((tm, D_MODEL), lambda i: (i, 0))
    vec_spec = pl.BlockSpec((8, D_MODEL), lambda i: (0, 0))
    grp = pl.BlockSpec((3, tm, ATT_WIDTH), lambda i: (0, i, 0))
    return pl.pallas_call(
        body, name="proj_dgrad_norm_bwd", grid=(S // tm,),
        in_specs=[grp, grp, grp, pl.BlockSpec((DREST_CHUNKS, tm, ATT_WIDTH), lambda i: (0, i, 0)),
                  pl.BlockSpec(memory_space=pltpu.VMEM), row, row, vec_spec],
        out_specs=(row, vec_spec),
        out_shape=(jax.ShapeDtypeStruct((S, D_MODEL), F32), jax.ShapeDtypeStruct((8, D_MODEL), F32)),
        compiler_params=_params(("arbitrary",)))(*dqkv, drest, w_int, x, dxn, vec)


def _proj_wgrad_part(h_t, d, n_chunks, chunk_block, name):
    S = h_t.shape[1]
    ts = 512

    def body(h_ref, d_ref, o_ref):
        i = pl.program_id(0)

        @pl.when(i == 0)
        def _():
            o_ref[...] = jnp.zeros_like(o_ref)

        ht = h_ref[...]
        for t in range(n_chunks):
            o_ref[:, t * ATT_WIDTH:(t + 1) * ATT_WIDTH] += _dot(ht, d_ref[t])

    return pl.pallas_call(
        body, name=name, grid=(S // ts,),
        in_specs=[pl.BlockSpec((D_MODEL, ts), lambda i: (0, i)),
                  pl.BlockSpec((n_chunks, ts, ATT_WIDTH), lambda i: (chunk_block, i, 0))],
        out_specs=pl.BlockSpec((D_MODEL, n_chunks * ATT_WIDTH), lambda i: (0, 0)),
        out_shape=jax.ShapeDtypeStruct((D_MODEL, n_chunks * ATT_WIDTH), F32),
        compiler_params=_params(("arbitrary",)))(h_t, d)


def _wgrad_tn(a, b, name):
    S, M = a.shape
    N = b.shape[1]
    ts = 1024

    def body(a_ref, b_ref, o_ref):
        i = pl.program_id(0)
        part = _dot_tn(a_ref[...], b_ref[...])

        @pl.when(i == 0)
        def _():
            o_ref[...] = part

        @pl.when(i > 0)
        def _():
            o_ref[...] += part

    return pl.pallas_call(
        body, name=name, grid=(S // ts,),
        in_specs=[pl.BlockSpec((ts, M), lambda i: (i, 0)), pl.BlockSpec((ts, N), lambda i: (i, 0))],
        out_specs=pl.BlockSpec((M, N), lambda i: (0, 0)),
        out_shape=jax.ShapeDtypeStruct((M, N), F32),
        compiler_params=_params(("arbitrary",)))(a, b)


def _attn_fwd(qkv, g, r):
    S = qkv.shape[0]
    L = S // r
    nb = L // BAND
    scale = HEAD_DIM ** -0.5
    qv = qkv.reshape(L, r * 3 * ATT_WIDTH)

    def body(q_ref, kp_ref, kc_ref, vp_ref, vc_ref, o_ref, st_ref):
        n = pl.program_id(1)
        ii = lax.broadcasted_iota(jnp.int32, (BAND, 2 * BAND), 0)
        kk = lax.broadcasted_iota(jnp.int32, (BAND, 2 * BAND), 1)
        mask = jnp.logical_and(jnp.logical_and(kk >= ii, kk <= ii + BAND),
                               jnp.logical_or(kk >= BAND, n > 0))
        lane = lax.broadcasted_iota(jnp.int32, (BAND, 128), 1)
        stats = jnp.zeros((BAND, 128), F32)
        for h in range(HEADS):
            sl = slice(h * HEAD_DIM, (h + 1) * HEAD_DIM)
            k_cat = jnp.concatenate([kp_ref[:, sl], kc_ref[:, sl]], axis=0)
            v_cat = jnp.concatenate([vp_ref[:, sl], vc_ref[:, sl]], axis=0)
            s = jnp.where(mask, _dot_nt(q_ref[:, sl], k_cat) * scale, NEG_INF)
            m = jnp.max(s, axis=1, keepdims=True)
            p = jnp.exp(s - m)
            l = jnp.sum(p, axis=1, keepdims=True)
            o_ref[:, sl] = _dot(p.astype(BF16), v_cat) / l
            stats = jnp.where(lane == h, m, stats)
            stats = jnp.where(lane == HEADS + h, l, stats)
        st_ref[...] = stats

    blk = (BAND, ATT_WIDTH)
    return pl.pallas_call(
        body, name=f"attn_fwd_g{g}", grid=(r, nb),
        in_specs=[
            pl.BlockSpec(blk, lambda rho, n: (n, rho * 3)),
            pl.BlockSpec(blk, lambda rho, n: (jnp.maximum(n - 1, 0), rho * 3 + 1)),
            pl.BlockSpec(blk, lambda rho, n: (n, rho * 3 + 1)),
            pl.BlockSpec(blk, lambda rho, n: (jnp.maximum(n - 1, 0), rho * 3 + 2)),
            pl.BlockSpec(blk, lambda rho, n: (n, rho * 3 + 2)),
        ],
        out_specs=(pl.BlockSpec(blk, lambda rho, n: (n, rho)),
                   pl.BlockSpec((BAND, 128), lambda rho, n: (n, rho))),
        out_shape=(jax.ShapeDtypeStruct((L, r * ATT_WIDTH), F32),
                   jax.ShapeDtypeStruct((L, r * 128), F32)),
        compiler_params=_params(("parallel", "parallel")))(qv, qv, qv, qv, qv)


def _attn_combine(outs, stats, rest):
    S = rest.shape[0]
    tm = 512
    gatt_blk = 8

    def body(o0_ref, o1_ref, o2_ref, s0_ref, s1_ref, s2_ref, g_ref, o_ref, a_ref, lse_ref):
        o_refs = (o0_ref, o1_ref, o2_ref)
        st = (s0_ref[...], s1_ref[...], s2_ref[...])
        lane = lax.broadcasted_iota(jnp.int32, (tm, 128), 1)
        lse_out = jnp.zeros((tm, 128), F32)
        for h in range(HEADS):
            sl = slice(h * HEAD_DIM, (h + 1) * HEAD_DIM)
            ms = [s[:, h:h + 1] for s in st]
            ls = [s[:, HEADS + h:HEADS + h + 1] for s in st]
            m_all = jnp.maximum(jnp.maximum(ms[0], ms[1]), ms[2])
            ws = [l * jnp.exp(m - m_all) for m, l in zip(ms, ls)]
            den = ws[0] + ws[1] + ws[2]
            o = (ws[0] * o_refs[0][:, sl] + ws[1] * o_refs[1][:, sl] + ws[2] * o_refs[2][:, sl]) / den
            o_ref[:, sl] = o
            a_ref[:, sl] = (o * _silu(g_ref[:, sl])).astype(BF16)
            lse_out = jnp.where(lane == h, m_all + jnp.log(den), lse_out)
        lse_ref[...] = lse_out

    o_spec = pl.BlockSpec((tm, ATT_WIDTH), lambda i: (i, 0))
    s_spec = pl.BlockSpec((tm, 128), lambda i: (i, 0))
    return pl.pallas_call(
        body, name="attn_combine", grid=(S // tm,),
        in_specs=[o_spec, o_spec, o_spec, s_spec, s_spec, s_spec,
                  pl.BlockSpec((tm, ATT_WIDTH), lambda i: (i, gatt_blk))],
        out_specs=(o_spec, o_spec, s_spec),
        out_shape=(jax.ShapeDtypeStruct((S, ATT_WIDTH), F32), jax.ShapeDtypeStruct((S, ATT_WIDTH), BF16),
                   jax.ShapeDtypeStruct((S, 128), F32)),
        compiler_params=_params(("parallel",)))(*outs, *stats, rest)


def _attn_bwd(qkv, do, lse, dvec, g, r):
    S = qkv.shape[0]
    L = S // r
    nb = L // BAND
    scale = HEAD_DIM ** -0.5
    qv = qkv.reshape(L, r * 3 * ATT_WIDTH)
    dov = do.reshape(L, r * ATT_WIDTH)
    lsev = lse.reshape(L, r * 128)
    dvv = dvec.reshape(L, r * 128)

    def body(qc_ref, qn_ref, k_ref, v_ref, doc_ref, don_ref, lc_ref, ln_ref, dc_ref, dn_ref,
             out_ref, carry_ref):
        j = pl.program_id(1)

        @pl.when(j == 0)
        def _():
            carry_ref[...] = jnp.zeros_like(carry_ref)

        ii = lax.broadcasted_iota(jnp.int32, (2 * BAND, BAND), 0)
        kk = lax.broadcasted_iota(jnp.int32, (2 * BAND, BAND), 1)
        mask = jnp.logical_or(
            jnp.logical_and(ii < BAND, kk <= ii),
            jnp.logical_and(jnp.logical_and(ii >= BAND, kk >= ii - BAND), j < nb - 1))
        lse2 = jnp.concatenate([lc_ref[...], ln_ref[...]], axis=0)
        dvec2 = jnp.concatenate([dc_ref[...], dn_ref[...]], axis=0)
        for h in range(HEADS):
            sl = slice(h * HEAD_DIM, (h + 1) * HEAD_DIM)
            k = k_ref[:, sl]
            v = v_ref[:, sl]
            q = jnp.concatenate([qc_ref[:, sl], qn_ref[:, sl]], axis=0)
            do = jnp.concatenate([doc_ref[:, sl], don_ref[:, sl]], axis=0)
            p = jnp.where(mask, jnp.exp(_dot_nt(q, k) * scale - lse2[:, h:h + 1]), 0.0)
            ds = (p * (_dot_nt(do, v) - dvec2[:, h:h + 1])).astype(BF16)
            dq2 = _dot(ds, k) * scale
            out_ref[0, :, sl] = (carry_ref[:, sl] + dq2[:BAND]).astype(BF16)
            out_ref[1, :, sl] = (_dot_tn(ds, q) * scale).astype(BF16)
            out_ref[2, :, sl] = _dot_tn(p.astype(BF16), do).astype(BF16)
            carry_ref[:, sl] = dq2[BAND:]

    blk = (BAND, ATT_WIDTH)
    sblk = (BAND, 128)

    def nxt(j):
        return jnp.minimum(j + 1, nb - 1)

    out = pl.pallas_call(
        body, name=f"attn_bwd_g{g}", grid=(r, nb),
        in_specs=[
            pl.BlockSpec(blk, lambda rho, j: (j, rho * 3)),
            pl.BlockSpec(blk, lambda rho, j: (nxt(j), rho * 3)),
            pl.BlockSpec(blk, lambda rho, j: (j, rho * 3 + 1)),
            pl.BlockSpec(blk, lambda rho, j: (j, rho * 3 + 2)),
            pl.BlockSpec(blk, lambda rho, j: (j, rho)),
            pl.BlockSpec(blk, lambda rho, j: (nxt(j), rho)),
            pl.BlockSpec(sblk, lambda rho, j: (j, rho)),
            pl.BlockSpec(sblk, lambda rho, j: (nxt(j), rho)),
            pl.BlockSpec(sblk, lambda rho, j: (j, rho)),
            pl.BlockSpec(sblk, lambda rho, j: (nxt(j), rho)),
        ],
        out_specs=pl.BlockSpec((3, BAND, ATT_WIDTH), lambda rho, j: (0, j, rho)),
        out_shape=jax.ShapeDtypeStruct((3, L, r * ATT_WIDTH), BF16),
        scratch_shapes=[pltpu.VMEM((BAND, ATT_WIDTH), F32)],
        compiler_params=_params(("parallel", "arbitrary")),
    )(qv, qv, qv, qv, dov, dov, lsev, lsev, dvv, dvv)
    return out.reshape(3, S, ATT_WIDTH)


LRU_T = 256


def _shift_down(x, s, rows):
    return pltpu.roll(x, s, 0)


def _gate_matmuls(ucb, w_ref, bias):
    parts = [_dot(ucb[:, j * 128:(j + 1) * 128], w_ref[j]) for j in range(8)]
    return jnp.concatenate(parts, axis=1) + bias


def _conv_fwd(u, u_prev8, cw_ref):
    T = u.shape[0]
    ue = jnp.concatenate([u_prev8, u], axis=0)
    uc = cw_ref[4:5, :] + cw_ref[0:1, :] * u
    for j in range(1, CONV_WIDTH):
        uc = uc + cw_ref[j:j + 1, :] * pltpu.roll(ue, j, 0)[8:8 + T]
    return uc


def _lru_fwd(rest, cw, wr2, wi2):
    S = rest.shape[0]
    T = LRU_T

    def body(u_ref, cw_ref, wr_ref, wi_ref, h_ref, a_ref, ucar_ref, hcar_ref):
        c = pl.program_id(0)

        @pl.when(c == 0)
        def _():
            ucar_ref[...] = jnp.zeros_like(ucar_ref)
            hcar_ref[...] = jnp.zeros_like(hcar_ref)

        u = u_ref[...]
        uc = _conv_fwd(u, ucar_ref[...], cw_ref)
        ucar_ref[...] = u[T - 8:, :]
        ucb = uc.astype(BF16)
        r = jax.nn.sigmoid(_gate_matmuls(ucb, wr_ref, cw_ref[5:6, :]))
        ig = jax.nn.sigmoid(_gate_matmuls(ucb, wi_ref, cw_ref[6:7, :]))
        log_a = -LRU_C * r * _softplus_neg(cw_ref[7:8, :])
        a = jnp.exp(log_a)
        b = jnp.sqrt(_neg_expm1(2.0 * log_a)) * (ig * uc)
        a_ref[...] = a
        row = lax.broadcasted_iota(jnp.int32, (T, D_MODEL), 0)
        acc_a, acc_b = a, b
        s = 1
        while s < T:
            keep = row >= s
            b_sh = jnp.where(keep, pltpu.roll(acc_b, s, 0), 0.0)
            a_sh = jnp.where(keep, pltpu.roll(acc_a, s, 0), 1.0)
            acc_b = acc_a * b_sh + acc_b
            acc_a = acc_a * a_sh
            s *= 2
        h = acc_b + acc_a * hcar_ref[0:1, :]
        h_ref[...] = h
        hcar_ref[...] = jnp.broadcast_to(h[T - 1:T, :], (8, D_MODEL))

    row_spec = pl.BlockSpec((T, D_MODEL), lambda c: (c, 0))
    return pl.pallas_call(
        body, name="lru_fwd", grid=(S // T,),
        in_specs=[row_spec, pl.BlockSpec((8, D_MODEL), lambda c: (0, 0)),
                  pl.BlockSpec((8, 128, 128), lambda c: (0, 0, 0)),
                  pl.BlockSpec((8, 128, 128), lambda c: (0, 0, 0))],
        out_specs=(row_spec, row_spec),
        out_shape=(jax.ShapeDtypeStruct((S, D_MODEL), F32), jax.ShapeDtypeStruct((S, D_MODEL), F32)),
        scratch_shapes=[pltpu.VMEM((8, D_MODEL), F32), pltpu.VMEM((8, D_MODEL), F32)],
        compiler_params=_params(("arbitrary",)))(rest, cw, wr2, wi2)


def _lru_bwd(dhl, hl, a_all, rest, cw, wr2, wi2, dproj):
    S = rest.shape[0]
    T = LRU_T
    nc = S // T

    def body(dh_ref, h_ref, hp_ref, a_ref, u_ref, up_ref, cw_ref, wr_ref, wi_ref, _alias,
             du_ref, acc_ref, gwr_ref, gwi_ref, gcar_ref, acar_ref, dcar_ref):
        step = pl.program_id(0)
        c = nc - 1 - step

        @pl.when(step == 0)
        def _():
            acc_ref[...] = jnp.zeros_like(acc_ref)
            gwr_ref[...] = jnp.zeros_like(gwr_ref)
            gwi_ref[...] = jnp.zeros_like(gwi_ref)
            gcar_ref[...] = jnp.zeros_like(gcar_ref)
            acar_ref[...] = jnp.zeros_like(acar_ref)
            dcar_ref[...] = jnp.zeros_like(dcar_ref)

        row = lax.broadcasted_iota(jnp.int32, (T, D_MODEL), 0)
        u = u_ref[...]
        u_prev = jnp.where(c > 0, up_ref[...], 0.0)
        h_prev8 = jnp.where(c > 0, hp_ref[...], 0.0)
        a = a_ref[...]
        h = h_ref[...]
        uc = _conv_fwd(u, u_prev, cw_ref)
        ucb = uc.astype(BF16)
        r = jax.nn.sigmoid(_gate_matmuls(ucb, wr_ref, cw_ref[5:6, :]))
        ig = jax.nn.sigmoid(_gate_matmuls(ucb, wi_ref, cw_ref[6:7, :]))
        sp = _softplus_neg(cw_ref[7:8, :])
        log_a = -LRU_C * r * sp
        mult = jnp.sqrt(_neg_expm1(2.0 * log_a))
        a_next = jnp.where(row < T - 1, pltpu.roll(a, T - 1, 0), acar_ref[0:1, :])
        acc_a, acc_b = a_next, dh_ref[...]
        s = 1
        while s < T:
            keep = row < T - s
            b_sh = jnp.where(keep, pltpu.roll(acc_b, T - s, 0), 0.0)
            a_sh = jnp.where(keep, pltpu.roll(acc_a, T - s, 0), 1.0)
            acc_b = acc_a * b_sh + acc_b
            acc_a = acc_a * a_sh
            s *= 2
        G = acc_b + acc_a * gcar_ref[0:1, :]
        gcar_ref[...] = jnp.broadcast_to(G[0:1, :], (8, D_MODEL))
        acar_ref[...] = jnp.broadcast_to(a[0:1, :], (8, D_MODEL))
        he = jnp.concatenate([h_prev8, h], axis=0)
        h_before = pltpu.roll(he, 1, 0)[8:8 + T]
        d_a = G * h_before
        d_mult = G * (ig * uc)
        d_ig = G * (mult * uc)
        duc = G * (mult * ig)
        d_log_a = d_a * a - d_mult * (a * a) / mult
        d_r = d_log_a * (-LRU_C * sp)
        d_sp = jnp.sum(d_log_a * (-LRU_C * r), axis=0, keepdims=True)
        dpre_r = d_r * r * (1.0 - r)
        dpre_i = d_ig * ig * (1.0 - ig)
        dprb = dpre_r.astype(BF16)
        dpib = dpre_i.astype(BF16)
        back = []
        for j in range(8):
            sl = slice(j * 128, (j + 1) * 128)
            back.append(_dot_nt(dprb[:, sl], wr_ref[j]) + _dot_nt(dpib[:, sl], wi_ref[j]))
            gwr_ref[j] += _dot_tn(ucb[:, sl], dprb[:, sl])
            gwi_ref[j] += _dot_tn(ucb[:, sl], dpib[:, sl])
        duc = duc + jnp.concatenate(back, axis=1)
        de = jnp.concatenate([duc, dcar_ref[...]], axis=0)
        du = cw_ref[0:1, :] * duc
        ue = jnp.concatenate([u_prev, u], axis=0)
        acc_ref[0:1, :] += jnp.sum(duc * u, axis=0, keepdims=True)
        for j in range(1, CONV_WIDTH):
            du = du + cw_ref[j:j + 1, :] * pltpu.roll(de, T + 8 - j, 0)[0:T]
            acc_ref[j:j + 1, :] += jnp.sum(duc * pltpu.roll(ue, j, 0)[8:8 + T], axis=0, keepdims=True)
        dcar_ref[...] = duc[0:8, :]
        acc_ref[4:5, :] += jnp.sum(duc, axis=0, keepdims=True)
        acc_ref[5:6, :] += jnp.sum(dpre_r, axis=0, keepdims=True)
        acc_ref[6:7, :] += jnp.sum(dpre_i, axis=0, keepdims=True)
        acc_ref[7:8, :] += d_sp
        du_ref[0] = du[:, 0:ATT_WIDTH].astype(BF16)
        du_ref[1] = du[:, ATT_WIDTH:].astype(BF16)

    def rev(step):
        return nc - 1 - step

    def prev8(step):
        return jnp.maximum(rev(step) * (T // 8) - 1, 0)

    row_spec = pl.BlockSpec((T, D_MODEL), lambda s: (rev(s), 0))
    prev_spec = pl.BlockSpec((8, D_MODEL), lambda s: (prev8(s), 0))
    vec_spec = pl.BlockSpec((8, D_MODEL), lambda s: (0, 0))
    w_spec = pl.BlockSpec((8, 128, 128), lambda s: (0, 0, 0))
    outs = pl.pallas_call(
        body, name="lru_bwd", grid=(nc,),
        in_specs=[row_spec, row_spec, prev_spec, row_spec, row_spec, prev_spec, vec_spec, w_spec, w_spec,
                  pl.BlockSpec(memory_space=pl.ANY)],
        out_specs=(pl.BlockSpec((2, T, ATT_WIDTH), lambda s: (4, rev(s), 0)), vec_spec, w_spec, w_spec),
        out_shape=(jax.ShapeDtypeStruct(dproj.shape, BF16), jax.ShapeDtypeStruct((8, D_MODEL), F32),
                   jax.ShapeDtypeStruct((8, 128, 128), F32), jax.ShapeDtypeStruct((8, 128, 128), F32)),
        scratch_shapes=[pltpu.VMEM((8, D_MODEL), F32), pltpu.VMEM((8, D_MODEL), F32),
                        pltpu.VMEM((8, D_MODEL), F32)],
        input_output_aliases={9: 0},
        compiler_params=_params(("arbitrary",)),
    )(dhl, hl, hl, a_all, rest, rest, cw, wr2, wi2, dproj)
    return outs


OUT_TM = 256


def _out_fwd(a_gated, hl, rest, x, vec, w_pa, w_pb, w_o):
    S = x.shape[0]
    tm = OUT_TM

    def body(ag_ref, hl_ref, gl_ref, ma_ref, mb_ref, x_ref, v_ref, wpa_ref, wpb_ref, wo_ref,
             xn_ref, mix_ref, ya_ref, yb_ref, bg_ref):
        bg = (hl_ref[...] * _silu(gl_ref[...])).astype(BF16)
        ya = _dot(ag_ref[...], wpa_ref[...])
        yb = _dot(bg, wpb_ref[...])
        mix = (jax.nn.sigmoid(ma_ref[...]) * ya + jax.nn.sigmoid(mb_ref[...]) * yb).astype(BF16)
        out = _dot(mix, wo_ref[...])
        rstd = lax.rsqrt(jnp.mean(out * out, axis=-1, keepdims=True) + NORM_EPS)
        xn_ref[...] = x_ref[...] + v_ref[1:2, :] * ((out * rstd) * v_ref[0:1, :])
        mix_ref[...] = mix
        ya_ref[...] = ya.astype(BF16)
        yb_ref[...] = yb.astype(BF16)
        bg_ref[...] = bg

    def col(j):
        return pl.BlockSpec((tm, D_MODEL), lambda i: (i, j))

    def whole(shape):
        return pl.BlockSpec(shape, lambda i: (0, 0))

    row = col(0)
    bf = jax.ShapeDtypeStruct((S, D_MODEL), BF16)
    return pl.pallas_call(
        body, name="out_fwd", grid=(S // tm,),
        in_specs=[pl.BlockSpec((tm, ATT_WIDTH), lambda i: (i, 0)), row, col(1), col(2), col(3), row,
                  whole((8, D_MODEL)), whole((ATT_WIDTH, D_MODEL)), whole((D_MODEL, D_MODEL)),
                  whole((D_MODEL, D_MODEL))],
        out_specs=(row, row, row, row, row),
        out_shape=(jax.ShapeDtypeStruct((S, D_MODEL), F32), bf, bf, bf, bf),
        compiler_params=_params(("parallel",)))(a_gated, hl, rest, rest, rest, x, vec, w_pa, w_pb, w_o)


def _out_bwd(dxn, mix, ya, yb, hl, rest, o_att, vec, w_pa, w_pb, w_o):
    S = dxn.shape[0]
    tm = OUT_TM

    def body(dxn_ref, mix_ref, ya_ref, yb_ref, hl_ref, gl_ref, ma_ref, mb_ref, ga_ref, o_ref, v_ref,
             wpa_ref, wpb_ref, wo_ref,
             dout_ref, dya_ref, dyb_ref, do_ref, dv_ref, dhl_ref, dp_ref, acc_ref):
        i = pl.program_id(0)

        @pl.when(i == 0)
        def _():
            acc_ref[...] = jnp.zeros_like(acc_ref)

        g_post = v_ref[0:1, :]
        gate = v_ref[1:2, :]
        dxn_v = dxn_ref[...]
        out = _dot(mix_ref[...], wo_ref[...])
        rstd = lax.rsqrt(jnp.mean(out * out, axis=-1, keepdims=True) + NORM_EPS)
        nrm = out * rstd
        acc_ref[0:1, :] += jnp.sum(dxn_v * (nrm * g_post), axis=0, keepdims=True)
        acc_ref[1:2, :] += jnp.sum(dxn_v * gate * nrm, axis=0, keepdims=True)
        dn = dxn_v * (gate * g_post)
        dout = (rstd * (dn - nrm * jnp.mean(dn * nrm, axis=-1, keepdims=True))).astype(BF16)
        dout_ref[...] = dout
        dmix = _dot_nt(dout, wo_ref[...])
        sa = jax.nn.sigmoid(ma_ref[...])
        sb = jax.nn.sigmoid(mb_ref[...])
        dya = (dmix * sa).astype(BF16)
        dyb = (dmix * sb).astype(BF16)
        dya_ref[...] = dya
        dyb_ref[...] = dyb
        dma = dmix * ya_ref[...].astype(F32) * (sa * (1.0 - sa))
        dmb = dmix * yb_ref[...].astype(F32) * (sb * (1.0 - sb))
        d_ag = _dot_nt(dya, wpa_ref[...])
        d_bg = _dot_nt(dyb, wpb_ref[...])
        gl = gl_ref[...]
        hl_v = hl_ref[...]
        dhl_ref[...] = d_bg * _silu(gl)
        dgl = d_bg * hl_v * _dsilu(gl)
        ga = ga_ref[...]
        o = o_ref[...]
        do = d_ag * _silu(ga)
        dga = d_ag * o * _dsilu(ga)
        do_ref[...] = do.astype(BF16)
        lane = lax.broadcasted_iota(jnp.int32, (tm, 128), 1)
        dvec = jnp.zeros((tm, 128), F32)
        for h in range(HEADS):
            sl = slice(h * HEAD_DIM, (h + 1) * HEAD_DIM)
            dvec = jnp.where(lane == h, jnp.sum(do[:, sl] * o[:, sl], axis=1, keepdims=True), dvec)
        dv_ref[...] = dvec
        dp_ref[0] = dgl[:, 0:ATT_WIDTH].astype(BF16)
        dp_ref[1] = dgl[:, ATT_WIDTH:].astype(BF16)
        dp_ref[2] = dma[:, 0:ATT_WIDTH].astype(BF16)
        dp_ref[3] = dma[:, ATT_WIDTH:].astype(BF16)
        dp_ref[4] = dmb[:, 0:ATT_WIDTH].astype(BF16)
        dp_ref[5] = dmb[:, ATT_WIDTH:].astype(BF16)
        dp_ref[6] = dga.astype(BF16)

    def col(j):
        return pl.BlockSpec((tm, D_MODEL), lambda i: (i, j))

    def whole(shape):
        return pl.BlockSpec(shape, lambda i: (0, 0))

    row = col(0)
    att = pl.BlockSpec((tm, ATT_WIDTH), lambda i: (i, 0))
    bf = jax.ShapeDtypeStruct((S, D_MODEL), BF16)
    return pl.pallas_call(
        body, name="out_bwd", grid=(S // tm,),
        in_specs=[row, row, row, row, row, col(1), col(2), col(3),
                  pl.BlockSpec((tm, ATT_WIDTH), lambda i: (i, 8)), att,
                  whole((8, D_MODEL)), whole((ATT_WIDTH, D_MODEL)), whole((D_MODEL, D_MODEL)),
                  whole((D_MODEL, D_MODEL))],
        out_specs=(row, row, row, att, pl.BlockSpec((tm, 128), lambda i: (i, 0)), row,
                   pl.BlockSpec((7, tm, ATT_WIDTH), lambda i: (0, i, 0)), whole((8, D_MODEL))),
        out_shape=(bf, bf, bf, jax.ShapeDtypeStruct((S, ATT_WIDTH), BF16),
                   jax.ShapeDtypeStruct((S, 128), F32), jax.ShapeDtypeStruct((S, D_MODEL), F32),
                   jax.ShapeDtypeStruct((DREST_CHUNKS, S, ATT_WIDTH), BF16),
                   jax.ShapeDtypeStruct((8, D_MODEL), F32)),
        compiler_params=_params(("arbitrary",)),
    )(dxn, mix, ya, yb, hl, rest, rest, rest, rest, o_att, vec, w_pa, w_pb, w_o)


def _loss_head(y, target):
    S = y.shape[0]
    tm = 512

    def body(y_ref, t_ref, dy_ref, acc_ref):
        i = pl.program_id(0)

        @pl.when(i == 0)
        def _():
            acc_ref[...] = jnp.zeros_like(acc_ref)

        err = y_ref[...] - t_ref[...]
        dy_ref[...] = err * (1.0 / D_MODEL)
        part = jnp.sum(jnp.sum(err * err, axis=1, keepdims=True), axis=0, keepdims=True)
        acc_ref[...] += jnp.broadcast_to(part, acc_ref.shape)

    row = pl.BlockSpec((tm, D_MODEL), lambda i: (i, 0))
    return pl.pallas_call(
        body, name="loss_head", grid=(S // tm,),
        in_specs=[row, row],
        out_specs=(row, pl.BlockSpec((8, 128), lambda i: (0, 0))),
        out_shape=(jax.ShapeDtypeStruct((S, D_MODEL), F32), jax.ShapeDtypeStruct((8, 128), F32)),
        compiler_params=_params(("arbitrary",)))(y, target)


def _adamw(gsrc, w, m, v, n_stack, name, rows=None):
    R, C = w.shape
    budget = 96 * 1024
    tr = rows
    if tr is None:
        tr = R
        while tr * C > budget and tr % 16 == 0:
            tr //= 2
    assert R % tr == 0 and (tr % 8 == 0 or tr == R)
    c1 = 1.0 - ADAM_B1 ** ADAM_STEP
    c2 = 1.0 - ADAM_B2 ** ADAM_STEP

    def body(g_ref, w_ref, m_ref, v_ref, go_ref, d_ref, mo_ref, vo_ref):
        if n_stack:
            g = g_ref[0].astype(F32)
            for s in range(1, n_stack):
                g = g + g_ref[s].astype(F32)
        else:
            g = g_ref[...]
        m_new = ADAM_B1 * m_ref[...] + (1.0 - ADAM_B1) * g
        v_new = ADAM_B2 * v_ref[...] + (1.0 - ADAM_B2) * (g * g)
        m_hat = m_new / c1
        v_hat = v_new / c2
        go_ref[...] = g
        d_ref[...] = -ADAM_LR * (m_hat / (jnp.sqrt(v_hat) + ADAM_EPS) + ADAM_WD * w_ref[...])
        mo_ref[...] = m_new
        vo_ref[...] = v_new

    blk = pl.BlockSpec((tr, C), lambda i: (i, 0))
    g_spec = pl.BlockSpec((n_stack, tr, C), lambda i: (0, i, 0)) if n_stack else blk
    shp = jax.ShapeDtypeStruct((R, C), F32)
    return pl.pallas_call(
        body, name=name, grid=(R // tr,),
        in_specs=[g_spec, blk, blk, blk], out_specs=(blk, blk, blk, blk),
        out_shape=(shp, shp, shp, shp),
        compiler_params=_params(("parallel",)))(gsrc, w, m, v)


def _pair_blocks(w):
    w = w.reshape(8, 2, 64, 64)
    z = jnp.zeros((8, 64, 64), w.dtype)
    top = jnp.concatenate([w[:, 0], z], axis=2)
    bot = jnp.concatenate([z, w[:, 1]], axis=2)
    return jnp.concatenate([top, bot], axis=1).astype(BF16)


def _unpair_blocks(g):
    return jnp.stack([g[:, :64, :64], g[:, 64:, 64:]], axis=1).reshape(16, 64, 64)


def _layer_fwd(x, mod, p):
    zeros = jnp.zeros((5, D_MODEL), F32)
    vec_pre = jnp.concatenate([p["g_pre"][None], mod[1:2], mod[0:1], zeros], axis=0)
    h_t, q0, q1, q2, rest = _norm_proj_fwd(x, vec_pre, p["w_int"])
    S = x.shape[0]
    qkv, outs, stats = [q0, q1, q2], [], []
    for g, r in enumerate(DILATIONS):
        o_g, st_g = _attn_fwd(qkv[g], g, r)
        outs.append(o_g.reshape(S, ATT_WIDTH))
        stats.append(st_g.reshape(S, 128))
    o_att, a_gated, lse = _attn_combine(outs, stats, rest)
    hl, a_dec = _lru_fwd(rest, p["cw"], p["wr2"], p["wi2"])
    vec_post = jnp.concatenate([p["g_post"][None], mod[2:3], jnp.zeros((6, D_MODEL), F32)], axis=0)
    x_new, mix, ya, yb, b_gated = _out_fwd(a_gated, hl, rest, x, vec_post, p["w_pa"], p["w_pb"], p["w_o"])
    saved = dict(x=x, h_t=h_t, qkv=qkv, rest=rest, o_att=o_att, a_gated=a_gated, lse=lse, hl=hl, a_dec=a_dec,
                 mix=mix, ya=ya, yb=yb, b_gated=b_gated, vec_pre=vec_pre, vec_post=vec_post)
    return x_new, saved


def _layer_bwd(dxn, sv, p):
    dout, dya, dyb, do, dvec, dhl, drest, acc_out = _out_bwd(
        dxn, sv["mix"], sv["ya"], sv["yb"], sv["hl"], sv["rest"], sv["o_att"], sv["vec_post"],
        p["w_pa"], p["w_pb"], p["w_o"])
    drest, acc_lru, gwr2, gwi2 = _lru_bwd(dhl, sv["hl"], sv["a_dec"], sv["rest"], p["cw"], p["wr2"],
                                          p["wi2"], drest)
    dqkv = [_attn_bwd(sv["qkv"][g], do, sv["lse"], dvec, g, r) for g, r in enumerate(DILATIONS)]
    dx, acc_pre = _proj_dgrad_norm_bwd(dqkv, drest, p["w_int"], sv["x"], dxn, sv["vec_pre"])
    parts = [_proj_wgrad_part(sv["h_t"], dqkv[g], 3, 0, f"proj_wgrad_g{g}") for g in range(3)]
    parts.append(_proj_wgrad_part(sv["h_t"], drest, 2, 4, "proj_wgrad_u"))
    parts.append(_proj_wgrad_part(sv["h_t"], drest, 7, 0, "proj_wgrad_rest"))
    gw_int = jnp.concatenate(parts, axis=1).astype(BF16).reshape(D_MODEL, N_CHUNKS, ATT_WIDTH)
    inv_perm = [CHUNK_PERM.index(k) for k in range(N_CHUNKS)]
    grads = dict(
        w_in=gw_int[:, jnp.array(inv_perm)].reshape(D_MODEL, N_CHUNKS * ATT_WIDTH),
        w_o=_wgrad_tn(sv["mix"], dout, "wgrad_o"),
        w_pa=_wgrad_tn(sv["a_gated"], dya, "wgrad_pa"),
        w_pb=_wgrad_tn(sv["b_gated"], dyb, "wgrad_pb"),
        dmod=jnp.concatenate([acc_pre[0], acc_pre[1], acc_out[0]]),
        g_pre=acc_pre[2], g_post=acc_out[1],
        conv_w=acc_lru[0:4], conv_b=acc_lru[4], b_rg=acc_lru[5], b_ig=acc_lru[6],
        lru_lambda=acc_lru[7] * (-jax.nn.sigmoid(-p["lru_lambda"])),
        w_rg=_unpair_blocks(gwr2), w_ig=_unpair_blocks(gwi2))
    return dx, grads


SMALL_NAMES = ("b_mod", "g_pre", "conv_b", "w_rg", "b_rg", "w_ig", "b_ig", "lru_lambda", "g_post")


def kernel(x, c, w_mod, b_mod, g_pre, w_in, conv_w, conv_b, w_rg, b_rg, w_ig, b_ig, lru_lambda, w_pa, w_pb, w_o, g_post, loss_target, m_w_mod, m_b_mod, m_g_pre, m_w_in, m_conv_w, m_conv_b, m_w_rg, m_b_rg, m_w_ig, m_b_ig, m_lru_lambda, m_w_pa, m_w_pb, m_w_o, m_g_post, v_w_mod, v_b_mod, v_g_pre, v_w_in, v_conv_w, v_conv_b, v_w_rg, v_b_rg, v_w_ig, v_b_ig, v_lru_lambda, v_w_pa, v_w_pb, v_w_o, v_g_post):
    W = dict(w_mod=w_mod, b_mod=b_mod, g_pre=g_pre, w_in=w_in, conv_w=conv_w, conv_b=conv_b, w_rg=w_rg,
             b_rg=b_rg, w_ig=w_ig, b_ig=b_ig, lru_lambda=lru_lambda, w_pa=w_pa, w_pb=w_pb, w_o=w_o,
             g_post=g_post)
    M = dict(w_mod=m_w_mod, b_mod=m_b_mod, g_pre=m_g_pre, w_in=m_w_in, conv_w=m_conv_w, conv_b=m_conv_b,
             w_rg=m_w_rg, b_rg=m_b_rg, w_ig=m_w_ig, b_ig=m_b_ig, lru_lambda=m_lru_lambda, w_pa=m_w_pa,
             w_pb=m_w_pb, w_o=m_w_o, g_post=m_g_post)
    V = dict(w_mod=v_w_mod, b_mod=v_b_mod, g_pre=v_g_pre, w_in=v_w_in, conv_w=v_conv_w, conv_b=v_conv_b,
             w_rg=v_w_rg, b_rg=v_b_rg, w_ig=v_w_ig, b_ig=v_b_ig, lru_lambda=v_lru_lambda, w_pa=v_w_pa,
             w_pb=v_w_pb, w_o=v_w_o, g_post=v_g_post)
    S = x.shape[1]
    me = 4 * lax.axis_index("x") + 2 * lax.axis_index("y") + lax.axis_index("c")
    n_mod = w_mod.shape[2]
    n_in = w_in.shape[2]
    n_conv = conv_w.shape[2]

    c_rows = jnp.broadcast_to(c, (8, D_MODEL))
    g_c, g_win, g_wpa, g_wpb, g_wo, g_cw = _exchange(
        [c_rows, w_in.astype(BF16), w_pa.astype(BF16), w_pb.astype(BF16), w_o.astype(BF16), conv_w],
        ["ag"] * 6, "gather_weights")
    c_all = g_c[:, 0, :]
    c_pad = jnp.concatenate([c_all, jnp.zeros((8, D_MODEL), F32)], axis=0)
    conv_w_full = jnp.transpose(g_cw, (1, 2, 0, 3)).reshape(2, CONV_WIDTH, D_MODEL)

    mod_cols = _mod_fwd(c_pad, w_mod)
    mod_src = jnp.transpose(mod_cols[:, :8], (1, 0, 2))
    mod_src = jnp.concatenate([mod_src, jnp.zeros((8, 6, n_mod), F32)], axis=1)
    (mod_got,) = _exchange([mod_src], ["a2a"], "scatter_mod")
    mod = jnp.transpose(mod_got[:, :2], (1, 0, 2)).reshape(2, 3 * D_MODEL) + b_mod

    layers = []
    for layer in range(2):
        w_full = jnp.transpose(g_win[:, layer], (1, 0, 2)).reshape(D_MODEL, N_CHUNKS, ATT_WIDTH)
        w_int = w_full[:, jnp.array(CHUNK_PERM)].reshape(D_MODEL, N_CHUNKS * ATT_WIDTH)
        cw = jnp.concatenate([conv_w_full[layer], conv_b[layer][None], b_rg[layer][None], b_ig[layer][None],
                              lru_lambda[layer][None]], axis=0)
        layers.append(dict(
            w_int=w_int,
            w_pa=jnp.transpose(g_wpa[:, layer], (1, 0, 2)).reshape(ATT_WIDTH, D_MODEL),
            w_pb=g_wpb[:, layer].reshape(D_MODEL, D_MODEL),
            w_o=g_wo[:, layer].reshape(D_MODEL, D_MODEL),
            cw=cw, wr2=_pair_blocks(w_rg[layer]), wi2=_pair_blocks(w_ig[layer]),
            g_pre=g_pre[layer], g_post=g_post[layer], lru_lambda=lru_lambda[layer]))

    act = x[0]
    saved = []
    for layer in range(2):
        act, sv = _layer_fwd(act, mod[layer].reshape(3, D_MODEL), layers[layer])
        saved.append(sv)
    dy, loss_acc = _loss_head(act, loss_target[0])
    loss = lax.psum(0.5 * loss_acc[0, 0] / D_MODEL, MESH_AXES)
    grads = [None, None]
    for layer in (1, 0):
        dy, grads[layer] = _layer_bwd(dy, saved[layer], layers[layer])
    grad_x = dy[None]

    def stack2(name):
        return jnp.stack([grads[0][name], grads[1][name]], axis=0)

    gw_in = stack2("w_in").astype(BF16).reshape(2, D_MODEL, N_DEV, n_in)
    gw_in = jnp.transpose(gw_in, (2, 0, 1, 3))
    gw_pa = jnp.transpose(stack2("w_pa").astype(BF16).reshape(2, ATT_WIDTH, N_DEV, 128), (2, 0, 1, 3))
    gw_pb = jnp.transpose(stack2("w_pb").astype(BF16).reshape(2, N_DEV, 128, D_MODEL), (1, 0, 2, 3))
    gw_o = jnp.transpose(stack2("w_o").astype(BF16).reshape(2, N_DEV, 128, D_MODEL), (1, 0, 2, 3))
    dmod = stack2("dmod")
    dmod_src = jnp.transpose(dmod.reshape(2, N_DEV, n_mod), (1, 0, 2))
    dmod_src = jnp.concatenate([dmod_src, jnp.zeros((8, 6, n_mod), F32)], axis=1)
    small = {"b_mod": dmod}
    for name in SMALL_NAMES[1:]:
        small[name] = stack2(name)
    small_vec = jnp.concatenate([small[n].reshape(-1) for n in SMALL_NAMES] + [stack2("conv_w").reshape(-1)])
    n_small = small_vec.shape[0]
    small_vec = small_vec.reshape(n_small // 128, 128)
    r_in, r_pa, r_pb, r_o, r_dmod, r_small = _exchange(
        [gw_in, gw_pa, gw_pb, gw_o, dmod_src, small_vec], ["a2a", "a2a", "a2a", "a2a", "a2a", "ag"],
        "exchange_grads")

    res = {}
    res["w_in"] = _adamw(r_in.reshape(8, 2 * D_MODEL, n_in), w_in.reshape(2 * D_MODEL, n_in),
                         m_w_in.reshape(2 * D_MODEL, n_in), v_w_in.reshape(2 * D_MODEL, n_in), 8, "adamw_w_in")
    res["w_pa"] = _adamw(r_pa.reshape(8, 2 * ATT_WIDTH, 128), w_pa.reshape(2 * ATT_WIDTH, 128),
                         m_w_pa.reshape(2 * ATT_WIDTH, 128), v_w_pa.reshape(2 * ATT_WIDTH, 128), 8, "adamw_w_pa")
    res["w_pb"] = _adamw(r_pb.reshape(8, 256, D_MODEL), w_pb.reshape(256, D_MODEL),
                         m_w_pb.reshape(256, D_MODEL), v_w_pb.reshape(256, D_MODEL), 8, "adamw_w_pb")
    res["w_o"] = _adamw(r_o.reshape(8, 256, D_MODEL), w_o.reshape(256, D_MODEL),
                        m_w_o.reshape(256, D_MODEL), v_w_o.reshape(256, D_MODEL), 8, "adamw_w_o")
    dmod_all = jnp.transpose(r_dmod[:, :2], (1, 0, 2))
    gw_mod = _mod_wgrad(jnp.transpose(c_all), dmod_all)
    res["w_mod"] = _adamw(gw_mod.reshape(2 * D_MODEL, n_mod), w_mod.reshape(2 * D_MODEL, n_mod),
                          m_w_mod.reshape(2 * D_MODEL, n_mod), v_w_mod.reshape(2 * D_MODEL, n_mod), 0,
                          "adamw_w_mod")
    n_rep = sum(W[n].size for n in SMALL_NAMES)
    rep_rows = n_rep // 128

    def flat_rep(src):
        return jnp.concatenate([src[n].reshape(-1) for n in SMALL_NAMES]).reshape(rep_rows, 128)

    rep = _adamw(r_small, flat_rep(W), flat_rep(M), flat_rep(V), 8, "adamw_small", rows=rep_rows // 2)
    off = 0
    for name in SMALL_NAMES:
        size = W[name].size
        res[name] = tuple(t.reshape(-1)[off:off + size].reshape(W[name].shape) for t in rep)
        off += size
    conv_stack = r_small[:, rep_rows:].reshape(8, 2, CONV_WIDTH, D_MODEL)
    conv_stack = lax.dynamic_slice_in_dim(conv_stack, me * n_conv, n_conv, axis=3).reshape(8, 8, n_conv)
    res["conv_w"] = _adamw(conv_stack, conv_w.reshape(8, n_conv), m_conv_w.reshape(8, n_conv),
                           v_conv_w.reshape(8, n_conv), 8, "adamw_conv_w")

    names = ("w_mod", "b_mod", "g_pre", "w_in", "conv_w", "conv_b", "w_rg", "b_rg", "w_ig", "b_ig",
             "lru_lambda", "w_pa", "w_pb", "w_o", "g_post")
    outs = [loss, grad_x]
    for k in range(4):
        outs.extend(res[n][k].reshape(W[n].shape) for n in names)
    return tuple(outs)
```

```python
import functools

import jax
import jax.numpy as jnp
from jax import lax
from jax.experimental import pallas as pl
from jax.experimental.pallas import tpu as pltpu

F32 = jnp.float32
BF16 = jnp.bfloat16

N_DEV = 8
D_MODEL = 1024
HEAD_DIM = 128
HEADS = 4
ATT_WIDTH = HEADS * HEAD_DIM
DILATIONS = (1, 4, 16)
BAND = 128
N_CHUNKS = 18
QKV_CHUNKS = 9
CONV_WIDTH = 4
LRU_C = 8.0
NORM_EPS = 1e-6
NEG_INF = -1e30
ADAM_LR = 0.001
ADAM_B1 = 0.9
ADAM_B2 = 0.999
ADAM_EPS = 1e-08
ADAM_WD = 0.01
ADAM_STEP = 10
CHUNK_PERM = (0, 3, 6, 1, 4, 7, 2, 5, 8, 10, 11, 12, 13, 14, 15, 16, 17, 9)
DREST_CHUNKS = 10
VMEM_LIMIT = 56 * 1024 * 1024
MESH_AXES = ("x", "y", "c")


def _params(sem=None):
    return pltpu.CompilerParams(dimension_semantics=sem, vmem_limit_bytes=VMEM_LIMIT)


def _silu(x):
    return x * jax.nn.sigmoid(x)


def _dsilu(x):
    s = jax.nn.sigmoid(x)
    return s * (1.0 + x * (1.0 - s))


def _neg_expm1(x):
    series = -x * (1.0 + x * (0.5 + x * (1.0 / 6.0 + x * (1.0 / 24.0 + x * (1.0 / 120.0)))))
    return jnp.where(x > -0.05, series, 1.0 - jnp.exp(x))


def _softplus_neg(lam):
    z = jnp.exp(-jnp.abs(lam))
    small = z * (1.0 - z * (0.5 - z * (1.0 / 3.0 - z * 0.25)))
    log1p_z = jnp.where(z < 1e-2, small, jnp.log(1.0 + z))
    return jnp.maximum(-lam, 0.0) + log1p_z


def _dot(a, b):
    return jnp.dot(a, b, preferred_element_type=F32)


def _dot_nt(a, b):
    return lax.dot_general(a, b, (((1,), (1,)), ((), ())), preferred_element_type=F32)


def _dot_tn(a, b):
    return lax.dot_general(a, b, (((0,), (0,)), ((), ())), preferred_element_type=F32)


ANY_SPEC = pl.BlockSpec(memory_space=pl.ANY)


def _exchange_shapes(arrs, modes):
    return [jax.ShapeDtypeStruct(((N_DEV,) + a.shape) if mode == "ag" else a.shape, a.dtype)
            for a, mode in zip(arrs, modes)]


def _exchange_sems(n):
    return [pltpu.SemaphoreType.DMA((7 * n,)), pltpu.SemaphoreType.DMA((7 * n,)),
            pltpu.SemaphoreType.DMA((n,))]


def _exchange_copies(ins, outs, modes, sems):
    n = len(ins)
    send_sems, recv_sems, local_sems = sems
    x, y, c = lax.axis_index("x"), lax.axis_index("y"), lax.axis_index("c")
    me = 4 * x + 2 * y + c

    def src_for(a, dev):
        return ins[a] if modes[a] == "ag" else ins[a].at[dev]

    local = [pltpu.make_async_copy(src_for(a, me), outs[a].at[me], local_sems.at[a]) for a in range(n)]
    sends, arrivals = [], []
    for k in range(1, N_DEV):
        px = 1 - x if (k >> 2) & 1 else x
        py = 1 - y if (k >> 1) & 1 else y
        pc = 1 - c if k & 1 else c
        peer = 4 * px + 2 * py + pc
        for a in range(n):
            s = (k - 1) * n + a
            for dst, group in ((me, sends), (peer, arrivals)):
                group.append(pltpu.make_async_remote_copy(
                    src_ref=src_for(a, peer), dst_ref=outs[a].at[dst],
                    send_sem=send_sems.at[s], recv_sem=recv_sems.at[s],
                    device_id=(px, py, pc), device_id_type=pl.DeviceIdType.MESH))
    return local, sends, arrivals


def _exchange_start(copies):
    local, sends, _ = copies
    for cp in local + sends:
        cp.start()


def _exchange_wait(copies):
    local, sends, arrivals = copies
    for cp in arrivals:
        cp.wait_recv()
    for cp in sends:
        cp.wait_send()
    for cp in local:
        cp.wait()


def _exchange(arrs, modes, name):
    n = len(arrs)

    def body(*refs):
        copies = _exchange_copies(refs[:n], refs[n:2 * n], modes, refs[2 * n:])
        _exchange_start(copies)
        _exchange_wait(copies)

    outs = pl.pallas_call(
        body, name=name, out_shape=tuple(_exchange_shapes(arrs, modes)),
        in_specs=[ANY_SPEC] * n, out_specs=tuple([ANY_SPEC] * n),
        scratch_shapes=_exchange_sems(n),
    )(*arrs)
    return list(outs)


def _call(body, args, *, name, grid, in_specs, out_specs, out_shape, scratch_shapes=(), aliases=None,
          hosted=None):
    sem = ("arbitrary",) * len(grid)
    if hosted is None:
        outs = pl.pallas_call(
            body, name=name, grid=grid, in_specs=in_specs, out_specs=tuple(out_specs),
            out_shape=tuple(out_shape), scratch_shapes=list(scratch_shapes),
            input_output_aliases=aliases or {}, compiler_params=_params(sem))(*args)
        return list(outs), []
    x_arrs, modes = hosted
    n_in, n_out, n_scr, nx = len(args), len(out_shape), len(scratch_shapes), len(x_arrs)

    def wrapped(*refs):
        ins, x_ins = refs[:n_in], refs[n_in:n_in + nx]
        outs = refs[n_in + nx:n_in + nx + n_out]
        x_outs = refs[n_in + nx + n_out:n_in + 2 * nx + n_out]
        scr = refs[n_in + 2 * nx + n_out:n_in + 2 * nx + n_out + n_scr]
        sems = refs[n_in + 2 * nx + n_out + n_scr:]
        first = pl.program_id(0) == 0
        last = pl.program_id(0) == grid[0] - 1
        for axis in range(1, len(grid)):
            first = jnp.logical_and(first, pl.program_id(axis) == 0)
            last = jnp.logical_and(last, pl.program_id(axis) == grid[axis] - 1)

        @pl.when(first)
        def _():
            _exchange_start(_exchange_copies(x_ins, x_outs, modes, sems))

        body(*ins, *outs, *scr)

        @pl.when(last)
        def _():
            _exchange_wait(_exchange_copies(x_ins, x_outs, modes, sems))

    outs = pl.pallas_call(
        wrapped, name=name, grid=grid, in_specs=list(in_specs) + [ANY_SPEC] * nx,
        out_specs=tuple(out_specs) + tuple([ANY_SPEC] * nx),
        out_shape=tuple(out_shape) + tuple(_exchange_shapes(x_arrs, modes)),
        scratch_shapes=list(scratch_shapes) + _exchange_sems(nx),
        input_output_aliases=aliases or {}, compiler_params=_params(sem))(*args, *x_arrs)
    return list(outs[:n_out]), list(outs[n_out:])


def _mod_fwd(c_pad, w_mod):
    def body(c_ref, w_ref, o_ref):
        sc = _silu(c_ref[...]).astype(BF16)
        for layer in range(2):
            o_ref[layer] = _dot(sc, w_ref[layer].astype(BF16))

    return pl.pallas_call(
        body, name="mod_fwd", out_shape=jax.ShapeDtypeStruct((2, 16, w_mod.shape[2]), F32),
        compiler_params=_params())(c_pad, w_mod)


def _mod_wgrad(c_t, dmod):
    n_cols = dmod.shape[2]

    def body(c_ref, d_ref, o_ref):
        sc = _silu(c_ref[...]).astype(BF16).astype(F32)
        for layer in range(2):
            dm = d_ref[layer].astype(BF16).astype(F32)
            acc = sc[:, 0:1] * dm[0:1, :]
            for b in range(1, N_DEV):
                acc = acc + sc[:, b:b + 1] * dm[b:b + 1, :]
            o_ref[layer] = acc

    return pl.pallas_call(
        body, name="mod_wgrad", out_shape=jax.ShapeDtypeStruct((2, D_MODEL, n_cols), F32),
        compiler_params=_params())(c_t, dmod)


GROUP_COLS = 3 * ATT_WIDTH
PROJ_TM = 256


def _norm_proj_fwd(x, vec, w_int, hosted=None):
    S = x.shape[0]
    tm = PROJ_TM

    def body(x_ref, v_ref, w_ref, ht_ref, q0_ref, q1_ref, q2_ref, rest_ref):
        xv = x_ref[...]
        rstd = lax.rsqrt(jnp.mean(xv * xv, axis=-1, keepdims=True) + NORM_EPS)
        hf = ((xv * rstd) * v_ref[0:1, :]) * (1.0 + v_ref[1:2, :]) + v_ref[2:3, :]
        ht_ref[...] = hf.T.astype(BF16)
        h = hf.astype(BF16)
        for g, q_ref in enumerate((q0_ref, q1_ref, q2_ref)):
            q_ref[...] = _dot(h, w_ref[:, g * GROUP_COLS:(g + 1) * GROUP_COLS]).astype(BF16)
        rest_ref[...] = _dot(h, w_ref[:, 3 * GROUP_COLS:])

    n_rest = QKV_CHUNKS * ATT_WIDTH
    qkv_shape = jax.ShapeDtypeStruct((S, GROUP_COLS), BF16)
    qkv_spec = pl.BlockSpec((tm, GROUP_COLS), lambda i: (i, 0))
    return _call(
        body, (x, vec, w_int), name="norm_proj_fwd", grid=(S // tm,),
        in_specs=[pl.BlockSpec((tm, D_MODEL), lambda i: (i, 0)),
                  pl.BlockSpec((8, D_MODEL), lambda i: (0, 0)),
                  pl.BlockSpec(memory_space=pltpu.VMEM)],
        out_specs=(pl.BlockSpec((D_MODEL, tm), lambda i: (0, i)), qkv_spec, qkv_spec, qkv_spec,
                   pl.BlockSpec((tm, n_rest), lambda i: (i, 0))),
        out_shape=(jax.ShapeDtypeStruct((D_MODEL, S), BF16), qkv_shape, qkv_shape, qkv_shape,
                   jax.ShapeDtypeStruct((S, n_rest), F32)),
        hosted=hosted)


def _proj_dgrad_norm_bwd(dqkv, drest, w_int, x, dxn, vec, hosted=None):
    S = x.shape[0]
    tm = PROJ_TM

    def body(d0_ref, d1_ref, d2_ref, dr_ref, w_ref, x_ref, dxn_ref, v_ref, dx_ref, acc_ref):
        i = pl.program_id(0)

        @pl.when(i == 0)
        def _():
            acc_ref[...] = jnp.zeros_like(acc_ref)

        def wcols(k):
            return w_ref[:, k * ATT_WIDTH:(k + 1) * ATT_WIDTH]

        dhv = None
        for g, d_ref in enumerate((d0_ref, d1_ref, d2_ref)):
            for t in range(3):
                part = _dot_nt(d_ref[t], wcols(3 * g + t))
                dhv = part if dhv is None else dhv + part
        for t in range(7):
            dhv = dhv + _dot_nt(dr_ref[t], wcols(11 + t))
        for t in range(2):
            dhv = dhv + _dot_nt(dr_ref[8 + t], wcols(9 + t))

        xv = x_ref[...]
        g = v_ref[0:1, :]
        sc1 = 1.0 + v_ref[1:2, :]
        rstd = lax.rsqrt(jnp.mean(xv * xv, axis=-1, keepdims=True) + NORM_EPS)
        xhat = xv * rstd
        acc_ref[0:1, :] += jnp.sum(dhv, axis=0, keepdims=True)
        acc_ref[1:2, :] += jnp.sum(dhv * (xhat * g), axis=0, keepdims=True)
        acc_ref[2:3, :] += jnp.sum(dhv * xhat * sc1, axis=0, keepdims=True)
        dxhat = dhv * (g * sc1)
        dx = rstd * (dxhat - xhat * jnp.mean(dxhat * xhat, axis=-1, keepdims=True))
        dx_ref[...] = dx + dxn_ref[...]

    row = pl.BlockSpec((tm, D_MODEL), lambda i: (i, 0))
    vec_spec = pl.BlockSpec((8, D_MODEL), lambda i: (0, 0))
    grp = pl.BlockSpec((3, tm, ATT_WIDTH), lambda i: (0, i, 0))
    return _call(
        body, (*dqkv, drest, w_int, x, dxn, vec), name="proj_dgrad_norm_bwd", grid=(S // tm,),
        in_specs=[grp, grp, grp, pl.BlockSpec((DREST_CHUNKS, tm, ATT_WIDTH), lambda i: (0, i, 0)),
                  pl.BlockSpec(memory_space=pltpu.VMEM), row, row, vec_spec],
        out_specs=(row, vec_spec),
        out_shape=(jax.ShapeDtypeStruct((S, D_MODEL), F32), jax.ShapeDtypeStruct((8, D_MODEL), F32)),
        hosted=hosted)


def _proj_wgrad_part(h_t, d, n_chunks, chunk_block, name):
    S = h_t.shape[1]
    ts = 512

    def body(h_ref, d_ref, o_ref):
        i = pl.program_id(0)

        @pl.when(i == 0)
        def _():
            o_ref[...] = jnp.zeros_like(o_ref)

        ht = h_ref[...]
        for t in range(n_chunks):
            o_ref[:, t * ATT_WIDTH:(t + 1) * ATT_WIDTH] += _dot(ht, d_ref[t])

    return pl.pallas_call(
        body, name=name, grid=(S // ts,),
        in_specs=[pl.BlockSpec((D_MODEL, ts), lambda i: (0, i)),
                  pl.BlockSpec((n_chunks, ts, ATT_WIDTH), lambda i: (chunk_block, i, 0))],
        out_specs=pl.BlockSpec((D_MODEL, n_chunks * ATT_WIDTH), lambda i: (0, 0)),
        out_shape=jax.ShapeDtypeStruct((D_MODEL, n_chunks * ATT_WIDTH), F32),
        compiler_params=_params(("arbitrary",)))(h_t, d)


def _wgrad_tn(a, b, name):
    S, M = a.shape
    N = b.shape[1]
    ts = 1024

    def body(a_ref, b_ref, o_ref):
        i = pl.program_id(0)
        part = _dot_tn(a_ref[...], b_ref[...])

        @pl.when(i == 0)
        def _():
            o_ref[...] = part

        @pl.when(i > 0)
        def _():
            o_ref[...] += part

    return pl.pallas_call(
        body, name=name, grid=(S // ts,),
        in_specs=[pl.BlockSpec((ts, M), lambda i: (i, 0)), pl.BlockSpec((ts, N), lambda i: (i, 0))],
        out_specs=pl.BlockSpec((M, N), lambda i: (0, 0)),
        out_shape=jax.ShapeDtypeStruct((M, N), F32),
        compiler_params=_params(("arbitrary",)))(a, b)


def _attn_fwd(qkv, g, r):
    S = qkv.shape[0]
    L = S // r
    nb = L // BAND
    scale = HEAD_DIM ** -0.5
    qv = qkv.reshape(L, r * 3 * ATT_WIDTH)

    def body(q_ref, kp_ref, kc_ref, vp_ref, vc_ref, o_ref, st_ref):
        n = pl.program_id(1)
        ii = lax.broadcasted_iota(jnp.int32, (BAND, 2 * BAND), 0)
        kk = lax.broadcasted_iota(jnp.int32, (BAND, 2 * BAND), 1)
        mask = jnp.logical_and(jnp.logical_and(kk >= ii, kk <= ii + BAND),
                               jnp.logical_or(kk >= BAND, n > 0))
        lane = lax.broadcasted_iota(jnp.int32, (BAND, 128), 1)
        stats = jnp.zeros((BAND, 128), F32)
        for h in range(HEADS):
            sl = slice(h * HEAD_DIM, (h + 1) * HEAD_DIM)
            k_cat = jnp.concatenate([kp_ref[:, sl], kc_ref[:, sl]], axis=0)
            v_cat = jnp.concatenate([vp_ref[:, sl], vc_ref[:, sl]], axis=0)
            s = jnp.where(mask, _dot_nt(q_ref[:, sl], k_cat) * scale, NEG_INF)
            m = jnp.max(s, axis=1, keepdims=True)
            p = jnp.exp(s - m)
            l = jnp.sum(p, axis=1, keepdims=True)
            o_ref[:, sl] = _dot(p.astype(BF16), v_cat) / l
            stats = jnp.where(lane == h, m, stats)
            stats = jnp.where(lane == HEADS + h, l, stats)
        st_ref[...] = stats

    blk = (BAND, ATT_WIDTH)
    return pl.pallas_call(
        body, name=f"attn_fwd_g{g}", grid=(r, nb),
        in_specs=[
            pl.BlockSpec(blk, lambda rho, n: (n, rho * 3)),
            pl.BlockSpec(blk, lambda rho, n: (jnp.maximum(n - 1, 0), rho * 3 + 1)),
            pl.BlockSpec(blk, lambda rho, n: (n, rho * 3 + 1)),
            pl.BlockSpec(blk, lambda rho, n: (jnp.maximum(n - 1, 0), rho * 3 + 2)),
            pl.BlockSpec(blk, lambda rho, n: (n, rho * 3 + 2)),
        ],
        out_specs=(pl.BlockSpec(blk, lambda rho, n: (n, rho)),
                   pl.BlockSpec((BAND, 128), lambda rho, n: (n, rho))),
        out_shape=(jax.ShapeDtypeStruct((L, r * ATT_WIDTH), F32),
                   jax.ShapeDtypeStruct((L, r * 128), F32)),
        compiler_params=_params(("parallel", "parallel")))(qv, qv, qv, qv, qv)


def _attn_combine(outs, stats, rest):
    S = rest.shape[0]
    tm = 512
    gatt_blk = 8

    def body(o0_ref, o1_ref, o2_ref, s0_ref, s1_ref, s2_ref, g_ref, o_ref, a_ref, lse_ref):
        o_refs = (o0_ref, o1_ref, o2_ref)
        st = (s0_ref[...], s1_ref[...], s2_ref[...])
        lane = lax.broadcasted_iota(jnp.int32, (tm, 128), 1)
        lse_out = jnp.zeros((tm, 128), F32)
        for h in range(HEADS):
            sl = slice(h * HEAD_DIM, (h + 1) * HEAD_DIM)
            ms = [s[:, h:h + 1] for s in st]
            ls = [s[:, HEADS + h:HEADS + h + 1] for s in st]
            m_all = jnp.maximum(jnp.maximum(ms[0], ms[1]), ms[2])
            ws = [l * jnp.exp(m - m_all) for m, l in zip(ms, ls)]
            den = ws[0] + ws[1] + ws[2]
            o = (ws[0] * o_refs[0][:, sl] + ws[1] * o_refs[1][:, sl] + ws[2] * o_refs[2][:, sl]) / den
            o_ref[:, sl] = o
            a_ref[:, sl] = (o * _silu(g_ref[:, sl])).astype(BF16)
            lse_out = jnp.where(lane == h, m_all + jnp.log(den), lse_out)
        lse_ref[...] = lse_out

    o_spec = pl.BlockSpec((tm, ATT_WIDTH), lambda i: (i, 0))
    s_spec = pl.BlockSpec((tm, 128), lambda i: (i, 0))
    return pl.pallas_call(
        body, name="attn_combine", grid=(S // tm,),
        in_specs=[o_spec, o_spec, o_spec, s_spec, s_spec, s_spec,
                  pl.BlockSpec((tm, ATT_WIDTH), lambda i: (i, gatt_blk))],
        out_specs=(o_spec, o_spec, s_spec),
        out_shape=(jax.ShapeDtypeStruct((S, ATT_WIDTH), F32), jax.ShapeDtypeStruct((S, ATT_WIDTH), BF16),
                   jax.ShapeDtypeStruct((S, 128), F32)),
        compiler_params=_params(("parallel",)))(*outs, *stats, rest)


def _attn_bwd(qkv, do, lse, dvec, g, r, hosted=None):
    S = qkv.shape[0]
    L = S // r
    nb = L // BAND
    scale = HEAD_DIM ** -0.5
    qv = qkv.reshape(L, r * 3 * ATT_WIDTH)
    dov = do.reshape(L, r * ATT_WIDTH)
    lsev = lse.reshape(L, r * 128)
    dvv = dvec.reshape(L, r * 128)

    def body(qc_ref, qn_ref, k_ref, v_ref, doc_ref, don_ref, lc_ref, ln_ref, dc_ref, dn_ref,
             out_ref, carry_ref):
        j = pl.program_id(1)

        @pl.when(j == 0)
        def _():
            carry_ref[...] = jnp.zeros_like(carry_ref)

        ii = lax.broadcasted_iota(jnp.int32, (2 * BAND, BAND), 0)
        kk = lax.broadcasted_iota(jnp.int32, (2 * BAND, BAND), 1)
        mask = jnp.logical_or(
            jnp.logical_and(ii < BAND, kk <= ii),
            jnp.logical_and(jnp.logical_and(ii >= BAND, kk >= ii - BAND), j < nb - 1))
        lse2 = jnp.concatenate([lc_ref[...], ln_ref[...]], axis=0)
        dvec2 = jnp.concatenate([dc_ref[...], dn_ref[...]], axis=0)
        for h in range(HEADS):
            sl = slice(h * HEAD_DIM, (h + 1) * HEAD_DIM)
            k = k_ref[:, sl]
            v = v_ref[:, sl]
            q = jnp.concatenate([qc_ref[:, sl], qn_ref[:, sl]], axis=0)
            do = jnp.concatenate([doc_ref[:, sl], don_ref[:, sl]], axis=0)
            p = jnp.where(mask, jnp.exp(_dot_nt(q, k) * scale - lse2[:, h:h + 1]), 0.0)
            ds = (p * (_dot_nt(do, v) - dvec2[:, h:h + 1])).astype(BF16)
            dq2 = _dot(ds, k) * scale
            out_ref[0, :, sl] = (carry_ref[:, sl] + dq2[:BAND]).astype(BF16)
            out_ref[1, :, sl] = (_dot_tn(ds, q) * scale).astype(BF16)
            out_ref[2, :, sl] = _dot_tn(p.astype(BF16), do).astype(BF16)
            carry_ref[:, sl] = dq2[BAND:]

    blk = (BAND, ATT_WIDTH)
    sblk = (BAND, 128)

    def nxt(j):
        return jnp.minimum(j + 1, nb - 1)

    (out,), got = _call(
        body, (qv, qv, qv, qv, dov, dov, lsev, lsev, dvv, dvv), name=f"attn_bwd_g{g}", grid=(r, nb),
        in_specs=[
            pl.BlockSpec(blk, lambda rho, j: (j, rho * 3)),
            pl.BlockSpec(blk, lambda rho, j: (nxt(j), rho * 3)),
            pl.BlockSpec(blk, lambda rho, j: (j, rho * 3 + 1)),
            pl.BlockSpec(blk, lambda rho, j: (j, rho * 3 + 2)),
            pl.BlockSpec(blk, lambda rho, j: (j, rho)),
            pl.BlockSpec(blk, lambda rho, j: (nxt(j), rho)),
            pl.BlockSpec(sblk, lambda rho, j: (j, rho)),
            pl.BlockSpec(sblk, lambda rho, j: (nxt(j), rho)),
            pl.BlockSpec(sblk, lambda rho, j: (j, rho)),
            pl.BlockSpec(sblk, lambda rho, j: (nxt(j), rho)),
        ],
        out_specs=(pl.BlockSpec((3, BAND, ATT_WIDTH), lambda rho, j: (0, j, rho)),),
        out_shape=(jax.ShapeDtypeStruct((3, L, r * ATT_WIDTH), BF16),),
        scratch_shapes=[pltpu.VMEM((BAND, ATT_WIDTH), F32)],
        hosted=hosted)
    return out.reshape(3, S, ATT_WIDTH), got


LRU_T = 256


def _shift_down(x, s, rows):
    return pltpu.roll(x, s, 0)


def _gate_matmuls(ucb, w_ref, bias):
    parts = [_dot(ucb[:, j * 128:(j + 1) * 128], w_ref[j]) for j in range(8)]
    return jnp.concatenate(parts, axis=1) + bias


def _conv_fwd(u, u_prev8, cw_ref):
    T = u.shape[0]
    ue = jnp.concatenate([u_prev8, u], axis=0)
    uc = cw_ref[4:5, :] + cw_ref[0:1, :] * u
    for j in range(1, CONV_WIDTH):
        uc = uc + cw_ref[j:j + 1, :] * pltpu.roll(ue, j, 0)[8:8 + T]
    return uc


def _lru_fwd(rest, cw, wr2, wi2, hosted=None):
    S = rest.shape[0]
    T = LRU_T

    def body(u_ref, cw_ref, wr_ref, wi_ref, h_ref, a_ref, ucar_ref, hcar_ref):
        c = pl.program_id(0)

        @pl.when(c == 0)
        def _():
            ucar_ref[...] = jnp.zeros_like(ucar_ref)
            hcar_ref[...] = jnp.zeros_like(hcar_ref)

        u = u_ref[...]
        uc = _conv_fwd(u, ucar_ref[...], cw_ref)
        ucar_ref[...] = u[T - 8:, :]
        ucb = uc.astype(BF16)
        r = jax.nn.sigmoid(_gate_matmuls(ucb, wr_ref, cw_ref[5:6, :]))
        ig = jax.nn.sigmoid(_gate_matmuls(ucb, wi_ref, cw_ref[6:7, :]))
        log_a = -LRU_C * r * _softplus_neg(cw_ref[7:8, :])
        a = jnp.exp(log_a)
        b = jnp.sqrt(_neg_expm1(2.0 * log_a)) * (ig * uc)
        a_ref[...] = a
        row = lax.broadcasted_iota(jnp.int32, (T, D_MODEL), 0)
        acc_a, acc_b = a, b
        s = 1
        while s < T:
            keep = row >= s
            b_sh = jnp.where(keep, pltpu.roll(acc_b, s, 0), 0.0)
            a_sh = jnp.where(keep, pltpu.roll(acc_a, s, 0), 1.0)
            acc_b = acc_a * b_sh + acc_b
            acc_a = acc_a * a_sh
            s *= 2
        h = acc_b + acc_a * hcar_ref[0:1, :]
        h_ref[...] = h
        hcar_ref[...] = jnp.broadcast_to(h[T - 1:T, :], (8, D_MODEL))

    row_spec = pl.BlockSpec((T, D_MODEL), lambda c: (c, 0))
    return _call(
        body, (rest, cw, wr2, wi2), name="lru_fwd", grid=(S // T,),
        in_specs=[row_spec, pl.BlockSpec((8, D_MODEL), lambda c: (0, 0)),
                  pl.BlockSpec((8, 128, 128), lambda c: (0, 0, 0)),
                  pl.BlockSpec((8, 128, 128), lambda c: (0, 0, 0))],
        out_specs=(row_spec, row_spec),
        out_shape=(jax.ShapeDtypeStruct((S, D_MODEL), F32), jax.ShapeDtypeStruct((S, D_MODEL), F32)),
        scratch_shapes=[pltpu.VMEM((8, D_MODEL), F32), pltpu.VMEM((8, D_MODEL), F32)],
        hosted=hosted)


def _lru_bwd(dhl, hl, a_all, rest, cw, wr2, wi2, dproj, hosted=None):
    S = rest.shape[0]
    T = LRU_T
    nc = S // T

    def body(dh_ref, h_ref, hp_ref, a_ref, u_ref, up_ref, cw_ref, wr_ref, wi_ref, _alias,
             du_ref, acc_ref, gwr_ref, gwi_ref, gcar_ref, acar_ref, dcar_ref):
        step = pl.program_id(0)
        c = nc - 1 - step

        @pl.when(step == 0)
        def _():
            acc_ref[...] = jnp.zeros_like(acc_ref)
            gwr_ref[...] = jnp.zeros_like(gwr_ref)
            gwi_ref[...] = jnp.zeros_like(gwi_ref)
            gcar_ref[...] = jnp.zeros_like(gcar_ref)
            acar_ref[...] = jnp.zeros_like(acar_ref)
            dcar_ref[...] = jnp.zeros_like(dcar_ref)

        row = lax.broadcasted_iota(jnp.int32, (T, D_MODEL), 0)
        u = u_ref[...]
        u_prev = jnp.where(c > 0, up_ref[...], 0.0)
        h_prev8 = jnp.where(c > 0, hp_ref[...], 0.0)
        a = a_ref[...]
        h = h_ref[...]
        uc = _conv_fwd(u, u_prev, cw_ref)
        ucb = uc.astype(BF16)
        r = jax.nn.sigmoid(_gate_matmuls(ucb, wr_ref, cw_ref[5:6, :]))
        ig = jax.nn.sigmoid(_gate_matmuls(ucb, wi_ref, cw_ref[6:7, :]))
        sp = _softplus_neg(cw_ref[7:8, :])
        log_a = -LRU_C * r * sp
        mult = jnp.sqrt(_neg_expm1(2.0 * log_a))
        a_next = jnp.where(row < T - 1, pltpu.roll(a, T - 1, 0), acar_ref[0:1, :])
        acc_a, acc_b = a_next, dh_ref[...]
        s = 1
        while s < T:
            keep = row < T - s
            b_sh = jnp.where(keep, pltpu.roll(acc_b, T - s, 0), 0.0)
            a_sh = jnp.where(keep, pltpu.roll(acc_a, T - s, 0), 1.0)
            acc_b = acc_a * b_sh + acc_b
            acc_a = acc_a * a_sh
            s *= 2
        G = acc_b + acc_a * gcar_ref[0:1, :]
        gcar_ref[...] = jnp.broadcast_to(G[0:1, :], (8, D_MODEL))
        acar_ref[...] = jnp.broadcast_to(a[0:1, :], (8, D_MODEL))
        he = jnp.concatenate([h_prev8, h], axis=0)
        h_before = pltpu.roll(he, 1, 0)[8:8 + T]
        d_a = G * h_before
        d_mult = G * (ig * uc)
        d_ig = G * (mult * uc)
        duc = G * (mult * ig)
        d_log_a = d_a * a - d_mult * (a * a) / mult
        d_r = d_log_a * (-LRU_C * sp)
        d_sp = jnp.sum(d_log_a * (-LRU_C * r), axis=0, keepdims=True)
        dpre_r = d_r * r * (1.0 - r)
        dpre_i = d_ig * ig * (1.0 - ig)
        dprb = dpre_r.astype(BF16)
        dpib = dpre_i.astype(BF16)
        back = []
        for j in range(8):
            sl = slice(j * 128, (j + 1) * 128)
            back.append(_dot_nt(dprb[:, sl], wr_ref[j]) + _dot_nt(dpib[:, sl], wi_ref[j]))
            gwr_ref[j] += _dot_tn(ucb[:, sl], dprb[:, sl])
            gwi_ref[j] += _dot_tn(ucb[:, sl], dpib[:, sl])
        duc = duc + jnp.concatenate(back, axis=1)
        de = jnp.concatenate([duc, dcar_ref[...]], axis=0)
        du = cw_ref[0:1, :] * duc
        ue = jnp.concatenate([u_prev, u], axis=0)
        acc_ref[0:1, :] += jnp.sum(duc * u, axis=0, keepdims=True)
        for j in range(1, CONV_WIDTH):
            du = du + cw_ref[j:j + 1, :] * pltpu.roll(de, T + 8 - j, 0)[0:T]
            acc_ref[j:j + 1, :] += jnp.sum(duc * pltpu.roll(ue, j, 0)[8:8 + T], axis=0, keepdims=True)
        dcar_ref[...] = duc[0:8, :]
        acc_ref[4:5, :] += jnp.sum(duc, axis=0, keepdims=True)
        acc_ref[5:6, :] += jnp.sum(dpre_r, axis=0, keepdims=True)
        acc_ref[6:7, :] += jnp.sum(dpre_i, axis=0, keepdims=True)
        acc_ref[7:8, :] += d_sp
        du_ref[0] = du[:, 0:ATT_WIDTH].astype(BF16)
        du_ref[1] = du[:, ATT_WIDTH:].astype(BF16)

    def rev(step):
        return nc - 1 - step

    def prev8(step):
        return jnp.maximum(rev(step) * (T // 8) - 1, 0)

    row_spec = pl.BlockSpec((T, D_MODEL), lambda s: (rev(s), 0))
    prev_spec = pl.BlockSpec((8, D_MODEL), lambda s: (prev8(s), 0))
    vec_spec = pl.BlockSpec((8, D_MODEL), lambda s: (0, 0))
    w_spec = pl.BlockSpec((8, 128, 128), lambda s: (0, 0, 0))
    return _call(
        body, (dhl, hl, hl, a_all, rest, rest, cw, wr2, wi2, dproj), name="lru_bwd", grid=(nc,),
        in_specs=[row_spec, row_spec, prev_spec, row_spec, row_spec, prev_spec, vec_spec, w_spec, w_spec,
                  ANY_SPEC],
        out_specs=(pl.BlockSpec((2, T, ATT_WIDTH), lambda s: (4, rev(s), 0)), vec_spec, w_spec, w_spec),
        out_shape=(jax.ShapeDtypeStruct(dproj.shape, BF16), jax.ShapeDtypeStruct((8, D_MODEL), F32),
                   jax.ShapeDtypeStruct((8, 128, 128), F32), jax.ShapeDtypeStruct((8, 128, 128), F32)),
        scratch_shapes=[pltpu.VMEM((8, D_MODEL), F32), pltpu.VMEM((8, D_MODEL), F32),
                        pltpu.VMEM((8, D_MODEL), F32)],
        aliases={9: 0}, hosted=hosted)


OUT_TM = 256


def _out_fwd(a_gated, hl, rest, x, vec, w_pa, w_pb, w_o):
    S = x.shape[0]
    tm = OUT_TM

    def body(ag_ref, hl_ref, gl_ref, ma_ref, mb_ref, x_ref, v_ref, wpa_ref, wpb_ref, wo_ref,
             xn_ref, mix_ref, ya_ref, yb_ref, bg_ref):
        bg = (hl_ref[...] * _silu(gl_ref[...])).astype(BF16)
        ya = _dot(ag_ref[...], wpa_ref[...])
        yb = _dot(bg, wpb_ref[...])
        mix = (jax.nn.sigmoid(ma_ref[...]) * ya + jax.nn.sigmoid(mb_ref[...]) * yb).astype(BF16)
        out = _dot(mix, wo_ref[...])
        rstd = lax.rsqrt(jnp.mean(out * out, axis=-1, keepdims=True) + NORM_EPS)
        xn_ref[...] = x_ref[...] + v_ref[1:2, :] * ((out * rstd) * v_ref[0:1, :])
        mix_ref[...] = mix
        ya_ref[...] = ya.astype(BF16)
        yb_ref[...] = yb.astype(BF16)
        bg_ref[...] = bg

    def col(j):
        return pl.BlockSpec((tm, D_MODEL), lambda i: (i, j))

    def whole(shape):
        return pl.BlockSpec(shape, lambda i: (0, 0))

    row = col(0)
    bf = jax.ShapeDtypeStruct((S, D_MODEL), BF16)
    return pl.pallas_call(
        body, name="out_fwd", grid=(S // tm,),
        in_specs=[pl.BlockSpec((tm, ATT_WIDTH), lambda i: (i, 0)), row, col(1), col(2), col(3), row,
                  whole((8, D_MODEL)), whole((ATT_WIDTH, D_MODEL)), whole((D_MODEL, D_MODEL)),
                  whole((D_MODEL, D_MODEL))],
        out_specs=(row, row, row, row, row),
        out_shape=(jax.ShapeDtypeStruct((S, D_MODEL), F32), bf, bf, bf, bf),
        compiler_params=_params(("parallel",)))(a_gated, hl, rest, rest, rest, x, vec, w_pa, w_pb, w_o)


def _out_bwd(dxn, mix, ya, yb, hl, rest, o_att, vec, w_pa, w_pb, w_o, hosted=None):
    S = dxn.shape[0]
    tm = OUT_TM

    def body(dxn_ref, mix_ref, ya_ref, yb_ref, hl_ref, gl_ref, ma_ref, mb_ref, ga_ref, o_ref, v_ref,
             wpa_ref, wpb_ref, wo_ref,
             dout_ref, dya_ref, dyb_ref, do_ref, dv_ref, dhl_ref, dp_ref, acc_ref):
        i = pl.program_id(0)

        @pl.when(i == 0)
        def _():
            acc_ref[...] = jnp.zeros_like(acc_ref)

        g_post = v_ref[0:1, :]
        gate = v_ref[1:2, :]
        dxn_v = dxn_ref[...]
        out = _dot(mix_ref[...], wo_ref[...])
        rstd = lax.rsqrt(jnp.mean(out * out, axis=-1, keepdims=True) + NORM_EPS)
        nrm = out * rstd
        acc_ref[0:1, :] += jnp.sum(dxn_v * (nrm * g_post), axis=0, keepdims=True)
        acc_ref[1:2, :] += jnp.sum(dxn_v * gate * nrm, axis=0, keepdims=True)
        dn = dxn_v * (gate * g_post)
        dout = (rstd * (dn - nrm * jnp.mean(dn * nrm, axis=-1, keepdims=True))).astype(BF16)
        dout_ref[...] = dout
        dmix = _dot_nt(dout, wo_ref[...])
        sa = jax.nn.sigmoid(ma_ref[...])
        sb = jax.nn.sigmoid(mb_ref[...])
        dya = (dmix * sa).astype(BF16)
        dyb = (dmix * sb).astype(BF16)
        dya_ref[...] = dya
        dyb_ref[...] = dyb
        dma = dmix * ya_ref[...].astype(F32) * (sa * (1.0 - sa))
        dmb = dmix * yb_ref[...].astype(F32) * (sb * (1.0 - sb))
        d_ag = _dot_nt(dya, wpa_ref[...])
        d_bg = _dot_nt(dyb, wpb_ref[...])
        gl = gl_ref[...]
        hl_v = hl_ref[...]
        dhl_ref[...] = d_bg * _silu(gl)
        dgl = d_bg * hl_v * _dsilu(gl)
        ga = ga_ref[...]
        o = o_ref[...]
        do = d_ag * _silu(ga)
        dga = d_ag * o * _dsilu(ga)
        do_ref[...] = do.astype(BF16)
        lane = lax.broadcasted_iota(jnp.int32, (tm, 128), 1)
        dvec = jnp.zeros((tm, 128), F32)
        for h in range(HEADS):
            sl = slice(h * HEAD_DIM, (h + 1) * HEAD_DIM)
            dvec = jnp.where(lane == h, jnp.sum(do[:, sl] * o[:, sl], axis=1, keepdims=True), dvec)
        dv_ref[...] = dvec
        dp_ref[0] = dgl[:, 0:ATT_WIDTH].astype(BF16)
        dp_ref[1] = dgl[:, ATT_WIDTH:].astype(BF16)
        dp_ref[2] = dma[:, 0:ATT_WIDTH].astype(BF16)
        dp_ref[3] = dma[:, ATT_WIDTH:].astype(BF16)
        dp_ref[4] = dmb[:, 0:ATT_WIDTH].astype(BF16)
        dp_ref[5] = dmb[:, ATT_WIDTH:].astype(BF16)
        dp_ref[6] = dga.astype(BF16)

    def col(j):
        return pl.BlockSpec((tm, D_MODEL), lambda i: (i, j))

    def whole(shape):
        return pl.BlockSpec(shape, lambda i: (0, 0))

    row = col(0)
    att = pl.BlockSpec((tm, ATT_WIDTH), lambda i: (i, 0))
    bf = jax.ShapeDtypeStruct((S, D_MODEL), BF16)
    return _call(
        body, (dxn, mix, ya, yb, hl, rest, rest, rest, rest, o_att, vec, w_pa, w_pb, w_o),
        name="out_bwd", grid=(S // tm,),
        in_specs=[row, row, row, row, row, col(1), col(2), col(3),
                  pl.BlockSpec((tm, ATT_WIDTH), lambda i: (i, 8)), att,
                  whole((8, D_MODEL)), whole((ATT_WIDTH, D_MODEL)), whole((D_MODEL, D_MODEL)),
                  whole((D_MODEL, D_MODEL))],
        out_specs=(row, row, row, att, pl.BlockSpec((tm, 128), lambda i: (i, 0)), row,
                   pl.BlockSpec((7, tm, ATT_WIDTH), lambda i: (0, i, 0)), whole((8, D_MODEL))),
        out_shape=(bf, bf, bf, jax.ShapeDtypeStruct((S, ATT_WIDTH), BF16),
                   jax.ShapeDtypeStruct((S, 128), F32), jax.ShapeDtypeStruct((S, D_MODEL), F32),
                   jax.ShapeDtypeStruct((DREST_CHUNKS, S, ATT_WIDTH), BF16),
                   jax.ShapeDtypeStruct((8, D_MODEL), F32)),
        hosted=hosted)


def _loss_head(y, target):
    S = y.shape[0]
    tm = 512

    def body(y_ref, t_ref, dy_ref, acc_ref):
        i = pl.program_id(0)

        @pl.when(i == 0)
        def _():
            acc_ref[...] = jnp.zeros_like(acc_ref)

        err = y_ref[...] - t_ref[...]
        dy_ref[...] = err * (1.0 / D_MODEL)
        part = jnp.sum(jnp.sum(err * err, axis=1, keepdims=True), axis=0, keepdims=True)
        acc_ref[...] += jnp.broadcast_to(part, acc_ref.shape)

    row = pl.BlockSpec((tm, D_MODEL), lambda i: (i, 0))
    return pl.pallas_call(
        body, name="loss_head", grid=(S // tm,),
        in_specs=[row, row],
        out_specs=(row, pl.BlockSpec((8, 128), lambda i: (0, 0))),
        out_shape=(jax.ShapeDtypeStruct((S, D_MODEL), F32), jax.ShapeDtypeStruct((8, 128), F32)),
        compiler_params=_params(("arbitrary",)))(y, target)


def _adamw(gsrcs, w, m, v, n_stack, name, rows=None):
    n_layers, R, C = w.shape
    assert len(gsrcs) == n_layers
    budget = 96 * 1024
    tr = rows
    if tr is None:
        tr = R
        while tr * C > budget and tr % 16 == 0:
            tr //= 2
    assert R % tr == 0 and (tr % 8 == 0 or tr == R)
    c1 = 1.0 - ADAM_B1 ** ADAM_STEP
    c2 = 1.0 - ADAM_B2 ** ADAM_STEP

    def body(*refs):
        g_refs = refs[:n_layers]
        w_ref, m_ref, v_ref, go_ref, d_ref, mo_ref, vo_ref = refs[n_layers:]

        def update(g_ref):
            if n_stack:
                g = g_ref[0].astype(F32)
                for s in range(1, n_stack):
                    g = g + g_ref[s].astype(F32)
            else:
                g = g_ref[...]
            m_new = ADAM_B1 * m_ref[...] + (1.0 - ADAM_B1) * g
            v_new = ADAM_B2 * v_ref[...] + (1.0 - ADAM_B2) * (g * g)
            m_hat = m_new / c1
            v_hat = v_new / c2
            go_ref[...] = g
            d_ref[...] = -ADAM_LR * (m_hat / (jnp.sqrt(v_hat) + ADAM_EPS) + ADAM_WD * w_ref[...])
            mo_ref[...] = m_new
            vo_ref[...] = v_new

        for layer in range(n_layers):
            pl.when(pl.program_id(0) == layer)(functools.partial(update, g_refs[layer]))

    def g_spec(layer):
        def rows_of(l, i):
            return jnp.where(l == layer, i, 0)
        if n_stack:
            return pl.BlockSpec((n_stack, tr, C), lambda l, i: (0, rows_of(l, i), 0))
        return pl.BlockSpec((tr, C), lambda l, i: (rows_of(l, i), 0))

    blk = pl.BlockSpec((None, tr, C), lambda l, i: (l, i, 0))
    shp = jax.ShapeDtypeStruct((n_layers, R, C), F32)
    return pl.pallas_call(
        body, name=name, grid=(n_layers, R // tr),
        in_specs=[g_spec(layer) for layer in range(n_layers)] + [blk, blk, blk],
        out_specs=(blk, blk, blk, blk), out_shape=(shp, shp, shp, shp),
        compiler_params=_params(("arbitrary", "arbitrary")))(*gsrcs, w, m, v)


def _pair_blocks(w):
    w = w.reshape(8, 2, 64, 64)
    z = jnp.zeros((8, 64, 64), w.dtype)
    top = jnp.concatenate([w[:, 0], z], axis=2)
    bot = jnp.concatenate([z, w[:, 1]], axis=2)
    return jnp.concatenate([top, bot], axis=1).astype(BF16)


def _unpair_blocks(g):
    return jnp.stack([g[:, :64, :64], g[:, 64:, 64:]], axis=1).reshape(16, 64, 64)


def _layer_params(w_in_g, w_pa_g, w_pb_g, w_o_g, conv_w_full, conv_b, b_rg, b_ig, lru_lambda, w_rg, w_ig,
                  g_pre, g_post):
    w_full = jnp.transpose(w_in_g, (1, 0, 2)).reshape(D_MODEL, N_CHUNKS, ATT_WIDTH)
    w_int = w_full[:, jnp.array(CHUNK_PERM)].reshape(D_MODEL, N_CHUNKS * ATT_WIDTH)
    cw = jnp.concatenate([conv_w_full, conv_b[None], b_rg[None], b_ig[None], lru_lambda[None]], axis=0)
    return dict(
        w_int=w_int,
        w_pa=jnp.transpose(w_pa_g, (1, 0, 2)).reshape(ATT_WIDTH, D_MODEL),
        w_pb=w_pb_g.reshape(D_MODEL, D_MODEL), w_o=w_o_g.reshape(D_MODEL, D_MODEL),
        cw=cw, wr2=_pair_blocks(w_rg), wi2=_pair_blocks(w_ig),
        g_pre=g_pre, g_post=g_post, lru_lambda=lru_lambda)


def _layer_fwd(x, mod, p, host_proj=None, host_lru=None):
    zeros = jnp.zeros((5, D_MODEL), F32)
    vec_pre = jnp.concatenate([p["g_pre"][None], mod[1:2], mod[0:1], zeros], axis=0)
    (h_t, q0, q1, q2, rest), got_proj = _norm_proj_fwd(x, vec_pre, p["w_int"], hosted=host_proj)
    S = x.shape[0]
    qkv, outs, stats = [q0, q1, q2], [], []
    for g, r in enumerate(DILATIONS):
        o_g, st_g = _attn_fwd(qkv[g], g, r)
        outs.append(o_g.reshape(S, ATT_WIDTH))
        stats.append(st_g.reshape(S, 128))
    o_att, a_gated, lse = _attn_combine(outs, stats, rest)
    (hl, a_dec), got_lru = _lru_fwd(rest, p["cw"], p["wr2"], p["wi2"], hosted=host_lru)
    vec_post = jnp.concatenate([p["g_post"][None], mod[2:3], jnp.zeros((6, D_MODEL), F32)], axis=0)
    x_new, mix, ya, yb, b_gated = _out_fwd(a_gated, hl, rest, x, vec_post, p["w_pa"], p["w_pb"], p["w_o"])
    saved = dict(x=x, h_t=h_t, qkv=qkv, rest=rest, o_att=o_att, a_gated=a_gated, lse=lse, hl=hl, a_dec=a_dec,
                 mix=mix, ya=ya, yb=yb, b_gated=b_gated, vec_pre=vec_pre, vec_post=vec_post)
    return x_new, saved, got_proj, got_lru


def _mats_sources(g_pa, g_pb, g_o):
    return [jnp.transpose(g_pa.astype(BF16).reshape(ATT_WIDTH, N_DEV, 128), (1, 0, 2)),
            g_pb.astype(BF16).reshape(N_DEV, 128, D_MODEL), g_o.astype(BF16).reshape(N_DEV, 128, D_MODEL)]


def _layer_bwd(dxn, sv, p, upper=None, host_own=False):
    n_in = N_CHUNKS * ATT_WIDTH // N_DEV
    a2a3 = ["a2a"] * 3
    (dout, dya, dyb, do, dvec, dhl, drest, acc_out), got_up_mats = _out_bwd(
        dxn, sv["mix"], sv["ya"], sv["yb"], sv["hl"], sv["rest"], sv["o_att"], sv["vec_post"],
        p["w_pa"], p["w_pb"], p["w_o"], hosted=(upper["mats"], a2a3) if upper else None)
    mats = _mats_sources(_wgrad_tn(sv["a_gated"], dya, "wgrad_pa"), _wgrad_tn(sv["b_gated"], dyb, "wgrad_pb"),
                         _wgrad_tn(sv["mix"], dout, "wgrad_o"))
    (drest, acc_lru, gwr2, gwi2), got_up_in = _lru_bwd(
        dhl, sv["hl"], sv["a_dec"], sv["rest"], p["cw"], p["wr2"], p["wi2"], drest,
        hosted=([upper["w_in"]], ["a2a"]) if upper else None)
    dqkv, got_mats = [], []
    for g, r in enumerate(DILATIONS):
        d_g, got = _attn_bwd(sv["qkv"][g], do, sv["lse"], dvec, g, r,
                             hosted=(mats, a2a3) if (host_own and g == 0) else None)
        dqkv.append(d_g)
        got_mats = got_mats or got
    parts = [_proj_wgrad_part(sv["h_t"], dqkv[g], 3, 0, f"proj_wgrad_g{g}") for g in range(3)]
    parts.append(_proj_wgrad_part(sv["h_t"], drest, 2, 4, "proj_wgrad_u"))
    parts.append(_proj_wgrad_part(sv["h_t"], drest, 7, 0, "proj_wgrad_rest"))
    gw_int = jnp.concatenate(parts, axis=1).astype(BF16).reshape(D_MODEL, N_CHUNKS, ATT_WIDTH)
    inv_perm = [CHUNK_PERM.index(k) for k in range(N_CHUNKS)]
    gw_in = jnp.transpose(gw_int[:, jnp.array(inv_perm)].reshape(D_MODEL, N_DEV, n_in), (1, 0, 2))
    (dx, acc_pre), got_in = _proj_dgrad_norm_bwd(
        dqkv, drest, p["w_int"], sv["x"], dxn, sv["vec_pre"],
        hosted=([gw_in], ["a2a"]) if host_own else None)
    small = dict(
        dmod=jnp.concatenate([acc_pre[0], acc_pre[1], acc_out[0]]),
        g_pre=acc_pre[2], g_post=acc_out[1],
        conv_w=acc_lru[0:4], conv_b=acc_lru[4], b_rg=acc_lru[5], b_ig=acc_lru[6],
        lru_lambda=acc_lru[7] * (-jax.nn.sigmoid(-p["lru_lambda"])),
        w_rg=_unpair_blocks(gwr2), w_ig=_unpair_blocks(gwi2))
    comm = dict(sources=dict(mats=mats, w_in=gw_in), own=dict(mats=got_mats, w_in=got_in),
                upper=dict(mats=got_up_mats, w_in=got_up_in))
    return dx, small, comm


SMALL_NAMES = ("b_mod", "g_pre", "conv_b", "w_rg", "b_rg", "w_ig", "b_ig", "lru_lambda", "g_post")


def kernel(x, c, w_mod, b_mod, g_pre, w_in, conv_w, conv_b, w_rg, b_rg, w_ig, b_ig, lru_lambda, w_pa, w_pb, w_o, g_post, loss_target, m_w_mod, m_b_mod, m_g_pre, m_w_in, m_conv_w, m_conv_b, m_w_rg, m_b_rg, m_w_ig, m_b_ig, m_lru_lambda, m_w_pa, m_w_pb, m_w_o, m_g_post, v_w_mod, v_b_mod, v_g_pre, v_w_in, v_conv_w, v_conv_b, v_w_rg, v_b_rg, v_w_ig, v_b_ig, v_lru_lambda, v_w_pa, v_w_pb, v_w_o, v_g_post):
    W = dict(w_mod=w_mod, b_mod=b_mod, g_pre=g_pre, w_in=w_in, conv_w=conv_w, conv_b=conv_b, w_rg=w_rg,
             b_rg=b_rg, w_ig=w_ig, b_ig=b_ig, lru_lambda=lru_lambda, w_pa=w_pa, w_pb=w_pb, w_o=w_o,
             g_post=g_post)
    M = dict(w_mod=m_w_mod, b_mod=m_b_mod, g_pre=m_g_pre, w_in=m_w_in, conv_w=m_conv_w, conv_b=m_conv_b,
             w_rg=m_w_rg, b_rg=m_b_rg, w_ig=m_w_ig, b_ig=m_b_ig, lru_lambda=m_lru_lambda, w_pa=m_w_pa,
             w_pb=m_w_pb, w_o=m_w_o, g_post=m_g_post)
    V = dict(w_mod=v_w_mod, b_mod=v_b_mod, g_pre=v_g_pre, w_in=v_w_in, conv_w=v_conv_w, conv_b=v_conv_b,
             w_rg=v_w_rg, b_rg=v_b_rg, w_ig=v_w_ig, b_ig=v_b_ig, lru_lambda=v_lru_lambda, w_pa=v_w_pa,
             w_pb=v_w_pb, w_o=v_w_o, g_post=v_g_post)
    S = x.shape[1]
    me = 4 * lax.axis_index("x") + 2 * lax.axis_index("y") + lax.axis_index("c")
    n_mod = w_mod.shape[2]
    n_in = w_in.shape[2]
    n_conv = conv_w.shape[2]

    c_rows = jnp.broadcast_to(c, (8, D_MODEL))
    w_in_b, w_pa_b, w_pb_b, w_o_b = (t.astype(BF16) for t in (w_in, w_pa, w_pb, w_o))
    g_c, g_win0, g_wpa0, g_wpb0, g_wo0, g_cw = _exchange(
        [c_rows, w_in_b[0], w_pa_b[0], w_pb_b[0], w_o_b[0], conv_w], ["ag"] * 6, "gather_weights")
    c_all = g_c[:, 0, :]
    c_pad = jnp.concatenate([c_all, jnp.zeros((8, D_MODEL), F32)], axis=0)
    conv_w_full = jnp.transpose(g_cw, (1, 2, 0, 3)).reshape(2, CONV_WIDTH, D_MODEL)

    mod_cols = _mod_fwd(c_pad, w_mod)
    mod_src = jnp.transpose(mod_cols[:, :8], (1, 0, 2))
    mod_src = jnp.concatenate([mod_src, jnp.zeros((8, 6, n_mod), F32)], axis=1)
    (mod_got,) = _exchange([mod_src], ["a2a"], "scatter_mod")
    mod = jnp.transpose(mod_got[:, :2], (1, 0, 2)).reshape(2, 3 * D_MODEL) + b_mod

    def layer_params(layer, w_in_g, w_pa_g, w_pb_g, w_o_g):
        return _layer_params(w_in_g, w_pa_g, w_pb_g, w_o_g, conv_w_full[layer], conv_b[layer], b_rg[layer],
                             b_ig[layer], lru_lambda[layer], w_rg[layer], w_ig[layer], g_pre[layer],
                             g_post[layer])

    layers = [layer_params(0, g_win0, g_wpa0, g_wpb0, g_wo0), None]
    act, sv0, (g_win1,), (g_wpa1, g_wpb1, g_wo1) = _layer_fwd(
        x[0], mod[0].reshape(3, D_MODEL), layers[0],
        host_proj=([w_in_b[1]], ["ag"]), host_lru=([w_pa_b[1], w_pb_b[1], w_o_b[1]], ["ag"] * 3))
    layers[1] = layer_params(1, g_win1, g_wpa1, g_wpb1, g_wo1)
    act, sv1, _, _ = _layer_fwd(act, mod[1].reshape(3, D_MODEL), layers[1])
    dy, loss_acc = _loss_head(act, loss_target[0])
    loss = lax.psum(0.5 * loss_acc[0, 0] / D_MODEL, MESH_AXES)

    dy, small1, comm1 = _layer_bwd(dy, sv1, layers[1])
    dy, small0, comm0 = _layer_bwd(dy, sv0, layers[0], upper=comm1["sources"], host_own=True)
    grad_x = dy[None]
    r_pa, r_pb, r_o = ([comm0["own"]["mats"][k], comm0["upper"]["mats"][k]] for k in range(3))
    r_in = [comm0["own"]["w_in"][0], comm0["upper"]["w_in"][0]]
    grads = [small0, small1]

    def stack2(name):
        return jnp.stack([grads[0][name], grads[1][name]], axis=0)

    dmod = stack2("dmod")
    dmod_src = jnp.transpose(dmod.reshape(2, N_DEV, n_mod), (1, 0, 2))
    dmod_src = jnp.concatenate([dmod_src, jnp.zeros((8, 6, n_mod), F32)], axis=1)
    small = {"b_mod": dmod}
    for name in SMALL_NAMES[1:]:
        small[name] = stack2(name)
    small_vec = jnp.concatenate([small[n].reshape(-1) for n in SMALL_NAMES] + [stack2("conv_w").reshape(-1)])
    n_small = small_vec.shape[0]
    small_vec = small_vec.reshape(n_small // 128, 128)
    r_dmod, r_small = _exchange([dmod_src, small_vec], ["a2a", "ag"], "exchange_grads")

    res = {}
    res["w_in"] = _adamw(r_in, w_in, m_w_in, v_w_in, 8, "adamw_w_in")
    res["w_pa"] = _adamw(r_pa, w_pa, m_w_pa, v_w_pa, 8, "adamw_w_pa")
    res["w_pb"] = _adamw(r_pb, w_pb, m_w_pb, v_w_pb, 8, "adamw_w_pb")
    res["w_o"] = _adamw(r_o, w_o, m_w_o, v_w_o, 8, "adamw_w_o")
    dmod_all = jnp.transpose(r_dmod[:, :2], (1, 0, 2))
    gw_mod = _mod_wgrad(jnp.transpose(c_all), dmod_all)
    res["w_mod"] = _adamw([gw_mod[0], gw_mod[1]], w_mod, m_w_mod, v_w_mod, 0, "adamw_w_mod")
    n_rep = sum(W[n].size for n in SMALL_NAMES)
    rep_rows = n_rep // 128

    def flat_rep(src):
        return jnp.concatenate([src[n].reshape(-1) for n in SMALL_NAMES]).reshape(1, rep_rows, 128)

    rep = _adamw([r_small], flat_rep(W), flat_rep(M), flat_rep(V), 8, "adamw_small", rows=rep_rows // 2)
    off = 0
    for name in SMALL_NAMES:
        size = W[name].size
        res[name] = tuple(t.reshape(-1)[off:off + size].reshape(W[name].shape) for t in rep)
        off += size
    conv_stack = r_small[:, rep_rows:].reshape(8, 2, CONV_WIDTH, D_MODEL)
    conv_stack = lax.dynamic_slice_in_dim(conv_stack, me * n_conv, n_conv, axis=3).reshape(8, 8, n_conv)
    res["conv_w"] = _adamw([conv_stack], conv_w.reshape(1, 8, n_conv), m_conv_w.reshape(1, 8, n_conv),
                           v_conv_w.reshape(1, 8, n_conv), 8, "adamw_conv_w")

    names = ("w_mod", "b_mod", "g_pre", "w_in", "conv_w", "conv_b", "w_rg", "b_rg", "w_ig", "b_ig",
             "lru_lambda", "w_pa", "w_pb", "w_o", "g_post")
    outs = [loss, grad_x]
    for k in range(4):
        outs.extend(res[n][k].reshape(W[n].shape) for n in names)
    return tuple(outs)
```

```python
import functools

import jax
import jax.numpy as jnp
from jax import lax
from jax.experimental import pallas as pl
from jax.experimental.pallas import tpu as pltpu

F32 = jnp.float32
BF16 = jnp.bfloat16

N_DEV = 8
D_MODEL = 1024
HEAD_DIM = 128
HEADS = 4
ATT_WIDTH = HEADS * HEAD_DIM
DILATIONS = (1, 4, 16)
BAND = 128
N_CHUNKS = 18
QKV_CHUNKS = 9
CONV_WIDTH = 4
LRU_C = 8.0
NORM_EPS = 1e-6
NEG_INF = -1e30
ADAM_LR = 0.001
ADAM_B1 = 0.9
ADAM_B2 = 0.999
ADAM_EPS = 1e-08
ADAM_WD = 0.01
ADAM_STEP = 10
CHUNK_PERM = (0, 3, 6, 1, 4, 7, 2, 5, 8, 10, 11, 12, 13, 14, 15, 16, 17, 9)
DREST_CHUNKS = 10
VMEM_LIMIT = 56 * 1024 * 1024
MESH_AXES = ("x", "y", "c")


def _params(sem=None):
    return pltpu.CompilerParams(dimension_semantics=sem, vmem_limit_bytes=VMEM_LIMIT)


def _silu(x):
    return x * jax.nn.sigmoid(x)


def _dsilu(x):
    s = jax.nn.sigmoid(x)
    return s * (1.0 + x * (1.0 - s))


def _neg_expm1(x):
    series = -x * (1.0 + x * (0.5 + x * (1.0 / 6.0 + x * (1.0 / 24.0 + x * (1.0 / 120.0)))))
    return jnp.where(x > -0.05, series, 1.0 - jnp.exp(x))


def _softplus_neg(lam):
    z = jnp.exp(-jnp.abs(lam))
    small = z * (1.0 - z * (0.5 - z * (1.0 / 3.0 - z * 0.25)))
    log1p_z = jnp.where(z < 1e-2, small, jnp.log(1.0 + z))
    return jnp.maximum(-lam, 0.0) + log1p_z


def _dot(a, b):
    return jnp.dot(a, b, preferred_element_type=F32)


def _dot_nt(a, b):
    return lax.dot_general(a, b, (((1,), (1,)), ((), ())), preferred_element_type=F32)


def _dot_tn(a, b):
    return lax.dot_general(a, b, (((0,), (0,)), ((), ())), preferred_element_type=F32)


ANY_SPEC = pl.BlockSpec(memory_space=pl.ANY)


def _exchange_shapes(arrs, modes):
    return [jax.ShapeDtypeStruct(((N_DEV,) + a.shape) if mode == "ag" else a.shape, a.dtype)
            for a, mode in zip(arrs, modes)]


def _exchange_sems(n):
    return [pltpu.SemaphoreType.DMA((7 * n,)), pltpu.SemaphoreType.DMA((7 * n,)),
            pltpu.SemaphoreType.DMA((n,))]


def _exchange_copies(ins, outs, modes, sems):
    n = len(ins)
    send_sems, recv_sems, local_sems = sems
    x, y, c = lax.axis_index("x"), lax.axis_index("y"), lax.axis_index("c")
    me = 4 * x + 2 * y + c

    def src_for(a, dev):
        return ins[a] if modes[a] == "ag" else ins[a].at[dev]

    local = [pltpu.make_async_copy(src_for(a, me), outs[a].at[me], local_sems.at[a]) for a in range(n)]
    sends, arrivals = [], []
    for k in range(1, N_DEV):
        px = 1 - x if (k >> 2) & 1 else x
        py = 1 - y if (k >> 1) & 1 else y
        pc = 1 - c if k & 1 else c
        peer = 4 * px + 2 * py + pc
        for a in range(n):
            s = (k - 1) * n + a
            for dst, group in ((me, sends), (peer, arrivals)):
                group.append(pltpu.make_async_remote_copy(
                    src_ref=src_for(a, peer), dst_ref=outs[a].at[dst],
                    send_sem=send_sems.at[s], recv_sem=recv_sems.at[s],
                    device_id=(px, py, pc), device_id_type=pl.DeviceIdType.MESH))
    return local, sends, arrivals


def _exchange_start(copies):
    local, sends, _ = copies
    for cp in local + sends:
        cp.start()


def _exchange_wait(copies):
    local, sends, arrivals = copies
    for cp in arrivals:
        cp.wait_recv()
    for cp in sends:
        cp.wait_send()
    for cp in local:
        cp.wait()


def _exchange(arrs, modes, name):
    n = len(arrs)

    def body(*refs):
        copies = _exchange_copies(refs[:n], refs[n:2 * n], modes, refs[2 * n:])
        _exchange_start(copies)
        _exchange_wait(copies)

    outs = pl.pallas_call(
        body, name=name, out_shape=tuple(_exchange_shapes(arrs, modes)),
        in_specs=[ANY_SPEC] * n, out_specs=tuple([ANY_SPEC] * n),
        scratch_shapes=_exchange_sems(n),
    )(*arrs)
    return list(outs)


def _call(body, args, *, name, grid, in_specs, out_specs, out_shape, scratch_shapes=(), aliases=None,
          hosted=None):
    sem = ("arbitrary",) * len(grid)
    if hosted is None:
        outs = pl.pallas_call(
            body, name=name, grid=grid, in_specs=in_specs, out_specs=tuple(out_specs),
            out_shape=tuple(out_shape), scratch_shapes=list(scratch_shapes),
            input_output_aliases=aliases or {}, compiler_params=_params(sem))(*args)
        return list(outs), []
    x_arrs, modes = hosted
    n_in, n_out, n_scr, nx = len(args), len(out_shape), len(scratch_shapes), len(x_arrs)

    def wrapped(*refs):
        ins, x_ins = refs[:n_in], refs[n_in:n_in + nx]
        outs = refs[n_in + nx:n_in + nx + n_out]
        x_outs = refs[n_in + nx + n_out:n_in + 2 * nx + n_out]
        scr = refs[n_in + 2 * nx + n_out:n_in + 2 * nx + n_out + n_scr]
        sems = refs[n_in + 2 * nx + n_out + n_scr:]
        first = pl.program_id(0) == 0
        last = pl.program_id(0) == grid[0] - 1
        for axis in range(1, len(grid)):
            first = jnp.logical_and(first, pl.program_id(axis) == 0)
            last = jnp.logical_and(last, pl.program_id(axis) == grid[axis] - 1)

        @pl.when(first)
        def _():
            _exchange_start(_exchange_copies(x_ins, x_outs, modes, sems))

        body(*ins, *outs, *scr)

        @pl.when(last)
        def _():
            _exchange_wait(_exchange_copies(x_ins, x_outs, modes, sems))

    outs = pl.pallas_call(
        wrapped, name=name, grid=grid, in_specs=list(in_specs) + [ANY_SPEC] * nx,
        out_specs=tuple(out_specs) + tuple([ANY_SPEC] * nx),
        out_shape=tuple(out_shape) + tuple(_exchange_shapes(x_arrs, modes)),
        scratch_shapes=list(scratch_shapes) + _exchange_sems(nx),
        input_output_aliases=aliases or {}, compiler_params=_params(sem))(*args, *x_arrs)
    return list(outs[:n_out]), list(outs[n_out:])


def _mod_fwd(c_pad, w_mod):
    def body(c_ref, w_ref, o_ref):
        sc = _silu(c_ref[...]).astype(BF16)
        for layer in range(2):
            o_ref[layer] = _dot(sc, w_ref[layer].astype(BF16))

    return pl.pallas_call(
        body, name="mod_fwd", out_shape=jax.ShapeDtypeStruct((2, 16, w_mod.shape[2]), F32),
        compiler_params=_params())(c_pad, w_mod)


def _mod_wgrad(c_t, dmod):
    n_cols = dmod.shape[2]

    def body(c_ref, d_ref, o_ref):
        sc = _silu(c_ref[...]).astype(BF16).astype(F32)
        for layer in range(2):
            dm = d_ref[layer].astype(BF16).astype(F32)
            acc = sc[:, 0:1] * dm[0:1, :]
            for b in range(1, N_DEV):
                acc = acc + sc[:, b:b + 1] * dm[b:b + 1, :]
            o_ref[layer] = acc

    return pl.pallas_call(
        body, name="mod_wgrad", out_shape=jax.ShapeDtypeStruct((2, D_MODEL, n_cols), F32),
        compiler_params=_params())(c_t, dmod)


GROUP_COLS = 3 * ATT_WIDTH
PROJ_TM = 256


def _norm_proj_fwd(x, vec, w_int, hosted=None):
    S = x.shape[0]
    tm = PROJ_TM

    def body(x_ref, v_ref, w_ref, ht_ref, q0_ref, q1_ref, q2_ref, rest_ref):
        xv = x_ref[...]
        rstd = lax.rsqrt(jnp.mean(xv * xv, axis=-1, keepdims=True) + NORM_EPS)
        hf = ((xv * rstd) * v_ref[0:1, :]) * (1.0 + v_ref[1:2, :]) + v_ref[2:3, :]
        ht_ref[...] = hf.T.astype(BF16)
        h = hf.astype(BF16)
        for g, q_ref in enumerate((q0_ref, q1_ref, q2_ref)):
            q_ref[...] = _dot(h, w_ref[:, g * GROUP_COLS:(g + 1) * GROUP_COLS]).astype(BF16)
        rest_ref[...] = _dot(h, w_ref[:, 3 * GROUP_COLS:])

    n_rest = QKV_CHUNKS * ATT_WIDTH
    qkv_shape = jax.ShapeDtypeStruct((S, GROUP_COLS), BF16)
    qkv_spec = pl.BlockSpec((tm, GROUP_COLS), lambda i: (i, 0))
    return _call(
        body, (x, vec, w_int), name="norm_proj_fwd", grid=(S // tm,),
        in_specs=[pl.BlockSpec((tm, D_MODEL), lambda i: (i, 0)),
                  pl.BlockSpec((8, D_MODEL), lambda i: (0, 0)),
                  pl.BlockSpec(memory_space=pltpu.VMEM)],
        out_specs=(pl.BlockSpec((D_MODEL, tm), lambda i: (0, i)), qkv_spec, qkv_spec, qkv_spec,
                   pl.BlockSpec((tm, n_rest), lambda i: (i, 0))),
        out_shape=(jax.ShapeDtypeStruct((D_MODEL, S), BF16), qkv_shape, qkv_shape, qkv_shape,
                   jax.ShapeDtypeStruct((S, n_rest), F32)),
        hosted=hosted)


def _proj_dgrad_norm_bwd(dqkv, drest, w_int, x, dxn, vec, hosted=None):
    S = x.shape[0]
    tm = PROJ_TM

    def body(d0_ref, d1_ref, d2_ref, dr_ref, w_ref, x_ref, dxn_ref, v_ref, dx_ref, acc_ref):
        i = pl.program_id(0)

        @pl.when(i == 0)
        def _():
            acc_ref[...] = jnp.zeros_like(acc_ref)

        def wcols(k):
            return w_ref[:, k * ATT_WIDTH:(k + 1) * ATT_WIDTH]

        dhv = None
        for g, d_ref in enumerate((d0_ref, d1_ref, d2_ref)):
            for t in range(3):
                part = _dot_nt(d_ref[t], wcols(3 * g + t))
                dhv = part if dhv is None else dhv + part
        for t in range(7):
            dhv = dhv + _dot_nt(dr_ref[t], wcols(11 + t))
        for t in range(2):
            dhv = dhv + _dot_nt(dr_ref[8 + t], wcols(9 + t))

        xv = x_ref[...]
        g = v_ref[0:1, :]
        sc1 = 1.0 + v_ref[1:2, :]
        rstd = lax.rsqrt(jnp.mean(xv * xv, axis=-1, keepdims=True) + NORM_EPS)
        xhat = xv * rstd
        acc_ref[0:1, :] += jnp.sum(dhv, axis=0, keepdims=True)
        acc_ref[1:2, :] += jnp.sum(dhv * (xhat * g), axis=0, keepdims=True)
        acc_ref[2:3, :] += jnp.sum(dhv * xhat * sc1, axis=0, keepdims=True)
        dxhat = dhv * (g * sc1)
        dx = rstd * (dxhat - xhat * jnp.mean(dxhat * xhat, axis=-1, keepdims=True))
        dx_ref[...] = dx + dxn_ref[...]

    row = pl.BlockSpec((tm, D_MODEL), lambda i: (i, 0))
    vec_spec = pl.BlockSpec((8, D_MODEL), lambda i: (0, 0))
    grp = pl.BlockSpec((3, tm, ATT_WIDTH), lambda i: (0, i, 0))
    return _call(
        body, (*dqkv, drest, w_int, x, dxn, vec), name="proj_dgrad_norm_bwd", grid=(S // tm,),
        in_specs=[grp, grp, grp, pl.BlockSpec((DREST_CHUNKS, tm, ATT_WIDTH), lambda i: (0, i, 0)),
                  pl.BlockSpec(memory_space=pltpu.VMEM), row, row, vec_spec],
        out_specs=(row, vec_spec),
        out_shape=(jax.ShapeDtypeStruct((S, D_MODEL), F32), jax.ShapeDtypeStruct((8, D_MODEL), F32)),
        hosted=hosted)


def _proj_wgrad_part(h_t, d, n_chunks, chunk_block, name):
    S = h_t.shape[1]
    ts = 512

    def body(h_ref, d_ref, o_ref):
        i = pl.program_id(0)

        @pl.when(i == 0)
        def _():
            o_ref[...] = jnp.zeros_like(o_ref)

        ht = h_ref[...]
        for t in range(n_chunks):
            o_ref[:, t * ATT_WIDTH:(t + 1) * ATT_WIDTH] += _dot(ht, d_ref[t])

    return pl.pallas_call(
        body, name=name, grid=(S // ts,),
        in_specs=[pl.BlockSpec((D_MODEL, ts), lambda i: (0, i)),
                  pl.BlockSpec((n_chunks, ts, ATT_WIDTH), lambda i: (chunk_block, i, 0))],
        out_specs=pl.BlockSpec((D_MODEL, n_chunks * ATT_WIDTH), lambda i: (0, 0)),
        out_shape=jax.ShapeDtypeStruct((D_MODEL, n_chunks * ATT_WIDTH), F32),
        compiler_params=_params(("arbitrary",)))(h_t, d)


def _wgrad_tn(a, b, name):
    S, M = a.shape
    N = b.shape[1]
    ts = 1024

    def body(a_ref, b_ref, o_ref):
        i = pl.program_id(0)
        part = _dot_tn(a_ref[...], b_ref[...])

        @pl.when(i == 0)
        def _():
            o_ref[...] = part

        @pl.when(i > 0)
        def _():
            o_ref[...] += part

    return pl.pallas_call(
        body, name=name, grid=(S // ts,),
        in_specs=[pl.BlockSpec((ts, M), lambda i: (i, 0)), pl.BlockSpec((ts, N), lambda i: (i, 0))],
        out_specs=pl.BlockSpec((M, N), lambda i: (0, 0)),
        out_shape=jax.ShapeDtypeStruct((M, N), F32),
        compiler_params=_params(("arbitrary",)))(a, b)


def _attn_fwd(qkv, g, r):
    S = qkv.shape[0]
    L = S // r
    nb = L // BAND
    scale = HEAD_DIM ** -0.5
    qv = qkv.reshape(L, r * 3 * ATT_WIDTH)

    def body(q_ref, kp_ref, kc_ref, vp_ref, vc_ref, o_ref, st_ref):
        n = pl.program_id(1)
        ii = lax.broadcasted_iota(jnp.int32, (BAND, 2 * BAND), 0)
        kk = lax.broadcasted_iota(jnp.int32, (BAND, 2 * BAND), 1)
        mask = jnp.logical_and(jnp.logical_and(kk >= ii, kk <= ii + BAND),
                               jnp.logical_or(kk >= BAND, n > 0))
        lane = lax.broadcasted_iota(jnp.int32, (BAND, 128), 1)
        stats = jnp.zeros((BAND, 128), F32)
        for h in range(HEADS):
            sl = slice(h * HEAD_DIM, (h + 1) * HEAD_DIM)
            k_cat = jnp.concatenate([kp_ref[:, sl], kc_ref[:, sl]], axis=0)
            v_cat = jnp.concatenate([vp_ref[:, sl], vc_ref[:, sl]], axis=0)
            s = jnp.where(mask, _dot_nt(q_ref[:, sl], k_cat) * scale, NEG_INF)
            m = jnp.max(s, axis=1, keepdims=True)
            p = jnp.exp(s - m)
            l = jnp.sum(p, axis=1, keepdims=True)
            o_ref[:, sl] = _dot(p.astype(BF16), v_cat) / l
            stats = jnp.where(lane == h, m, stats)
            stats = jnp.where(lane == HEADS + h, l, stats)
        st_ref[...] = stats

    blk = (BAND, ATT_WIDTH)
    return pl.pallas_call(
        body, name=f"attn_fwd_g{g}", grid=(r, nb),
        in_specs=[
            pl.BlockSpec(blk, lambda rho, n: (n, rho * 3)),
            pl.BlockSpec(blk, lambda rho, n: (jnp.maximum(n - 1, 0), rho * 3 + 1)),
            pl.BlockSpec(blk, lambda rho, n: (n, rho * 3 + 1)),
            pl.BlockSpec(blk, lambda rho, n: (jnp.maximum(n - 1, 0), rho * 3 + 2)),
            pl.BlockSpec(blk, lambda rho, n: (n, rho * 3 + 2)),
        ],
        out_specs=(pl.BlockSpec(blk, lambda rho, n: (n, rho)),
                   pl.BlockSpec((BAND, 128), lambda rho, n: (n, rho))),
        out_shape=(jax.ShapeDtypeStruct((L, r * ATT_WIDTH), F32),
                   jax.ShapeDtypeStruct((L, r * 128), F32)),
        compiler_params=_params(("parallel", "parallel")))(qv, qv, qv, qv, qv)


def _attn_combine(outs, stats, rest):
    S = rest.shape[0]
    tm = 512
    gatt_blk = 8

    def body(o0_ref, o1_ref, o2_ref, s0_ref, s1_ref, s2_ref, g_ref, o_ref, a_ref, lse_ref):
        o_refs = (o0_ref, o1_ref, o2_ref)
        st = (s0_ref[...], s1_ref[...], s2_ref[...])
        lane = lax.broadcasted_iota(jnp.int32, (tm, 128), 1)
        lse_out = jnp.zeros((tm, 128), F32)
        for h in range(HEADS):
            sl = slice(h * HEAD_DIM, (h + 1) * HEAD_DIM)
            ms = [s[:, h:h + 1] for s in st]
            ls = [s[:, HEADS + h:HEADS + h + 1] for s in st]
            m_all = jnp.maximum(jnp.maximum(ms[0], ms[1]), ms[2])
            ws = [l * jnp.exp(m - m_all) for m, l in zip(ms, ls)]
            den = ws[0] + ws[1] + ws[2]
            o = (ws[0] * o_refs[0][:, sl] + ws[1] * o_refs[1][:, sl] + ws[2] * o_refs[2][:, sl]) / den
            o_ref[:, sl] = o
            a_ref[:, sl] = (o * _silu(g_ref[:, sl])).astype(BF16)
            lse_out = jnp.where(lane == h, m_all + jnp.log(den), lse_out)
        lse_ref[...] = lse_out

    o_spec = pl.BlockSpec((tm, ATT_WIDTH), lambda i: (i, 0))
    s_spec = pl.BlockSpec((tm, 128), lambda i: (i, 0))
    return pl.pallas_call(
        body, name="attn_combine", grid=(S // tm,),
        in_specs=[o_spec, o_spec, o_spec, s_spec, s_spec, s_spec,
                  pl.BlockSpec((tm, ATT_WIDTH), lambda i: (i, gatt_blk))],
        out_specs=(o_spec, o_spec, s_spec),
        out_shape=(jax.ShapeDtypeStruct((S, ATT_WIDTH), F32), jax.ShapeDtypeStruct((S, ATT_WIDTH), BF16),
                   jax.ShapeDtypeStruct((S, 128), F32)),
        compiler_params=_params(("parallel",)))(*outs, *stats, rest)


def _attn_bwd(qkv, do, lse, dvec, g, r, hosted=None):
    S = qkv.shape[0]
    L = S // r
    nb = L // BAND
    scale = HEAD_DIM ** -0.5
    qv = qkv.reshape(L, r * 3 * ATT_WIDTH)
    dov = do.reshape(L, r * ATT_WIDTH)
    lsev = lse.reshape(L, r * 128)
    dvv = dvec.reshape(L, r * 128)

    def body(qc_ref, qn_ref, k_ref, v_ref, doc_ref, don_ref, lc_ref, ln_ref, dc_ref, dn_ref,
             out_ref, carry_ref):
        j = pl.program_id(1)

        @pl.when(j == 0)
        def _():
            carry_ref[...] = jnp.zeros_like(carry_ref)

        ii = lax.broadcasted_iota(jnp.int32, (2 * BAND, BAND), 0)
        kk = lax.broadcasted_iota(jnp.int32, (2 * BAND, BAND), 1)
        mask = jnp.logical_or(
            jnp.logical_and(ii < BAND, kk <= ii),
            jnp.logical_and(jnp.logical_and(ii >= BAND, kk >= ii - BAND), j < nb - 1))
        lse2 = jnp.concatenate([lc_ref[...], ln_ref[...]], axis=0)
        dvec2 = jnp.concatenate([dc_ref[...], dn_ref[...]], axis=0)
        for h in range(HEADS):
            sl = slice(h * HEAD_DIM, (h + 1) * HEAD_DIM)
            k = k_ref[:, sl]
            v = v_ref[:, sl]
            q = jnp.concatenate([qc_ref[:, sl], qn_ref[:, sl]], axis=0)
            do = jnp.concatenate([doc_ref[:, sl], don_ref[:, sl]], axis=0)
            p = jnp.where(mask, jnp.exp(_dot_nt(q, k) * scale - lse2[:, h:h + 1]), 0.0)
            ds = (p * (_dot_nt(do, v) - dvec2[:, h:h + 1])).astype(BF16)
            dq2 = _dot(ds, k) * scale
            out_ref[0, :, sl] = (carry_ref[:, sl] + dq2[:BAND]).astype(BF16)
            out_ref[1, :, sl] = (_dot_tn(ds, q) * scale).astype(BF16)
            out_ref[2, :, sl] = _dot_tn(p.astype(BF16), do).astype(BF16)
            carry_ref[:, sl] = dq2[BAND:]

    blk = (BAND, ATT_WIDTH)
    sblk = (BAND, 128)

    def nxt(j):
        return jnp.minimum(j + 1, nb - 1)

    (out,), got = _call(
        body, (qv, qv, qv, qv, dov, dov, lsev, lsev, dvv, dvv), name=f"attn_bwd_g{g}", grid=(r, nb),
        in_specs=[
            pl.BlockSpec(blk, lambda rho, j: (j, rho * 3)),
            pl.BlockSpec(blk, lambda rho, j: (nxt(j), rho * 3)),
            pl.BlockSpec(blk, lambda rho, j: (j, rho * 3 + 1)),
            pl.BlockSpec(blk, lambda rho, j: (j, rho * 3 + 2)),
            pl.BlockSpec(blk, lambda rho, j: (j, rho)),
            pl.BlockSpec(blk, lambda rho, j: (nxt(j), rho)),
            pl.BlockSpec(sblk, lambda rho, j: (j, rho)),
            pl.BlockSpec(sblk, lambda rho, j: (nxt(j), rho)),
            pl.BlockSpec(sblk, lambda rho, j: (j, rho)),
            pl.BlockSpec(sblk, lambda rho, j: (nxt(j), rho)),
        ],
        out_specs=(pl.BlockSpec((3, BAND, ATT_WIDTH), lambda rho, j: (0, j, rho)),),
        out_shape=(jax.ShapeDtypeStruct((3, L, r * ATT_WIDTH), BF16),),
        scratch_shapes=[pltpu.VMEM((BAND, ATT_WIDTH), F32)],
        hosted=hosted)
    return out.reshape(3, S, ATT_WIDTH), got


LRU_T = 256


def _linear_scan(a, b, carry, reverse):
    T = a.shape[0]
    row8 = lax.broadcasted_iota(jnp.int32, a.shape, 0) & 7
    for s in (1, 2, 4):
        keep = (row8 < 8 - s) if reverse else (row8 >= s)
        shift = T - s if reverse else s
        b_sh = jnp.where(keep, pltpu.roll(b, shift, 0), 0.0)
        a_sh = jnp.where(keep, pltpu.roll(a, shift, 0), 1.0)
        b = a * b_sh + b
        a = a * a_sh
    tiles = [None] * (T // 8)
    order = range(T // 8 - 1, -1, -1) if reverse else range(T // 8)
    for k in order:
        y = b[8 * k:8 * k + 8] + a[8 * k:8 * k + 8] * carry
        tiles[k] = y
        carry = y[0:1] if reverse else y[7:8]
    return jnp.concatenate(tiles, axis=0), carry


def _gate_matmuls(ucb, w_ref, bias):
    parts = [_dot(ucb[:, j * 128:(j + 1) * 128], w_ref[j]) for j in range(8)]
    return jnp.concatenate(parts, axis=1) + bias


def _conv_fwd(u, u_prev8, cw_ref):
    T = u.shape[0]
    ue = jnp.concatenate([u_prev8, u], axis=0)
    uc = cw_ref[4:5, :] + cw_ref[0:1, :] * u
    for j in range(1, CONV_WIDTH):
        uc = uc + cw_ref[j:j + 1, :] * pltpu.roll(ue, j, 0)[8:8 + T]
    return uc


def _lru_fwd(rest, cw, wr2, wi2, hosted=None):
    S = rest.shape[0]
    T = LRU_T

    def body(u_ref, cw_ref, wr_ref, wi_ref, h_ref, a_ref, ucar_ref, hcar_ref):
        c = pl.program_id(0)

        @pl.when(c == 0)
        def _():
            ucar_ref[...] = jnp.zeros_like(ucar_ref)
            hcar_ref[...] = jnp.zeros_like(hcar_ref)

        u = u_ref[...]
        uc = _conv_fwd(u, ucar_ref[...], cw_ref)
        ucar_ref[...] = u[T - 8:, :]
        ucb = uc.astype(BF16)
        r = jax.nn.sigmoid(_gate_matmuls(ucb, wr_ref, cw_ref[5:6, :]))
        ig = jax.nn.sigmoid(_gate_matmuls(ucb, wi_ref, cw_ref[6:7, :]))
        log_a = -LRU_C * r * _softplus_neg(cw_ref[7:8, :])
        a = jnp.exp(log_a)
        b = jnp.sqrt(_neg_expm1(2.0 * log_a)) * (ig * uc)
        a_ref[...] = a
        h, last = _linear_scan(a, b, hcar_ref[0:1, :], reverse=False)
        h_ref[...] = h
        hcar_ref[...] = jnp.broadcast_to(last, (8, D_MODEL))

    row_spec = pl.BlockSpec((T, D_MODEL), lambda c: (c, 0))
    return _call(
        body, (rest, cw, wr2, wi2), name="lru_fwd", grid=(S // T,),
        in_specs=[row_spec, pl.BlockSpec((8, D_MODEL), lambda c: (0, 0)),
                  pl.BlockSpec((8, 128, 128), lambda c: (0, 0, 0)),
                  pl.BlockSpec((8, 128, 128), lambda c: (0, 0, 0))],
        out_specs=(row_spec, row_spec),
        out_shape=(jax.ShapeDtypeStruct((S, D_MODEL), F32), jax.ShapeDtypeStruct((S, D_MODEL), F32)),
        scratch_shapes=[pltpu.VMEM((8, D_MODEL), F32), pltpu.VMEM((8, D_MODEL), F32)],
        hosted=hosted)


def _lru_bwd(dhl, hl, a_all, rest, cw, wr2, wi2, dproj, hosted=None):
    S = rest.shape[0]
    T = LRU_T
    nc = S // T

    def body(dh_ref, h_ref, hp_ref, a_ref, u_ref, up_ref, cw_ref, wr_ref, wi_ref, _alias,
             du_ref, acc_ref, gwr_ref, gwi_ref, gcar_ref, acar_ref, dcar_ref):
        step = pl.program_id(0)
        c = nc - 1 - step

        @pl.when(step == 0)
        def _():
            acc_ref[...] = jnp.zeros_like(acc_ref)
            gwr_ref[...] = jnp.zeros_like(gwr_ref)
            gwi_ref[...] = jnp.zeros_like(gwi_ref)
            gcar_ref[...] = jnp.zeros_like(gcar_ref)
            acar_ref[...] = jnp.zeros_like(acar_ref)
            dcar_ref[...] = jnp.zeros_like(dcar_ref)

        row = lax.broadcasted_iota(jnp.int32, (T, D_MODEL), 0)
        u = u_ref[...]
        u_prev = jnp.where(c > 0, up_ref[...], 0.0)
        h_prev8 = jnp.where(c > 0, hp_ref[...], 0.0)
        a = a_ref[...]
        h = h_ref[...]
        uc = _conv_fwd(u, u_prev, cw_ref)
        ucb = uc.astype(BF16)
        r = jax.nn.sigmoid(_gate_matmuls(ucb, wr_ref, cw_ref[5:6, :]))
        ig = jax.nn.sigmoid(_gate_matmuls(ucb, wi_ref, cw_ref[6:7, :]))
        sp = _softplus_neg(cw_ref[7:8, :])
        log_a = -LRU_C * r * sp
        mult = jnp.sqrt(_neg_expm1(2.0 * log_a))
        a_next = jnp.where(row < T - 1, pltpu.roll(a, T - 1, 0), acar_ref[0:1, :])
        G, first = _linear_scan(a_next, dh_ref[...], gcar_ref[0:1, :], reverse=True)
        gcar_ref[...] = jnp.broadcast_to(first, (8, D_MODEL))
        acar_ref[...] = jnp.broadcast_to(a[0:1, :], (8, D_MODEL))
        he = jnp.concatenate([h_prev8, h], axis=0)
        h_before = pltpu.roll(he, 1, 0)[8:8 + T]
        d_a = G * h_before
        d_mult = G * (ig * uc)
        d_ig = G * (mult * uc)
        duc = G * (mult * ig)
        d_log_a = d_a * a - d_mult * (a * a) / mult
        d_r = d_log_a * (-LRU_C * sp)
        d_sp = jnp.sum(d_log_a * (-LRU_C * r), axis=0, keepdims=True)
        dpre_r = d_r * r * (1.0 - r)
        dpre_i = d_ig * ig * (1.0 - ig)
        dprb = dpre_r.astype(BF16)
        dpib = dpre_i.astype(BF16)
        back = []
        for j in range(8):
            sl = slice(j * 128, (j + 1) * 128)
            back.append(_dot_nt(dprb[:, sl], wr_ref[j]) + _dot_nt(dpib[:, sl], wi_ref[j]))
            gwr_ref[j] += _dot_tn(ucb[:, sl], dprb[:, sl])
            gwi_ref[j] += _dot_tn(ucb[:, sl], dpib[:, sl])
        duc = duc + jnp.concatenate(back, axis=1)
        de = jnp.concatenate([duc, dcar_ref[...]], axis=0)
        du = cw_ref[0:1, :] * duc
        ue = jnp.concatenate([u_prev, u], axis=0)
        acc_ref[0:1, :] += jnp.sum(duc * u, axis=0, keepdims=True)
        for j in range(1, CONV_WIDTH):
            du = du + cw_ref[j:j + 1, :] * pltpu.roll(de, T + 8 - j, 0)[0:T]
            acc_ref[j:j + 1, :] += jnp.sum(duc * pltpu.roll(ue, j, 0)[8:8 + T], axis=0, keepdims=True)
        dcar_ref[...] = duc[0:8, :]
        acc_ref[4:5, :] += jnp.sum(duc, axis=0, keepdims=True)
        acc_ref[5:6, :] += jnp.sum(dpre_r, axis=0, keepdims=True)
        acc_ref[6:7, :] += jnp.sum(dpre_i, axis=0, keepdims=True)
        acc_ref[7:8, :] += d_sp
        du_ref[0] = du[:, 0:ATT_WIDTH].astype(BF16)
        du_ref[1] = du[:, ATT_WIDTH:].astype(BF16)

    def rev(step):
        return nc - 1 - step

    def prev8(step):
        return jnp.maximum(rev(step) * (T // 8) - 1, 0)

    row_spec = pl.BlockSpec((T, D_MODEL), lambda s: (rev(s), 0))
    prev_spec = pl.BlockSpec((8, D_MODEL), lambda s: (prev8(s), 0))
    vec_spec = pl.BlockSpec((8, D_MODEL), lambda s: (0, 0))
    w_spec = pl.BlockSpec((8, 128, 128), lambda s: (0, 0, 0))
    return _call(
        body, (dhl, hl, hl, a_all, rest, rest, cw, wr2, wi2, dproj), name="lru_bwd", grid=(nc,),
        in_specs=[row_spec, row_spec, prev_spec, row_spec, row_spec, prev_spec, vec_spec, w_spec, w_spec,
                  ANY_SPEC],
        out_specs=(pl.BlockSpec((2, T, ATT_WIDTH), lambda s: (4, rev(s), 0)), vec_spec, w_spec, w_spec),
        out_shape=(jax.ShapeDtypeStruct(dproj.shape, BF16), jax.ShapeDtypeStruct((8, D_MODEL), F32),
                   jax.ShapeDtypeStruct((8, 128, 128), F32), jax.ShapeDtypeStruct((8, 128, 128), F32)),
        scratch_shapes=[pltpu.VMEM((8, D_MODEL), F32), pltpu.VMEM((8, D_MODEL), F32),
                        pltpu.VMEM((8, D_MODEL), F32)],
        aliases={9: 0}, hosted=hosted)


OUT_TM = 256


def _out_fwd(a_gated, hl, rest, x, vec, w_pa, w_pb, w_o):
    S = x.shape[0]
    tm = OUT_TM

    def body(ag_ref, hl_ref, gl_ref, ma_ref, mb_ref, x_ref, v_ref, wpa_ref, wpb_ref, wo_ref,
             xn_ref, mix_ref, ya_ref, yb_ref, bg_ref):
        bg = (hl_ref[...] * _silu(gl_ref[...])).astype(BF16)
        ya = _dot(ag_ref[...], wpa_ref[...])
        yb = _dot(bg, wpb_ref[...])
        mix = (jax.nn.sigmoid(ma_ref[...]) * ya + jax.nn.sigmoid(mb_ref[...]) * yb).astype(BF16)
        out = _dot(mix, wo_ref[...])
        rstd = lax.rsqrt(jnp.mean(out * out, axis=-1, keepdims=True) + NORM_EPS)
        xn_ref[...] = x_ref[...] + v_ref[1:2, :] * ((out * rstd) * v_ref[0:1, :])
        mix_ref[...] = mix
        ya_ref[...] = ya.astype(BF16)
        yb_ref[...] = yb.astype(BF16)
        bg_ref[...] = bg

    def col(j):
        return pl.BlockSpec((tm, D_MODEL), lambda i: (i, j))

    def whole(shape):
        return pl.BlockSpec(shape, lambda i: (0, 0))

    row = col(0)
    bf = jax.ShapeDtypeStruct((S, D_MODEL), BF16)
    return pl.pallas_call(
        body, name="out_fwd", grid=(S // tm,),
        in_specs=[pl.BlockSpec((tm, ATT_WIDTH), lambda i: (i, 0)), row, col(1), col(2), col(3), row,
                  whole((8, D_MODEL)), whole((ATT_WIDTH, D_MODEL)), whole((D_MODEL, D_MODEL)),
                  whole((D_MODEL, D_MODEL))],
        out_specs=(row, row, row, row, row),
        out_shape=(jax.ShapeDtypeStruct((S, D_MODEL), F32), bf, bf, bf, bf),
        compiler_params=_params(("parallel",)))(a_gated, hl, rest, rest, rest, x, vec, w_pa, w_pb, w_o)


def _out_bwd(dxn, mix, ya, yb, hl, rest, o_att, vec, w_pa, w_pb, w_o, hosted=None):
    S = dxn.shape[0]
    tm = OUT_TM

    def body(dxn_ref, mix_ref, ya_ref, yb_ref, hl_ref, gl_ref, ma_ref, mb_ref, ga_ref, o_ref, v_ref,
             wpa_ref, wpb_ref, wo_ref,
             dout_ref, dya_ref, dyb_ref, do_ref, dv_ref, dhl_ref, dp_ref, acc_ref):
        i = pl.program_id(0)

        @pl.when(i == 0)
        def _():
            acc_ref[...] = jnp.zeros_like(acc_ref)

        g_post = v_ref[0:1, :]
        gate = v_ref[1:2, :]
        dxn_v = dxn_ref[...]
        out = _dot(mix_ref[...], wo_ref[...])
        rstd = lax.rsqrt(jnp.mean(out * out, axis=-1, keepdims=True) + NORM_EPS)
        nrm = out * rstd
        acc_ref[0:1, :] += jnp.sum(dxn_v * (nrm * g_post), axis=0, keepdims=True)
        acc_ref[1:2, :] += jnp.sum(dxn_v * gate * nrm, axis=0, keepdims=True)
        dn = dxn_v * (gate * g_post)
        dout = (rstd * (dn - nrm * jnp.mean(dn * nrm, axis=-1, keepdims=True))).astype(BF16)
        dout_ref[...] = dout
        dmix = _dot_nt(dout, wo_ref[...])
        sa = jax.nn.sigmoid(ma_ref[...])
        sb = jax.nn.sigmoid(mb_ref[...])
        dya = (dmix * sa).astype(BF16)
        dyb = (dmix * sb).astype(BF16)
        dya_ref[...] = dya
        dyb_ref[...] = dyb
        dma = dmix * ya_ref[...].astype(F32) * (sa * (1.0 - sa))
        dmb = dmix * yb_ref[...].astype(F32) * (sb * (1.0 - sb))
        d_ag = _dot_nt(dya, wpa_ref[...])
        d_bg = _dot_nt(dyb, wpb_ref[...])
        gl = gl_ref[...]
        hl_v = hl_ref[...]
        dhl_ref[...] = d_bg * _silu(gl)
        dgl = d_bg * hl_v * _dsilu(gl)
        ga = ga_ref[...]
        o = o_ref[...]
        do = d_ag * _silu(ga)
        dga = d_ag * o * _dsilu(ga)
        do_ref[...] = do.astype(BF16)
        lane = lax.broadcasted_iota(jnp.int32, (tm, 128), 1)
        dvec = jnp.zeros((tm, 128), F32)
        for h in range(HEADS):
            sl = slice(h * HEAD_DIM, (h + 1) * HEAD_DIM)
            dvec = jnp.where(lane == h, jnp.sum(do[:, sl] * o[:, sl], axis=1, keepdims=True), dvec)
        dv_ref[...] = dvec
        dp_ref[0] = dgl[:, 0:ATT_WIDTH].astype(BF16)
        dp_ref[1] = dgl[:, ATT_WIDTH:].astype(BF16)
        dp_ref[2] = dma[:, 0:ATT_WIDTH].astype(BF16)
        dp_ref[3] = dma[:, ATT_WIDTH:].astype(BF16)
        dp_ref[4] = dmb[:, 0:ATT_WIDTH].astype(BF16)
        dp_ref[5] = dmb[:, ATT_WIDTH:].astype(BF16)
        dp_ref[6] = dga.astype(BF16)

    def col(j):
        return pl.BlockSpec((tm, D_MODEL), lambda i: (i, j))

    def whole(shape):
        return pl.BlockSpec(shape, lambda i: (0, 0))

    row = col(0)
    att = pl.BlockSpec((tm, ATT_WIDTH), lambda i: (i, 0))
    bf = jax.ShapeDtypeStruct((S, D_MODEL), BF16)
    return _call(
        body, (dxn, mix, ya, yb, hl, rest, rest, rest, rest, o_att, vec, w_pa, w_pb, w_o),
        name="out_bwd", grid=(S // tm,),
        in_specs=[row, row, row, row, row, col(1), col(2), col(3),
                  pl.BlockSpec((tm, ATT_WIDTH), lambda i: (i, 8)), att,
                  whole((8, D_MODEL)), whole((ATT_WIDTH, D_MODEL)), whole((D_MODEL, D_MODEL)),
                  whole((D_MODEL, D_MODEL))],
        out_specs=(row, row, row, att, pl.BlockSpec((tm, 128), lambda i: (i, 0)), row,
                   pl.BlockSpec((7, tm, ATT_WIDTH), lambda i: (0, i, 0)), whole((8, D_MODEL))),
        out_shape=(bf, bf, bf, jax.ShapeDtypeStruct((S, ATT_WIDTH), BF16),
                   jax.ShapeDtypeStruct((S, 128), F32), jax.ShapeDtypeStruct((S, D_MODEL), F32),
                   jax.ShapeDtypeStruct((DREST_CHUNKS, S, ATT_WIDTH), BF16),
                   jax.ShapeDtypeStruct((8, D_MODEL), F32)),
        hosted=hosted)


def _loss_head(y, target):
    S = y.shape[0]
    tm = 512

    def body(y_ref, t_ref, dy_ref, acc_ref):
        i = pl.program_id(0)

        @pl.when(i == 0)
        def _():
            acc_ref[...] = jnp.zeros_like(acc_ref)

        err = y_ref[...] - t_ref[...]
        dy_ref[...] = err * (1.0 / D_MODEL)
        part = jnp.sum(jnp.sum(err * err, axis=1, keepdims=True), axis=0, keepdims=True)
        acc_ref[...] += jnp.broadcast_to(part, acc_ref.shape)

    row = pl.BlockSpec((tm, D_MODEL), lambda i: (i, 0))
    return pl.pallas_call(
        body, name="loss_head", grid=(S // tm,),
        in_specs=[row, row],
        out_specs=(row, pl.BlockSpec((8, 128), lambda i: (0, 0))),
        out_shape=(jax.ShapeDtypeStruct((S, D_MODEL), F32), jax.ShapeDtypeStruct((8, 128), F32)),
        compiler_params=_params(("arbitrary",)))(y, target)


def _adamw(gsrcs, w, m, v, n_stack, name, rows=None):
    n_layers, R, C = w.shape
    assert len(gsrcs) == n_layers
    budget = 96 * 1024
    tr = rows
    if tr is None:
        tr = R
        while tr * C > budget and tr % 16 == 0:
            tr //= 2
    assert R % tr == 0 and (tr % 8 == 0 or tr == R)
    c1 = 1.0 - ADAM_B1 ** ADAM_STEP
    c2 = 1.0 - ADAM_B2 ** ADAM_STEP

    def body(*refs):
        g_refs = refs[:n_layers]
        w_ref, m_ref, v_ref, go_ref, d_ref, mo_ref, vo_ref = refs[n_layers:]

        def update(g_ref):
            if n_stack:
                g = g_ref[0].astype(F32)
                for s in range(1, n_stack):
                    g = g + g_ref[s].astype(F32)
            else:
                g = g_ref[...]
            m_new = ADAM_B1 * m_ref[...] + (1.0 - ADAM_B1) * g
            v_new = ADAM_B2 * v_ref[...] + (1.0 - ADAM_B2) * (g * g)
            m_hat = m_new / c1
            v_hat = v_new / c2
            go_ref[...] = g
            d_ref[...] = -ADAM_LR * (m_hat / (jnp.sqrt(v_hat) + ADAM_EPS) + ADAM_WD * w_ref[...])
            mo_ref[...] = m_new
            vo_ref[...] = v_new

        for layer in range(n_layers):
            pl.when(pl.program_id(0) == layer)(functools.partial(update, g_refs[layer]))

    def g_spec(layer):
        def rows_of(l, i):
            return jnp.where(l == layer, i, 0)
        if n_stack:
            return pl.BlockSpec((n_stack, tr, C), lambda l, i: (0, rows_of(l, i), 0))
        return pl.BlockSpec((tr, C), lambda l, i: (rows_of(l, i), 0))

    blk = pl.BlockSpec((None, tr, C), lambda l, i: (l, i, 0))
    shp = jax.ShapeDtypeStruct((n_layers, R, C), F32)
    return pl.pallas_call(
        body, name=name, grid=(n_layers, R // tr),
        in_specs=[g_spec(layer) for layer in range(n_layers)] + [blk, blk, blk],
        out_specs=(blk, blk, blk, blk), out_shape=(shp, shp, shp, shp),
        compiler_params=_params(("arbitrary", "arbitrary")))(*gsrcs, w, m, v)


def _pair_blocks(w):
    w = w.reshape(8, 2, 64, 64)
    z = jnp.zeros((8, 64, 64), w.dtype)
    top = jnp.concatenate([w[:, 0], z], axis=2)
    bot = jnp.concatenate([z, w[:, 1]], axis=2)
    return jnp.concatenate([top, bot], axis=1).astype(BF16)


def _unpair_blocks(g):
    return jnp.stack([g[:, :64, :64], g[:, 64:, 64:]], axis=1).reshape(16, 64, 64)


def _layer_params(w_in_g, w_pa_g, w_pb_g, w_o_g, conv_w_full, conv_b, b_rg, b_ig, lru_lambda, w_rg, w_ig,
                  g_pre, g_post):
    w_full = jnp.transpose(w_in_g, (1, 0, 2)).reshape(D_MODEL, N_CHUNKS, ATT_WIDTH)
    w_int = w_full[:, jnp.array(CHUNK_PERM)].reshape(D_MODEL, N_CHUNKS * ATT_WIDTH)
    cw = jnp.concatenate([conv_w_full, conv_b[None], b_rg[None], b_ig[None], lru_lambda[None]], axis=0)
    return dict(
        w_int=w_int,
        w_pa=jnp.transpose(w_pa_g, (1, 0, 2)).reshape(ATT_WIDTH, D_MODEL),
        w_pb=w_pb_g.reshape(D_MODEL, D_MODEL), w_o=w_o_g.reshape(D_MODEL, D_MODEL),
        cw=cw, wr2=_pair_blocks(w_rg), wi2=_pair_blocks(w_ig),
        g_pre=g_pre, g_post=g_post, lru_lambda=lru_lambda)


def _layer_fwd(x, mod, p, host_proj=None, host_lru=None):
    zeros = jnp.zeros((5, D_MODEL), F32)
    vec_pre = jnp.concatenate([p["g_pre"][None], mod[1:2], mod[0:1], zeros], axis=0)
    (h_t, q0, q1, q2, rest), got_proj = _norm_proj_fwd(x, vec_pre, p["w_int"], hosted=host_proj)
    S = x.shape[0]
    qkv, outs, stats = [q0, q1, q2], [], []
    for g, r in enumerate(DILATIONS):
        o_g, st_g = _attn_fwd(qkv[g], g, r)
        outs.append(o_g.reshape(S, ATT_WIDTH))
        stats.append(st_g.reshape(S, 128))
    o_att, a_gated, lse = _attn_combine(outs, stats, rest)
    (hl, a_dec), got_lru = _lru_fwd(rest, p["cw"], p["wr2"], p["wi2"], hosted=host_lru)
    vec_post = jnp.concatenate([p["g_post"][None], mod[2:3], jnp.zeros((6, D_MODEL), F32)], axis=0)
    x_new, mix, ya, yb, b_gated = _out_fwd(a_gated, hl, rest, x, vec_post, p["w_pa"], p["w_pb"], p["w_o"])
    saved = dict(x=x, h_t=h_t, qkv=qkv, rest=rest, o_att=o_att, a_gated=a_gated, lse=lse, hl=hl, a_dec=a_dec,
                 mix=mix, ya=ya, yb=yb, b_gated=b_gated, vec_pre=vec_pre, vec_post=vec_post)
    return x_new, saved, got_proj, got_lru


def _mats_sources(g_pa, g_pb, g_o):
    return [jnp.transpose(g_pa.astype(BF16).reshape(ATT_WIDTH, N_DEV, 128), (1, 0, 2)),
            g_pb.astype(BF16).reshape(N_DEV, 128, D_MODEL), g_o.astype(BF16).reshape(N_DEV, 128, D_MODEL)]


LRU_SMALL = ("conv_b", "w_rg", "b_rg", "w_ig", "b_ig", "lru_lambda")
MOD_SMALL = ("b_mod", "g_pre", "g_post")


def _flat_rows(arrs):
    return jnp.concatenate([a.reshape(-1) for a in arrs]).reshape(-1, 128)


def _layer_bwd(dxn, sv, p, upper=None, host_own=False):
    n_in = N_CHUNKS * ATT_WIDTH // N_DEV
    a2a3 = ["a2a"] * 3
    (dout, dya, dyb, do, dvec, dhl, drest, acc_out), got_up_mats = _out_bwd(
        dxn, sv["mix"], sv["ya"], sv["yb"], sv["hl"], sv["rest"], sv["o_att"], sv["vec_post"],
        p["w_pa"], p["w_pb"], p["w_o"], hosted=(upper["mats"], a2a3) if upper else None)
    mats = _mats_sources(_wgrad_tn(sv["a_gated"], dya, "wgrad_pa"), _wgrad_tn(sv["b_gated"], dyb, "wgrad_pb"),
                         _wgrad_tn(sv["mix"], dout, "wgrad_o"))
    (drest, acc_lru, gwr2, gwi2), got_up_in = _lru_bwd(
        dhl, sv["hl"], sv["a_dec"], sv["rest"], p["cw"], p["wr2"], p["wi2"], drest,
        hosted=([upper["w_in"]], ["a2a"]) if upper else None)
    small = dict(
        conv_w=acc_lru[0:4], conv_b=acc_lru[4], b_rg=acc_lru[5], b_ig=acc_lru[6],
        lru_lambda=acc_lru[7] * (-jax.nn.sigmoid(-p["lru_lambda"])),
        w_rg=_unpair_blocks(gwr2), w_ig=_unpair_blocks(gwi2))
    hosts = [None, None, None]
    if host_own:
        lru_vec = _flat_rows([jnp.stack([small[n], upper["small"][n]]) for n in LRU_SMALL + ("conv_w",)])
        hosts = [(mats, a2a3), ([lru_vec], ["ag"]), None]
    dqkv, got_attn = [], []
    for g, r in enumerate(DILATIONS):
        d_g, got = _attn_bwd(sv["qkv"][g], do, sv["lse"], dvec, g, r, hosted=hosts[g])
        dqkv.append(d_g)
        got_attn.append(got)
    parts = [_proj_wgrad_part(sv["h_t"], dqkv[g], 3, 0, f"proj_wgrad_g{g}") for g in range(3)]
    parts.append(_proj_wgrad_part(sv["h_t"], drest, 2, 4, "proj_wgrad_u"))
    parts.append(_proj_wgrad_part(sv["h_t"], drest, 7, 0, "proj_wgrad_rest"))
    gw_int = jnp.concatenate(parts, axis=1).astype(BF16).reshape(D_MODEL, N_CHUNKS, ATT_WIDTH)
    inv_perm = [CHUNK_PERM.index(k) for k in range(N_CHUNKS)]
    gw_in = jnp.transpose(gw_int[:, jnp.array(inv_perm)].reshape(D_MODEL, N_DEV, n_in), (1, 0, 2))
    (dx, acc_pre), got_in = _proj_dgrad_norm_bwd(
        dqkv, drest, p["w_int"], sv["x"], dxn, sv["vec_pre"],
        hosted=([gw_in], ["a2a"]) if host_own else None)
    small.update(dmod=jnp.concatenate([acc_pre[0], acc_pre[1], acc_out[0]]), g_pre=acc_pre[2],
                 g_post=acc_out[1])
    comm = dict(sources=dict(mats=mats, w_in=gw_in, small=small),
                own=dict(mats=got_attn[0], w_in=got_in, lru_small=got_attn[1]),
                upper=dict(mats=got_up_mats, w_in=got_up_in))
    return dx, small, comm


SMALL_NAMES = ("b_mod", "g_pre", "conv_b", "w_rg", "b_rg", "w_ig", "b_ig", "lru_lambda", "g_post")


def kernel(x, c, w_mod, b_mod, g_pre, w_in, conv_w, conv_b, w_rg, b_rg, w_ig, b_ig, lru_lambda, w_pa, w_pb, w_o, g_post, loss_target, m_w_mod, m_b_mod, m_g_pre, m_w_in, m_conv_w, m_conv_b, m_w_rg, m_b_rg, m_w_ig, m_b_ig, m_lru_lambda, m_w_pa, m_w_pb, m_w_o, m_g_post, v_w_mod, v_b_mod, v_g_pre, v_w_in, v_conv_w, v_conv_b, v_w_rg, v_b_rg, v_w_ig, v_b_ig, v_lru_lambda, v_w_pa, v_w_pb, v_w_o, v_g_post):
    W = dict(w_mod=w_mod, b_mod=b_mod, g_pre=g_pre, w_in=w_in, conv_w=conv_w, conv_b=conv_b, w_rg=w_rg,
             b_rg=b_rg, w_ig=w_ig, b_ig=b_ig, lru_lambda=lru_lambda, w_pa=w_pa, w_pb=w_pb, w_o=w_o,
             g_post=g_post)
    M = dict(w_mod=m_w_mod, b_mod=m_b_mod, g_pre=m_g_pre, w_in=m_w_in, conv_w=m_conv_w, conv_b=m_conv_b,
             w_rg=m_w_rg, b_rg=m_b_rg, w_ig=m_w_ig, b_ig=m_b_ig, lru_lambda=m_lru_lambda, w_pa=m_w_pa,
             w_pb=m_w_pb, w_o=m_w_o, g_post=m_g_post)
    V = dict(w_mod=v_w_mod, b_mod=v_b_mod, g_pre=v_g_pre, w_in=v_w_in, conv_w=v_conv_w, conv_b=v_conv_b,
             w_rg=v_w_rg, b_rg=v_b_rg, w_ig=v_w_ig, b_ig=v_b_ig, lru_lambda=v_lru_lambda, w_pa=v_w_pa,
             w_pb=v_w_pb, w_o=v_w_o, g_post=v_g_post)
    S = x.shape[1]
    me = 4 * lax.axis_index("x") + 2 * lax.axis_index("y") + lax.axis_index("c")
    n_mod = w_mod.shape[2]
    n_in = w_in.shape[2]
    n_conv = conv_w.shape[2]

    c_rows = jnp.broadcast_to(c, (8, D_MODEL))
    w_in_b, w_pa_b, w_pb_b, w_o_b = (t.astype(BF16) for t in (w_in, w_pa, w_pb, w_o))
    g_c, g_win0, g_wpa0, g_wpb0, g_wo0, g_cw = _exchange(
        [c_rows, w_in_b[0], w_pa_b[0], w_pb_b[0], w_o_b[0], conv_w], ["ag"] * 6, "gather_weights")
    c_all = g_c[:, 0, :]
    c_pad = jnp.concatenate([c_all, jnp.zeros((8, D_MODEL), F32)], axis=0)
    conv_w_full = jnp.transpose(g_cw, (1, 2, 0, 3)).reshape(2, CONV_WIDTH, D_MODEL)

    mod_cols = _mod_fwd(c_pad, w_mod)
    mod_src = jnp.transpose(mod_cols[:, :8], (1, 0, 2))
    mod_src = jnp.concatenate([mod_src, jnp.zeros((8, 6, n_mod), F32)], axis=1)
    (mod_got,) = _exchange([mod_src], ["a2a"], "scatter_mod")
    mod = jnp.transpose(mod_got[:, :2], (1, 0, 2)).reshape(2, 3 * D_MODEL) + b_mod

    def layer_params(layer, w_in_g, w_pa_g, w_pb_g, w_o_g):
        return _layer_params(w_in_g, w_pa_g, w_pb_g, w_o_g, conv_w_full[layer], conv_b[layer], b_rg[layer],
                             b_ig[layer], lru_lambda[layer], w_rg[layer], w_ig[layer], g_pre[layer],
                             g_post[layer])

    layers = [layer_params(0, g_win0, g_wpa0, g_wpb0, g_wo0), None]
    act, sv0, (g_win1,), (g_wpa1, g_wpb1, g_wo1) = _layer_fwd(
        x[0], mod[0].reshape(3, D_MODEL), layers[0],
        host_proj=([w_in_b[1]], ["ag"]), host_lru=([w_pa_b[1], w_pb_b[1], w_o_b[1]], ["ag"] * 3))
    layers[1] = layer_params(1, g_win1, g_wpa1, g_wpb1, g_wo1)
    act, sv1, _, _ = _layer_fwd(act, mod[1].reshape(3, D_MODEL), layers[1])
    dy, loss_acc = _loss_head(act, loss_target[0])
    loss = lax.psum(0.5 * loss_acc[0, 0] / D_MODEL, MESH_AXES)

    dy, small1, comm1 = _layer_bwd(dy, sv1, layers[1])
    dy, small0, comm0 = _layer_bwd(dy, sv0, layers[0], upper=comm1["sources"], host_own=True)
    grad_x = dy[None]
    r_pa, r_pb, r_o = ([comm0["own"]["mats"][k], comm0["upper"]["mats"][k]] for k in range(3))
    r_in = [comm0["own"]["w_in"][0], comm0["upper"]["w_in"][0]]
    grads = [small0, small1]

    def stack2(name):
        return jnp.stack([grads[0][name], grads[1][name]], axis=0)

    dmod = stack2("dmod")
    dmod_src = jnp.transpose(dmod.reshape(2, N_DEV, n_mod), (1, 0, 2))
    dmod_src = jnp.concatenate([dmod_src, jnp.zeros((8, 6, n_mod), F32)], axis=1)
    mod_vec = _flat_rows([dmod, stack2("g_pre"), stack2("g_post")])
    r_dmod, r_mod_small = _exchange([dmod_src, mod_vec], ["a2a", "ag"], "exchange_grads")
    (r_lru_small,) = comm0["own"]["lru_small"]

    res = {}
    res["w_in"] = _adamw(r_in, w_in, m_w_in, v_w_in, 8, "adamw_w_in")
    res["w_pa"] = _adamw(r_pa, w_pa, m_w_pa, v_w_pa, 8, "adamw_w_pa")
    res["w_pb"] = _adamw(r_pb, w_pb, m_w_pb, v_w_pb, 8, "adamw_w_pb")
    res["w_o"] = _adamw(r_o, w_o, m_w_o, v_w_o, 8, "adamw_w_o")
    dmod_all = jnp.transpose(r_dmod[:, :2], (1, 0, 2))
    gw_mod = _mod_wgrad(jnp.transpose(c_all), dmod_all)
    res["w_mod"] = _adamw([gw_mod[0], gw_mod[1]], w_mod, m_w_mod, v_w_mod, 0, "adamw_w_mod")
    for names_rep, stack, tag in ((LRU_SMALL, r_lru_small, "lru"), (MOD_SMALL, r_mod_small, "mod")):
        rows = sum(W[n].size for n in names_rep) // 128
        rep = _adamw([stack], *(_flat_rows([src[n] for n in names_rep])[None] for src in (W, M, V)), 8,
                     f"adamw_small_{tag}", rows=rows // 2 if rows % 16 == 0 else rows)
        off = 0
        for name in names_rep:
            size = W[name].size
            res[name] = tuple(t.reshape(-1)[off:off + size].reshape(W[name].shape) for t in rep)
            off += size
    lru_rows = sum(W[n].size for n in LRU_SMALL) // 128
    conv_stack = r_lru_small[:, lru_rows:].reshape(8, 2, CONV_WIDTH, D_MODEL)
    conv_stack = lax.dynamic_slice_in_dim(conv_stack, me * n_conv, n_conv, axis=3).reshape(8, 8, n_conv)
    res["conv_w"] = _adamw([conv_stack], conv_w.reshape(1, 8, n_conv), m_conv_w.reshape(1, 8, n_conv),
                           v_conv_w.reshape(1, 8, n_conv), 8, "adamw_conv_w")

    names = ("w_mod", "b_mod", "g_pre", "w_in", "conv_w", "conv_b", "w_rg", "b_rg", "w_ig", "b_ig",
             "lru_lambda", "w_pa", "w_pb", "w_o", "g_post")
    outs = [loss, grad_x]
    for k in range(4):
        outs.extend(res[n][k].reshape(W[n].shape) for n in names)
    return tuple(outs)
```

```python
import functools

import jax
import jax.numpy as jnp
import numpy as np
from jax import lax
from jax.experimental import pallas as pl
from jax.experimental.pallas import tpu as pltpu

F32 = jnp.float32
BF16 = jnp.bfloat16

N_DEV = 8
D_MODEL = 1024
HEAD_DIM = 128
HEADS = 4
ATT_WIDTH = HEADS * HEAD_DIM
DILATIONS = (1, 4, 16)
BAND = 128
N_CHUNKS = 18
QKV_CHUNKS = 9
CONV_WIDTH = 4
LRU_C = 8.0
NORM_EPS = 1e-6
NEG_INF = -1e30
ADAM_LR = 0.001
ADAM_B1 = 0.9
ADAM_B2 = 0.999
ADAM_EPS = 1e-08
ADAM_WD = 0.01
ADAM_STEP = 10
CHUNK_PERM = (0, 3, 6, 1, 4, 7, 2, 5, 8, 10, 11, 12, 13, 14, 15, 16, 17, 9)
DREST_CHUNKS = 10
VMEM_LIMIT = 56 * 1024 * 1024
MESH_AXES = ("x", "y", "c")


def _params(sem=None):
    return pltpu.CompilerParams(dimension_semantics=sem, vmem_limit_bytes=VMEM_LIMIT)


def _silu(x):
    return x * jax.nn.sigmoid(x)


def _dsilu(x):
    s = jax.nn.sigmoid(x)
    return s * (1.0 + x * (1.0 - s))


def _neg_expm1(x):
    series = -x * (1.0 + x * (0.5 + x * (1.0 / 6.0 + x * (1.0 / 24.0 + x * (1.0 / 120.0)))))
    return jnp.where(x > -0.05, series, 1.0 - jnp.exp(x))


def _softplus_neg(lam):
    z = jnp.exp(-jnp.abs(lam))
    small = z * (1.0 - z * (0.5 - z * (1.0 / 3.0 - z * 0.25)))
    log1p_z = jnp.where(z < 1e-2, small, jnp.log(1.0 + z))
    return jnp.maximum(-lam, 0.0) + log1p_z


def _dot(a, b):
    return jnp.dot(a, b, preferred_element_type=F32)


def _dot_nt(a, b):
    return lax.dot_general(a, b, (((1,), (1,)), ((), ())), preferred_element_type=F32)


def _dot_tn(a, b):
    return lax.dot_general(a, b, (((0,), (0,)), ((), ())), preferred_element_type=F32)


def _perm_matrix(r, rows, transpose=False):
    n = rows // r
    p = np.zeros((rows, rows), np.float32)
    dst = np.arange(rows)
    p[dst, (dst % n) * r + dst // n] = 1.0
    return jnp.asarray(p.T if transpose else p, dtype=BF16)


def _permute_f32(p, x, pieces):
    part = x.astype(BF16)
    acc = _dot(p, part)
    for _ in range(pieces - 1):
        x = x - part.astype(F32)
        part = x.astype(BF16)
        acc = acc + _dot(p, part)
    return acc


def _rows_of_residues(ref, lead=()):
    r = ref.shape[len(lead)]
    if r == 1:
        return ref[lead + (0,)]
    return jnp.concatenate([ref[lead + (rho,)] for rho in range(r)], axis=0)


def _store_residues(ref, x, lead=()):
    r = ref.shape[len(lead)]
    n = x.shape[0] // r
    for rho in range(r):
        ref[lead + (rho,)] = x[rho * n:(rho + 1) * n]


ANY_SPEC = pl.BlockSpec(memory_space=pl.ANY)


def _exchange_shapes(arrs, modes):
    return [jax.ShapeDtypeStruct(((N_DEV,) + a.shape) if mode == "ag" else a.shape, a.dtype)
            for a, mode in zip(arrs, modes)]


def _exchange_sems(n):
    return [pltpu.SemaphoreType.DMA((7 * n,)), pltpu.SemaphoreType.DMA((7 * n,)),
            pltpu.SemaphoreType.DMA((n,))]


def _exchange_copies(ins, outs, modes, sems):
    n = len(ins)
    send_sems, recv_sems, local_sems = sems
    x, y, c = lax.axis_index("x"), lax.axis_index("y"), lax.axis_index("c")
    me = 4 * x + 2 * y + c

    def src_for(a, dev):
        return ins[a] if modes[a] == "ag" else ins[a].at[dev]

    local = [pltpu.make_async_copy(src_for(a, me), outs[a].at[me], local_sems.at[a]) for a in range(n)]
    sends, arrivals = [], []
    for k in range(1, N_DEV):
        px = 1 - x if (k >> 2) & 1 else x
        py = 1 - y if (k >> 1) & 1 else y
        pc = 1 - c if k & 1 else c
        peer = 4 * px + 2 * py + pc
        for a in range(n):
            s = (k - 1) * n + a
            for dst, group in ((me, sends), (peer, arrivals)):
                group.append(pltpu.make_async_remote_copy(
                    src_ref=src_for(a, peer), dst_ref=outs[a].at[dst],
                    send_sem=send_sems.at[s], recv_sem=recv_sems.at[s],
                    device_id=(px, py, pc), device_id_type=pl.DeviceIdType.MESH))
    return local, sends, arrivals


def _exchange_start(copies):
    local, sends, _ = copies
    for cp in local + sends:
        cp.start()


def _exchange_wait(copies):
    local, sends, arrivals = copies
    for cp in arrivals:
        cp.wait_recv()
    for cp in sends:
        cp.wait_send()
    for cp in local:
        cp.wait()


def _exchange(arrs, modes, name):
    n = len(arrs)

    def body(*refs):
        copies = _exchange_copies(refs[:n], refs[n:2 * n], modes, refs[2 * n:])
        _exchange_start(copies)
        _exchange_wait(copies)

    outs = pl.pallas_call(
        body, name=name, out_shape=tuple(_exchange_shapes(arrs, modes)),
        in_specs=[ANY_SPEC] * n, out_specs=tuple([ANY_SPEC] * n),
        scratch_shapes=_exchange_sems(n),
    )(*arrs)
    return list(outs)


def _call(body, args, *, name, grid, in_specs, out_specs, out_shape, scratch_shapes=(), aliases=None,
          hosted=None):
    sem = ("arbitrary",) * len(grid)
    if hosted is None:
        outs = pl.pallas_call(
            body, name=name, grid=grid, in_specs=in_specs, out_specs=tuple(out_specs),
            out_shape=tuple(out_shape), scratch_shapes=list(scratch_shapes),
            input_output_aliases=aliases or {}, compiler_params=_params(sem))(*args)
        return list(outs), []
    x_arrs, modes = hosted
    n_in, n_out, n_scr, nx = len(args), len(out_shape), len(scratch_shapes), len(x_arrs)

    def wrapped(*refs):
        ins, x_ins = refs[:n_in], refs[n_in:n_in + nx]
        outs = refs[n_in + nx:n_in + nx + n_out]
        x_outs = refs[n_in + nx + n_out:n_in + 2 * nx + n_out]
        scr = refs[n_in + 2 * nx + n_out:n_in + 2 * nx + n_out + n_scr]
        sems = refs[n_in + 2 * nx + n_out + n_scr:]
        first = pl.program_id(0) == 0
        last = pl.program_id(0) == grid[0] - 1
        for axis in range(1, len(grid)):
            first = jnp.logical_and(first, pl.program_id(axis) == 0)
            last = jnp.logical_and(last, pl.program_id(axis) == grid[axis] - 1)

        @pl.when(first)
        def _():
            _exchange_start(_exchange_copies(x_ins, x_outs, modes, sems))

        body(*ins, *outs, *scr)

        @pl.when(last)
        def _():
            _exchange_wait(_exchange_copies(x_ins, x_outs, modes, sems))

    outs = pl.pallas_call(
        wrapped, name=name, grid=grid, in_specs=list(in_specs) + [ANY_SPEC] * nx,
        out_specs=tuple(out_specs) + tuple([ANY_SPEC] * nx),
        out_shape=tuple(out_shape) + tuple(_exchange_shapes(x_arrs, modes)),
        scratch_shapes=list(scratch_shapes) + _exchange_sems(nx),
        input_output_aliases=aliases or {}, compiler_params=_params(sem))(*args, *x_arrs)
    return list(outs[:n_out]), list(outs[n_out:])


def _mod_fwd(c_pad, w_mod):
    def body(c_ref, w_ref, o_ref):
        sc = _silu(c_ref[...]).astype(BF16)
        for layer in range(2):
            o_ref[layer] = _dot(sc, w_ref[layer].astype(BF16))

    return pl.pallas_call(
        body, name="mod_fwd", out_shape=jax.ShapeDtypeStruct((2, 16, w_mod.shape[2]), F32),
        compiler_params=_params())(c_pad, w_mod)


def _mod_wgrad(c_t, dmod):
    n_cols = dmod.shape[2]

    def body(c_ref, d_ref, o_ref):
        sc = _silu(c_ref[...]).astype(BF16).astype(F32)
        for layer in range(2):
            dm = d_ref[layer].astype(BF16).astype(F32)
            acc = sc[:, 0:1] * dm[0:1, :]
            for b in range(1, N_DEV):
                acc = acc + sc[:, b:b + 1] * dm[b:b + 1, :]
            o_ref[layer] = acc

    return pl.pallas_call(
        body, name="mod_wgrad", out_shape=jax.ShapeDtypeStruct((2, D_MODEL, n_cols), F32),
        compiler_params=_params())(c_t, dmod)


GROUP_COLS = 3 * ATT_WIDTH
PROJ_TM = 256


def _norm_proj_fwd(x, vec, w_int, hosted=None):
    S = x.shape[0]
    tm = PROJ_TM
    perms = [_perm_matrix(r, tm) for r in DILATIONS[1:]]

    def body(x_ref, v_ref, w_ref, p1_ref, p2_ref, ht_ref, q0_ref, q1_ref, q2_ref, rest_ref):
        xv = x_ref[...]
        rstd = lax.rsqrt(jnp.mean(xv * xv, axis=-1, keepdims=True) + NORM_EPS)
        hf = ((xv * rstd) * v_ref[0:1, :]) * (1.0 + v_ref[1:2, :]) + v_ref[2:3, :]
        ht_ref[...] = hf.T.astype(BF16)
        h = hf.astype(BF16)
        for g, (q_ref, p_ref) in enumerate(((q0_ref, None), (q1_ref, p1_ref), (q2_ref, p2_ref))):
            rows = h if p_ref is None else _dot(p_ref[...], h).astype(BF16)
            _store_residues(q_ref, _dot(rows, w_ref[:, g * GROUP_COLS:(g + 1) * GROUP_COLS]).astype(BF16))
        rest_ref[...] = _dot(h, w_ref[:, 3 * GROUP_COLS:])

    n_rest = QKV_CHUNKS * ATT_WIDTH
    whole = pl.BlockSpec(memory_space=pltpu.VMEM)
    return _call(
        body, (x, vec, w_int, *perms), name="norm_proj_fwd", grid=(S // tm,),
        in_specs=[pl.BlockSpec((tm, D_MODEL), lambda i: (i, 0)),
                  pl.BlockSpec((8, D_MODEL), lambda i: (0, 0)), whole, whole, whole],
        out_specs=(pl.BlockSpec((D_MODEL, tm), lambda i: (0, i)),
                   *[pl.BlockSpec((r, tm // r, GROUP_COLS), lambda i: (0, i, 0)) for r in DILATIONS],
                   pl.BlockSpec((tm, n_rest), lambda i: (i, 0))),
        out_shape=(jax.ShapeDtypeStruct((D_MODEL, S), BF16),
                   *[jax.ShapeDtypeStruct((r, S // r, GROUP_COLS), BF16) for r in DILATIONS],
                   jax.ShapeDtypeStruct((S, n_rest), F32)),
        hosted=hosted)


def _proj_dgrad_norm_bwd(dqkv, drest, w_int, x, dxn, vec, hosted=None):
    S = x.shape[0]
    tm = PROJ_TM
    perms = [_perm_matrix(r, tm, transpose=True) for r in DILATIONS[1:]]

    def body(d0_ref, d1_ref, d2_ref, dr_ref, w_ref, p1_ref, p2_ref, x_ref, dxn_ref, v_ref, dx_ref, acc_ref):
        i = pl.program_id(0)

        @pl.when(i == 0)
        def _():
            acc_ref[...] = jnp.zeros_like(acc_ref)

        def wcols(k):
            return w_ref[:, k * ATT_WIDTH:(k + 1) * ATT_WIDTH]

        dhv = None
        for g, (d_ref, p_ref) in enumerate(((d0_ref, None), (d1_ref, p1_ref), (d2_ref, p2_ref))):
            for t in range(3):
                d = _rows_of_residues(d_ref, (t,))
                if p_ref is not None:
                    d = _dot(p_ref[...], d).astype(BF16)
                part = _dot_nt(d, wcols(3 * g + t))
                dhv = part if dhv is None else dhv + part
        for t in range(7):
            dhv = dhv + _dot_nt(dr_ref[t], wcols(11 + t))
        for t in range(2):
            dhv = dhv + _dot_nt(dr_ref[8 + t], wcols(9 + t))

        xv = x_ref[...]
        g = v_ref[0:1, :]
        sc1 = 1.0 + v_ref[1:2, :]
        rstd = lax.rsqrt(jnp.mean(xv * xv, axis=-1, keepdims=True) + NORM_EPS)
        xhat = xv * rstd
        acc_ref[0:1, :] += jnp.sum(dhv, axis=0, keepdims=True)
        acc_ref[1:2, :] += jnp.sum(dhv * (xhat * g), axis=0, keepdims=True)
        acc_ref[2:3, :] += jnp.sum(dhv * xhat * sc1, axis=0, keepdims=True)
        dxhat = dhv * (g * sc1)
        dx = rstd * (dxhat - xhat * jnp.mean(dxhat * xhat, axis=-1, keepdims=True))
        dx_ref[...] = dx + dxn_ref[...]

    row = pl.BlockSpec((tm, D_MODEL), lambda i: (i, 0))
    vec_spec = pl.BlockSpec((8, D_MODEL), lambda i: (0, 0))
    whole = pl.BlockSpec(memory_space=pltpu.VMEM)
    return _call(
        body, (*dqkv, drest, w_int, *perms, x, dxn, vec), name="proj_dgrad_norm_bwd", grid=(S // tm,),
        in_specs=[*[pl.BlockSpec((3, r, tm // r, ATT_WIDTH), lambda i: (0, 0, i, 0)) for r in DILATIONS],
                  pl.BlockSpec((DREST_CHUNKS, tm, ATT_WIDTH), lambda i: (0, i, 0)),
                  whole, whole, whole, row, row, vec_spec],
        out_specs=(row, vec_spec),
        out_shape=(jax.ShapeDtypeStruct((S, D_MODEL), F32), jax.ShapeDtypeStruct((8, D_MODEL), F32)),
        hosted=hosted)


def _proj_wgrad_part(h_t, d, n_chunks, chunk_block, name):
    S = h_t.shape[1]
    ts = 512

    def body(h_ref, d_ref, o_ref):
        i = pl.program_id(0)

        @pl.when(i == 0)
        def _():
            o_ref[...] = jnp.zeros_like(o_ref)

        ht = h_ref[...]
        for t in range(n_chunks):
            o_ref[:, t * ATT_WIDTH:(t + 1) * ATT_WIDTH] += _dot(ht, d_ref[t])

    return pl.pallas_call(
        body, name=name, grid=(S // ts,),
        in_specs=[pl.BlockSpec((D_MODEL, ts), lambda i: (0, i)),
                  pl.BlockSpec((n_chunks, ts, ATT_WIDTH), lambda i: (chunk_block, i, 0))],
        out_specs=pl.BlockSpec((D_MODEL, n_chunks * ATT_WIDTH), lambda i: (0, 0)),
        out_shape=jax.ShapeDtypeStruct((D_MODEL, n_chunks * ATT_WIDTH), F32),
        compiler_params=_params(("arbitrary",)))(h_t, d)


def _proj_wgrad_group(h_t, d, r, name):
    S = h_t.shape[1]
    ts = 512
    back = _perm_matrix(r, ts, transpose=True)

    def body(h_ref, d_ref, p_ref, o_ref):
        i = pl.program_id(0)

        @pl.when(i == 0)
        def _():
            o_ref[...] = jnp.zeros_like(o_ref)

        ht = h_ref[...]
        for t in range(3):
            rows = _rows_of_residues(d_ref, (t,))
            if r > 1:
                rows = _dot(p_ref[...], rows).astype(BF16)
            o_ref[:, t * ATT_WIDTH:(t + 1) * ATT_WIDTH] += _dot(ht, rows)

    return pl.pallas_call(
        body, name=name, grid=(S // ts,),
        in_specs=[pl.BlockSpec((D_MODEL, ts), lambda i: (0, i)),
                  pl.BlockSpec((3, r, ts // r, ATT_WIDTH), lambda i: (0, 0, i, 0)),
                  pl.BlockSpec(memory_space=pltpu.VMEM)],
        out_specs=pl.BlockSpec((D_MODEL, GROUP_COLS), lambda i: (0, 0)),
        out_shape=jax.ShapeDtypeStruct((D_MODEL, GROUP_COLS), F32),
        compiler_params=_params(("arbitrary",)))(h_t, d, back)


def _wgrad_tn(a, b, name):
    S, M = a.shape
    N = b.shape[1]
    ts = 1024

    def body(a_ref, b_ref, o_ref):
        i = pl.program_id(0)
        part = _dot_tn(a_ref[...], b_ref[...])

        @pl.when(i == 0)
        def _():
            o_ref[...] = part

        @pl.when(i > 0)
        def _():
            o_ref[...] += part

    return pl.pallas_call(
        body, name=name, grid=(S // ts,),
        in_specs=[pl.BlockSpec((ts, M), lambda i: (i, 0)), pl.BlockSpec((ts, N), lambda i: (i, 0))],
        out_specs=pl.BlockSpec((M, N), lambda i: (0, 0)),
        out_shape=jax.ShapeDtypeStruct((M, N), F32),
        compiler_params=_params(("arbitrary",)))(a, b)


def _attn_fwd(qkv, g, r):
    L = qkv.shape[1]
    nb = L // BAND
    scale = HEAD_DIM ** -0.5

    def body(q_ref, kp_ref, kc_ref, vp_ref, vc_ref, o_ref, st_ref):
        n = pl.program_id(1)
        ii = lax.broadcasted_iota(jnp.int32, (BAND, 2 * BAND), 0)
        kk = lax.broadcasted_iota(jnp.int32, (BAND, 2 * BAND), 1)
        mask = jnp.logical_and(jnp.logical_and(kk >= ii, kk <= ii + BAND),
                               jnp.logical_or(kk >= BAND, n > 0))
        lane = lax.broadcasted_iota(jnp.int32, (BAND, 128), 1)
        stats = jnp.zeros((BAND, 128), F32)
        for h in range(HEADS):
            sl = slice(h * HEAD_DIM, (h + 1) * HEAD_DIM)
            k_cat = jnp.concatenate([kp_ref[:, sl], kc_ref[:, sl]], axis=0)
            v_cat = jnp.concatenate([vp_ref[:, sl], vc_ref[:, sl]], axis=0)
            s = jnp.where(mask, _dot_nt(q_ref[:, sl], k_cat) * scale, NEG_INF)
            m = jnp.max(s, axis=1, keepdims=True)
            p = jnp.exp(s - m)
            l = jnp.sum(p, axis=1, keepdims=True)
            o_ref[:, sl] = _dot(p.astype(BF16), v_cat) / l
            stats = jnp.where(lane == h, m, stats)
            stats = jnp.where(lane == HEADS + h, l, stats)
        st_ref[...] = stats

    blk = (None, BAND, ATT_WIDTH)
    return pl.pallas_call(
        body, name=f"attn_fwd_g{g}", grid=(r, nb),
        in_specs=[
            pl.BlockSpec(blk, lambda rho, n: (rho, n, 0)),
            pl.BlockSpec(blk, lambda rho, n: (rho, jnp.maximum(n - 1, 0), 1)),
            pl.BlockSpec(blk, lambda rho, n: (rho, n, 1)),
            pl.BlockSpec(blk, lambda rho, n: (rho, jnp.maximum(n - 1, 0), 2)),
            pl.BlockSpec(blk, lambda rho, n: (rho, n, 2)),
        ],
        out_specs=(pl.BlockSpec(blk, lambda rho, n: (rho, n, 0)),
                   pl.BlockSpec((None, BAND, 128), lambda rho, n: (rho, n, 0))),
        out_shape=(jax.ShapeDtypeStruct((r, L, ATT_WIDTH), F32),
                   jax.ShapeDtypeStruct((r, L, 128), F32)),
        compiler_params=_params(("parallel", "parallel")))(qkv, qkv, qkv, qkv, qkv)


def _attn_combine(outs, stats, rest):
    S = rest.shape[0]
    tm = 512
    gatt_blk = 8
    back = [_perm_matrix(r, tm, transpose=True) for r in DILATIONS[1:]]
    forth = [_perm_matrix(r, tm) for r in DILATIONS[1:]]

    def body(o0_ref, o1_ref, o2_ref, s0_ref, s1_ref, s2_ref, g_ref, b1_ref, b2_ref, f1_ref, f2_ref,
             o_ref, a_ref, l0_ref, l1_ref, l2_ref):
        outs_nat = [o0_ref[0]] + [_permute_f32(b_ref[...], _rows_of_residues(o_g), 2)
                                  for o_g, b_ref in ((o1_ref, b1_ref), (o2_ref, b2_ref))]
        st = [s0_ref[0]] + [_permute_f32(b_ref[...], _rows_of_residues(s_g), 3)
                            for s_g, b_ref in ((s1_ref, b1_ref), (s2_ref, b2_ref))]
        lane = lax.broadcasted_iota(jnp.int32, (tm, 128), 1)
        lse_out = jnp.zeros((tm, 128), F32)
        for h in range(HEADS):
            sl = slice(h * HEAD_DIM, (h + 1) * HEAD_DIM)
            ms = [s[:, h:h + 1] for s in st]
            ls = [s[:, HEADS + h:HEADS + h + 1] for s in st]
            m_all = jnp.maximum(jnp.maximum(ms[0], ms[1]), ms[2])
            ws = [l * jnp.exp(m - m_all) for m, l in zip(ms, ls)]
            den = ws[0] + ws[1] + ws[2]
            o = (ws[0] * outs_nat[0][:, sl] + ws[1] * outs_nat[1][:, sl] + ws[2] * outs_nat[2][:, sl]) / den
            o_ref[:, sl] = o
            a_ref[:, sl] = (o * _silu(g_ref[:, sl])).astype(BF16)
            lse_out = jnp.where(lane == h, m_all + jnp.log(den), lse_out)
        l0_ref[0] = lse_out
        for l_ref, f_ref in ((l1_ref, f1_ref), (l2_ref, f2_ref)):
            _store_residues(l_ref, _permute_f32(f_ref[...], lse_out, 3))

    o_spec = pl.BlockSpec((tm, ATT_WIDTH), lambda i: (i, 0))
    whole = pl.BlockSpec(memory_space=pltpu.VMEM)

    def res_spec(r, width):
        return pl.BlockSpec((r, tm // r, width), lambda i: (0, i, 0))

    return pl.pallas_call(
        body, name="attn_combine", grid=(S // tm,),
        in_specs=[*[res_spec(r, ATT_WIDTH) for r in DILATIONS], *[res_spec(r, 128) for r in DILATIONS],
                  pl.BlockSpec((tm, ATT_WIDTH), lambda i: (i, gatt_blk)), whole, whole, whole, whole],
        out_specs=(o_spec, o_spec, *[res_spec(r, 128) for r in DILATIONS]),
        out_shape=(jax.ShapeDtypeStruct((S, ATT_WIDTH), F32), jax.ShapeDtypeStruct((S, ATT_WIDTH), BF16),
                   *[jax.ShapeDtypeStruct((r, S // r, 128), F32) for r in DILATIONS]),
        compiler_params=_params(("parallel",)))(*outs, *stats, rest, *back, *forth)


def _attn_bwd(qkv, do, lse, dvec, g, r, hosted=None):
    L = qkv.shape[1]
    nb = L // BAND
    scale = HEAD_DIM ** -0.5

    def body(qc_ref, qn_ref, k_ref, v_ref, doc_ref, don_ref, lc_ref, ln_ref, dc_ref, dn_ref,
             out_ref, carry_ref):
        j = pl.program_id(1)

        @pl.when(j == 0)
        def _():
            carry_ref[...] = jnp.zeros_like(carry_ref)

        ii = lax.broadcasted_iota(jnp.int32, (2 * BAND, BAND), 0)
        kk = lax.broadcasted_iota(jnp.int32, (2 * BAND, BAND), 1)
        mask = jnp.logical_or(
            jnp.logical_and(ii < BAND, kk <= ii),
            jnp.logical_and(jnp.logical_and(ii >= BAND, kk >= ii - BAND), j < nb - 1))
        lse2 = jnp.concatenate([lc_ref[...], ln_ref[...]], axis=0)
        dvec2 = jnp.concatenate([dc_ref[...], dn_ref[...]], axis=0)
        for h in range(HEADS):
            sl = slice(h * HEAD_DIM, (h + 1) * HEAD_DIM)
            k = k_ref[:, sl]
            v = v_ref[:, sl]
            q = jnp.concatenate([qc_ref[:, sl], qn_ref[:, sl]], axis=0)
            do = jnp.concatenate([doc_ref[:, sl], don_ref[:, sl]], axis=0)
            p = jnp.where(mask, jnp.exp(_dot_nt(q, k) * scale - lse2[:, h:h + 1]), 0.0)
            ds = (p * (_dot_nt(do, v) - dvec2[:, h:h + 1])).astype(BF16)
            dq2 = _dot(ds, k) * scale
            out_ref[0, :, sl] = (carry_ref[:, sl] + dq2[:BAND]).astype(BF16)
            out_ref[1, :, sl] = (_dot_tn(ds, q) * scale).astype(BF16)
            out_ref[2, :, sl] = _dot_tn(p.astype(BF16), do).astype(BF16)
            carry_ref[:, sl] = dq2[BAND:]

    blk = (None, BAND, ATT_WIDTH)
    sblk = (None, BAND, 128)

    def nxt(j):
        return jnp.minimum(j + 1, nb - 1)

    (out,), got = _call(
        body, (qkv, qkv, qkv, qkv, do, do, lse, lse, dvec, dvec), name=f"attn_bwd_g{g}", grid=(r, nb),
        in_specs=[
            pl.BlockSpec(blk, lambda rho, j: (rho, j, 0)),
            pl.BlockSpec(blk, lambda rho, j: (rho, nxt(j), 0)),
            pl.BlockSpec(blk, lambda rho, j: (rho, j, 1)),
            pl.BlockSpec(blk, lambda rho, j: (rho, j, 2)),
            pl.BlockSpec(blk, lambda rho, j: (rho, j, 0)),
            pl.BlockSpec(blk, lambda rho, j: (rho, nxt(j), 0)),
            pl.BlockSpec(sblk, lambda rho, j: (rho, j, 0)),
            pl.BlockSpec(sblk, lambda rho, j: (rho, nxt(j), 0)),
            pl.BlockSpec(sblk, lambda rho, j: (rho, j, 0)),
            pl.BlockSpec(sblk, lambda rho, j: (rho, nxt(j), 0)),
        ],
        out_specs=(pl.BlockSpec((3, None, BAND, ATT_WIDTH), lambda rho, j: (0, rho, j, 0)),),
        out_shape=(jax.ShapeDtypeStruct((3, r, L, ATT_WIDTH), BF16),),
        scratch_shapes=[pltpu.VMEM((BAND, ATT_WIDTH), F32)],
        hosted=hosted)
    return out, got


LRU_T = 256


def _linear_scan(a, b, carry, reverse):
    T = a.shape[0]
    row8 = lax.broadcasted_iota(jnp.int32, a.shape, 0) & 7
    for s in (1, 2, 4):
        keep = (row8 < 8 - s) if reverse else (row8 >= s)
        shift = T - s if reverse else s
        b_sh = jnp.where(keep, pltpu.roll(b, shift, 0), 0.0)
        a_sh = jnp.where(keep, pltpu.roll(a, shift, 0), 1.0)
        b = a * b_sh + b
        a = a * a_sh
    tiles = [None] * (T // 8)
    order = range(T // 8 - 1, -1, -1) if reverse else range(T // 8)
    for k in order:
        y = b[8 * k:8 * k + 8] + a[8 * k:8 * k + 8] * carry
        tiles[k] = y
        carry = y[0:1] if reverse else y[7:8]
    return jnp.concatenate(tiles, axis=0), carry


def _gate_matmuls(ucb, w_ref, bias):
    parts = [_dot(ucb[:, j * 128:(j + 1) * 128], w_ref[j]) for j in range(8)]
    return jnp.concatenate(parts, axis=1) + bias


def _conv_fwd(u, u_prev8, cw_ref):
    T = u.shape[0]
    ue = jnp.concatenate([u_prev8, u], axis=0)
    uc = cw_ref[4:5, :] + cw_ref[0:1, :] * u
    for j in range(1, CONV_WIDTH):
        uc = uc + cw_ref[j:j + 1, :] * pltpu.roll(ue, j, 0)[8:8 + T]
    return uc


def _lru_fwd(rest, cw, wr2, wi2, hosted=None):
    S = rest.shape[0]
    T = LRU_T

    def body(u_ref, cw_ref, wr_ref, wi_ref, h_ref, a_ref, ucar_ref, hcar_ref):
        c = pl.program_id(0)

        @pl.when(c == 0)
        def _():
            ucar_ref[...] = jnp.zeros_like(ucar_ref)
            hcar_ref[...] = jnp.zeros_like(hcar_ref)

        u = u_ref[...]
        uc = _conv_fwd(u, ucar_ref[...], cw_ref)
        ucar_ref[...] = u[T - 8:, :]
        ucb = uc.astype(BF16)
        r = jax.nn.sigmoid(_gate_matmuls(ucb, wr_ref, cw_ref[5:6, :]))
        ig = jax.nn.sigmoid(_gate_matmuls(ucb, wi_ref, cw_ref[6:7, :]))
        log_a = -LRU_C * r * _softplus_neg(cw_ref[7:8, :])
        a = jnp.exp(log_a)
        b = jnp.sqrt(_neg_expm1(2.0 * log_a)) * (ig * uc)
        a_ref[...] = a
        h, last = _linear_scan(a, b, hcar_ref[0:1, :], reverse=False)
        h_ref[...] = h
        hcar_ref[...] = jnp.broadcast_to(last, (8, D_MODEL))

    row_spec = pl.BlockSpec((T, D_MODEL), lambda c: (c, 0))
    return _call(
        body, (rest, cw, wr2, wi2), name="lru_fwd", grid=(S // T,),
        in_specs=[row_spec, pl.BlockSpec((8, D_MODEL), lambda c: (0, 0)),
                  pl.BlockSpec((8, 128, 128), lambda c: (0, 0, 0)),
                  pl.BlockSpec((8, 128, 128), lambda c: (0, 0, 0))],
        out_specs=(row_spec, row_spec),
        out_shape=(jax.ShapeDtypeStruct((S, D_MODEL), F32), jax.ShapeDtypeStruct((S, D_MODEL), F32)),
        scratch_shapes=[pltpu.VMEM((8, D_MODEL), F32), pltpu.VMEM((8, D_MODEL), F32)],
        hosted=hosted)


def _lru_bwd(dhl, hl, a_all, rest, cw, wr2, wi2, dproj, hosted=None):
    S = rest.shape[0]
    T = LRU_T
    nc = S // T

    def body(dh_ref, h_ref, hp_ref, a_ref, u_ref, up_ref, cw_ref, wr_ref, wi_ref, _alias,
             du_ref, acc_ref, gwr_ref, gwi_ref, gcar_ref, acar_ref, dcar_ref):
        step = pl.program_id(0)
        c = nc - 1 - step

        @pl.when(step == 0)
        def _():
            acc_ref[...] = jnp.zeros_like(acc_ref)
            gwr_ref[...] = jnp.zeros_like(gwr_ref)
            gwi_ref[...] = jnp.zeros_like(gwi_ref)
            gcar_ref[...] = jnp.zeros_like(gcar_ref)
            acar_ref[...] = jnp.zeros_like(acar_ref)
            dcar_ref[...] = jnp.zeros_like(dcar_ref)

        row = lax.broadcasted_iota(jnp.int32, (T, D_MODEL), 0)
        u = u_ref[...]
        u_prev = jnp.where(c > 0, up_ref[...], 0.0)
        h_prev8 = jnp.where(c > 0, hp_ref[...], 0.0)
        a = a_ref[...]
        h = h_ref[...]
        uc = _conv_fwd(u, u_prev, cw_ref)
        ucb = uc.astype(BF16)
        r = jax.nn.sigmoid(_gate_matmuls(ucb, wr_ref, cw_ref[5:6, :]))
        ig = jax.nn.sigmoid(_gate_matmuls(ucb, wi_ref, cw_ref[6:7, :]))
        sp = _softplus_neg(cw_ref[7:8, :])
        log_a = -LRU_C * r * sp
        mult = jnp.sqrt(_neg_expm1(2.0 * log_a))
        a_next = jnp.where(row < T - 1, pltpu.roll(a, T - 1, 0), acar_ref[0:1, :])
        G, first = _linear_scan(a_next, dh_ref[...], gcar_ref[0:1, :], reverse=True)
        gcar_ref[...] = jnp.broadcast_to(first, (8, D_MODEL))
        acar_ref[...] = jnp.broadcast_to(a[0:1, :], (8, D_MODEL))
        he = jnp.concatenate([h_prev8, h], axis=0)
        h_before = pltpu.roll(he, 1, 0)[8:8 + T]
        d_a = G * h_before
        d_mult = G * (ig * uc)
        d_ig = G * (mult * uc)
        duc = G * (mult * ig)
        d_log_a = d_a * a - d_mult * (a * a) / mult
        d_r = d_log_a * (-LRU_C * sp)
        d_sp = jnp.sum(d_log_a * (-LRU_C * r), axis=0, keepdims=True)
        dpre_r = d_r * r * (1.0 - r)
        dpre_i = d_ig * ig * (1.0 - ig)
        dprb = dpre_r.astype(BF16)
        dpib = dpre_i.astype(BF16)
        back = []
        for j in range(8):
            sl = slice(j * 128, (j + 1) * 128)
            back.append(_dot_nt(dprb[:, sl], wr_ref[j]) + _dot_nt(dpib[:, sl], wi_ref[j]))
            gwr_ref[j] += _dot_tn(ucb[:, sl], dprb[:, sl])
            gwi_ref[j] += _dot_tn(ucb[:, sl], dpib[:, sl])
        duc = duc + jnp.concatenate(back, axis=1)
        de = jnp.concatenate([duc, dcar_ref[...]], axis=0)
        du = cw_ref[0:1, :] * duc
        ue = jnp.concatenate([u_prev, u], axis=0)
        acc_ref[0:1, :] += jnp.sum(duc * u, axis=0, keepdims=True)
        for j in range(1, CONV_WIDTH):
            du = du + cw_ref[j:j + 1, :] * pltpu.roll(de, T + 8 - j, 0)[0:T]
            acc_ref[j:j + 1, :] += jnp.sum(duc * pltpu.roll(ue, j, 0)[8:8 + T], axis=0, keepdims=True)
        dcar_ref[...] = duc[0:8, :]
        acc_ref[4:5, :] += jnp.sum(duc, axis=0, keepdims=True)
        acc_ref[5:6, :] += jnp.sum(dpre_r, axis=0, keepdims=True)
        acc_ref[6:7, :] += jnp.sum(dpre_i, axis=0, keepdims=True)
        acc_ref[7:8, :] += d_sp
        du_ref[0] = du[:, 0:ATT_WIDTH].astype(BF16)
        du_ref[1] = du[:, ATT_WIDTH:].astype(BF16)

    def rev(step):
        return nc - 1 - step

    def prev8(step):
        return jnp.maximum(rev(step) * (T // 8) - 1, 0)

    row_spec = pl.BlockSpec((T, D_MODEL), lambda s: (rev(s), 0))
    prev_spec = pl.BlockSpec((8, D_MODEL), lambda s: (prev8(s), 0))
    vec_spec = pl.BlockSpec((8, D_MODEL), lambda s: (0, 0))
    w_spec = pl.BlockSpec((8, 128, 128), lambda s: (0, 0, 0))
    return _call(
        body, (dhl, hl, hl, a_all, rest, rest, cw, wr2, wi2, dproj), name="lru_bwd", grid=(nc,),
        in_specs=[row_spec, row_spec, prev_spec, row_spec, row_spec, prev_spec, vec_spec, w_spec, w_spec,
                  ANY_SPEC],
        out_specs=(pl.BlockSpec((2, T, ATT_WIDTH), lambda s: (4, rev(s), 0)), vec_spec, w_spec, w_spec),
        out_shape=(jax.ShapeDtypeStruct(dproj.shape, BF16), jax.ShapeDtypeStruct((8, D_MODEL), F32),
                   jax.ShapeDtypeStruct((8, 128, 128), F32), jax.ShapeDtypeStruct((8, 128, 128), F32)),
        scratch_shapes=[pltpu.VMEM((8, D_MODEL), F32), pltpu.VMEM((8, D_MODEL), F32),
                        pltpu.VMEM((8, D_MODEL), F32)],
        aliases={9: 0}, hosted=hosted)


OUT_TM = 256


def _out_fwd(a_gated, hl, rest, x, vec, w_pa, w_pb, w_o):
    S = x.shape[0]
    tm = OUT_TM

    def body(ag_ref, hl_ref, gl_ref, ma_ref, mb_ref, x_ref, v_ref, wpa_ref, wpb_ref, wo_ref,
             xn_ref, mix_ref, ya_ref, yb_ref, bg_ref):
        bg = (hl_ref[...] * _silu(gl_ref[...])).astype(BF16)
        ya = _dot(ag_ref[...], wpa_ref[...])
        yb = _dot(bg, wpb_ref[...])
        mix = (jax.nn.sigmoid(ma_ref[...]) * ya + jax.nn.sigmoid(mb_ref[...]) * yb).astype(BF16)
        out = _dot(mix, wo_ref[...])
        rstd = lax.rsqrt(jnp.mean(out * out, axis=-1, keepdims=True) + NORM_EPS)
        xn_ref[...] = x_ref[...] + v_ref[1:2, :] * ((out * rstd) * v_ref[0:1, :])
        mix_ref[...] = mix
        ya_ref[...] = ya.astype(BF16)
        yb_ref[...] = yb.astype(BF16)
        bg_ref[...] = bg

    def col(j):
        return pl.BlockSpec((tm, D_MODEL), lambda i: (i, j))

    def whole(shape):
        return pl.BlockSpec(shape, lambda i: (0, 0))

    row = col(0)
    bf = jax.ShapeDtypeStruct((S, D_MODEL), BF16)
    return pl.pallas_call(
        body, name="out_fwd", grid=(S // tm,),
        in_specs=[pl.BlockSpec((tm, ATT_WIDTH), lambda i: (i, 0)), row, col(1), col(2), col(3), row,
                  whole((8, D_MODEL)), whole((ATT_WIDTH, D_MODEL)), whole((D_MODEL, D_MODEL)),
                  whole((D_MODEL, D_MODEL))],
        out_specs=(row, row, row, row, row),
        out_shape=(jax.ShapeDtypeStruct((S, D_MODEL), F32), bf, bf, bf, bf),
        compiler_params=_params(("parallel",)))(a_gated, hl, rest, rest, rest, x, vec, w_pa, w_pb, w_o)


def _out_bwd(dxn, mix, ya, yb, hl, rest, o_att, vec, w_pa, w_pb, w_o, hosted=None):
    S = dxn.shape[0]
    tm = OUT_TM
    forth = [_perm_matrix(r, tm) for r in DILATIONS[1:]]

    def body(dxn_ref, mix_ref, ya_ref, yb_ref, hl_ref, gl_ref, ma_ref, mb_ref, ga_ref, o_ref, v_ref,
             wpa_ref, wpb_ref, wo_ref, f1_ref, f2_ref,
             dout_ref, dya_ref, dyb_ref, dhl_ref, dp_ref, acc_ref,
             do0_ref, do1_ref, do2_ref, dv0_ref, dv1_ref, dv2_ref):
        i = pl.program_id(0)

        @pl.when(i == 0)
        def _():
            acc_ref[...] = jnp.zeros_like(acc_ref)

        g_post = v_ref[0:1, :]
        gate = v_ref[1:2, :]
        dxn_v = dxn_ref[...]
        out = _dot(mix_ref[...], wo_ref[...])
        rstd = lax.rsqrt(jnp.mean(out * out, axis=-1, keepdims=True) + NORM_EPS)
        nrm = out * rstd
        acc_ref[0:1, :] += jnp.sum(dxn_v * (nrm * g_post), axis=0, keepdims=True)
        acc_ref[1:2, :] += jnp.sum(dxn_v * gate * nrm, axis=0, keepdims=True)
        dn = dxn_v * (gate * g_post)
        dout = (rstd * (dn - nrm * jnp.mean(dn * nrm, axis=-1, keepdims=True))).astype(BF16)
        dout_ref[...] = dout
        dmix = _dot_nt(dout, wo_ref[...])
        sa = jax.nn.sigmoid(ma_ref[...])
        sb = jax.nn.sigmoid(mb_ref[...])
        dya = (dmix * sa).astype(BF16)
        dyb = (dmix * sb).astype(BF16)
        dya_ref[...] = dya
        dyb_ref[...] = dyb
        dma = dmix * ya_ref[...].astype(F32) * (sa * (1.0 - sa))
        dmb = dmix * yb_ref[...].astype(F32) * (sb * (1.0 - sb))
        d_ag = _dot_nt(dya, wpa_ref[...])
        d_bg = _dot_nt(dyb, wpb_ref[...])
        gl = gl_ref[...]
        hl_v = hl_ref[...]
        dhl_ref[...] = d_bg * _silu(gl)
        dgl = d_bg * hl_v * _dsilu(gl)
        ga = ga_ref[...]
        o = o_ref[...]
        do = d_ag * _silu(ga)
        dga = d_ag * o * _dsilu(ga)
        lane = lax.broadcasted_iota(jnp.int32, (tm, 128), 1)
        dvec = jnp.zeros((tm, 128), F32)
        for h in range(HEADS):
            sl = slice(h * HEAD_DIM, (h + 1) * HEAD_DIM)
            dvec = jnp.where(lane == h, jnp.sum(do[:, sl] * o[:, sl], axis=1, keepdims=True), dvec)
        do_b = do.astype(BF16)
        do0_ref[0] = do_b
        dv0_ref[0] = dvec
        for do_ref, dv_ref, f_ref in ((do1_ref, dv1_ref, f1_ref), (do2_ref, dv2_ref, f2_ref)):
            _store_residues(do_ref, _dot(f_ref[...], do_b).astype(BF16))
            _store_residues(dv_ref, _permute_f32(f_ref[...], dvec, 3))
        dp_ref[0] = dgl[:, 0:ATT_WIDTH].astype(BF16)
        dp_ref[1] = dgl[:, ATT_WIDTH:].astype(BF16)
        dp_ref[2] = dma[:, 0:ATT_WIDTH].astype(BF16)
        dp_ref[3] = dma[:, ATT_WIDTH:].astype(BF16)
        dp_ref[4] = dmb[:, 0:ATT_WIDTH].astype(BF16)
        dp_ref[5] = dmb[:, ATT_WIDTH:].astype(BF16)
        dp_ref[6] = dga.astype(BF16)

    def col(j):
        return pl.BlockSpec((tm, D_MODEL), lambda i: (i, j))

    def whole(shape):
        return pl.BlockSpec(shape, lambda i: (0, 0))

    def res_spec(r, width):
        return pl.BlockSpec((r, tm // r, width), lambda i: (0, i, 0))

    row = col(0)
    att = pl.BlockSpec((tm, ATT_WIDTH), lambda i: (i, 0))
    bf = jax.ShapeDtypeStruct((S, D_MODEL), BF16)
    vmem = pl.BlockSpec(memory_space=pltpu.VMEM)
    return _call(
        body, (dxn, mix, ya, yb, hl, rest, rest, rest, rest, o_att, vec, w_pa, w_pb, w_o, *forth),
        name="out_bwd", grid=(S // tm,),
        in_specs=[row, row, row, row, row, col(1), col(2), col(3),
                  pl.BlockSpec((tm, ATT_WIDTH), lambda i: (i, 8)), att,
                  whole((8, D_MODEL)), whole((ATT_WIDTH, D_MODEL)), whole((D_MODEL, D_MODEL)),
                  whole((D_MODEL, D_MODEL)), vmem, vmem],
        out_specs=(row, row, row, row,
                   pl.BlockSpec((7, tm, ATT_WIDTH), lambda i: (0, i, 0)), whole((8, D_MODEL)),
                   *[res_spec(r, ATT_WIDTH) for r in DILATIONS], *[res_spec(r, 128) for r in DILATIONS]),
        out_shape=(bf, bf, bf, jax.ShapeDtypeStruct((S, D_MODEL), F32),
                   jax.ShapeDtypeStruct((DREST_CHUNKS, S, ATT_WIDTH), BF16),
                   jax.ShapeDtypeStruct((8, D_MODEL), F32),
                   *[jax.ShapeDtypeStruct((r, S // r, ATT_WIDTH), BF16) for r in DILATIONS],
                   *[jax.ShapeDtypeStruct((r, S // r, 128), F32) for r in DILATIONS]),
        hosted=hosted)


def _loss_head(y, target):
    S = y.shape[0]
    tm = 512

    def body(y_ref, t_ref, dy_ref, acc_ref):
        i = pl.program_id(0)

        @pl.when(i == 0)
        def _():
            acc_ref[...] = jnp.zeros_like(acc_ref)

        err = y_ref[...] - t_ref[...]
        dy_ref[...] = err * (1.0 / D_MODEL)
        part = jnp.sum(jnp.sum(err * err, axis=1, keepdims=True), axis=0, keepdims=True)
        acc_ref[...] += jnp.broadcast_to(part, acc_ref.shape)

    row = pl.BlockSpec((tm, D_MODEL), lambda i: (i, 0))
    return pl.pallas_call(
        body, name="loss_head", grid=(S // tm,),
        in_specs=[row, row],
        out_specs=(row, pl.BlockSpec((8, 128), lambda i: (0, 0))),
        out_shape=(jax.ShapeDtypeStruct((S, D_MODEL), F32), jax.ShapeDtypeStruct((8, 128), F32)),
        compiler_params=_params(("arbitrary",)))(y, target)


def _adamw(gsrcs, w, m, v, n_stack, name, rows=None):
    n_layers, R, C = w.shape
    assert len(gsrcs) == n_layers
    budget = 96 * 1024
    tr = rows
    if tr is None:
        tr = R
        while tr * C > budget and tr % 16 == 0:
            tr //= 2
    assert R % tr == 0 and (tr % 8 == 0 or tr == R)
    c1 = 1.0 - ADAM_B1 ** ADAM_STEP
    c2 = 1.0 - ADAM_B2 ** ADAM_STEP

    def body(*refs):
        g_refs = refs[:n_layers]
        w_ref, m_ref, v_ref, go_ref, d_ref, mo_ref, vo_ref = refs[n_layers:]

        def update(g_ref):
            if n_stack:
                g = g_ref[0].astype(F32)
                for s in range(1, n_stack):
                    g = g + g_ref[s].astype(F32)
            else:
                g = g_ref[...]
            m_new = ADAM_B1 * m_ref[...] + (1.0 - ADAM_B1) * g
            v_new = ADAM_B2 * v_ref[...] + (1.0 - ADAM_B2) * (g * g)
            m_hat = m_new / c1
            v_hat = v_new / c2
            go_ref[...] = g
            d_ref[...] = -ADAM_LR * (m_hat / (jnp.sqrt(v_hat) + ADAM_EPS) + ADAM_WD * w_ref[...])
            mo_ref[...] = m_new
            vo_ref[...] = v_new

        for layer in range(n_layers):
            pl.when(pl.program_id(0) == layer)(functools.partial(update, g_refs[layer]))

    def g_spec(layer):
        def rows_of(l, i):
            return jnp.where(l == layer, i, 0)
        if n_stack:
            return pl.BlockSpec((n_stack, tr, C), lambda l, i: (0, rows_of(l, i), 0))
        return pl.BlockSpec((tr, C), lambda l, i: (rows_of(l, i), 0))

    blk = pl.BlockSpec((None, tr, C), lambda l, i: (l, i, 0))
    shp = jax.ShapeDtypeStruct((n_layers, R, C), F32)
    return pl.pallas_call(
        body, name=name, grid=(n_layers, R // tr),
        in_specs=[g_spec(layer) for layer in range(n_layers)] + [blk, blk, blk],
        out_specs=(blk, blk, blk, blk), out_shape=(shp, shp, shp, shp),
        compiler_params=_params(("arbitrary", "arbitrary")))(*gsrcs, w, m, v)


def _pair_blocks(w):
    w = w.reshape(8, 2, 64, 64)
    z = jnp.zeros((8, 64, 64), w.dtype)
    top = jnp.concatenate([w[:, 0], z], axis=2)
    bot = jnp.concatenate([z, w[:, 1]], axis=2)
    return jnp.concatenate([top, bot], axis=1).astype(BF16)


def _unpair_blocks(g):
    return jnp.stack([g[:, :64, :64], g[:, 64:, 64:]], axis=1).reshape(16, 64, 64)


def _layer_params(w_in_g, w_pa_g, w_pb_g, w_o_g, conv_w_full, conv_b, b_rg, b_ig, lru_lambda, w_rg, w_ig,
                  g_pre, g_post):
    w_full = jnp.transpose(w_in_g, (1, 0, 2)).reshape(D_MODEL, N_CHUNKS, ATT_WIDTH)
    w_int = w_full[:, jnp.array(CHUNK_PERM)].reshape(D_MODEL, N_CHUNKS * ATT_WIDTH)
    cw = jnp.concatenate([conv_w_full, conv_b[None], b_rg[None], b_ig[None], lru_lambda[None]], axis=0)
    return dict(
        w_int=w_int,
        w_pa=jnp.transpose(w_pa_g, (1, 0, 2)).reshape(ATT_WIDTH, D_MODEL),
        w_pb=w_pb_g.reshape(D_MODEL, D_MODEL), w_o=w_o_g.reshape(D_MODEL, D_MODEL),
        cw=cw, wr2=_pair_blocks(w_rg), wi2=_pair_blocks(w_ig),
        g_pre=g_pre, g_post=g_post, lru_lambda=lru_lambda)


def _layer_fwd(x, mod, p, host_proj=None, host_lru=None):
    zeros = jnp.zeros((5, D_MODEL), F32)
    vec_pre = jnp.concatenate([p["g_pre"][None], mod[1:2], mod[0:1], zeros], axis=0)
    (h_t, q0, q1, q2, rest), got_proj = _norm_proj_fwd(x, vec_pre, p["w_int"], hosted=host_proj)
    qkv, outs, stats = [q0, q1, q2], [], []
    for g, r in enumerate(DILATIONS):
        o_g, st_g = _attn_fwd(qkv[g], g, r)
        outs.append(o_g)
        stats.append(st_g)
    o_att, a_gated, *lse = _attn_combine(outs, stats, rest)
    (hl, a_dec), got_lru = _lru_fwd(rest, p["cw"], p["wr2"], p["wi2"], hosted=host_lru)
    vec_post = jnp.concatenate([p["g_post"][None], mod[2:3], jnp.zeros((6, D_MODEL), F32)], axis=0)
    x_new, mix, ya, yb, b_gated = _out_fwd(a_gated, hl, rest, x, vec_post, p["w_pa"], p["w_pb"], p["w_o"])
    saved = dict(x=x, h_t=h_t, qkv=qkv, rest=rest, o_att=o_att, a_gated=a_gated, lse=lse, hl=hl, a_dec=a_dec,
                 mix=mix, ya=ya, yb=yb, b_gated=b_gated, vec_pre=vec_pre, vec_post=vec_post)
    return x_new, saved, got_proj, got_lru


def _mats_sources(g_pa, g_pb, g_o):
    return [jnp.transpose(g_pa.astype(BF16).reshape(ATT_WIDTH, N_DEV, 128), (1, 0, 2)),
            g_pb.astype(BF16).reshape(N_DEV, 128, D_MODEL), g_o.astype(BF16).reshape(N_DEV, 128, D_MODEL)]


LRU_SMALL = ("conv_b", "w_rg", "b_rg", "w_ig", "b_ig", "lru_lambda")
MOD_SMALL = ("b_mod", "g_pre", "g_post")


def _flat_rows(arrs):
    return jnp.concatenate([a.reshape(-1) for a in arrs]).reshape(-1, 128)


def _layer_bwd(dxn, sv, p, upper=None, host_own=False):
    n_in = N_CHUNKS * ATT_WIDTH // N_DEV
    a2a3 = ["a2a"] * 3
    (dout, dya, dyb, dhl, drest, acc_out, *do_dvec), got_up_mats = _out_bwd(
        dxn, sv["mix"], sv["ya"], sv["yb"], sv["hl"], sv["rest"], sv["o_att"], sv["vec_post"],
        p["w_pa"], p["w_pb"], p["w_o"], hosted=(upper["mats"], a2a3) if upper else None)
    mats = _mats_sources(_wgrad_tn(sv["a_gated"], dya, "wgrad_pa"), _wgrad_tn(sv["b_gated"], dyb, "wgrad_pb"),
                         _wgrad_tn(sv["mix"], dout, "wgrad_o"))
    (drest, acc_lru, gwr2, gwi2), got_up_in = _lru_bwd(
        dhl, sv["hl"], sv["a_dec"], sv["rest"], p["cw"], p["wr2"], p["wi2"], drest,
        hosted=([upper["w_in"]], ["a2a"]) if upper else None)
    small = dict(
        conv_w=acc_lru[0:4], conv_b=acc_lru[4], b_rg=acc_lru[5], b_ig=acc_lru[6],
        lru_lambda=acc_lru[7] * (-jax.nn.sigmoid(-p["lru_lambda"])),
        w_rg=_unpair_blocks(gwr2), w_ig=_unpair_blocks(gwi2))
    hosts = [None, None, None]
    if host_own:
        lru_vec = _flat_rows([jnp.stack([small[n], upper["small"][n]]) for n in LRU_SMALL + ("conv_w",)])
        hosts = [(mats, a2a3), ([lru_vec], ["ag"]), None]
    dqkv, got_attn = [], []
    for g, r in enumerate(DILATIONS):
        d_g, got = _attn_bwd(sv["qkv"][g], do_dvec[g], sv["lse"][g], do_dvec[3 + g], g, r, hosted=hosts[g])
        dqkv.append(d_g)
        got_attn.append(got)
    parts = [_proj_wgrad_group(sv["h_t"], dqkv[g], r, f"proj_wgrad_g{g}") for g, r in enumerate(DILATIONS)]
    parts.append(_proj_wgrad_part(sv["h_t"], drest, 2, 4, "proj_wgrad_u"))
    parts.append(_proj_wgrad_part(sv["h_t"], drest, 7, 0, "proj_wgrad_rest"))
    gw_int = jnp.concatenate(parts, axis=1).astype(BF16).reshape(D_MODEL, N_CHUNKS, ATT_WIDTH)
    inv_perm = [CHUNK_PERM.index(k) for k in range(N_CHUNKS)]
    gw_in = jnp.transpose(gw_int[:, jnp.array(inv_perm)].reshape(D_MODEL, N_DEV, n_in), (1, 0, 2))
    (dx, acc_pre), got_in = _proj_dgrad_norm_bwd(
        dqkv, drest, p["w_int"], sv["x"], dxn, sv["vec_pre"],
        hosted=([gw_in], ["a2a"]) if host_own else None)
    small.update(dmod=jnp.concatenate([acc_pre[0], acc_pre[1], acc_out[0]]), g_pre=acc_pre[2],
                 g_post=acc_out[1])
    comm = dict(sources=dict(mats=mats, w_in=gw_in, small=small),
                own=dict(mats=got_attn[0], w_in=got_in, lru_small=got_attn[1]),
                upper=dict(mats=got_up_mats, w_in=got_up_in))
    return dx, small, comm


SMALL_NAMES = ("b_mod", "g_pre", "conv_b", "w_rg", "b_rg", "w_ig", "b_ig", "lru_lambda", "g_post")


def kernel(x, c, w_mod, b_mod, g_pre, w_in, conv_w, conv_b, w_rg, b_rg, w_ig, b_ig, lru_lambda, w_pa, w_pb, w_o, g_post, loss_target, m_w_mod, m_b_mod, m_g_pre, m_w_in, m_conv_w, m_conv_b, m_w_rg, m_b_rg, m_w_ig, m_b_ig, m_lru_lambda, m_w_pa, m_w_pb, m_w_o, m_g_post, v_w_mod, v_b_mod, v_g_pre, v_w_in, v_conv_w, v_conv_b, v_w_rg, v_b_rg, v_w_ig, v_b_ig, v_lru_lambda, v_w_pa, v_w_pb, v_w_o, v_g_post):
    W = dict(w_mod=w_mod, b_mod=b_mod, g_pre=g_pre, w_in=w_in, conv_w=conv_w, conv_b=conv_b, w_rg=w_rg,
             b_rg=b_rg, w_ig=w_ig, b_ig=b_ig, lru_lambda=lru_lambda, w_pa=w_pa, w_pb=w_pb, w_o=w_o,
             g_post=g_post)
    M = dict(w_mod=m_w_mod, b_mod=m_b_mod, g_pre=m_g_pre, w_in=m_w_in, conv_w=m_conv_w, conv_b=m_conv_b,
             w_rg=m_w_rg, b_rg=m_b_rg, w_ig=m_w_ig, b_ig=m_b_ig, lru_lambda=m_lru_lambda, w_pa=m_w_pa,
             w_pb=m_w_pb, w_o=m_w_o, g_post=m_g_post)
    V = dict(w_mod=v_w_mod, b_mod=v_b_mod, g_pre=v_g_pre, w_in=v_w_in, conv_w=v_conv_w, conv_b=v_conv_b,
             w_rg=v_w_rg, b_rg=v_b_rg, w_ig=v_w_ig, b_ig=v_b_ig, lru_lambda=v_lru_lambda, w_pa=v_w_pa,
             w_pb=v_w_pb, w_o=v_w_o, g_post=v_g_post)
    S = x.shape[1]
    me = 4 * lax.axis_index("x") + 2 * lax.axis_index("y") + lax.axis_index("c")
    n_mod = w_mod.shape[2]
    n_in = w_in.shape[2]
    n_conv = conv_w.shape[2]

    c_rows = jnp.broadcast_to(c, (8, D_MODEL))
    w_in_b, w_pa_b, w_pb_b, w_o_b = (t.astype(BF16) for t in (w_in, w_pa, w_pb, w_o))
    g_c, g_win0, g_wpa0, g_wpb0, g_wo0, g_cw = _exchange(
        [c_rows, w_in_b[0], w_pa_b[0], w_pb_b[0], w_o_b[0], conv_w], ["ag"] * 6, "gather_weights")
    c_all = g_c[:, 0, :]
    c_pad = jnp.concatenate([c_all, jnp.zeros((8, D_MODEL), F32)], axis=0)
    conv_w_full = jnp.transpose(g_cw, (1, 2, 0, 3)).reshape(2, CONV_WIDTH, D_MODEL)

    mod_cols = _mod_fwd(c_pad, w_mod)
    mod_src = jnp.transpose(mod_cols[:, :8], (1, 0, 2))
    mod_src = jnp.concatenate([mod_src, jnp.zeros((8, 6, n_mod), F32)], axis=1)
    (mod_got,) = _exchange([mod_src], ["a2a"], "scatter_mod")
    mod = jnp.transpose(mod_got[:, :2], (1, 0, 2)).reshape(2, 3 * D_MODEL) + b_mod

    def layer_params(layer, w_in_g, w_pa_g, w_pb_g, w_o_g):
        return _layer_params(w_in_g, w_pa_g, w_pb_g, w_o_g, conv_w_full[layer], conv_b[layer], b_rg[layer],
                             b_ig[layer], lru_lambda[layer], w_rg[layer], w_ig[layer], g_pre[layer],
                             g_post[layer])

    layers = [layer_params(0, g_win0, g_wpa0, g_wpb0, g_wo0), None]
    act, sv0, (g_win1,), (g_wpa1, g_wpb1, g_wo1) = _layer_fwd(
        x[0], mod[0].reshape(3, D_MODEL), layers[0],
        host_proj=([w_in_b[1]], ["ag"]), host_lru=([w_pa_b[1], w_pb_b[1], w_o_b[1]], ["ag"] * 3))
    layers[1] = layer_params(1, g_win1, g_wpa1, g_wpb1, g_wo1)
    act, sv1, _, _ = _layer_fwd(act, mod[1].reshape(3, D_MODEL), layers[1])
    dy, loss_acc = _loss_head(act, loss_target[0])
    loss = lax.psum(0.5 * loss_acc[0, 0] / D_MODEL, MESH_AXES)

    dy, small1, comm1 = _layer_bwd(dy, sv1, layers[1])
    dy, small0, comm0 = _layer_bwd(dy, sv0, layers[0], upper=comm1["sources"], host_own=True)
    grad_x = dy[None]
    r_pa, r_pb, r_o = ([comm0["own"]["mats"][k], comm0["upper"]["mats"][k]] for k in range(3))
    r_in = [comm0["own"]["w_in"][0], comm0["upper"]["w_in"][0]]
    grads = [small0, small1]

    def stack2(name):
        return jnp.stack([grads[0][name], grads[1][name]], axis=0)

    dmod = stack2("dmod")
    dmod_src = jnp.transpose(dmod.reshape(2, N_DEV, n_mod), (1, 0, 2))
    dmod_src = jnp.concatenate([dmod_src, jnp.zeros((8, 6, n_mod), F32)], axis=1)
    mod_vec = _flat_rows([dmod, stack2("g_pre"), stack2("g_post")])
    r_dmod, r_mod_small = _exchange([dmod_src, mod_vec], ["a2a", "ag"], "exchange_grads")
    (r_lru_small,) = comm0["own"]["lru_small"]

    res = {}
    res["w_in"] = _adamw(r_in, w_in, m_w_in, v_w_in, 8, "adamw_w_in")
    res["w_pa"] = _adamw(r_pa, w_pa, m_w_pa, v_w_pa, 8, "adamw_w_pa")
    res["w_pb"] = _adamw(r_pb, w_pb, m_w_pb, v_w_pb, 8, "adamw_w_pb")
    res["w_o"] = _adamw(r_o, w_o, m_w_o, v_w_o, 8, "adamw_w_o")
    dmod_all = jnp.transpose(r_dmod[:, :2], (1, 0, 2))
    gw_mod = _mod_wgrad(jnp.transpose(c_all), dmod_all)
    res["w_mod"] = _adamw([gw_mod[0], gw_mod[1]], w_mod, m_w_mod, v_w_mod, 0, "adamw_w_mod")
    for names_rep, stack, tag in ((LRU_SMALL, r_lru_small, "lru"), (MOD_SMALL, r_mod_small, "mod")):
        rows = sum(W[n].size for n in names_rep) // 128
        rep = _adamw([stack], *(_flat_rows([src[n] for n in names_rep])[None] for src in (W, M, V)), 8,
                     f"adamw_small_{tag}", rows=rows // 2 if rows % 16 == 0 else rows)
        off = 0
        for name in names_rep:
            size = W[name].size
            res[name] = tuple(t.reshape(-1)[off:off + size].reshape(W[name].shape) for t in rep)
            off += size
    lru_rows = sum(W[n].size for n in LRU_SMALL) // 128
    conv_stack = r_lru_small[:, lru_rows:].reshape(8, 2, CONV_WIDTH, D_MODEL)
    conv_stack = lax.dynamic_slice_in_dim(conv_stack, me * n_conv, n_conv, axis=3).reshape(8, 8, n_conv)
    res["conv_w"] = _adamw([conv_stack], conv_w.reshape(1, 8, n_conv), m_conv_w.reshape(1, 8, n_conv),
                           v_conv_w.reshape(1, 8, n_conv), 8, "adamw_conv_w")

    names = ("w_mod", "b_mod", "g_pre", "w_in", "conv_w", "conv_b", "w_rg", "b_rg", "w_ig", "b_ig",
             "lru_lambda", "w_pa", "w_pb", "w_o", "g_post")
    outs = [loss, grad_x]
    for k in range(4):
        outs.extend(res[n][k].reshape(W[n].shape) for n in names)
    return tuple(outs)
```

```python
import functools

import jax
import jax.numpy as jnp
import numpy as np
from jax import lax
from jax.experimental import pallas as pl
from jax.experimental.pallas import tpu as pltpu

F32 = jnp.float32
BF16 = jnp.bfloat16

N_DEV = 8
D_MODEL = 1024
HEAD_DIM = 128
HEADS = 4
ATT_WIDTH = HEADS * HEAD_DIM
DILATIONS = (1, 4, 16)
BAND = 128
N_CHUNKS = 18
QKV_CHUNKS = 9
CONV_WIDTH = 4
LRU_C = 8.0
NORM_EPS = 1e-6
NEG_INF = -1e30
ADAM_LR = 0.001
ADAM_B1 = 0.9
ADAM_B2 = 0.999
ADAM_EPS = 1e-08
ADAM_WD = 0.01
ADAM_STEP = 10
CHUNK_PERM = (0, 3, 6, 1, 4, 7, 2, 5, 8, 10, 11, 12, 13, 14, 15, 16, 17, 9)
DREST_CHUNKS = 10
VMEM_LIMIT = 56 * 1024 * 1024
MESH_AXES = ("x", "y", "c")


def _params(sem=None):
    return pltpu.CompilerParams(dimension_semantics=sem, vmem_limit_bytes=VMEM_LIMIT)


def _silu(x):
    return x * jax.nn.sigmoid(x)


def _dsilu(x):
    s = jax.nn.sigmoid(x)
    return s * (1.0 + x * (1.0 - s))


def _neg_expm1(x):
    series = -x * (1.0 + x * (0.5 + x * (1.0 / 6.0 + x * (1.0 / 24.0 + x * (1.0 / 120.0)))))
    return jnp.where(x > -0.05, series, 1.0 - jnp.exp(x))


def _softplus_neg(lam):
    z = jnp.exp(-jnp.abs(lam))
    small = z * (1.0 - z * (0.5 - z * (1.0 / 3.0 - z * 0.25)))
    log1p_z = jnp.where(z < 1e-2, small, jnp.log(1.0 + z))
    return jnp.maximum(-lam, 0.0) + log1p_z


def _dot(a, b):
    return jnp.dot(a, b, preferred_element_type=F32)


def _dot_nt(a, b):
    return lax.dot_general(a, b, (((1,), (1,)), ((), ())), preferred_element_type=F32)


def _dot_tn(a, b):
    return lax.dot_general(a, b, (((0,), (0,)), ((), ())), preferred_element_type=F32)


def _perm_matrix(r, rows, transpose=False):
    n = rows // r
    p = np.zeros((rows, rows), np.float32)
    dst = np.arange(rows)
    p[dst, (dst % n) * r + dst // n] = 1.0
    return jnp.asarray(p.T if transpose else p, dtype=BF16)


def _permute_f32(p, x, pieces):
    part = x.astype(BF16)
    acc = _dot(p, part)
    for _ in range(pieces - 1):
        x = x - part.astype(F32)
        part = x.astype(BF16)
        acc = acc + _dot(p, part)
    return acc


def _rows_of_residues(ref, lead=()):
    r = ref.shape[len(lead)]
    if r == 1:
        return ref[lead + (0,)]
    return jnp.concatenate([ref[lead + (rho,)] for rho in range(r)], axis=0)


def _store_residues(ref, x, lead=()):
    r = ref.shape[len(lead)]
    n = x.shape[0] // r
    for rho in range(r):
        ref[lead + (rho,)] = x[rho * n:(rho + 1) * n]


ANY_SPEC = pl.BlockSpec(memory_space=pl.ANY)


def _exchange_shapes(arrs, modes):
    return [jax.ShapeDtypeStruct(((N_DEV,) + a.shape) if mode == "ag" else a.shape, a.dtype)
            for a, mode in zip(arrs, modes)]


def _exchange_sems(n):
    return [pltpu.SemaphoreType.DMA((7 * n,)), pltpu.SemaphoreType.DMA((7 * n,)),
            pltpu.SemaphoreType.DMA((n,))]


def _exchange_copies(ins, outs, modes, sems):
    n = len(ins)
    send_sems, recv_sems, local_sems = sems
    x, y, c = lax.axis_index("x"), lax.axis_index("y"), lax.axis_index("c")
    me = 4 * x + 2 * y + c

    def src_for(a, dev):
        return ins[a] if modes[a] == "ag" else ins[a].at[dev]

    local = [pltpu.make_async_copy(src_for(a, me), outs[a].at[me], local_sems.at[a]) for a in range(n)]
    sends, arrivals = [], []
    for k in range(1, N_DEV):
        px = 1 - x if (k >> 2) & 1 else x
        py = 1 - y if (k >> 1) & 1 else y
        pc = 1 - c if k & 1 else c
        peer = 4 * px + 2 * py + pc
        for a in range(n):
            s = (k - 1) * n + a
            for dst, group in ((me, sends), (peer, arrivals)):
                group.append(pltpu.make_async_remote_copy(
                    src_ref=src_for(a, peer), dst_ref=outs[a].at[dst],
                    send_sem=send_sems.at[s], recv_sem=recv_sems.at[s],
                    device_id=(px, py, pc), device_id_type=pl.DeviceIdType.MESH))
    return local, sends, arrivals


def _exchange_start(copies):
    local, sends, _ = copies
    for cp in local + sends:
        cp.start()


def _exchange_wait(copies):
    local, sends, arrivals = copies
    for cp in arrivals:
        cp.wait_recv()
    for cp in sends:
        cp.wait_send()
    for cp in local:
        cp.wait()


def _exchange(arrs, modes, name):
    n = len(arrs)

    def body(*refs):
        copies = _exchange_copies(refs[:n], refs[n:2 * n], modes, refs[2 * n:])
        _exchange_start(copies)
        _exchange_wait(copies)

    outs = pl.pallas_call(
        body, name=name, out_shape=tuple(_exchange_shapes(arrs, modes)),
        in_specs=[ANY_SPEC] * n, out_specs=tuple([ANY_SPEC] * n),
        scratch_shapes=_exchange_sems(n),
    )(*arrs)
    return list(outs)


def _call(body, args, *, name, grid, in_specs, out_specs, out_shape, scratch_shapes=(), aliases=None,
          hosted=None):
    sem = ("arbitrary",) * len(grid)
    if hosted is None:
        outs = pl.pallas_call(
            body, name=name, grid=grid, in_specs=in_specs, out_specs=tuple(out_specs),
            out_shape=tuple(out_shape), scratch_shapes=list(scratch_shapes),
            input_output_aliases=aliases or {}, compiler_params=_params(sem))(*args)
        return list(outs), []
    x_arrs, modes = hosted
    n_in, n_out, n_scr, nx = len(args), len(out_shape), len(scratch_shapes), len(x_arrs)

    def wrapped(*refs):
        ins, x_ins = refs[:n_in], refs[n_in:n_in + nx]
        outs = refs[n_in + nx:n_in + nx + n_out]
        x_outs = refs[n_in + nx + n_out:n_in + 2 * nx + n_out]
        scr = refs[n_in + 2 * nx + n_out:n_in + 2 * nx + n_out + n_scr]
        sems = refs[n_in + 2 * nx + n_out + n_scr:]
        first = pl.program_id(0) == 0
        last = pl.program_id(0) == grid[0] - 1
        for axis in range(1, len(grid)):
            first = jnp.logical_and(first, pl.program_id(axis) == 0)
            last = jnp.logical_and(last, pl.program_id(axis) == grid[axis] - 1)

        @pl.when(first)
        def _():
            _exchange_start(_exchange_copies(x_ins, x_outs, modes, sems))

        body(*ins, *outs, *scr)

        @pl.when(last)
        def _():
            _exchange_wait(_exchange_copies(x_ins, x_outs, modes, sems))

    outs = pl.pallas_call(
        wrapped, name=name, grid=grid, in_specs=list(in_specs) + [ANY_SPEC] * nx,
        out_specs=tuple(out_specs) + tuple([ANY_SPEC] * nx),
        out_shape=tuple(out_shape) + tuple(_exchange_shapes(x_arrs, modes)),
        scratch_shapes=list(scratch_shapes) + _exchange_sems(nx),
        input_output_aliases=aliases or {}, compiler_params=_params(sem))(*args, *x_arrs)
    return list(outs[:n_out]), list(outs[n_out:])


def _mod_fwd(c_pad, w_mod):
    def body(c_ref, w_ref, o_ref):
        sc = _silu(c_ref[...]).astype(BF16)
        for layer in range(2):
            o_ref[layer] = _dot(sc, w_ref[layer].astype(BF16))

    return pl.pallas_call(
        body, name="mod_fwd", out_shape=jax.ShapeDtypeStruct((2, 16, w_mod.shape[2]), F32),
        compiler_params=_params())(c_pad, w_mod)


def _mod_wgrad(c_t, dmod):
    n_cols = dmod.shape[2]

    def body(c_ref, d_ref, o_ref):
        sc = _silu(c_ref[...]).astype(BF16).astype(F32)
        for layer in range(2):
            dm = d_ref[layer].astype(BF16).astype(F32)
            acc = sc[:, 0:1] * dm[0:1, :]
            for b in range(1, N_DEV):
                acc = acc + sc[:, b:b + 1] * dm[b:b + 1, :]
            o_ref[layer] = acc

    return pl.pallas_call(
        body, name="mod_wgrad", out_shape=jax.ShapeDtypeStruct((2, D_MODEL, n_cols), F32),
        compiler_params=_params())(c_t, dmod)


GROUP_COLS = 3 * ATT_WIDTH
PROJ_TM = 256


def _norm_proj_fwd(x, vec, w_int, hosted=None):
    S = x.shape[0]
    tm = PROJ_TM
    perms = [_perm_matrix(r, tm) for r in DILATIONS[1:]]

    def body(x_ref, v_ref, w_ref, p1_ref, p2_ref, ht0_ref, ht1_ref, ht2_ref, q0_ref, q1_ref, q2_ref,
             rest_ref):
        xv = x_ref[...]
        rstd = lax.rsqrt(jnp.mean(xv * xv, axis=-1, keepdims=True) + NORM_EPS)
        hf = ((xv * rstd) * v_ref[0:1, :]) * (1.0 + v_ref[1:2, :]) + v_ref[2:3, :]
        h = hf.astype(BF16)
        for g, (q_ref, ht_ref, p_ref) in enumerate(((q0_ref, ht0_ref, None), (q1_ref, ht1_ref, p1_ref),
                                                    (q2_ref, ht2_ref, p2_ref))):
            rows_f = hf if p_ref is None else _dot(p_ref[...], h)
            ht_ref[...] = rows_f.T.astype(BF16)
            _store_residues(q_ref, _dot(rows_f.astype(BF16),
                                        w_ref[:, g * GROUP_COLS:(g + 1) * GROUP_COLS]).astype(BF16))
        rest_ref[...] = _dot(h, w_ref[:, 3 * GROUP_COLS:])

    n_rest = QKV_CHUNKS * ATT_WIDTH
    whole = pl.BlockSpec(memory_space=pltpu.VMEM)
    ht_spec = pl.BlockSpec((D_MODEL, tm), lambda i: (0, i))
    ht_shape = jax.ShapeDtypeStruct((D_MODEL, S), BF16)
    return _call(
        body, (x, vec, w_int, *perms), name="norm_proj_fwd", grid=(S // tm,),
        in_specs=[pl.BlockSpec((tm, D_MODEL), lambda i: (i, 0)),
                  pl.BlockSpec((8, D_MODEL), lambda i: (0, 0)), whole, whole, whole],
        out_specs=(ht_spec, ht_spec, ht_spec,
                   *[pl.BlockSpec((r, tm // r, GROUP_COLS), lambda i: (0, i, 0)) for r in DILATIONS],
                   pl.BlockSpec((tm, n_rest), lambda i: (i, 0))),
        out_shape=(ht_shape, ht_shape, ht_shape,
                   *[jax.ShapeDtypeStruct((r, S // r, GROUP_COLS), BF16) for r in DILATIONS],
                   jax.ShapeDtypeStruct((S, n_rest), F32)),
        hosted=hosted)


def _proj_dgrad_norm_bwd(dqkv, drest, w_int, x, dxn, vec, hosted=None):
    S = x.shape[0]
    tm = PROJ_TM
    perms = [_perm_matrix(r, tm, transpose=True) for r in DILATIONS[1:]]

    def body(d0_ref, d1_ref, d2_ref, dr_ref, w_ref, p1_ref, p2_ref, x_ref, dxn_ref, v_ref, dx_ref, acc_ref):
        i = pl.program_id(0)

        @pl.when(i == 0)
        def _():
            acc_ref[...] = jnp.zeros_like(acc_ref)

        def wcols(k):
            return w_ref[:, k * ATT_WIDTH:(k + 1) * ATT_WIDTH]

        dhv = None
        for g, (d_ref, p_ref) in enumerate(((d0_ref, None), (d1_ref, p1_ref), (d2_ref, p2_ref))):
            for t in range(3):
                d = _rows_of_residues(d_ref, (t,))
                if p_ref is not None:
                    d = _dot(p_ref[...], d).astype(BF16)
                part = _dot_nt(d, wcols(3 * g + t))
                dhv = part if dhv is None else dhv + part
        for t in range(7):
            dhv = dhv + _dot_nt(dr_ref[t], wcols(11 + t))
        for t in range(2):
            dhv = dhv + _dot_nt(dr_ref[8 + t], wcols(9 + t))

        xv = x_ref[...]
        g = v_ref[0:1, :]
        sc1 = 1.0 + v_ref[1:2, :]
        rstd = lax.rsqrt(jnp.mean(xv * xv, axis=-1, keepdims=True) + NORM_EPS)
        xhat = xv * rstd
        acc_ref[0:1, :] += jnp.sum(dhv, axis=0, keepdims=True)
        acc_ref[1:2, :] += jnp.sum(dhv * (xhat * g), axis=0, keepdims=True)
        acc_ref[2:3, :] += jnp.sum(dhv * xhat * sc1, axis=0, keepdims=True)
        dxhat = dhv * (g * sc1)
        dx = rstd * (dxhat - xhat * jnp.mean(dxhat * xhat, axis=-1, keepdims=True))
        dx_ref[...] = dx + dxn_ref[...]

    row = pl.BlockSpec((tm, D_MODEL), lambda i: (i, 0))
    vec_spec = pl.BlockSpec((8, D_MODEL), lambda i: (0, 0))
    whole = pl.BlockSpec(memory_space=pltpu.VMEM)
    return _call(
        body, (*dqkv, drest, w_int, *perms, x, dxn, vec), name="proj_dgrad_norm_bwd", grid=(S // tm,),
        in_specs=[*[pl.BlockSpec((3, r, tm // r, ATT_WIDTH), lambda i: (0, 0, i, 0)) for r in DILATIONS],
                  pl.BlockSpec((DREST_CHUNKS, tm, ATT_WIDTH), lambda i: (0, i, 0)),
                  whole, whole, whole, row, row, vec_spec],
        out_specs=(row, vec_spec),
        out_shape=(jax.ShapeDtypeStruct((S, D_MODEL), F32), jax.ShapeDtypeStruct((8, D_MODEL), F32)),
        hosted=hosted)


def _proj_wgrad_part(h_t, d, n_chunks, chunk_block, name):
    S = h_t.shape[1]
    ts = 512

    def body(h_ref, d_ref, o_ref):
        i = pl.program_id(0)

        @pl.when(i == 0)
        def _():
            o_ref[...] = jnp.zeros_like(o_ref)

        ht = h_ref[...]
        for t in range(n_chunks):
            o_ref[:, t * ATT_WIDTH:(t + 1) * ATT_WIDTH] += _dot(ht, d_ref[t])

    return pl.pallas_call(
        body, name=name, grid=(S // ts,),
        in_specs=[pl.BlockSpec((D_MODEL, ts), lambda i: (0, i)),
                  pl.BlockSpec((n_chunks, ts, ATT_WIDTH), lambda i: (chunk_block, i, 0))],
        out_specs=pl.BlockSpec((D_MODEL, n_chunks * ATT_WIDTH), lambda i: (0, 0)),
        out_shape=jax.ShapeDtypeStruct((D_MODEL, n_chunks * ATT_WIDTH), F32),
        compiler_params=_params(("arbitrary",)))(h_t, d)


def _proj_wgrad_group(h_t, d, r, name):
    S = h_t.shape[1]
    ts = 512
    n = PROJ_TM // r

    def body(h_ref, d_ref, o_ref):
        i = pl.program_id(0)

        @pl.when(i == 0)
        def _():
            o_ref[...] = jnp.zeros_like(o_ref)

        ht = h_ref[...]
        for t in range(3):
            rows = jnp.concatenate([d_ref[t, rho, s * n:(s + 1) * n] for s in range(ts // PROJ_TM)
                                    for rho in range(r)], axis=0) if r > 1 else d_ref[t, 0]
            o_ref[:, t * ATT_WIDTH:(t + 1) * ATT_WIDTH] += _dot(ht, rows)

    return pl.pallas_call(
        body, name=name, grid=(S // ts,),
        in_specs=[pl.BlockSpec((D_MODEL, ts), lambda i: (0, i)),
                  pl.BlockSpec((3, r, ts // r, ATT_WIDTH), lambda i: (0, 0, i, 0))],
        out_specs=pl.BlockSpec((D_MODEL, GROUP_COLS), lambda i: (0, 0)),
        out_shape=jax.ShapeDtypeStruct((D_MODEL, GROUP_COLS), F32),
        compiler_params=_params(("arbitrary",)))(h_t, d)


def _wgrad_tn(a, b, name):
    S, M = a.shape
    N = b.shape[1]
    ts = 1024

    def body(a_ref, b_ref, o_ref):
        i = pl.program_id(0)
        part = _dot_tn(a_ref[...], b_ref[...])

        @pl.when(i == 0)
        def _():
            o_ref[...] = part

        @pl.when(i > 0)
        def _():
            o_ref[...] += part

    return pl.pallas_call(
        body, name=name, grid=(S // ts,),
        in_specs=[pl.BlockSpec((ts, M), lambda i: (i, 0)), pl.BlockSpec((ts, N), lambda i: (i, 0))],
        out_specs=pl.BlockSpec((M, N), lambda i: (0, 0)),
        out_shape=jax.ShapeDtypeStruct((M, N), F32),
        compiler_params=_params(("arbitrary",)))(a, b)


def _attn_fwd(qkv, g, r):
    L = qkv.shape[1]
    nb2 = L // (2 * BAND)
    scale = HEAD_DIM ** -0.5

    def body(q_ref, kp_ref, kc_ref, vp_ref, vc_ref, o_ref, st_ref):
        m_step = pl.program_id(1)
        ii = lax.broadcasted_iota(jnp.int32, (2 * BAND, 3 * BAND), 0)
        kk = lax.broadcasted_iota(jnp.int32, (2 * BAND, 3 * BAND), 1)
        mask = jnp.logical_and(jnp.logical_and(kk >= ii, kk <= ii + BAND),
                               jnp.logical_or(kk >= BAND, m_step > 0))
        lane = lax.broadcasted_iota(jnp.int32, (2 * BAND, 128), 1)
        stats = jnp.zeros((2 * BAND, 128), F32)
        for h in range(HEADS):
            sl = slice(h * HEAD_DIM, (h + 1) * HEAD_DIM)
            k_cat = jnp.concatenate([kp_ref[:, sl], kc_ref[:, sl]], axis=0)
            v_cat = jnp.concatenate([vp_ref[:, sl], vc_ref[:, sl]], axis=0)
            s = jnp.where(mask, _dot_nt(q_ref[:, sl], k_cat) * scale, NEG_INF)
            m = jnp.max(s, axis=1, keepdims=True)
            p = jnp.exp(s - m)
            l = jnp.sum(p, axis=1, keepdims=True)
            o_ref[:, sl] = _dot(p.astype(BF16), v_cat) / l
            stats = jnp.where(lane == h, m, stats)
            stats = jnp.where(lane == HEADS + h, l, stats)
        st_ref[...] = stats

    two = (None, 2 * BAND, ATT_WIDTH)
    one = (None, BAND, ATT_WIDTH)

    def prev(m):
        return jnp.maximum(2 * m - 1, 0)

    return pl.pallas_call(
        body, name=f"attn_fwd_g{g}", grid=(r, nb2),
        in_specs=[
            pl.BlockSpec(two, lambda rho, m: (rho, m, 0)),
            pl.BlockSpec(one, lambda rho, m: (rho, prev(m), 1)),
            pl.BlockSpec(two, lambda rho, m: (rho, m, 1)),
            pl.BlockSpec(one, lambda rho, m: (rho, prev(m), 2)),
            pl.BlockSpec(two, lambda rho, m: (rho, m, 2)),
        ],
        out_specs=(pl.BlockSpec(two, lambda rho, m: (rho, m, 0)),
                   pl.BlockSpec((None, 2 * BAND, 128), lambda rho, m: (rho, m, 0))),
        out_shape=(jax.ShapeDtypeStruct((r, L, ATT_WIDTH), F32),
                   jax.ShapeDtypeStruct((r, L, 128), F32)),
        compiler_params=_params(("parallel", "parallel")))(qkv, qkv, qkv, qkv, qkv)


def _attn_combine(outs, stats, rest):
    S = rest.shape[0]
    tm = 512
    gatt_blk = 8
    back = [_perm_matrix(r, tm, transpose=True) for r in DILATIONS[1:]]
    forth = [_perm_matrix(r, tm) for r in DILATIONS[1:]]

    def body(o0_ref, o1_ref, o2_ref, s0_ref, s1_ref, s2_ref, g_ref, b1_ref, b2_ref, f1_ref, f2_ref,
             o_ref, a_ref, l0_ref, l1_ref, l2_ref):
        outs_nat = [o0_ref[0]] + [_permute_f32(b_ref[...], _rows_of_residues(o_g), 2)
                                  for o_g, b_ref in ((o1_ref, b1_ref), (o2_ref, b2_ref))]
        st = [s0_ref[0]] + [_permute_f32(b_ref[...], _rows_of_residues(s_g), 3)
                            for s_g, b_ref in ((s1_ref, b1_ref), (s2_ref, b2_ref))]
        lane = lax.broadcasted_iota(jnp.int32, (tm, 128), 1)
        lse_out = jnp.zeros((tm, 128), F32)
        for h in range(HEADS):
            sl = slice(h * HEAD_DIM, (h + 1) * HEAD_DIM)
            ms = [s[:, h:h + 1] for s in st]
            ls = [s[:, HEADS + h:HEADS + h + 1] for s in st]
            m_all = jnp.maximum(jnp.maximum(ms[0], ms[1]), ms[2])
            ws = [l * jnp.exp(m - m_all) for m, l in zip(ms, ls)]
            den = ws[0] + ws[1] + ws[2]
            o = (ws[0] * outs_nat[0][:, sl] + ws[1] * outs_nat[1][:, sl] + ws[2] * outs_nat[2][:, sl]) / den
            o_ref[:, sl] = o
            a_ref[:, sl] = (o * _silu(g_ref[:, sl])).astype(BF16)
            lse_out = jnp.where(lane == h, m_all + jnp.log(den), lse_out)
        l0_ref[0] = lse_out
        for l_ref, f_ref in ((l1_ref, f1_ref), (l2_ref, f2_ref)):
            _store_residues(l_ref, _permute_f32(f_ref[...], lse_out, 3))

    o_spec = pl.BlockSpec((tm, ATT_WIDTH), lambda i: (i, 0))
    whole = pl.BlockSpec(memory_space=pltpu.VMEM)

    def res_spec(r, width):
        return pl.BlockSpec((r, tm // r, width), lambda i: (0, i, 0))

    return pl.pallas_call(
        body, name="attn_combine", grid=(S // tm,),
        in_specs=[*[res_spec(r, ATT_WIDTH) for r in DILATIONS], *[res_spec(r, 128) for r in DILATIONS],
                  pl.BlockSpec((tm, ATT_WIDTH), lambda i: (i, gatt_blk)), whole, whole, whole, whole],
        out_specs=(o_spec, o_spec, *[res_spec(r, 128) for r in DILATIONS]),
        out_shape=(jax.ShapeDtypeStruct((S, ATT_WIDTH), F32), jax.ShapeDtypeStruct((S, ATT_WIDTH), BF16),
                   *[jax.ShapeDtypeStruct((r, S // r, 128), F32) for r in DILATIONS]),
        compiler_params=_params(("parallel",)))(*outs, *stats, rest, *back, *forth)


def _attn_bwd(qkv, do, lse, dvec, g, r, hosted=None):
    L = qkv.shape[1]
    nb = L // BAND
    nb2 = nb // 2
    scale = HEAD_DIM ** -0.5

    def body(qc_ref, qn_ref, k_ref, v_ref, doc_ref, don_ref, lc_ref, ln_ref, dc_ref, dn_ref,
             out_ref, carry_ref):
        j = pl.program_id(1)

        @pl.when(j == 0)
        def _():
            carry_ref[...] = jnp.zeros_like(carry_ref)

        ii = lax.broadcasted_iota(jnp.int32, (3 * BAND, 2 * BAND), 0)
        kk = lax.broadcasted_iota(jnp.int32, (3 * BAND, 2 * BAND), 1)
        mask = jnp.logical_and(jnp.logical_and(kk <= ii, kk >= ii - BAND),
                               jnp.logical_or(ii < 2 * BAND, j < nb2 - 1))
        lse3 = jnp.concatenate([lc_ref[...], ln_ref[...]], axis=0)
        dvec3 = jnp.concatenate([dc_ref[...], dn_ref[...]], axis=0)
        for h in range(HEADS):
            sl = slice(h * HEAD_DIM, (h + 1) * HEAD_DIM)
            k = k_ref[:, sl]
            v = v_ref[:, sl]
            q = jnp.concatenate([qc_ref[:, sl], qn_ref[:, sl]], axis=0)
            do = jnp.concatenate([doc_ref[:, sl], don_ref[:, sl]], axis=0)
            p = jnp.where(mask, jnp.exp(_dot_nt(q, k) * scale - lse3[:, h:h + 1]), 0.0)
            ds = (p * (_dot_nt(do, v) - dvec3[:, h:h + 1])).astype(BF16)
            dq3 = _dot(ds, k) * scale
            out_ref[0, 0:BAND, sl] = (carry_ref[:, sl] + dq3[:BAND]).astype(BF16)
            out_ref[0, BAND:, sl] = dq3[BAND:2 * BAND].astype(BF16)
            out_ref[1, :, sl] = (_dot_tn(ds, q) * scale).astype(BF16)
            out_ref[2, :, sl] = _dot_tn(p.astype(BF16), do).astype(BF16)
            carry_ref[:, sl] = dq3[2 * BAND:]

    two = (None, 2 * BAND, ATT_WIDTH)
    one = (None, BAND, ATT_WIDTH)
    stwo = (None, 2 * BAND, 128)
    sone = (None, BAND, 128)

    def nxt(j):
        return jnp.minimum(2 * j + 2, nb - 1)

    (out,), got = _call(
        body, (qkv, qkv, qkv, qkv, do, do, lse, lse, dvec, dvec), name=f"attn_bwd_g{g}", grid=(r, nb2),
        in_specs=[
            pl.BlockSpec(two, lambda rho, j: (rho, j, 0)),
            pl.BlockSpec(one, lambda rho, j: (rho, nxt(j), 0)),
            pl.BlockSpec(two, lambda rho, j: (rho, j, 1)),
            pl.BlockSpec(two, lambda rho, j: (rho, j, 2)),
            pl.BlockSpec(two, lambda rho, j: (rho, j, 0)),
            pl.BlockSpec(one, lambda rho, j: (rho, nxt(j), 0)),
            pl.BlockSpec(stwo, lambda rho, j: (rho, j, 0)),
            pl.BlockSpec(sone, lambda rho, j: (rho, nxt(j), 0)),
            pl.BlockSpec(stwo, lambda rho, j: (rho, j, 0)),
            pl.BlockSpec(sone, lambda rho, j: (rho, nxt(j), 0)),
        ],
        out_specs=(pl.BlockSpec((3, None, 2 * BAND, ATT_WIDTH), lambda rho, j: (0, rho, j, 0)),),
        out_shape=(jax.ShapeDtypeStruct((3, r, L, ATT_WIDTH), BF16),),
        scratch_shapes=[pltpu.VMEM((BAND, ATT_WIDTH), F32)],
        hosted=hosted)
    return out, got


LRU_T = 256


def _linear_scan(a, b, carry, reverse):
    T = a.shape[0]
    row8 = lax.broadcasted_iota(jnp.int32, a.shape, 0) & 7
    for s in (1, 2, 4):
        keep = (row8 < 8 - s) if reverse else (row8 >= s)
        shift = T - s if reverse else s
        b_sh = jnp.where(keep, pltpu.roll(b, shift, 0), 0.0)
        a_sh = jnp.where(keep, pltpu.roll(a, shift, 0), 1.0)
        b = a * b_sh + b
        a = a * a_sh
    tiles = [None] * (T // 8)
    order = range(T // 8 - 1, -1, -1) if reverse else range(T // 8)
    for k in order:
        y = b[8 * k:8 * k + 8] + a[8 * k:8 * k + 8] * carry
        tiles[k] = y
        carry = y[0:1] if reverse else y[7:8]
    return jnp.concatenate(tiles, axis=0), carry


def _gate_matmuls(ucb, w_ref, bias):
    parts = [_dot(ucb[:, j * 128:(j + 1) * 128], w_ref[j]) for j in range(8)]
    return jnp.concatenate(parts, axis=1) + bias


def _conv_fwd(u, u_prev8, cw_ref):
    T = u.shape[0]
    ue = jnp.concatenate([u_prev8, u], axis=0)
    uc = cw_ref[4:5, :] + cw_ref[0:1, :] * u
    for j in range(1, CONV_WIDTH):
        uc = uc + cw_ref[j:j + 1, :] * pltpu.roll(ue, j, 0)[8:8 + T]
    return uc


def _lru_fwd(rest, cw, wr2, wi2, hosted=None):
    S = rest.shape[0]
    T = LRU_T

    def body(u_ref, cw_ref, wr_ref, wi_ref, h_ref, a_ref, ucar_ref, hcar_ref):
        c = pl.program_id(0)

        @pl.when(c == 0)
        def _():
            ucar_ref[...] = jnp.zeros_like(ucar_ref)
            hcar_ref[...] = jnp.zeros_like(hcar_ref)

        u = u_ref[...]
        uc = _conv_fwd(u, ucar_ref[...], cw_ref)
        ucar_ref[...] = u[T - 8:, :]
        ucb = uc.astype(BF16)
        r = jax.nn.sigmoid(_gate_matmuls(ucb, wr_ref, cw_ref[5:6, :]))
        ig = jax.nn.sigmoid(_gate_matmuls(ucb, wi_ref, cw_ref[6:7, :]))
        log_a = -LRU_C * r * _softplus_neg(cw_ref[7:8, :])
        a = jnp.exp(log_a)
        b = jnp.sqrt(_neg_expm1(2.0 * log_a)) * (ig * uc)
        a_ref[...] = a
        h, last = _linear_scan(a, b, hcar_ref[0:1, :], reverse=False)
        h_ref[...] = h
        hcar_ref[...] = jnp.broadcast_to(last, (8, D_MODEL))

    row_spec = pl.BlockSpec((T, D_MODEL), lambda c: (c, 0))
    return _call(
        body, (rest, cw, wr2, wi2), name="lru_fwd", grid=(S // T,),
        in_specs=[row_spec, pl.BlockSpec((8, D_MODEL), lambda c: (0, 0)),
                  pl.BlockSpec((8, 128, 128), lambda c: (0, 0, 0)),
                  pl.BlockSpec((8, 128, 128), lambda c: (0, 0, 0))],
        out_specs=(row_spec, row_spec),
        out_shape=(jax.ShapeDtypeStruct((S, D_MODEL), F32), jax.ShapeDtypeStruct((S, D_MODEL), F32)),
        scratch_shapes=[pltpu.VMEM((8, D_MODEL), F32), pltpu.VMEM((8, D_MODEL), F32)],
        hosted=hosted)


def _lru_bwd(dhl, hl, a_all, rest, cw, wr2, wi2, dproj, hosted=None):
    S = rest.shape[0]
    T = LRU_T
    nc = S // T

    def body(dh_ref, h_ref, hp_ref, a_ref, u_ref, up_ref, cw_ref, wr_ref, wi_ref, _alias,
             du_ref, acc_ref, gwr_ref, gwi_ref, gcar_ref, acar_ref, dcar_ref):
        step = pl.program_id(0)
        c = nc - 1 - step

        @pl.when(step == 0)
        def _():
            acc_ref[...] = jnp.zeros_like(acc_ref)
            gwr_ref[...] = jnp.zeros_like(gwr_ref)
            gwi_ref[...] = jnp.zeros_like(gwi_ref)
            gcar_ref[...] = jnp.zeros_like(gcar_ref)
            acar_ref[...] = jnp.zeros_like(acar_ref)
            dcar_ref[...] = jnp.zeros_like(dcar_ref)

        row = lax.broadcasted_iota(jnp.int32, (T, D_MODEL), 0)
        u = u_ref[...]
        u_prev = jnp.where(c > 0, up_ref[...], 0.0)
        h_prev8 = jnp.where(c > 0, hp_ref[...], 0.0)
        a = a_ref[...]
        h = h_ref[...]
        uc = _conv_fwd(u, u_prev, cw_ref)
        ucb = uc.astype(BF16)
        r = jax.nn.sigmoid(_gate_matmuls(ucb, wr_ref, cw_ref[5:6, :]))
        ig = jax.nn.sigmoid(_gate_matmuls(ucb, wi_ref, cw_ref[6:7, :]))
        sp = _softplus_neg(cw_ref[7:8, :])
        log_a = -LRU_C * r * sp
        mult = jnp.sqrt(_neg_expm1(2.0 * log_a))
        a_next = jnp.where(row < T - 1, pltpu.roll(a, T - 1, 0), acar_ref[0:1, :])
        G, first = _linear_scan(a_next, dh_ref[...], gcar_ref[0:1, :], reverse=True)
        gcar_ref[...] = jnp.broadcast_to(first, (8, D_MODEL))
        acar_ref[...] = jnp.broadcast_to(a[0:1, :], (8, D_MODEL))
        he = jnp.concatenate([h_prev8, h], axis=0)
        h_before = pltpu.roll(he, 1, 0)[8:8 + T]
        d_a = G * h_before
        d_mult = G * (ig * uc)
        d_ig = G * (mult * uc)
        duc = G * (mult * ig)
        d_log_a = d_a * a - d_mult * (a * a) / mult
        d_r = d_log_a * (-LRU_C * sp)
        d_sp = jnp.sum(d_log_a * (-LRU_C * r), axis=0, keepdims=True)
        dpre_r = d_r * r * (1.0 - r)
        dpre_i = d_ig * ig * (1.0 - ig)
        dprb = dpre_r.astype(BF16)
        dpib = dpre_i.astype(BF16)
        back = []
        for j in range(8):
            sl = slice(j * 128, (j + 1) * 128)
            back.append(_dot_nt(dprb[:, sl], wr_ref[j]) + _dot_nt(dpib[:, sl], wi_ref[j]))
            gwr_ref[j] += _dot_tn(ucb[:, sl], dprb[:, sl])
            gwi_ref[j] += _dot_tn(ucb[:, sl], dpib[:, sl])
        duc = duc + jnp.concatenate(back, axis=1)
        de = jnp.concatenate([duc, dcar_ref[...]], axis=0)
        du = cw_ref[0:1, :] * duc
        ue = jnp.concatenate([u_prev, u], axis=0)
        acc_ref[0:1, :] += jnp.sum(duc * u, axis=0, keepdims=True)
        for j in range(1, CONV_WIDTH):
            du = du + cw_ref[j:j + 1, :] * pltpu.roll(de, T + 8 - j, 0)[0:T]
            acc_ref[j:j + 1, :] += jnp.sum(duc * pltpu.roll(ue, j, 0)[8:8 + T], axis=0, keepdims=True)
        dcar_ref[...] = duc[0:8, :]
        acc_ref[4:5, :] += jnp.sum(duc, axis=0, keepdims=True)
        acc_ref[5:6, :] += jnp.sum(dpre_r, axis=0, keepdims=True)
        acc_ref[6:7, :] += jnp.sum(dpre_i, axis=0, keepdims=True)
        acc_ref[7:8, :] += d_sp
        du_ref[0] = du[:, 0:ATT_WIDTH].astype(BF16)
        du_ref[1] = du[:, ATT_WIDTH:].astype(BF16)

    def rev(step):
        return nc - 1 - step

    def prev8(step):
        return jnp.maximum(rev(step) * (T // 8) - 1, 0)

    row_spec = pl.BlockSpec((T, D_MODEL), lambda s: (rev(s), 0))
    prev_spec = pl.BlockSpec((8, D_MODEL), lambda s: (prev8(s), 0))
    vec_spec = pl.BlockSpec((8, D_MODEL), lambda s: (0, 0))
    w_spec = pl.BlockSpec((8, 128, 128), lambda s: (0, 0, 0))
    return _call(
        body, (dhl, hl, hl, a_all, rest, rest, cw, wr2, wi2, dproj), name="lru_bwd", grid=(nc,),
        in_specs=[row_spec, row_spec, prev_spec, row_spec, row_spec, prev_spec, vec_spec, w_spec, w_spec,
                  ANY_SPEC],
        out_specs=(pl.BlockSpec((2, T, ATT_WIDTH), lambda s: (4, rev(s), 0)), vec_spec, w_spec, w_spec),
        out_shape=(jax.ShapeDtypeStruct(dproj.shape, BF16), jax.ShapeDtypeStruct((8, D_MODEL), F32),
                   jax.ShapeDtypeStruct((8, 128, 128), F32), jax.ShapeDtypeStruct((8, 128, 128), F32)),
        scratch_shapes=[pltpu.VMEM((8, D_MODEL), F32), pltpu.VMEM((8, D_MODEL), F32),
                        pltpu.VMEM((8, D_MODEL), F32)],
        aliases={9: 0}, hosted=hosted)


OUT_TM = 256


def _out_fwd(a_gated, hl, rest, x, vec, w_pa, w_pb, w_o):
    S = x.shape[0]
    tm = OUT_TM

    def body(ag_ref, hl_ref, gl_ref, ma_ref, mb_ref, x_ref, v_ref, wpa_ref, wpb_ref, wo_ref,
             xn_ref, mix_ref, ya_ref, yb_ref, bg_ref):
        bg = (hl_ref[...] * _silu(gl_ref[...])).astype(BF16)
        ya = _dot(ag_ref[...], wpa_ref[...])
        yb = _dot(bg, wpb_ref[...])
        mix = (jax.nn.sigmoid(ma_ref[...]) * ya + jax.nn.sigmoid(mb_ref[...]) * yb).astype(BF16)
        out = _dot(mix, wo_ref[...])
        rstd = lax.rsqrt(jnp.mean(out * out, axis=-1, keepdims=True) + NORM_EPS)
        xn_ref[...] = x_ref[...] + v_ref[1:2, :] * ((out * rstd) * v_ref[0:1, :])
        mix_ref[...] = mix
        ya_ref[...] = ya.astype(BF16)
        yb_ref[...] = yb.astype(BF16)
        bg_ref[...] = bg

    def col(j):
        return pl.BlockSpec((tm, D_MODEL), lambda i: (i, j))

    def whole(shape):
        return pl.BlockSpec(shape, lambda i: (0, 0))

    row = col(0)
    bf = jax.ShapeDtypeStruct((S, D_MODEL), BF16)
    return pl.pallas_call(
        body, name="out_fwd", grid=(S // tm,),
        in_specs=[pl.BlockSpec((tm, ATT_WIDTH), lambda i: (i, 0)), row, col(1), col(2), col(3), row,
                  whole((8, D_MODEL)), whole((ATT_WIDTH, D_MODEL)), whole((D_MODEL, D_MODEL)),
                  whole((D_MODEL, D_MODEL))],
        out_specs=(row, row, row, row, row),
        out_shape=(jax.ShapeDtypeStruct((S, D_MODEL), F32), bf, bf, bf, bf),
        compiler_params=_params(("parallel",)))(a_gated, hl, rest, rest, rest, x, vec, w_pa, w_pb, w_o)


def _out_bwd(dxn, mix, ya, yb, hl, rest, o_att, vec, w_pa, w_pb, w_o, hosted=None):
    S = dxn.shape[0]
    tm = OUT_TM
    forth = [_perm_matrix(r, tm) for r in DILATIONS[1:]]

    def body(dxn_ref, mix_ref, ya_ref, yb_ref, hl_ref, gl_ref, ma_ref, mb_ref, ga_ref, o_ref, v_ref,
             wpa_ref, wpb_ref, wo_ref, f1_ref, f2_ref,
             dout_ref, dya_ref, dyb_ref, dhl_ref, dp_ref, acc_ref,
             do0_ref, do1_ref, do2_ref, dv0_ref, dv1_ref, dv2_ref):
        i = pl.program_id(0)

        @pl.when(i == 0)
        def _():
            acc_ref[...] = jnp.zeros_like(acc_ref)

        g_post = v_ref[0:1, :]
        gate = v_ref[1:2, :]
        dxn_v = dxn_ref[...]
        out = _dot(mix_ref[...], wo_ref[...])
        rstd = lax.rsqrt(jnp.mean(out * out, axis=-1, keepdims=True) + NORM_EPS)
        nrm = out * rstd
        acc_ref[0:1, :] += jnp.sum(dxn_v * (nrm * g_post), axis=0, keepdims=True)
        acc_ref[1:2, :] += jnp.sum(dxn_v * gate * nrm, axis=0, keepdims=True)
        dn = dxn_v * (gate * g_post)
        dout = (rstd * (dn - nrm * jnp.mean(dn * nrm, axis=-1, keepdims=True))).astype(BF16)
        dout_ref[...] = dout
        dmix = _dot_nt(dout, wo_ref[...])
        sa = jax.nn.sigmoid(ma_ref[...])
        sb = jax.nn.sigmoid(mb_ref[...])
        dya = (dmix * sa).astype(BF16)
        dyb = (dmix * sb).astype(BF16)
        dya_ref[...] = dya
        dyb_ref[...] = dyb
        dma = dmix * ya_ref[...].astype(F32) * (sa * (1.0 - sa))
        dmb = dmix * yb_ref[...].astype(F32) * (sb * (1.0 - sb))
        d_ag = _dot_nt(dya, wpa_ref[...])
        d_bg = _dot_nt(dyb, wpb_ref[...])
        gl = gl_ref[...]
        hl_v = hl_ref[...]
        dhl_ref[...] = d_bg * _silu(gl)
        dgl = d_bg * hl_v * _dsilu(gl)
        ga = ga_ref[...]
        o = o_ref[...]
        do = d_ag * _silu(ga)
        dga = d_ag * o * _dsilu(ga)
        lane = lax.broadcasted_iota(jnp.int32, (tm, 128), 1)
        dvec = jnp.zeros((tm, 128), F32)
        for h in range(HEADS):
            sl = slice(h * HEAD_DIM, (h + 1) * HEAD_DIM)
            dvec = jnp.where(lane == h, jnp.sum(do[:, sl] * o[:, sl], axis=1, keepdims=True), dvec)
        do_b = do.astype(BF16)
        do0_ref[0] = do_b
        dv0_ref[0] = dvec
        for do_ref, dv_ref, f_ref in ((do1_ref, dv1_ref, f1_ref), (do2_ref, dv2_ref, f2_ref)):
            _store_residues(do_ref, _dot(f_ref[...], do_b).astype(BF16))
            _store_residues(dv_ref, _permute_f32(f_ref[...], dvec, 3))
        dp_ref[0] = dgl[:, 0:ATT_WIDTH].astype(BF16)
        dp_ref[1] = dgl[:, ATT_WIDTH:].astype(BF16)
        dp_ref[2] = dma[:, 0:ATT_WIDTH].astype(BF16)
        dp_ref[3] = dma[:, ATT_WIDTH:].astype(BF16)
        dp_ref[4] = dmb[:, 0:ATT_WIDTH].astype(BF16)
        dp_ref[5] = dmb[:, ATT_WIDTH:].astype(BF16)
        dp_ref[6] = dga.astype(BF16)

    def col(j):
        return pl.BlockSpec((tm, D_MODEL), lambda i: (i, j))

    def whole(shape):
        return pl.BlockSpec(shape, lambda i: (0, 0))

    def res_spec(r, width):
        return pl.BlockSpec((r, tm // r, width), lambda i: (0, i, 0))

    row = col(0)
    att = pl.BlockSpec((tm, ATT_WIDTH), lambda i: (i, 0))
    bf = jax.ShapeDtypeStruct((S, D_MODEL), BF16)
    vmem = pl.BlockSpec(memory_space=pltpu.VMEM)
    return _call(
        body, (dxn, mix, ya, yb, hl, rest, rest, rest, rest, o_att, vec, w_pa, w_pb, w_o, *forth),
        name="out_bwd", grid=(S // tm,),
        in_specs=[row, row, row, row, row, col(1), col(2), col(3),
                  pl.BlockSpec((tm, ATT_WIDTH), lambda i: (i, 8)), att,
                  whole((8, D_MODEL)), whole((ATT_WIDTH, D_MODEL)), whole((D_MODEL, D_MODEL)),
                  whole((D_MODEL, D_MODEL)), vmem, vmem],
        out_specs=(row, row, row, row,
                   pl.BlockSpec((7, tm, ATT_WIDTH), lambda i: (0, i, 0)), whole((8, D_MODEL)),
                   *[res_spec(r, ATT_WIDTH) for r in DILATIONS], *[res_spec(r, 128) for r in DILATIONS]),
        out_shape=(bf, bf, bf, jax.ShapeDtypeStruct((S, D_MODEL), F32),
                   jax.ShapeDtypeStruct((DREST_CHUNKS, S, ATT_WIDTH), BF16),
                   jax.ShapeDtypeStruct((8, D_MODEL), F32),
                   *[jax.ShapeDtypeStruct((r, S // r, ATT_WIDTH), BF16) for r in DILATIONS],
                   *[jax.ShapeDtypeStruct((r, S // r, 128), F32) for r in DILATIONS]),
        hosted=hosted)


def _loss_head(y, target):
    S = y.shape[0]
    tm = 512

    def body(y_ref, t_ref, dy_ref, acc_ref):
        i = pl.program_id(0)

        @pl.when(i == 0)
        def _():
            acc_ref[...] = jnp.zeros_like(acc_ref)

        err = y_ref[...] - t_ref[...]
        dy_ref[...] = err * (1.0 / D_MODEL)
        part = jnp.sum(jnp.sum(err * err, axis=1, keepdims=True), axis=0, keepdims=True)
        acc_ref[...] += jnp.broadcast_to(part, acc_ref.shape)

    row = pl.BlockSpec((tm, D_MODEL), lambda i: (i, 0))
    return pl.pallas_call(
        body, name="loss_head", grid=(S // tm,),
        in_specs=[row, row],
        out_specs=(row, pl.BlockSpec((8, 128), lambda i: (0, 0))),
        out_shape=(jax.ShapeDtypeStruct((S, D_MODEL), F32), jax.ShapeDtypeStruct((8, 128), F32)),
        compiler_params=_params(("arbitrary",)))(y, target)


def _adamw(gsrcs, w, m, v, n_stack, name, rows=None):
    n_layers, R, C = w.shape
    assert len(gsrcs) == n_layers
    budget = 96 * 1024
    tr = rows
    if tr is None:
        tr = R
        while tr * C > budget and tr % 16 == 0:
            tr //= 2
    assert R % tr == 0 and (tr % 8 == 0 or tr == R)
    c1 = 1.0 - ADAM_B1 ** ADAM_STEP
    c2 = 1.0 - ADAM_B2 ** ADAM_STEP

    def body(*refs):
        g_refs = refs[:n_layers]
        w_ref, m_ref, v_ref, go_ref, d_ref, mo_ref, vo_ref = refs[n_layers:]

        def update(g_ref):
            if n_stack:
                g = g_ref[0].astype(F32)
                for s in range(1, n_stack):
                    g = g + g_ref[s].astype(F32)
            else:
                g = g_ref[...]
            m_new = ADAM_B1 * m_ref[...] + (1.0 - ADAM_B1) * g
            v_new = ADAM_B2 * v_ref[...] + (1.0 - ADAM_B2) * (g * g)
            m_hat = m_new / c1
            v_hat = v_new / c2
            go_ref[...] = g
            d_ref[...] = -ADAM_LR * (m_hat / (jnp.sqrt(v_hat) + ADAM_EPS) + ADAM_WD * w_ref[...])
            mo_ref[...] = m_new
            vo_ref[...] = v_new

        for layer in range(n_layers):
            pl.when(pl.program_id(0) == layer)(functools.partial(update, g_refs[layer]))

    def g_spec(layer):
        def rows_of(l, i):
            return jnp.where(l == layer, i, 0)
        if n_stack:
            return pl.BlockSpec((n_stack, tr, C), lambda l, i: (0, rows_of(l, i), 0))
        return pl.BlockSpec((tr, C), lambda l, i: (rows_of(l, i), 0))

    blk = pl.BlockSpec((None, tr, C), lambda l, i: (l, i, 0))
    shp = jax.ShapeDtypeStruct((n_layers, R, C), F32)
    return pl.pallas_call(
        body, name=name, grid=(n_layers, R // tr),
        in_specs=[g_spec(layer) for layer in range(n_layers)] + [blk, blk, blk],
        out_specs=(blk, blk, blk, blk), out_shape=(shp, shp, shp, shp),
        compiler_params=_params(("arbitrary", "arbitrary")))(*gsrcs, w, m, v)


def _pair_blocks(w):
    w = w.reshape(8, 2, 64, 64)
    z = jnp.zeros((8, 64, 64), w.dtype)
    top = jnp.concatenate([w[:, 0], z], axis=2)
    bot = jnp.concatenate([z, w[:, 1]], axis=2)
    return jnp.concatenate([top, bot], axis=1).astype(BF16)


def _unpair_blocks(g):
    return jnp.stack([g[:, :64, :64], g[:, 64:, 64:]], axis=1).reshape(16, 64, 64)


def _layer_params(w_in_g, w_pa_g, w_pb_g, w_o_g, conv_w_full, conv_b, b_rg, b_ig, lru_lambda, w_rg, w_ig,
                  g_pre, g_post):
    w_full = jnp.transpose(w_in_g, (1, 0, 2)).reshape(D_MODEL, N_CHUNKS, ATT_WIDTH)
    w_int = w_full[:, jnp.array(CHUNK_PERM)].reshape(D_MODEL, N_CHUNKS * ATT_WIDTH)
    cw = jnp.concatenate([conv_w_full, conv_b[None], b_rg[None], b_ig[None], lru_lambda[None]], axis=0)
    return dict(
        w_int=w_int,
        w_pa=jnp.transpose(w_pa_g, (1, 0, 2)).reshape(ATT_WIDTH, D_MODEL),
        w_pb=w_pb_g.reshape(D_MODEL, D_MODEL), w_o=w_o_g.reshape(D_MODEL, D_MODEL),
        cw=cw, wr2=_pair_blocks(w_rg), wi2=_pair_blocks(w_ig),
        g_pre=g_pre, g_post=g_post, lru_lambda=lru_lambda)


def _layer_fwd(x, mod, p, host_proj=None, host_lru=None):
    zeros = jnp.zeros((5, D_MODEL), F32)
    vec_pre = jnp.concatenate([p["g_pre"][None], mod[1:2], mod[0:1], zeros], axis=0)
    (ht0, ht1, ht2, q0, q1, q2, rest), got_proj = _norm_proj_fwd(x, vec_pre, p["w_int"], hosted=host_proj)
    h_t = [ht0, ht1, ht2]
    qkv, outs, stats = [q0, q1, q2], [], []
    for g, r in enumerate(DILATIONS):
        o_g, st_g = _attn_fwd(qkv[g], g, r)
        outs.append(o_g)
        stats.append(st_g)
    o_att, a_gated, *lse = _attn_combine(outs, stats, rest)
    (hl, a_dec), got_lru = _lru_fwd(rest, p["cw"], p["wr2"], p["wi2"], hosted=host_lru)
    vec_post = jnp.concatenate([p["g_post"][None], mod[2:3], jnp.zeros((6, D_MODEL), F32)], axis=0)
    x_new, mix, ya, yb, b_gated = _out_fwd(a_gated, hl, rest, x, vec_post, p["w_pa"], p["w_pb"], p["w_o"])
    saved = dict(x=x, h_t=h_t, qkv=qkv, rest=rest, o_att=o_att, a_gated=a_gated, lse=lse, hl=hl, a_dec=a_dec,
                 mix=mix, ya=ya, yb=yb, b_gated=b_gated, vec_pre=vec_pre, vec_post=vec_post)
    return x_new, saved, got_proj, got_lru


def _mats_sources(g_pa, g_pb, g_o):
    return [jnp.transpose(g_pa.astype(BF16).reshape(ATT_WIDTH, N_DEV, 128), (1, 0, 2)),
            g_pb.astype(BF16).reshape(N_DEV, 128, D_MODEL), g_o.astype(BF16).reshape(N_DEV, 128, D_MODEL)]


LRU_SMALL = ("conv_b", "w_rg", "b_rg", "w_ig", "b_ig", "lru_lambda")
MOD_SMALL = ("b_mod", "g_pre", "g_post")


def _flat_rows(arrs):
    return jnp.concatenate([a.reshape(-1) for a in arrs]).reshape(-1, 128)


def _layer_bwd(dxn, sv, p, upper=None, host_own=False):
    n_in = N_CHUNKS * ATT_WIDTH // N_DEV
    a2a3 = ["a2a"] * 3
    (dout, dya, dyb, dhl, drest, acc_out, *do_dvec), got_up_mats = _out_bwd(
        dxn, sv["mix"], sv["ya"], sv["yb"], sv["hl"], sv["rest"], sv["o_att"], sv["vec_post"],
        p["w_pa"], p["w_pb"], p["w_o"], hosted=(upper["mats"], a2a3) if upper else None)
    mats = _mats_sources(_wgrad_tn(sv["a_gated"], dya, "wgrad_pa"), _wgrad_tn(sv["b_gated"], dyb, "wgrad_pb"),
                         _wgrad_tn(sv["mix"], dout, "wgrad_o"))
    (drest, acc_lru, gwr2, gwi2), got_up_in = _lru_bwd(
        dhl, sv["hl"], sv["a_dec"], sv["rest"], p["cw"], p["wr2"], p["wi2"], drest,
        hosted=([upper["w_in"]], ["a2a"]) if upper else None)
    small = dict(
        conv_w=acc_lru[0:4], conv_b=acc_lru[4], b_rg=acc_lru[5], b_ig=acc_lru[6],
        lru_lambda=acc_lru[7] * (-jax.nn.sigmoid(-p["lru_lambda"])),
        w_rg=_unpair_blocks(gwr2), w_ig=_unpair_blocks(gwi2))
    hosts = [None, None, None]
    if host_own:
        lru_vec = _flat_rows([jnp.stack([small[n], upper["small"][n]]) for n in LRU_SMALL + ("conv_w",)])
        hosts = [(mats, a2a3), ([lru_vec], ["ag"]), None]
    dqkv, got_attn = [], []
    for g, r in enumerate(DILATIONS):
        d_g, got = _attn_bwd(sv["qkv"][g], do_dvec[g], sv["lse"][g], do_dvec[3 + g], g, r, hosted=hosts[g])
        dqkv.append(d_g)
        got_attn.append(got)
    parts = [_proj_wgrad_group(sv["h_t"][g], dqkv[g], r, f"proj_wgrad_g{g}") for g, r in enumerate(DILATIONS)]
    parts.append(_proj_wgrad_part(sv["h_t"][0], drest, 2, 4, "proj_wgrad_u"))
    parts.append(_proj_wgrad_part(sv["h_t"][0], drest, 7, 0, "proj_wgrad_rest"))
    gw_int = jnp.concatenate(parts, axis=1).astype(BF16).reshape(D_MODEL, N_CHUNKS, ATT_WIDTH)
    inv_perm = [CHUNK_PERM.index(k) for k in range(N_CHUNKS)]
    gw_in = jnp.transpose(gw_int[:, jnp.array(inv_perm)].reshape(D_MODEL, N_DEV, n_in), (1, 0, 2))
    (dx, acc_pre), got_in = _proj_dgrad_norm_bwd(
        dqkv, drest, p["w_int"], sv["x"], dxn, sv["vec_pre"],
        hosted=([gw_in], ["a2a"]) if host_own else None)
    small.update(dmod=jnp.concatenate([acc_pre[0], acc_pre[1], acc_out[0]]), g_pre=acc_pre[2],
                 g_post=acc_out[1])
    comm = dict(sources=dict(mats=mats, w_in=gw_in, small=small),
                own=dict(mats=got_attn[0], w_in=got_in, lru_small=got_attn[1]),
                upper=dict(mats=got_up_mats, w_in=got_up_in))
    return dx, small, comm


SMALL_NAMES = ("b_mod", "g_pre", "conv_b", "w_rg", "b_rg", "w_ig", "b_ig", "lru_lambda", "g_post")


def kernel(x, c, w_mod, b_mod, g_pre, w_in, conv_w, conv_b, w_rg, b_rg, w_ig, b_ig, lru_lambda, w_pa, w_pb, w_o, g_post, loss_target, m_w_mod, m_b_mod, m_g_pre, m_w_in, m_conv_w, m_conv_b, m_w_rg, m_b_rg, m_w_ig, m_b_ig, m_lru_lambda, m_w_pa, m_w_pb, m_w_o, m_g_post, v_w_mod, v_b_mod, v_g_pre, v_w_in, v_conv_w, v_conv_b, v_w_rg, v_b_rg, v_w_ig, v_b_ig, v_lru_lambda, v_w_pa, v_w_pb, v_w_o, v_g_post):
    W = dict(w_mod=w_mod, b_mod=b_mod, g_pre=g_pre, w_in=w_in, conv_w=conv_w, conv_b=conv_b, w_rg=w_rg,
             b_rg=b_rg, w_ig=w_ig, b_ig=b_ig, lru_lambda=lru_lambda, w_pa=w_pa, w_pb=w_pb, w_o=w_o,
             g_post=g_post)
    M = dict(w_mod=m_w_mod, b_mod=m_b_mod, g_pre=m_g_pre, w_in=m_w_in, conv_w=m_conv_w, conv_b=m_conv_b,
             w_rg=m_w_rg, b_rg=m_b_rg, w_ig=m_w_ig, b_ig=m_b_ig, lru_lambda=m_lru_lambda, w_pa=m_w_pa,
             w_pb=m_w_pb, w_o=m_w_o, g_post=m_g_post)
    V = dict(w_mod=v_w_mod, b_mod=v_b_mod, g_pre=v_g_pre, w_in=v_w_in, conv_w=v_conv_w, conv_b=v_conv_b,
             w_rg=v_w_rg, b_rg=v_b_rg, w_ig=v_w_ig, b_ig=v_b_ig, lru_lambda=v_lru_lambda, w_pa=v_w_pa,
             w_pb=v_w_pb, w_o=v_w_o, g_post=v_g_post)
    S = x.shape[1]
    me = 4 * lax.axis_index("x") + 2 * lax.axis_index("y") + lax.axis_index("c")
    n_mod = w_mod.shape[2]
    n_in = w_in.shape[2]
    n_conv = conv_w.shape[2]

    c_rows = jnp.broadcast_to(c, (8, D_MODEL))
    w_in_b, w_pa_b, w_pb_b, w_o_b = (t.astype(BF16) for t in (w_in, w_pa, w_pb, w_o))
    g_c, g_win0, g_wpa0, g_wpb0, g_wo0, g_cw = _exchange(
        [c_rows, w_in_b[0], w_pa_b[0], w_pb_b[0], w_o_b[0], conv_w], ["ag"] * 6, "gather_weights")
    c_all = g_c[:, 0, :]
    c_pad = jnp.concatenate([c_all, jnp.zeros((8, D_MODEL), F32)], axis=0)
    conv_w_full = jnp.transpose(g_cw, (1, 2, 0, 3)).reshape(2, CONV_WIDTH, D_MODEL)

    mod_cols = _mod_fwd(c_pad, w_mod)
    mod_src = jnp.transpose(mod_cols[:, :8], (1, 0, 2))
    mod_src = jnp.concatenate([mod_src, jnp.zeros((8, 6, n_mod), F32)], axis=1)
    (mod_got,) = _exchange([mod_src], ["a2a"], "scatter_mod")
    mod = jnp.transpose(mod_got[:, :2], (1, 0, 2)).reshape(2, 3 * D_MODEL) + b_mod

    def layer_params(layer, w_in_g, w_pa_g, w_pb_g, w_o_g):
        return _layer_params(w_in_g, w_pa_g, w_pb_g, w_o_g, conv_w_full[layer], conv_b[layer], b_rg[layer],
                             b_ig[layer], lru_lambda[layer], w_rg[layer], w_ig[layer], g_pre[layer],
                             g_post[layer])

    layers = [layer_params(0, g_win0, g_wpa0, g_wpb0, g_wo0), None]
    act, sv0, (g_win1,), (g_wpa1, g_wpb1, g_wo1) = _layer_fwd(
        x[0], mod[0].reshape(3, D_MODEL), layers[0],
        host_proj=([w_in_b[1]], ["ag"]), host_lru=([w_pa_b[1], w_pb_b[1], w_o_b[1]], ["ag"] * 3))
    layers[1] = layer_params(1, g_win1, g_wpa1, g_wpb1, g_wo1)
    act, sv1, _, _ = _layer_fwd(act, mod[1].reshape(3, D_MODEL), layers[1])
    dy, loss_acc = _loss_head(act, loss_target[0])
    loss = lax.psum(0.5 * loss_acc[0, 0] / D_MODEL, MESH_AXES)

    dy, small1, comm1 = _layer_bwd(dy, sv1, layers[1])
    dy, small0, comm0 = _layer_bwd(dy, sv0, layers[0], upper=comm1["sources"], host_own=True)
    grad_x = dy[None]
    r_pa, r_pb, r_o = ([comm0["own"]["mats"][k], comm0["upper"]["mats"][k]] for k in range(3))
    r_in = [comm0["own"]["w_in"][0], comm0["upper"]["w_in"][0]]
    grads = [small0, small1]

    def stack2(name):
        return jnp.stack([grads[0][name], grads[1][name]], axis=0)

    dmod = stack2("dmod")
    dmod_src = jnp.transpose(dmod.reshape(2, N_DEV, n_mod), (1, 0, 2))
    dmod_src = jnp.concatenate([dmod_src, jnp.zeros((8, 6, n_mod), F32)], axis=1)
    mod_vec = _flat_rows([dmod, stack2("g_pre"), stack2("g_post")])
    r_dmod, r_mod_small = _exchange([dmod_src, mod_vec], ["a2a", "ag"], "exchange_grads")
    (r_lru_small,) = comm0["own"]["lru_small"]

    res = {}
    res["w_in"] = _adamw(r_in, w_in, m_w_in, v_w_in, 8, "adamw_w_in")
    res["w_pa"] = _adamw(r_pa, w_pa, m_w_pa, v_w_pa, 8, "adamw_w_pa")
    res["w_pb"] = _adamw(r_pb, w_pb, m_w_pb, v_w_pb, 8, "adamw_w_pb")
    res["w_o"] = _adamw(r_o, w_o, m_w_o, v_w_o, 8, "adamw_w_o")
    dmod_all = jnp.transpose(r_dmod[:, :2], (1, 0, 2))
    gw_mod = _mod_wgrad(jnp.transpose(c_all), dmod_all)
    res["w_mod"] = _adamw([gw_mod[0], gw_mod[1]], w_mod, m_w_mod, v_w_mod, 0, "adamw_w_mod")
    for names_rep, stack, tag in ((LRU_SMALL, r_lru_small, "lru"), (MOD_SMALL, r_mod_small, "mod")):
        rows = sum(W[n].size for n in names_rep) // 128
        rep = _adamw([stack], *(_flat_rows([src[n] for n in names_rep])[None] for src in (W, M, V)), 8,
                     f"adamw_small_{tag}", rows=rows // 2 if rows % 16 == 0 else rows)
        off = 0
        for name in names_rep:
            size = W[name].size
            res[name] = tuple(t.reshape(-1)[off:off + size].reshape(W[name].shape) for t in rep)
            off += size
    lru_rows = sum(W[n].size for n in LRU_SMALL) // 128
    conv_stack = r_lru_small[:, lru_rows:].reshape(8, 2, CONV_WIDTH, D_MODEL)
    conv_stack = lax.dynamic_slice_in_dim(conv_stack, me * n_conv, n_conv, axis=3).reshape(8, 8, n_conv)
    res["conv_w"] = _adamw([conv_stack], conv_w.reshape(1, 8, n_conv), m_conv_w.reshape(1, 8, n_conv),
                           v_conv_w.reshape(1, 8, n_conv), 8, "adamw_conv_w")

    names = ("w_mod", "b_mod", "g_pre", "w_in", "conv_w", "conv_b", "w_rg", "b_rg", "w_ig", "b_ig",
             "lru_lambda", "w_pa", "w_pb", "w_o", "g_post")
    outs = [loss, grad_x]
    for k in range(4):
        outs.extend(res[n][k].reshape(W[n].shape) for n in names)
    return tuple(outs)
```

```python
import functools

import jax
import jax.numpy as jnp
import numpy as np
from jax import lax
from jax.experimental import pallas as pl
from jax.experimental.pallas import tpu as pltpu

F32 = jnp.float32
BF16 = jnp.bfloat16

N_DEV = 8
D_MODEL = 1024
HEAD_DIM = 128
HEADS = 4
ATT_WIDTH = HEADS * HEAD_DIM
DILATIONS = (1, 4, 16)
BAND = 128
N_CHUNKS = 18
QKV_CHUNKS = 9
CONV_WIDTH = 4
LRU_C = 8.0
NORM_EPS = 1e-6
NEG_INF = -1e30
ADAM_LR = 0.001
ADAM_B1 = 0.9
ADAM_B2 = 0.999
ADAM_EPS = 1e-08
ADAM_WD = 0.01
ADAM_STEP = 10
CHUNK_PERM = (0, 3, 6, 1, 4, 7, 2, 5, 8, 10, 11, 12, 13, 14, 15, 16, 17, 9)
DREST_CHUNKS = 10
VMEM_LIMIT = 56 * 1024 * 1024
MESH_AXES = ("x", "y", "c")


def _params(sem=None):
    return pltpu.CompilerParams(dimension_semantics=sem, vmem_limit_bytes=VMEM_LIMIT)


def _silu(x):
    return x * jax.nn.sigmoid(x)


def _dsilu(x):
    s = jax.nn.sigmoid(x)
    return s * (1.0 + x * (1.0 - s))


def _neg_expm1(x):
    series = -x * (1.0 + x * (0.5 + x * (1.0 / 6.0 + x * (1.0 / 24.0 + x * (1.0 / 120.0)))))
    return jnp.where(x > -0.05, series, 1.0 - jnp.exp(x))


def _softplus_neg(lam):
    z = jnp.exp(-jnp.abs(lam))
    small = z * (1.0 - z * (0.5 - z * (1.0 / 3.0 - z * 0.25)))
    log1p_z = jnp.where(z < 1e-2, small, jnp.log(1.0 + z))
    return jnp.maximum(-lam, 0.0) + log1p_z


def _dot(a, b):
    return jnp.dot(a, b, preferred_element_type=F32)


def _dot_nt(a, b):
    return lax.dot_general(a, b, (((1,), (1,)), ((), ())), preferred_element_type=F32)


def _dot_tn(a, b):
    return lax.dot_general(a, b, (((0,), (0,)), ((), ())), preferred_element_type=F32)


def _perm_matrix(r, rows, transpose=False):
    n = rows // r
    p = np.zeros((rows, rows), np.float32)
    dst = np.arange(rows)
    p[dst, (dst % n) * r + dst // n] = 1.0
    return jnp.asarray(p.T if transpose else p, dtype=BF16)


def _permute_f32(p, x, pieces):
    part = x.astype(BF16)
    acc = _dot(p, part)
    for _ in range(pieces - 1):
        x = x - part.astype(F32)
        part = x.astype(BF16)
        acc = acc + _dot(p, part)
    return acc


def _rows_of_residues(ref, lead=()):
    r = ref.shape[len(lead)]
    if r == 1:
        return ref[lead + (0,)]
    return jnp.concatenate([ref[lead + (rho,)] for rho in range(r)], axis=0)


def _store_residues(ref, x, lead=()):
    r = ref.shape[len(lead)]
    n = x.shape[0] // r
    for rho in range(r):
        ref[lead + (rho,)] = x[rho * n:(rho + 1) * n]


ANY_SPEC = pl.BlockSpec(memory_space=pl.ANY)


def _exchange_shapes(arrs, modes):
    return [jax.ShapeDtypeStruct(((N_DEV,) + a.shape) if mode == "ag" else a.shape, a.dtype)
            for a, mode in zip(arrs, modes)]


def _exchange_sems(n):
    return [pltpu.SemaphoreType.DMA((7 * n,)), pltpu.SemaphoreType.DMA((7 * n,)),
            pltpu.SemaphoreType.DMA((n,))]


def _exchange_copies(ins, outs, modes, sems):
    n = len(ins)
    send_sems, recv_sems, local_sems = sems
    x, y, c = lax.axis_index("x"), lax.axis_index("y"), lax.axis_index("c")
    me = 4 * x + 2 * y + c

    def src_for(a, dev):
        return ins[a] if modes[a] == "ag" else ins[a].at[dev]

    local = [pltpu.make_async_copy(src_for(a, me), outs[a].at[me], local_sems.at[a]) for a in range(n)]
    sends, arrivals = [], []
    for k in range(1, N_DEV):
        px = 1 - x if (k >> 2) & 1 else x
        py = 1 - y if (k >> 1) & 1 else y
        pc = 1 - c if k & 1 else c
        peer = 4 * px + 2 * py + pc
        for a in range(n):
            s = (k - 1) * n + a
            for dst, group in ((me, sends), (peer, arrivals)):
                group.append(pltpu.make_async_remote_copy(
                    src_ref=src_for(a, peer), dst_ref=outs[a].at[dst],
                    send_sem=send_sems.at[s], recv_sem=recv_sems.at[s],
                    device_id=(px, py, pc), device_id_type=pl.DeviceIdType.MESH))
    return local, sends, arrivals


def _exchange_start(copies):
    local, sends, _ = copies
    for cp in local + sends:
        cp.start()


def _exchange_wait(copies):
    local, sends, arrivals = copies
    for cp in arrivals:
        cp.wait_recv()
    for cp in sends:
        cp.wait_send()
    for cp in local:
        cp.wait()


def _exchange(arrs, modes, name):
    n = len(arrs)

    def body(*refs):
        copies = _exchange_copies(refs[:n], refs[n:2 * n], modes, refs[2 * n:])
        _exchange_start(copies)
        _exchange_wait(copies)

    outs = pl.pallas_call(
        body, name=name, out_shape=tuple(_exchange_shapes(arrs, modes)),
        in_specs=[ANY_SPEC] * n, out_specs=tuple([ANY_SPEC] * n),
        scratch_shapes=_exchange_sems(n),
    )(*arrs)
    return list(outs)


def _gather_two_level(arrs, name):
    n = len(arrs)

    def body(*refs):
        ins, outs = refs[:n], refs[n:2 * n]
        send_sems, recv_sems, local_sems = refs[2 * n:]
        x, y, c = lax.axis_index("x"), lax.axis_index("y"), lax.axis_index("c")
        me, sibling = (x, y, c), (x, y, 1 - c)
        chips = [(1 - x, y), (x, 1 - y), (1 - x, 1 - y)]

        def copy(a, k, block, to, src=None):
            slot = 4 * block[0] + 2 * block[1] + block[2]
            return pltpu.make_async_remote_copy(
                src_ref=outs[a].at[slot] if src is None else src, dst_ref=outs[a].at[slot],
                send_sem=send_sems.at[7 * a + k], recv_sem=recv_sems.at[7 * a + k],
                device_id=to, device_id_type=pl.DeviceIdType.MESH)

        mine = [pltpu.make_async_copy(ins[a], outs[a].at[4 * x + 2 * y + c], local_sems.at[a])
                for a in range(n)]
        first = []
        for a in range(n):
            first.append(copy(a, 0, me, sibling, src=ins[a]))
            first += [copy(a, 1 + j, me, (*chip, c), src=ins[a]) for j, chip in enumerate(chips)]
        for cp in mine + first:
            cp.start()
        passed = []
        for j, chip in enumerate(chips):
            for a in range(n):
                copy(a, 1 + j, (*chip, c), me).wait_recv()
                passed.append(copy(a, 4 + j, (*chip, c), sibling))
                passed[-1].start()
        for a in range(n):
            copy(a, 0, sibling, me).wait_recv()
            for j, chip in enumerate(chips):
                copy(a, 4 + j, (*chip, 1 - c), me).wait_recv()
        for cp in first + passed:
            cp.wait_send()
        for cp in mine:
            cp.wait()

    outs = pl.pallas_call(
        body, name=name, out_shape=tuple(_exchange_shapes(arrs, ["ag"] * n)),
        in_specs=[ANY_SPEC] * n, out_specs=tuple([ANY_SPEC] * n),
        scratch_shapes=_exchange_sems(n),
    )(*arrs)
    return list(outs)


def _call(body, args, *, name, grid, in_specs, out_specs, out_shape, scratch_shapes=(), aliases=None,
          hosted=None):
    sem = ("arbitrary",) * len(grid)
    if hosted is None:
        outs = pl.pallas_call(
            body, name=name, grid=grid, in_specs=in_specs, out_specs=tuple(out_specs),
            out_shape=tuple(out_shape), scratch_shapes=list(scratch_shapes),
            input_output_aliases=aliases or {}, compiler_params=_params(sem))(*args)
        return list(outs), []
    x_arrs, modes = hosted
    n_in, n_out, n_scr, nx = len(args), len(out_shape), len(scratch_shapes), len(x_arrs)

    def wrapped(*refs):
        ins, x_ins = refs[:n_in], refs[n_in:n_in + nx]
        outs = refs[n_in + nx:n_in + nx + n_out]
        x_outs = refs[n_in + nx + n_out:n_in + 2 * nx + n_out]
        scr = refs[n_in + 2 * nx + n_out:n_in + 2 * nx + n_out + n_scr]
        sems = refs[n_in + 2 * nx + n_out + n_scr:]
        first = pl.program_id(0) == 0
        last = pl.program_id(0) == grid[0] - 1
        for axis in range(1, len(grid)):
            first = jnp.logical_and(first, pl.program_id(axis) == 0)
            last = jnp.logical_and(last, pl.program_id(axis) == grid[axis] - 1)

        @pl.when(first)
        def _():
            _exchange_start(_exchange_copies(x_ins, x_outs, modes, sems))

        body(*ins, *outs, *scr)

        @pl.when(last)
        def _():
            _exchange_wait(_exchange_copies(x_ins, x_outs, modes, sems))

    outs = pl.pallas_call(
        wrapped, name=name, grid=grid, in_specs=list(in_specs) + [ANY_SPEC] * nx,
        out_specs=tuple(out_specs) + tuple([ANY_SPEC] * nx),
        out_shape=tuple(out_shape) + tuple(_exchange_shapes(x_arrs, modes)),
        scratch_shapes=list(scratch_shapes) + _exchange_sems(nx),
        input_output_aliases=aliases or {}, compiler_params=_params(sem))(*args, *x_arrs)
    return list(outs[:n_out]), list(outs[n_out:])


def _mod_fwd(c_pad, w_mod):
    def body(c_ref, w_ref, o_ref):
        sc = _silu(c_ref[...]).astype(BF16)
        for layer in range(2):
            o_ref[layer] = _dot(sc, w_ref[layer].astype(BF16))

    return pl.pallas_call(
        body, name="mod_fwd", out_shape=jax.ShapeDtypeStruct((2, 16, w_mod.shape[2]), F32),
        compiler_params=_params())(c_pad, w_mod)


def _mod_wgrad(c_t, dmod):
    n_cols = dmod.shape[2]

    def body(c_ref, d_ref, o_ref):
        sc = _silu(c_ref[...]).astype(BF16).astype(F32)
        for layer in range(2):
            dm = d_ref[layer].astype(BF16).astype(F32)
            acc = sc[:, 0:1] * dm[0:1, :]
            for b in range(1, N_DEV):
                acc = acc + sc[:, b:b + 1] * dm[b:b + 1, :]
            o_ref[layer] = acc

    return pl.pallas_call(
        body, name="mod_wgrad", out_shape=jax.ShapeDtypeStruct((2, D_MODEL, n_cols), F32),
        compiler_params=_params())(c_t, dmod)


GROUP_COLS = 3 * ATT_WIDTH
PROJ_TM = 256


def _norm_proj_fwd(x, vec, w_int, hosted=None):
    S = x.shape[0]
    tm = PROJ_TM
    perms = [_perm_matrix(r, tm) for r in DILATIONS[1:]]

    def body(x_ref, v_ref, w_ref, p1_ref, p2_ref, ht0_ref, ht1_ref, ht2_ref, q0_ref, q1_ref, q2_ref,
             rest_ref):
        xv = x_ref[...]
        rstd = lax.rsqrt(jnp.mean(xv * xv, axis=-1, keepdims=True) + NORM_EPS)
        hf = ((xv * rstd) * v_ref[0:1, :]) * (1.0 + v_ref[1:2, :]) + v_ref[2:3, :]
        h = hf.astype(BF16)
        for g, (q_ref, ht_ref, p_ref) in enumerate(((q0_ref, ht0_ref, None), (q1_ref, ht1_ref, p1_ref),
                                                    (q2_ref, ht2_ref, p2_ref))):
            rows_f = hf if p_ref is None else _dot(p_ref[...], h)
            ht_ref[...] = rows_f.T.astype(BF16)
            _store_residues(q_ref, _dot(rows_f.astype(BF16),
                                        w_ref[:, g * GROUP_COLS:(g + 1) * GROUP_COLS]).astype(BF16))
        rest_ref[...] = _dot(h, w_ref[:, 3 * GROUP_COLS:])

    n_rest = QKV_CHUNKS * ATT_WIDTH
    whole = pl.BlockSpec(memory_space=pltpu.VMEM)
    ht_spec = pl.BlockSpec((D_MODEL, tm), lambda i: (0, i))
    ht_shape = jax.ShapeDtypeStruct((D_MODEL, S), BF16)
    return _call(
        body, (x, vec, w_int, *perms), name="norm_proj_fwd", grid=(S // tm,),
        in_specs=[pl.BlockSpec((tm, D_MODEL), lambda i: (i, 0)),
                  pl.BlockSpec((8, D_MODEL), lambda i: (0, 0)), whole, whole, whole],
        out_specs=(ht_spec, ht_spec, ht_spec,
                   *[pl.BlockSpec((r, tm // r, GROUP_COLS), lambda i: (0, i, 0)) for r in DILATIONS],
                   pl.BlockSpec((tm, n_rest), lambda i: (i, 0))),
        out_shape=(ht_shape, ht_shape, ht_shape,
                   *[jax.ShapeDtypeStruct((r, S // r, GROUP_COLS), BF16) for r in DILATIONS],
                   jax.ShapeDtypeStruct((S, n_rest), F32)),
        hosted=hosted)


def _proj_dgrad_norm_bwd(dqkv, drest, w_int, x, dxn, vec, hosted=None):
    S = x.shape[0]
    tm = PROJ_TM
    perms = [_perm_matrix(r, tm, transpose=True) for r in DILATIONS[1:]]

    def body(d0_ref, d1_ref, d2_ref, dr_ref, w_ref, p1_ref, p2_ref, x_ref, dxn_ref, v_ref, dx_ref, acc_ref):
        i = pl.program_id(0)

        @pl.when(i == 0)
        def _():
            acc_ref[...] = jnp.zeros_like(acc_ref)

        def wcols(k):
            return w_ref[:, k * ATT_WIDTH:(k + 1) * ATT_WIDTH]

        dhv = None
        for g, (d_ref, p_ref) in enumerate(((d0_ref, None), (d1_ref, p1_ref), (d2_ref, p2_ref))):
            for t in range(3):
                d = _rows_of_residues(d_ref, (t,))
                if p_ref is not None:
                    d = _dot(p_ref[...], d).astype(BF16)
                part = _dot_nt(d, wcols(3 * g + t))
                dhv = part if dhv is None else dhv + part
        for t in range(7):
            dhv = dhv + _dot_nt(dr_ref[t], wcols(11 + t))
        for t in range(2):
            dhv = dhv + _dot_nt(dr_ref[8 + t], wcols(9 + t))

        xv = x_ref[...]
        g = v_ref[0:1, :]
        sc1 = 1.0 + v_ref[1:2, :]
        rstd = lax.rsqrt(jnp.mean(xv * xv, axis=-1, keepdims=True) + NORM_EPS)
        xhat = xv * rstd
        acc_ref[0:1, :] += jnp.sum(dhv, axis=0, keepdims=True)
        acc_ref[1:2, :] += jnp.sum(dhv * (xhat * g), axis=0, keepdims=True)
        acc_ref[2:3, :] += jnp.sum(dhv * xhat * sc1, axis=0, keepdims=True)
        dxhat = dhv * (g * sc1)
        dx = rstd * (dxhat - xhat * jnp.mean(dxhat * xhat, axis=-1, keepdims=True))
        dx_ref[...] = dx + dxn_ref[...]

    row = pl.BlockSpec((tm, D_MODEL), lambda i: (i, 0))
    vec_spec = pl.BlockSpec((8, D_MODEL), lambda i: (0, 0))
    whole = pl.BlockSpec(memory_space=pltpu.VMEM)
    return _call(
        body, (*dqkv, drest, w_int, *perms, x, dxn, vec), name="proj_dgrad_norm_bwd", grid=(S // tm,),
        in_specs=[*[pl.BlockSpec((3, r, tm // r, ATT_WIDTH), lambda i: (0, 0, i, 0)) for r in DILATIONS],
                  pl.BlockSpec((DREST_CHUNKS, tm, ATT_WIDTH), lambda i: (0, i, 0)),
                  whole, whole, whole, row, row, vec_spec],
        out_specs=(row, vec_spec),
        out_shape=(jax.ShapeDtypeStruct((S, D_MODEL), F32), jax.ShapeDtypeStruct((8, D_MODEL), F32)),
        hosted=hosted)


def _proj_wgrad_part(h_t, d, n_chunks, chunk_block, name):
    S = h_t.shape[1]
    ts = 512

    def body(h_ref, d_ref, o_ref):
        i = pl.program_id(0)

        @pl.when(i == 0)
        def _():
            o_ref[...] = jnp.zeros_like(o_ref)

        ht = h_ref[...]
        for t in range(n_chunks):
            o_ref[:, t * ATT_WIDTH:(t + 1) * ATT_WIDTH] += _dot(ht, d_ref[t])

    return pl.pallas_call(
        body, name=name, grid=(S // ts,),
        in_specs=[pl.BlockSpec((D_MODEL, ts), lambda i: (0, i)),
                  pl.BlockSpec((n_chunks, ts, ATT_WIDTH), lambda i: (chunk_block, i, 0))],
        out_specs=pl.BlockSpec((D_MODEL, n_chunks * ATT_WIDTH), lambda i: (0, 0)),
        out_shape=jax.ShapeDtypeStruct((D_MODEL, n_chunks * ATT_WIDTH), F32),
        compiler_params=_params(("arbitrary",)))(h_t, d)


def _proj_wgrad_group(h_t, d, r, name):
    S = h_t.shape[1]
    ts = 512
    n = PROJ_TM // r

    def body(h_ref, d_ref, o_ref):
        i = pl.program_id(0)

        @pl.when(i == 0)
        def _():
            o_ref[...] = jnp.zeros_like(o_ref)

        ht = h_ref[...]
        for t in range(3):
            rows = jnp.concatenate([d_ref[t, rho, s * n:(s + 1) * n] for s in range(ts // PROJ_TM)
                                    for rho in range(r)], axis=0) if r > 1 else d_ref[t, 0]
            o_ref[:, t * ATT_WIDTH:(t + 1) * ATT_WIDTH] += _dot(ht, rows)

    return pl.pallas_call(
        body, name=name, grid=(S // ts,),
        in_specs=[pl.BlockSpec((D_MODEL, ts), lambda i: (0, i)),
                  pl.BlockSpec((3, r, ts // r, ATT_WIDTH), lambda i: (0, 0, i, 0))],
        out_specs=pl.BlockSpec((D_MODEL, GROUP_COLS), lambda i: (0, 0)),
        out_shape=jax.ShapeDtypeStruct((D_MODEL, GROUP_COLS), F32),
        compiler_params=_params(("arbitrary",)))(h_t, d)


def _wgrad_tn(a, b, name):
    S, M = a.shape
    N = b.shape[1]
    ts = 1024

    def body(a_ref, b_ref, o_ref):
        i = pl.program_id(0)
        part = _dot_tn(a_ref[...], b_ref[...])

        @pl.when(i == 0)
        def _():
            o_ref[...] = part

        @pl.when(i > 0)
        def _():
            o_ref[...] += part

    return pl.pallas_call(
        body, name=name, grid=(S // ts,),
        in_specs=[pl.BlockSpec((ts, M), lambda i: (i, 0)), pl.BlockSpec((ts, N), lambda i: (i, 0))],
        out_specs=pl.BlockSpec((M, N), lambda i: (0, 0)),
        out_shape=jax.ShapeDtypeStruct((M, N), F32),
        compiler_params=_params(("arbitrary",)))(a, b)


def _attn_fwd(qkv, g, r):
    L = qkv.shape[1]
    nb2 = L // (2 * BAND)
    scale = HEAD_DIM ** -0.5

    def body(q_ref, kp_ref, kc_ref, vp_ref, vc_ref, o_ref, st_ref):
        m_step = pl.program_id(1)
        ii = lax.broadcasted_iota(jnp.int32, (2 * BAND, 3 * BAND), 0)
        kk = lax.broadcasted_iota(jnp.int32, (2 * BAND, 3 * BAND), 1)
        mask = jnp.logical_and(jnp.logical_and(kk >= ii, kk <= ii + BAND),
                               jnp.logical_or(kk >= BAND, m_step > 0))
        lane = lax.broadcasted_iota(jnp.int32, (2 * BAND, 128), 1)
        stats = jnp.zeros((2 * BAND, 128), F32)
        for h in range(HEADS):
            sl = slice(h * HEAD_DIM, (h + 1) * HEAD_DIM)
            k_cat = jnp.concatenate([kp_ref[:, sl], kc_ref[:, sl]], axis=0)
            v_cat = jnp.concatenate([vp_ref[:, sl], vc_ref[:, sl]], axis=0)
            s = jnp.where(mask, _dot_nt(q_ref[:, sl], k_cat) * scale, NEG_INF)
            m = jnp.max(s, axis=1, keepdims=True)
            p = jnp.exp(s - m)
            l = jnp.sum(p, axis=1, keepdims=True)
            o_ref[:, sl] = _dot(p.astype(BF16), v_cat) / l
            stats = jnp.where(lane == h, m, stats)
            stats = jnp.where(lane == HEADS + h, l, stats)
        st_ref[...] = stats

    two = (None, 2 * BAND, ATT_WIDTH)
    one = (None, BAND, ATT_WIDTH)

    def prev(m):
        return jnp.maximum(2 * m - 1, 0)

    return pl.pallas_call(
        body, name=f"attn_fwd_g{g}", grid=(r, nb2),
        in_specs=[
            pl.BlockSpec(two, lambda rho, m: (rho, m, 0)),
            pl.BlockSpec(one, lambda rho, m: (rho, prev(m), 1)),
            pl.BlockSpec(two, lambda rho, m: (rho, m, 1)),
            pl.BlockSpec(one, lambda rho, m: (rho, prev(m), 2)),
            pl.BlockSpec(two, lambda rho, m: (rho, m, 2)),
        ],
        out_specs=(pl.BlockSpec(two, lambda rho, m: (rho, m, 0)),
                   pl.BlockSpec((None, 2 * BAND, 128), lambda rho, m: (rho, m, 0))),
        out_shape=(jax.ShapeDtypeStruct((r, L, ATT_WIDTH), F32),
                   jax.ShapeDtypeStruct((r, L, 128), F32)),
        compiler_params=_params(("parallel", "parallel")))(qkv, qkv, qkv, qkv, qkv)


def _attn_combine(outs, stats, rest):
    S = rest.shape[0]
    tm = 512
    gatt_blk = 8
    back = [_perm_matrix(r, tm, transpose=True) for r in DILATIONS[1:]]
    forth = [_perm_matrix(r, tm) for r in DILATIONS[1:]]

    def body(o0_ref, o1_ref, o2_ref, s0_ref, s1_ref, s2_ref, g_ref, b1_ref, b2_ref, f1_ref, f2_ref,
             o_ref, a_ref, l0_ref, l1_ref, l2_ref):
        outs_nat = [o0_ref[0]] + [_permute_f32(b_ref[...], _rows_of_residues(o_g), 2)
                                  for o_g, b_ref in ((o1_ref, b1_ref), (o2_ref, b2_ref))]
        st = [s0_ref[0]] + [_permute_f32(b_ref[...], _rows_of_residues(s_g), 3)
                            for s_g, b_ref in ((s1_ref, b1_ref), (s2_ref, b2_ref))]
        lane = lax.broadcasted_iota(jnp.int32, (tm, 128), 1)
        lse_out = jnp.zeros((tm, 128), F32)
        for h in range(HEADS):
            sl = slice(h * HEAD_DIM, (h + 1) * HEAD_DIM)
            ms = [s[:, h:h + 1] for s in st]
            ls = [s[:, HEADS + h:HEADS + h + 1] for s in st]
            m_all = jnp.maximum(jnp.maximum(ms[0], ms[1]), ms[2])
            ws = [l * jnp.exp(m - m_all) for m, l in zip(ms, ls)]
            den = ws[0] + ws[1] + ws[2]
            o = (ws[0] * outs_nat[0][:, sl] + ws[1] * outs_nat[1][:, sl] + ws[2] * outs_nat[2][:, sl]) / den
            o_ref[:, sl] = o
            a_ref[:, sl] = (o * _silu(g_ref[:, sl])).astype(BF16)
            lse_out = jnp.where(lane == h, m_all + jnp.log(den), lse_out)
        l0_ref[0] = lse_out
        for l_ref, f_ref in ((l1_ref, f1_ref), (l2_ref, f2_ref)):
            _store_residues(l_ref, _permute_f32(f_ref[...], lse_out, 3))

    o_spec = pl.BlockSpec((tm, ATT_WIDTH), lambda i: (i, 0))
    whole = pl.BlockSpec(memory_space=pltpu.VMEM)

    def res_spec(r, width):
        return pl.BlockSpec((r, tm // r, width), lambda i: (0, i, 0))

    return pl.pallas_call(
        body, name="attn_combine", grid=(S // tm,),
        in_specs=[*[res_spec(r, ATT_WIDTH) for r in DILATIONS], *[res_spec(r, 128) for r in DILATIONS],
                  pl.BlockSpec((tm, ATT_WIDTH), lambda i: (i, gatt_blk)), whole, whole, whole, whole],
        out_specs=(o_spec, o_spec, *[res_spec(r, 128) for r in DILATIONS]),
        out_shape=(jax.ShapeDtypeStruct((S, ATT_WIDTH), F32), jax.ShapeDtypeStruct((S, ATT_WIDTH), BF16),
                   *[jax.ShapeDtypeStruct((r, S // r, 128), F32) for r in DILATIONS]),
        compiler_params=_params(("parallel",)))(*outs, *stats, rest, *back, *forth)


def _attn_bwd(qkv, do, lse, dvec, g, r, hosted=None):
    L = qkv.shape[1]
    nb = L // BAND
    nb2 = nb // 2
    scale = HEAD_DIM ** -0.5

    def body(qc_ref, qn_ref, k_ref, v_ref, doc_ref, don_ref, lc_ref, ln_ref, dc_ref, dn_ref,
             out_ref, carry_ref):
        j = pl.program_id(1)

        @pl.when(j == 0)
        def _():
            carry_ref[...] = jnp.zeros_like(carry_ref)

        ii = lax.broadcasted_iota(jnp.int32, (3 * BAND, 2 * BAND), 0)
        kk = lax.broadcasted_iota(jnp.int32, (3 * BAND, 2 * BAND), 1)
        mask = jnp.logical_and(jnp.logical_and(kk <= ii, kk >= ii - BAND),
                               jnp.logical_or(ii < 2 * BAND, j < nb2 - 1))
        lse3 = jnp.concatenate([lc_ref[...], ln_ref[...]], axis=0)
        dvec3 = jnp.concatenate([dc_ref[...], dn_ref[...]], axis=0)
        for h in range(HEADS):
            sl = slice(h * HEAD_DIM, (h + 1) * HEAD_DIM)
            k = k_ref[:, sl]
            v = v_ref[:, sl]
            q = jnp.concatenate([qc_ref[:, sl], qn_ref[:, sl]], axis=0)
            do = jnp.concatenate([doc_ref[:, sl], don_ref[:, sl]], axis=0)
            p = jnp.where(mask, jnp.exp(_dot_nt(q, k) * scale - lse3[:, h:h + 1]), 0.0)
            ds = (p * (_dot_nt(do, v) - dvec3[:, h:h + 1])).astype(BF16)
            dq3 = _dot(ds, k) * scale
            out_ref[0, 0:BAND, sl] = (carry_ref[:, sl] + dq3[:BAND]).astype(BF16)
            out_ref[0, BAND:, sl] = dq3[BAND:2 * BAND].astype(BF16)
            out_ref[1, :, sl] = (_dot_tn(ds, q) * scale).astype(BF16)
            out_ref[2, :, sl] = _dot_tn(p.astype(BF16), do).astype(BF16)
            carry_ref[:, sl] = dq3[2 * BAND:]

    two = (None, 2 * BAND, ATT_WIDTH)
    one = (None, BAND, ATT_WIDTH)
    stwo = (None, 2 * BAND, 128)
    sone = (None, BAND, 128)

    def nxt(j):
        return jnp.minimum(2 * j + 2, nb - 1)

    (out,), got = _call(
        body, (qkv, qkv, qkv, qkv, do, do, lse, lse, dvec, dvec), name=f"attn_bwd_g{g}", grid=(r, nb2),
        in_specs=[
            pl.BlockSpec(two, lambda rho, j: (rho, j, 0)),
            pl.BlockSpec(one, lambda rho, j: (rho, nxt(j), 0)),
            pl.BlockSpec(two, lambda rho, j: (rho, j, 1)),
            pl.BlockSpec(two, lambda rho, j: (rho, j, 2)),
            pl.BlockSpec(two, lambda rho, j: (rho, j, 0)),
            pl.BlockSpec(one, lambda rho, j: (rho, nxt(j), 0)),
            pl.BlockSpec(stwo, lambda rho, j: (rho, j, 0)),
            pl.BlockSpec(sone, lambda rho, j: (rho, nxt(j), 0)),
            pl.BlockSpec(stwo, lambda rho, j: (rho, j, 0)),
            pl.BlockSpec(sone, lambda rho, j: (rho, nxt(j), 0)),
        ],
        out_specs=(pl.BlockSpec((3, None, 2 * BAND, ATT_WIDTH), lambda rho, j: (0, rho, j, 0)),),
        out_shape=(jax.ShapeDtypeStruct((3, r, L, ATT_WIDTH), BF16),),
        scratch_shapes=[pltpu.VMEM((BAND, ATT_WIDTH), F32)],
        hosted=hosted)
    return out, got


LRU_T = 256


def _linear_scan(a, b, carry, reverse):
    T = a.shape[0]
    row8 = lax.broadcasted_iota(jnp.int32, a.shape, 0) & 7
    for s in (1, 2, 4):
        keep = (row8 < 8 - s) if reverse else (row8 >= s)
        shift = T - s if reverse else s
        b_sh = jnp.where(keep, pltpu.roll(b, shift, 0), 0.0)
        a_sh = jnp.where(keep, pltpu.roll(a, shift, 0), 1.0)
        b = a * b_sh + b
        a = a * a_sh
    tiles = [None] * (T // 8)
    order = range(T // 8 - 1, -1, -1) if reverse else range(T // 8)
    for k in order:
        y = b[8 * k:8 * k + 8] + a[8 * k:8 * k + 8] * carry
        tiles[k] = y
        carry = y[0:1] if reverse else y[7:8]
    return jnp.concatenate(tiles, axis=0), carry


def _gate_matmuls(ucb, w_ref, bias):
    parts = [_dot(ucb[:, j * 128:(j + 1) * 128], w_ref[j]) for j in range(8)]
    return jnp.concatenate(parts, axis=1) + bias


def _conv_fwd(u, u_prev8, cw_ref):
    T = u.shape[0]
    ue = jnp.concatenate([u_prev8, u], axis=0)
    uc = cw_ref[4:5, :] + cw_ref[0:1, :] * u
    for j in range(1, CONV_WIDTH):
        uc = uc + cw_ref[j:j + 1, :] * pltpu.roll(ue, j, 0)[8:8 + T]
    return uc


def _lru_fwd(rest, cw, wr2, wi2, hosted=None):
    S = rest.shape[0]
    T = LRU_T

    def body(u_ref, cw_ref, wr_ref, wi_ref, h_ref, a_ref, uc_ref, r_ref, ig_ref, mult_ref,
             ucar_ref, hcar_ref):
        c = pl.program_id(0)

        @pl.when(c == 0)
        def _():
            ucar_ref[...] = jnp.zeros_like(ucar_ref)
            hcar_ref[...] = jnp.zeros_like(hcar_ref)

        u = u_ref[...]
        uc = _conv_fwd(u, ucar_ref[...], cw_ref)
        ucar_ref[...] = u[T - 8:, :]
        ucb = uc.astype(BF16)
        r = jax.nn.sigmoid(_gate_matmuls(ucb, wr_ref, cw_ref[5:6, :]))
        ig = jax.nn.sigmoid(_gate_matmuls(ucb, wi_ref, cw_ref[6:7, :]))
        log_a = -LRU_C * r * _softplus_neg(cw_ref[7:8, :])
        a = jnp.exp(log_a)
        mult = jnp.sqrt(_neg_expm1(2.0 * log_a))
        b = mult * (ig * uc)
        a_ref[...] = a
        uc_ref[...] = uc
        r_ref[...] = r
        ig_ref[...] = ig
        mult_ref[...] = mult
        h, last = _linear_scan(a, b, hcar_ref[0:1, :], reverse=False)
        h_ref[...] = h
        hcar_ref[...] = jnp.broadcast_to(last, (8, D_MODEL))

    row_spec = pl.BlockSpec((T, D_MODEL), lambda c: (c, 0))
    row_shape = jax.ShapeDtypeStruct((S, D_MODEL), F32)
    return _call(
        body, (rest, cw, wr2, wi2), name="lru_fwd", grid=(S // T,),
        in_specs=[row_spec, pl.BlockSpec((8, D_MODEL), lambda c: (0, 0)),
                  pl.BlockSpec((8, 128, 128), lambda c: (0, 0, 0)),
                  pl.BlockSpec((8, 128, 128), lambda c: (0, 0, 0))],
        out_specs=(row_spec,) * 6, out_shape=(row_shape,) * 6,
        scratch_shapes=[pltpu.VMEM((8, D_MODEL), F32), pltpu.VMEM((8, D_MODEL), F32)],
        hosted=hosted)


def _lru_bwd(dhl, hl, a_all, gates, rest, cw, wr2, wi2, dproj, hosted=None):
    S = rest.shape[0]
    T = LRU_T
    nc = S // T

    def body(dh_ref, h_ref, hp_ref, a_ref, uc_ref, r_ref, ig_ref, mult_ref, u_ref, up_ref, cw_ref, wr_ref,
             wi_ref, _alias, du_ref, acc_ref, gwr_ref, gwi_ref, gcar_ref, acar_ref, dcar_ref):
        step = pl.program_id(0)
        c = nc - 1 - step

        @pl.when(step == 0)
        def _():
            acc_ref[...] = jnp.zeros_like(acc_ref)
            gwr_ref[...] = jnp.zeros_like(gwr_ref)
            gwi_ref[...] = jnp.zeros_like(gwi_ref)
            gcar_ref[...] = jnp.zeros_like(gcar_ref)
            acar_ref[...] = jnp.zeros_like(acar_ref)
            dcar_ref[...] = jnp.zeros_like(dcar_ref)

        row = lax.broadcasted_iota(jnp.int32, (T, D_MODEL), 0)
        u = u_ref[...]
        u_prev = jnp.where(c > 0, up_ref[...], 0.0)
        h_prev8 = jnp.where(c > 0, hp_ref[...], 0.0)
        a = a_ref[...]
        h = h_ref[...]
        uc, r, ig, mult = uc_ref[...], r_ref[...], ig_ref[...], mult_ref[...]
        ucb = uc.astype(BF16)
        sp = _softplus_neg(cw_ref[7:8, :])
        a_next = jnp.where(row < T - 1, pltpu.roll(a, T - 1, 0), acar_ref[0:1, :])
        G, first = _linear_scan(a_next, dh_ref[...], gcar_ref[0:1, :], reverse=True)
        gcar_ref[...] = jnp.broadcast_to(first, (8, D_MODEL))
        acar_ref[...] = jnp.broadcast_to(a[0:1, :], (8, D_MODEL))
        he = jnp.concatenate([h_prev8, h], axis=0)
        h_before = pltpu.roll(he, 1, 0)[8:8 + T]
        d_a = G * h_before
        d_mult = G * (ig * uc)
        d_ig = G * (mult * uc)
        duc = G * (mult * ig)
        d_log_a = d_a * a - d_mult * (a * a) / mult
        d_r = d_log_a * (-LRU_C * sp)
        d_sp = jnp.sum(d_log_a * (-LRU_C * r), axis=0, keepdims=True)
        dpre_r = d_r * r * (1.0 - r)
        dpre_i = d_ig * ig * (1.0 - ig)
        dprb = dpre_r.astype(BF16)
        dpib = dpre_i.astype(BF16)
        back = []
        for j in range(8):
            sl = slice(j * 128, (j + 1) * 128)
            back.append(_dot_nt(dprb[:, sl], wr_ref[j]) + _dot_nt(dpib[:, sl], wi_ref[j]))
            gwr_ref[j] += _dot_tn(ucb[:, sl], dprb[:, sl])
            gwi_ref[j] += _dot_tn(ucb[:, sl], dpib[:, sl])
        duc = duc + jnp.concatenate(back, axis=1)
        de = jnp.concatenate([duc, dcar_ref[...]], axis=0)
        du = cw_ref[0:1, :] * duc
        ue = jnp.concatenate([u_prev, u], axis=0)
        acc_ref[0:1, :] += jnp.sum(duc * u, axis=0, keepdims=True)
        for j in range(1, CONV_WIDTH):
            du = du + cw_ref[j:j + 1, :] * pltpu.roll(de, T + 8 - j, 0)[0:T]
            acc_ref[j:j + 1, :] += jnp.sum(duc * pltpu.roll(ue, j, 0)[8:8 + T], axis=0, keepdims=True)
        dcar_ref[...] = duc[0:8, :]
        acc_ref[4:5, :] += jnp.sum(duc, axis=0, keepdims=True)
        acc_ref[5:6, :] += jnp.sum(dpre_r, axis=0, keepdims=True)
        acc_ref[6:7, :] += jnp.sum(dpre_i, axis=0, keepdims=True)
        acc_ref[7:8, :] += d_sp
        du_ref[0] = du[:, 0:ATT_WIDTH].astype(BF16)
        du_ref[1] = du[:, ATT_WIDTH:].astype(BF16)

    def rev(step):
        return nc - 1 - step

    def prev8(step):
        return jnp.maximum(rev(step) * (T // 8) - 1, 0)

    row_spec = pl.BlockSpec((T, D_MODEL), lambda s: (rev(s), 0))
    prev_spec = pl.BlockSpec((8, D_MODEL), lambda s: (prev8(s), 0))
    vec_spec = pl.BlockSpec((8, D_MODEL), lambda s: (0, 0))
    w_spec = pl.BlockSpec((8, 128, 128), lambda s: (0, 0, 0))
    return _call(
        body, (dhl, hl, hl, a_all, *gates, rest, rest, cw, wr2, wi2, dproj), name="lru_bwd", grid=(nc,),
        in_specs=[row_spec, row_spec, prev_spec, row_spec, row_spec, row_spec, row_spec, row_spec,
                  row_spec, prev_spec, vec_spec, w_spec, w_spec, ANY_SPEC],
        out_specs=(pl.BlockSpec((2, T, ATT_WIDTH), lambda s: (4, rev(s), 0)), vec_spec, w_spec, w_spec),
        out_shape=(jax.ShapeDtypeStruct(dproj.shape, BF16), jax.ShapeDtypeStruct((8, D_MODEL), F32),
                   jax.ShapeDtypeStruct((8, 128, 128), F32), jax.ShapeDtypeStruct((8, 128, 128), F32)),
        scratch_shapes=[pltpu.VMEM((8, D_MODEL), F32), pltpu.VMEM((8, D_MODEL), F32),
                        pltpu.VMEM((8, D_MODEL), F32)],
        aliases={13: 0}, hosted=hosted)


OUT_TM = 256


def _out_fwd(a_gated, hl, rest, x, vec, w_pa, w_pb, w_o):
    S = x.shape[0]
    tm = OUT_TM

    def body(ag_ref, hl_ref, gl_ref, ma_ref, mb_ref, x_ref, v_ref, wpa_ref, wpb_ref, wo_ref,
             xn_ref, mix_ref, ya_ref, yb_ref, bg_ref):
        bg = (hl_ref[...] * _silu(gl_ref[...])).astype(BF16)
        ya = _dot(ag_ref[...], wpa_ref[...])
        yb = _dot(bg, wpb_ref[...])
        mix = (jax.nn.sigmoid(ma_ref[...]) * ya + jax.nn.sigmoid(mb_ref[...]) * yb).astype(BF16)
        out = _dot(mix, wo_ref[...])
        rstd = lax.rsqrt(jnp.mean(out * out, axis=-1, keepdims=True) + NORM_EPS)
        xn_ref[...] = x_ref[...] + v_ref[1:2, :] * ((out * rstd) * v_ref[0:1, :])
        mix_ref[...] = mix
        ya_ref[...] = ya.astype(BF16)
        yb_ref[...] = yb.astype(BF16)
        bg_ref[...] = bg

    def col(j):
        return pl.BlockSpec((tm, D_MODEL), lambda i: (i, j))

    def whole(shape):
        return pl.BlockSpec(shape, lambda i: (0, 0))

    row = col(0)
    bf = jax.ShapeDtypeStruct((S, D_MODEL), BF16)
    return pl.pallas_call(
        body, name="out_fwd", grid=(S // tm,),
        in_specs=[pl.BlockSpec((tm, ATT_WIDTH), lambda i: (i, 0)), row, col(1), col(2), col(3), row,
                  whole((8, D_MODEL)), whole((ATT_WIDTH, D_MODEL)), whole((D_MODEL, D_MODEL)),
                  whole((D_MODEL, D_MODEL))],
        out_specs=(row, row, row, row, row),
        out_shape=(jax.ShapeDtypeStruct((S, D_MODEL), F32), bf, bf, bf, bf),
        compiler_params=_params(("parallel",)))(a_gated, hl, rest, rest, rest, x, vec, w_pa, w_pb, w_o)


def _out_bwd(dxn, mix, ya, yb, hl, rest, o_att, vec, w_pa, w_pb, w_o, hosted=None):
    S = dxn.shape[0]
    tm = OUT_TM
    forth = [_perm_matrix(r, tm) for r in DILATIONS[1:]]

    def body(dxn_ref, mix_ref, ya_ref, yb_ref, hl_ref, gl_ref, ma_ref, mb_ref, ga_ref, o_ref, v_ref,
             wpa_ref, wpb_ref, wo_ref, f1_ref, f2_ref,
             dout_ref, dya_ref, dyb_ref, dhl_ref, dp_ref, acc_ref,
             do0_ref, do1_ref, do2_ref, dv0_ref, dv1_ref, dv2_ref):
        i = pl.program_id(0)

        @pl.when(i == 0)
        def _():
            acc_ref[...] = jnp.zeros_like(acc_ref)

        g_post = v_ref[0:1, :]
        gate = v_ref[1:2, :]
        dxn_v = dxn_ref[...]
        out = _dot(mix_ref[...], wo_ref[...])
        rstd = lax.rsqrt(jnp.mean(out * out, axis=-1, keepdims=True) + NORM_EPS)
        nrm = out * rstd
        acc_ref[0:1, :] += jnp.sum(dxn_v * (nrm * g_post), axis=0, keepdims=True)
        acc_ref[1:2, :] += jnp.sum(dxn_v * gate * nrm, axis=0, keepdims=True)
        dn = dxn_v * (gate * g_post)
        dout = (rstd * (dn - nrm * jnp.mean(dn * nrm, axis=-1, keepdims=True))).astype(BF16)
        dout_ref[...] = dout
        dmix = _dot_nt(dout, wo_ref[...])
        sa = jax.nn.sigmoid(ma_ref[...])
        sb = jax.nn.sigmoid(mb_ref[...])
        dya = (dmix * sa).astype(BF16)
        dyb = (dmix * sb).astype(BF16)
        dya_ref[...] = dya
        dyb_ref[...] = dyb
        dma = dmix * ya_ref[...].astype(F32) * (sa * (1.0 - sa))
        dmb = dmix * yb_ref[...].astype(F32) * (sb * (1.0 - sb))
        d_ag = _dot_nt(dya, wpa_ref[...])
        d_bg = _dot_nt(dyb, wpb_ref[...])
        gl = gl_ref[...]
        hl_v = hl_ref[...]
        dhl_ref[...] = d_bg * _silu(gl)
        dgl = d_bg * hl_v * _dsilu(gl)
        ga = ga_ref[...]
        o = o_ref[...]
        do = d_ag * _silu(ga)
        dga = d_ag * o * _dsilu(ga)
        lane = lax.broadcasted_iota(jnp.int32, (tm, 128), 1)
        dvec = jnp.zeros((tm, 128), F32)
        for h in range(HEADS):
            sl = slice(h * HEAD_DIM, (h + 1) * HEAD_DIM)
            dvec = jnp.where(lane == h, jnp.sum(do[:, sl] * o[:, sl], axis=1, keepdims=True), dvec)
        do_b = do.astype(BF16)
        do0_ref[0] = do_b
        dv0_ref[0] = dvec
        for do_ref, dv_ref, f_ref in ((do1_ref, dv1_ref, f1_ref), (do2_ref, dv2_ref, f2_ref)):
            _store_residues(do_ref, _dot(f_ref[...], do_b).astype(BF16))
            _store_residues(dv_ref, _permute_f32(f_ref[...], dvec, 3))
        dp_ref[0] = dgl[:, 0:ATT_WIDTH].astype(BF16)
        dp_ref[1] = dgl[:, ATT_WIDTH:].astype(BF16)
        dp_ref[2] = dma[:, 0:ATT_WIDTH].astype(BF16)
        dp_ref[3] = dma[:, ATT_WIDTH:].astype(BF16)
        dp_ref[4] = dmb[:, 0:ATT_WIDTH].astype(BF16)
        dp_ref[5] = dmb[:, ATT_WIDTH:].astype(BF16)
        dp_ref[6] = dga.astype(BF16)

    def col(j):
        return pl.BlockSpec((tm, D_MODEL), lambda i: (i, j))

    def whole(shape):
        return pl.BlockSpec(shape, lambda i: (0, 0))

    def res_spec(r, width):
        return pl.BlockSpec((r, tm // r, width), lambda i: (0, i, 0))

    row = col(0)
    att = pl.BlockSpec((tm, ATT_WIDTH), lambda i: (i, 0))
    bf = jax.ShapeDtypeStruct((S, D_MODEL), BF16)
    vmem = pl.BlockSpec(memory_space=pltpu.VMEM)
    return _call(
        body, (dxn, mix, ya, yb, hl, rest, rest, rest, rest, o_att, vec, w_pa, w_pb, w_o, *forth),
        name="out_bwd", grid=(S // tm,),
        in_specs=[row, row, row, row, row, col(1), col(2), col(3),
                  pl.BlockSpec((tm, ATT_WIDTH), lambda i: (i, 8)), att,
                  whole((8, D_MODEL)), whole((ATT_WIDTH, D_MODEL)), whole((D_MODEL, D_MODEL)),
                  whole((D_MODEL, D_MODEL)), vmem, vmem],
        out_specs=(row, row, row, row,
                   pl.BlockSpec((7, tm, ATT_WIDTH), lambda i: (0, i, 0)), whole((8, D_MODEL)),
                   *[res_spec(r, ATT_WIDTH) for r in DILATIONS], *[res_spec(r, 128) for r in DILATIONS]),
        out_shape=(bf, bf, bf, jax.ShapeDtypeStruct((S, D_MODEL), F32),
                   jax.ShapeDtypeStruct((DREST_CHUNKS, S, ATT_WIDTH), BF16),
                   jax.ShapeDtypeStruct((8, D_MODEL), F32),
                   *[jax.ShapeDtypeStruct((r, S // r, ATT_WIDTH), BF16) for r in DILATIONS],
                   *[jax.ShapeDtypeStruct((r, S // r, 128), F32) for r in DILATIONS]),
        hosted=hosted)


def _loss_head(y, target):
    S = y.shape[0]
    tm = 512

    def body(y_ref, t_ref, dy_ref, acc_ref):
        i = pl.program_id(0)

        @pl.when(i == 0)
        def _():
            acc_ref[...] = jnp.zeros_like(acc_ref)

        err = y_ref[...] - t_ref[...]
        dy_ref[...] = err * (1.0 / D_MODEL)
        part = jnp.sum(jnp.sum(err * err, axis=1, keepdims=True), axis=0, keepdims=True)
        acc_ref[...] += jnp.broadcast_to(part, acc_ref.shape)

    row = pl.BlockSpec((tm, D_MODEL), lambda i: (i, 0))
    return pl.pallas_call(
        body, name="loss_head", grid=(S // tm,),
        in_specs=[row, row],
        out_specs=(row, pl.BlockSpec((8, 128), lambda i: (0, 0))),
        out_shape=(jax.ShapeDtypeStruct((S, D_MODEL), F32), jax.ShapeDtypeStruct((8, 128), F32)),
        compiler_params=_params(("arbitrary",)))(y, target)


def _adamw(gsrcs, w, m, v, n_stack, name, rows=None):
    n_layers, R, C = w.shape
    assert len(gsrcs) == n_layers
    budget = 96 * 1024
    tr = rows
    if tr is None:
        tr = R
        while tr * C > budget and tr % 16 == 0:
            tr //= 2
    assert R % tr == 0 and (tr % 8 == 0 or tr == R)
    c1 = 1.0 - ADAM_B1 ** ADAM_STEP
    c2 = 1.0 - ADAM_B2 ** ADAM_STEP

    def body(*refs):
        g_refs = refs[:n_layers]
        w_ref, m_ref, v_ref, go_ref, d_ref, mo_ref, vo_ref = refs[n_layers:]

        def update(g_ref):
            if n_stack:
                g = g_ref[0].astype(F32)
                for s in range(1, n_stack):
                    g = g + g_ref[s].astype(F32)
            else:
                g = g_ref[...]
            m_new = ADAM_B1 * m_ref[...] + (1.0 - ADAM_B1) * g
            v_new = ADAM_B2 * v_ref[...] + (1.0 - ADAM_B2) * (g * g)
            m_hat = m_new / c1
            v_hat = v_new / c2
            go_ref[...] = g
            d_ref[...] = -ADAM_LR * (m_hat / (jnp.sqrt(v_hat) + ADAM_EPS) + ADAM_WD * w_ref[...])
            mo_ref[...] = m_new
            vo_ref[...] = v_new

        for layer in range(n_layers):
            pl.when(pl.program_id(0) == layer)(functools.partial(update, g_refs[layer]))

    def g_spec(layer):
        def rows_of(l, i):
            return jnp.where(l == layer, i, 0)
        if n_stack:
            return pl.BlockSpec((n_stack, tr, C), lambda l, i: (0, rows_of(l, i), 0))
        return pl.BlockSpec((tr, C), lambda l, i: (rows_of(l, i), 0))

    blk = pl.BlockSpec((None, tr, C), lambda l, i: (l, i, 0))
    shp = jax.ShapeDtypeStruct((n_layers, R, C), F32)
    return pl.pallas_call(
        body, name=name, grid=(n_layers, R // tr),
        in_specs=[g_spec(layer) for layer in range(n_layers)] + [blk, blk, blk],
        out_specs=(blk, blk, blk, blk), out_shape=(shp, shp, shp, shp),
        compiler_params=_params(("arbitrary", "arbitrary")))(*gsrcs, w, m, v)


def _pair_blocks(w):
    w = w.reshape(8, 2, 64, 64)
    z = jnp.zeros((8, 64, 64), w.dtype)
    top = jnp.concatenate([w[:, 0], z], axis=2)
    bot = jnp.concatenate([z, w[:, 1]], axis=2)
    return jnp.concatenate([top, bot], axis=1).astype(BF16)


def _unpair_blocks(g):
    return jnp.stack([g[:, :64, :64], g[:, 64:, 64:]], axis=1).reshape(16, 64, 64)


def _layer_params(w_in_g, w_pa_g, w_pb_g, w_o_g, conv_w_full, conv_b, b_rg, b_ig, lru_lambda, w_rg, w_ig,
                  g_pre, g_post):
    w_full = jnp.transpose(w_in_g, (1, 0, 2)).reshape(D_MODEL, N_CHUNKS, ATT_WIDTH)
    w_int = w_full[:, jnp.array(CHUNK_PERM)].reshape(D_MODEL, N_CHUNKS * ATT_WIDTH)
    cw = jnp.concatenate([conv_w_full, conv_b[None], b_rg[None], b_ig[None], lru_lambda[None]], axis=0)
    return dict(
        w_int=w_int,
        w_pa=jnp.transpose(w_pa_g, (1, 0, 2)).reshape(ATT_WIDTH, D_MODEL),
        w_pb=w_pb_g.reshape(D_MODEL, D_MODEL), w_o=w_o_g.reshape(D_MODEL, D_MODEL),
        cw=cw, wr2=_pair_blocks(w_rg), wi2=_pair_blocks(w_ig),
        g_pre=g_pre, g_post=g_post, lru_lambda=lru_lambda)


def _layer_fwd(x, mod, p, host_proj=None, host_lru=None):
    zeros = jnp.zeros((5, D_MODEL), F32)
    vec_pre = jnp.concatenate([p["g_pre"][None], mod[1:2], mod[0:1], zeros], axis=0)
    (ht0, ht1, ht2, q0, q1, q2, rest), got_proj = _norm_proj_fwd(x, vec_pre, p["w_int"], hosted=host_proj)
    h_t = [ht0, ht1, ht2]
    qkv, outs, stats = [q0, q1, q2], [], []
    for g, r in enumerate(DILATIONS):
        o_g, st_g = _attn_fwd(qkv[g], g, r)
        outs.append(o_g)
        stats.append(st_g)
    o_att, a_gated, *lse = _attn_combine(outs, stats, rest)
    (hl, a_dec, *gates), got_lru = _lru_fwd(rest, p["cw"], p["wr2"], p["wi2"], hosted=host_lru)
    vec_post = jnp.concatenate([p["g_post"][None], mod[2:3], jnp.zeros((6, D_MODEL), F32)], axis=0)
    x_new, mix, ya, yb, b_gated = _out_fwd(a_gated, hl, rest, x, vec_post, p["w_pa"], p["w_pb"], p["w_o"])
    saved = dict(x=x, h_t=h_t, qkv=qkv, rest=rest, o_att=o_att, a_gated=a_gated, lse=lse, hl=hl, a_dec=a_dec, gates=gates,
                 mix=mix, ya=ya, yb=yb, b_gated=b_gated, vec_pre=vec_pre, vec_post=vec_post)
    return x_new, saved, got_proj, got_lru


def _mats_sources(g_pa, g_pb, g_o):
    return [jnp.transpose(g_pa.astype(BF16).reshape(ATT_WIDTH, N_DEV, 128), (1, 0, 2)),
            g_pb.astype(BF16).reshape(N_DEV, 128, D_MODEL), g_o.astype(BF16).reshape(N_DEV, 128, D_MODEL)]


LRU_SMALL = ("conv_b", "w_rg", "b_rg", "w_ig", "b_ig", "lru_lambda")
MOD_SMALL = ("b_mod", "g_pre", "g_post")


def _flat_rows(arrs):
    return jnp.concatenate([a.reshape(-1) for a in arrs]).reshape(-1, 128)


def _layer_bwd(dxn, sv, p, upper=None, host_own=False):
    n_in = N_CHUNKS * ATT_WIDTH // N_DEV
    a2a3 = ["a2a"] * 3
    (dout, dya, dyb, dhl, drest, acc_out, *do_dvec), got_up_mats = _out_bwd(
        dxn, sv["mix"], sv["ya"], sv["yb"], sv["hl"], sv["rest"], sv["o_att"], sv["vec_post"],
        p["w_pa"], p["w_pb"], p["w_o"], hosted=(upper["mats"], a2a3) if upper else None)
    mats = _mats_sources(_wgrad_tn(sv["a_gated"], dya, "wgrad_pa"), _wgrad_tn(sv["b_gated"], dyb, "wgrad_pb"),
                         _wgrad_tn(sv["mix"], dout, "wgrad_o"))
    (drest, acc_lru, gwr2, gwi2), got_up_in = _lru_bwd(
        dhl, sv["hl"], sv["a_dec"], sv["gates"], sv["rest"], p["cw"], p["wr2"], p["wi2"], drest,
        hosted=([upper["w_in"]], ["a2a"]) if upper else None)
    small = dict(
        conv_w=acc_lru[0:4], conv_b=acc_lru[4], b_rg=acc_lru[5], b_ig=acc_lru[6],
        lru_lambda=acc_lru[7] * (-jax.nn.sigmoid(-p["lru_lambda"])),
        w_rg=_unpair_blocks(gwr2), w_ig=_unpair_blocks(gwi2))
    hosts = [None, None, None]
    if host_own:
        lru_vec = _flat_rows([jnp.stack([small[n], upper["small"][n]]) for n in LRU_SMALL + ("conv_w",)])
        hosts = [(mats, a2a3), ([lru_vec], ["ag"]), None]
    dqkv, got_attn = [], []
    for g, r in enumerate(DILATIONS):
        d_g, got = _attn_bwd(sv["qkv"][g], do_dvec[g], sv["lse"][g], do_dvec[3 + g], g, r, hosted=hosts[g])
        dqkv.append(d_g)
        got_attn.append(got)
    parts = [_proj_wgrad_group(sv["h_t"][g], dqkv[g], r, f"proj_wgrad_g{g}") for g, r in enumerate(DILATIONS)]
    parts.append(_proj_wgrad_part(sv["h_t"][0], drest, 2, 4, "proj_wgrad_u"))
    parts.append(_proj_wgrad_part(sv["h_t"][0], drest, 7, 0, "proj_wgrad_rest"))
    gw_int = jnp.concatenate(parts, axis=1).astype(BF16).reshape(D_MODEL, N_CHUNKS, ATT_WIDTH)
    inv_perm = [CHUNK_PERM.index(k) for k in range(N_CHUNKS)]
    gw_in = jnp.transpose(gw_int[:, jnp.array(inv_perm)].reshape(D_MODEL, N_DEV, n_in), (1, 0, 2))
    (dx, acc_pre), got_in = _proj_dgrad_norm_bwd(
        dqkv, drest, p["w_int"], sv["x"], dxn, sv["vec_pre"],
        hosted=([gw_in], ["a2a"]) if host_own else None)
    small.update(dmod=jnp.concatenate([acc_pre[0], acc_pre[1], acc_out[0]]), g_pre=acc_pre[2],
                 g_post=acc_out[1])
    comm = dict(sources=dict(mats=mats, w_in=gw_in, small=small),
                own=dict(mats=got_attn[0], w_in=got_in, lru_small=got_attn[1]),
                upper=dict(mats=got_up_mats, w_in=got_up_in))
    return dx, small, comm


SMALL_NAMES = ("b_mod", "g_pre", "conv_b", "w_rg", "b_rg", "w_ig", "b_ig", "lru_lambda", "g_post")


def kernel(x, c, w_mod, b_mod, g_pre, w_in, conv_w, conv_b, w_rg, b_rg, w_ig, b_ig, lru_lambda, w_pa, w_pb, w_o, g_post, loss_target, m_w_mod, m_b_mod, m_g_pre, m_w_in, m_conv_w, m_conv_b, m_w_rg, m_b_rg, m_w_ig, m_b_ig, m_lru_lambda, m_w_pa, m_w_pb, m_w_o, m_g_post, v_w_mod, v_b_mod, v_g_pre, v_w_in, v_conv_w, v_conv_b, v_w_rg, v_b_rg, v_w_ig, v_b_ig, v_lru_lambda, v_w_pa, v_w_pb, v_w_o, v_g_post):
    W = dict(w_mod=w_mod, b_mod=b_mod, g_pre=g_pre, w_in=w_in, conv_w=conv_w, conv_b=conv_b, w_rg=w_rg,
             b_rg=b_rg, w_ig=w_ig, b_ig=b_ig, lru_lambda=lru_lambda, w_pa=w_pa, w_pb=w_pb, w_o=w_o,
             g_post=g_post)
    M = dict(w_mod=m_w_mod, b_mod=m_b_mod, g_pre=m_g_pre, w_in=m_w_in, conv_w=m_conv_w, conv_b=m_conv_b,
             w_rg=m_w_rg, b_rg=m_b_rg, w_ig=m_w_ig, b_ig=m_b_ig, lru_lambda=m_lru_lambda, w_pa=m_w_pa,
             w_pb=m_w_pb, w_o=m_w_o, g_post=m_g_post)
    V = dict(w_mod=v_w_mod, b_mod=v_b_mod, g_pre=v_g_pre, w_in=v_w_in, conv_w=v_conv_w, conv_b=v_conv_b,
             w_rg=v_w_rg, b_rg=v_b_rg, w_ig=v_w_ig, b_ig=v_b_ig, lru_lambda=v_lru_lambda, w_pa=v_w_pa,
             w_pb=v_w_pb, w_o=v_w_o, g_post=v_g_post)
    S = x.shape[1]
    me = 4 * lax.axis_index("x") + 2 * lax.axis_index("y") + lax.axis_index("c")
    n_mod = w_mod.shape[2]
    n_in = w_in.shape[2]
    n_conv = conv_w.shape[2]

    c_rows = jnp.broadcast_to(c, (8, D_MODEL))
    w_in_b, w_pa_b, w_pb_b, w_o_b = (t.astype(BF16) for t in (w_in, w_pa, w_pb, w_o))
    g_c, g_win0, g_wpa0, g_wpb0, g_wo0, g_cw = _gather_two_level(
        [c_rows, w_in_b[0], w_pa_b[0], w_pb_b[0], w_o_b[0], conv_w], "gather_weights")
    c_all = g_c[:, 0, :]
    c_pad = jnp.concatenate([c_all, jnp.zeros((8, D_MODEL), F32)], axis=0)
    conv_w_full = jnp.transpose(g_cw, (1, 2, 0, 3)).reshape(2, CONV_WIDTH, D_MODEL)

    mod_cols = _mod_fwd(c_pad, w_mod)
    mod_src = jnp.transpose(mod_cols[:, :8], (1, 0, 2))
    mod_src = jnp.concatenate([mod_src, jnp.zeros((8, 6, n_mod), F32)], axis=1)
    (mod_got,) = _exchange([mod_src], ["a2a"], "scatter_mod")
    mod = jnp.transpose(mod_got[:, :2], (1, 0, 2)).reshape(2, 3 * D_MODEL) + b_mod

    def layer_params(layer, w_in_g, w_pa_g, w_pb_g, w_o_g):
        return _layer_params(w_in_g, w_pa_g, w_pb_g, w_o_g, conv_w_full[layer], conv_b[layer], b_rg[layer],
                             b_ig[layer], lru_lambda[layer], w_rg[layer], w_ig[layer], g_pre[layer],
                             g_post[layer])

    layers = [layer_params(0, g_win0, g_wpa0, g_wpb0, g_wo0), None]
    act, sv0, (g_win1,), (g_wpa1, g_wpb1, g_wo1) = _layer_fwd(
        x[0], mod[0].reshape(3, D_MODEL), layers[0],
        host_proj=([w_in_b[1]], ["ag"]), host_lru=([w_pa_b[1], w_pb_b[1], w_o_b[1]], ["ag"] * 3))
    layers[1] = layer_params(1, g_win1, g_wpa1, g_wpb1, g_wo1)
    act, sv1, _, _ = _layer_fwd(act, mod[1].reshape(3, D_MODEL), layers[1])
    dy, loss_acc = _loss_head(act, loss_target[0])

    dy, small1, comm1 = _layer_bwd(dy, sv1, layers[1])
    dy, small0, comm0 = _layer_bwd(dy, sv0, layers[0], upper=comm1["sources"], host_own=True)
    grad_x = dy[None]
    r_pa, r_pb, r_o = ([comm0["own"]["mats"][k], comm0["upper"]["mats"][k]] for k in range(3))
    r_in = [comm0["own"]["w_in"][0], comm0["upper"]["w_in"][0]]
    grads = [small0, small1]

    def stack2(name):
        return jnp.stack([grads[0][name], grads[1][name]], axis=0)

    dmod = stack2("dmod")
    dmod_src = jnp.transpose(dmod.reshape(2, N_DEV, n_mod), (1, 0, 2))
    dmod_src = jnp.concatenate([dmod_src, jnp.zeros((8, 6, n_mod), F32)], axis=1)
    mod_vec = _flat_rows([dmod, stack2("g_pre"), stack2("g_post"), loss_acc])
    r_dmod, r_mod_small = _exchange([dmod_src, mod_vec], ["a2a", "ag"], "exchange_grads")
    mod_rows = mod_vec.shape[0] - loss_acc.shape[0]
    loss = 0.5 * jnp.sum(r_mod_small[:, mod_rows, 0]) / D_MODEL
    (r_lru_small,) = comm0["own"]["lru_small"]

    res = {}
    res["w_in"] = _adamw(r_in, w_in, m_w_in, v_w_in, 8, "adamw_w_in")
    res["w_pa"] = _adamw(r_pa, w_pa, m_w_pa, v_w_pa, 8, "adamw_w_pa")
    res["w_pb"] = _adamw(r_pb, w_pb, m_w_pb, v_w_pb, 8, "adamw_w_pb")
    res["w_o"] = _adamw(r_o, w_o, m_w_o, v_w_o, 8, "adamw_w_o")
    dmod_all = jnp.transpose(r_dmod[:, :2], (1, 0, 2))
    gw_mod = _mod_wgrad(jnp.transpose(c_all), dmod_all)
    res["w_mod"] = _adamw([gw_mod[0], gw_mod[1]], w_mod, m_w_mod, v_w_mod, 0, "adamw_w_mod")
    for names_rep, stack, tag in ((LRU_SMALL, r_lru_small, "lru"), (MOD_SMALL, r_mod_small, "mod")):
        rows = sum(W[n].size for n in names_rep) // 128
        rep = _adamw([stack], *(_flat_rows([src[n] for n in names_rep])[None] for src in (W, M, V)), 8,
                     f"adamw_small_{tag}", rows=rows // 2 if rows % 16 == 0 else rows)
        off = 0
        for name in names_rep:
            size = W[name].size
            res[name] = tuple(t.reshape(-1)[off:off + size].reshape(W[name].shape) for t in rep)
            off += size
    lru_rows = sum(W[n].size for n in LRU_SMALL) // 128
    conv_stack = r_lru_small[:, lru_rows:].reshape(8, 2, CONV_WIDTH, D_MODEL)
    conv_stack = lax.dynamic_slice_in_dim(conv_stack, me * n_conv, n_conv, axis=3).reshape(8, 8, n_conv)
    res["conv_w"] = _adamw([conv_stack], conv_w.reshape(1, 8, n_conv), m_conv_w.reshape(1, 8, n_conv),
                           v_conv_w.reshape(1, 8, n_conv), 8, "adamw_conv_w")

    names = ("w_mod", "b_mod", "g_pre", "w_in", "conv_w", "conv_b", "w_rg", "b_rg", "w_ig", "b_ig",
             "lru_lambda", "w_pa", "w_pb", "w_o", "g_post")
    outs = [loss, grad_x]
    for k in range(4):
        outs.extend(res[n][k].reshape(W[n].shape) for n in names)
    return tuple(outs)
```

```python
import functools

import jax
import jax.numpy as jnp
import numpy as np
from jax import lax
from jax.experimental import pallas as pl
from jax.experimental.pallas import tpu as pltpu

F32 = jnp.float32
BF16 = jnp.bfloat16

N_DEV = 8
D_MODEL = 1024
HEAD_DIM = 128
HEADS = 4
ATT_WIDTH = HEADS * HEAD_DIM
DILATIONS = (1, 4, 16)
BAND = 128
N_CHUNKS = 18
QKV_CHUNKS = 9
CONV_WIDTH = 4
LRU_C = 8.0
NORM_EPS = 1e-6
NEG_INF = -1e30
ADAM_LR = 0.001
ADAM_B1 = 0.9
ADAM_B2 = 0.999
ADAM_EPS = 1e-08
ADAM_WD = 0.01
ADAM_STEP = 10
CHUNK_PERM = (0, 3, 6, 1, 4, 7, 2, 5, 8, 10, 11, 12, 13, 14, 15, 16, 17, 9)
DREST_CHUNKS = 10
VMEM_LIMIT = 56 * 1024 * 1024
MESH_AXES = ("x", "y", "c")


def _params(sem=None):
    return pltpu.CompilerParams(dimension_semantics=sem, vmem_limit_bytes=VMEM_LIMIT)


def _silu(x):
    return x * jax.nn.sigmoid(x)


def _dsilu(x):
    s = jax.nn.sigmoid(x)
    return s * (1.0 + x * (1.0 - s))


def _neg_expm1(x):
    series = -x * (1.0 + x * (0.5 + x * (1.0 / 6.0 + x * (1.0 / 24.0 + x * (1.0 / 120.0)))))
    return jnp.where(x > -0.05, series, 1.0 - jnp.exp(x))


def _softplus_neg(lam):
    z = jnp.exp(-jnp.abs(lam))
    small = z * (1.0 - z * (0.5 - z * (1.0 / 3.0 - z * 0.25)))
    log1p_z = jnp.where(z < 1e-2, small, jnp.log(1.0 + z))
    return jnp.maximum(-lam, 0.0) + log1p_z


def _dot(a, b):
    return jnp.dot(a, b, preferred_element_type=F32)


def _dot_nt(a, b):
    return lax.dot_general(a, b, (((1,), (1,)), ((), ())), preferred_element_type=F32)


def _dot_tn(a, b):
    return lax.dot_general(a, b, (((0,), (0,)), ((), ())), preferred_element_type=F32)


def _perm_matrix(r, rows, transpose=False):
    n = rows // r
    p = np.zeros((rows, rows), np.float32)
    dst = np.arange(rows)
    p[dst, (dst % n) * r + dst // n] = 1.0
    return jnp.asarray(p.T if transpose else p, dtype=BF16)


def _permute_f32(p, x, pieces):
    part = x.astype(BF16)
    acc = _dot(p, part)
    for _ in range(pieces - 1):
        x = x - part.astype(F32)
        part = x.astype(BF16)
        acc = acc + _dot(p, part)
    return acc


def _rows_of_residues(ref, lead=()):
    r = ref.shape[len(lead)]
    if r == 1:
        return ref[lead + (0,)]
    return jnp.concatenate([ref[lead + (rho,)] for rho in range(r)], axis=0)


def _store_residues(ref, x, lead=()):
    r = ref.shape[len(lead)]
    n = x.shape[0] // r
    for rho in range(r):
        ref[lead + (rho,)] = x[rho * n:(rho + 1) * n]


ANY_SPEC = pl.BlockSpec(memory_space=pl.ANY)


def _exchange_shapes(arrs, modes):
    return [jax.ShapeDtypeStruct(((N_DEV,) + a.shape) if mode == "ag" else a.shape, a.dtype)
            for a, mode in zip(arrs, modes)]


def _exchange_sems(n):
    return [pltpu.SemaphoreType.DMA((7 * n,)), pltpu.SemaphoreType.DMA((7 * n,)),
            pltpu.SemaphoreType.DMA((n,))]


def _exchange_copies(ins, outs, modes, sems):
    n = len(ins)
    send_sems, recv_sems, local_sems = sems
    x, y, c = lax.axis_index("x"), lax.axis_index("y"), lax.axis_index("c")
    me = 4 * x + 2 * y + c

    def src_for(a, dev):
        return ins[a] if modes[a] == "ag" else ins[a].at[dev]

    local = [pltpu.make_async_copy(src_for(a, me), outs[a].at[me], local_sems.at[a]) for a in range(n)]
    sends, arrivals = [], []
    for k in range(1, N_DEV):
        px = 1 - x if (k >> 2) & 1 else x
        py = 1 - y if (k >> 1) & 1 else y
        pc = 1 - c if k & 1 else c
        peer = 4 * px + 2 * py + pc
        for a in range(n):
            s = (k - 1) * n + a
            for dst, group in ((me, sends), (peer, arrivals)):
                group.append(pltpu.make_async_remote_copy(
                    src_ref=src_for(a, peer), dst_ref=outs[a].at[dst],
                    send_sem=send_sems.at[s], recv_sem=recv_sems.at[s],
                    device_id=(px, py, pc), device_id_type=pl.DeviceIdType.MESH))
    return local, sends, arrivals


def _exchange_start(copies):
    local, sends, _ = copies
    for cp in local + sends:
        cp.start()


def _exchange_wait(copies):
    local, sends, arrivals = copies
    for cp in arrivals:
        cp.wait_recv()
    for cp in sends:
        cp.wait_send()
    for cp in local:
        cp.wait()


def _exchange(arrs, modes, name):
    n = len(arrs)

    def body(*refs):
        copies = _exchange_copies(refs[:n], refs[n:2 * n], modes, refs[2 * n:])
        _exchange_start(copies)
        _exchange_wait(copies)

    outs = pl.pallas_call(
        body, name=name, out_shape=tuple(_exchange_shapes(arrs, modes)),
        in_specs=[ANY_SPEC] * n, out_specs=tuple([ANY_SPEC] * n),
        scratch_shapes=_exchange_sems(n),
    )(*arrs)
    return list(outs)


def _gather_two_level(arrs, name):
    n = len(arrs)

    def body(*refs):
        ins, outs = refs[:n], refs[n:2 * n]
        send_sems, recv_sems, local_sems = refs[2 * n:]
        x, y, c = lax.axis_index("x"), lax.axis_index("y"), lax.axis_index("c")
        me, sibling = (x, y, c), (x, y, 1 - c)
        chips = [(1 - x, y), (x, 1 - y), (1 - x, 1 - y)]

        def copy(a, k, block, to, src=None):
            slot = 4 * block[0] + 2 * block[1] + block[2]
            return pltpu.make_async_remote_copy(
                src_ref=outs[a].at[slot] if src is None else src, dst_ref=outs[a].at[slot],
                send_sem=send_sems.at[7 * a + k], recv_sem=recv_sems.at[7 * a + k],
                device_id=to, device_id_type=pl.DeviceIdType.MESH)

        mine = [pltpu.make_async_copy(ins[a], outs[a].at[4 * x + 2 * y + c], local_sems.at[a])
                for a in range(n)]
        first = []
        for a in range(n):
            first.append(copy(a, 0, me, sibling, src=ins[a]))
            first += [copy(a, 1 + j, me, (*chip, c), src=ins[a]) for j, chip in enumerate(chips)]
        for cp in mine + first:
            cp.start()
        passed = []
        for j, chip in enumerate(chips):
            for a in range(n):
                copy(a, 1 + j, (*chip, c), me).wait_recv()
                passed.append(copy(a, 4 + j, (*chip, c), sibling))
                passed[-1].start()
        for a in range(n):
            copy(a, 0, sibling, me).wait_recv()
            for j, chip in enumerate(chips):
                copy(a, 4 + j, (*chip, 1 - c), me).wait_recv()
        for cp in first + passed:
            cp.wait_send()
        for cp in mine:
            cp.wait()

    outs = pl.pallas_call(
        body, name=name, out_shape=tuple(_exchange_shapes(arrs, ["ag"] * n)),
        in_specs=[ANY_SPEC] * n, out_specs=tuple([ANY_SPEC] * n),
        scratch_shapes=_exchange_sems(n),
    )(*arrs)
    return list(outs)


def _dma_relayout(srcs, moves, out_shapes, name):
    n_src, n_dst = len(srcs), len(out_shapes)

    def body(*refs):
        ins, outs, sems = refs[:n_src], refs[n_src:n_src + n_dst], refs[n_src + n_dst]
        copies = [pltpu.make_async_copy(ins[s].at[s_idx], outs[d].at[d_idx], sems.at[k])
                  for k, (s, s_idx, d, d_idx) in enumerate(moves)]
        for cp in copies:
            cp.start()
        for cp in copies:
            cp.wait()

    outs = pl.pallas_call(
        body, name=name, out_shape=tuple(out_shapes), in_specs=[ANY_SPEC] * n_src,
        out_specs=tuple([ANY_SPEC] * n_dst), scratch_shapes=[pltpu.SemaphoreType.DMA((len(moves),))],
    )(*srcs)
    return list(outs)


SHARD_COLS = N_CHUNKS * ATT_WIDTH // N_DEV


def _shard_pieces(orig_chunk):
    col, end, pieces = ATT_WIDTH * orig_chunk, ATT_WIDTH * (orig_chunk + 1), []
    while col < end:
        dev = col // SHARD_COLS
        upto = min(end, (dev + 1) * SHARD_COLS)
        pieces.append((dev, col - dev * SHARD_COLS, col - ATT_WIDTH * orig_chunk, upto - col))
        col = upto
    return pieces


def _weights_to_internal(shards, name):
    moves, row0 = [], 0
    for s, shard in enumerate(shards):
        rows = pl.ds(row0, shard.shape[1])
        for k, orig in enumerate(CHUNK_PERM):
            for dev, src_col, in_chunk, width in _shard_pieces(orig):
                moves.append((s, (dev, slice(None), pl.ds(src_col, width)),
                              0, (rows, pl.ds(ATT_WIDTH * k + in_chunk, width))))
        row0 += shard.shape[1]
    (w_int,) = _dma_relayout(shards, moves, [jax.ShapeDtypeStruct((D_MODEL, N_CHUNKS * ATT_WIDTH), BF16)], name)
    return w_int


def _grads_to_shards(parts, name):
    half = D_MODEL // 2
    offsets = np.cumsum([0] + [p.shape[1] for p in parts])
    moves = []
    for k, orig in enumerate(CHUNK_PERM):
        part = int(np.searchsorted(offsets, ATT_WIDTH * k, side="right")) - 1
        part_col = ATT_WIDTH * k - int(offsets[part])
        for dev, dst_col, in_chunk, width in _shard_pieces(orig):
            for h in range(2):
                moves.append((part, (pl.ds(h * half, half), pl.ds(part_col + in_chunk, width)),
                              h, (dev, slice(None), pl.ds(dst_col, width))))
    shape = jax.ShapeDtypeStruct((N_DEV, half, SHARD_COLS), BF16)
    return _dma_relayout(parts, moves, [shape, shape], name)


def _call(body, args, *, name, grid, in_specs, out_specs, out_shape, scratch_shapes=(), aliases=None,
          hosted=None):
    sem = ("arbitrary",) * len(grid)
    if hosted is None:
        outs = pl.pallas_call(
            body, name=name, grid=grid, in_specs=in_specs, out_specs=tuple(out_specs),
            out_shape=tuple(out_shape), scratch_shapes=list(scratch_shapes),
            input_output_aliases=aliases or {}, compiler_params=_params(sem))(*args)
        return list(outs), []
    x_arrs, modes = hosted
    n_in, n_out, n_scr, nx = len(args), len(out_shape), len(scratch_shapes), len(x_arrs)

    def wrapped(*refs):
        ins, x_ins = refs[:n_in], refs[n_in:n_in + nx]
        outs = refs[n_in + nx:n_in + nx + n_out]
        x_outs = refs[n_in + nx + n_out:n_in + 2 * nx + n_out]
        scr = refs[n_in + 2 * nx + n_out:n_in + 2 * nx + n_out + n_scr]
        sems = refs[n_in + 2 * nx + n_out + n_scr:]
        first = pl.program_id(0) == 0
        last = pl.program_id(0) == grid[0] - 1
        for axis in range(1, len(grid)):
            first = jnp.logical_and(first, pl.program_id(axis) == 0)
            last = jnp.logical_and(last, pl.program_id(axis) == grid[axis] - 1)

        @pl.when(first)
        def _():
            _exchange_start(_exchange_copies(x_ins, x_outs, modes, sems))

        body(*ins, *outs, *scr)

        @pl.when(last)
        def _():
            _exchange_wait(_exchange_copies(x_ins, x_outs, modes, sems))

    outs = pl.pallas_call(
        wrapped, name=name, grid=grid, in_specs=list(in_specs) + [ANY_SPEC] * nx,
        out_specs=tuple(out_specs) + tuple([ANY_SPEC] * nx),
        out_shape=tuple(out_shape) + tuple(_exchange_shapes(x_arrs, modes)),
        scratch_shapes=list(scratch_shapes) + _exchange_sems(nx),
        input_output_aliases=aliases or {}, compiler_params=_params(sem))(*args, *x_arrs)
    return list(outs[:n_out]), list(outs[n_out:])


def _mod_fwd(c_pad, w_mod):
    def body(c_ref, w_ref, o_ref):
        sc = _silu(c_ref[...]).astype(BF16)
        for layer in range(2):
            o_ref[layer] = _dot(sc, w_ref[layer].astype(BF16))

    return pl.pallas_call(
        body, name="mod_fwd", out_shape=jax.ShapeDtypeStruct((2, 16, w_mod.shape[2]), F32),
        compiler_params=_params())(c_pad, w_mod)


def _mod_wgrad(c_t, dmod):
    n_cols = dmod.shape[2]

    def body(c_ref, d_ref, o_ref):
        sc = _silu(c_ref[...]).astype(BF16).astype(F32)
        for layer in range(2):
            dm = d_ref[layer].astype(BF16).astype(F32)
            acc = sc[:, 0:1] * dm[0:1, :]
            for b in range(1, N_DEV):
                acc = acc + sc[:, b:b + 1] * dm[b:b + 1, :]
            o_ref[layer] = acc

    return pl.pallas_call(
        body, name="mod_wgrad", out_shape=jax.ShapeDtypeStruct((2, D_MODEL, n_cols), F32),
        compiler_params=_params())(c_t, dmod)


GROUP_COLS = 3 * ATT_WIDTH
PROJ_TM = 256


def _norm_proj_fwd(x, vec, w_int, hosted=None):
    S = x.shape[0]
    tm = PROJ_TM
    perms = [_perm_matrix(r, tm) for r in DILATIONS[1:]]

    def body(x_ref, v_ref, w_ref, p1_ref, p2_ref, ht0_ref, ht1_ref, ht2_ref, q0_ref, q1_ref, q2_ref,
             rest_ref):
        xv = x_ref[...]
        rstd = lax.rsqrt(jnp.mean(xv * xv, axis=-1, keepdims=True) + NORM_EPS)
        hf = ((xv * rstd) * v_ref[0:1, :]) * (1.0 + v_ref[1:2, :]) + v_ref[2:3, :]
        h = hf.astype(BF16)
        for g, (q_ref, ht_ref, p_ref) in enumerate(((q0_ref, ht0_ref, None), (q1_ref, ht1_ref, p1_ref),
                                                    (q2_ref, ht2_ref, p2_ref))):
            rows_f = hf if p_ref is None else _dot(p_ref[...], h)
            ht_ref[...] = rows_f.T.astype(BF16)
            _store_residues(q_ref, _dot(rows_f.astype(BF16),
                                        w_ref[:, g * GROUP_COLS:(g + 1) * GROUP_COLS]).astype(BF16))
        rest_ref[...] = _dot(h, w_ref[:, 3 * GROUP_COLS:])

    n_rest = QKV_CHUNKS * ATT_WIDTH
    whole = pl.BlockSpec(memory_space=pltpu.VMEM)
    ht_spec = pl.BlockSpec((D_MODEL, tm), lambda i: (0, i))
    ht_shape = jax.ShapeDtypeStruct((D_MODEL, S), BF16)
    return _call(
        body, (x, vec, w_int, *perms), name="norm_proj_fwd", grid=(S // tm,),
        in_specs=[pl.BlockSpec((tm, D_MODEL), lambda i: (i, 0)),
                  pl.BlockSpec((8, D_MODEL), lambda i: (0, 0)), whole, whole, whole],
        out_specs=(ht_spec, ht_spec, ht_spec,
                   *[pl.BlockSpec((r, tm // r, GROUP_COLS), lambda i: (0, i, 0)) for r in DILATIONS],
                   pl.BlockSpec((tm, n_rest), lambda i: (i, 0))),
        out_shape=(ht_shape, ht_shape, ht_shape,
                   *[jax.ShapeDtypeStruct((r, S // r, GROUP_COLS), BF16) for r in DILATIONS],
                   jax.ShapeDtypeStruct((S, n_rest), F32)),
        hosted=hosted)


def _proj_dgrad_norm_bwd(dqkv, drest, w_int, x, dxn, vec, hosted=None):
    S = x.shape[0]
    tm = PROJ_TM
    perms = [_perm_matrix(r, tm, transpose=True) for r in DILATIONS[1:]]

    def body(d0_ref, d1_ref, d2_ref, dr_ref, w_ref, p1_ref, p2_ref, x_ref, dxn_ref, v_ref, dx_ref, acc_ref):
        i = pl.program_id(0)

        @pl.when(i == 0)
        def _():
            acc_ref[...] = jnp.zeros_like(acc_ref)

        def wcols(k):
            return w_ref[:, k * ATT_WIDTH:(k + 1) * ATT_WIDTH]

        dhv = None
        for g, (d_ref, p_ref) in enumerate(((d0_ref, None), (d1_ref, p1_ref), (d2_ref, p2_ref))):
            for t in range(3):
                d = _rows_of_residues(d_ref, (t,))
                if p_ref is not None:
                    d = _dot(p_ref[...], d).astype(BF16)
                part = _dot_nt(d, wcols(3 * g + t))
                dhv = part if dhv is None else dhv + part
        for t in range(7):
            dhv = dhv + _dot_nt(dr_ref[t], wcols(11 + t))
        for t in range(2):
            dhv = dhv + _dot_nt(dr_ref[8 + t], wcols(9 + t))

        xv = x_ref[...]
        g = v_ref[0:1, :]
        sc1 = 1.0 + v_ref[1:2, :]
        rstd = lax.rsqrt(jnp.mean(xv * xv, axis=-1, keepdims=True) + NORM_EPS)
        xhat = xv * rstd
        acc_ref[0:1, :] += jnp.sum(dhv, axis=0, keepdims=True)
        acc_ref[1:2, :] += jnp.sum(dhv * (xhat * g), axis=0, keepdims=True)
        acc_ref[2:3, :] += jnp.sum(dhv * xhat * sc1, axis=0, keepdims=True)
        dxhat = dhv * (g * sc1)
        dx = rstd * (dxhat - xhat * jnp.mean(dxhat * xhat, axis=-1, keepdims=True))
        dx_ref[...] = dx + dxn_ref[...]

    row = pl.BlockSpec((tm, D_MODEL), lambda i: (i, 0))
    vec_spec = pl.BlockSpec((8, D_MODEL), lambda i: (0, 0))
    whole = pl.BlockSpec(memory_space=pltpu.VMEM)
    return _call(
        body, (*dqkv, drest, w_int, *perms, x, dxn, vec), name="proj_dgrad_norm_bwd", grid=(S // tm,),
        in_specs=[*[pl.BlockSpec((3, r, tm // r, ATT_WIDTH), lambda i: (0, 0, i, 0)) for r in DILATIONS],
                  pl.BlockSpec((DREST_CHUNKS, tm, ATT_WIDTH), lambda i: (0, i, 0)),
                  whole, whole, whole, row, row, vec_spec],
        out_specs=(row, vec_spec),
        out_shape=(jax.ShapeDtypeStruct((S, D_MODEL), F32), jax.ShapeDtypeStruct((8, D_MODEL), F32)),
        hosted=hosted)


def _proj_wgrad_part(h_t, d, n_chunks, chunk_block, name):
    S = h_t.shape[1]
    ts = 512
    n_steps = S // ts

    def body(h_ref, d_ref, o_ref, acc_ref):
        i = pl.program_id(0)

        @pl.when(i == 0)
        def _():
            acc_ref[...] = jnp.zeros_like(acc_ref)

        ht = h_ref[...]
        for t in range(n_chunks):
            acc_ref[:, t * ATT_WIDTH:(t + 1) * ATT_WIDTH] += _dot(ht, d_ref[t])

        @pl.when(i == n_steps - 1)
        def _():
            o_ref[...] = acc_ref[...].astype(BF16)

    return pl.pallas_call(
        body, name=name, grid=(n_steps,),
        in_specs=[pl.BlockSpec((D_MODEL, ts), lambda i: (0, i)),
                  pl.BlockSpec((n_chunks, ts, ATT_WIDTH), lambda i: (chunk_block, i, 0))],
        out_specs=pl.BlockSpec((D_MODEL, n_chunks * ATT_WIDTH), lambda i: (0, 0)),
        out_shape=jax.ShapeDtypeStruct((D_MODEL, n_chunks * ATT_WIDTH), BF16),
        scratch_shapes=[pltpu.VMEM((D_MODEL, n_chunks * ATT_WIDTH), F32)],
        compiler_params=_params(("arbitrary",)))(h_t, d)


def _proj_wgrad_group(h_t, d, r, name):
    S = h_t.shape[1]
    ts = 512
    n_steps = S // ts
    n = PROJ_TM // r

    def body(h_ref, d_ref, o_ref, acc_ref):
        i = pl.program_id(0)

        @pl.when(i == 0)
        def _():
            acc_ref[...] = jnp.zeros_like(acc_ref)

        ht = h_ref[...]
        for t in range(3):
            rows = jnp.concatenate([d_ref[t, rho, s * n:(s + 1) * n] for s in range(ts // PROJ_TM)
                                    for rho in range(r)], axis=0) if r > 1 else d_ref[t, 0]
            acc_ref[:, t * ATT_WIDTH:(t + 1) * ATT_WIDTH] += _dot(ht, rows)

        @pl.when(i == n_steps - 1)
        def _():
            o_ref[...] = acc_ref[...].astype(BF16)

    return pl.pallas_call(
        body, name=name, grid=(n_steps,),
        in_specs=[pl.BlockSpec((D_MODEL, ts), lambda i: (0, i)),
                  pl.BlockSpec((3, r, ts // r, ATT_WIDTH), lambda i: (0, 0, i, 0))],
        out_specs=pl.BlockSpec((D_MODEL, GROUP_COLS), lambda i: (0, 0)),
        out_shape=jax.ShapeDtypeStruct((D_MODEL, GROUP_COLS), BF16),
        scratch_shapes=[pltpu.VMEM((D_MODEL, GROUP_COLS), F32)],
        compiler_params=_params(("arbitrary",)))(h_t, d)


def _wgrad_tn(a, b, name):
    S, M = a.shape
    N = b.shape[1]
    ts = 1024

    def body(a_ref, b_ref, o_ref):
        i = pl.program_id(0)
        part = _dot_tn(a_ref[...], b_ref[...])

        @pl.when(i == 0)
        def _():
            o_ref[...] = part

        @pl.when(i > 0)
        def _():
            o_ref[...] += part

    return pl.pallas_call(
        body, name=name, grid=(S // ts,),
        in_specs=[pl.BlockSpec((ts, M), lambda i: (i, 0)), pl.BlockSpec((ts, N), lambda i: (i, 0))],
        out_specs=pl.BlockSpec((M, N), lambda i: (0, 0)),
        out_shape=jax.ShapeDtypeStruct((M, N), F32),
        compiler_params=_params(("arbitrary",)))(a, b)


def _attn_fwd(qkv, g, r):
    L = qkv.shape[1]
    nb2 = L // (2 * BAND)
    scale = HEAD_DIM ** -0.5

    def body(q_ref, kp_ref, kc_ref, vp_ref, vc_ref, o_ref, st_ref):
        m_step = pl.program_id(1)
        ii = lax.broadcasted_iota(jnp.int32, (2 * BAND, 3 * BAND), 0)
        kk = lax.broadcasted_iota(jnp.int32, (2 * BAND, 3 * BAND), 1)
        mask = jnp.logical_and(jnp.logical_and(kk >= ii, kk <= ii + BAND),
                               jnp.logical_or(kk >= BAND, m_step > 0))
        lane = lax.broadcasted_iota(jnp.int32, (2 * BAND, 128), 1)
        stats = jnp.zeros((2 * BAND, 128), F32)
        for h in range(HEADS):
            sl = slice(h * HEAD_DIM, (h + 1) * HEAD_DIM)
            k_cat = jnp.concatenate([kp_ref[:, sl], kc_ref[:, sl]], axis=0)
            v_cat = jnp.concatenate([vp_ref[:, sl], vc_ref[:, sl]], axis=0)
            s = jnp.where(mask, _dot_nt(q_ref[:, sl], k_cat) * scale, NEG_INF)
            m = jnp.max(s, axis=1, keepdims=True)
            p = jnp.exp(s - m)
            l = jnp.sum(p, axis=1, keepdims=True)
            o_ref[:, sl] = _dot(p.astype(BF16), v_cat) / l
            stats = jnp.where(lane == h, m, stats)
            stats = jnp.where(lane == HEADS + h, l, stats)
        st_ref[...] = stats

    two = (None, 2 * BAND, ATT_WIDTH)
    one = (None, BAND, ATT_WIDTH)

    def prev(m):
        return jnp.maximum(2 * m - 1, 0)

    return pl.pallas_call(
        body, name=f"attn_fwd_g{g}", grid=(r, nb2),
        in_specs=[
            pl.BlockSpec(two, lambda rho, m: (rho, m, 0)),
            pl.BlockSpec(one, lambda rho, m: (rho, prev(m), 1)),
            pl.BlockSpec(two, lambda rho, m: (rho, m, 1)),
            pl.BlockSpec(one, lambda rho, m: (rho, prev(m), 2)),
            pl.BlockSpec(two, lambda rho, m: (rho, m, 2)),
        ],
        out_specs=(pl.BlockSpec(two, lambda rho, m: (rho, m, 0)),
                   pl.BlockSpec((None, 2 * BAND, 128), lambda rho, m: (rho, m, 0))),
        out_shape=(jax.ShapeDtypeStruct((r, L, ATT_WIDTH), F32),
                   jax.ShapeDtypeStruct((r, L, 128), F32)),
        compiler_params=_params(("parallel", "parallel")))(qkv, qkv, qkv, qkv, qkv)


def _attn_combine(outs, stats, rest):
    S = rest.shape[0]
    tm = 512
    gatt_blk = 8
    back = [_perm_matrix(r, tm, transpose=True) for r in DILATIONS[1:]]
    forth = [_perm_matrix(r, tm) for r in DILATIONS[1:]]

    def body(o0_ref, o1_ref, o2_ref, s0_ref, s1_ref, s2_ref, g_ref, b1_ref, b2_ref, f1_ref, f2_ref,
             o_ref, a_ref, l0_ref, l1_ref, l2_ref):
        outs_nat = [o0_ref[0]] + [_permute_f32(b_ref[...], _rows_of_residues(o_g), 2)
                                  for o_g, b_ref in ((o1_ref, b1_ref), (o2_ref, b2_ref))]
        st = [s0_ref[0]] + [_permute_f32(b_ref[...], _rows_of_residues(s_g), 3)
                            for s_g, b_ref in ((s1_ref, b1_ref), (s2_ref, b2_ref))]
        lane = lax.broadcasted_iota(jnp.int32, (tm, 128), 1)
        lse_out = jnp.zeros((tm, 128), F32)
        for h in range(HEADS):
            sl = slice(h * HEAD_DIM, (h + 1) * HEAD_DIM)
            ms = [s[:, h:h + 1] for s in st]
            ls = [s[:, HEADS + h:HEADS + h + 1] for s in st]
            m_all = jnp.maximum(jnp.maximum(ms[0], ms[1]), ms[2])
            ws = [l * jnp.exp(m - m_all) for m, l in zip(ms, ls)]
            den = ws[0] + ws[1] + ws[2]
            o = (ws[0] * outs_nat[0][:, sl] + ws[1] * outs_nat[1][:, sl] + ws[2] * outs_nat[2][:, sl]) / den
            o_ref[:, sl] = o
            a_ref[:, sl] = (o * _silu(g_ref[:, sl])).astype(BF16)
            lse_out = jnp.where(lane == h, m_all + jnp.log(den), lse_out)
        l0_ref[0] = lse_out
        for l_ref, f_ref in ((l1_ref, f1_ref), (l2_ref, f2_ref)):
            _store_residues(l_ref, _permute_f32(f_ref[...], lse_out, 3))

    o_spec = pl.BlockSpec((tm, ATT_WIDTH), lambda i: (i, 0))
    whole = pl.BlockSpec(memory_space=pltpu.VMEM)

    def res_spec(r, width):
        return pl.BlockSpec((r, tm // r, width), lambda i: (0, i, 0))

    return pl.pallas_call(
        body, name="attn_combine", grid=(S // tm,),
        in_specs=[*[res_spec(r, ATT_WIDTH) for r in DILATIONS], *[res_spec(r, 128) for r in DILATIONS],
                  pl.BlockSpec((tm, ATT_WIDTH), lambda i: (i, gatt_blk)), whole, whole, whole, whole],
        out_specs=(o_spec, o_spec, *[res_spec(r, 128) for r in DILATIONS]),
        out_shape=(jax.ShapeDtypeStruct((S, ATT_WIDTH), F32), jax.ShapeDtypeStruct((S, ATT_WIDTH), BF16),
                   *[jax.ShapeDtypeStruct((r, S // r, 128), F32) for r in DILATIONS]),
        compiler_params=_params(("parallel",)))(*outs, *stats, rest, *back, *forth)


def _attn_bwd(qkv, do, lse, dvec, g, r, hosted=None):
    L = qkv.shape[1]
    nb = L // BAND
    nb2 = nb // 2
    scale = HEAD_DIM ** -0.5

    def body(qc_ref, qn_ref, k_ref, v_ref, doc_ref, don_ref, lc_ref, ln_ref, dc_ref, dn_ref,
             out_ref, carry_ref):
        j = pl.program_id(1)

        @pl.when(j == 0)
        def _():
            carry_ref[...] = jnp.zeros_like(carry_ref)

        ii = lax.broadcasted_iota(jnp.int32, (3 * BAND, 2 * BAND), 0)
        kk = lax.broadcasted_iota(jnp.int32, (3 * BAND, 2 * BAND), 1)
        mask = jnp.logical_and(jnp.logical_and(kk <= ii, kk >= ii - BAND),
                               jnp.logical_or(ii < 2 * BAND, j < nb2 - 1))
        lse3 = jnp.concatenate([lc_ref[...], ln_ref[...]], axis=0)
        dvec3 = jnp.concatenate([dc_ref[...], dn_ref[...]], axis=0)
        for h in range(HEADS):
            sl = slice(h * HEAD_DIM, (h + 1) * HEAD_DIM)
            k = k_ref[:, sl]
            v = v_ref[:, sl]
            q = jnp.concatenate([qc_ref[:, sl], qn_ref[:, sl]], axis=0)
            do = jnp.concatenate([doc_ref[:, sl], don_ref[:, sl]], axis=0)
            p = jnp.where(mask, jnp.exp(_dot_nt(q, k) * scale - lse3[:, h:h + 1]), 0.0)
            ds = (p * (_dot_nt(do, v) - dvec3[:, h:h + 1])).astype(BF16)
            dq3 = _dot(ds, k) * scale
            out_ref[0, 0:BAND, sl] = (carry_ref[:, sl] + dq3[:BAND]).astype(BF16)
            out_ref[0, BAND:, sl] = dq3[BAND:2 * BAND].astype(BF16)
            out_ref[1, :, sl] = (_dot_tn(ds, q) * scale).astype(BF16)
            out_ref[2, :, sl] = _dot_tn(p.astype(BF16), do).astype(BF16)
            carry_ref[:, sl] = dq3[2 * BAND:]

    two = (None, 2 * BAND, ATT_WIDTH)
    one = (None, BAND, ATT_WIDTH)
    stwo = (None, 2 * BAND, 128)
    sone = (None, BAND, 128)

    def nxt(j):
        return jnp.minimum(2 * j + 2, nb - 1)

    (out,), got = _call(
        body, (qkv, qkv, qkv, qkv, do, do, lse, lse, dvec, dvec), name=f"attn_bwd_g{g}", grid=(r, nb2),
        in_specs=[
            pl.BlockSpec(two, lambda rho, j: (rho, j, 0)),
            pl.BlockSpec(one, lambda rho, j: (rho, nxt(j), 0)),
            pl.BlockSpec(two, lambda rho, j: (rho, j, 1)),
            pl.BlockSpec(two, lambda rho, j: (rho, j, 2)),
            pl.BlockSpec(two, lambda rho, j: (rho, j, 0)),
            pl.BlockSpec(one, lambda rho, j: (rho, nxt(j), 0)),
            pl.BlockSpec(stwo, lambda rho, j: (rho, j, 0)),
            pl.BlockSpec(sone, lambda rho, j: (rho, nxt(j), 0)),
            pl.BlockSpec(stwo, lambda rho, j: (rho, j, 0)),
            pl.BlockSpec(sone, lambda rho, j: (rho, nxt(j), 0)),
        ],
        out_specs=(pl.BlockSpec((3, None, 2 * BAND, ATT_WIDTH), lambda rho, j: (0, rho, j, 0)),),
        out_shape=(jax.ShapeDtypeStruct((3, r, L, ATT_WIDTH), BF16),),
        scratch_shapes=[pltpu.VMEM((BAND, ATT_WIDTH), F32)],
        hosted=hosted)
    return out, got


LRU_T = 256


def _linear_scan(a, b, carry, reverse):
    T = a.shape[0]
    row8 = lax.broadcasted_iota(jnp.int32, a.shape, 0) & 7
    for s in (1, 2, 4):
        keep = (row8 < 8 - s) if reverse else (row8 >= s)
        shift = T - s if reverse else s
        b_sh = jnp.where(keep, pltpu.roll(b, shift, 0), 0.0)
        a_sh = jnp.where(keep, pltpu.roll(a, shift, 0), 1.0)
        b = a * b_sh + b
        a = a * a_sh
    tiles = [None] * (T // 8)
    order = range(T // 8 - 1, -1, -1) if reverse else range(T // 8)
    for k in order:
        y = b[8 * k:8 * k + 8] + a[8 * k:8 * k + 8] * carry
        tiles[k] = y
        carry = y[0:1] if reverse else y[7:8]
    return jnp.concatenate(tiles, axis=0), carry


def _gate_matmuls(ucb, w_ref, bias):
    parts = [_dot(ucb[:, j * 128:(j + 1) * 128], w_ref[j]) for j in range(8)]
    return jnp.concatenate(parts, axis=1) + bias


def _conv_fwd(u, u_prev8, cw_ref):
    T = u.shape[0]
    ue = jnp.concatenate([u_prev8, u], axis=0)
    uc = cw_ref[4:5, :] + cw_ref[0:1, :] * u
    for j in range(1, CONV_WIDTH):
        uc = uc + cw_ref[j:j + 1, :] * pltpu.roll(ue, j, 0)[8:8 + T]
    return uc


def _lru_fwd(rest, cw, wr2, wi2, hosted=None):
    S = rest.shape[0]
    T = LRU_T

    def body(u_ref, cw_ref, wr_ref, wi_ref, h_ref, a_ref, uc_ref, r_ref, ig_ref, mult_ref,
             ucar_ref, hcar_ref):
        c = pl.program_id(0)

        @pl.when(c == 0)
        def _():
            ucar_ref[...] = jnp.zeros_like(ucar_ref)
            hcar_ref[...] = jnp.zeros_like(hcar_ref)

        u = u_ref[...]
        uc = _conv_fwd(u, ucar_ref[...], cw_ref)
        ucar_ref[...] = u[T - 8:, :]
        ucb = uc.astype(BF16)
        r = jax.nn.sigmoid(_gate_matmuls(ucb, wr_ref, cw_ref[5:6, :]))
        ig = jax.nn.sigmoid(_gate_matmuls(ucb, wi_ref, cw_ref[6:7, :]))
        log_a = -LRU_C * r * _softplus_neg(cw_ref[7:8, :])
        a = jnp.exp(log_a)
        mult = jnp.sqrt(_neg_expm1(2.0 * log_a))
        b = mult * (ig * uc)
        a_ref[...] = a
        uc_ref[...] = uc
        r_ref[...] = r
        ig_ref[...] = ig
        mult_ref[...] = mult
        h, last = _linear_scan(a, b, hcar_ref[0:1, :], reverse=False)
        h_ref[...] = h
        hcar_ref[...] = jnp.broadcast_to(last, (8, D_MODEL))

    row_spec = pl.BlockSpec((T, D_MODEL), lambda c: (c, 0))
    row_shape = jax.ShapeDtypeStruct((S, D_MODEL), F32)
    return _call(
        body, (rest, cw, wr2, wi2), name="lru_fwd", grid=(S // T,),
        in_specs=[row_spec, pl.BlockSpec((8, D_MODEL), lambda c: (0, 0)),
                  pl.BlockSpec((8, 128, 128), lambda c: (0, 0, 0)),
                  pl.BlockSpec((8, 128, 128), lambda c: (0, 0, 0))],
        out_specs=(row_spec,) * 6, out_shape=(row_shape,) * 6,
        scratch_shapes=[pltpu.VMEM((8, D_MODEL), F32), pltpu.VMEM((8, D_MODEL), F32)],
        hosted=hosted)


def _lru_bwd(dhl, hl, a_all, gates, rest, cw, wr2, wi2, dproj, hosted=None):
    S = rest.shape[0]
    T = LRU_T
    nc = S // T

    def body(dh_ref, h_ref, hp_ref, a_ref, uc_ref, r_ref, ig_ref, mult_ref, u_ref, up_ref, cw_ref, wr_ref,
             wi_ref, _alias, du_ref, acc_ref, gwr_ref, gwi_ref, gcar_ref, acar_ref, dcar_ref):
        step = pl.program_id(0)
        c = nc - 1 - step

        @pl.when(step == 0)
        def _():
            acc_ref[...] = jnp.zeros_like(acc_ref)
            gwr_ref[...] = jnp.zeros_like(gwr_ref)
            gwi_ref[...] = jnp.zeros_like(gwi_ref)
            gcar_ref[...] = jnp.zeros_like(gcar_ref)
            acar_ref[...] = jnp.zeros_like(acar_ref)
            dcar_ref[...] = jnp.zeros_like(dcar_ref)

        row = lax.broadcasted_iota(jnp.int32, (T, D_MODEL), 0)
        u = u_ref[...]
        u_prev = jnp.where(c > 0, up_ref[...], 0.0)
        h_prev8 = jnp.where(c > 0, hp_ref[...], 0.0)
        a = a_ref[...]
        h = h_ref[...]
        uc, r, ig, mult = uc_ref[...], r_ref[...], ig_ref[...], mult_ref[...]
        ucb = uc.astype(BF16)
        sp = _softplus_neg(cw_ref[7:8, :])
        a_next = jnp.where(row < T - 1, pltpu.roll(a, T - 1, 0), acar_ref[0:1, :])
        G, first = _linear_scan(a_next, dh_ref[...], gcar_ref[0:1, :], reverse=True)
        gcar_ref[...] = jnp.broadcast_to(first, (8, D_MODEL))
        acar_ref[...] = jnp.broadcast_to(a[0:1, :], (8, D_MODEL))
        he = jnp.concatenate([h_prev8, h], axis=0)
        h_before = pltpu.roll(he, 1, 0)[8:8 + T]
        d_a = G * h_before
        d_mult = G * (ig * uc)
        d_ig = G * (mult * uc)
        duc = G * (mult * ig)
        d_log_a = d_a * a - d_mult * (a * a) / mult
        d_r = d_log_a * (-LRU_C * sp)
        d_sp = jnp.sum(d_log_a * (-LRU_C * r), axis=0, keepdims=True)
        dpre_r = d_r * r * (1.0 - r)
        dpre_i = d_ig * ig * (1.0 - ig)
        dprb = dpre_r.astype(BF16)
        dpib = dpre_i.astype(BF16)
        back = []
        for j in range(8):
            sl = slice(j * 128, (j + 1) * 128)
            back.append(_dot_nt(dprb[:, sl], wr_ref[j]) + _dot_nt(dpib[:, sl], wi_ref[j]))
            gwr_ref[j] += _dot_tn(ucb[:, sl], dprb[:, sl])
            gwi_ref[j] += _dot_tn(ucb[:, sl], dpib[:, sl])
        duc = duc + jnp.concatenate(back, axis=1)
        de = jnp.concatenate([duc, dcar_ref[...]], axis=0)
        du = cw_ref[0:1, :] * duc
        ue = jnp.concatenate([u_prev, u], axis=0)
        acc_ref[0:1, :] += jnp.sum(duc * u, axis=0, keepdims=True)
        for j in range(1, CONV_WIDTH):
            du = du + cw_ref[j:j + 1, :] * pltpu.roll(de, T + 8 - j, 0)[0:T]
            acc_ref[j:j + 1, :] += jnp.sum(duc * pltpu.roll(ue, j, 0)[8:8 + T], axis=0, keepdims=True)
        dcar_ref[...] = duc[0:8, :]
        acc_ref[4:5, :] += jnp.sum(duc, axis=0, keepdims=True)
        acc_ref[5:6, :] += jnp.sum(dpre_r, axis=0, keepdims=True)
        acc_ref[6:7, :] += jnp.sum(dpre_i, axis=0, keepdims=True)
        acc_ref[7:8, :] += d_sp
        du_ref[0] = du[:, 0:ATT_WIDTH].astype(BF16)
        du_ref[1] = du[:, ATT_WIDTH:].astype(BF16)

    def rev(step):
        return nc - 1 - step

    def prev8(step):
        return jnp.maximum(rev(step) * (T // 8) - 1, 0)

    row_spec = pl.BlockSpec((T, D_MODEL), lambda s: (rev(s), 0))
    prev_spec = pl.BlockSpec((8, D_MODEL), lambda s: (prev8(s), 0))
    vec_spec = pl.BlockSpec((8, D_MODEL), lambda s: (0, 0))
    w_spec = pl.BlockSpec((8, 128, 128), lambda s: (0, 0, 0))
    return _call(
        body, (dhl, hl, hl, a_all, *gates, rest, rest, cw, wr2, wi2, dproj), name="lru_bwd", grid=(nc,),
        in_specs=[row_spec, row_spec, prev_spec, row_spec, row_spec, row_spec, row_spec, row_spec,
                  row_spec, prev_spec, vec_spec, w_spec, w_spec, ANY_SPEC],
        out_specs=(pl.BlockSpec((2, T, ATT_WIDTH), lambda s: (4, rev(s), 0)), vec_spec, w_spec, w_spec),
        out_shape=(jax.ShapeDtypeStruct(dproj.shape, BF16), jax.ShapeDtypeStruct((8, D_MODEL), F32),
                   jax.ShapeDtypeStruct((8, 128, 128), F32), jax.ShapeDtypeStruct((8, 128, 128), F32)),
        scratch_shapes=[pltpu.VMEM((8, D_MODEL), F32), pltpu.VMEM((8, D_MODEL), F32),
                        pltpu.VMEM((8, D_MODEL), F32)],
        aliases={13: 0}, hosted=hosted)


OUT_TM = 256


def _out_fwd(a_gated, hl, rest, x, vec, w_pa, w_pb, w_o, hosted=None):
    S = x.shape[0]
    tm = OUT_TM

    def body(ag_ref, hl_ref, gl_ref, ma_ref, mb_ref, x_ref, v_ref, wpa_ref, wpb_ref, wo_ref,
             xn_ref, mix_ref, ya_ref, yb_ref, bg_ref):
        bg = (hl_ref[...] * _silu(gl_ref[...])).astype(BF16)
        ya = _dot(ag_ref[...], wpa_ref[...])
        yb = _dot(bg, wpb_ref[...])
        mix = (jax.nn.sigmoid(ma_ref[...]) * ya + jax.nn.sigmoid(mb_ref[...]) * yb).astype(BF16)
        out = _dot(mix, wo_ref[...])
        rstd = lax.rsqrt(jnp.mean(out * out, axis=-1, keepdims=True) + NORM_EPS)
        xn_ref[...] = x_ref[...] + v_ref[1:2, :] * ((out * rstd) * v_ref[0:1, :])
        mix_ref[...] = mix
        ya_ref[...] = ya.astype(BF16)
        yb_ref[...] = yb.astype(BF16)
        bg_ref[...] = bg

    def col(j):
        return pl.BlockSpec((tm, D_MODEL), lambda i: (i, j))

    def whole(shape):
        return pl.BlockSpec(shape, lambda i: (0, 0))

    row = col(0)
    bf = jax.ShapeDtypeStruct((S, D_MODEL), BF16)
    return _call(
        body, (a_gated, hl, rest, rest, rest, x, vec, w_pa, w_pb, w_o), name="out_fwd", grid=(S // tm,),
        in_specs=[pl.BlockSpec((tm, ATT_WIDTH), lambda i: (i, 0)), row, col(1), col(2), col(3), row,
                  whole((8, D_MODEL)), whole((ATT_WIDTH, D_MODEL)), whole((D_MODEL, D_MODEL)),
                  whole((D_MODEL, D_MODEL))],
        out_specs=(row, row, row, row, row),
        out_shape=(jax.ShapeDtypeStruct((S, D_MODEL), F32), bf, bf, bf, bf),
        hosted=hosted)


def _out_bwd(dxn, mix, ya, yb, hl, rest, o_att, vec, w_pa, w_pb, w_o, hosted=None):
    S = dxn.shape[0]
    tm = OUT_TM
    forth = [_perm_matrix(r, tm) for r in DILATIONS[1:]]

    def body(dxn_ref, mix_ref, ya_ref, yb_ref, hl_ref, gl_ref, ma_ref, mb_ref, ga_ref, o_ref, v_ref,
             wpa_ref, wpb_ref, wo_ref, f1_ref, f2_ref,
             dout_ref, dya_ref, dyb_ref, dhl_ref, dp_ref, acc_ref,
             do0_ref, do1_ref, do2_ref, dv0_ref, dv1_ref, dv2_ref):
        i = pl.program_id(0)

        @pl.when(i == 0)
        def _():
            acc_ref[...] = jnp.zeros_like(acc_ref)

        g_post = v_ref[0:1, :]
        gate = v_ref[1:2, :]
        dxn_v = dxn_ref[...]
        out = _dot(mix_ref[...], wo_ref[...])
        rstd = lax.rsqrt(jnp.mean(out * out, axis=-1, keepdims=True) + NORM_EPS)
        nrm = out * rstd
        acc_ref[0:1, :] += jnp.sum(dxn_v * (nrm * g_post), axis=0, keepdims=True)
        acc_ref[1:2, :] += jnp.sum(dxn_v * gate * nrm, axis=0, keepdims=True)
        dn = dxn_v * (gate * g_post)
        dout = (rstd * (dn - nrm * jnp.mean(dn * nrm, axis=-1, keepdims=True))).astype(BF16)
        dout_ref[...] = dout
        dmix = _dot_nt(dout, wo_ref[...])
        sa = jax.nn.sigmoid(ma_ref[...])
        sb = jax.nn.sigmoid(mb_ref[...])
        dya = (dmix * sa).astype(BF16)
        dyb = (dmix * sb).astype(BF16)
        dya_ref[...] = dya
        dyb_ref[...] = dyb
        dma = dmix * ya_ref[...].astype(F32) * (sa * (1.0 - sa))
        dmb = dmix * yb_ref[...].astype(F32) * (sb * (1.0 - sb))
        d_ag = _dot_nt(dya, wpa_ref[...])
        d_bg = _dot_nt(dyb, wpb_ref[...])
        gl = gl_ref[...]
        hl_v = hl_ref[...]
        dhl_ref[...] = d_bg * _silu(gl)
        dgl = d_bg * hl_v * _dsilu(gl)
        ga = ga_ref[...]
        o = o_ref[...]
        do = d_ag * _silu(ga)
        dga = d_ag * o * _dsilu(ga)
        lane = lax.broadcasted_iota(jnp.int32, (tm, 128), 1)
        dvec = jnp.zeros((tm, 128), F32)
        for h in range(HEADS):
            sl = slice(h * HEAD_DIM, (h + 1) * HEAD_DIM)
            dvec = jnp.where(lane == h, jnp.sum(do[:, sl] * o[:, sl], axis=1, keepdims=True), dvec)
        do_b = do.astype(BF16)
        do0_ref[0] = do_b
        dv0_ref[0] = dvec
        for do_ref, dv_ref, f_ref in ((do1_ref, dv1_ref, f1_ref), (do2_ref, dv2_ref, f2_ref)):
            _store_residues(do_ref, _dot(f_ref[...], do_b).astype(BF16))
            _store_residues(dv_ref, _permute_f32(f_ref[...], dvec, 3))
        dp_ref[0] = dgl[:, 0:ATT_WIDTH].astype(BF16)
        dp_ref[1] = dgl[:, ATT_WIDTH:].astype(BF16)
        dp_ref[2] = dma[:, 0:ATT_WIDTH].astype(BF16)
        dp_ref[3] = dma[:, ATT_WIDTH:].astype(BF16)
        dp_ref[4] = dmb[:, 0:ATT_WIDTH].astype(BF16)
        dp_ref[5] = dmb[:, ATT_WIDTH:].astype(BF16)
        dp_ref[6] = dga.astype(BF16)

    def col(j):
        return pl.BlockSpec((tm, D_MODEL), lambda i: (i, j))

    def whole(shape):
        return pl.BlockSpec(shape, lambda i: (0, 0))

    def res_spec(r, width):
        return pl.BlockSpec((r, tm // r, width), lambda i: (0, i, 0))

    row = col(0)
    att = pl.BlockSpec((tm, ATT_WIDTH), lambda i: (i, 0))
    bf = jax.ShapeDtypeStruct((S, D_MODEL), BF16)
    vmem = pl.BlockSpec(memory_space=pltpu.VMEM)
    return _call(
        body, (dxn, mix, ya, yb, hl, rest, rest, rest, rest, o_att, vec, w_pa, w_pb, w_o, *forth),
        name="out_bwd", grid=(S // tm,),
        in_specs=[row, row, row, row, row, col(1), col(2), col(3),
                  pl.BlockSpec((tm, ATT_WIDTH), lambda i: (i, 8)), att,
                  whole((8, D_MODEL)), whole((ATT_WIDTH, D_MODEL)), whole((D_MODEL, D_MODEL)),
                  whole((D_MODEL, D_MODEL)), vmem, vmem],
        out_specs=(row, row, row, row,
                   pl.BlockSpec((7, tm, ATT_WIDTH), lambda i: (0, i, 0)), whole((8, D_MODEL)),
                   *[res_spec(r, ATT_WIDTH) for r in DILATIONS], *[res_spec(r, 128) for r in DILATIONS]),
        out_shape=(bf, bf, bf, jax.ShapeDtypeStruct((S, D_MODEL), F32),
                   jax.ShapeDtypeStruct((DREST_CHUNKS, S, ATT_WIDTH), BF16),
                   jax.ShapeDtypeStruct((8, D_MODEL), F32),
                   *[jax.ShapeDtypeStruct((r, S // r, ATT_WIDTH), BF16) for r in DILATIONS],
                   *[jax.ShapeDtypeStruct((r, S // r, 128), F32) for r in DILATIONS]),
        hosted=hosted)


def _loss_head(y, target):
    S = y.shape[0]
    tm = 512

    def body(y_ref, t_ref, dy_ref, acc_ref):
        i = pl.program_id(0)

        @pl.when(i == 0)
        def _():
            acc_ref[...] = jnp.zeros_like(acc_ref)

        err = y_ref[...] - t_ref[...]
        dy_ref[...] = err * (1.0 / D_MODEL)
        part = jnp.sum(jnp.sum(err * err, axis=1, keepdims=True), axis=0, keepdims=True)
        acc_ref[...] += jnp.broadcast_to(part, acc_ref.shape)

    row = pl.BlockSpec((tm, D_MODEL), lambda i: (i, 0))
    return pl.pallas_call(
        body, name="loss_head", grid=(S // tm,),
        in_specs=[row, row],
        out_specs=(row, pl.BlockSpec((8, 128), lambda i: (0, 0))),
        out_shape=(jax.ShapeDtypeStruct((S, D_MODEL), F32), jax.ShapeDtypeStruct((8, 128), F32)),
        compiler_params=_params(("arbitrary",)))(y, target)


def _adamw(gsrcs, w, m, v, n_stack, name, rows=None):
    n_layers, R, C = w.shape
    assert len(gsrcs) == n_layers
    budget = 96 * 1024
    tr = rows
    if tr is None:
        tr = R
        while tr * C > budget and tr % 16 == 0:
            tr //= 2
    assert R % tr == 0 and (tr % 8 == 0 or tr == R)
    c1 = 1.0 - ADAM_B1 ** ADAM_STEP
    c2 = 1.0 - ADAM_B2 ** ADAM_STEP

    def body(*refs):
        g_refs = refs[:n_layers]
        w_ref, m_ref, v_ref, go_ref, d_ref, mo_ref, vo_ref = refs[n_layers:]

        def update(g_ref):
            if n_stack:
                g = g_ref[0].astype(F32)
                for s in range(1, n_stack):
                    g = g + g_ref[s].astype(F32)
            else:
                g = g_ref[...]
            m_new = ADAM_B1 * m_ref[...] + (1.0 - ADAM_B1) * g
            v_new = ADAM_B2 * v_ref[...] + (1.0 - ADAM_B2) * (g * g)
            m_hat = m_new / c1
            v_hat = v_new / c2
            go_ref[...] = g
            d_ref[...] = -ADAM_LR * (m_hat / (jnp.sqrt(v_hat) + ADAM_EPS) + ADAM_WD * w_ref[...])
            mo_ref[...] = m_new
            vo_ref[...] = v_new

        for layer in range(n_layers):
            pl.when(pl.program_id(0) == layer)(functools.partial(update, g_refs[layer]))

    def g_spec(layer):
        def rows_of(l, i):
            return jnp.where(l == layer, i, 0)
        if n_stack:
            return pl.BlockSpec((n_stack, tr, C), lambda l, i: (0, rows_of(l, i), 0))
        return pl.BlockSpec((tr, C), lambda l, i: (rows_of(l, i), 0))

    blk = pl.BlockSpec((None, tr, C), lambda l, i: (l, i, 0))
    shp = jax.ShapeDtypeStruct((n_layers, R, C), F32)
    return pl.pallas_call(
        body, name=name, grid=(n_layers, R // tr),
        in_specs=[g_spec(layer) for layer in range(n_layers)] + [blk, blk, blk],
        out_specs=(blk, blk, blk, blk), out_shape=(shp, shp, shp, shp),
        compiler_params=_params(("arbitrary", "arbitrary")))(*gsrcs, w, m, v)


def _pair_blocks(w):
    w = w.reshape(8, 2, 64, 64)
    z = jnp.zeros((8, 64, 64), w.dtype)
    top = jnp.concatenate([w[:, 0], z], axis=2)
    bot = jnp.concatenate([z, w[:, 1]], axis=2)
    return jnp.concatenate([top, bot], axis=1).astype(BF16)


def _unpair_blocks(g):
    return jnp.stack([g[:, :64, :64], g[:, 64:, 64:]], axis=1).reshape(16, 64, 64)


def _layer_params(layer, w_in_g, w_pa_g, w_pb_g, w_o_g, conv_w_full, conv_b, b_rg, b_ig, lru_lambda, w_rg,
                  w_ig, g_pre, g_post):
    cw = jnp.concatenate([conv_w_full, conv_b[None], b_rg[None], b_ig[None], lru_lambda[None]], axis=0)
    return dict(
        w_int=_weights_to_internal(w_in_g, f"weights_to_internal_{layer}"),
        w_pa=jnp.transpose(w_pa_g, (1, 0, 2)).reshape(ATT_WIDTH, D_MODEL),
        w_pb=w_pb_g.reshape(D_MODEL, D_MODEL), w_o=w_o_g.reshape(D_MODEL, D_MODEL),
        cw=cw, wr2=_pair_blocks(w_rg), wi2=_pair_blocks(w_ig),
        g_pre=g_pre, g_post=g_post, lru_lambda=lru_lambda)


def _layer_fwd(x, mod, p, host_proj=None, host_lru=None, host_out=None):
    zeros = jnp.zeros((5, D_MODEL), F32)
    vec_pre = jnp.concatenate([p["g_pre"][None], mod[1:2], mod[0:1], zeros], axis=0)
    (ht0, ht1, ht2, q0, q1, q2, rest), got_proj = _norm_proj_fwd(x, vec_pre, p["w_int"], hosted=host_proj)
    h_t = [ht0, ht1, ht2]
    qkv, outs, stats = [q0, q1, q2], [], []
    for g, r in enumerate(DILATIONS):
        o_g, st_g = _attn_fwd(qkv[g], g, r)
        outs.append(o_g)
        stats.append(st_g)
    o_att, a_gated, *lse = _attn_combine(outs, stats, rest)
    (hl, a_dec, *gates), got_lru = _lru_fwd(rest, p["cw"], p["wr2"], p["wi2"], hosted=host_lru)
    vec_post = jnp.concatenate([p["g_post"][None], mod[2:3], jnp.zeros((6, D_MODEL), F32)], axis=0)
    (x_new, mix, ya, yb, b_gated), got_out = _out_fwd(a_gated, hl, rest, x, vec_post, p["w_pa"], p["w_pb"],
                                                      p["w_o"], hosted=host_out)
    saved = dict(x=x, h_t=h_t, qkv=qkv, rest=rest, o_att=o_att, a_gated=a_gated, lse=lse, hl=hl, a_dec=a_dec,
                 gates=gates, mix=mix, ya=ya, yb=yb, b_gated=b_gated, vec_pre=vec_pre, vec_post=vec_post)
    return x_new, saved, got_proj, got_lru, got_out


def _mats_sources(g_pa, g_pb, g_o):
    return [jnp.transpose(g_pa.astype(BF16).reshape(ATT_WIDTH, N_DEV, 128), (1, 0, 2)),
            g_pb.astype(BF16).reshape(N_DEV, 128, D_MODEL), g_o.astype(BF16).reshape(N_DEV, 128, D_MODEL)]


LRU_SMALL = ("conv_b", "w_rg", "b_rg", "w_ig", "b_ig", "lru_lambda")
MOD_SMALL = ("b_mod", "g_pre", "g_post")


def _flat_rows(arrs):
    return jnp.concatenate([a.reshape(-1) for a in arrs]).reshape(-1, 128)


def _layer_bwd(dxn, sv, p, upper=None, host_own=False):
    a2a3 = ["a2a"] * 3
    (dout, dya, dyb, dhl, drest, acc_out, *do_dvec), got_up_top = _out_bwd(
        dxn, sv["mix"], sv["ya"], sv["yb"], sv["hl"], sv["rest"], sv["o_att"], sv["vec_post"],
        p["w_pa"], p["w_pb"], p["w_o"], hosted=([upper["w_in"][0]], ["a2a"]) if upper else None)
    mats = _mats_sources(_wgrad_tn(sv["a_gated"], dya, "wgrad_pa"), _wgrad_tn(sv["b_gated"], dyb, "wgrad_pb"),
                         _wgrad_tn(sv["mix"], dout, "wgrad_o"))
    (drest, acc_lru, gwr2, gwi2), got_up_bottom = _lru_bwd(
        dhl, sv["hl"], sv["a_dec"], sv["gates"], sv["rest"], p["cw"], p["wr2"], p["wi2"], drest,
        hosted=([upper["w_in"][1]], ["a2a"]) if upper else None)
    small = dict(
        conv_w=acc_lru[0:4], conv_b=acc_lru[4], b_rg=acc_lru[5], b_ig=acc_lru[6],
        lru_lambda=acc_lru[7] * (-jax.nn.sigmoid(-p["lru_lambda"])),
        w_rg=_unpair_blocks(gwr2), w_ig=_unpair_blocks(gwi2))
    hosts = [None, None, None]
    if host_own:
        lru_vec = _flat_rows([jnp.stack([small[n], upper["small"][n]]) for n in LRU_SMALL + ("conv_w",)])
        hosts = [(mats, a2a3), ([lru_vec], ["ag"]), (upper["mats"], a2a3)]
    dqkv, got_attn = [], []
    for g, r in enumerate(DILATIONS):
        d_g, got = _attn_bwd(sv["qkv"][g], do_dvec[g], sv["lse"][g], do_dvec[3 + g], g, r, hosted=hosts[g])
        dqkv.append(d_g)
        got_attn.append(got)
    parts = [_proj_wgrad_group(sv["h_t"][g], dqkv[g], r, f"proj_wgrad_g{g}") for g, r in enumerate(DILATIONS)]
    parts.append(_proj_wgrad_part(sv["h_t"][0], drest, 2, 4, "proj_wgrad_u"))
    parts.append(_proj_wgrad_part(sv["h_t"][0], drest, 7, 0, "proj_wgrad_rest"))
    gw_in = _grads_to_shards(parts, "grads_to_shards")
    (dx, acc_pre), got_in = _proj_dgrad_norm_bwd(
        dqkv, drest, p["w_int"], sv["x"], dxn, sv["vec_pre"],
        hosted=(gw_in, ["a2a"] * 2) if host_own else None)
    small.update(dmod=jnp.concatenate([acc_pre[0], acc_pre[1], acc_out[0]]), g_pre=acc_pre[2],
                 g_post=acc_out[1])
    comm = dict(sources=dict(mats=mats, w_in=gw_in, small=small),
                own=dict(mats=got_attn[0], w_in=got_in, lru_small=got_attn[1]),
                upper=dict(mats=got_attn[2], w_in=got_up_top + got_up_bottom))
    return dx, small, comm


SMALL_NAMES = ("b_mod", "g_pre", "conv_b", "w_rg", "b_rg", "w_ig", "b_ig", "lru_lambda", "g_post")


def kernel(x, c, w_mod, b_mod, g_pre, w_in, conv_w, conv_b, w_rg, b_rg, w_ig, b_ig, lru_lambda, w_pa, w_pb, w_o, g_post, loss_target, m_w_mod, m_b_mod, m_g_pre, m_w_in, m_conv_w, m_conv_b, m_w_rg, m_b_rg, m_w_ig, m_b_ig, m_lru_lambda, m_w_pa, m_w_pb, m_w_o, m_g_post, v_w_mod, v_b_mod, v_g_pre, v_w_in, v_conv_w, v_conv_b, v_w_rg, v_b_rg, v_w_ig, v_b_ig, v_lru_lambda, v_w_pa, v_w_pb, v_w_o, v_g_post):
    W = dict(w_mod=w_mod, b_mod=b_mod, g_pre=g_pre, w_in=w_in, conv_w=conv_w, conv_b=conv_b, w_rg=w_rg,
             b_rg=b_rg, w_ig=w_ig, b_ig=b_ig, lru_lambda=lru_lambda, w_pa=w_pa, w_pb=w_pb, w_o=w_o,
             g_post=g_post)
    M = dict(w_mod=m_w_mod, b_mod=m_b_mod, g_pre=m_g_pre, w_in=m_w_in, conv_w=m_conv_w, conv_b=m_conv_b,
             w_rg=m_w_rg, b_rg=m_b_rg, w_ig=m_w_ig, b_ig=m_b_ig, lru_lambda=m_lru_lambda, w_pa=m_w_pa,
             w_pb=m_w_pb, w_o=m_w_o, g_post=m_g_post)
    V = dict(w_mod=v_w_mod, b_mod=v_b_mod, g_pre=v_g_pre, w_in=v_w_in, conv_w=v_conv_w, conv_b=v_conv_b,
             w_rg=v_w_rg, b_rg=v_b_rg, w_ig=v_w_ig, b_ig=v_b_ig, lru_lambda=v_lru_lambda, w_pa=v_w_pa,
             w_pb=v_w_pb, w_o=v_w_o, g_post=v_g_post)
    S = x.shape[1]
    me = 4 * lax.axis_index("x") + 2 * lax.axis_index("y") + lax.axis_index("c")
    n_mod = w_mod.shape[2]
    n_in = w_in.shape[2]
    n_conv = conv_w.shape[2]

    c_rows = jnp.broadcast_to(c, (8, D_MODEL))
    w_in_b, w_pa_b, w_pb_b, w_o_b = (t.astype(BF16) for t in (w_in, w_pa, w_pb, w_o))
    g_c, g_win0, g_wpa0, g_wpb0, g_wo0, g_cw = _gather_two_level(
        [c_rows, w_in_b[0], w_pa_b[0], w_pb_b[0], w_o_b[0], conv_w], "gather_weights")
    c_all = g_c[:, 0, :]
    c_pad = jnp.concatenate([c_all, jnp.zeros((8, D_MODEL), F32)], axis=0)
    conv_w_full = jnp.transpose(g_cw, (1, 2, 0, 3)).reshape(2, CONV_WIDTH, D_MODEL)

    mod_cols = _mod_fwd(c_pad, w_mod)
    mod_src = jnp.transpose(mod_cols[:, :8], (1, 0, 2))
    mod_src = jnp.concatenate([mod_src, jnp.zeros((8, 6, n_mod), F32)], axis=1)
    (mod_got,) = _exchange([mod_src], ["a2a"], "scatter_mod")
    mod = jnp.transpose(mod_got[:, :2], (1, 0, 2)).reshape(2, 3 * D_MODEL) + b_mod

    def layer_params(layer, w_in_g, w_pa_g, w_pb_g, w_o_g):
        return _layer_params(layer, w_in_g, w_pa_g, w_pb_g, w_o_g, conv_w_full[layer], conv_b[layer],
                             b_rg[layer], b_ig[layer], lru_lambda[layer], w_rg[layer], w_ig[layer],
                             g_pre[layer], g_post[layer])

    half = D_MODEL // 2
    layers = [layer_params(0, [g_win0], g_wpa0, g_wpb0, g_wo0), None]
    act, sv0, (g_win1_top,), (g_win1_bottom,), (g_wpa1, g_wpb1, g_wo1) = _layer_fwd(
        x[0], mod[0].reshape(3, D_MODEL), layers[0],
        host_proj=([w_in_b[1, :half]], ["ag"]), host_lru=([w_in_b[1, half:]], ["ag"]),
        host_out=([w_pa_b[1], w_pb_b[1], w_o_b[1]], ["ag"] * 3))
    layers[1] = layer_params(1, [g_win1_top, g_win1_bottom], g_wpa1, g_wpb1, g_wo1)
    act, sv1, _, _, _ = _layer_fwd(act, mod[1].reshape(3, D_MODEL), layers[1])
    dy, loss_acc = _loss_head(act, loss_target[0])

    dy, small1, comm1 = _layer_bwd(dy, sv1, layers[1])
    dy, small0, comm0 = _layer_bwd(dy, sv0, layers[0], upper=comm1["sources"], host_own=True)
    grad_x = dy[None]
    r_pa, r_pb, r_o = ([comm0["own"]["mats"][k], comm0["upper"]["mats"][k]] for k in range(3))
    r_in = comm0["own"]["w_in"] + comm0["upper"]["w_in"]
    grads = [small0, small1]

    def stack2(name):
        return jnp.stack([grads[0][name], grads[1][name]], axis=0)

    dmod = stack2("dmod")
    dmod_src = jnp.transpose(dmod.reshape(2, N_DEV, n_mod), (1, 0, 2))
    dmod_src = jnp.concatenate([dmod_src, jnp.zeros((8, 6, n_mod), F32)], axis=1)
    mod_vec = _flat_rows([dmod, stack2("g_pre"), stack2("g_post"), loss_acc])
    r_dmod, r_mod_small = _exchange([dmod_src, mod_vec], ["a2a", "ag"], "exchange_grads")
    mod_rows = mod_vec.shape[0] - loss_acc.shape[0]
    loss = 0.5 * jnp.sum(r_mod_small[:, mod_rows, 0]) / D_MODEL
    (r_lru_small,) = comm0["own"]["lru_small"]

    res = {}
    res["w_in"] = _adamw(r_in, *(t.reshape(4, half, n_in) for t in (w_in, m_w_in, v_w_in)), 8, "adamw_w_in")
    res["w_pa"] = _adamw(r_pa, w_pa, m_w_pa, v_w_pa, 8, "adamw_w_pa")
    res["w_pb"] = _adamw(r_pb, w_pb, m_w_pb, v_w_pb, 8, "adamw_w_pb")
    res["w_o"] = _adamw(r_o, w_o, m_w_o, v_w_o, 8, "adamw_w_o")
    dmod_all = jnp.transpose(r_dmod[:, :2], (1, 0, 2))
    gw_mod = _mod_wgrad(jnp.transpose(c_all), dmod_all)
    res["w_mod"] = _adamw([gw_mod[0], gw_mod[1]], w_mod, m_w_mod, v_w_mod, 0, "adamw_w_mod")
    for names_rep, stack, tag in ((LRU_SMALL, r_lru_small, "lru"), (MOD_SMALL, r_mod_small, "mod")):
        rows = sum(W[n].size for n in names_rep) // 128
        rep = _adamw([stack], *(_flat_rows([src[n] for n in names_rep])[None] for src in (W, M, V)), 8,
                     f"adamw_small_{tag}", rows=rows // 2 if rows % 16 == 0 else rows)
        off = 0
        for name in names_rep:
            size = W[name].size
            res[name] = tuple(t.reshape(-1)[off:off + size].reshape(W[name].shape) for t in rep)
            off += size
    lru_rows = sum(W[n].size for n in LRU_SMALL) // 128
    conv_stack = r_lru_small[:, lru_rows:].reshape(8, 2, CONV_WIDTH, D_MODEL)
    conv_stack = lax.dynamic_slice_in_dim(conv_stack, me * n_conv, n_conv, axis=3).reshape(8, 8, n_conv)
    res["conv_w"] = _adamw([conv_stack], conv_w.reshape(1, 8, n_conv), m_conv_w.reshape(1, 8, n_conv),
                           v_conv_w.reshape(1, 8, n_conv), 8, "adamw_conv_w")

    names = ("w_mod", "b_mod", "g_pre", "w_in", "conv_w", "conv_b", "w_rg", "b_rg", "w_ig", "b_ig",
             "lru_lambda", "w_pa", "w_pb", "w_o", "g_post")
    outs = [loss, grad_x]
    for k in range(4):
        outs.extend(res[n][k].reshape(W[n].shape) for n in names)
    return tuple(outs)
```

```python
import functools

import jax
import jax.numpy as jnp
import numpy as np
from jax import lax
from jax.experimental import pallas as pl
from jax.experimental.pallas import tpu as pltpu

F32 = jnp.float32
BF16 = jnp.bfloat16

N_DEV = 8
D_MODEL = 1024
HEAD_DIM = 128
HEADS = 4
ATT_WIDTH = HEADS * HEAD_DIM
DILATIONS = (1, 4, 16)
BAND = 128
N_CHUNKS = 18
QKV_CHUNKS = 9
CONV_WIDTH = 4
LRU_C = 8.0
NORM_EPS = 1e-6
NEG_INF = -1e30
ADAM_LR = 0.001
ADAM_B1 = 0.9
ADAM_B2 = 0.999
ADAM_EPS = 1e-08
ADAM_WD = 0.01
ADAM_STEP = 10
CHUNK_PERM = (0, 3, 6, 1, 4, 7, 2, 5, 8, 10, 11, 12, 13, 14, 15, 16, 17, 9)
DREST_CHUNKS = 10
VMEM_LIMIT = 56 * 1024 * 1024
MESH_AXES = ("x", "y", "c")


def _params(sem=None):
    return pltpu.CompilerParams(dimension_semantics=sem, vmem_limit_bytes=VMEM_LIMIT)


def _silu(x):
    return x * jax.nn.sigmoid(x)


def _dsilu(x):
    s = jax.nn.sigmoid(x)
    return s * (1.0 + x * (1.0 - s))


def _neg_expm1(x):
    series = -x * (1.0 + x * (0.5 + x * (1.0 / 6.0 + x * (1.0 / 24.0 + x * (1.0 / 120.0)))))
    return jnp.where(x > -0.05, series, 1.0 - jnp.exp(x))


def _softplus_neg(lam):
    z = jnp.exp(-jnp.abs(lam))
    small = z * (1.0 - z * (0.5 - z * (1.0 / 3.0 - z * 0.25)))
    log1p_z = jnp.where(z < 1e-2, small, jnp.log(1.0 + z))
    return jnp.maximum(-lam, 0.0) + log1p_z


def _dot(a, b):
    return jnp.dot(a, b, preferred_element_type=F32)


def _dot_nt(a, b):
    return lax.dot_general(a, b, (((1,), (1,)), ((), ())), preferred_element_type=F32)


def _dot_tn(a, b):
    return lax.dot_general(a, b, (((0,), (0,)), ((), ())), preferred_element_type=F32)


def _perm_matrix(r, rows, transpose=False):
    n = rows // r
    p = np.zeros((rows, rows), np.float32)
    dst = np.arange(rows)
    p[dst, (dst % n) * r + dst // n] = 1.0
    return jnp.asarray(p.T if transpose else p, dtype=BF16)


def _permute_f32(p, x, pieces):
    part = x.astype(BF16)
    acc = _dot(p, part)
    for _ in range(pieces - 1):
        x = x - part.astype(F32)
        part = x.astype(BF16)
        acc = acc + _dot(p, part)
    return acc


def _rows_of_residues(ref, lead=()):
    r = ref.shape[len(lead)]
    if r == 1:
        return ref[lead + (0,)]
    return jnp.concatenate([ref[lead + (rho,)] for rho in range(r)], axis=0)


def _store_residues(ref, x, lead=()):
    r = ref.shape[len(lead)]
    n = x.shape[0] // r
    for rho in range(r):
        ref[lead + (rho,)] = x[rho * n:(rho + 1) * n]


ANY_SPEC = pl.BlockSpec(memory_space=pl.ANY)


def _exchange_shapes(arrs, modes):
    return [jax.ShapeDtypeStruct(((N_DEV,) + a.shape) if mode == "ag" else a.shape, a.dtype)
            for a, mode in zip(arrs, modes)]


def _exchange_sems(n):
    return [pltpu.SemaphoreType.DMA((7 * n,)), pltpu.SemaphoreType.DMA((7 * n,)),
            pltpu.SemaphoreType.DMA((n,))]


def _exchange_copies(ins, outs, modes, sems):
    n = len(ins)
    send_sems, recv_sems, local_sems = sems
    x, y, c = lax.axis_index("x"), lax.axis_index("y"), lax.axis_index("c")
    me = 4 * x + 2 * y + c

    def src_for(a, dev):
        return ins[a] if modes[a] == "ag" else ins[a].at[dev]

    local = [pltpu.make_async_copy(src_for(a, me), outs[a].at[me], local_sems.at[a]) for a in range(n)]
    sends, arrivals = [], []
    for k in range(1, N_DEV):
        px = 1 - x if (k >> 2) & 1 else x
        py = 1 - y if (k >> 1) & 1 else y
        pc = 1 - c if k & 1 else c
        peer = 4 * px + 2 * py + pc
        for a in range(n):
            s = (k - 1) * n + a
            for dst, group in ((me, sends), (peer, arrivals)):
                group.append(pltpu.make_async_remote_copy(
                    src_ref=src_for(a, peer), dst_ref=outs[a].at[dst],
                    send_sem=send_sems.at[s], recv_sem=recv_sems.at[s],
                    device_id=(px, py, pc), device_id_type=pl.DeviceIdType.MESH))
    return local, sends, arrivals


def _exchange_start(copies):
    local, sends, _ = copies
    for cp in local + sends:
        cp.start()


def _exchange_wait(copies):
    local, sends, arrivals = copies
    for cp in arrivals:
        cp.wait_recv()
    for cp in sends:
        cp.wait_send()
    for cp in local:
        cp.wait()


def _exchange(arrs, modes, name):
    n = len(arrs)

    def body(*refs):
        copies = _exchange_copies(refs[:n], refs[n:2 * n], modes, refs[2 * n:])
        _exchange_start(copies)
        _exchange_wait(copies)

    outs = pl.pallas_call(
        body, name=name, out_shape=tuple(_exchange_shapes(arrs, modes)),
        in_specs=[ANY_SPEC] * n, out_specs=tuple([ANY_SPEC] * n),
        scratch_shapes=_exchange_sems(n),
    )(*arrs)
    return list(outs)


def _gather_two_level(arrs, name):
    n = len(arrs)

    def body(*refs):
        ins, outs = refs[:n], refs[n:2 * n]
        send_sems, recv_sems, local_sems = refs[2 * n:]
        x, y, c = lax.axis_index("x"), lax.axis_index("y"), lax.axis_index("c")
        me, sibling = (x, y, c), (x, y, 1 - c)
        chips = [(1 - x, y), (x, 1 - y), (1 - x, 1 - y)]

        def copy(a, k, block, to, src=None):
            slot = 4 * block[0] + 2 * block[1] + block[2]
            return pltpu.make_async_remote_copy(
                src_ref=outs[a].at[slot] if src is None else src, dst_ref=outs[a].at[slot],
                send_sem=send_sems.at[7 * a + k], recv_sem=recv_sems.at[7 * a + k],
                device_id=to, device_id_type=pl.DeviceIdType.MESH)

        mine = [pltpu.make_async_copy(ins[a], outs[a].at[4 * x + 2 * y + c], local_sems.at[a])
                for a in range(n)]
        first = []
        for a in range(n):
            first.append(copy(a, 0, me, sibling, src=ins[a]))
            first += [copy(a, 1 + j, me, (*chip, c), src=ins[a]) for j, chip in enumerate(chips)]
        for cp in mine + first:
            cp.start()
        passed = []
        for j, chip in enumerate(chips):
            for a in range(n):
                copy(a, 1 + j, (*chip, c), me).wait_recv()
                passed.append(copy(a, 4 + j, (*chip, c), sibling))
                passed[-1].start()
        for a in range(n):
            copy(a, 0, sibling, me).wait_recv()
            for j, chip in enumerate(chips):
                copy(a, 4 + j, (*chip, 1 - c), me).wait_recv()
        for cp in first + passed:
            cp.wait_send()
        for cp in mine:
            cp.wait()

    outs = pl.pallas_call(
        body, name=name, out_shape=tuple(_exchange_shapes(arrs, ["ag"] * n)),
        in_specs=[ANY_SPEC] * n, out_specs=tuple([ANY_SPEC] * n),
        scratch_shapes=_exchange_sems(n),
    )(*arrs)
    return list(outs)


SHARD_COLS = N_CHUNKS * ATT_WIDTH // N_DEV


def _weights_to_internal(shards):
    blocks = []
    for shard in shards:
        full = jnp.transpose(shard, (1, 0, 2)).reshape(shard.shape[1], N_CHUNKS, ATT_WIDTH)
        blocks.append(full[:, jnp.array(CHUNK_PERM)].reshape(shard.shape[1], N_CHUNKS * ATT_WIDTH))
    return blocks[0] if len(blocks) == 1 else jnp.concatenate(blocks, axis=0)


def _grads_to_shards(parts):
    half = D_MODEL // 2
    inv_perm = jnp.array([CHUNK_PERM.index(k) for k in range(N_CHUNKS)])
    halves = []
    for h in range(2):
        rows = jnp.concatenate([p[h * half:(h + 1) * half] for p in parts], axis=1)
        rows = rows.reshape(half, N_CHUNKS, ATT_WIDTH)[:, inv_perm].reshape(half, N_DEV, SHARD_COLS)
        halves.append(jnp.transpose(rows, (1, 0, 2)))
    return halves


def _call(body, args, *, name, grid, in_specs, out_specs, out_shape, scratch_shapes=(), aliases=None,
          hosted=None):
    sem = ("arbitrary",) * len(grid)
    if hosted is None:
        outs = pl.pallas_call(
            body, name=name, grid=grid, in_specs=in_specs, out_specs=tuple(out_specs),
            out_shape=tuple(out_shape), scratch_shapes=list(scratch_shapes),
            input_output_aliases=aliases or {}, compiler_params=_params(sem))(*args)
        return list(outs), []
    x_arrs, modes = hosted
    n_in, n_out, n_scr, nx = len(args), len(out_shape), len(scratch_shapes), len(x_arrs)

    def wrapped(*refs):
        ins, x_ins = refs[:n_in], refs[n_in:n_in + nx]
        outs = refs[n_in + nx:n_in + nx + n_out]
        x_outs = refs[n_in + nx + n_out:n_in + 2 * nx + n_out]
        scr = refs[n_in + 2 * nx + n_out:n_in + 2 * nx + n_out + n_scr]
        sems = refs[n_in + 2 * nx + n_out + n_scr:]
        first = pl.program_id(0) == 0
        last = pl.program_id(0) == grid[0] - 1
        for axis in range(1, len(grid)):
            first = jnp.logical_and(first, pl.program_id(axis) == 0)
            last = jnp.logical_and(last, pl.program_id(axis) == grid[axis] - 1)

        @pl.when(first)
        def _():
            _exchange_start(_exchange_copies(x_ins, x_outs, modes, sems))

        body(*ins, *outs, *scr)

        @pl.when(last)
        def _():
            _exchange_wait(_exchange_copies(x_ins, x_outs, modes, sems))

    outs = pl.pallas_call(
        wrapped, name=name, grid=grid, in_specs=list(in_specs) + [ANY_SPEC] * nx,
        out_specs=tuple(out_specs) + tuple([ANY_SPEC] * nx),
        out_shape=tuple(out_shape) + tuple(_exchange_shapes(x_arrs, modes)),
        scratch_shapes=list(scratch_shapes) + _exchange_sems(nx),
        input_output_aliases=aliases or {}, compiler_params=_params(sem))(*args, *x_arrs)
    return list(outs[:n_out]), list(outs[n_out:])


def _mod_fwd(c_pad, w_mod):
    def body(c_ref, w_ref, o_ref):
        sc = _silu(c_ref[...]).astype(BF16)
        for layer in range(2):
            o_ref[layer] = _dot(sc, w_ref[layer].astype(BF16))

    return pl.pallas_call(
        body, name="mod_fwd", out_shape=jax.ShapeDtypeStruct((2, 16, w_mod.shape[2]), F32),
        compiler_params=_params())(c_pad, w_mod)


def _mod_wgrad(c_t, dmod):
    n_cols = dmod.shape[2]

    def body(c_ref, d_ref, o_ref):
        sc = _silu(c_ref[...]).astype(BF16).astype(F32)
        for layer in range(2):
            dm = d_ref[layer].astype(BF16).astype(F32)
            acc = sc[:, 0:1] * dm[0:1, :]
            for b in range(1, N_DEV):
                acc = acc + sc[:, b:b + 1] * dm[b:b + 1, :]
            o_ref[layer] = acc

    return pl.pallas_call(
        body, name="mod_wgrad", out_shape=jax.ShapeDtypeStruct((2, D_MODEL, n_cols), F32),
        compiler_params=_params())(c_t, dmod)


GROUP_COLS = 3 * ATT_WIDTH
PROJ_TM = 256


def _norm_proj_fwd(x, vec, w_int, hosted=None):
    S = x.shape[0]
    tm = PROJ_TM
    perms = [_perm_matrix(r, tm) for r in DILATIONS[1:]]

    def body(x_ref, v_ref, w_ref, p1_ref, p2_ref, ht0_ref, ht1_ref, ht2_ref, q0_ref, q1_ref, q2_ref,
             rest_ref):
        xv = x_ref[...]
        rstd = lax.rsqrt(jnp.mean(xv * xv, axis=-1, keepdims=True) + NORM_EPS)
        hf = ((xv * rstd) * v_ref[0:1, :]) * (1.0 + v_ref[1:2, :]) + v_ref[2:3, :]
        h = hf.astype(BF16)
        for g, (q_ref, ht_ref, p_ref) in enumerate(((q0_ref, ht0_ref, None), (q1_ref, ht1_ref, p1_ref),
                                                    (q2_ref, ht2_ref, p2_ref))):
            rows_f = hf if p_ref is None else _dot(p_ref[...], h)
            ht_ref[...] = rows_f.T.astype(BF16)
            _store_residues(q_ref, _dot(rows_f.astype(BF16),
                                        w_ref[:, g * GROUP_COLS:(g + 1) * GROUP_COLS]).astype(BF16))
        rest_ref[...] = _dot(h, w_ref[:, 3 * GROUP_COLS:])

    n_rest = QKV_CHUNKS * ATT_WIDTH
    whole = pl.BlockSpec(memory_space=pltpu.VMEM)
    ht_spec = pl.BlockSpec((D_MODEL, tm), lambda i: (0, i))
    ht_shape = jax.ShapeDtypeStruct((D_MODEL, S), BF16)
    return _call(
        body, (x, vec, w_int, *perms), name="norm_proj_fwd", grid=(S // tm,),
        in_specs=[pl.BlockSpec((tm, D_MODEL), lambda i: (i, 0)),
                  pl.BlockSpec((8, D_MODEL), lambda i: (0, 0)), whole, whole, whole],
        out_specs=(ht_spec, ht_spec, ht_spec,
                   *[pl.BlockSpec((r, tm // r, GROUP_COLS), lambda i: (0, i, 0)) for r in DILATIONS],
                   pl.BlockSpec((tm, n_rest), lambda i: (i, 0))),
        out_shape=(ht_shape, ht_shape, ht_shape,
                   *[jax.ShapeDtypeStruct((r, S // r, GROUP_COLS), BF16) for r in DILATIONS],
                   jax.ShapeDtypeStruct((S, n_rest), F32)),
        hosted=hosted)


def _proj_dgrad_norm_bwd(dqkv, drest, w_int, x, dxn, vec, hosted=None):
    S = x.shape[0]
    tm = PROJ_TM
    perms = [_perm_matrix(r, tm, transpose=True) for r in DILATIONS[1:]]

    def body(d0_ref, d1_ref, d2_ref, dr_ref, w_ref, p1_ref, p2_ref, x_ref, dxn_ref, v_ref, dx_ref, acc_ref):
        i = pl.program_id(0)

        @pl.when(i == 0)
        def _():
            acc_ref[...] = jnp.zeros_like(acc_ref)

        def wcols(k):
            return w_ref[:, k * ATT_WIDTH:(k + 1) * ATT_WIDTH]

        dhv = None
        for g, (d_ref, p_ref) in enumerate(((d0_ref, None), (d1_ref, p1_ref), (d2_ref, p2_ref))):
            for t in range(3):
                d = _rows_of_residues(d_ref, (t,))
                if p_ref is not None:
                    d = _dot(p_ref[...], d).astype(BF16)
                part = _dot_nt(d, wcols(3 * g + t))
                dhv = part if dhv is None else dhv + part
        for t in range(7):
            dhv = dhv + _dot_nt(dr_ref[t], wcols(11 + t))
        for t in range(2):
            dhv = dhv + _dot_nt(dr_ref[8 + t], wcols(9 + t))

        xv = x_ref[...]
        g = v_ref[0:1, :]
        sc1 = 1.0 + v_ref[1:2, :]
        rstd = lax.rsqrt(jnp.mean(xv * xv, axis=-1, keepdims=True) + NORM_EPS)
        xhat = xv * rstd
        acc_ref[0:1, :] += jnp.sum(dhv, axis=0, keepdims=True)
        acc_ref[1:2, :] += jnp.sum(dhv * (xhat * g), axis=0, keepdims=True)
        acc_ref[2:3, :] += jnp.sum(dhv * xhat * sc1, axis=0, keepdims=True)
        dxhat = dhv * (g * sc1)
        dx = rstd * (dxhat - xhat * jnp.mean(dxhat * xhat, axis=-1, keepdims=True))
        dx_ref[...] = dx + dxn_ref[...]

    row = pl.BlockSpec((tm, D_MODEL), lambda i: (i, 0))
    vec_spec = pl.BlockSpec((8, D_MODEL), lambda i: (0, 0))
    whole = pl.BlockSpec(memory_space=pltpu.VMEM)
    return _call(
        body, (*dqkv, drest, w_int, *perms, x, dxn, vec), name="proj_dgrad_norm_bwd", grid=(S // tm,),
        in_specs=[*[pl.BlockSpec((3, r, tm // r, ATT_WIDTH), lambda i: (0, 0, i, 0)) for r in DILATIONS],
                  pl.BlockSpec((DREST_CHUNKS, tm, ATT_WIDTH), lambda i: (0, i, 0)),
                  whole, whole, whole, row, row, vec_spec],
        out_specs=(row, vec_spec),
        out_shape=(jax.ShapeDtypeStruct((S, D_MODEL), F32), jax.ShapeDtypeStruct((8, D_MODEL), F32)),
        hosted=hosted)


def _proj_wgrad_part(h_t, d, n_chunks, chunk_block, name):
    S = h_t.shape[1]
    ts = 512
    n_steps = S // ts

    def body(h_ref, d_ref, o_ref, acc_ref):
        i = pl.program_id(0)

        @pl.when(i == 0)
        def _():
            acc_ref[...] = jnp.zeros_like(acc_ref)

        ht = h_ref[...]
        for t in range(n_chunks):
            acc_ref[:, t * ATT_WIDTH:(t + 1) * ATT_WIDTH] += _dot(ht, d_ref[t])

        @pl.when(i == n_steps - 1)
        def _():
            o_ref[...] = acc_ref[...].astype(BF16)

    return pl.pallas_call(
        body, name=name, grid=(n_steps,),
        in_specs=[pl.BlockSpec((D_MODEL, ts), lambda i: (0, i)),
                  pl.BlockSpec((n_chunks, ts, ATT_WIDTH), lambda i: (chunk_block, i, 0))],
        out_specs=pl.BlockSpec((D_MODEL, n_chunks * ATT_WIDTH), lambda i: (0, 0)),
        out_shape=jax.ShapeDtypeStruct((D_MODEL, n_chunks * ATT_WIDTH), BF16),
        scratch_shapes=[pltpu.VMEM((D_MODEL, n_chunks * ATT_WIDTH), F32)],
        compiler_params=_params(("arbitrary",)))(h_t, d)


def _proj_wgrad_group(h_t, d, r, name):
    S = h_t.shape[1]
    ts = 512
    n_steps = S // ts
    n = PROJ_TM // r

    def body(h_ref, d_ref, o_ref, acc_ref):
        i = pl.program_id(0)

        @pl.when(i == 0)
        def _():
            acc_ref[...] = jnp.zeros_like(acc_ref)

        ht = h_ref[...]
        for t in range(3):
            rows = jnp.concatenate([d_ref[t, rho, s * n:(s + 1) * n] for s in range(ts // PROJ_TM)
                                    for rho in range(r)], axis=0) if r > 1 else d_ref[t, 0]
            acc_ref[:, t * ATT_WIDTH:(t + 1) * ATT_WIDTH] += _dot(ht, rows)

        @pl.when(i == n_steps - 1)
        def _():
            o_ref[...] = acc_ref[...].astype(BF16)

    return pl.pallas_call(
        body, name=name, grid=(n_steps,),
        in_specs=[pl.BlockSpec((D_MODEL, ts), lambda i: (0, i)),
                  pl.BlockSpec((3, r, ts // r, ATT_WIDTH), lambda i: (0, 0, i, 0))],
        out_specs=pl.BlockSpec((D_MODEL, GROUP_COLS), lambda i: (0, 0)),
        out_shape=jax.ShapeDtypeStruct((D_MODEL, GROUP_COLS), BF16),
        scratch_shapes=[pltpu.VMEM((D_MODEL, GROUP_COLS), F32)],
        compiler_params=_params(("arbitrary",)))(h_t, d)


def _wgrad_tn(a, b, name):
    S, M = a.shape
    N = b.shape[1]
    ts = 1024

    def body(a_ref, b_ref, o_ref):
        i = pl.program_id(0)
        part = _dot_tn(a_ref[...], b_ref[...])

        @pl.when(i == 0)
        def _():
            o_ref[...] = part

        @pl.when(i > 0)
        def _():
            o_ref[...] += part

    return pl.pallas_call(
        body, name=name, grid=(S // ts,),
        in_specs=[pl.BlockSpec((ts, M), lambda i: (i, 0)), pl.BlockSpec((ts, N), lambda i: (i, 0))],
        out_specs=pl.BlockSpec((M, N), lambda i: (0, 0)),
        out_shape=jax.ShapeDtypeStruct((M, N), F32),
        compiler_params=_params(("arbitrary",)))(a, b)


def _attn_fwd(qkv, g, r):
    L = qkv.shape[1]
    nb2 = L // (2 * BAND)
    scale = HEAD_DIM ** -0.5

    def body(q_ref, kp_ref, kc_ref, vp_ref, vc_ref, o_ref, st_ref):
        m_step = pl.program_id(1)
        ii = lax.broadcasted_iota(jnp.int32, (2 * BAND, 3 * BAND), 0)
        kk = lax.broadcasted_iota(jnp.int32, (2 * BAND, 3 * BAND), 1)
        mask = jnp.logical_and(jnp.logical_and(kk >= ii, kk <= ii + BAND),
                               jnp.logical_or(kk >= BAND, m_step > 0))
        lane = lax.broadcasted_iota(jnp.int32, (2 * BAND, 128), 1)
        stats = jnp.zeros((2 * BAND, 128), F32)
        for h in range(HEADS):
            sl = slice(h * HEAD_DIM, (h + 1) * HEAD_DIM)
            k_cat = jnp.concatenate([kp_ref[:, sl], kc_ref[:, sl]], axis=0)
            v_cat = jnp.concatenate([vp_ref[:, sl], vc_ref[:, sl]], axis=0)
            s = jnp.where(mask, _dot_nt(q_ref[:, sl], k_cat) * scale, NEG_INF)
            m = jnp.max(s, axis=1, keepdims=True)
            p = jnp.exp(s - m)
            l = jnp.sum(p, axis=1, keepdims=True)
            o_ref[:, sl] = _dot(p.astype(BF16), v_cat) / l
            stats = jnp.where(lane == h, m, stats)
            stats = jnp.where(lane == HEADS + h, l, stats)
        st_ref[...] = stats

    two = (None, 2 * BAND, ATT_WIDTH)
    one = (None, BAND, ATT_WIDTH)

    def prev(m):
        return jnp.maximum(2 * m - 1, 0)

    return pl.pallas_call(
        body, name=f"attn_fwd_g{g}", grid=(r, nb2),
        in_specs=[
            pl.BlockSpec(two, lambda rho, m: (rho, m, 0)),
            pl.BlockSpec(one, lambda rho, m: (rho, prev(m), 1)),
            pl.BlockSpec(two, lambda rho, m: (rho, m, 1)),
            pl.BlockSpec(one, lambda rho, m: (rho, prev(m), 2)),
            pl.BlockSpec(two, lambda rho, m: (rho, m, 2)),
        ],
        out_specs=(pl.BlockSpec(two, lambda rho, m: (rho, m, 0)),
                   pl.BlockSpec((None, 2 * BAND, 128), lambda rho, m: (rho, m, 0))),
        out_shape=(jax.ShapeDtypeStruct((r, L, ATT_WIDTH), F32),
                   jax.ShapeDtypeStruct((r, L, 128), F32)),
        compiler_params=_params(("parallel", "parallel")))(qkv, qkv, qkv, qkv, qkv)


def _attn_combine(outs, stats, rest):
    S = rest.shape[0]
    tm = 512
    gatt_blk = 8
    back = [_perm_matrix(r, tm, transpose=True) for r in DILATIONS[1:]]
    forth = [_perm_matrix(r, tm) for r in DILATIONS[1:]]

    def body(o0_ref, o1_ref, o2_ref, s0_ref, s1_ref, s2_ref, g_ref, b1_ref, b2_ref, f1_ref, f2_ref,
             o_ref, a_ref, l0_ref, l1_ref, l2_ref):
        outs_nat = [o0_ref[0]] + [_permute_f32(b_ref[...], _rows_of_residues(o_g), 2)
                                  for o_g, b_ref in ((o1_ref, b1_ref), (o2_ref, b2_ref))]
        st = [s0_ref[0]] + [_permute_f32(b_ref[...], _rows_of_residues(s_g), 3)
                            for s_g, b_ref in ((s1_ref, b1_ref), (s2_ref, b2_ref))]
        lane = lax.broadcasted_iota(jnp.int32, (tm, 128), 1)
        lse_out = jnp.zeros((tm, 128), F32)
        for h in range(HEADS):
            sl = slice(h * HEAD_DIM, (h + 1) * HEAD_DIM)
            ms = [s[:, h:h + 1] for s in st]
            ls = [s[:, HEADS + h:HEADS + h + 1] for s in st]
            m_all = jnp.maximum(jnp.maximum(ms[0], ms[1]), ms[2])
            ws = [l * jnp.exp(m - m_all) for m, l in zip(ms, ls)]
            den = ws[0] + ws[1] + ws[2]
            o = (ws[0] * outs_nat[0][:, sl] + ws[1] * outs_nat[1][:, sl] + ws[2] * outs_nat[2][:, sl]) / den
            o_ref[:, sl] = o
            a_ref[:, sl] = (o * _silu(g_ref[:, sl])).astype(BF16)
            lse_out = jnp.where(lane == h, m_all + jnp.log(den), lse_out)
        l0_ref[0] = lse_out
        for l_ref, f_ref in ((l1_ref, f1_ref), (l2_ref, f2_ref)):
            _store_residues(l_ref, _permute_f32(f_ref[...], lse_out, 3))

    o_spec = pl.BlockSpec((tm, ATT_WIDTH), lambda i: (i, 0))
    whole = pl.BlockSpec(memory_space=pltpu.VMEM)

    def res_spec(r, width):
        return pl.BlockSpec((r, tm // r, width), lambda i: (0, i, 0))

    return pl.pallas_call(
        body, name="attn_combine", grid=(S // tm,),
        in_specs=[*[res_spec(r, ATT_WIDTH) for r in DILATIONS], *[res_spec(r, 128) for r in DILATIONS],
                  pl.BlockSpec((tm, ATT_WIDTH), lambda i: (i, gatt_blk)), whole, whole, whole, whole],
        out_specs=(o_spec, o_spec, *[res_spec(r, 128) for r in DILATIONS]),
        out_shape=(jax.ShapeDtypeStruct((S, ATT_WIDTH), F32), jax.ShapeDtypeStruct((S, ATT_WIDTH), BF16),
                   *[jax.ShapeDtypeStruct((r, S // r, 128), F32) for r in DILATIONS]),
        compiler_params=_params(("parallel",)))(*outs, *stats, rest, *back, *forth)


def _attn_bwd(qkv, do, lse, dvec, g, r, hosted=None):
    L = qkv.shape[1]
    nb = L // BAND
    nb2 = nb // 2
    scale = HEAD_DIM ** -0.5

    def body(qc_ref, qn_ref, k_ref, v_ref, doc_ref, don_ref, lc_ref, ln_ref, dc_ref, dn_ref,
             out_ref, carry_ref):
        j = pl.program_id(1)

        @pl.when(j == 0)
        def _():
            carry_ref[...] = jnp.zeros_like(carry_ref)

        ii = lax.broadcasted_iota(jnp.int32, (3 * BAND, 2 * BAND), 0)
        kk = lax.broadcasted_iota(jnp.int32, (3 * BAND, 2 * BAND), 1)
        mask = jnp.logical_and(jnp.logical_and(kk <= ii, kk >= ii - BAND),
                               jnp.logical_or(ii < 2 * BAND, j < nb2 - 1))
        lse3 = jnp.concatenate([lc_ref[...], ln_ref[...]], axis=0)
        dvec3 = jnp.concatenate([dc_ref[...], dn_ref[...]], axis=0)
        for h in range(HEADS):
            sl = slice(h * HEAD_DIM, (h + 1) * HEAD_DIM)
            k = k_ref[:, sl]
            v = v_ref[:, sl]
            q = jnp.concatenate([qc_ref[:, sl], qn_ref[:, sl]], axis=0)
            do = jnp.concatenate([doc_ref[:, sl], don_ref[:, sl]], axis=0)
            p = jnp.where(mask, jnp.exp(_dot_nt(q, k) * scale - lse3[:, h:h + 1]), 0.0)
            ds = (p * (_dot_nt(do, v) - dvec3[:, h:h + 1])).astype(BF16)
            dq3 = _dot(ds, k) * scale
            out_ref[0, 0:BAND, sl] = (carry_ref[:, sl] + dq3[:BAND]).astype(BF16)
            out_ref[0, BAND:, sl] = dq3[BAND:2 * BAND].astype(BF16)
            out_ref[1, :, sl] = (_dot_tn(ds, q) * scale).astype(BF16)
            out_ref[2, :, sl] = _dot_tn(p.astype(BF16), do).astype(BF16)
            carry_ref[:, sl] = dq3[2 * BAND:]

    two = (None, 2 * BAND, ATT_WIDTH)
    one = (None, BAND, ATT_WIDTH)
    stwo = (None, 2 * BAND, 128)
    sone = (None, BAND, 128)

    def nxt(j):
        return jnp.minimum(2 * j + 2, nb - 1)

    (out,), got = _call(
        body, (qkv, qkv, qkv, qkv, do, do, lse, lse, dvec, dvec), name=f"attn_bwd_g{g}", grid=(r, nb2),
        in_specs=[
            pl.BlockSpec(two, lambda rho, j: (rho, j, 0)),
            pl.BlockSpec(one, lambda rho, j: (rho, nxt(j), 0)),
            pl.BlockSpec(two, lambda rho, j: (rho, j, 1)),
            pl.BlockSpec(two, lambda rho, j: (rho, j, 2)),
            pl.BlockSpec(two, lambda rho, j: (rho, j, 0)),
            pl.BlockSpec(one, lambda rho, j: (rho, nxt(j), 0)),
            pl.BlockSpec(stwo, lambda rho, j: (rho, j, 0)),
            pl.BlockSpec(sone, lambda rho, j: (rho, nxt(j), 0)),
            pl.BlockSpec(stwo, lambda rho, j: (rho, j, 0)),
            pl.BlockSpec(sone, lambda rho, j: (rho, nxt(j), 0)),
        ],
        out_specs=(pl.BlockSpec((3, None, 2 * BAND, ATT_WIDTH), lambda rho, j: (0, rho, j, 0)),),
        out_shape=(jax.ShapeDtypeStruct((3, r, L, ATT_WIDTH), BF16),),
        scratch_shapes=[pltpu.VMEM((BAND, ATT_WIDTH), F32)],
        hosted=hosted)
    return out, got


LRU_T = 256


def _linear_scan(a, b, carry, reverse):
    T = a.shape[0]
    row8 = lax.broadcasted_iota(jnp.int32, a.shape, 0) & 7
    for s in (1, 2, 4):
        keep = (row8 < 8 - s) if reverse else (row8 >= s)
        shift = T - s if reverse else s
        b_sh = jnp.where(keep, pltpu.roll(b, shift, 0), 0.0)
        a_sh = jnp.where(keep, pltpu.roll(a, shift, 0), 1.0)
        b = a * b_sh + b
        a = a * a_sh
    tiles = [None] * (T // 8)
    order = range(T // 8 - 1, -1, -1) if reverse else range(T // 8)
    for k in order:
        y = b[8 * k:8 * k + 8] + a[8 * k:8 * k + 8] * carry
        tiles[k] = y
        carry = y[0:1] if reverse else y[7:8]
    return jnp.concatenate(tiles, axis=0), carry


def _gate_matmuls(ucb, w_ref, bias):
    parts = [_dot(ucb[:, j * 128:(j + 1) * 128], w_ref[j]) for j in range(8)]
    return jnp.concatenate(parts, axis=1) + bias


def _conv_fwd(u, u_prev8, cw_ref):
    T = u.shape[0]
    ue = jnp.concatenate([u_prev8, u], axis=0)
    uc = cw_ref[4:5, :] + cw_ref[0:1, :] * u
    for j in range(1, CONV_WIDTH):
        uc = uc + cw_ref[j:j + 1, :] * pltpu.roll(ue, j, 0)[8:8 + T]
    return uc


def _lru_fwd(rest, cw, wr2, wi2, hosted=None):
    S = rest.shape[0]
    T = LRU_T

    def body(u_ref, cw_ref, wr_ref, wi_ref, h_ref, a_ref, uc_ref, r_ref, ig_ref, mult_ref,
             ucar_ref, hcar_ref):
        c = pl.program_id(0)

        @pl.when(c == 0)
        def _():
            ucar_ref[...] = jnp.zeros_like(ucar_ref)
            hcar_ref[...] = jnp.zeros_like(hcar_ref)

        u = u_ref[...]
        uc = _conv_fwd(u, ucar_ref[...], cw_ref)
        ucar_ref[...] = u[T - 8:, :]
        ucb = uc.astype(BF16)
        r = jax.nn.sigmoid(_gate_matmuls(ucb, wr_ref, cw_ref[5:6, :]))
        ig = jax.nn.sigmoid(_gate_matmuls(ucb, wi_ref, cw_ref[6:7, :]))
        log_a = -LRU_C * r * _softplus_neg(cw_ref[7:8, :])
        a = jnp.exp(log_a)
        mult = jnp.sqrt(_neg_expm1(2.0 * log_a))
        b = mult * (ig * uc)
        a_ref[...] = a
        uc_ref[...] = uc
        r_ref[...] = r
        ig_ref[...] = ig
        mult_ref[...] = mult
        h, last = _linear_scan(a, b, hcar_ref[0:1, :], reverse=False)
        h_ref[...] = h
        hcar_ref[...] = jnp.broadcast_to(last, (8, D_MODEL))

    row_spec = pl.BlockSpec((T, D_MODEL), lambda c: (c, 0))
    row_shape = jax.ShapeDtypeStruct((S, D_MODEL), F32)
    return _call(
        body, (rest, cw, wr2, wi2), name="lru_fwd", grid=(S // T,),
        in_specs=[row_spec, pl.BlockSpec((8, D_MODEL), lambda c: (0, 0)),
                  pl.BlockSpec((8, 128, 128), lambda c: (0, 0, 0)),
                  pl.BlockSpec((8, 128, 128), lambda c: (0, 0, 0))],
        out_specs=(row_spec,) * 6, out_shape=(row_shape,) * 6,
        scratch_shapes=[pltpu.VMEM((8, D_MODEL), F32), pltpu.VMEM((8, D_MODEL), F32)],
        hosted=hosted)


def _lru_bwd(dhl, hl, a_all, gates, rest, cw, wr2, wi2, dproj, hosted=None):
    S = rest.shape[0]
    T = LRU_T
    nc = S // T

    def body(dh_ref, h_ref, hp_ref, a_ref, uc_ref, r_ref, ig_ref, mult_ref, u_ref, up_ref, cw_ref, wr_ref,
             wi_ref, _alias, du_ref, acc_ref, gwr_ref, gwi_ref, gcar_ref, acar_ref, dcar_ref):
        step = pl.program_id(0)
        c = nc - 1 - step

        @pl.when(step == 0)
        def _():
            acc_ref[...] = jnp.zeros_like(acc_ref)
            gwr_ref[...] = jnp.zeros_like(gwr_ref)
            gwi_ref[...] = jnp.zeros_like(gwi_ref)
            gcar_ref[...] = jnp.zeros_like(gcar_ref)
            acar_ref[...] = jnp.zeros_like(acar_ref)
            dcar_ref[...] = jnp.zeros_like(dcar_ref)

        row = lax.broadcasted_iota(jnp.int32, (T, D_MODEL), 0)
        u = u_ref[...]
        u_prev = jnp.where(c > 0, up_ref[...], 0.0)
        h_prev8 = jnp.where(c > 0, hp_ref[...], 0.0)
        a = a_ref[...]
        h = h_ref[...]
        uc, r, ig, mult = uc_ref[...], r_ref[...], ig_ref[...], mult_ref[...]
        ucb = uc.astype(BF16)
        sp = _softplus_neg(cw_ref[7:8, :])
        a_next = jnp.where(row < T - 1, pltpu.roll(a, T - 1, 0), acar_ref[0:1, :])
        G, first = _linear_scan(a_next, dh_ref[...], gcar_ref[0:1, :], reverse=True)
        gcar_ref[...] = jnp.broadcast_to(first, (8, D_MODEL))
        acar_ref[...] = jnp.broadcast_to(a[0:1, :], (8, D_MODEL))
        he = jnp.concatenate([h_prev8, h], axis=0)
        h_before = pltpu.roll(he, 1, 0)[8:8 + T]
        d_a = G * h_before
        d_mult = G * (ig * uc)
        d_ig = G * (mult * uc)
        duc = G * (mult * ig)
        d_log_a = d_a * a - d_mult * (a * a) / mult
        d_r = d_log_a * (-LRU_C * sp)
        d_sp = jnp.sum(d_log_a * (-LRU_C * r), axis=0, keepdims=True)
        dpre_r = d_r * r * (1.0 - r)
        dpre_i = d_ig * ig * (1.0 - ig)
        dprb = dpre_r.astype(BF16)
        dpib = dpre_i.astype(BF16)
        back = []
        for j in range(8):
            sl = slice(j * 128, (j + 1) * 128)
            back.append(_dot_nt(dprb[:, sl], wr_ref[j]) + _dot_nt(dpib[:, sl], wi_ref[j]))
            gwr_ref[j] += _dot_tn(ucb[:, sl], dprb[:, sl])
            gwi_ref[j] += _dot_tn(ucb[:, sl], dpib[:, sl])
        duc = duc + jnp.concatenate(back, axis=1)
        de = jnp.concatenate([duc, dcar_ref[...]], axis=0)
        du = cw_ref[0:1, :] * duc
        ue = jnp.concatenate([u_prev, u], axis=0)
        acc_ref[0:1, :] += jnp.sum(duc * u, axis=0, keepdims=True)
        for j in range(1, CONV_WIDTH):
            du = du + cw_ref[j:j + 1, :] * pltpu.roll(de, T + 8 - j, 0)[0:T]
            acc_ref[j:j + 1, :] += jnp.sum(duc * pltpu.roll(ue, j, 0)[8:8 + T], axis=0, keepdims=True)
        dcar_ref[...] = duc[0:8, :]
        acc_ref[4:5, :] += jnp.sum(duc, axis=0, keepdims=True)
        acc_ref[5:6, :] += jnp.sum(dpre_r, axis=0, keepdims=True)
        acc_ref[6:7, :] += jnp.sum(dpre_i, axis=0, keepdims=True)
        acc_ref[7:8, :] += d_sp
        du_ref[0] = du[:, 0:ATT_WIDTH].astype(BF16)
        du_ref[1] = du[:, ATT_WIDTH:].astype(BF16)

    def rev(step):
        return nc - 1 - step

    def prev8(step):
        return jnp.maximum(rev(step) * (T // 8) - 1, 0)

    row_spec = pl.BlockSpec((T, D_MODEL), lambda s: (rev(s), 0))
    prev_spec = pl.BlockSpec((8, D_MODEL), lambda s: (prev8(s), 0))
    vec_spec = pl.BlockSpec((8, D_MODEL), lambda s: (0, 0))
    w_spec = pl.BlockSpec((8, 128, 128), lambda s: (0, 0, 0))
    return _call(
        body, (dhl, hl, hl, a_all, *gates, rest, rest, cw, wr2, wi2, dproj), name="lru_bwd", grid=(nc,),
        in_specs=[row_spec, row_spec, prev_spec, row_spec, row_spec, row_spec, row_spec, row_spec,
                  row_spec, prev_spec, vec_spec, w_spec, w_spec, ANY_SPEC],
        out_specs=(pl.BlockSpec((2, T, ATT_WIDTH), lambda s: (4, rev(s), 0)), vec_spec, w_spec, w_spec),
        out_shape=(jax.ShapeDtypeStruct(dproj.shape, BF16), jax.ShapeDtypeStruct((8, D_MODEL), F32),
                   jax.ShapeDtypeStruct((8, 128, 128), F32), jax.ShapeDtypeStruct((8, 128, 128), F32)),
        scratch_shapes=[pltpu.VMEM((8, D_MODEL), F32), pltpu.VMEM((8, D_MODEL), F32),
                        pltpu.VMEM((8, D_MODEL), F32)],
        aliases={13: 0}, hosted=hosted)


OUT_TM = 256


def _out_fwd(a_gated, hl, rest, x, vec, w_pa, w_pb, w_o, hosted=None):
    S = x.shape[0]
    tm = OUT_TM

    def body(ag_ref, hl_ref, gl_ref, ma_ref, mb_ref, x_ref, v_ref, wpa_ref, wpb_ref, wo_ref,
             xn_ref, mix_ref, ya_ref, yb_ref, bg_ref):
        bg = (hl_ref[...] * _silu(gl_ref[...])).astype(BF16)
        ya = _dot(ag_ref[...], wpa_ref[...])
        yb = _dot(bg, wpb_ref[...])
        mix = (jax.nn.sigmoid(ma_ref[...]) * ya + jax.nn.sigmoid(mb_ref[...]) * yb).astype(BF16)
        out = _dot(mix, wo_ref[...])
        rstd = lax.rsqrt(jnp.mean(out * out, axis=-1, keepdims=True) + NORM_EPS)
        xn_ref[...] = x_ref[...] + v_ref[1:2, :] * ((out * rstd) * v_ref[0:1, :])
        mix_ref[...] = mix
        ya_ref[...] = ya.astype(BF16)
        yb_ref[...] = yb.astype(BF16)
        bg_ref[...] = bg

    def col(j):
        return pl.BlockSpec((tm, D_MODEL), lambda i: (i, j))

    def whole(shape):
        return pl.BlockSpec(shape, lambda i: (0, 0))

    row = col(0)
    bf = jax.ShapeDtypeStruct((S, D_MODEL), BF16)
    return _call(
        body, (a_gated, hl, rest, rest, rest, x, vec, w_pa, w_pb, w_o), name="out_fwd", grid=(S // tm,),
        in_specs=[pl.BlockSpec((tm, ATT_WIDTH), lambda i: (i, 0)), row, col(1), col(2), col(3), row,
                  whole((8, D_MODEL)), whole((ATT_WIDTH, D_MODEL)), whole((D_MODEL, D_MODEL)),
                  whole((D_MODEL, D_MODEL))],
        out_specs=(row, row, row, row, row),
        out_shape=(jax.ShapeDtypeStruct((S, D_MODEL), F32), bf, bf, bf, bf),
        hosted=hosted)


def _out_bwd(dxn, mix, ya, yb, hl, rest, o_att, vec, w_pa, w_pb, w_o, hosted=None):
    S = dxn.shape[0]
    tm = OUT_TM
    forth = [_perm_matrix(r, tm) for r in DILATIONS[1:]]

    def body(dxn_ref, mix_ref, ya_ref, yb_ref, hl_ref, gl_ref, ma_ref, mb_ref, ga_ref, o_ref, v_ref,
             wpa_ref, wpb_ref, wo_ref, f1_ref, f2_ref,
             dout_ref, dya_ref, dyb_ref, dhl_ref, dp_ref, acc_ref,
             do0_ref, do1_ref, do2_ref, dv0_ref, dv1_ref, dv2_ref):
        i = pl.program_id(0)

        @pl.when(i == 0)
        def _():
            acc_ref[...] = jnp.zeros_like(acc_ref)

        g_post = v_ref[0:1, :]
        gate = v_ref[1:2, :]
        dxn_v = dxn_ref[...]
        out = _dot(mix_ref[...], wo_ref[...])
        rstd = lax.rsqrt(jnp.mean(out * out, axis=-1, keepdims=True) + NORM_EPS)
        nrm = out * rstd
        acc_ref[0:1, :] += jnp.sum(dxn_v * (nrm * g_post), axis=0, keepdims=True)
        acc_ref[1:2, :] += jnp.sum(dxn_v * gate * nrm, axis=0, keepdims=True)
        dn = dxn_v * (gate * g_post)
        dout = (rstd * (dn - nrm * jnp.mean(dn * nrm, axis=-1, keepdims=True))).astype(BF16)
        dout_ref[...] = dout
        dmix = _dot_nt(dout, wo_ref[...])
        sa = jax.nn.sigmoid(ma_ref[...])
        sb = jax.nn.sigmoid(mb_ref[...])
        dya = (dmix * sa).astype(BF16)
        dyb = (dmix * sb).astype(BF16)
        dya_ref[...] = dya
        dyb_ref[...] = dyb
        dma = dmix * ya_ref[...].astype(F32) * (sa * (1.0 - sa))
        dmb = dmix * yb_ref[...].astype(F32) * (sb * (1.0 - sb))
        d_ag = _dot_nt(dya, wpa_ref[...])
        d_bg = _dot_nt(dyb, wpb_ref[...])
        gl = gl_ref[...]
        hl_v = hl_ref[...]
        dhl_ref[...] = d_bg * _silu(gl)
        dgl = d_bg * hl_v * _dsilu(gl)
        ga = ga_ref[...]
        o = o_ref[...]
        do = d_ag * _silu(ga)
        dga = d_ag * o * _dsilu(ga)
        lane = lax.broadcasted_iota(jnp.int32, (tm, 128), 1)
        dvec = jnp.zeros((tm, 128), F32)
        for h in range(HEADS):
            sl = slice(h * HEAD_DIM, (h + 1) * HEAD_DIM)
            dvec = jnp.where(lane == h, jnp.sum(do[:, sl] * o[:, sl], axis=1, keepdims=True), dvec)
        do_b = do.astype(BF16)
        do0_ref[0] = do_b
        dv0_ref[0] = dvec
        for do_ref, dv_ref, f_ref in ((do1_ref, dv1_ref, f1_ref), (do2_ref, dv2_ref, f2_ref)):
            _store_residues(do_ref, _dot(f_ref[...], do_b).astype(BF16))
            _store_residues(dv_ref, _permute_f32(f_ref[...], dvec, 3))
        dp_ref[0] = dgl[:, 0:ATT_WIDTH].astype(BF16)
        dp_ref[1] = dgl[:, ATT_WIDTH:].astype(BF16)
        dp_ref[2] = dma[:, 0:ATT_WIDTH].astype(BF16)
        dp_ref[3] = dma[:, ATT_WIDTH:].astype(BF16)
        dp_ref[4] = dmb[:, 0:ATT_WIDTH].astype(BF16)
        dp_ref[5] = dmb[:, ATT_WIDTH:].astype(BF16)
        dp_ref[6] = dga.astype(BF16)

    def col(j):
        return pl.BlockSpec((tm, D_MODEL), lambda i: (i, j))

    def whole(shape):
        return pl.BlockSpec(shape, lambda i: (0, 0))

    def res_spec(r, width):
        return pl.BlockSpec((r, tm // r, width), lambda i: (0, i, 0))

    row = col(0)
    att = pl.BlockSpec((tm, ATT_WIDTH), lambda i: (i, 0))
    bf = jax.ShapeDtypeStruct((S, D_MODEL), BF16)
    vmem = pl.BlockSpec(memory_space=pltpu.VMEM)
    return _call(
        body, (dxn, mix, ya, yb, hl, rest, rest, rest, rest, o_att, vec, w_pa, w_pb, w_o, *forth),
        name="out_bwd", grid=(S // tm,),
        in_specs=[row, row, row, row, row, col(1), col(2), col(3),
                  pl.BlockSpec((tm, ATT_WIDTH), lambda i: (i, 8)), att,
                  whole((8, D_MODEL)), whole((ATT_WIDTH, D_MODEL)), whole((D_MODEL, D_MODEL)),
                  whole((D_MODEL, D_MODEL)), vmem, vmem],
        out_specs=(row, row, row, row,
                   pl.BlockSpec((7, tm, ATT_WIDTH), lambda i: (0, i, 0)), whole((8, D_MODEL)),
                   *[res_spec(r, ATT_WIDTH) for r in DILATIONS], *[res_spec(r, 128) for r in DILATIONS]),
        out_shape=(bf, bf, bf, jax.ShapeDtypeStruct((S, D_MODEL), F32),
                   jax.ShapeDtypeStruct((DREST_CHUNKS, S, ATT_WIDTH), BF16),
                   jax.ShapeDtypeStruct((8, D_MODEL), F32),
                   *[jax.ShapeDtypeStruct((r, S // r, ATT_WIDTH), BF16) for r in DILATIONS],
                   *[jax.ShapeDtypeStruct((r, S // r, 128), F32) for r in DILATIONS]),
        hosted=hosted)


def _loss_head(y, target):
    S = y.shape[0]
    tm = 512

    def body(y_ref, t_ref, dy_ref, acc_ref):
        i = pl.program_id(0)

        @pl.when(i == 0)
        def _():
            acc_ref[...] = jnp.zeros_like(acc_ref)

        err = y_ref[...] - t_ref[...]
        dy_ref[...] = err * (1.0 / D_MODEL)
        part = jnp.sum(jnp.sum(err * err, axis=1, keepdims=True), axis=0, keepdims=True)
        acc_ref[...] += jnp.broadcast_to(part, acc_ref.shape)

    row = pl.BlockSpec((tm, D_MODEL), lambda i: (i, 0))
    return pl.pallas_call(
        body, name="loss_head", grid=(S // tm,),
        in_specs=[row, row],
        out_specs=(row, pl.BlockSpec((8, 128), lambda i: (0, 0))),
        out_shape=(jax.ShapeDtypeStruct((S, D_MODEL), F32), jax.ShapeDtypeStruct((8, 128), F32)),
        compiler_params=_params(("arbitrary",)))(y, target)


def _adamw(gsrcs, w, m, v, n_stack, name, rows=None):
    n_layers, R, C = w.shape
    assert len(gsrcs) == n_layers
    budget = 96 * 1024
    tr = rows
    if tr is None:
        tr = R
        while tr * C > budget and tr % 16 == 0:
            tr //= 2
    assert R % tr == 0 and (tr % 8 == 0 or tr == R)
    c1 = 1.0 - ADAM_B1 ** ADAM_STEP
    c2 = 1.0 - ADAM_B2 ** ADAM_STEP

    def body(*refs):
        g_refs = refs[:n_layers]
        w_ref, m_ref, v_ref, go_ref, d_ref, mo_ref, vo_ref = refs[n_layers:]

        def update(g_ref):
            if n_stack:
                g = g_ref[0].astype(F32)
                for s in range(1, n_stack):
                    g = g + g_ref[s].astype(F32)
            else:
                g = g_ref[...]
            m_new = ADAM_B1 * m_ref[...] + (1.0 - ADAM_B1) * g
            v_new = ADAM_B2 * v_ref[...] + (1.0 - ADAM_B2) * (g * g)
            m_hat = m_new / c1
            v_hat = v_new / c2
            go_ref[...] = g
            d_ref[...] = -ADAM_LR * (m_hat / (jnp.sqrt(v_hat) + ADAM_EPS) + ADAM_WD * w_ref[...])
            mo_ref[...] = m_new
            vo_ref[...] = v_new

        for layer in range(n_layers):
            pl.when(pl.program_id(0) == layer)(functools.partial(update, g_refs[layer]))

    def g_spec(layer):
        def rows_of(l, i):
            return jnp.where(l == layer, i, 0)
        if n_stack:
            return pl.BlockSpec((n_stack, tr, C), lambda l, i: (0, rows_of(l, i), 0))
        return pl.BlockSpec((tr, C), lambda l, i: (rows_of(l, i), 0))

    blk = pl.BlockSpec((None, tr, C), lambda l, i: (l, i, 0))
    shp = jax.ShapeDtypeStruct((n_layers, R, C), F32)
    return pl.pallas_call(
        body, name=name, grid=(n_layers, R // tr),
        in_specs=[g_spec(layer) for layer in range(n_layers)] + [blk, blk, blk],
        out_specs=(blk, blk, blk, blk), out_shape=(shp, shp, shp, shp),
        compiler_params=_params(("arbitrary", "arbitrary")))(*gsrcs, w, m, v)


def _pair_blocks(w):
    w = w.reshape(8, 2, 64, 64)
    z = jnp.zeros((8, 64, 64), w.dtype)
    top = jnp.concatenate([w[:, 0], z], axis=2)
    bot = jnp.concatenate([z, w[:, 1]], axis=2)
    return jnp.concatenate([top, bot], axis=1).astype(BF16)


def _unpair_blocks(g):
    return jnp.stack([g[:, :64, :64], g[:, 64:, 64:]], axis=1).reshape(16, 64, 64)


def _layer_params(layer, w_in_g, w_pa_g, w_pb_g, w_o_g, conv_w_full, conv_b, b_rg, b_ig, lru_lambda, w_rg,
                  w_ig, g_pre, g_post):
    cw = jnp.concatenate([conv_w_full, conv_b[None], b_rg[None], b_ig[None], lru_lambda[None]], axis=0)
    return dict(
        w_int=_weights_to_internal(w_in_g),
        w_pa=jnp.transpose(w_pa_g, (1, 0, 2)).reshape(ATT_WIDTH, D_MODEL),
        w_pb=w_pb_g.reshape(D_MODEL, D_MODEL), w_o=w_o_g.reshape(D_MODEL, D_MODEL),
        cw=cw, wr2=_pair_blocks(w_rg), wi2=_pair_blocks(w_ig),
        g_pre=g_pre, g_post=g_post, lru_lambda=lru_lambda)


def _layer_fwd(x, mod, p, host_proj=None, host_lru=None, host_out=None):
    zeros = jnp.zeros((5, D_MODEL), F32)
    vec_pre = jnp.concatenate([p["g_pre"][None], mod[1:2], mod[0:1], zeros], axis=0)
    (ht0, ht1, ht2, q0, q1, q2, rest), got_proj = _norm_proj_fwd(x, vec_pre, p["w_int"], hosted=host_proj)
    h_t = [ht0, ht1, ht2]
    qkv, outs, stats = [q0, q1, q2], [], []
    for g, r in enumerate(DILATIONS):
        o_g, st_g = _attn_fwd(qkv[g], g, r)
        outs.append(o_g)
        stats.append(st_g)
    o_att, a_gated, *lse = _attn_combine(outs, stats, rest)
    (hl, a_dec, *gates), got_lru = _lru_fwd(rest, p["cw"], p["wr2"], p["wi2"], hosted=host_lru)
    vec_post = jnp.concatenate([p["g_post"][None], mod[2:3], jnp.zeros((6, D_MODEL), F32)], axis=0)
    (x_new, mix, ya, yb, b_gated), got_out = _out_fwd(a_gated, hl, rest, x, vec_post, p["w_pa"], p["w_pb"],
                                                      p["w_o"], hosted=host_out)
    saved = dict(x=x, h_t=h_t, qkv=qkv, rest=rest, o_att=o_att, a_gated=a_gated, lse=lse, hl=hl, a_dec=a_dec,
                 gates=gates, mix=mix, ya=ya, yb=yb, b_gated=b_gated, vec_pre=vec_pre, vec_post=vec_post)
    return x_new, saved, got_proj, got_lru, got_out


def _mats_sources(g_pa, g_pb, g_o):
    return [jnp.transpose(g_pa.astype(BF16).reshape(ATT_WIDTH, N_DEV, 128), (1, 0, 2)),
            g_pb.astype(BF16).reshape(N_DEV, 128, D_MODEL), g_o.astype(BF16).reshape(N_DEV, 128, D_MODEL)]


LRU_SMALL = ("conv_b", "w_rg", "b_rg", "w_ig", "b_ig", "lru_lambda")
MOD_SMALL = ("b_mod", "g_pre", "g_post")


def _flat_rows(arrs):
    return jnp.concatenate([a.reshape(-1) for a in arrs]).reshape(-1, 128)


def _layer_bwd(dxn, sv, p, upper=None, host_own=False):
    a2a3 = ["a2a"] * 3
    (dout, dya, dyb, dhl, drest, acc_out, *do_dvec), got_up_top = _out_bwd(
        dxn, sv["mix"], sv["ya"], sv["yb"], sv["hl"], sv["rest"], sv["o_att"], sv["vec_post"],
        p["w_pa"], p["w_pb"], p["w_o"], hosted=([upper["w_in"][0]], ["a2a"]) if upper else None)
    mats = _mats_sources(_wgrad_tn(sv["a_gated"], dya, "wgrad_pa"), _wgrad_tn(sv["b_gated"], dyb, "wgrad_pb"),
                         _wgrad_tn(sv["mix"], dout, "wgrad_o"))
    (drest, acc_lru, gwr2, gwi2), got_up_bottom = _lru_bwd(
        dhl, sv["hl"], sv["a_dec"], sv["gates"], sv["rest"], p["cw"], p["wr2"], p["wi2"], drest,
        hosted=([upper["w_in"][1]], ["a2a"]) if upper else None)
    small = dict(
        conv_w=acc_lru[0:4], conv_b=acc_lru[4], b_rg=acc_lru[5], b_ig=acc_lru[6],
        lru_lambda=acc_lru[7] * (-jax.nn.sigmoid(-p["lru_lambda"])),
        w_rg=_unpair_blocks(gwr2), w_ig=_unpair_blocks(gwi2))
    hosts = [None, None, None]
    if host_own:
        lru_vec = _flat_rows([jnp.stack([small[n], upper["small"][n]]) for n in LRU_SMALL + ("conv_w",)])
        hosts = [(mats, a2a3), ([lru_vec], ["ag"]), (upper["mats"], a2a3)]
    dqkv, got_attn = [], []
    for g, r in enumerate(DILATIONS):
        d_g, got = _attn_bwd(sv["qkv"][g], do_dvec[g], sv["lse"][g], do_dvec[3 + g], g, r, hosted=hosts[g])
        dqkv.append(d_g)
        got_attn.append(got)
    parts = [_proj_wgrad_group(sv["h_t"][g], dqkv[g], r, f"proj_wgrad_g{g}") for g, r in enumerate(DILATIONS)]
    parts.append(_proj_wgrad_part(sv["h_t"][0], drest, 2, 4, "proj_wgrad_u"))
    parts.append(_proj_wgrad_part(sv["h_t"][0], drest, 7, 0, "proj_wgrad_rest"))
    gw_in = _grads_to_shards(parts)
    (dx, acc_pre), got_in = _proj_dgrad_norm_bwd(
        dqkv, drest, p["w_int"], sv["x"], dxn, sv["vec_pre"],
        hosted=(gw_in, ["a2a"] * 2) if host_own else None)
    small.update(dmod=jnp.concatenate([acc_pre[0], acc_pre[1], acc_out[0]]), g_pre=acc_pre[2],
                 g_post=acc_out[1])
    comm = dict(sources=dict(mats=mats, w_in=gw_in, small=small),
                own=dict(mats=got_attn[0], w_in=got_in, lru_small=got_attn[1]),
                upper=dict(mats=got_attn[2], w_in=got_up_top + got_up_bottom))
    return dx, small, comm


SMALL_NAMES = ("b_mod", "g_pre", "conv_b", "w_rg", "b_rg", "w_ig", "b_ig", "lru_lambda", "g_post")


def kernel(x, c, w_mod, b_mod, g_pre, w_in, conv_w, conv_b, w_rg, b_rg, w_ig, b_ig, lru_lambda, w_pa, w_pb, w_o, g_post, loss_target, m_w_mod, m_b_mod, m_g_pre, m_w_in, m_conv_w, m_conv_b, m_w_rg, m_b_rg, m_w_ig, m_b_ig, m_lru_lambda, m_w_pa, m_w_pb, m_w_o, m_g_post, v_w_mod, v_b_mod, v_g_pre, v_w_in, v_conv_w, v_conv_b, v_w_rg, v_b_rg, v_w_ig, v_b_ig, v_lru_lambda, v_w_pa, v_w_pb, v_w_o, v_g_post):
    W = dict(w_mod=w_mod, b_mod=b_mod, g_pre=g_pre, w_in=w_in, conv_w=conv_w, conv_b=conv_b, w_rg=w_rg,
             b_rg=b_rg, w_ig=w_ig, b_ig=b_ig, lru_lambda=lru_lambda, w_pa=w_pa, w_pb=w_pb, w_o=w_o,
             g_post=g_post)
    M = dict(w_mod=m_w_mod, b_mod=m_b_mod, g_pre=m_g_pre, w_in=m_w_in, conv_w=m_conv_w, conv_b=m_conv_b,
             w_rg=m_w_rg, b_rg=m_b_rg, w_ig=m_w_ig, b_ig=m_b_ig, lru_lambda=m_lru_lambda, w_pa=m_w_pa,
             w_pb=m_w_pb, w_o=m_w_o, g_post=m_g_post)
    V = dict(w_mod=v_w_mod, b_mod=v_b_mod, g_pre=v_g_pre, w_in=v_w_in, conv_w=v_conv_w, conv_b=v_conv_b,
             w_rg=v_w_rg, b_rg=v_b_rg, w_ig=v_w_ig, b_ig=v_b_ig, lru_lambda=v_lru_lambda, w_pa=v_w_pa,
             w_pb=v_w_pb, w_o=v_w_o, g_post=v_g_post)
    S = x.shape[1]
    me = 4 * lax.axis_index("x") + 2 * lax.axis_index("y") + lax.axis_index("c")
    n_mod = w_mod.shape[2]
    n_in = w_in.shape[2]
    n_conv = conv_w.shape[2]

    c_rows = jnp.broadcast_to(c, (8, D_MODEL))
    w_in_b, w_pa_b, w_pb_b, w_o_b = (t.astype(BF16) for t in (w_in, w_pa, w_pb, w_o))
    g_c, g_win0, g_wpa0, g_wpb0, g_wo0, g_cw = _gather_two_level(
        [c_rows, w_in_b[0], w_pa_b[0], w_pb_b[0], w_o_b[0], conv_w], "gather_weights")
    c_all = g_c[:, 0, :]
    c_pad = jnp.concatenate([c_all, jnp.zeros((8, D_MODEL), F32)], axis=0)
    conv_w_full = jnp.transpose(g_cw, (1, 2, 0, 3)).reshape(2, CONV_WIDTH, D_MODEL)

    mod_cols = _mod_fwd(c_pad, w_mod)
    mod_src = jnp.transpose(mod_cols[:, :8], (1, 0, 2))
    mod_src = jnp.concatenate([mod_src, jnp.zeros((8, 6, n_mod), F32)], axis=1)
    (mod_got,) = _exchange([mod_src], ["a2a"], "scatter_mod")
    mod = jnp.transpose(mod_got[:, :2], (1, 0, 2)).reshape(2, 3 * D_MODEL) + b_mod

    def layer_params(layer, w_in_g, w_pa_g, w_pb_g, w_o_g):
        return _layer_params(layer, w_in_g, w_pa_g, w_pb_g, w_o_g, conv_w_full[layer], conv_b[layer],
                             b_rg[layer], b_ig[layer], lru_lambda[layer], w_rg[layer], w_ig[layer],
                             g_pre[layer], g_post[layer])

    half = D_MODEL // 2
    layers = [layer_params(0, [g_win0], g_wpa0, g_wpb0, g_wo0), None]
    act, sv0, (g_win1_top,), (g_win1_bottom,), (g_wpa1, g_wpb1, g_wo1) = _layer_fwd(
        x[0], mod[0].reshape(3, D_MODEL), layers[0],
        host_proj=([w_in_b[1, :half]], ["ag"]), host_lru=([w_in_b[1, half:]], ["ag"]),
        host_out=([w_pa_b[1], w_pb_b[1], w_o_b[1]], ["ag"] * 3))
    layers[1] = layer_params(1, [g_win1_top, g_win1_bottom], g_wpa1, g_wpb1, g_wo1)
    act, sv1, _, _, _ = _layer_fwd(act, mod[1].reshape(3, D_MODEL), layers[1])
    dy, loss_acc = _loss_head(act, loss_target[0])

    dy, small1, comm1 = _layer_bwd(dy, sv1, layers[1])
    dy, small0, comm0 = _layer_bwd(dy, sv0, layers[0], upper=comm1["sources"], host_own=True)
    grad_x = dy[None]
    r_pa, r_pb, r_o = ([comm0["own"]["mats"][k], comm0["upper"]["mats"][k]] for k in range(3))
    r_in = comm0["own"]["w_in"] + comm0["upper"]["w_in"]
    grads = [small0, small1]

    def stack2(name):
        return jnp.stack([grads[0][name], grads[1][name]], axis=0)

    dmod = stack2("dmod")
    dmod_src = jnp.transpose(dmod.reshape(2, N_DEV, n_mod), (1, 0, 2))
    dmod_src = jnp.concatenate([dmod_src, jnp.zeros((8, 6, n_mod), F32)], axis=1)
    mod_vec = _flat_rows([dmod, stack2("g_pre"), stack2("g_post"), loss_acc])
    r_dmod, r_mod_small = _exchange([dmod_src, mod_vec], ["a2a", "ag"], "exchange_grads")
    mod_rows = mod_vec.shape[0] - loss_acc.shape[0]
    loss = 0.5 * jnp.sum(r_mod_small[:, mod_rows, 0]) / D_MODEL
    (r_lru_small,) = comm0["own"]["lru_small"]

    res = {}
    res["w_in"] = _adamw(r_in, *(t.reshape(4, half, n_in) for t in (w_in, m_w_in, v_w_in)), 8, "adamw_w_in")
    res["w_pa"] = _adamw(r_pa, w_pa, m_w_pa, v_w_pa, 8, "adamw_w_pa")
    res["w_pb"] = _adamw(r_pb, w_pb, m_w_pb, v_w_pb, 8, "adamw_w_pb")
    res["w_o"] = _adamw(r_o, w_o, m_w_o, v_w_o, 8, "adamw_w_o")
    dmod_all = jnp.transpose(r_dmod[:, :2], (1, 0, 2))
    gw_mod = _mod_wgrad(jnp.transpose(c_all), dmod_all)
    res["w_mod"] = _adamw([gw_mod[0], gw_mod[1]], w_mod, m_w_mod, v_w_mod, 0, "adamw_w_mod")
    for names_rep, stack, tag in ((LRU_SMALL, r_lru_small, "lru"), (MOD_SMALL, r_mod_small, "mod")):
        rows = sum(W[n].size for n in names_rep) // 128
        rep = _adamw([stack], *(_flat_rows([src[n] for n in names_rep])[None] for src in (W, M, V)), 8,
                     f"adamw_small_{tag}", rows=rows // 2 if rows % 16 == 0 else rows)
        off = 0
        for name in names_rep:
            size = W[name].size
            res[name] = tuple(t.reshape(-1)[off:off + size].reshape(W[name].shape) for t in rep)
            off += size
    lru_rows = sum(W[n].size for n in LRU_SMALL) // 128
    conv_stack = r_lru_small[:, lru_rows:].reshape(8, 2, CONV_WIDTH, D_MODEL)
    conv_stack = lax.dynamic_slice_in_dim(conv_stack, me * n_conv, n_conv, axis=3).reshape(8, 8, n_conv)
    res["conv_w"] = _adamw([conv_stack], conv_w.reshape(1, 8, n_conv), m_conv_w.reshape(1, 8, n_conv),
                           v_conv_w.reshape(1, 8, n_conv), 8, "adamw_conv_w")

    names = ("w_mod", "b_mod", "g_pre", "w_in", "conv_w", "conv_b", "w_rg", "b_rg", "w_ig", "b_ig",
             "lru_lambda", "w_pa", "w_pb", "w_o", "g_post")
    outs = [loss, grad_x]
    for k in range(4):
        outs.extend(res[n][k].reshape(W[n].shape) for n in names)
    return tuple(outs)
```

```python
import functools

import jax
import jax.numpy as jnp
import numpy as np
from jax import lax
from jax.experimental import pallas as pl
from jax.experimental.pallas import tpu as pltpu

F32 = jnp.float32
BF16 = jnp.bfloat16

N_DEV = 8
D_MODEL = 1024
HEAD_DIM = 128
HEADS = 4
ATT_WIDTH = HEADS * HEAD_DIM
DILATIONS = (1, 4, 16)
BAND = 128
N_CHUNKS = 18
QKV_CHUNKS = 9
CONV_WIDTH = 4
LRU_C = 8.0
NORM_EPS = 1e-6
NEG_INF = -1e30
ADAM_LR = 0.001
ADAM_B1 = 0.9
ADAM_B2 = 0.999
ADAM_EPS = 1e-08
ADAM_WD = 0.01
ADAM_STEP = 10
CHUNK_PERM = (0, 3, 6, 1, 4, 7, 2, 5, 8, 10, 11, 12, 13, 14, 15, 16, 17, 9)
DREST_CHUNKS = 10
VMEM_LIMIT = 56 * 1024 * 1024
MESH_AXES = ("x", "y", "c")


def _params(sem=None):
    return pltpu.CompilerParams(dimension_semantics=sem, vmem_limit_bytes=VMEM_LIMIT)


def _silu(x):
    return x * jax.nn.sigmoid(x)


def _dsilu(x):
    s = jax.nn.sigmoid(x)
    return s * (1.0 + x * (1.0 - s))


def _neg_expm1(x):
    series = -x * (1.0 + x * (0.5 + x * (1.0 / 6.0 + x * (1.0 / 24.0 + x * (1.0 / 120.0)))))
    return jnp.where(x > -0.05, series, 1.0 - jnp.exp(x))


def _softplus_neg(lam):
    z = jnp.exp(-jnp.abs(lam))
    small = z * (1.0 - z * (0.5 - z * (1.0 / 3.0 - z * 0.25)))
    log1p_z = jnp.where(z < 1e-2, small, jnp.log(1.0 + z))
    return jnp.maximum(-lam, 0.0) + log1p_z


def _dot(a, b):
    return jnp.dot(a, b, preferred_element_type=F32)


def _dot_nt(a, b):
    return lax.dot_general(a, b, (((1,), (1,)), ((), ())), preferred_element_type=F32)


def _dot_tn(a, b):
    return lax.dot_general(a, b, (((0,), (0,)), ((), ())), preferred_element_type=F32)


def _perm_matrix(r, rows, transpose=False):
    n = rows // r
    p = np.zeros((rows, rows), np.float32)
    dst = np.arange(rows)
    p[dst, (dst % n) * r + dst // n] = 1.0
    return jnp.asarray(p.T if transpose else p, dtype=BF16)


def _permute_f32(p, x, pieces):
    part = x.astype(BF16)
    acc = _dot(p, part)
    for _ in range(pieces - 1):
        x = x - part.astype(F32)
        part = x.astype(BF16)
        acc = acc + _dot(p, part)
    return acc


def _rows_of_residues(ref, lead=()):
    r = ref.shape[len(lead)]
    if r == 1:
        return ref[lead + (0,)]
    return jnp.concatenate([ref[lead + (rho,)] for rho in range(r)], axis=0)


def _store_residues(ref, x, lead=()):
    r = ref.shape[len(lead)]
    n = x.shape[0] // r
    for rho in range(r):
        ref[lead + (rho,)] = x[rho * n:(rho + 1) * n]


ANY_SPEC = pl.BlockSpec(memory_space=pl.ANY)


def _exchange_shapes(arrs, modes):
    return [jax.ShapeDtypeStruct(((N_DEV,) + a.shape) if mode == "ag" else a.shape, a.dtype)
            for a, mode in zip(arrs, modes)]


def _exchange_sems(n):
    return [pltpu.SemaphoreType.DMA((7 * n,)), pltpu.SemaphoreType.DMA((7 * n,)),
            pltpu.SemaphoreType.DMA((n,))]


def _exchange_copies(ins, outs, modes, sems):
    n = len(ins)
    send_sems, recv_sems, local_sems = sems
    x, y, c = lax.axis_index("x"), lax.axis_index("y"), lax.axis_index("c")
    me = 4 * x + 2 * y + c

    def src_for(a, dev):
        return ins[a] if modes[a] == "ag" else ins[a].at[dev]

    local = [pltpu.make_async_copy(src_for(a, me), outs[a].at[me], local_sems.at[a]) for a in range(n)]
    sends, arrivals = [], []
    for k in range(1, N_DEV):
        px = 1 - x if (k >> 2) & 1 else x
        py = 1 - y if (k >> 1) & 1 else y
        pc = 1 - c if k & 1 else c
        peer = 4 * px + 2 * py + pc
        for a in range(n):
            s = (k - 1) * n + a
            for dst, group in ((me, sends), (peer, arrivals)):
                group.append(pltpu.make_async_remote_copy(
                    src_ref=src_for(a, peer), dst_ref=outs[a].at[dst],
                    send_sem=send_sems.at[s], recv_sem=recv_sems.at[s],
                    device_id=(px, py, pc), device_id_type=pl.DeviceIdType.MESH))
    return local, sends, arrivals


def _exchange_start(copies):
    local, sends, _ = copies
    for cp in local + sends:
        cp.start()


def _exchange_wait(copies):
    local, sends, arrivals = copies
    for cp in arrivals:
        cp.wait_recv()
    for cp in sends:
        cp.wait_send()
    for cp in local:
        cp.wait()


def _exchange(arrs, modes, name):
    n = len(arrs)

    def body(*refs):
        copies = _exchange_copies(refs[:n], refs[n:2 * n], modes, refs[2 * n:])
        _exchange_start(copies)
        _exchange_wait(copies)

    outs = pl.pallas_call(
        body, name=name, out_shape=tuple(_exchange_shapes(arrs, modes)),
        in_specs=[ANY_SPEC] * n, out_specs=tuple([ANY_SPEC] * n),
        scratch_shapes=_exchange_sems(n),
    )(*arrs)
    return list(outs)


def _gather_two_level(arrs, name):
    n = len(arrs)

    def body(*refs):
        ins, outs = refs[:n], refs[n:2 * n]
        send_sems, recv_sems, local_sems = refs[2 * n:]
        x, y, c = lax.axis_index("x"), lax.axis_index("y"), lax.axis_index("c")
        me, sibling = (x, y, c), (x, y, 1 - c)
        chips = [(1 - x, y), (x, 1 - y), (1 - x, 1 - y)]

        def copy(a, k, block, to, src=None):
            slot = 4 * block[0] + 2 * block[1] + block[2]
            return pltpu.make_async_remote_copy(
                src_ref=outs[a].at[slot] if src is None else src, dst_ref=outs[a].at[slot],
                send_sem=send_sems.at[7 * a + k], recv_sem=recv_sems.at[7 * a + k],
                device_id=to, device_id_type=pl.DeviceIdType.MESH)

        mine = [pltpu.make_async_copy(ins[a], outs[a].at[4 * x + 2 * y + c], local_sems.at[a])
                for a in range(n)]
        first = []
        for a in range(n):
            first.append(copy(a, 0, me, sibling, src=ins[a]))
            first += [copy(a, 1 + j, me, (*chip, c), src=ins[a]) for j, chip in enumerate(chips)]
        for cp in mine + first:
            cp.start()
        passed = []
        for j, chip in enumerate(chips):
            for a in range(n):
                copy(a, 1 + j, (*chip, c), me).wait_recv()
                passed.append(copy(a, 4 + j, (*chip, c), sibling))
                passed[-1].start()
        for a in range(n):
            copy(a, 0, sibling, me).wait_recv()
            for j, chip in enumerate(chips):
                copy(a, 4 + j, (*chip, 1 - c), me).wait_recv()
        for cp in first + passed:
            cp.wait_send()
        for cp in mine:
            cp.wait()

    outs = pl.pallas_call(
        body, name=name, out_shape=tuple(_exchange_shapes(arrs, ["ag"] * n)),
        in_specs=[ANY_SPEC] * n, out_specs=tuple([ANY_SPEC] * n),
        scratch_shapes=_exchange_sems(n),
    )(*arrs)
    return list(outs)


SHARD_COLS = N_CHUNKS * ATT_WIDTH // N_DEV


def _weights_to_internal(shards):
    blocks = []
    for shard in shards:
        full = jnp.transpose(shard, (1, 0, 2)).reshape(shard.shape[1], N_CHUNKS, ATT_WIDTH)
        blocks.append(full[:, jnp.array(CHUNK_PERM)].reshape(shard.shape[1], N_CHUNKS * ATT_WIDTH))
    return blocks[0] if len(blocks) == 1 else jnp.concatenate(blocks, axis=0)


def _grads_to_shards(parts):
    inv_perm = jnp.array([CHUNK_PERM.index(k) for k in range(N_CHUNKS)])
    full = jnp.concatenate(parts, axis=1).reshape(D_MODEL, N_CHUNKS, ATT_WIDTH)[:, inv_perm]
    return jnp.transpose(full.reshape(D_MODEL, N_DEV, SHARD_COLS), (1, 0, 2))


def _call(body, args, *, name, grid, in_specs, out_specs, out_shape, scratch_shapes=(), aliases=None,
          hosted=None):
    sem = ("arbitrary",) * len(grid)
    if hosted is None:
        outs = pl.pallas_call(
            body, name=name, grid=grid, in_specs=in_specs, out_specs=tuple(out_specs),
            out_shape=tuple(out_shape), scratch_shapes=list(scratch_shapes),
            input_output_aliases=aliases or {}, compiler_params=_params(sem))(*args)
        return list(outs), []
    x_arrs, modes = hosted
    n_in, n_out, n_scr, nx = len(args), len(out_shape), len(scratch_shapes), len(x_arrs)

    def wrapped(*refs):
        ins, x_ins = refs[:n_in], refs[n_in:n_in + nx]
        outs = refs[n_in + nx:n_in + nx + n_out]
        x_outs = refs[n_in + nx + n_out:n_in + 2 * nx + n_out]
        scr = refs[n_in + 2 * nx + n_out:n_in + 2 * nx + n_out + n_scr]
        sems = refs[n_in + 2 * nx + n_out + n_scr:]
        first = pl.program_id(0) == 0
        last = pl.program_id(0) == grid[0] - 1
        for axis in range(1, len(grid)):
            first = jnp.logical_and(first, pl.program_id(axis) == 0)
            last = jnp.logical_and(last, pl.program_id(axis) == grid[axis] - 1)

        @pl.when(first)
        def _():
            _exchange_start(_exchange_copies(x_ins, x_outs, modes, sems))

        body(*ins, *outs, *scr)

        @pl.when(last)
        def _():
            _exchange_wait(_exchange_copies(x_ins, x_outs, modes, sems))

    outs = pl.pallas_call(
        wrapped, name=name, grid=grid, in_specs=list(in_specs) + [ANY_SPEC] * nx,
        out_specs=tuple(out_specs) + tuple([ANY_SPEC] * nx),
        out_shape=tuple(out_shape) + tuple(_exchange_shapes(x_arrs, modes)),
        scratch_shapes=list(scratch_shapes) + _exchange_sems(nx),
        input_output_aliases=aliases or {}, compiler_params=_params(sem))(*args, *x_arrs)
    return list(outs[:n_out]), list(outs[n_out:])


def _mod_fwd(c_pad, w_mod):
    def body(c_ref, w_ref, o_ref):
        sc = _silu(c_ref[...]).astype(BF16)
        for layer in range(2):
            o_ref[layer] = _dot(sc, w_ref[layer].astype(BF16))

    return pl.pallas_call(
        body, name="mod_fwd", out_shape=jax.ShapeDtypeStruct((2, 16, w_mod.shape[2]), F32),
        compiler_params=_params())(c_pad, w_mod)


def _mod_wgrad(c_t, dmod):
    n_cols = dmod.shape[2]

    def body(c_ref, d_ref, o_ref):
        sc = _silu(c_ref[...]).astype(BF16).astype(F32)
        for layer in range(2):
            dm = d_ref[layer].astype(BF16).astype(F32)
            acc = sc[:, 0:1] * dm[0:1, :]
            for b in range(1, N_DEV):
                acc = acc + sc[:, b:b + 1] * dm[b:b + 1, :]
            o_ref[layer] = acc

    return pl.pallas_call(
        body, name="mod_wgrad", out_shape=jax.ShapeDtypeStruct((2, D_MODEL, n_cols), F32),
        compiler_params=_params())(c_t, dmod)


GROUP_COLS = 3 * ATT_WIDTH
PROJ_TM = 256


def _norm_proj_fwd(x, vec, w_int, hosted=None):
    S = x.shape[0]
    tm = PROJ_TM
    perms = [_perm_matrix(r, tm) for r in DILATIONS[1:]]

    def body(x_ref, v_ref, w_ref, p1_ref, p2_ref, ht0_ref, ht1_ref, ht2_ref, q0_ref, q1_ref, q2_ref,
             rest_ref):
        xv = x_ref[...]
        rstd = lax.rsqrt(jnp.mean(xv * xv, axis=-1, keepdims=True) + NORM_EPS)
        hf = ((xv * rstd) * v_ref[0:1, :]) * (1.0 + v_ref[1:2, :]) + v_ref[2:3, :]
        h = hf.astype(BF16)
        for g, (q_ref, ht_ref, p_ref) in enumerate(((q0_ref, ht0_ref, None), (q1_ref, ht1_ref, p1_ref),
                                                    (q2_ref, ht2_ref, p2_ref))):
            rows_f = hf if p_ref is None else _dot(p_ref[...], h)
            ht_ref[...] = rows_f.T.astype(BF16)
            _store_residues(q_ref, _dot(rows_f.astype(BF16),
                                        w_ref[:, g * GROUP_COLS:(g + 1) * GROUP_COLS]).astype(BF16))
        rest_ref[...] = _dot(h, w_ref[:, 3 * GROUP_COLS:])

    n_rest = QKV_CHUNKS * ATT_WIDTH
    whole = pl.BlockSpec(memory_space=pltpu.VMEM)
    ht_spec = pl.BlockSpec((D_MODEL, tm), lambda i: (0, i))
    ht_shape = jax.ShapeDtypeStruct((D_MODEL, S), BF16)
    return _call(
        body, (x, vec, w_int, *perms), name="norm_proj_fwd", grid=(S // tm,),
        in_specs=[pl.BlockSpec((tm, D_MODEL), lambda i: (i, 0)),
                  pl.BlockSpec((8, D_MODEL), lambda i: (0, 0)), whole, whole, whole],
        out_specs=(ht_spec, ht_spec, ht_spec,
                   *[pl.BlockSpec((r, tm // r, GROUP_COLS), lambda i: (0, i, 0)) for r in DILATIONS],
                   pl.BlockSpec((tm, n_rest), lambda i: (i, 0))),
        out_shape=(ht_shape, ht_shape, ht_shape,
                   *[jax.ShapeDtypeStruct((r, S // r, GROUP_COLS), BF16) for r in DILATIONS],
                   jax.ShapeDtypeStruct((S, n_rest), F32)),
        hosted=hosted)


def _proj_dgrad_norm_bwd(dqkv, drest, w_int, x, dxn, vec, hosted=None):
    S = x.shape[0]
    tm = PROJ_TM
    perms = [_perm_matrix(r, tm, transpose=True) for r in DILATIONS[1:]]

    def body(d0_ref, d1_ref, d2_ref, dr_ref, w_ref, p1_ref, p2_ref, x_ref, dxn_ref, v_ref, dx_ref, acc_ref):
        i = pl.program_id(0)

        @pl.when(i == 0)
        def _():
            acc_ref[...] = jnp.zeros_like(acc_ref)

        def wcols(k):
            return w_ref[:, k * ATT_WIDTH:(k + 1) * ATT_WIDTH]

        dhv = None
        for g, (d_ref, p_ref) in enumerate(((d0_ref, None), (d1_ref, p1_ref), (d2_ref, p2_ref))):
            for t in range(3):
                d = _rows_of_residues(d_ref, (t,))
                if p_ref is not None:
                    d = _dot(p_ref[...], d).astype(BF16)
                part = _dot_nt(d, wcols(3 * g + t))
                dhv = part if dhv is None else dhv + part
        for t in range(7):
            dhv = dhv + _dot_nt(dr_ref[t], wcols(11 + t))
        for t in range(2):
            dhv = dhv + _dot_nt(dr_ref[8 + t], wcols(9 + t))

        xv = x_ref[...]
        g = v_ref[0:1, :]
        sc1 = 1.0 + v_ref[1:2, :]
        rstd = lax.rsqrt(jnp.mean(xv * xv, axis=-1, keepdims=True) + NORM_EPS)
        xhat = xv * rstd
        acc_ref[0:1, :] += jnp.sum(dhv, axis=0, keepdims=True)
        acc_ref[1:2, :] += jnp.sum(dhv * (xhat * g), axis=0, keepdims=True)
        acc_ref[2:3, :] += jnp.sum(dhv * xhat * sc1, axis=0, keepdims=True)
        dxhat = dhv * (g * sc1)
        dx = rstd * (dxhat - xhat * jnp.mean(dxhat * xhat, axis=-1, keepdims=True))
        dx_ref[...] = dx + dxn_ref[...]

    row = pl.BlockSpec((tm, D_MODEL), lambda i: (i, 0))
    vec_spec = pl.BlockSpec((8, D_MODEL), lambda i: (0, 0))
    whole = pl.BlockSpec(memory_space=pltpu.VMEM)
    return _call(
        body, (*dqkv, drest, w_int, *perms, x, dxn, vec), name="proj_dgrad_norm_bwd", grid=(S // tm,),
        in_specs=[*[pl.BlockSpec((3, r, tm // r, ATT_WIDTH), lambda i: (0, 0, i, 0)) for r in DILATIONS],
                  pl.BlockSpec((DREST_CHUNKS, tm, ATT_WIDTH), lambda i: (0, i, 0)),
                  whole, whole, whole, row, row, vec_spec],
        out_specs=(row, vec_spec),
        out_shape=(jax.ShapeDtypeStruct((S, D_MODEL), F32), jax.ShapeDtypeStruct((8, D_MODEL), F32)),
        hosted=hosted)


def _proj_wgrad_part(h_t, d, n_chunks, chunk_block, name):
    S = h_t.shape[1]
    ts = 512
    n_steps = S // ts

    def body(h_ref, d_ref, o_ref, acc_ref):
        i = pl.program_id(0)

        @pl.when(i == 0)
        def _():
            acc_ref[...] = jnp.zeros_like(acc_ref)

        ht = h_ref[...]
        for t in range(n_chunks):
            acc_ref[:, t * ATT_WIDTH:(t + 1) * ATT_WIDTH] += _dot(ht, d_ref[t])

        @pl.when(i == n_steps - 1)
        def _():
            o_ref[...] = acc_ref[...].astype(BF16)

    return pl.pallas_call(
        body, name=name, grid=(n_steps,),
        in_specs=[pl.BlockSpec((D_MODEL, ts), lambda i: (0, i)),
                  pl.BlockSpec((n_chunks, ts, ATT_WIDTH), lambda i: (chunk_block, i, 0))],
        out_specs=pl.BlockSpec((D_MODEL, n_chunks * ATT_WIDTH), lambda i: (0, 0)),
        out_shape=jax.ShapeDtypeStruct((D_MODEL, n_chunks * ATT_WIDTH), BF16),
        scratch_shapes=[pltpu.VMEM((D_MODEL, n_chunks * ATT_WIDTH), F32)],
        compiler_params=_params(("arbitrary",)))(h_t, d)


def _proj_wgrad_group(h_t, d, r, name):
    S = h_t.shape[1]
    ts = 512
    n_steps = S // ts
    n = PROJ_TM // r

    def body(h_ref, d_ref, o_ref, acc_ref):
        i = pl.program_id(0)

        @pl.when(i == 0)
        def _():
            acc_ref[...] = jnp.zeros_like(acc_ref)

        ht = h_ref[...]
        for t in range(3):
            rows = jnp.concatenate([d_ref[t, rho, s * n:(s + 1) * n] for s in range(ts // PROJ_TM)
                                    for rho in range(r)], axis=0) if r > 1 else d_ref[t, 0]
            acc_ref[:, t * ATT_WIDTH:(t + 1) * ATT_WIDTH] += _dot(ht, rows)

        @pl.when(i == n_steps - 1)
        def _():
            o_ref[...] = acc_ref[...].astype(BF16)

    return pl.pallas_call(
        body, name=name, grid=(n_steps,),
        in_specs=[pl.BlockSpec((D_MODEL, ts), lambda i: (0, i)),
                  pl.BlockSpec((3, r, ts // r, ATT_WIDTH), lambda i: (0, 0, i, 0))],
        out_specs=pl.BlockSpec((D_MODEL, GROUP_COLS), lambda i: (0, 0)),
        out_shape=jax.ShapeDtypeStruct((D_MODEL, GROUP_COLS), BF16),
        scratch_shapes=[pltpu.VMEM((D_MODEL, GROUP_COLS), F32)],
        compiler_params=_params(("arbitrary",)))(h_t, d)


def _wgrad_tn(a, b, name):
    S, M = a.shape
    N = b.shape[1]
    ts = 1024

    def body(a_ref, b_ref, o_ref):
        i = pl.program_id(0)
        part = _dot_tn(a_ref[...], b_ref[...])

        @pl.when(i == 0)
        def _():
            o_ref[...] = part

        @pl.when(i > 0)
        def _():
            o_ref[...] += part

    return pl.pallas_call(
        body, name=name, grid=(S // ts,),
        in_specs=[pl.BlockSpec((ts, M), lambda i: (i, 0)), pl.BlockSpec((ts, N), lambda i: (i, 0))],
        out_specs=pl.BlockSpec((M, N), lambda i: (0, 0)),
        out_shape=jax.ShapeDtypeStruct((M, N), F32),
        compiler_params=_params(("arbitrary",)))(a, b)


def _attn_fwd(qkv, g, r):
    L = qkv.shape[1]
    nb2 = L // (2 * BAND)
    scale = HEAD_DIM ** -0.5

    def body(q_ref, kp_ref, kc_ref, vp_ref, vc_ref, o_ref, st_ref):
        m_step = pl.program_id(1)
        ii = lax.broadcasted_iota(jnp.int32, (2 * BAND, 3 * BAND), 0)
        kk = lax.broadcasted_iota(jnp.int32, (2 * BAND, 3 * BAND), 1)
        mask = jnp.logical_and(jnp.logical_and(kk >= ii, kk <= ii + BAND),
                               jnp.logical_or(kk >= BAND, m_step > 0))
        lane = lax.broadcasted_iota(jnp.int32, (2 * BAND, 128), 1)
        stats = jnp.zeros((2 * BAND, 128), F32)
        for h in range(HEADS):
            sl = slice(h * HEAD_DIM, (h + 1) * HEAD_DIM)
            k_cat = jnp.concatenate([kp_ref[:, sl], kc_ref[:, sl]], axis=0)
            v_cat = jnp.concatenate([vp_ref[:, sl], vc_ref[:, sl]], axis=0)
            s = jnp.where(mask, _dot_nt(q_ref[:, sl], k_cat) * scale, NEG_INF)
            m = jnp.max(s, axis=1, keepdims=True)
            p = jnp.exp(s - m)
            l = jnp.sum(p, axis=1, keepdims=True)
            o_ref[:, sl] = _dot(p.astype(BF16), v_cat) / l
            stats = jnp.where(lane == h, m, stats)
            stats = jnp.where(lane == HEADS + h, l, stats)
        st_ref[...] = stats

    two = (None, 2 * BAND, ATT_WIDTH)
    one = (None, BAND, ATT_WIDTH)

    def prev(m):
        return jnp.maximum(2 * m - 1, 0)

    return pl.pallas_call(
        body, name=f"attn_fwd_g{g}", grid=(r, nb2),
        in_specs=[
            pl.BlockSpec(two, lambda rho, m: (rho, m, 0)),
            pl.BlockSpec(one, lambda rho, m: (rho, prev(m), 1)),
            pl.BlockSpec(two, lambda rho, m: (rho, m, 1)),
            pl.BlockSpec(one, lambda rho, m: (rho, prev(m), 2)),
            pl.BlockSpec(two, lambda rho, m: (rho, m, 2)),
        ],
        out_specs=(pl.BlockSpec(two, lambda rho, m: (rho, m, 0)),
                   pl.BlockSpec((None, 2 * BAND, 128), lambda rho, m: (rho, m, 0))),
        out_shape=(jax.ShapeDtypeStruct((r, L, ATT_WIDTH), F32),
                   jax.ShapeDtypeStruct((r, L, 128), F32)),
        compiler_params=_params(("parallel", "parallel")))(qkv, qkv, qkv, qkv, qkv)


def _attn_combine(outs, stats, rest):
    S = rest.shape[0]
    tm = 512
    gatt_blk = 8
    back = [_perm_matrix(r, tm, transpose=True) for r in DILATIONS[1:]]
    forth = [_perm_matrix(r, tm) for r in DILATIONS[1:]]

    def body(o0_ref, o1_ref, o2_ref, s0_ref, s1_ref, s2_ref, g_ref, b1_ref, b2_ref, f1_ref, f2_ref,
             o_ref, a_ref, l0_ref, l1_ref, l2_ref):
        outs_nat = [o0_ref[0]] + [_permute_f32(b_ref[...], _rows_of_residues(o_g), 2)
                                  for o_g, b_ref in ((o1_ref, b1_ref), (o2_ref, b2_ref))]
        st = [s0_ref[0]] + [_permute_f32(b_ref[...], _rows_of_residues(s_g), 3)
                            for s_g, b_ref in ((s1_ref, b1_ref), (s2_ref, b2_ref))]
        lane = lax.broadcasted_iota(jnp.int32, (tm, 128), 1)
        lse_out = jnp.zeros((tm, 128), F32)
        for h in range(HEADS):
            sl = slice(h * HEAD_DIM, (h + 1) * HEAD_DIM)
            ms = [s[:, h:h + 1] for s in st]
            ls = [s[:, HEADS + h:HEADS + h + 1] for s in st]
            m_all = jnp.maximum(jnp.maximum(ms[0], ms[1]), ms[2])
            ws = [l * jnp.exp(m - m_all) for m, l in zip(ms, ls)]
            den = ws[0] + ws[1] + ws[2]
            o = (ws[0] * outs_nat[0][:, sl] + ws[1] * outs_nat[1][:, sl] + ws[2] * outs_nat[2][:, sl]) / den
            o_ref[:, sl] = o
            a_ref[:, sl] = (o * _silu(g_ref[:, sl])).astype(BF16)
            lse_out = jnp.where(lane == h, m_all + jnp.log(den), lse_out)
        l0_ref[0] = lse_out
        for l_ref, f_ref in ((l1_ref, f1_ref), (l2_ref, f2_ref)):
            _store_residues(l_ref, _permute_f32(f_ref[...], lse_out, 3))

    o_spec = pl.BlockSpec((tm, ATT_WIDTH), lambda i: (i, 0))
    whole = pl.BlockSpec(memory_space=pltpu.VMEM)

    def res_spec(r, width):
        return pl.BlockSpec((r, tm // r, width), lambda i: (0, i, 0))

    return pl.pallas_call(
        body, name="attn_combine", grid=(S // tm,),
        in_specs=[*[res_spec(r, ATT_WIDTH) for r in DILATIONS], *[res_spec(r, 128) for r in DILATIONS],
                  pl.BlockSpec((tm, ATT_WIDTH), lambda i: (i, gatt_blk)), whole, whole, whole, whole],
        out_specs=(o_spec, o_spec, *[res_spec(r, 128) for r in DILATIONS]),
        out_shape=(jax.ShapeDtypeStruct((S, ATT_WIDTH), F32), jax.ShapeDtypeStruct((S, ATT_WIDTH), BF16),
                   *[jax.ShapeDtypeStruct((r, S // r, 128), F32) for r in DILATIONS]),
        compiler_params=_params(("parallel",)))(*outs, *stats, rest, *back, *forth)


def _attn_bwd(qkv, do, lse, dvec, g, r, hosted=None):
    L = qkv.shape[1]
    nb = L // BAND
    nb2 = nb // 2
    scale = HEAD_DIM ** -0.5

    def body(qc_ref, qn_ref, k_ref, v_ref, doc_ref, don_ref, lc_ref, ln_ref, dc_ref, dn_ref,
             out_ref, carry_ref):
        j = pl.program_id(1)

        @pl.when(j == 0)
        def _():
            carry_ref[...] = jnp.zeros_like(carry_ref)

        ii = lax.broadcasted_iota(jnp.int32, (3 * BAND, 2 * BAND), 0)
        kk = lax.broadcasted_iota(jnp.int32, (3 * BAND, 2 * BAND), 1)
        mask = jnp.logical_and(jnp.logical_and(kk <= ii, kk >= ii - BAND),
                               jnp.logical_or(ii < 2 * BAND, j < nb2 - 1))
        lse3 = jnp.concatenate([lc_ref[...], ln_ref[...]], axis=0)
        dvec3 = jnp.concatenate([dc_ref[...], dn_ref[...]], axis=0)
        for h in range(HEADS):
            sl = slice(h * HEAD_DIM, (h + 1) * HEAD_DIM)
            k = k_ref[:, sl]
            v = v_ref[:, sl]
            q = jnp.concatenate([qc_ref[:, sl], qn_ref[:, sl]], axis=0)
            do = jnp.concatenate([doc_ref[:, sl], don_ref[:, sl]], axis=0)
            p = jnp.where(mask, jnp.exp(_dot_nt(q, k) * scale - lse3[:, h:h + 1]), 0.0)
            ds = (p * (_dot_nt(do, v) - dvec3[:, h:h + 1])).astype(BF16)
            dq3 = _dot(ds, k) * scale
            out_ref[0, 0:BAND, sl] = (carry_ref[:, sl] + dq3[:BAND]).astype(BF16)
            out_ref[0, BAND:, sl] = dq3[BAND:2 * BAND].astype(BF16)
            out_ref[1, :, sl] = (_dot_tn(ds, q) * scale).astype(BF16)
            out_ref[2, :, sl] = _dot_tn(p.astype(BF16), do).astype(BF16)
            carry_ref[:, sl] = dq3[2 * BAND:]

    two = (None, 2 * BAND, ATT_WIDTH)
    one = (None, BAND, ATT_WIDTH)
    stwo = (None, 2 * BAND, 128)
    sone = (None, BAND, 128)

    def nxt(j):
        return jnp.minimum(2 * j + 2, nb - 1)

    (out,), got = _call(
        body, (qkv, qkv, qkv, qkv, do, do, lse, lse, dvec, dvec), name=f"attn_bwd_g{g}", grid=(r, nb2),
        in_specs=[
            pl.BlockSpec(two, lambda rho, j: (rho, j, 0)),
            pl.BlockSpec(one, lambda rho, j: (rho, nxt(j), 0)),
            pl.BlockSpec(two, lambda rho, j: (rho, j, 1)),
            pl.BlockSpec(two, lambda rho, j: (rho, j, 2)),
            pl.BlockSpec(two, lambda rho, j: (rho, j, 0)),
            pl.BlockSpec(one, lambda rho, j: (rho, nxt(j), 0)),
            pl.BlockSpec(stwo, lambda rho, j: (rho, j, 0)),
            pl.BlockSpec(sone, lambda rho, j: (rho, nxt(j), 0)),
            pl.BlockSpec(stwo, lambda rho, j: (rho, j, 0)),
            pl.BlockSpec(sone, lambda rho, j: (rho, nxt(j), 0)),
        ],
        out_specs=(pl.BlockSpec((3, None, 2 * BAND, ATT_WIDTH), lambda rho, j: (0, rho, j, 0)),),
        out_shape=(jax.ShapeDtypeStruct((3, r, L, ATT_WIDTH), BF16),),
        scratch_shapes=[pltpu.VMEM((BAND, ATT_WIDTH), F32)],
        hosted=hosted)
    return out, got


LRU_T = 256


def _linear_scan(a, b, carry, reverse):
    T = a.shape[0]
    row8 = lax.broadcasted_iota(jnp.int32, a.shape, 0) & 7
    for s in (1, 2, 4):
        keep = (row8 < 8 - s) if reverse else (row8 >= s)
        shift = T - s if reverse else s
        b_sh = jnp.where(keep, pltpu.roll(b, shift, 0), 0.0)
        a_sh = jnp.where(keep, pltpu.roll(a, shift, 0), 1.0)
        b = a * b_sh + b
        a = a * a_sh
    tiles = [None] * (T // 8)
    order = range(T // 8 - 1, -1, -1) if reverse else range(T // 8)
    for k in order:
        y = b[8 * k:8 * k + 8] + a[8 * k:8 * k + 8] * carry
        tiles[k] = y
        carry = y[0:1] if reverse else y[7:8]
    return jnp.concatenate(tiles, axis=0), carry


def _gate_matmuls(ucb, w_ref, bias):
    parts = [_dot(ucb[:, j * 128:(j + 1) * 128], w_ref[j]) for j in range(8)]
    return jnp.concatenate(parts, axis=1) + bias


def _conv_fwd(u, u_prev8, cw_ref):
    T = u.shape[0]
    ue = jnp.concatenate([u_prev8, u], axis=0)
    uc = cw_ref[4:5, :] + cw_ref[0:1, :] * u
    for j in range(1, CONV_WIDTH):
        uc = uc + cw_ref[j:j + 1, :] * pltpu.roll(ue, j, 0)[8:8 + T]
    return uc


def _lru_fwd(rest, cw, wr2, wi2, hosted=None):
    S = rest.shape[0]
    T = LRU_T

    def body(u_ref, cw_ref, wr_ref, wi_ref, h_ref, a_ref, uc_ref, r_ref, ig_ref, mult_ref,
             ucar_ref, hcar_ref):
        c = pl.program_id(0)

        @pl.when(c == 0)
        def _():
            ucar_ref[...] = jnp.zeros_like(ucar_ref)
            hcar_ref[...] = jnp.zeros_like(hcar_ref)

        u = u_ref[...]
        uc = _conv_fwd(u, ucar_ref[...], cw_ref)
        ucar_ref[...] = u[T - 8:, :]
        ucb = uc.astype(BF16)
        r = jax.nn.sigmoid(_gate_matmuls(ucb, wr_ref, cw_ref[5:6, :]))
        ig = jax.nn.sigmoid(_gate_matmuls(ucb, wi_ref, cw_ref[6:7, :]))
        log_a = -LRU_C * r * _softplus_neg(cw_ref[7:8, :])
        a = jnp.exp(log_a)
        mult = jnp.sqrt(_neg_expm1(2.0 * log_a))
        b = mult * (ig * uc)
        a_ref[...] = a
        uc_ref[...] = uc
        r_ref[...] = r
        ig_ref[...] = ig
        mult_ref[...] = mult
        h, last = _linear_scan(a, b, hcar_ref[0:1, :], reverse=False)
        h_ref[...] = h
        hcar_ref[...] = jnp.broadcast_to(last, (8, D_MODEL))

    row_spec = pl.BlockSpec((T, D_MODEL), lambda c: (c, 0))
    row_shape = jax.ShapeDtypeStruct((S, D_MODEL), F32)
    return _call(
        body, (rest, cw, wr2, wi2), name="lru_fwd", grid=(S // T,),
        in_specs=[row_spec, pl.BlockSpec((8, D_MODEL), lambda c: (0, 0)),
                  pl.BlockSpec((8, 128, 128), lambda c: (0, 0, 0)),
                  pl.BlockSpec((8, 128, 128), lambda c: (0, 0, 0))],
        out_specs=(row_spec,) * 6, out_shape=(row_shape,) * 6,
        scratch_shapes=[pltpu.VMEM((8, D_MODEL), F32), pltpu.VMEM((8, D_MODEL), F32)],
        hosted=hosted)


def _lru_bwd(dhl, hl, a_all, gates, rest, cw, wr2, wi2, dproj, hosted=None):
    S = rest.shape[0]
    T = LRU_T
    nc = S // T

    def body(dh_ref, h_ref, hp_ref, a_ref, uc_ref, r_ref, ig_ref, mult_ref, u_ref, up_ref, cw_ref, wr_ref,
             wi_ref, _alias, du_ref, acc_ref, gwr_ref, gwi_ref, gcar_ref, acar_ref, dcar_ref):
        step = pl.program_id(0)
        c = nc - 1 - step

        @pl.when(step == 0)
        def _():
            acc_ref[...] = jnp.zeros_like(acc_ref)
            gwr_ref[...] = jnp.zeros_like(gwr_ref)
            gwi_ref[...] = jnp.zeros_like(gwi_ref)
            gcar_ref[...] = jnp.zeros_like(gcar_ref)
            acar_ref[...] = jnp.zeros_like(acar_ref)
            dcar_ref[...] = jnp.zeros_like(dcar_ref)

        row = lax.broadcasted_iota(jnp.int32, (T, D_MODEL), 0)
        u = u_ref[...]
        u_prev = jnp.where(c > 0, up_ref[...], 0.0)
        h_prev8 = jnp.where(c > 0, hp_ref[...], 0.0)
        a = a_ref[...]
        h = h_ref[...]
        uc, r, ig, mult = uc_ref[...], r_ref[...], ig_ref[...], mult_ref[...]
        ucb = uc.astype(BF16)
        sp = _softplus_neg(cw_ref[7:8, :])
        a_next = jnp.where(row < T - 1, pltpu.roll(a, T - 1, 0), acar_ref[0:1, :])
        G, first = _linear_scan(a_next, dh_ref[...], gcar_ref[0:1, :], reverse=True)
        gcar_ref[...] = jnp.broadcast_to(first, (8, D_MODEL))
        acar_ref[...] = jnp.broadcast_to(a[0:1, :], (8, D_MODEL))
        he = jnp.concatenate([h_prev8, h], axis=0)
        h_before = pltpu.roll(he, 1, 0)[8:8 + T]
        d_a = G * h_before
        d_mult = G * (ig * uc)
        d_ig = G * (mult * uc)
        duc = G * (mult * ig)
        d_log_a = d_a * a - d_mult * (a * a) / mult
        d_r = d_log_a * (-LRU_C * sp)
        d_sp = jnp.sum(d_log_a * (-LRU_C * r), axis=0, keepdims=True)
        dpre_r = d_r * r * (1.0 - r)
        dpre_i = d_ig * ig * (1.0 - ig)
        dprb = dpre_r.astype(BF16)
        dpib = dpre_i.astype(BF16)
        back = []
        for j in range(8):
            sl = slice(j * 128, (j + 1) * 128)
            back.append(_dot_nt(dprb[:, sl], wr_ref[j]) + _dot_nt(dpib[:, sl], wi_ref[j]))
            gwr_ref[j] += _dot_tn(ucb[:, sl], dprb[:, sl])
            gwi_ref[j] += _dot_tn(ucb[:, sl], dpib[:, sl])
        duc = duc + jnp.concatenate(back, axis=1)
        de = jnp.concatenate([duc, dcar_ref[...]], axis=0)
        du = cw_ref[0:1, :] * duc
        ue = jnp.concatenate([u_prev, u], axis=0)
        acc_ref[0:1, :] += jnp.sum(duc * u, axis=0, keepdims=True)
        for j in range(1, CONV_WIDTH):
            du = du + cw_ref[j:j + 1, :] * pltpu.roll(de, T + 8 - j, 0)[0:T]
            acc_ref[j:j + 1, :] += jnp.sum(duc * pltpu.roll(ue, j, 0)[8:8 + T], axis=0, keepdims=True)
        dcar_ref[...] = duc[0:8, :]
        acc_ref[4:5, :] += jnp.sum(duc, axis=0, keepdims=True)
        acc_ref[5:6, :] += jnp.sum(dpre_r, axis=0, keepdims=True)
        acc_ref[6:7, :] += jnp.sum(dpre_i, axis=0, keepdims=True)
        acc_ref[7:8, :] += d_sp
        du_ref[0] = du[:, 0:ATT_WIDTH].astype(BF16)
        du_ref[1] = du[:, ATT_WIDTH:].astype(BF16)

    def rev(step):
        return nc - 1 - step

    def prev8(step):
        return jnp.maximum(rev(step) * (T // 8) - 1, 0)

    row_spec = pl.BlockSpec((T, D_MODEL), lambda s: (rev(s), 0))
    prev_spec = pl.BlockSpec((8, D_MODEL), lambda s: (prev8(s), 0))
    vec_spec = pl.BlockSpec((8, D_MODEL), lambda s: (0, 0))
    w_spec = pl.BlockSpec((8, 128, 128), lambda s: (0, 0, 0))
    return _call(
        body, (dhl, hl, hl, a_all, *gates, rest, rest, cw, wr2, wi2, dproj), name="lru_bwd", grid=(nc,),
        in_specs=[row_spec, row_spec, prev_spec, row_spec, row_spec, row_spec, row_spec, row_spec,
                  row_spec, prev_spec, vec_spec, w_spec, w_spec, ANY_SPEC],
        out_specs=(pl.BlockSpec((2, T, ATT_WIDTH), lambda s: (4, rev(s), 0)), vec_spec, w_spec, w_spec),
        out_shape=(jax.ShapeDtypeStruct(dproj.shape, BF16), jax.ShapeDtypeStruct((8, D_MODEL), F32),
                   jax.ShapeDtypeStruct((8, 128, 128), F32), jax.ShapeDtypeStruct((8, 128, 128), F32)),
        scratch_shapes=[pltpu.VMEM((8, D_MODEL), F32), pltpu.VMEM((8, D_MODEL), F32),
                        pltpu.VMEM((8, D_MODEL), F32)],
        aliases={13: 0}, hosted=hosted)


OUT_TM = 256


def _out_fwd(a_gated, hl, rest, x, vec, w_pa, w_pb, w_o, hosted=None):
    S = x.shape[0]
    tm = OUT_TM

    def body(ag_ref, hl_ref, gl_ref, ma_ref, mb_ref, x_ref, v_ref, wpa_ref, wpb_ref, wo_ref,
             xn_ref, mix_ref, ya_ref, yb_ref, bg_ref):
        bg = (hl_ref[...] * _silu(gl_ref[...])).astype(BF16)
        ya = _dot(ag_ref[...], wpa_ref[...])
        yb = _dot(bg, wpb_ref[...])
        mix = (jax.nn.sigmoid(ma_ref[...]) * ya + jax.nn.sigmoid(mb_ref[...]) * yb).astype(BF16)
        out = _dot(mix, wo_ref[...])
        rstd = lax.rsqrt(jnp.mean(out * out, axis=-1, keepdims=True) + NORM_EPS)
        xn_ref[...] = x_ref[...] + v_ref[1:2, :] * ((out * rstd) * v_ref[0:1, :])
        mix_ref[...] = mix
        ya_ref[...] = ya.astype(BF16)
        yb_ref[...] = yb.astype(BF16)
        bg_ref[...] = bg

    def col(j):
        return pl.BlockSpec((tm, D_MODEL), lambda i: (i, j))

    def whole(shape):
        return pl.BlockSpec(shape, lambda i: (0, 0))

    row = col(0)
    bf = jax.ShapeDtypeStruct((S, D_MODEL), BF16)
    return _call(
        body, (a_gated, hl, rest, rest, rest, x, vec, w_pa, w_pb, w_o), name="out_fwd", grid=(S // tm,),
        in_specs=[pl.BlockSpec((tm, ATT_WIDTH), lambda i: (i, 0)), row, col(1), col(2), col(3), row,
                  whole((8, D_MODEL)), whole((ATT_WIDTH, D_MODEL)), whole((D_MODEL, D_MODEL)),
                  whole((D_MODEL, D_MODEL))],
        out_specs=(row, row, row, row, row),
        out_shape=(jax.ShapeDtypeStruct((S, D_MODEL), F32), bf, bf, bf, bf),
        hosted=hosted)


def _out_bwd(dxn, mix, ya, yb, hl, rest, o_att, vec, w_pa, w_pb, w_o, hosted=None):
    S = dxn.shape[0]
    tm = OUT_TM
    forth = [_perm_matrix(r, tm) for r in DILATIONS[1:]]

    def body(dxn_ref, mix_ref, ya_ref, yb_ref, hl_ref, gl_ref, ma_ref, mb_ref, ga_ref, o_ref, v_ref,
             wpa_ref, wpb_ref, wo_ref, f1_ref, f2_ref,
             dout_ref, dya_ref, dyb_ref, dhl_ref, dp_ref, acc_ref,
             do0_ref, do1_ref, do2_ref, dv0_ref, dv1_ref, dv2_ref):
        i = pl.program_id(0)

        @pl.when(i == 0)
        def _():
            acc_ref[...] = jnp.zeros_like(acc_ref)

        g_post = v_ref[0:1, :]
        gate = v_ref[1:2, :]
        dxn_v = dxn_ref[...]
        out = _dot(mix_ref[...], wo_ref[...])
        rstd = lax.rsqrt(jnp.mean(out * out, axis=-1, keepdims=True) + NORM_EPS)
        nrm = out * rstd
        acc_ref[0:1, :] += jnp.sum(dxn_v * (nrm * g_post), axis=0, keepdims=True)
        acc_ref[1:2, :] += jnp.sum(dxn_v * gate * nrm, axis=0, keepdims=True)
        dn = dxn_v * (gate * g_post)
        dout = (rstd * (dn - nrm * jnp.mean(dn * nrm, axis=-1, keepdims=True))).astype(BF16)
        dout_ref[...] = dout
        dmix = _dot_nt(dout, wo_ref[...])
        sa = jax.nn.sigmoid(ma_ref[...])
        sb = jax.nn.sigmoid(mb_ref[...])
        dya = (dmix * sa).astype(BF16)
        dyb = (dmix * sb).astype(BF16)
        dya_ref[...] = dya
        dyb_ref[...] = dyb
        dma = dmix * ya_ref[...].astype(F32) * (sa * (1.0 - sa))
        dmb = dmix * yb_ref[...].astype(F32) * (sb * (1.0 - sb))
        d_ag = _dot_nt(dya, wpa_ref[...])
        d_bg = _dot_nt(dyb, wpb_ref[...])
        gl = gl_ref[...]
        hl_v = hl_ref[...]
        dhl_ref[...] = d_bg * _silu(gl)
        dgl = d_bg * hl_v * _dsilu(gl)
        ga = ga_ref[...]
        o = o_ref[...]
        do = d_ag * _silu(ga)
        dga = d_ag * o * _dsilu(ga)
        lane = lax.broadcasted_iota(jnp.int32, (tm, 128), 1)
        dvec = jnp.zeros((tm, 128), F32)
        for h in range(HEADS):
            sl = slice(h * HEAD_DIM, (h + 1) * HEAD_DIM)
            dvec = jnp.where(lane == h, jnp.sum(do[:, sl] * o[:, sl], axis=1, keepdims=True), dvec)
        do_b = do.astype(BF16)
        do0_ref[0] = do_b
        dv0_ref[0] = dvec
        for do_ref, dv_ref, f_ref in ((do1_ref, dv1_ref, f1_ref), (do2_ref, dv2_ref, f2_ref)):
            _store_residues(do_ref, _dot(f_ref[...], do_b).astype(BF16))
            _store_residues(dv_ref, _permute_f32(f_ref[...], dvec, 3))
        dp_ref[0] = dgl[:, 0:ATT_WIDTH].astype(BF16)
        dp_ref[1] = dgl[:, ATT_WIDTH:].astype(BF16)
        dp_ref[2] = dma[:, 0:ATT_WIDTH].astype(BF16)
        dp_ref[3] = dma[:, ATT_WIDTH:].astype(BF16)
        dp_ref[4] = dmb[:, 0:ATT_WIDTH].astype(BF16)
        dp_ref[5] = dmb[:, ATT_WIDTH:].astype(BF16)
        dp_ref[6] = dga.astype(BF16)

    def col(j):
        return pl.BlockSpec((tm, D_MODEL), lambda i: (i, j))

    def whole(shape):
        return pl.BlockSpec(shape, lambda i: (0, 0))

    def res_spec(r, width):
        return pl.BlockSpec((r, tm // r, width), lambda i: (0, i, 0))

    row = col(0)
    att = pl.BlockSpec((tm, ATT_WIDTH), lambda i: (i, 0))
    bf = jax.ShapeDtypeStruct((S, D_MODEL), BF16)
    vmem = pl.BlockSpec(memory_space=pltpu.VMEM)
    return _call(
        body, (dxn, mix, ya, yb, hl, rest, rest, rest, rest, o_att, vec, w_pa, w_pb, w_o, *forth),
        name="out_bwd", grid=(S // tm,),
        in_specs=[row, row, row, row, row, col(1), col(2), col(3),
                  pl.BlockSpec((tm, ATT_WIDTH), lambda i: (i, 8)), att,
                  whole((8, D_MODEL)), whole((ATT_WIDTH, D_MODEL)), whole((D_MODEL, D_MODEL)),
                  whole((D_MODEL, D_MODEL)), vmem, vmem],
        out_specs=(row, row, row, row,
                   pl.BlockSpec((7, tm, ATT_WIDTH), lambda i: (0, i, 0)), whole((8, D_MODEL)),
                   *[res_spec(r, ATT_WIDTH) for r in DILATIONS], *[res_spec(r, 128) for r in DILATIONS]),
        out_shape=(bf, bf, bf, jax.ShapeDtypeStruct((S, D_MODEL), F32),
                   jax.ShapeDtypeStruct((DREST_CHUNKS, S, ATT_WIDTH), BF16),
                   jax.ShapeDtypeStruct((8, D_MODEL), F32),
                   *[jax.ShapeDtypeStruct((r, S // r, ATT_WIDTH), BF16) for r in DILATIONS],
                   *[jax.ShapeDtypeStruct((r, S // r, 128), F32) for r in DILATIONS]),
        hosted=hosted)


def _loss_head(y, target):
    S = y.shape[0]
    tm = 512

    def body(y_ref, t_ref, dy_ref, acc_ref):
        i = pl.program_id(0)

        @pl.when(i == 0)
        def _():
            acc_ref[...] = jnp.zeros_like(acc_ref)

        err = y_ref[...] - t_ref[...]
        dy_ref[...] = err * (1.0 / D_MODEL)
        part = jnp.sum(jnp.sum(err * err, axis=1, keepdims=True), axis=0, keepdims=True)
        acc_ref[...] += jnp.broadcast_to(part, acc_ref.shape)

    row = pl.BlockSpec((tm, D_MODEL), lambda i: (i, 0))
    return pl.pallas_call(
        body, name="loss_head", grid=(S // tm,),
        in_specs=[row, row],
        out_specs=(row, pl.BlockSpec((8, 128), lambda i: (0, 0))),
        out_shape=(jax.ShapeDtypeStruct((S, D_MODEL), F32), jax.ShapeDtypeStruct((8, 128), F32)),
        compiler_params=_params(("arbitrary",)))(y, target)


def _adamw(gsrcs, w, m, v, n_stack, name, rows=None):
    n_layers, R, C = w.shape
    assert len(gsrcs) == n_layers
    budget = 96 * 1024
    tr = rows
    if tr is None:
        tr = R
        while tr * C > budget and tr % 16 == 0:
            tr //= 2
    assert R % tr == 0 and (tr % 8 == 0 or tr == R)
    c1 = 1.0 - ADAM_B1 ** ADAM_STEP
    c2 = 1.0 - ADAM_B2 ** ADAM_STEP

    def body(*refs):
        g_refs = refs[:n_layers]
        w_ref, m_ref, v_ref, go_ref, d_ref, mo_ref, vo_ref = refs[n_layers:]

        def update(g_ref):
            if n_stack:
                g = g_ref[0].astype(F32)
                for s in range(1, n_stack):
                    g = g + g_ref[s].astype(F32)
            else:
                g = g_ref[...]
            m_new = ADAM_B1 * m_ref[...] + (1.0 - ADAM_B1) * g
            v_new = ADAM_B2 * v_ref[...] + (1.0 - ADAM_B2) * (g * g)
            m_hat = m_new / c1
            v_hat = v_new / c2
            go_ref[...] = g
            d_ref[...] = -ADAM_LR * (m_hat / (jnp.sqrt(v_hat) + ADAM_EPS) + ADAM_WD * w_ref[...])
            mo_ref[...] = m_new
            vo_ref[...] = v_new

        for layer in range(n_layers):
            pl.when(pl.program_id(0) == layer)(functools.partial(update, g_refs[layer]))

    def g_spec(layer):
        def rows_of(l, i):
            return jnp.where(l == layer, i, 0)
        if n_stack:
            return pl.BlockSpec((n_stack, tr, C), lambda l, i: (0, rows_of(l, i), 0))
        return pl.BlockSpec((tr, C), lambda l, i: (rows_of(l, i), 0))

    blk = pl.BlockSpec((None, tr, C), lambda l, i: (l, i, 0))
    shp = jax.ShapeDtypeStruct((n_layers, R, C), F32)
    return pl.pallas_call(
        body, name=name, grid=(n_layers, R // tr),
        in_specs=[g_spec(layer) for layer in range(n_layers)] + [blk, blk, blk],
        out_specs=(blk, blk, blk, blk), out_shape=(shp, shp, shp, shp),
        compiler_params=_params(("arbitrary", "arbitrary")))(*gsrcs, w, m, v)


def _pair_blocks(w):
    w = w.reshape(8, 2, 64, 64)
    z = jnp.zeros((8, 64, 64), w.dtype)
    top = jnp.concatenate([w[:, 0], z], axis=2)
    bot = jnp.concatenate([z, w[:, 1]], axis=2)
    return jnp.concatenate([top, bot], axis=1).astype(BF16)


def _unpair_blocks(g):
    return jnp.stack([g[:, :64, :64], g[:, 64:, 64:]], axis=1).reshape(16, 64, 64)


def _layer_params(layer, w_in_g, w_pa_g, w_pb_g, w_o_g, conv_w_full, conv_b, b_rg, b_ig, lru_lambda, w_rg,
                  w_ig, g_pre, g_post):
    cw = jnp.concatenate([conv_w_full, conv_b[None], b_rg[None], b_ig[None], lru_lambda[None]], axis=0)
    return dict(
        w_int=_weights_to_internal(w_in_g),
        w_pa=jnp.transpose(w_pa_g, (1, 0, 2)).reshape(ATT_WIDTH, D_MODEL),
        w_pb=w_pb_g.reshape(D_MODEL, D_MODEL), w_o=w_o_g.reshape(D_MODEL, D_MODEL),
        cw=cw, wr2=_pair_blocks(w_rg), wi2=_pair_blocks(w_ig),
        g_pre=g_pre, g_post=g_post, lru_lambda=lru_lambda)


def _layer_fwd(x, mod, p, host_proj=None, host_lru=None, host_out=None):
    zeros = jnp.zeros((5, D_MODEL), F32)
    vec_pre = jnp.concatenate([p["g_pre"][None], mod[1:2], mod[0:1], zeros], axis=0)
    (ht0, ht1, ht2, q0, q1, q2, rest), got_proj = _norm_proj_fwd(x, vec_pre, p["w_int"], hosted=host_proj)
    h_t = [ht0, ht1, ht2]
    qkv, outs, stats = [q0, q1, q2], [], []
    for g, r in enumerate(DILATIONS):
        o_g, st_g = _attn_fwd(qkv[g], g, r)
        outs.append(o_g)
        stats.append(st_g)
    o_att, a_gated, *lse = _attn_combine(outs, stats, rest)
    (hl, a_dec, *gates), got_lru = _lru_fwd(rest, p["cw"], p["wr2"], p["wi2"], hosted=host_lru)
    vec_post = jnp.concatenate([p["g_post"][None], mod[2:3], jnp.zeros((6, D_MODEL), F32)], axis=0)
    (x_new, mix, ya, yb, b_gated), got_out = _out_fwd(a_gated, hl, rest, x, vec_post, p["w_pa"], p["w_pb"],
                                                      p["w_o"], hosted=host_out)
    saved = dict(x=x, h_t=h_t, qkv=qkv, rest=rest, o_att=o_att, a_gated=a_gated, lse=lse, hl=hl, a_dec=a_dec,
                 gates=gates, mix=mix, ya=ya, yb=yb, b_gated=b_gated, vec_pre=vec_pre, vec_post=vec_post)
    return x_new, saved, got_proj, got_lru, got_out


def _mats_sources(g_pa, g_pb, g_o):
    return [jnp.transpose(g_pa.astype(BF16).reshape(ATT_WIDTH, N_DEV, 128), (1, 0, 2)),
            g_pb.astype(BF16).reshape(N_DEV, 128, D_MODEL), g_o.astype(BF16).reshape(N_DEV, 128, D_MODEL)]


LRU_SMALL = ("conv_b", "w_rg", "b_rg", "w_ig", "b_ig", "lru_lambda")
MOD_SMALL = ("b_mod", "g_pre", "g_post")


def _flat_rows(arrs):
    return jnp.concatenate([a.reshape(-1) for a in arrs]).reshape(-1, 128)


def _layer_bwd(dxn, sv, p, upper_small=None):
    a2a3 = ["a2a"] * 3
    (dout, dya, dyb, dhl, drest, acc_out, *do_dvec), _ = _out_bwd(
        dxn, sv["mix"], sv["ya"], sv["yb"], sv["hl"], sv["rest"], sv["o_att"], sv["vec_post"],
        p["w_pa"], p["w_pb"], p["w_o"])
    mats = _mats_sources(_wgrad_tn(sv["a_gated"], dya, "wgrad_pa"), _wgrad_tn(sv["b_gated"], dyb, "wgrad_pb"),
                         _wgrad_tn(sv["mix"], dout, "wgrad_o"))
    (drest, acc_lru, gwr2, gwi2), _ = _lru_bwd(
        dhl, sv["hl"], sv["a_dec"], sv["gates"], sv["rest"], p["cw"], p["wr2"], p["wi2"], drest)
    small = dict(
        conv_w=acc_lru[0:4], conv_b=acc_lru[4], b_rg=acc_lru[5], b_ig=acc_lru[6],
        lru_lambda=acc_lru[7] * (-jax.nn.sigmoid(-p["lru_lambda"])),
        w_rg=_unpair_blocks(gwr2), w_ig=_unpair_blocks(gwi2))
    hosts = [(mats, a2a3), None, None]
    if upper_small is not None:
        lru_vec = _flat_rows([jnp.stack([small[n], upper_small[n]]) for n in LRU_SMALL + ("conv_w",)])
        hosts[1] = ([lru_vec], ["ag"])
    dqkv, got_attn = [], []
    for g, r in enumerate(DILATIONS):
        d_g, got = _attn_bwd(sv["qkv"][g], do_dvec[g], sv["lse"][g], do_dvec[3 + g], g, r, hosted=hosts[g])
        dqkv.append(d_g)
        got_attn.append(got)
    parts = [_proj_wgrad_group(sv["h_t"][g], dqkv[g], r, f"proj_wgrad_g{g}") for g, r in enumerate(DILATIONS)]
    parts.append(_proj_wgrad_part(sv["h_t"][0], drest, 2, 4, "proj_wgrad_u"))
    parts.append(_proj_wgrad_part(sv["h_t"][0], drest, 7, 0, "proj_wgrad_rest"))
    (dx, acc_pre), got_in = _proj_dgrad_norm_bwd(
        dqkv, drest, p["w_int"], sv["x"], dxn, sv["vec_pre"], hosted=([_grads_to_shards(parts)], ["a2a"]))
    small.update(dmod=jnp.concatenate([acc_pre[0], acc_pre[1], acc_out[0]]), g_pre=acc_pre[2],
                 g_post=acc_out[1])
    return dx, small, dict(mats=got_attn[0], w_in=got_in[0], lru_small=got_attn[1])


SMALL_NAMES = ("b_mod", "g_pre", "conv_b", "w_rg", "b_rg", "w_ig", "b_ig", "lru_lambda", "g_post")


def kernel(x, c, w_mod, b_mod, g_pre, w_in, conv_w, conv_b, w_rg, b_rg, w_ig, b_ig, lru_lambda, w_pa, w_pb, w_o, g_post, loss_target, m_w_mod, m_b_mod, m_g_pre, m_w_in, m_conv_w, m_conv_b, m_w_rg, m_b_rg, m_w_ig, m_b_ig, m_lru_lambda, m_w_pa, m_w_pb, m_w_o, m_g_post, v_w_mod, v_b_mod, v_g_pre, v_w_in, v_conv_w, v_conv_b, v_w_rg, v_b_rg, v_w_ig, v_b_ig, v_lru_lambda, v_w_pa, v_w_pb, v_w_o, v_g_post):
    W = dict(w_mod=w_mod, b_mod=b_mod, g_pre=g_pre, w_in=w_in, conv_w=conv_w, conv_b=conv_b, w_rg=w_rg,
             b_rg=b_rg, w_ig=w_ig, b_ig=b_ig, lru_lambda=lru_lambda, w_pa=w_pa, w_pb=w_pb, w_o=w_o,
             g_post=g_post)
    M = dict(w_mod=m_w_mod, b_mod=m_b_mod, g_pre=m_g_pre, w_in=m_w_in, conv_w=m_conv_w, conv_b=m_conv_b,
             w_rg=m_w_rg, b_rg=m_b_rg, w_ig=m_w_ig, b_ig=m_b_ig, lru_lambda=m_lru_lambda, w_pa=m_w_pa,
             w_pb=m_w_pb, w_o=m_w_o, g_post=m_g_post)
    V = dict(w_mod=v_w_mod, b_mod=v_b_mod, g_pre=v_g_pre, w_in=v_w_in, conv_w=v_conv_w, conv_b=v_conv_b,
             w_rg=v_w_rg, b_rg=v_b_rg, w_ig=v_w_ig, b_ig=v_b_ig, lru_lambda=v_lru_lambda, w_pa=v_w_pa,
             w_pb=v_w_pb, w_o=v_w_o, g_post=v_g_post)
    S = x.shape[1]
    me = 4 * lax.axis_index("x") + 2 * lax.axis_index("y") + lax.axis_index("c")
    n_mod = w_mod.shape[2]
    n_in = w_in.shape[2]
    n_conv = conv_w.shape[2]

    c_rows = jnp.broadcast_to(c, (8, D_MODEL))
    w_in_b, w_pa_b, w_pb_b, w_o_b = (t.astype(BF16) for t in (w_in, w_pa, w_pb, w_o))
    g_c, g_win0, g_wpa0, g_wpb0, g_wo0, g_cw = _gather_two_level(
        [c_rows, w_in_b[0], w_pa_b[0], w_pb_b[0], w_o_b[0], conv_w], "gather_weights")
    c_all = g_c[:, 0, :]
    c_pad = jnp.concatenate([c_all, jnp.zeros((8, D_MODEL), F32)], axis=0)
    conv_w_full = jnp.transpose(g_cw, (1, 2, 0, 3)).reshape(2, CONV_WIDTH, D_MODEL)

    mod_cols = _mod_fwd(c_pad, w_mod)
    mod_src = jnp.transpose(mod_cols[:, :8], (1, 0, 2))
    mod_src = jnp.concatenate([mod_src, jnp.zeros((8, 6, n_mod), F32)], axis=1)
    (mod_got,) = _exchange([mod_src], ["a2a"], "scatter_mod")
    mod = jnp.transpose(mod_got[:, :2], (1, 0, 2)).reshape(2, 3 * D_MODEL) + b_mod

    def layer_params(layer, w_in_g, w_pa_g, w_pb_g, w_o_g):
        return _layer_params(layer, w_in_g, w_pa_g, w_pb_g, w_o_g, conv_w_full[layer], conv_b[layer],
                             b_rg[layer], b_ig[layer], lru_lambda[layer], w_rg[layer], w_ig[layer],
                             g_pre[layer], g_post[layer])

    layers = [layer_params(0, [g_win0], g_wpa0, g_wpb0, g_wo0), None]
    act, sv0, (g_win1,), _, (g_wpa1, g_wpb1, g_wo1) = _layer_fwd(
        x[0], mod[0].reshape(3, D_MODEL), layers[0],
        host_proj=([w_in_b[1]], ["ag"]), host_out=([w_pa_b[1], w_pb_b[1], w_o_b[1]], ["ag"] * 3))
    layers[1] = layer_params(1, [g_win1], g_wpa1, g_wpb1, g_wo1)
    act, sv1, _, _, _ = _layer_fwd(act, mod[1].reshape(3, D_MODEL), layers[1])
    dy, loss_acc = _loss_head(act, loss_target[0])

    dy, small1, got1 = _layer_bwd(dy, sv1, layers[1])
    dy, small0, got0 = _layer_bwd(dy, sv0, layers[0], upper_small=small1)
    grad_x = dy[None]
    r_pa, r_pb, r_o = ([got0["mats"][k], got1["mats"][k]] for k in range(3))
    r_in = [got0["w_in"], got1["w_in"]]
    grads = [small0, small1]

    def stack2(name):
        return jnp.stack([grads[0][name], grads[1][name]], axis=0)

    dmod = stack2("dmod")
    dmod_src = jnp.transpose(dmod.reshape(2, N_DEV, n_mod), (1, 0, 2))
    dmod_src = jnp.concatenate([dmod_src, jnp.zeros((8, 6, n_mod), F32)], axis=1)
    mod_vec = _flat_rows([dmod, stack2("g_pre"), stack2("g_post"), loss_acc])
    r_dmod, r_mod_small = _exchange([dmod_src, mod_vec], ["a2a", "ag"], "exchange_grads")
    mod_rows = mod_vec.shape[0] - loss_acc.shape[0]
    loss = 0.5 * jnp.sum(r_mod_small[:, mod_rows, 0]) / D_MODEL
    (r_lru_small,) = got0["lru_small"]

    res = {}
    res["w_in"] = _adamw(r_in, w_in, m_w_in, v_w_in, 8, "adamw_w_in")
    res["w_pa"] = _adamw(r_pa, w_pa, m_w_pa, v_w_pa, 8, "adamw_w_pa")
    res["w_pb"] = _adamw(r_pb, w_pb, m_w_pb, v_w_pb, 8, "adamw_w_pb")
    res["w_o"] = _adamw(r_o, w_o, m_w_o, v_w_o, 8, "adamw_w_o")
    dmod_all = jnp.transpose(r_dmod[:, :2], (1, 0, 2))
    gw_mod = _mod_wgrad(jnp.transpose(c_all), dmod_all)
    res["w_mod"] = _adamw([gw_mod[0], gw_mod[1]], w_mod, m_w_mod, v_w_mod, 0, "adamw_w_mod")
    for names_rep, stack, tag in ((LRU_SMALL, r_lru_small, "lru"), (MOD_SMALL, r_mod_small, "mod")):
        rows = sum(W[n].size for n in names_rep) // 128
        rep = _adamw([stack], *(_flat_rows([src[n] for n in names_rep])[None] for src in (W, M, V)), 8,
                     f"adamw_small_{tag}", rows=rows // 2 if rows % 16 == 0 else rows)
        off = 0
        for name in names_rep:
            size = W[name].size
            res[name] = tuple(t.reshape(-1)[off:off + size].reshape(W[name].shape) for t in rep)
            off += size
    lru_rows = sum(W[n].size for n in LRU_SMALL) // 128
    conv_stack = r_lru_small[:, lru_rows:].reshape(8, 2, CONV_WIDTH, D_MODEL)
    conv_stack = lax.dynamic_slice_in_dim(conv_stack, me * n_conv, n_conv, axis=3).reshape(8, 8, n_conv)
    res["conv_w"] = _adamw([conv_stack], conv_w.reshape(1, 8, n_conv), m_conv_w.reshape(1, 8, n_conv),
                           v_conv_w.reshape(1, 8, n_conv), 8, "adamw_conv_w")

    names = ("w_mod", "b_mod", "g_pre", "w_in", "conv_w", "conv_b", "w_rg", "b_rg", "w_ig", "b_ig",
             "lru_lambda", "w_pa", "w_pb", "w_o", "g_post")
    outs = [loss, grad_x]
    for k in range(4):
        outs.extend(res[n][k].reshape(W[n].shape) for n in names)
    return tuple(outs)
```

```python
import functools

import jax
import jax.numpy as jnp
import numpy as np
from jax import lax
from jax.experimental import pallas as pl
from jax.experimental.pallas import tpu as pltpu

F32 = jnp.float32
BF16 = jnp.bfloat16

N_DEV = 8
D_MODEL = 1024
HEAD_DIM = 128
HEADS = 4
ATT_WIDTH = HEADS * HEAD_DIM
DILATIONS = (1, 4, 16)
BAND = 128
N_CHUNKS = 18
QKV_CHUNKS = 9
CONV_WIDTH = 4
LRU_C = 8.0
NORM_EPS = 1e-6
NEG_INF = -1e30
ADAM_LR = 0.001
ADAM_B1 = 0.9
ADAM_B2 = 0.999
ADAM_EPS = 1e-08
ADAM_WD = 0.01
ADAM_STEP = 10
CHUNK_PERM = (0, 3, 6, 1, 4, 7, 2, 5, 8, 10, 11, 12, 13, 14, 15, 16, 17, 9)
DREST_CHUNKS = 10
VMEM_LIMIT = 56 * 1024 * 1024
MESH_AXES = ("x", "y", "c")


def _params(sem=None):
    return pltpu.CompilerParams(dimension_semantics=sem, vmem_limit_bytes=VMEM_LIMIT)


def _silu(x):
    return x * jax.nn.sigmoid(x)


def _dsilu(x):
    s = jax.nn.sigmoid(x)
    return s * (1.0 + x * (1.0 - s))


def _neg_expm1(x):
    series = -x * (1.0 + x * (0.5 + x * (1.0 / 6.0 + x * (1.0 / 24.0 + x * (1.0 / 120.0)))))
    return jnp.where(x > -0.05, series, 1.0 - jnp.exp(x))


def _softplus_neg(lam):
    z = jnp.exp(-jnp.abs(lam))
    small = z * (1.0 - z * (0.5 - z * (1.0 / 3.0 - z * 0.25)))
    log1p_z = jnp.where(z < 1e-2, small, jnp.log(1.0 + z))
    return jnp.maximum(-lam, 0.0) + log1p_z


def _dot(a, b):
    return jnp.dot(a, b, preferred_element_type=F32)


def _dot_nt(a, b):
    return lax.dot_general(a, b, (((1,), (1,)), ((), ())), preferred_element_type=F32)


def _dot_tn(a, b):
    return lax.dot_general(a, b, (((0,), (0,)), ((), ())), preferred_element_type=F32)


def _perm_matrix(r, rows, transpose=False):
    n = rows // r
    p = np.zeros((rows, rows), np.float32)
    dst = np.arange(rows)
    p[dst, (dst % n) * r + dst // n] = 1.0
    return jnp.asarray(p.T if transpose else p, dtype=BF16)


def _permute_f32(p, x, pieces):
    part = x.astype(BF16)
    acc = _dot(p, part)
    for _ in range(pieces - 1):
        x = x - part.astype(F32)
        part = x.astype(BF16)
        acc = acc + _dot(p, part)
    return acc


def _rows_of_residues(ref, lead=()):
    r = ref.shape[len(lead)]
    if r == 1:
        return ref[lead + (0,)]
    return jnp.concatenate([ref[lead + (rho,)] for rho in range(r)], axis=0)


def _store_residues(ref, x, lead=()):
    r = ref.shape[len(lead)]
    n = x.shape[0] // r
    for rho in range(r):
        ref[lead + (rho,)] = x[rho * n:(rho + 1) * n]


ANY_SPEC = pl.BlockSpec(memory_space=pl.ANY)


def _exchange_shapes(arrs, modes):
    return [jax.ShapeDtypeStruct(((N_DEV,) + a.shape) if mode == "ag" else a.shape, a.dtype)
            for a, mode in zip(arrs, modes)]


def _exchange_sems(n):
    return [pltpu.SemaphoreType.DMA((7 * n,)), pltpu.SemaphoreType.DMA((7 * n,)),
            pltpu.SemaphoreType.DMA((n,))]


def _exchange_copies(ins, outs, modes, sems):
    n = len(ins)
    send_sems, recv_sems, local_sems = sems
    x, y, c = lax.axis_index("x"), lax.axis_index("y"), lax.axis_index("c")
    me = 4 * x + 2 * y + c

    def src_for(a, dev):
        return ins[a] if modes[a] == "ag" else ins[a].at[dev]

    local = [pltpu.make_async_copy(src_for(a, me), outs[a].at[me], local_sems.at[a]) for a in range(n)]
    sends, arrivals = [], []
    for k in range(1, N_DEV):
        px = 1 - x if (k >> 2) & 1 else x
        py = 1 - y if (k >> 1) & 1 else y
        pc = 1 - c if k & 1 else c
        peer = 4 * px + 2 * py + pc
        for a in range(n):
            s = (k - 1) * n + a
            for dst, group in ((me, sends), (peer, arrivals)):
                group.append(pltpu.make_async_remote_copy(
                    src_ref=src_for(a, peer), dst_ref=outs[a].at[dst],
                    send_sem=send_sems.at[s], recv_sem=recv_sems.at[s],
                    device_id=(px, py, pc), device_id_type=pl.DeviceIdType.MESH))
    return local, sends, arrivals


def _exchange_start(copies):
    local, sends, _ = copies
    for cp in local + sends:
        cp.start()


def _exchange_wait(copies):
    local, sends, arrivals = copies
    for cp in arrivals:
        cp.wait_recv()
    for cp in sends:
        cp.wait_send()
    for cp in local:
        cp.wait()


def _exchange(arrs, modes, name):
    n = len(arrs)

    def body(*refs):
        copies = _exchange_copies(refs[:n], refs[n:2 * n], modes, refs[2 * n:])
        _exchange_start(copies)
        _exchange_wait(copies)

    outs = pl.pallas_call(
        body, name=name, out_shape=tuple(_exchange_shapes(arrs, modes)),
        in_specs=[ANY_SPEC] * n, out_specs=tuple([ANY_SPEC] * n),
        scratch_shapes=_exchange_sems(n),
    )(*arrs)
    return list(outs)


def _gather_two_level(arrs, name):
    n = len(arrs)

    def body(*refs):
        ins, outs = refs[:n], refs[n:2 * n]
        send_sems, recv_sems, local_sems = refs[2 * n:]
        x, y, c = lax.axis_index("x"), lax.axis_index("y"), lax.axis_index("c")
        me, sibling = (x, y, c), (x, y, 1 - c)
        chips = [(1 - x, y), (x, 1 - y), (1 - x, 1 - y)]

        def copy(a, k, block, to, src=None):
            slot = 4 * block[0] + 2 * block[1] + block[2]
            return pltpu.make_async_remote_copy(
                src_ref=outs[a].at[slot] if src is None else src, dst_ref=outs[a].at[slot],
                send_sem=send_sems.at[7 * a + k], recv_sem=recv_sems.at[7 * a + k],
                device_id=to, device_id_type=pl.DeviceIdType.MESH)

        mine = [pltpu.make_async_copy(ins[a], outs[a].at[4 * x + 2 * y + c], local_sems.at[a])
                for a in range(n)]
        first = []
        for a in range(n):
            first.append(copy(a, 0, me, sibling, src=ins[a]))
            first += [copy(a, 1 + j, me, (*chip, c), src=ins[a]) for j, chip in enumerate(chips)]
        for cp in mine + first:
            cp.start()
        passed = []
        for j, chip in enumerate(chips):
            for a in range(n):
                copy(a, 1 + j, (*chip, c), me).wait_recv()
                passed.append(copy(a, 4 + j, (*chip, c), sibling))
                passed[-1].start()
        for a in range(n):
            copy(a, 0, sibling, me).wait_recv()
            for j, chip in enumerate(chips):
                copy(a, 4 + j, (*chip, 1 - c), me).wait_recv()
        for cp in first + passed:
            cp.wait_send()
        for cp in mine:
            cp.wait()

    outs = pl.pallas_call(
        body, name=name, out_shape=tuple(_exchange_shapes(arrs, ["ag"] * n)),
        in_specs=[ANY_SPEC] * n, out_specs=tuple([ANY_SPEC] * n),
        scratch_shapes=_exchange_sems(n),
    )(*arrs)
    return list(outs)


SHARD_COLS = N_CHUNKS * ATT_WIDTH // N_DEV


def _weights_full(shards):
    return jnp.transpose(shards, (1, 0, 2)).reshape(D_MODEL, N_CHUNKS * ATT_WIDTH)


def _wcols(w_ref, k):
    orig = CHUNK_PERM[k]
    return w_ref[:, orig * ATT_WIDTH:(orig + 1) * ATT_WIDTH]


def _grads_to_shards(parts):
    offsets = np.cumsum([0] + [p.shape[1] for p in parts])
    chunks = []
    for orig in range(N_CHUNKS):
        col = ATT_WIDTH * CHUNK_PERM.index(orig)
        part = int(np.searchsorted(offsets, col, side="right")) - 1
        chunks.append(parts[part][:, col - int(offsets[part]):col - int(offsets[part]) + ATT_WIDTH])
    full = jnp.concatenate(chunks, axis=1)
    return jnp.transpose(full.reshape(D_MODEL, N_DEV, SHARD_COLS), (1, 0, 2))


def _call(body, args, *, name, grid, in_specs, out_specs, out_shape, scratch_shapes=(), aliases=None,
          hosted=None):
    sem = ("arbitrary",) * len(grid)
    if hosted is None:
        outs = pl.pallas_call(
            body, name=name, grid=grid, in_specs=in_specs, out_specs=tuple(out_specs),
            out_shape=tuple(out_shape), scratch_shapes=list(scratch_shapes),
            input_output_aliases=aliases or {}, compiler_params=_params(sem))(*args)
        return list(outs), []
    x_arrs, modes = hosted
    n_in, n_out, n_scr, nx = len(args), len(out_shape), len(scratch_shapes), len(x_arrs)

    def wrapped(*refs):
        ins, x_ins = refs[:n_in], refs[n_in:n_in + nx]
        outs = refs[n_in + nx:n_in + nx + n_out]
        x_outs = refs[n_in + nx + n_out:n_in + 2 * nx + n_out]
        scr = refs[n_in + 2 * nx + n_out:n_in + 2 * nx + n_out + n_scr]
        sems = refs[n_in + 2 * nx + n_out + n_scr:]
        first = pl.program_id(0) == 0
        last = pl.program_id(0) == grid[0] - 1
        for axis in range(1, len(grid)):
            first = jnp.logical_and(first, pl.program_id(axis) == 0)
            last = jnp.logical_and(last, pl.program_id(axis) == grid[axis] - 1)

        @pl.when(first)
        def _():
            _exchange_start(_exchange_copies(x_ins, x_outs, modes, sems))

        body(*ins, *outs, *scr)

        @pl.when(last)
        def _():
            _exchange_wait(_exchange_copies(x_ins, x_outs, modes, sems))

    outs = pl.pallas_call(
        wrapped, name=name, grid=grid, in_specs=list(in_specs) + [ANY_SPEC] * nx,
        out_specs=tuple(out_specs) + tuple([ANY_SPEC] * nx),
        out_shape=tuple(out_shape) + tuple(_exchange_shapes(x_arrs, modes)),
        scratch_shapes=list(scratch_shapes) + _exchange_sems(nx),
        input_output_aliases=aliases or {}, compiler_params=_params(sem))(*args, *x_arrs)
    return list(outs[:n_out]), list(outs[n_out:])


def _mod_fwd(c_pad, w_mod):
    def body(c_ref, w_ref, o_ref):
        sc = _silu(c_ref[...]).astype(BF16)
        for layer in range(2):
            o_ref[layer] = _dot(sc, w_ref[layer].astype(BF16))

    return pl.pallas_call(
        body, name="mod_fwd", out_shape=jax.ShapeDtypeStruct((2, 16, w_mod.shape[2]), F32),
        compiler_params=_params())(c_pad, w_mod)


def _mod_wgrad(c_t, dmod):
    n_cols = dmod.shape[2]

    def body(c_ref, d_ref, o_ref):
        sc = _silu(c_ref[...]).astype(BF16).astype(F32)
        for layer in range(2):
            dm = d_ref[layer].astype(BF16).astype(F32)
            acc = sc[:, 0:1] * dm[0:1, :]
            for b in range(1, N_DEV):
                acc = acc + sc[:, b:b + 1] * dm[b:b + 1, :]
            o_ref[layer] = acc

    return pl.pallas_call(
        body, name="mod_wgrad", out_shape=jax.ShapeDtypeStruct((2, D_MODEL, n_cols), F32),
        compiler_params=_params())(c_t, dmod)


GROUP_COLS = 3 * ATT_WIDTH
PROJ_TM = 256


def _norm_proj_fwd(x, vec, w_int, hosted=None):
    S = x.shape[0]
    tm = PROJ_TM
    perms = [_perm_matrix(r, tm) for r in DILATIONS[1:]]

    def body(x_ref, v_ref, w_ref, p1_ref, p2_ref, ht0_ref, ht1_ref, ht2_ref, q0_ref, q1_ref, q2_ref,
             rest_ref):
        xv = x_ref[...]
        rstd = lax.rsqrt(jnp.mean(xv * xv, axis=-1, keepdims=True) + NORM_EPS)
        hf = ((xv * rstd) * v_ref[0:1, :]) * (1.0 + v_ref[1:2, :]) + v_ref[2:3, :]
        h = hf.astype(BF16)
        for g, (q_ref, ht_ref, p_ref) in enumerate(((q0_ref, ht0_ref, None), (q1_ref, ht1_ref, p1_ref),
                                                    (q2_ref, ht2_ref, p2_ref))):
            rows_f = hf if p_ref is None else _dot(p_ref[...], h)
            ht_ref[...] = rows_f.T.astype(BF16)
            rows = rows_f.astype(BF16)
            n = tm // q_ref.shape[0]
            for t in range(3):
                res = _dot(rows, _wcols(w_ref, 3 * g + t)).astype(BF16)
                for rho in range(q_ref.shape[0]):
                    q_ref[rho, :, t * ATT_WIDTH:(t + 1) * ATT_WIDTH] = res[rho * n:(rho + 1) * n]
        split = (QKV_CHUNKS - 1) * ATT_WIDTH
        rest_ref[:, :split] = _dot(h, w_ref[:, (QKV_CHUNKS + 1) * ATT_WIDTH:])
        rest_ref[:, split:] = _dot(h, w_ref[:, QKV_CHUNKS * ATT_WIDTH:(QKV_CHUNKS + 1) * ATT_WIDTH])

    n_rest = QKV_CHUNKS * ATT_WIDTH
    whole = pl.BlockSpec(memory_space=pltpu.VMEM)
    ht_spec = pl.BlockSpec((D_MODEL, tm), lambda i: (0, i))
    ht_shape = jax.ShapeDtypeStruct((D_MODEL, S), BF16)
    return _call(
        body, (x, vec, w_int, *perms), name="norm_proj_fwd", grid=(S // tm,),
        in_specs=[pl.BlockSpec((tm, D_MODEL), lambda i: (i, 0)),
                  pl.BlockSpec((8, D_MODEL), lambda i: (0, 0)), whole, whole, whole],
        out_specs=(ht_spec, ht_spec, ht_spec,
                   *[pl.BlockSpec((r, tm // r, GROUP_COLS), lambda i: (0, i, 0)) for r in DILATIONS],
                   pl.BlockSpec((tm, n_rest), lambda i: (i, 0))),
        out_shape=(ht_shape, ht_shape, ht_shape,
                   *[jax.ShapeDtypeStruct((r, S // r, GROUP_COLS), BF16) for r in DILATIONS],
                   jax.ShapeDtypeStruct((S, n_rest), F32)),
        hosted=hosted)


def _proj_dgrad_norm_bwd(dqkv, drest, w_int, x, dxn, vec, hosted=None):
    S = x.shape[0]
    tm = PROJ_TM
    perms = [_perm_matrix(r, tm, transpose=True) for r in DILATIONS[1:]]

    def body(d0_ref, d1_ref, d2_ref, dr_ref, w_ref, p1_ref, p2_ref, x_ref, dxn_ref, v_ref, dx_ref, acc_ref):
        i = pl.program_id(0)

        @pl.when(i == 0)
        def _():
            acc_ref[...] = jnp.zeros_like(acc_ref)

        wcols = functools.partial(_wcols, w_ref)
        dhv = None
        for g, (d_ref, p_ref) in enumerate(((d0_ref, None), (d1_ref, p1_ref), (d2_ref, p2_ref))):
            for t in range(3):
                d = _rows_of_residues(d_ref, (t,))
                if p_ref is not None:
                    d = _dot(p_ref[...], d).astype(BF16)
                part = _dot_nt(d, wcols(3 * g + t))
                dhv = part if dhv is None else dhv + part
        for t in range(7):
            dhv = dhv + _dot_nt(dr_ref[t], wcols(11 + t))
        for t in range(2):
            dhv = dhv + _dot_nt(dr_ref[8 + t], wcols(9 + t))

        xv = x_ref[...]
        g = v_ref[0:1, :]
        sc1 = 1.0 + v_ref[1:2, :]
        rstd = lax.rsqrt(jnp.mean(xv * xv, axis=-1, keepdims=True) + NORM_EPS)
        xhat = xv * rstd
        acc_ref[0:1, :] += jnp.sum(dhv, axis=0, keepdims=True)
        acc_ref[1:2, :] += jnp.sum(dhv * (xhat * g), axis=0, keepdims=True)
        acc_ref[2:3, :] += jnp.sum(dhv * xhat * sc1, axis=0, keepdims=True)
        dxhat = dhv * (g * sc1)
        dx = rstd * (dxhat - xhat * jnp.mean(dxhat * xhat, axis=-1, keepdims=True))
        dx_ref[...] = dx + dxn_ref[...]

    row = pl.BlockSpec((tm, D_MODEL), lambda i: (i, 0))
    vec_spec = pl.BlockSpec((8, D_MODEL), lambda i: (0, 0))
    whole = pl.BlockSpec(memory_space=pltpu.VMEM)
    return _call(
        body, (*dqkv, drest, w_int, *perms, x, dxn, vec), name="proj_dgrad_norm_bwd", grid=(S // tm,),
        in_specs=[*[pl.BlockSpec((3, r, tm // r, ATT_WIDTH), lambda i: (0, 0, i, 0)) for r in DILATIONS],
                  pl.BlockSpec((DREST_CHUNKS, tm, ATT_WIDTH), lambda i: (0, i, 0)),
                  whole, whole, whole, row, row, vec_spec],
        out_specs=(row, vec_spec),
        out_shape=(jax.ShapeDtypeStruct((S, D_MODEL), F32), jax.ShapeDtypeStruct((8, D_MODEL), F32)),
        hosted=hosted)


def _proj_wgrad_part(h_t, d, n_chunks, chunk_block, name):
    S = h_t.shape[1]
    ts = 512
    n_steps = S // ts

    def body(h_ref, d_ref, o_ref, acc_ref):
        i = pl.program_id(0)

        @pl.when(i == 0)
        def _():
            acc_ref[...] = jnp.zeros_like(acc_ref)

        ht = h_ref[...]
        for t in range(n_chunks):
            acc_ref[:, t * ATT_WIDTH:(t + 1) * ATT_WIDTH] += _dot(ht, d_ref[t])

        @pl.when(i == n_steps - 1)
        def _():
            o_ref[...] = acc_ref[...].astype(BF16)

    return pl.pallas_call(
        body, name=name, grid=(n_steps,),
        in_specs=[pl.BlockSpec((D_MODEL, ts), lambda i: (0, i)),
                  pl.BlockSpec((n_chunks, ts, ATT_WIDTH), lambda i: (chunk_block, i, 0))],
        out_specs=pl.BlockSpec((D_MODEL, n_chunks * ATT_WIDTH), lambda i: (0, 0)),
        out_shape=jax.ShapeDtypeStruct((D_MODEL, n_chunks * ATT_WIDTH), BF16),
        scratch_shapes=[pltpu.VMEM((D_MODEL, n_chunks * ATT_WIDTH), F32)],
        compiler_params=_params(("arbitrary",)))(h_t, d)


def _proj_wgrad_group(h_t, d, r, name):
    S = h_t.shape[1]
    ts = 512
    n_steps = S // ts
    n = PROJ_TM // r

    def body(h_ref, d_ref, o_ref, acc_ref):
        i = pl.program_id(0)

        @pl.when(i == 0)
        def _():
            acc_ref[...] = jnp.zeros_like(acc_ref)

        ht = h_ref[...]
        for t in range(3):
            rows = jnp.concatenate([d_ref[t, rho, s * n:(s + 1) * n] for s in range(ts // PROJ_TM)
                                    for rho in range(r)], axis=0) if r > 1 else d_ref[t, 0]
            acc_ref[:, t * ATT_WIDTH:(t + 1) * ATT_WIDTH] += _dot(ht, rows)

        @pl.when(i == n_steps - 1)
        def _():
            o_ref[...] = acc_ref[...].astype(BF16)

    return pl.pallas_call(
        body, name=name, grid=(n_steps,),
        in_specs=[pl.BlockSpec((D_MODEL, ts), lambda i: (0, i)),
                  pl.BlockSpec((3, r, ts // r, ATT_WIDTH), lambda i: (0, 0, i, 0))],
        out_specs=pl.BlockSpec((D_MODEL, GROUP_COLS), lambda i: (0, 0)),
        out_shape=jax.ShapeDtypeStruct((D_MODEL, GROUP_COLS), BF16),
        scratch_shapes=[pltpu.VMEM((D_MODEL, GROUP_COLS), F32)],
        compiler_params=_params(("arbitrary",)))(h_t, d)


def _wgrad_tn(a, b, name):
    S, M = a.shape
    N = b.shape[1]
    ts = 1024

    def body(a_ref, b_ref, o_ref):
        i = pl.program_id(0)
        part = _dot_tn(a_ref[...], b_ref[...])

        @pl.when(i == 0)
        def _():
            o_ref[...] = part

        @pl.when(i > 0)
        def _():
            o_ref[...] += part

    return pl.pallas_call(
        body, name=name, grid=(S // ts,),
        in_specs=[pl.BlockSpec((ts, M), lambda i: (i, 0)), pl.BlockSpec((ts, N), lambda i: (i, 0))],
        out_specs=pl.BlockSpec((M, N), lambda i: (0, 0)),
        out_shape=jax.ShapeDtypeStruct((M, N), F32),
        compiler_params=_params(("arbitrary",)))(a, b)


def _attn_fwd(qkv, g, r):
    L = qkv.shape[1]
    nb2 = L // (2 * BAND)
    scale = HEAD_DIM ** -0.5

    def body(q_ref, kp_ref, kc_ref, vp_ref, vc_ref, o_ref, st_ref):
        m_step = pl.program_id(1)
        ii = lax.broadcasted_iota(jnp.int32, (2 * BAND, 3 * BAND), 0)
        kk = lax.broadcasted_iota(jnp.int32, (2 * BAND, 3 * BAND), 1)
        mask = jnp.logical_and(jnp.logical_and(kk >= ii, kk <= ii + BAND),
                               jnp.logical_or(kk >= BAND, m_step > 0))
        lane = lax.broadcasted_iota(jnp.int32, (2 * BAND, 128), 1)
        stats = jnp.zeros((2 * BAND, 128), F32)
        for h in range(HEADS):
            sl = slice(h * HEAD_DIM, (h + 1) * HEAD_DIM)
            k_cat = jnp.concatenate([kp_ref[:, sl], kc_ref[:, sl]], axis=0)
            v_cat = jnp.concatenate([vp_ref[:, sl], vc_ref[:, sl]], axis=0)
            s = jnp.where(mask, _dot_nt(q_ref[:, sl], k_cat) * scale, NEG_INF)
            m = jnp.max(s, axis=1, keepdims=True)
            p = jnp.exp(s - m)
            l = jnp.sum(p, axis=1, keepdims=True)
            o_ref[:, sl] = _dot(p.astype(BF16), v_cat) / l
            stats = jnp.where(lane == h, m, stats)
            stats = jnp.where(lane == HEADS + h, l, stats)
        st_ref[...] = stats

    two = (None, 2 * BAND, ATT_WIDTH)
    one = (None, BAND, ATT_WIDTH)

    def prev(m):
        return jnp.maximum(2 * m - 1, 0)

    return pl.pallas_call(
        body, name=f"attn_fwd_g{g}", grid=(r, nb2),
        in_specs=[
            pl.BlockSpec(two, lambda rho, m: (rho, m, 0)),
            pl.BlockSpec(one, lambda rho, m: (rho, prev(m), 1)),
            pl.BlockSpec(two, lambda rho, m: (rho, m, 1)),
            pl.BlockSpec(one, lambda rho, m: (rho, prev(m), 2)),
            pl.BlockSpec(two, lambda rho, m: (rho, m, 2)),
        ],
        out_specs=(pl.BlockSpec(two, lambda rho, m: (rho, m, 0)),
                   pl.BlockSpec((None, 2 * BAND, 128), lambda rho, m: (rho, m, 0))),
        out_shape=(jax.ShapeDtypeStruct((r, L, ATT_WIDTH), F32),
                   jax.ShapeDtypeStruct((r, L, 128), F32)),
        compiler_params=_params(("parallel", "parallel")))(qkv, qkv, qkv, qkv, qkv)


def _attn_combine(outs, stats, rest):
    S = rest.shape[0]
    tm = 512
    gatt_blk = 8
    back = [_perm_matrix(r, tm, transpose=True) for r in DILATIONS[1:]]
    forth = [_perm_matrix(r, tm) for r in DILATIONS[1:]]

    def body(o0_ref, o1_ref, o2_ref, s0_ref, s1_ref, s2_ref, g_ref, b1_ref, b2_ref, f1_ref, f2_ref,
             o_ref, a_ref, l0_ref, l1_ref, l2_ref):
        outs_nat = [o0_ref[0]] + [_permute_f32(b_ref[...], _rows_of_residues(o_g), 2)
                                  for o_g, b_ref in ((o1_ref, b1_ref), (o2_ref, b2_ref))]
        st = [s0_ref[0]] + [_permute_f32(b_ref[...], _rows_of_residues(s_g), 3)
                            for s_g, b_ref in ((s1_ref, b1_ref), (s2_ref, b2_ref))]
        lane = lax.broadcasted_iota(jnp.int32, (tm, 128), 1)
        lse_out = jnp.zeros((tm, 128), F32)
        for h in range(HEADS):
            sl = slice(h * HEAD_DIM, (h + 1) * HEAD_DIM)
            ms = [s[:, h:h + 1] for s in st]
            ls = [s[:, HEADS + h:HEADS + h + 1] for s in st]
            m_all = jnp.maximum(jnp.maximum(ms[0], ms[1]), ms[2])
            ws = [l * jnp.exp(m - m_all) for m, l in zip(ms, ls)]
            den = ws[0] + ws[1] + ws[2]
            o = (ws[0] * outs_nat[0][:, sl] + ws[1] * outs_nat[1][:, sl] + ws[2] * outs_nat[2][:, sl]) / den
            o_ref[:, sl] = o
            a_ref[:, sl] = (o * _silu(g_ref[:, sl])).astype(BF16)
            lse_out = jnp.where(lane == h, m_all + jnp.log(den), lse_out)
        l0_ref[0] = lse_out
        for l_ref, f_ref in ((l1_ref, f1_ref), (l2_ref, f2_ref)):
            _store_residues(l_ref, _permute_f32(f_ref[...], lse_out, 3))

    o_spec = pl.BlockSpec((tm, ATT_WIDTH), lambda i: (i, 0))
    whole = pl.BlockSpec(memory_space=pltpu.VMEM)

    def res_spec(r, width):
        return pl.BlockSpec((r, tm // r, width), lambda i: (0, i, 0))

    return pl.pallas_call(
        body, name="attn_combine", grid=(S // tm,),
        in_specs=[*[res_spec(r, ATT_WIDTH) for r in DILATIONS], *[res_spec(r, 128) for r in DILATIONS],
                  pl.BlockSpec((tm, ATT_WIDTH), lambda i: (i, gatt_blk)), whole, whole, whole, whole],
        out_specs=(o_spec, o_spec, *[res_spec(r, 128) for r in DILATIONS]),
        out_shape=(jax.ShapeDtypeStruct((S, ATT_WIDTH), F32), jax.ShapeDtypeStruct((S, ATT_WIDTH), BF16),
                   *[jax.ShapeDtypeStruct((r, S // r, 128), F32) for r in DILATIONS]),
        compiler_params=_params(("parallel",)))(*outs, *stats, rest, *back, *forth)


def _attn_bwd(qkv, do, lse, dvec, g, r, hosted=None):
    L = qkv.shape[1]
    nb = L // BAND
    nb2 = nb // 2
    scale = HEAD_DIM ** -0.5

    def body(qc_ref, qn_ref, k_ref, v_ref, doc_ref, don_ref, lc_ref, ln_ref, dc_ref, dn_ref,
             out_ref, carry_ref):
        j = pl.program_id(1)

        @pl.when(j == 0)
        def _():
            carry_ref[...] = jnp.zeros_like(carry_ref)

        ii = lax.broadcasted_iota(jnp.int32, (3 * BAND, 2 * BAND), 0)
        kk = lax.broadcasted_iota(jnp.int32, (3 * BAND, 2 * BAND), 1)
        mask = jnp.logical_and(jnp.logical_and(kk <= ii, kk >= ii - BAND),
                               jnp.logical_or(ii < 2 * BAND, j < nb2 - 1))
        lse3 = jnp.concatenate([lc_ref[...], ln_ref[...]], axis=0)
        dvec3 = jnp.concatenate([dc_ref[...], dn_ref[...]], axis=0)
        for h in range(HEADS):
            sl = slice(h * HEAD_DIM, (h + 1) * HEAD_DIM)
            k = k_ref[:, sl]
            v = v_ref[:, sl]
            q = jnp.concatenate([qc_ref[:, sl], qn_ref[:, sl]], axis=0)
            do = jnp.concatenate([doc_ref[:, sl], don_ref[:, sl]], axis=0)
            p = jnp.where(mask, jnp.exp(_dot_nt(q, k) * scale - lse3[:, h:h + 1]), 0.0)
            ds = (p * (_dot_nt(do, v) - dvec3[:, h:h + 1])).astype(BF16)
            dq3 = _dot(ds, k) * scale
            out_ref[0, 0:BAND, sl] = (carry_ref[:, sl] + dq3[:BAND]).astype(BF16)
            out_ref[0, BAND:, sl] = dq3[BAND:2 * BAND].astype(BF16)
            out_ref[1, :, sl] = (_dot_tn(ds, q) * scale).astype(BF16)
            out_ref[2, :, sl] = _dot_tn(p.astype(BF16), do).astype(BF16)
            carry_ref[:, sl] = dq3[2 * BAND:]

    two = (None, 2 * BAND, ATT_WIDTH)
    one = (None, BAND, ATT_WIDTH)
    stwo = (None, 2 * BAND, 128)
    sone = (None, BAND, 128)

    def nxt(j):
        return jnp.minimum(2 * j + 2, nb - 1)

    (out,), got = _call(
        body, (qkv, qkv, qkv, qkv, do, do, lse, lse, dvec, dvec), name=f"attn_bwd_g{g}", grid=(r, nb2),
        in_specs=[
            pl.BlockSpec(two, lambda rho, j: (rho, j, 0)),
            pl.BlockSpec(one, lambda rho, j: (rho, nxt(j), 0)),
            pl.BlockSpec(two, lambda rho, j: (rho, j, 1)),
            pl.BlockSpec(two, lambda rho, j: (rho, j, 2)),
            pl.BlockSpec(two, lambda rho, j: (rho, j, 0)),
            pl.BlockSpec(one, lambda rho, j: (rho, nxt(j), 0)),
            pl.BlockSpec(stwo, lambda rho, j: (rho, j, 0)),
            pl.BlockSpec(sone, lambda rho, j: (rho, nxt(j), 0)),
            pl.BlockSpec(stwo, lambda rho, j: (rho, j, 0)),
            pl.BlockSpec(sone, lambda rho, j: (rho, nxt(j), 0)),
        ],
        out_specs=(pl.BlockSpec((3, None, 2 * BAND, ATT_WIDTH), lambda rho, j: (0, rho, j, 0)),),
        out_shape=(jax.ShapeDtypeStruct((3, r, L, ATT_WIDTH), BF16),),
        scratch_shapes=[pltpu.VMEM((BAND, ATT_WIDTH), F32)],
        hosted=hosted)
    return out, got


LRU_T = 256


def _linear_scan(a, b, carry, reverse):
    T = a.shape[0]
    row8 = lax.broadcasted_iota(jnp.int32, a.shape, 0) & 7
    for s in (1, 2, 4):
        keep = (row8 < 8 - s) if reverse else (row8 >= s)
        shift = T - s if reverse else s
        b_sh = jnp.where(keep, pltpu.roll(b, shift, 0), 0.0)
        a_sh = jnp.where(keep, pltpu.roll(a, shift, 0), 1.0)
        b = a * b_sh + b
        a = a * a_sh
    tiles = [None] * (T // 8)
    order = range(T // 8 - 1, -1, -1) if reverse else range(T // 8)
    for k in order:
        y = b[8 * k:8 * k + 8] + a[8 * k:8 * k + 8] * carry
        tiles[k] = y
        carry = y[0:1] if reverse else y[7:8]
    return jnp.concatenate(tiles, axis=0), carry


def _gate_matmuls(ucb, w_ref, bias):
    parts = [_dot(ucb[:, j * 128:(j + 1) * 128], w_ref[j]) for j in range(8)]
    return jnp.concatenate(parts, axis=1) + bias


def _conv_fwd(u, u_prev8, cw_ref):
    T = u.shape[0]
    ue = jnp.concatenate([u_prev8, u], axis=0)
    uc = cw_ref[4:5, :] + cw_ref[0:1, :] * u
    for j in range(1, CONV_WIDTH):
        uc = uc + cw_ref[j:j + 1, :] * pltpu.roll(ue, j, 0)[8:8 + T]
    return uc


def _lru_fwd(rest, cw, wr2, wi2, hosted=None):
    S = rest.shape[0]
    T = LRU_T

    def body(u_ref, cw_ref, wr_ref, wi_ref, h_ref, a_ref, uc_ref, r_ref, ig_ref, mult_ref,
             ucar_ref, hcar_ref):
        c = pl.program_id(0)

        @pl.when(c == 0)
        def _():
            ucar_ref[...] = jnp.zeros_like(ucar_ref)
            hcar_ref[...] = jnp.zeros_like(hcar_ref)

        u = u_ref[...]
        uc = _conv_fwd(u, ucar_ref[...], cw_ref)
        ucar_ref[...] = u[T - 8:, :]
        ucb = uc.astype(BF16)
        r = jax.nn.sigmoid(_gate_matmuls(ucb, wr_ref, cw_ref[5:6, :]))
        ig = jax.nn.sigmoid(_gate_matmuls(ucb, wi_ref, cw_ref[6:7, :]))
        log_a = -LRU_C * r * _softplus_neg(cw_ref[7:8, :])
        a = jnp.exp(log_a)
        mult = jnp.sqrt(_neg_expm1(2.0 * log_a))
        b = mult * (ig * uc)
        a_ref[...] = a
        uc_ref[...] = uc
        r_ref[...] = r
        ig_ref[...] = ig
        mult_ref[...] = mult
        h, last = _linear_scan(a, b, hcar_ref[0:1, :], reverse=False)
        h_ref[...] = h
        hcar_ref[...] = jnp.broadcast_to(last, (8, D_MODEL))

    row_spec = pl.BlockSpec((T, D_MODEL), lambda c: (c, 0))
    row_shape = jax.ShapeDtypeStruct((S, D_MODEL), F32)
    return _call(
        body, (rest, cw, wr2, wi2), name="lru_fwd", grid=(S // T,),
        in_specs=[row_spec, pl.BlockSpec((8, D_MODEL), lambda c: (0, 0)),
                  pl.BlockSpec((8, 128, 128), lambda c: (0, 0, 0)),
                  pl.BlockSpec((8, 128, 128), lambda c: (0, 0, 0))],
        out_specs=(row_spec,) * 6, out_shape=(row_shape,) * 6,
        scratch_shapes=[pltpu.VMEM((8, D_MODEL), F32), pltpu.VMEM((8, D_MODEL), F32)],
        hosted=hosted)


def _lru_bwd(dhl, hl, a_all, gates, rest, cw, wr2, wi2, dproj, hosted=None):
    S = rest.shape[0]
    T = LRU_T
    nc = S // T

    def body(dh_ref, h_ref, hp_ref, a_ref, uc_ref, r_ref, ig_ref, mult_ref, u_ref, up_ref, cw_ref, wr_ref,
             wi_ref, _alias, du_ref, acc_ref, gwr_ref, gwi_ref, gcar_ref, acar_ref, dcar_ref):
        step = pl.program_id(0)
        c = nc - 1 - step

        @pl.when(step == 0)
        def _():
            acc_ref[...] = jnp.zeros_like(acc_ref)
            gwr_ref[...] = jnp.zeros_like(gwr_ref)
            gwi_ref[...] = jnp.zeros_like(gwi_ref)
            gcar_ref[...] = jnp.zeros_like(gcar_ref)
            acar_ref[...] = jnp.zeros_like(acar_ref)
            dcar_ref[...] = jnp.zeros_like(dcar_ref)

        row = lax.broadcasted_iota(jnp.int32, (T, D_MODEL), 0)
        u = u_ref[...]
        u_prev = jnp.where(c > 0, up_ref[...], 0.0)
        h_prev8 = jnp.where(c > 0, hp_ref[...], 0.0)
        a = a_ref[...]
        h = h_ref[...]
        uc, r, ig, mult = uc_ref[...], r_ref[...], ig_ref[...], mult_ref[...]
        ucb = uc.astype(BF16)
        sp = _softplus_neg(cw_ref[7:8, :])
        a_next = jnp.where(row < T - 1, pltpu.roll(a, T - 1, 0), acar_ref[0:1, :])
        G, first = _linear_scan(a_next, dh_ref[...], gcar_ref[0:1, :], reverse=True)
        gcar_ref[...] = jnp.broadcast_to(first, (8, D_MODEL))
        acar_ref[...] = jnp.broadcast_to(a[0:1, :], (8, D_MODEL))
        he = jnp.concatenate([h_prev8, h], axis=0)
        h_before = pltpu.roll(he, 1, 0)[8:8 + T]
        d_a = G * h_before
        d_mult = G * (ig * uc)
        d_ig = G * (mult * uc)
        duc = G * (mult * ig)
        d_log_a = d_a * a - d_mult * (a * a) / mult
        d_r = d_log_a * (-LRU_C * sp)
        d_sp = jnp.sum(d_log_a * (-LRU_C * r), axis=0, keepdims=True)
        dpre_r = d_r * r * (1.0 - r)
        dpre_i = d_ig * ig * (1.0 - ig)
        dprb = dpre_r.astype(BF16)
        dpib = dpre_i.astype(BF16)
        back = []
        for j in range(8):
            sl = slice(j * 128, (j + 1) * 128)
            back.append(_dot_nt(dprb[:, sl], wr_ref[j]) + _dot_nt(dpib[:, sl], wi_ref[j]))
            gwr_ref[j] += _dot_tn(ucb[:, sl], dprb[:, sl])
            gwi_ref[j] += _dot_tn(ucb[:, sl], dpib[:, sl])
        duc = duc + jnp.concatenate(back, axis=1)
        de = jnp.concatenate([duc, dcar_ref[...]], axis=0)
        du = cw_ref[0:1, :] * duc
        ue = jnp.concatenate([u_prev, u], axis=0)
        acc_ref[0:1, :] += jnp.sum(duc * u, axis=0, keepdims=True)
        for j in range(1, CONV_WIDTH):
            du = du + cw_ref[j:j + 1, :] * pltpu.roll(de, T + 8 - j, 0)[0:T]
            acc_ref[j:j + 1, :] += jnp.sum(duc * pltpu.roll(ue, j, 0)[8:8 + T], axis=0, keepdims=True)
        dcar_ref[...] = duc[0:8, :]
        acc_ref[4:5, :] += jnp.sum(duc, axis=0, keepdims=True)
        acc_ref[5:6, :] += jnp.sum(dpre_r, axis=0, keepdims=True)
        acc_ref[6:7, :] += jnp.sum(dpre_i, axis=0, keepdims=True)
        acc_ref[7:8, :] += d_sp
        du_ref[0] = du[:, 0:ATT_WIDTH].astype(BF16)
        du_ref[1] = du[:, ATT_WIDTH:].astype(BF16)

    def rev(step):
        return nc - 1 - step

    def prev8(step):
        return jnp.maximum(rev(step) * (T // 8) - 1, 0)

    row_spec = pl.BlockSpec((T, D_MODEL), lambda s: (rev(s), 0))
    prev_spec = pl.BlockSpec((8, D_MODEL), lambda s: (prev8(s), 0))
    vec_spec = pl.BlockSpec((8, D_MODEL), lambda s: (0, 0))
    w_spec = pl.BlockSpec((8, 128, 128), lambda s: (0, 0, 0))
    return _call(
        body, (dhl, hl, hl, a_all, *gates, rest, rest, cw, wr2, wi2, dproj), name="lru_bwd", grid=(nc,),
        in_specs=[row_spec, row_spec, prev_spec, row_spec, row_spec, row_spec, row_spec, row_spec,
                  row_spec, prev_spec, vec_spec, w_spec, w_spec, ANY_SPEC],
        out_specs=(pl.BlockSpec((2, T, ATT_WIDTH), lambda s: (4, rev(s), 0)), vec_spec, w_spec, w_spec),
        out_shape=(jax.ShapeDtypeStruct(dproj.shape, BF16), jax.ShapeDtypeStruct((8, D_MODEL), F32),
                   jax.ShapeDtypeStruct((8, 128, 128), F32), jax.ShapeDtypeStruct((8, 128, 128), F32)),
        scratch_shapes=[pltpu.VMEM((8, D_MODEL), F32), pltpu.VMEM((8, D_MODEL), F32),
                        pltpu.VMEM((8, D_MODEL), F32)],
        aliases={13: 0}, hosted=hosted)


OUT_TM = 256


def _out_fwd(a_gated, hl, rest, x, vec, w_pa, w_pb, w_o, target=None, hosted=None):
    S = x.shape[0]
    tm = OUT_TM
    with_loss = target is not None

    def body(*refs):
        (ag_ref, hl_ref, gl_ref, ma_ref, mb_ref, x_ref, v_ref, wpa_ref, wpb_ref, wo_ref), refs = refs[:10], refs[10:]
        if with_loss:
            t_ref, refs = refs[0], refs[1:]
        xn_ref, mix_ref, ya_ref, yb_ref, bg_ref = refs[:5]
        bg = (hl_ref[...] * _silu(gl_ref[...])).astype(BF16)
        ya = _dot(ag_ref[...], wpa_ref[...])
        yb = _dot(bg, wpb_ref[...])
        mix = (jax.nn.sigmoid(ma_ref[...]) * ya + jax.nn.sigmoid(mb_ref[...]) * yb).astype(BF16)
        out = _dot(mix, wo_ref[...])
        rstd = lax.rsqrt(jnp.mean(out * out, axis=-1, keepdims=True) + NORM_EPS)
        x_new = x_ref[...] + v_ref[1:2, :] * ((out * rstd) * v_ref[0:1, :])
        if with_loss:
            acc_ref = refs[5]

            @pl.when(pl.program_id(0) == 0)
            def _():
                acc_ref[...] = jnp.zeros_like(acc_ref)

            err = x_new - t_ref[...]
            xn_ref[...] = err * (1.0 / D_MODEL)
            part = jnp.sum(jnp.sum(err * err, axis=1, keepdims=True), axis=0, keepdims=True)
            acc_ref[...] += jnp.broadcast_to(part, acc_ref.shape)
        else:
            xn_ref[...] = x_new
        mix_ref[...] = mix
        ya_ref[...] = ya.astype(BF16)
        yb_ref[...] = yb.astype(BF16)
        bg_ref[...] = bg

    def col(j):
        return pl.BlockSpec((tm, D_MODEL), lambda i: (i, j))

    def whole(shape):
        return pl.BlockSpec(shape, lambda i: (0, 0))

    row = col(0)
    bf = jax.ShapeDtypeStruct((S, D_MODEL), BF16)
    loss_in = ([target], [row]) if with_loss else ([], [])
    loss_out = ([jax.ShapeDtypeStruct((8, 128), F32)], [whole((8, 128))]) if with_loss else ([], [])
    return _call(
        body, (a_gated, hl, rest, rest, rest, x, vec, w_pa, w_pb, w_o, *loss_in[0]), name="out_fwd",
        grid=(S // tm,),
        in_specs=[pl.BlockSpec((tm, ATT_WIDTH), lambda i: (i, 0)), row, col(1), col(2), col(3), row,
                  whole((8, D_MODEL)), whole((ATT_WIDTH, D_MODEL)), whole((D_MODEL, D_MODEL)),
                  whole((D_MODEL, D_MODEL)), *loss_in[1]],
        out_specs=(row, row, row, row, row, *loss_out[1]),
        out_shape=(jax.ShapeDtypeStruct((S, D_MODEL), F32), bf, bf, bf, bf, *loss_out[0]),
        hosted=hosted)


def _out_bwd(dxn, mix, ya, yb, hl, rest, o_att, vec, w_pa, w_pb, w_o, hosted=None):
    S = dxn.shape[0]
    tm = OUT_TM
    forth = [_perm_matrix(r, tm) for r in DILATIONS[1:]]

    def body(dxn_ref, mix_ref, ya_ref, yb_ref, hl_ref, gl_ref, ma_ref, mb_ref, ga_ref, o_ref, v_ref,
             wpa_ref, wpb_ref, wo_ref, f1_ref, f2_ref,
             dout_ref, dya_ref, dyb_ref, dhl_ref, dp_ref, acc_ref,
             do0_ref, do1_ref, do2_ref, dv0_ref, dv1_ref, dv2_ref):
        i = pl.program_id(0)

        @pl.when(i == 0)
        def _():
            acc_ref[...] = jnp.zeros_like(acc_ref)

        g_post = v_ref[0:1, :]
        gate = v_ref[1:2, :]
        dxn_v = dxn_ref[...]
        out = _dot(mix_ref[...], wo_ref[...])
        rstd = lax.rsqrt(jnp.mean(out * out, axis=-1, keepdims=True) + NORM_EPS)
        nrm = out * rstd
        acc_ref[0:1, :] += jnp.sum(dxn_v * (nrm * g_post), axis=0, keepdims=True)
        acc_ref[1:2, :] += jnp.sum(dxn_v * gate * nrm, axis=0, keepdims=True)
        dn = dxn_v * (gate * g_post)
        dout = (rstd * (dn - nrm * jnp.mean(dn * nrm, axis=-1, keepdims=True))).astype(BF16)
        dout_ref[...] = dout
        dmix = _dot_nt(dout, wo_ref[...])
        sa = jax.nn.sigmoid(ma_ref[...])
        sb = jax.nn.sigmoid(mb_ref[...])
        dya = (dmix * sa).astype(BF16)
        dyb = (dmix * sb).astype(BF16)
        dya_ref[...] = dya
        dyb_ref[...] = dyb
        dma = dmix * ya_ref[...].astype(F32) * (sa * (1.0 - sa))
        dmb = dmix * yb_ref[...].astype(F32) * (sb * (1.0 - sb))
        d_ag = _dot_nt(dya, wpa_ref[...])
        d_bg = _dot_nt(dyb, wpb_ref[...])
        gl = gl_ref[...]
        hl_v = hl_ref[...]
        dhl_ref[...] = d_bg * _silu(gl)
        dgl = d_bg * hl_v * _dsilu(gl)
        ga = ga_ref[...]
        o = o_ref[...]
        do = d_ag * _silu(ga)
        dga = d_ag * o * _dsilu(ga)
        lane = lax.broadcasted_iota(jnp.int32, (tm, 128), 1)
        dvec = jnp.zeros((tm, 128), F32)
        for h in range(HEADS):
            sl = slice(h * HEAD_DIM, (h + 1) * HEAD_DIM)
            dvec = jnp.where(lane == h, jnp.sum(do[:, sl] * o[:, sl], axis=1, keepdims=True), dvec)
        do_b = do.astype(BF16)
        do0_ref[0] = do_b
        dv0_ref[0] = dvec
        for do_ref, dv_ref, f_ref in ((do1_ref, dv1_ref, f1_ref), (do2_ref, dv2_ref, f2_ref)):
            _store_residues(do_ref, _dot(f_ref[...], do_b).astype(BF16))
            _store_residues(dv_ref, _permute_f32(f_ref[...], dvec, 3))
        dp_ref[0] = dgl[:, 0:ATT_WIDTH].astype(BF16)
        dp_ref[1] = dgl[:, ATT_WIDTH:].astype(BF16)
        dp_ref[2] = dma[:, 0:ATT_WIDTH].astype(BF16)
        dp_ref[3] = dma[:, ATT_WIDTH:].astype(BF16)
        dp_ref[4] = dmb[:, 0:ATT_WIDTH].astype(BF16)
        dp_ref[5] = dmb[:, ATT_WIDTH:].astype(BF16)
        dp_ref[6] = dga.astype(BF16)

    def col(j):
        return pl.BlockSpec((tm, D_MODEL), lambda i: (i, j))

    def whole(shape):
        return pl.BlockSpec(shape, lambda i: (0, 0))

    def res_spec(r, width):
        return pl.BlockSpec((r, tm // r, width), lambda i: (0, i, 0))

    row = col(0)
    att = pl.BlockSpec((tm, ATT_WIDTH), lambda i: (i, 0))
    bf = jax.ShapeDtypeStruct((S, D_MODEL), BF16)
    vmem = pl.BlockSpec(memory_space=pltpu.VMEM)
    return _call(
        body, (dxn, mix, ya, yb, hl, rest, rest, rest, rest, o_att, vec, w_pa, w_pb, w_o, *forth),
        name="out_bwd", grid=(S // tm,),
        in_specs=[row, row, row, row, row, col(1), col(2), col(3),
                  pl.BlockSpec((tm, ATT_WIDTH), lambda i: (i, 8)), att,
                  whole((8, D_MODEL)), whole((ATT_WIDTH, D_MODEL)), whole((D_MODEL, D_MODEL)),
                  whole((D_MODEL, D_MODEL)), vmem, vmem],
        out_specs=(row, row, row, row,
                   pl.BlockSpec((7, tm, ATT_WIDTH), lambda i: (0, i, 0)), whole((8, D_MODEL)),
                   *[res_spec(r, ATT_WIDTH) for r in DILATIONS], *[res_spec(r, 128) for r in DILATIONS]),
        out_shape=(bf, bf, bf, jax.ShapeDtypeStruct((S, D_MODEL), F32),
                   jax.ShapeDtypeStruct((DREST_CHUNKS, S, ATT_WIDTH), BF16),
                   jax.ShapeDtypeStruct((8, D_MODEL), F32),
                   *[jax.ShapeDtypeStruct((r, S // r, ATT_WIDTH), BF16) for r in DILATIONS],
                   *[jax.ShapeDtypeStruct((r, S // r, 128), F32) for r in DILATIONS]),
        hosted=hosted)


def _adamw(gsrcs, w, m, v, n_stack, name, rows=None):
    n_layers, R, C = w.shape
    assert len(gsrcs) == n_layers
    budget = 96 * 1024
    tr = rows
    if tr is None:
        tr = R
        while tr * C > budget and tr % 16 == 0:
            tr //= 2
    assert R % tr == 0 and (tr % 8 == 0 or tr == R)
    c1 = 1.0 - ADAM_B1 ** ADAM_STEP
    c2 = 1.0 - ADAM_B2 ** ADAM_STEP

    def body(*refs):
        g_refs = refs[:n_layers]
        w_ref, m_ref, v_ref, go_ref, d_ref, mo_ref, vo_ref = refs[n_layers:]

        def update(g_ref):
            if n_stack:
                g = g_ref[0].astype(F32)
                for s in range(1, n_stack):
                    g = g + g_ref[s].astype(F32)
            else:
                g = g_ref[...]
            m_new = ADAM_B1 * m_ref[...] + (1.0 - ADAM_B1) * g
            v_new = ADAM_B2 * v_ref[...] + (1.0 - ADAM_B2) * (g * g)
            m_hat = m_new / c1
            v_hat = v_new / c2
            go_ref[...] = g
            d_ref[...] = -ADAM_LR * (m_hat / (jnp.sqrt(v_hat) + ADAM_EPS) + ADAM_WD * w_ref[...])
            mo_ref[...] = m_new
            vo_ref[...] = v_new

        for layer in range(n_layers):
            pl.when(pl.program_id(0) == layer)(functools.partial(update, g_refs[layer]))

    def g_spec(layer):
        def rows_of(l, i):
            return jnp.where(l == layer, i, 0)
        if n_stack:
            return pl.BlockSpec((n_stack, tr, C), lambda l, i: (0, rows_of(l, i), 0))
        return pl.BlockSpec((tr, C), lambda l, i: (rows_of(l, i), 0))

    blk = pl.BlockSpec((None, tr, C), lambda l, i: (l, i, 0))
    shp = jax.ShapeDtypeStruct((n_layers, R, C), F32)
    return pl.pallas_call(
        body, name=name, grid=(n_layers, R // tr),
        in_specs=[g_spec(layer) for layer in range(n_layers)] + [blk, blk, blk],
        out_specs=(blk, blk, blk, blk), out_shape=(shp, shp, shp, shp),
        compiler_params=_params(("arbitrary", "arbitrary")))(*gsrcs, w, m, v)


def _pair_blocks(w):
    w = w.reshape(8, 2, 64, 64)
    z = jnp.zeros((8, 64, 64), w.dtype)
    top = jnp.concatenate([w[:, 0], z], axis=2)
    bot = jnp.concatenate([z, w[:, 1]], axis=2)
    return jnp.concatenate([top, bot], axis=1).astype(BF16)


def _unpair_blocks(g):
    return jnp.stack([g[:, :64, :64], g[:, 64:, 64:]], axis=1).reshape(16, 64, 64)


def _layer_params(layer, w_in_g, w_pa_g, w_pb_g, w_o_g, conv_w_full, conv_b, b_rg, b_ig, lru_lambda, w_rg,
                  w_ig, g_pre, g_post):
    cw = jnp.concatenate([conv_w_full, conv_b[None], b_rg[None], b_ig[None], lru_lambda[None]], axis=0)
    return dict(
        w_int=_weights_full(w_in_g),
        w_pa=jnp.transpose(w_pa_g, (1, 0, 2)).reshape(ATT_WIDTH, D_MODEL),
        w_pb=w_pb_g.reshape(D_MODEL, D_MODEL), w_o=w_o_g.reshape(D_MODEL, D_MODEL),
        cw=cw, wr2=_pair_blocks(w_rg), wi2=_pair_blocks(w_ig),
        g_pre=g_pre, g_post=g_post, lru_lambda=lru_lambda)


def _layer_fwd(x, mod, p, host_proj=None, host_lru=None, host_out=None, target=None):
    zeros = jnp.zeros((5, D_MODEL), F32)
    vec_pre = jnp.concatenate([p["g_pre"][None], mod[1:2], mod[0:1], zeros], axis=0)
    (ht0, ht1, ht2, q0, q1, q2, rest), got_proj = _norm_proj_fwd(x, vec_pre, p["w_int"], hosted=host_proj)
    h_t = [ht0, ht1, ht2]
    qkv, outs, stats = [q0, q1, q2], [], []
    for g, r in enumerate(DILATIONS):
        o_g, st_g = _attn_fwd(qkv[g], g, r)
        outs.append(o_g)
        stats.append(st_g)
    o_att, a_gated, *lse = _attn_combine(outs, stats, rest)
    (hl, a_dec, *gates), got_lru = _lru_fwd(rest, p["cw"], p["wr2"], p["wi2"], hosted=host_lru)
    vec_post = jnp.concatenate([p["g_post"][None], mod[2:3], jnp.zeros((6, D_MODEL), F32)], axis=0)
    (x_new, mix, ya, yb, b_gated, *loss_acc), got_out = _out_fwd(
        a_gated, hl, rest, x, vec_post, p["w_pa"], p["w_pb"], p["w_o"], target=target, hosted=host_out)
    saved = dict(x=x, h_t=h_t, qkv=qkv, rest=rest, o_att=o_att, a_gated=a_gated, lse=lse, hl=hl, a_dec=a_dec,
                 gates=gates, mix=mix, ya=ya, yb=yb, b_gated=b_gated, vec_pre=vec_pre, vec_post=vec_post)
    return (x_new, *loss_acc), saved, got_proj, got_lru, got_out


def _mats_sources(g_pa, g_pb, g_o):
    return [jnp.transpose(g_pa.astype(BF16).reshape(ATT_WIDTH, N_DEV, 128), (1, 0, 2)),
            g_pb.astype(BF16).reshape(N_DEV, 128, D_MODEL), g_o.astype(BF16).reshape(N_DEV, 128, D_MODEL)]


LRU_SMALL = ("conv_b", "w_rg", "b_rg", "w_ig", "b_ig", "lru_lambda")
MOD_SMALL = ("b_mod", "g_pre", "g_post")


def _flat_rows(arrs):
    return jnp.concatenate([a.reshape(-1) for a in arrs]).reshape(-1, 128)


def _layer_bwd(dxn, sv, p, upper_small=None):
    a2a3 = ["a2a"] * 3
    (dout, dya, dyb, dhl, drest, acc_out, *do_dvec), _ = _out_bwd(
        dxn, sv["mix"], sv["ya"], sv["yb"], sv["hl"], sv["rest"], sv["o_att"], sv["vec_post"],
        p["w_pa"], p["w_pb"], p["w_o"])
    mats = _mats_sources(_wgrad_tn(sv["a_gated"], dya, "wgrad_pa"), _wgrad_tn(sv["b_gated"], dyb, "wgrad_pb"),
                         _wgrad_tn(sv["mix"], dout, "wgrad_o"))
    (drest, acc_lru, gwr2, gwi2), _ = _lru_bwd(
        dhl, sv["hl"], sv["a_dec"], sv["gates"], sv["rest"], p["cw"], p["wr2"], p["wi2"], drest)
    small = dict(
        conv_w=acc_lru[0:4], conv_b=acc_lru[4], b_rg=acc_lru[5], b_ig=acc_lru[6],
        lru_lambda=acc_lru[7] * (-jax.nn.sigmoid(-p["lru_lambda"])),
        w_rg=_unpair_blocks(gwr2), w_ig=_unpair_blocks(gwi2))
    hosts = [(mats, a2a3), None, None]
    if upper_small is not None:
        lru_vec = _flat_rows([jnp.stack([small[n], upper_small[n]]) for n in LRU_SMALL + ("conv_w",)])
        hosts[1] = ([lru_vec], ["ag"])
    dqkv, got_attn = [], []
    for g, r in enumerate(DILATIONS):
        d_g, got = _attn_bwd(sv["qkv"][g], do_dvec[g], sv["lse"][g], do_dvec[3 + g], g, r, hosted=hosts[g])
        dqkv.append(d_g)
        got_attn.append(got)
    parts = [_proj_wgrad_group(sv["h_t"][g], dqkv[g], r, f"proj_wgrad_g{g}") for g, r in enumerate(DILATIONS)]
    parts.append(_proj_wgrad_part(sv["h_t"][0], drest, 2, 4, "proj_wgrad_u"))
    parts.append(_proj_wgrad_part(sv["h_t"][0], drest, 7, 0, "proj_wgrad_rest"))
    (dx, acc_pre), got_in = _proj_dgrad_norm_bwd(
        dqkv, drest, p["w_int"], sv["x"], dxn, sv["vec_pre"], hosted=([_grads_to_shards(parts)], ["a2a"]))
    small.update(dmod=jnp.concatenate([acc_pre[0], acc_pre[1], acc_out[0]]), g_pre=acc_pre[2],
                 g_post=acc_out[1])
    return dx, small, dict(mats=got_attn[0], w_in=got_in[0], lru_small=got_attn[1])


SMALL_NAMES = ("b_mod", "g_pre", "conv_b", "w_rg", "b_rg", "w_ig", "b_ig", "lru_lambda", "g_post")


def kernel(x, c, w_mod, b_mod, g_pre, w_in, conv_w, conv_b, w_rg, b_rg, w_ig, b_ig, lru_lambda, w_pa, w_pb, w_o, g_post, loss_target, m_w_mod, m_b_mod, m_g_pre, m_w_in, m_conv_w, m_conv_b, m_w_rg, m_b_rg, m_w_ig, m_b_ig, m_lru_lambda, m_w_pa, m_w_pb, m_w_o, m_g_post, v_w_mod, v_b_mod, v_g_pre, v_w_in, v_conv_w, v_conv_b, v_w_rg, v_b_rg, v_w_ig, v_b_ig, v_lru_lambda, v_w_pa, v_w_pb, v_w_o, v_g_post):
    W = dict(w_mod=w_mod, b_mod=b_mod, g_pre=g_pre, w_in=w_in, conv_w=conv_w, conv_b=conv_b, w_rg=w_rg,
             b_rg=b_rg, w_ig=w_ig, b_ig=b_ig, lru_lambda=lru_lambda, w_pa=w_pa, w_pb=w_pb, w_o=w_o,
             g_post=g_post)
    M = dict(w_mod=m_w_mod, b_mod=m_b_mod, g_pre=m_g_pre, w_in=m_w_in, conv_w=m_conv_w, conv_b=m_conv_b,
             w_rg=m_w_rg, b_rg=m_b_rg, w_ig=m_w_ig, b_ig=m_b_ig, lru_lambda=m_lru_lambda, w_pa=m_w_pa,
             w_pb=m_w_pb, w_o=m_w_o, g_post=m_g_post)
    V = dict(w_mod=v_w_mod, b_mod=v_b_mod, g_pre=v_g_pre, w_in=v_w_in, conv_w=v_conv_w, conv_b=v_conv_b,
             w_rg=v_w_rg, b_rg=v_b_rg, w_ig=v_w_ig, b_ig=v_b_ig, lru_lambda=v_lru_lambda, w_pa=v_w_pa,
             w_pb=v_w_pb, w_o=v_w_o, g_post=v_g_post)
    S = x.shape[1]
    me = 4 * lax.axis_index("x") + 2 * lax.axis_index("y") + lax.axis_index("c")
    n_mod = w_mod.shape[2]
    n_in = w_in.shape[2]
    n_conv = conv_w.shape[2]

    c_rows = jnp.broadcast_to(c, (8, D_MODEL))
    w_in_b, w_pa_b, w_pb_b, w_o_b = (t.astype(BF16) for t in (w_in, w_pa, w_pb, w_o))
    g_c, g_win0, g_wpa0, g_wpb0, g_wo0, g_cw = _gather_two_level(
        [c_rows, w_in_b[0], w_pa_b[0], w_pb_b[0], w_o_b[0], conv_w], "gather_weights")
    c_all = g_c[:, 0, :]
    c_pad = jnp.concatenate([c_all, jnp.zeros((8, D_MODEL), F32)], axis=0)
    conv_w_full = jnp.transpose(g_cw, (1, 2, 0, 3)).reshape(2, CONV_WIDTH, D_MODEL)

    mod_cols = _mod_fwd(c_pad, w_mod)
    mod_src = jnp.transpose(mod_cols[:, :8], (1, 0, 2))
    mod_src = jnp.concatenate([mod_src, jnp.zeros((8, 6, n_mod), F32)], axis=1)
    (mod_got,) = _exchange([mod_src], ["a2a"], "scatter_mod")
    mod = jnp.transpose(mod_got[:, :2], (1, 0, 2)).reshape(2, 3 * D_MODEL) + b_mod

    def layer_params(layer, w_in_g, w_pa_g, w_pb_g, w_o_g):
        return _layer_params(layer, w_in_g, w_pa_g, w_pb_g, w_o_g, conv_w_full[layer], conv_b[layer],
                             b_rg[layer], b_ig[layer], lru_lambda[layer], w_rg[layer], w_ig[layer],
                             g_pre[layer], g_post[layer])

    layers = [layer_params(0, g_win0, g_wpa0, g_wpb0, g_wo0), None]
    (act,), sv0, (g_win1,), _, (g_wpa1, g_wpb1, g_wo1) = _layer_fwd(
        x[0], mod[0].reshape(3, D_MODEL), layers[0],
        host_proj=([w_in_b[1]], ["ag"]), host_out=([w_pa_b[1], w_pb_b[1], w_o_b[1]], ["ag"] * 3))
    layers[1] = layer_params(1, g_win1, g_wpa1, g_wpb1, g_wo1)
    (dy, loss_acc), sv1, _, _, _ = _layer_fwd(act, mod[1].reshape(3, D_MODEL), layers[1],
                                              target=loss_target[0])

    dy, small1, got1 = _layer_bwd(dy, sv1, layers[1])
    dy, small0, got0 = _layer_bwd(dy, sv0, layers[0], upper_small=small1)
    grad_x = dy[None]
    r_pa, r_pb, r_o = ([got0["mats"][k], got1["mats"][k]] for k in range(3))
    r_in = [got0["w_in"], got1["w_in"]]
    grads = [small0, small1]

    def stack2(name):
        return jnp.stack([grads[0][name], grads[1][name]], axis=0)

    dmod = stack2("dmod")
    dmod_src = jnp.transpose(dmod.reshape(2, N_DEV, n_mod), (1, 0, 2))
    dmod_src = jnp.concatenate([dmod_src, jnp.zeros((8, 6, n_mod), F32)], axis=1)
    mod_vec = _flat_rows([dmod, stack2("g_pre"), stack2("g_post"), loss_acc])
    r_dmod, r_mod_small = _exchange([dmod_src, mod_vec], ["a2a", "ag"], "exchange_grads")
    mod_rows = mod_vec.shape[0] - loss_acc.shape[0]
    loss = 0.5 * jnp.sum(r_mod_small[:, mod_rows, 0]) / D_MODEL
    (r_lru_small,) = got0["lru_small"]

    res = {}
    res["w_in"] = _adamw(r_in, w_in, m_w_in, v_w_in, 8, "adamw_w_in")
    res["w_pa"] = _adamw(r_pa, w_pa, m_w_pa, v_w_pa, 8, "adamw_w_pa")
    res["w_pb"] = _adamw(r_pb, w_pb, m_w_pb, v_w_pb, 8, "adamw_w_pb")
    res["w_o"] = _adamw(r_o, w_o, m_w_o, v_w_o, 8, "adamw_w_o")
    dmod_all = jnp.transpose(r_dmod[:, :2], (1, 0, 2))
    gw_mod = _mod_wgrad(jnp.transpose(c_all), dmod_all)
    res["w_mod"] = _adamw([gw_mod[0], gw_mod[1]], w_mod, m_w_mod, v_w_mod, 0, "adamw_w_mod")
    for names_rep, stack, tag in ((LRU_SMALL, r_lru_small, "lru"), (MOD_SMALL, r_mod_small, "mod")):
        rows = sum(W[n].size for n in names_rep) // 128
        rep = _adamw([stack], *(_flat_rows([src[n] for n in names_rep])[None] for src in (W, M, V)), 8,
                     f"adamw_small_{tag}", rows=rows // 2 if rows % 16 == 0 else rows)
        off = 0
        for name in names_rep:
            size = W[name].size
            res[name] = tuple(t.reshape(-1)[off:off + size].reshape(W[name].shape) for t in rep)
            off += size
    lru_rows = sum(W[n].size for n in LRU_SMALL) // 128
    conv_stack = r_lru_small[:, lru_rows:].reshape(8, 2, CONV_WIDTH, D_MODEL)
    conv_stack = lax.dynamic_slice_in_dim(conv_stack, me * n_conv, n_conv, axis=3).reshape(8, 8, n_conv)
    res["conv_w"] = _adamw([conv_stack], conv_w.reshape(1, 8, n_conv), m_conv_w.reshape(1, 8, n_conv),
                           v_conv_w.reshape(1, 8, n_conv), 8, "adamw_conv_w")

    names = ("w_mod", "b_mod", "g_pre", "w_in", "conv_w", "conv_b", "w_rg", "b_rg", "w_ig", "b_ig",
             "lru_lambda", "w_pa", "w_pb", "w_o", "g_post")
    outs = [loss, grad_x]
    for k in range(4):
        outs.extend(res[n][k].reshape(W[n].shape) for n in names)
    return tuple(outs)
```

```python
import functools

import jax
import jax.numpy as jnp
import numpy as np
from jax import lax
from jax.experimental import pallas as pl
from jax.experimental.pallas import tpu as pltpu

F32 = jnp.float32
BF16 = jnp.bfloat16

N_DEV = 8
D_MODEL = 1024
HEAD_DIM = 128
HEADS = 4
ATT_WIDTH = HEADS * HEAD_DIM
DILATIONS = (1, 4, 16)
BAND = 128
N_CHUNKS = 18
QKV_CHUNKS = 9
CONV_WIDTH = 4
LRU_C = 8.0
NORM_EPS = 1e-6
NEG_INF = -1e30
ADAM_LR = 0.001
ADAM_B1 = 0.9
ADAM_B2 = 0.999
ADAM_EPS = 1e-08
ADAM_WD = 0.01
ADAM_STEP = 10
CHUNK_PERM = (0, 3, 6, 1, 4, 7, 2, 5, 8, 10, 11, 12, 13, 14, 15, 16, 17, 9)
DREST_CHUNKS = 10
VMEM_LIMIT = 56 * 1024 * 1024
MESH_AXES = ("x", "y", "c")


def _params(sem=None):
    return pltpu.CompilerParams(dimension_semantics=sem, vmem_limit_bytes=VMEM_LIMIT)


def _silu(x):
    return x * jax.nn.sigmoid(x)


def _dsilu(x):
    s = jax.nn.sigmoid(x)
    return s * (1.0 + x * (1.0 - s))


def _neg_expm1(x):
    series = -x * (1.0 + x * (0.5 + x * (1.0 / 6.0 + x * (1.0 / 24.0 + x * (1.0 / 120.0)))))
    return jnp.where(x > -0.05, series, 1.0 - jnp.exp(x))


def _softplus_neg(lam):
    z = jnp.exp(-jnp.abs(lam))
    small = z * (1.0 - z * (0.5 - z * (1.0 / 3.0 - z * 0.25)))
    log1p_z = jnp.where(z < 1e-2, small, jnp.log(1.0 + z))
    return jnp.maximum(-lam, 0.0) + log1p_z


def _dot(a, b):
    return jnp.dot(a, b, preferred_element_type=F32)


def _dot_nt(a, b):
    return lax.dot_general(a, b, (((1,), (1,)), ((), ())), preferred_element_type=F32)


def _dot_tn(a, b):
    return lax.dot_general(a, b, (((0,), (0,)), ((), ())), preferred_element_type=F32)


def _perm_matrix(r, rows, transpose=False):
    n = rows // r
    p = np.zeros((rows, rows), np.float32)
    dst = np.arange(rows)
    p[dst, (dst % n) * r + dst // n] = 1.0
    return jnp.asarray(p.T if transpose else p, dtype=BF16)


def _permute_f32(p, x, pieces):
    part = x.astype(BF16)
    acc = _dot(p, part)
    for _ in range(pieces - 1):
        x = x - part.astype(F32)
        part = x.astype(BF16)
        acc = acc + _dot(p, part)
    return acc


def _rows_of_residues(ref, lead=()):
    r = ref.shape[len(lead)]
    if r == 1:
        return ref[lead + (0,)]
    return jnp.concatenate([ref[lead + (rho,)] for rho in range(r)], axis=0)


def _store_residues(ref, x, lead=()):
    r = ref.shape[len(lead)]
    n = x.shape[0] // r
    for rho in range(r):
        ref[lead + (rho,)] = x[rho * n:(rho + 1) * n]


ANY_SPEC = pl.BlockSpec(memory_space=pl.ANY)


def _exchange_shapes(arrs, modes):
    return [jax.ShapeDtypeStruct(((N_DEV,) + a.shape) if mode == "ag" else a.shape, a.dtype)
            for a, mode in zip(arrs, modes)]


def _exchange_sems(n):
    return [pltpu.SemaphoreType.DMA((7 * n,)), pltpu.SemaphoreType.DMA((7 * n,)),
            pltpu.SemaphoreType.DMA((n,))]


def _exchange_copies(ins, outs, modes, sems):
    n = len(ins)
    send_sems, recv_sems, local_sems = sems
    x, y, c = lax.axis_index("x"), lax.axis_index("y"), lax.axis_index("c")
    me = 4 * x + 2 * y + c

    def src_for(a, dev):
        return ins[a] if modes[a] == "ag" else ins[a].at[dev]

    local = [pltpu.make_async_copy(src_for(a, me), outs[a].at[me], local_sems.at[a]) for a in range(n)]
    sends, arrivals = [], []
    for k in range(1, N_DEV):
        px = 1 - x if (k >> 2) & 1 else x
        py = 1 - y if (k >> 1) & 1 else y
        pc = 1 - c if k & 1 else c
        peer = 4 * px + 2 * py + pc
        for a in range(n):
            s = (k - 1) * n + a
            for dst, group in ((me, sends), (peer, arrivals)):
                group.append(pltpu.make_async_remote_copy(
                    src_ref=src_for(a, peer), dst_ref=outs[a].at[dst],
                    send_sem=send_sems.at[s], recv_sem=recv_sems.at[s],
                    device_id=(px, py, pc), device_id_type=pl.DeviceIdType.MESH))
    return local, sends, arrivals


def _exchange_start(copies):
    local, sends, _ = copies
    for cp in local + sends:
        cp.start()


def _exchange_wait(copies):
    local, sends, arrivals = copies
    for cp in arrivals:
        cp.wait_recv()
    for cp in sends:
        cp.wait_send()
    for cp in local:
        cp.wait()


def _exchange(arrs, modes, name):
    n = len(arrs)

    def body(*refs):
        copies = _exchange_copies(refs[:n], refs[n:2 * n], modes, refs[2 * n:])
        _exchange_start(copies)
        _exchange_wait(copies)

    outs = pl.pallas_call(
        body, name=name, out_shape=tuple(_exchange_shapes(arrs, modes)),
        in_specs=[ANY_SPEC] * n, out_specs=tuple([ANY_SPEC] * n),
        scratch_shapes=_exchange_sems(n),
    )(*arrs)
    return list(outs)


def _gather_two_level(arrs, name):
    n = len(arrs)

    def body(*refs):
        ins, outs = refs[:n], refs[n:2 * n]
        send_sems, recv_sems, local_sems = refs[2 * n:]
        x, y, c = lax.axis_index("x"), lax.axis_index("y"), lax.axis_index("c")
        me, sibling = (x, y, c), (x, y, 1 - c)
        chips = [(1 - x, y), (x, 1 - y), (1 - x, 1 - y)]

        def copy(a, k, block, to, src=None):
            slot = 4 * block[0] + 2 * block[1] + block[2]
            return pltpu.make_async_remote_copy(
                src_ref=outs[a].at[slot] if src is None else src, dst_ref=outs[a].at[slot],
                send_sem=send_sems.at[7 * a + k], recv_sem=recv_sems.at[7 * a + k],
                device_id=to, device_id_type=pl.DeviceIdType.MESH)

        mine = [pltpu.make_async_copy(ins[a], outs[a].at[4 * x + 2 * y + c], local_sems.at[a])
                for a in range(n)]
        first = []
        for a in range(n):
            first.append(copy(a, 0, me, sibling, src=ins[a]))
            first += [copy(a, 1 + j, me, (*chip, c), src=ins[a]) for j, chip in enumerate(chips)]
        for cp in mine + first:
            cp.start()
        passed = []
        for j, chip in enumerate(chips):
            for a in range(n):
                copy(a, 1 + j, (*chip, c), me).wait_recv()
                passed.append(copy(a, 4 + j, (*chip, c), sibling))
                passed[-1].start()
        for a in range(n):
            copy(a, 0, sibling, me).wait_recv()
            for j, chip in enumerate(chips):
                copy(a, 4 + j, (*chip, 1 - c), me).wait_recv()
        for cp in first + passed:
            cp.wait_send()
        for cp in mine:
            cp.wait()

    outs = pl.pallas_call(
        body, name=name, out_shape=tuple(_exchange_shapes(arrs, ["ag"] * n)),
        in_specs=[ANY_SPEC] * n, out_specs=tuple([ANY_SPEC] * n),
        scratch_shapes=_exchange_sems(n),
    )(*arrs)
    return list(outs)


SHARD_COLS = N_CHUNKS * ATT_WIDTH // N_DEV


def _weights_full(shards):
    return jnp.transpose(shards, (1, 0, 2)).reshape(D_MODEL, N_CHUNKS * ATT_WIDTH)


def _wcols(w_ref, k):
    orig = CHUNK_PERM[k]
    return w_ref[:, orig * ATT_WIDTH:(orig + 1) * ATT_WIDTH]


def _grads_to_shards(parts):
    offsets = np.cumsum([0] + [p.shape[1] for p in parts])
    chunks = []
    for orig in range(N_CHUNKS):
        col = ATT_WIDTH * CHUNK_PERM.index(orig)
        part = int(np.searchsorted(offsets, col, side="right")) - 1
        chunks.append(parts[part][:, col - int(offsets[part]):col - int(offsets[part]) + ATT_WIDTH])
    full = jnp.concatenate(chunks, axis=1)
    return jnp.transpose(full.reshape(D_MODEL, N_DEV, SHARD_COLS), (1, 0, 2))


def _call(body, args, *, name, grid, in_specs, out_specs, out_shape, scratch_shapes=(), aliases=None,
          hosted=None):
    sem = ("arbitrary",) * len(grid)
    if hosted is None:
        outs = pl.pallas_call(
            body, name=name, grid=grid, in_specs=in_specs, out_specs=tuple(out_specs),
            out_shape=tuple(out_shape), scratch_shapes=list(scratch_shapes),
            input_output_aliases=aliases or {}, compiler_params=_params(sem))(*args)
        return list(outs), []
    x_arrs, modes = hosted
    n_in, n_out, n_scr, nx = len(args), len(out_shape), len(scratch_shapes), len(x_arrs)

    def wrapped(*refs):
        ins, x_ins = refs[:n_in], refs[n_in:n_in + nx]
        outs = refs[n_in + nx:n_in + nx + n_out]
        x_outs = refs[n_in + nx + n_out:n_in + 2 * nx + n_out]
        scr = refs[n_in + 2 * nx + n_out:n_in + 2 * nx + n_out + n_scr]
        sems = refs[n_in + 2 * nx + n_out + n_scr:]
        first = pl.program_id(0) == 0
        last = pl.program_id(0) == grid[0] - 1
        for axis in range(1, len(grid)):
            first = jnp.logical_and(first, pl.program_id(axis) == 0)
            last = jnp.logical_and(last, pl.program_id(axis) == grid[axis] - 1)

        @pl.when(first)
        def _():
            _exchange_start(_exchange_copies(x_ins, x_outs, modes, sems))

        body(*ins, *outs, *scr)

        @pl.when(last)
        def _():
            _exchange_wait(_exchange_copies(x_ins, x_outs, modes, sems))

    outs = pl.pallas_call(
        wrapped, name=name, grid=grid, in_specs=list(in_specs) + [ANY_SPEC] * nx,
        out_specs=tuple(out_specs) + tuple([ANY_SPEC] * nx),
        out_shape=tuple(out_shape) + tuple(_exchange_shapes(x_arrs, modes)),
        scratch_shapes=list(scratch_shapes) + _exchange_sems(nx),
        input_output_aliases=aliases or {}, compiler_params=_params(sem))(*args, *x_arrs)
    return list(outs[:n_out]), list(outs[n_out:])


def _mod_fwd(c_pad, w_mod):
    def body(c_ref, w_ref, o_ref):
        sc = _silu(c_ref[...]).astype(BF16)
        for layer in range(2):
            o_ref[layer] = _dot(sc, w_ref[layer].astype(BF16))

    return pl.pallas_call(
        body, name="mod_fwd", out_shape=jax.ShapeDtypeStruct((2, 16, w_mod.shape[2]), F32),
        compiler_params=_params())(c_pad, w_mod)


def _mod_wgrad(c_t, dmod):
    n_cols = dmod.shape[2]

    def body(c_ref, d_ref, o_ref):
        sc = _silu(c_ref[...]).astype(BF16).astype(F32)
        for layer in range(2):
            dm = d_ref[layer].astype(BF16).astype(F32)
            acc = sc[:, 0:1] * dm[0:1, :]
            for b in range(1, N_DEV):
                acc = acc + sc[:, b:b + 1] * dm[b:b + 1, :]
            o_ref[layer] = acc

    return pl.pallas_call(
        body, name="mod_wgrad", out_shape=jax.ShapeDtypeStruct((2, D_MODEL, n_cols), F32),
        compiler_params=_params())(c_t, dmod)


GROUP_COLS = 3 * ATT_WIDTH
PROJ_TM = 256


def _norm_proj_fwd(x, vec, w_int, hosted=None):
    S = x.shape[0]
    tm = PROJ_TM
    perms = [_perm_matrix(r, tm) for r in DILATIONS[1:]]

    def body(x_ref, v_ref, w_ref, p1_ref, p2_ref, ht0_ref, ht1_ref, ht2_ref, q0_ref, q1_ref, q2_ref,
             rest_ref):
        xv = x_ref[...]
        rstd = lax.rsqrt(jnp.mean(xv * xv, axis=-1, keepdims=True) + NORM_EPS)
        hf = ((xv * rstd) * v_ref[0:1, :]) * (1.0 + v_ref[1:2, :]) + v_ref[2:3, :]
        h = hf.astype(BF16)
        for g, (q_ref, ht_ref, p_ref) in enumerate(((q0_ref, ht0_ref, None), (q1_ref, ht1_ref, p1_ref),
                                                    (q2_ref, ht2_ref, p2_ref))):
            rows_f = hf if p_ref is None else _dot(p_ref[...], h)
            ht_ref[...] = rows_f.T.astype(BF16)
            rows = rows_f.astype(BF16)
            n = tm // q_ref.shape[0]
            for t in range(3):
                res = _dot(rows, _wcols(w_ref, 3 * g + t)).astype(BF16)
                for rho in range(q_ref.shape[0]):
                    q_ref[rho, :, t * ATT_WIDTH:(t + 1) * ATT_WIDTH] = res[rho * n:(rho + 1) * n]
        split = (QKV_CHUNKS - 1) * ATT_WIDTH
        rest_ref[:, :split] = _dot(h, w_ref[:, (QKV_CHUNKS + 1) * ATT_WIDTH:])
        rest_ref[:, split:] = _dot(h, w_ref[:, QKV_CHUNKS * ATT_WIDTH:(QKV_CHUNKS + 1) * ATT_WIDTH])

    n_rest = QKV_CHUNKS * ATT_WIDTH
    whole = pl.BlockSpec(memory_space=pltpu.VMEM)
    ht_spec = pl.BlockSpec((D_MODEL, tm), lambda i: (0, i))
    ht_shape = jax.ShapeDtypeStruct((D_MODEL, S), BF16)
    return _call(
        body, (x, vec, w_int, *perms), name="norm_proj_fwd", grid=(S // tm,),
        in_specs=[pl.BlockSpec((tm, D_MODEL), lambda i: (i, 0)),
                  pl.BlockSpec((8, D_MODEL), lambda i: (0, 0)), whole, whole, whole],
        out_specs=(ht_spec, ht_spec, ht_spec,
                   *[pl.BlockSpec((r, tm // r, GROUP_COLS), lambda i: (0, i, 0)) for r in DILATIONS],
                   pl.BlockSpec((tm, n_rest), lambda i: (i, 0))),
        out_shape=(ht_shape, ht_shape, ht_shape,
                   *[jax.ShapeDtypeStruct((r, S // r, GROUP_COLS), BF16) for r in DILATIONS],
                   jax.ShapeDtypeStruct((S, n_rest), F32)),
        hosted=hosted)


def _proj_dgrad_norm_bwd(dqkv, drest, w_int, x, dxn, vec, hosted=None):
    S = x.shape[0]
    tm = PROJ_TM
    perms = [_perm_matrix(r, tm, transpose=True) for r in DILATIONS[1:]]

    def body(d0_ref, d1_ref, d2_ref, dr_ref, w_ref, p1_ref, p2_ref, x_ref, dxn_ref, v_ref, dx_ref, acc_ref):
        i = pl.program_id(0)

        @pl.when(i == 0)
        def _():
            acc_ref[...] = jnp.zeros_like(acc_ref)

        wcols = functools.partial(_wcols, w_ref)
        dhv = None
        for g, (d_ref, p_ref) in enumerate(((d0_ref, None), (d1_ref, p1_ref), (d2_ref, p2_ref))):
            for t in range(3):
                d = _rows_of_residues(d_ref, (t,))
                if p_ref is not None:
                    d = _dot(p_ref[...], d).astype(BF16)
                part = _dot_nt(d, wcols(3 * g + t))
                dhv = part if dhv is None else dhv + part
        for t in range(7):
            dhv = dhv + _dot_nt(dr_ref[t], wcols(11 + t))
        for t in range(2):
            dhv = dhv + _dot_nt(dr_ref[8 + t], wcols(9 + t))

        xv = x_ref[...]
        g = v_ref[0:1, :]
        sc1 = 1.0 + v_ref[1:2, :]
        rstd = lax.rsqrt(jnp.mean(xv * xv, axis=-1, keepdims=True) + NORM_EPS)
        xhat = xv * rstd
        acc_ref[0:1, :] += jnp.sum(dhv, axis=0, keepdims=True)
        acc_ref[1:2, :] += jnp.sum(dhv * (xhat * g), axis=0, keepdims=True)
        acc_ref[2:3, :] += jnp.sum(dhv * xhat * sc1, axis=0, keepdims=True)
        dxhat = dhv * (g * sc1)
        dx = rstd * (dxhat - xhat * jnp.mean(dxhat * xhat, axis=-1, keepdims=True))
        dx_ref[...] = dx + dxn_ref[...]

    row = pl.BlockSpec((tm, D_MODEL), lambda i: (i, 0))
    vec_spec = pl.BlockSpec((8, D_MODEL), lambda i: (0, 0))
    whole = pl.BlockSpec(memory_space=pltpu.VMEM)
    return _call(
        body, (*dqkv, drest, w_int, *perms, x, dxn, vec), name="proj_dgrad_norm_bwd", grid=(S // tm,),
        in_specs=[*[pl.BlockSpec((3, r, tm // r, ATT_WIDTH), lambda i: (0, 0, i, 0)) for r in DILATIONS],
                  pl.BlockSpec((DREST_CHUNKS, tm, ATT_WIDTH), lambda i: (0, i, 0)),
                  whole, whole, whole, row, row, vec_spec],
        out_specs=(row, vec_spec),
        out_shape=(jax.ShapeDtypeStruct((S, D_MODEL), F32), jax.ShapeDtypeStruct((8, D_MODEL), F32)),
        hosted=hosted)


def _proj_wgrad_part(h_t, d, n_chunks, chunk_block, name):
    S = h_t.shape[1]
    ts = 512
    n_steps = S // ts

    def body(h_ref, d_ref, o_ref, acc_ref):
        i = pl.program_id(0)

        @pl.when(i == 0)
        def _():
            acc_ref[...] = jnp.zeros_like(acc_ref)

        ht = h_ref[...]
        for t in range(n_chunks):
            acc_ref[:, t * ATT_WIDTH:(t + 1) * ATT_WIDTH] += _dot(ht, d_ref[t])

        @pl.when(i == n_steps - 1)
        def _():
            o_ref[...] = acc_ref[...].astype(BF16)

    return pl.pallas_call(
        body, name=name, grid=(n_steps,),
        in_specs=[pl.BlockSpec((D_MODEL, ts), lambda i: (0, i)),
                  pl.BlockSpec((n_chunks, ts, ATT_WIDTH), lambda i: (chunk_block, i, 0))],
        out_specs=pl.BlockSpec((D_MODEL, n_chunks * ATT_WIDTH), lambda i: (0, 0)),
        out_shape=jax.ShapeDtypeStruct((D_MODEL, n_chunks * ATT_WIDTH), BF16),
        scratch_shapes=[pltpu.VMEM((D_MODEL, n_chunks * ATT_WIDTH), F32)],
        compiler_params=_params(("arbitrary",)))(h_t, d)


def _proj_wgrad_group(h_t, d, r, name):
    S = h_t.shape[1]
    ts = 512
    n_steps = S // ts
    n = PROJ_TM // r

    def body(h_ref, d_ref, o_ref, acc_ref):
        i = pl.program_id(0)

        @pl.when(i == 0)
        def _():
            acc_ref[...] = jnp.zeros_like(acc_ref)

        ht = h_ref[...]
        for t in range(3):
            rows = jnp.concatenate([d_ref[t, rho, s * n:(s + 1) * n] for s in range(ts // PROJ_TM)
                                    for rho in range(r)], axis=0) if r > 1 else d_ref[t, 0]
            acc_ref[:, t * ATT_WIDTH:(t + 1) * ATT_WIDTH] += _dot(ht, rows)

        @pl.when(i == n_steps - 1)
        def _():
            o_ref[...] = acc_ref[...].astype(BF16)

    return pl.pallas_call(
        body, name=name, grid=(n_steps,),
        in_specs=[pl.BlockSpec((D_MODEL, ts), lambda i: (0, i)),
                  pl.BlockSpec((3, r, ts // r, ATT_WIDTH), lambda i: (0, 0, i, 0))],
        out_specs=pl.BlockSpec((D_MODEL, GROUP_COLS), lambda i: (0, 0)),
        out_shape=jax.ShapeDtypeStruct((D_MODEL, GROUP_COLS), BF16),
        scratch_shapes=[pltpu.VMEM((D_MODEL, GROUP_COLS), F32)],
        compiler_params=_params(("arbitrary",)))(h_t, d)


def _wgrad_tn(a, b, name):
    S, M = a.shape
    N = b.shape[1]
    ts = 1024

    def body(a_ref, b_ref, o_ref):
        i = pl.program_id(0)
        part = _dot_tn(a_ref[...], b_ref[...])

        @pl.when(i == 0)
        def _():
            o_ref[...] = part

        @pl.when(i > 0)
        def _():
            o_ref[...] += part

    return pl.pallas_call(
        body, name=name, grid=(S // ts,),
        in_specs=[pl.BlockSpec((ts, M), lambda i: (i, 0)), pl.BlockSpec((ts, N), lambda i: (i, 0))],
        out_specs=pl.BlockSpec((M, N), lambda i: (0, 0)),
        out_shape=jax.ShapeDtypeStruct((M, N), F32),
        compiler_params=_params(("arbitrary",)))(a, b)


def _attn_fwd(qkv, g, r):
    L = qkv.shape[1]
    nb2 = L // (2 * BAND)
    scale = HEAD_DIM ** -0.5

    def body(q_ref, kp_ref, kc_ref, vp_ref, vc_ref, o_ref, st_ref):
        m_step = pl.program_id(1)
        ii = lax.broadcasted_iota(jnp.int32, (BAND, 2 * BAND), 0)
        kk = lax.broadcasted_iota(jnp.int32, (BAND, 2 * BAND), 1)
        band = jnp.logical_and(kk >= ii, kk <= ii + BAND)
        masks = (jnp.logical_and(band, jnp.logical_or(kk >= BAND, m_step > 0)), band)
        lane = lax.broadcasted_iota(jnp.int32, (BAND, 128), 1)
        stats = [jnp.zeros((BAND, 128), F32), jnp.zeros((BAND, 128), F32)]
        for h in range(HEADS):
            sl = slice(h * HEAD_DIM, (h + 1) * HEAD_DIM)
            kc, vc = kc_ref[:, sl], vc_ref[:, sl]
            keys = (jnp.concatenate([kp_ref[:, sl], kc[:BAND]], axis=0), kc)
            vals = (jnp.concatenate([vp_ref[:, sl], vc[:BAND]], axis=0), vc)
            for b in range(2):
                rows = slice(b * BAND, (b + 1) * BAND)
                s = jnp.where(masks[b], _dot_nt(q_ref[rows, sl], keys[b]) * scale, NEG_INF)
                m = jnp.max(s, axis=1, keepdims=True)
                p = jnp.exp(s - m)
                l = jnp.sum(p, axis=1, keepdims=True)
                o_ref[rows, sl] = _dot(p.astype(BF16), vals[b]) / l
                stats[b] = jnp.where(lane == h, m, stats[b])
                stats[b] = jnp.where(lane == HEADS + h, l, stats[b])
        st_ref[0:BAND, :] = stats[0]
        st_ref[BAND:, :] = stats[1]

    two = (None, 2 * BAND, ATT_WIDTH)
    one = (None, BAND, ATT_WIDTH)

    def prev(m):
        return jnp.maximum(2 * m - 1, 0)

    return pl.pallas_call(
        body, name=f"attn_fwd_g{g}", grid=(r, nb2),
        in_specs=[
            pl.BlockSpec(two, lambda rho, m: (rho, m, 0)),
            pl.BlockSpec(one, lambda rho, m: (rho, prev(m), 1)),
            pl.BlockSpec(two, lambda rho, m: (rho, m, 1)),
            pl.BlockSpec(one, lambda rho, m: (rho, prev(m), 2)),
            pl.BlockSpec(two, lambda rho, m: (rho, m, 2)),
        ],
        out_specs=(pl.BlockSpec(two, lambda rho, m: (rho, m, 0)),
                   pl.BlockSpec((None, 2 * BAND, 128), lambda rho, m: (rho, m, 0))),
        out_shape=(jax.ShapeDtypeStruct((r, L, ATT_WIDTH), F32),
                   jax.ShapeDtypeStruct((r, L, 128), F32)),
        compiler_params=_params(("parallel", "parallel")))(qkv, qkv, qkv, qkv, qkv)


def _attn_combine(outs, stats, rest):
    S = rest.shape[0]
    tm = 512
    gatt_blk = 8
    back = [_perm_matrix(r, tm, transpose=True) for r in DILATIONS[1:]]
    forth = [_perm_matrix(r, tm) for r in DILATIONS[1:]]

    def body(o0_ref, o1_ref, o2_ref, s0_ref, s1_ref, s2_ref, g_ref, b1_ref, b2_ref, f1_ref, f2_ref,
             o_ref, a_ref, l0_ref, l1_ref, l2_ref):
        outs_nat = [o0_ref[0]] + [_permute_f32(b_ref[...], _rows_of_residues(o_g), 2)
                                  for o_g, b_ref in ((o1_ref, b1_ref), (o2_ref, b2_ref))]
        st = [s0_ref[0]] + [_permute_f32(b_ref[...], _rows_of_residues(s_g), 3)
                            for s_g, b_ref in ((s1_ref, b1_ref), (s2_ref, b2_ref))]
        lane = lax.broadcasted_iota(jnp.int32, (tm, 128), 1)
        lse_out = jnp.zeros((tm, 128), F32)
        for h in range(HEADS):
            sl = slice(h * HEAD_DIM, (h + 1) * HEAD_DIM)
            ms = [s[:, h:h + 1] for s in st]
            ls = [s[:, HEADS + h:HEADS + h + 1] for s in st]
            m_all = jnp.maximum(jnp.maximum(ms[0], ms[1]), ms[2])
            ws = [l * jnp.exp(m - m_all) for m, l in zip(ms, ls)]
            den = ws[0] + ws[1] + ws[2]
            o = (ws[0] * outs_nat[0][:, sl] + ws[1] * outs_nat[1][:, sl] + ws[2] * outs_nat[2][:, sl]) / den
            o_ref[:, sl] = o
            a_ref[:, sl] = (o * _silu(g_ref[:, sl])).astype(BF16)
            lse_out = jnp.where(lane == h, m_all + jnp.log(den), lse_out)
        l0_ref[0] = lse_out
        for l_ref, f_ref in ((l1_ref, f1_ref), (l2_ref, f2_ref)):
            _store_residues(l_ref, _permute_f32(f_ref[...], lse_out, 3))

    o_spec = pl.BlockSpec((tm, ATT_WIDTH), lambda i: (i, 0))
    whole = pl.BlockSpec(memory_space=pltpu.VMEM)

    def res_spec(r, width):
        return pl.BlockSpec((r, tm // r, width), lambda i: (0, i, 0))

    return pl.pallas_call(
        body, name="attn_combine", grid=(S // tm,),
        in_specs=[*[res_spec(r, ATT_WIDTH) for r in DILATIONS], *[res_spec(r, 128) for r in DILATIONS],
                  pl.BlockSpec((tm, ATT_WIDTH), lambda i: (i, gatt_blk)), whole, whole, whole, whole],
        out_specs=(o_spec, o_spec, *[res_spec(r, 128) for r in DILATIONS]),
        out_shape=(jax.ShapeDtypeStruct((S, ATT_WIDTH), F32), jax.ShapeDtypeStruct((S, ATT_WIDTH), BF16),
                   *[jax.ShapeDtypeStruct((r, S // r, 128), F32) for r in DILATIONS]),
        compiler_params=_params(("parallel",)))(*outs, *stats, rest, *back, *forth)


def _attn_bwd(qkv, do, lse, dvec, g, r, hosted=None):
    L = qkv.shape[1]
    nb = L // BAND
    nb2 = nb // 2
    scale = HEAD_DIM ** -0.5

    def body(qc_ref, qn_ref, k_ref, v_ref, doc_ref, don_ref, lc_ref, ln_ref, dc_ref, dn_ref,
             out_ref, carry_ref):
        j = pl.program_id(1)

        @pl.when(j == 0)
        def _():
            carry_ref[...] = jnp.zeros_like(carry_ref)

        ii = lax.broadcasted_iota(jnp.int32, (3 * BAND, 2 * BAND), 0)
        kk = lax.broadcasted_iota(jnp.int32, (3 * BAND, 2 * BAND), 1)
        mask = jnp.logical_and(jnp.logical_and(kk <= ii, kk >= ii - BAND),
                               jnp.logical_or(ii < 2 * BAND, j < nb2 - 1))
        lse3 = jnp.concatenate([lc_ref[...], ln_ref[...]], axis=0)
        dvec3 = jnp.concatenate([dc_ref[...], dn_ref[...]], axis=0)
        for h in range(HEADS):
            sl = slice(h * HEAD_DIM, (h + 1) * HEAD_DIM)
            k = k_ref[:, sl]
            v = v_ref[:, sl]
            q = jnp.concatenate([qc_ref[:, sl], qn_ref[:, sl]], axis=0)
            do = jnp.concatenate([doc_ref[:, sl], don_ref[:, sl]], axis=0)
            p = jnp.where(mask, jnp.exp(_dot_nt(q, k) * scale - lse3[:, h:h + 1]), 0.0)
            ds = (p * (_dot_nt(do, v) - dvec3[:, h:h + 1])).astype(BF16)
            dq3 = _dot(ds, k) * scale
            out_ref[0, 0:BAND, sl] = (carry_ref[:, sl] + dq3[:BAND]).astype(BF16)
            out_ref[0, BAND:, sl] = dq3[BAND:2 * BAND].astype(BF16)
            out_ref[1, :, sl] = (_dot_tn(ds, q) * scale).astype(BF16)
            out_ref[2, :, sl] = _dot_tn(p.astype(BF16), do).astype(BF16)
            carry_ref[:, sl] = dq3[2 * BAND:]

    two = (None, 2 * BAND, ATT_WIDTH)
    one = (None, BAND, ATT_WIDTH)
    stwo = (None, 2 * BAND, 128)
    sone = (None, BAND, 128)

    def nxt(j):
        return jnp.minimum(2 * j + 2, nb - 1)

    (out,), got = _call(
        body, (qkv, qkv, qkv, qkv, do, do, lse, lse, dvec, dvec), name=f"attn_bwd_g{g}", grid=(r, nb2),
        in_specs=[
            pl.BlockSpec(two, lambda rho, j: (rho, j, 0)),
            pl.BlockSpec(one, lambda rho, j: (rho, nxt(j), 0)),
            pl.BlockSpec(two, lambda rho, j: (rho, j, 1)),
            pl.BlockSpec(two, lambda rho, j: (rho, j, 2)),
            pl.BlockSpec(two, lambda rho, j: (rho, j, 0)),
            pl.BlockSpec(one, lambda rho, j: (rho, nxt(j), 0)),
            pl.BlockSpec(stwo, lambda rho, j: (rho, j, 0)),
            pl.BlockSpec(sone, lambda rho, j: (rho, nxt(j), 0)),
            pl.BlockSpec(stwo, lambda rho, j: (rho, j, 0)),
            pl.BlockSpec(sone, lambda rho, j: (rho, nxt(j), 0)),
        ],
        out_specs=(pl.BlockSpec((3, None, 2 * BAND, ATT_WIDTH), lambda rho, j: (0, rho, j, 0)),),
        out_shape=(jax.ShapeDtypeStruct((3, r, L, ATT_WIDTH), BF16),),
        scratch_shapes=[pltpu.VMEM((BAND, ATT_WIDTH), F32)],
        hosted=hosted)
    return out, got


LRU_T = 256


def _linear_scan(a, b, carry, reverse):
    T = a.shape[0]
    row8 = lax.broadcasted_iota(jnp.int32, a.shape, 0) & 7
    for s in (1, 2, 4):
        keep = (row8 < 8 - s) if reverse else (row8 >= s)
        shift = T - s if reverse else s
        b_sh = jnp.where(keep, pltpu.roll(b, shift, 0), 0.0)
        a_sh = jnp.where(keep, pltpu.roll(a, shift, 0), 1.0)
        b = a * b_sh + b
        a = a * a_sh
    tiles = [None] * (T // 8)
    order = range(T // 8 - 1, -1, -1) if reverse else range(T // 8)
    for k in order:
        y = b[8 * k:8 * k + 8] + a[8 * k:8 * k + 8] * carry
        tiles[k] = y
        carry = y[0:1] if reverse else y[7:8]
    return jnp.concatenate(tiles, axis=0), carry


def _gate_matmuls(ucb, w_ref, bias):
    parts = [_dot(ucb[:, j * 128:(j + 1) * 128], w_ref[j]) for j in range(8)]
    return jnp.concatenate(parts, axis=1) + bias


def _conv_fwd(u, u_prev8, cw_ref):
    T = u.shape[0]
    ue = jnp.concatenate([u_prev8, u], axis=0)
    uc = cw_ref[4:5, :] + cw_ref[0:1, :] * u
    for j in range(1, CONV_WIDTH):
        uc = uc + cw_ref[j:j + 1, :] * pltpu.roll(ue, j, 0)[8:8 + T]
    return uc


def _lru_fwd(rest, cw, wr2, wi2, hosted=None):
    S = rest.shape[0]
    T = LRU_T

    def body(u_ref, cw_ref, wr_ref, wi_ref, h_ref, a_ref, uc_ref, r_ref, ig_ref, mult_ref,
             ucar_ref, hcar_ref):
        c = pl.program_id(0)

        @pl.when(c == 0)
        def _():
            ucar_ref[...] = jnp.zeros_like(ucar_ref)
            hcar_ref[...] = jnp.zeros_like(hcar_ref)

        u = u_ref[...]
        uc = _conv_fwd(u, ucar_ref[...], cw_ref)
        ucar_ref[...] = u[T - 8:, :]
        ucb = uc.astype(BF16)
        r = jax.nn.sigmoid(_gate_matmuls(ucb, wr_ref, cw_ref[5:6, :]))
        ig = jax.nn.sigmoid(_gate_matmuls(ucb, wi_ref, cw_ref[6:7, :]))
        log_a = -LRU_C * r * _softplus_neg(cw_ref[7:8, :])
        a = jnp.exp(log_a)
        mult = jnp.sqrt(_neg_expm1(2.0 * log_a))
        b = mult * (ig * uc)
        a_ref[...] = a
        uc_ref[...] = uc
        r_ref[...] = r
        ig_ref[...] = ig
        mult_ref[...] = mult
        h, last = _linear_scan(a, b, hcar_ref[0:1, :], reverse=False)
        h_ref[...] = h
        hcar_ref[...] = jnp.broadcast_to(last, (8, D_MODEL))

    row_spec = pl.BlockSpec((T, D_MODEL), lambda c: (c, 0))
    row_shape = jax.ShapeDtypeStruct((S, D_MODEL), F32)
    return _call(
        body, (rest, cw, wr2, wi2), name="lru_fwd", grid=(S // T,),
        in_specs=[row_spec, pl.BlockSpec((8, D_MODEL), lambda c: (0, 0)),
                  pl.BlockSpec((8, 128, 128), lambda c: (0, 0, 0)),
                  pl.BlockSpec((8, 128, 128), lambda c: (0, 0, 0))],
        out_specs=(row_spec,) * 6, out_shape=(row_shape,) * 6,
        scratch_shapes=[pltpu.VMEM((8, D_MODEL), F32), pltpu.VMEM((8, D_MODEL), F32)],
        hosted=hosted)


def _lru_bwd(dhl, hl, a_all, gates, rest, cw, wr2, wi2, dproj, hosted=None):
    S = rest.shape[0]
    T = LRU_T
    nc = S // T

    def body(dh_ref, h_ref, hp_ref, a_ref, uc_ref, r_ref, ig_ref, mult_ref, u_ref, up_ref, cw_ref, wr_ref,
             wi_ref, _alias, du_ref, acc_ref, gwr_ref, gwi_ref, gcar_ref, acar_ref, dcar_ref):
        step = pl.program_id(0)
        c = nc - 1 - step

        @pl.when(step == 0)
        def _():
            acc_ref[...] = jnp.zeros_like(acc_ref)
            gwr_ref[...] = jnp.zeros_like(gwr_ref)
            gwi_ref[...] = jnp.zeros_like(gwi_ref)
            gcar_ref[...] = jnp.zeros_like(gcar_ref)
            acar_ref[...] = jnp.zeros_like(acar_ref)
            dcar_ref[...] = jnp.zeros_like(dcar_ref)

        row = lax.broadcasted_iota(jnp.int32, (T, D_MODEL), 0)
        u = u_ref[...]
        u_prev = jnp.where(c > 0, up_ref[...], 0.0)
        h_prev8 = jnp.where(c > 0, hp_ref[...], 0.0)
        a = a_ref[...]
        h = h_ref[...]
        uc, r, ig, mult = uc_ref[...], r_ref[...], ig_ref[...], mult_ref[...]
        ucb = uc.astype(BF16)
        sp = _softplus_neg(cw_ref[7:8, :])
        a_next = jnp.where(row < T - 1, pltpu.roll(a, T - 1, 0), acar_ref[0:1, :])
        G, first = _linear_scan(a_next, dh_ref[...], gcar_ref[0:1, :], reverse=True)
        gcar_ref[...] = jnp.broadcast_to(first, (8, D_MODEL))
        acar_ref[...] = jnp.broadcast_to(a[0:1, :], (8, D_MODEL))
        he = jnp.concatenate([h_prev8, h], axis=0)
        h_before = pltpu.roll(he, 1, 0)[8:8 + T]
        d_a = G * h_before
        d_mult = G * (ig * uc)
        d_ig = G * (mult * uc)
        duc = G * (mult * ig)
        d_log_a = d_a * a - d_mult * (a * a) / mult
        d_r = d_log_a * (-LRU_C * sp)
        d_sp = jnp.sum(d_log_a * (-LRU_C * r), axis=0, keepdims=True)
        dpre_r = d_r * r * (1.0 - r)
        dpre_i = d_ig * ig * (1.0 - ig)
        dprb = dpre_r.astype(BF16)
        dpib = dpre_i.astype(BF16)
        back = []
        for j in range(8):
            sl = slice(j * 128, (j + 1) * 128)
            back.append(_dot_nt(dprb[:, sl], wr_ref[j]) + _dot_nt(dpib[:, sl], wi_ref[j]))
            gwr_ref[j] += _dot_tn(ucb[:, sl], dprb[:, sl])
            gwi_ref[j] += _dot_tn(ucb[:, sl], dpib[:, sl])
        duc = duc + jnp.concatenate(back, axis=1)
        de = jnp.concatenate([duc, dcar_ref[...]], axis=0)
        du = cw_ref[0:1, :] * duc
        ue = jnp.concatenate([u_prev, u], axis=0)
        acc_ref[0:1, :] += jnp.sum(duc * u, axis=0, keepdims=True)
        for j in range(1, CONV_WIDTH):
            du = du + cw_ref[j:j + 1, :] * pltpu.roll(de, T + 8 - j, 0)[0:T]
            acc_ref[j:j + 1, :] += jnp.sum(duc * pltpu.roll(ue, j, 0)[8:8 + T], axis=0, keepdims=True)
        dcar_ref[...] = duc[0:8, :]
        acc_ref[4:5, :] += jnp.sum(duc, axis=0, keepdims=True)
        acc_ref[5:6, :] += jnp.sum(dpre_r, axis=0, keepdims=True)
        acc_ref[6:7, :] += jnp.sum(dpre_i, axis=0, keepdims=True)
        acc_ref[7:8, :] += d_sp
        du_ref[0] = du[:, 0:ATT_WIDTH].astype(BF16)
        du_ref[1] = du[:, ATT_WIDTH:].astype(BF16)

    def rev(step):
        return nc - 1 - step

    def prev8(step):
        return jnp.maximum(rev(step) * (T // 8) - 1, 0)

    row_spec = pl.BlockSpec((T, D_MODEL), lambda s: (rev(s), 0))
    prev_spec = pl.BlockSpec((8, D_MODEL), lambda s: (prev8(s), 0))
    vec_spec = pl.BlockSpec((8, D_MODEL), lambda s: (0, 0))
    w_spec = pl.BlockSpec((8, 128, 128), lambda s: (0, 0, 0))
    return _call(
        body, (dhl, hl, hl, a_all, *gates, rest, rest, cw, wr2, wi2, dproj), name="lru_bwd", grid=(nc,),
        in_specs=[row_spec, row_spec, prev_spec, row_spec, row_spec, row_spec, row_spec, row_spec,
                  row_spec, prev_spec, vec_spec, w_spec, w_spec, ANY_SPEC],
        out_specs=(pl.BlockSpec((2, T, ATT_WIDTH), lambda s: (4, rev(s), 0)), vec_spec, w_spec, w_spec),
        out_shape=(jax.ShapeDtypeStruct(dproj.shape, BF16), jax.ShapeDtypeStruct((8, D_MODEL), F32),
                   jax.ShapeDtypeStruct((8, 128, 128), F32), jax.ShapeDtypeStruct((8, 128, 128), F32)),
        scratch_shapes=[pltpu.VMEM((8, D_MODEL), F32), pltpu.VMEM((8, D_MODEL), F32),
                        pltpu.VMEM((8, D_MODEL), F32)],
        aliases={13: 0}, hosted=hosted)


OUT_TM = 256


def _out_fwd(a_gated, hl, rest, x, vec, w_pa, w_pb, w_o, target=None, hosted=None):
    S = x.shape[0]
    tm = OUT_TM
    with_loss = target is not None

    def body(*refs):
        (ag_ref, hl_ref, gl_ref, ma_ref, mb_ref, x_ref, v_ref, wpa_ref, wpb_ref, wo_ref), refs = refs[:10], refs[10:]
        if with_loss:
            t_ref, refs = refs[0], refs[1:]
        xn_ref, mix_ref, ya_ref, yb_ref, bg_ref = refs[:5]
        bg = (hl_ref[...] * _silu(gl_ref[...])).astype(BF16)
        ya = _dot(ag_ref[...], wpa_ref[...])
        yb = _dot(bg, wpb_ref[...])
        mix = (jax.nn.sigmoid(ma_ref[...]) * ya + jax.nn.sigmoid(mb_ref[...]) * yb).astype(BF16)
        out = _dot(mix, wo_ref[...])
        rstd = lax.rsqrt(jnp.mean(out * out, axis=-1, keepdims=True) + NORM_EPS)
        x_new = x_ref[...] + v_ref[1:2, :] * ((out * rstd) * v_ref[0:1, :])
        if with_loss:
            acc_ref = refs[5]

            @pl.when(pl.program_id(0) == 0)
            def _():
                acc_ref[...] = jnp.zeros_like(acc_ref)

            err = x_new - t_ref[...]
            xn_ref[...] = err * (1.0 / D_MODEL)
            part = jnp.sum(jnp.sum(err * err, axis=1, keepdims=True), axis=0, keepdims=True)
            acc_ref[...] += jnp.broadcast_to(part, acc_ref.shape)
        else:
            xn_ref[...] = x_new
        mix_ref[...] = mix
        ya_ref[...] = ya.astype(BF16)
        yb_ref[...] = yb.astype(BF16)
        bg_ref[...] = bg

    def col(j):
        return pl.BlockSpec((tm, D_MODEL), lambda i: (i, j))

    def whole(shape):
        return pl.BlockSpec(shape, lambda i: (0, 0))

    row = col(0)
    bf = jax.ShapeDtypeStruct((S, D_MODEL), BF16)
    loss_in = ([target], [row]) if with_loss else ([], [])
    loss_out = ([jax.ShapeDtypeStruct((8, 128), F32)], [whole((8, 128))]) if with_loss else ([], [])
    return _call(
        body, (a_gated, hl, rest, rest, rest, x, vec, w_pa, w_pb, w_o, *loss_in[0]), name="out_fwd",
        grid=(S // tm,),
        in_specs=[pl.BlockSpec((tm, ATT_WIDTH), lambda i: (i, 0)), row, col(1), col(2), col(3), row,
                  whole((8, D_MODEL)), whole((ATT_WIDTH, D_MODEL)), whole((D_MODEL, D_MODEL)),
                  whole((D_MODEL, D_MODEL)), *loss_in[1]],
        out_specs=(row, row, row, row, row, *loss_out[1]),
        out_shape=(jax.ShapeDtypeStruct((S, D_MODEL), F32), bf, bf, bf, bf, *loss_out[0]),
        hosted=hosted)


def _out_bwd(dxn, mix, ya, yb, hl, rest, o_att, vec, w_pa, w_pb, w_o, hosted=None):
    S = dxn.shape[0]
    tm = OUT_TM
    forth = [_perm_matrix(r, tm) for r in DILATIONS[1:]]

    def body(dxn_ref, mix_ref, ya_ref, yb_ref, hl_ref, gl_ref, ma_ref, mb_ref, ga_ref, o_ref, v_ref,
             wpa_ref, wpb_ref, wo_ref, f1_ref, f2_ref,
             dout_ref, dya_ref, dyb_ref, dhl_ref, dp_ref, acc_ref,
             do0_ref, do1_ref, do2_ref, dv0_ref, dv1_ref, dv2_ref):
        i = pl.program_id(0)

        @pl.when(i == 0)
        def _():
            acc_ref[...] = jnp.zeros_like(acc_ref)

        g_post = v_ref[0:1, :]
        gate = v_ref[1:2, :]
        dxn_v = dxn_ref[...]
        out = _dot(mix_ref[...], wo_ref[...])
        rstd = lax.rsqrt(jnp.mean(out * out, axis=-1, keepdims=True) + NORM_EPS)
        nrm = out * rstd
        acc_ref[0:1, :] += jnp.sum(dxn_v * (nrm * g_post), axis=0, keepdims=True)
        acc_ref[1:2, :] += jnp.sum(dxn_v * gate * nrm, axis=0, keepdims=True)
        dn = dxn_v * (gate * g_post)
        dout = (rstd * (dn - nrm * jnp.mean(dn * nrm, axis=-1, keepdims=True))).astype(BF16)
        dout_ref[...] = dout
        dmix = _dot_nt(dout, wo_ref[...])
        sa = jax.nn.sigmoid(ma_ref[...])
        sb = jax.nn.sigmoid(mb_ref[...])
        dya = (dmix * sa).astype(BF16)
        dyb = (dmix * sb).astype(BF16)
        dya_ref[...] = dya
        dyb_ref[...] = dyb
        dma = dmix * ya_ref[...].astype(F32) * (sa * (1.0 - sa))
        dmb = dmix * yb_ref[...].astype(F32) * (sb * (1.0 - sb))
        d_ag = _dot_nt(dya, wpa_ref[...])
        d_bg = _dot_nt(dyb, wpb_ref[...])
        gl = gl_ref[...]
        hl_v = hl_ref[...]
        dhl_ref[...] = d_bg * _silu(gl)
        dgl = d_bg * hl_v * _dsilu(gl)
        ga = ga_ref[...]
        o = o_ref[...]
        do = d_ag * _silu(ga)
        dga = d_ag * o * _dsilu(ga)
        lane = lax.broadcasted_iota(jnp.int32, (tm, 128), 1)
        dvec = jnp.zeros((tm, 128), F32)
        for h in range(HEADS):
            sl = slice(h * HEAD_DIM, (h + 1) * HEAD_DIM)
            dvec = jnp.where(lane == h, jnp.sum(do[:, sl] * o[:, sl], axis=1, keepdims=True), dvec)
        do_b = do.astype(BF16)
        do0_ref[0] = do_b
        dv0_ref[0] = dvec
        for do_ref, dv_ref, f_ref in ((do1_ref, dv1_ref, f1_ref), (do2_ref, dv2_ref, f2_ref)):
            _store_residues(do_ref, _dot(f_ref[...], do_b).astype(BF16))
            _store_residues(dv_ref, _permute_f32(f_ref[...], dvec, 3))
        dp_ref[0] = dgl[:, 0:ATT_WIDTH].astype(BF16)
        dp_ref[1] = dgl[:, ATT_WIDTH:].astype(BF16)
        dp_ref[2] = dma[:, 0:ATT_WIDTH].astype(BF16)
        dp_ref[3] = dma[:, ATT_WIDTH:].astype(BF16)
        dp_ref[4] = dmb[:, 0:ATT_WIDTH].astype(BF16)
        dp_ref[5] = dmb[:, ATT_WIDTH:].astype(BF16)
        dp_ref[6] = dga.astype(BF16)

    def col(j):
        return pl.BlockSpec((tm, D_MODEL), lambda i: (i, j))

    def whole(shape):
        return pl.BlockSpec(shape, lambda i: (0, 0))

    def res_spec(r, width):
        return pl.BlockSpec((r, tm // r, width), lambda i: (0, i, 0))

    row = col(0)
    att = pl.BlockSpec((tm, ATT_WIDTH), lambda i: (i, 0))
    bf = jax.ShapeDtypeStruct((S, D_MODEL), BF16)
    vmem = pl.BlockSpec(memory_space=pltpu.VMEM)
    return _call(
        body, (dxn, mix, ya, yb, hl, rest, rest, rest, rest, o_att, vec, w_pa, w_pb, w_o, *forth),
        name="out_bwd", grid=(S // tm,),
        in_specs=[row, row, row, row, row, col(1), col(2), col(3),
                  pl.BlockSpec((tm, ATT_WIDTH), lambda i: (i, 8)), att,
                  whole((8, D_MODEL)), whole((ATT_WIDTH, D_MODEL)), whole((D_MODEL, D_MODEL)),
                  whole((D_MODEL, D_MODEL)), vmem, vmem],
        out_specs=(row, row, row, row,
                   pl.BlockSpec((7, tm, ATT_WIDTH), lambda i: (0, i, 0)), whole((8, D_MODEL)),
                   *[res_spec(r, ATT_WIDTH) for r in DILATIONS], *[res_spec(r, 128) for r in DILATIONS]),
        out_shape=(bf, bf, bf, jax.ShapeDtypeStruct((S, D_MODEL), F32),
                   jax.ShapeDtypeStruct((DREST_CHUNKS, S, ATT_WIDTH), BF16),
                   jax.ShapeDtypeStruct((8, D_MODEL), F32),
                   *[jax.ShapeDtypeStruct((r, S // r, ATT_WIDTH), BF16) for r in DILATIONS],
                   *[jax.ShapeDtypeStruct((r, S // r, 128), F32) for r in DILATIONS]),
        hosted=hosted)


def _adamw(gsrcs, w, m, v, n_stack, name, rows=None):
    n_layers, R, C = w.shape
    assert len(gsrcs) == n_layers
    budget = 96 * 1024
    tr = rows
    if tr is None:
        tr = R
        while tr * C > budget and tr % 16 == 0:
            tr //= 2
    assert R % tr == 0 and (tr % 8 == 0 or tr == R)
    c1 = 1.0 - ADAM_B1 ** ADAM_STEP
    c2 = 1.0 - ADAM_B2 ** ADAM_STEP

    def body(*refs):
        g_refs = refs[:n_layers]
        w_ref, m_ref, v_ref, go_ref, d_ref, mo_ref, vo_ref = refs[n_layers:]

        def update(g_ref):
            if n_stack:
                g = g_ref[0].astype(F32)
                for s in range(1, n_stack):
                    g = g + g_ref[s].astype(F32)
            else:
                g = g_ref[...]
            m_new = ADAM_B1 * m_ref[...] + (1.0 - ADAM_B1) * g
            v_new = ADAM_B2 * v_ref[...] + (1.0 - ADAM_B2) * (g * g)
            m_hat = m_new / c1
            v_hat = v_new / c2
            go_ref[...] = g
            d_ref[...] = -ADAM_LR * (m_hat / (jnp.sqrt(v_hat) + ADAM_EPS) + ADAM_WD * w_ref[...])
            mo_ref[...] = m_new
            vo_ref[...] = v_new

        for layer in range(n_layers):
            pl.when(pl.program_id(0) == layer)(functools.partial(update, g_refs[layer]))

    def g_spec(layer):
        def rows_of(l, i):
            return jnp.where(l == layer, i, 0)
        if n_stack:
            return pl.BlockSpec((n_stack, tr, C), lambda l, i: (0, rows_of(l, i), 0))
        return pl.BlockSpec((tr, C), lambda l, i: (rows_of(l, i), 0))

    blk = pl.BlockSpec((None, tr, C), lambda l, i: (l, i, 0))
    shp = jax.ShapeDtypeStruct((n_layers, R, C), F32)
    return pl.pallas_call(
        body, name=name, grid=(n_layers, R // tr),
        in_specs=[g_spec(layer) for layer in range(n_layers)] + [blk, blk, blk],
        out_specs=(blk, blk, blk, blk), out_shape=(shp, shp, shp, shp),
        compiler_params=_params(("arbitrary", "arbitrary")))(*gsrcs, w, m, v)


def _pair_blocks(w):
    w = w.reshape(8, 2, 64, 64)
    z = jnp.zeros((8, 64, 64), w.dtype)
    top = jnp.concatenate([w[:, 0], z], axis=2)
    bot = jnp.concatenate([z, w[:, 1]], axis=2)
    return jnp.concatenate([top, bot], axis=1).astype(BF16)


def _unpair_blocks(g):
    return jnp.stack([g[:, :64, :64], g[:, 64:, 64:]], axis=1).reshape(16, 64, 64)


def _layer_params(layer, w_in_g, w_pa_g, w_pb_g, w_o_g, conv_w_full, conv_b, b_rg, b_ig, lru_lambda, w_rg,
                  w_ig, g_pre, g_post):
    cw = jnp.concatenate([conv_w_full, conv_b[None], b_rg[None], b_ig[None], lru_lambda[None]], axis=0)
    return dict(
        w_int=_weights_full(w_in_g),
        w_pa=jnp.transpose(w_pa_g, (1, 0, 2)).reshape(ATT_WIDTH, D_MODEL),
        w_pb=w_pb_g.reshape(D_MODEL, D_MODEL), w_o=w_o_g.reshape(D_MODEL, D_MODEL),
        cw=cw, wr2=_pair_blocks(w_rg), wi2=_pair_blocks(w_ig),
        g_pre=g_pre, g_post=g_post, lru_lambda=lru_lambda)


def _layer_fwd(x, mod, p, host_proj=None, host_lru=None, host_out=None, target=None):
    zeros = jnp.zeros((5, D_MODEL), F32)
    vec_pre = jnp.concatenate([p["g_pre"][None], mod[1:2], mod[0:1], zeros], axis=0)
    (ht0, ht1, ht2, q0, q1, q2, rest), got_proj = _norm_proj_fwd(x, vec_pre, p["w_int"], hosted=host_proj)
    h_t = [ht0, ht1, ht2]
    qkv, outs, stats = [q0, q1, q2], [], []
    for g, r in enumerate(DILATIONS):
        o_g, st_g = _attn_fwd(qkv[g], g, r)
        outs.append(o_g)
        stats.append(st_g)
    o_att, a_gated, *lse = _attn_combine(outs, stats, rest)
    (hl, a_dec, *gates), got_lru = _lru_fwd(rest, p["cw"], p["wr2"], p["wi2"], hosted=host_lru)
    vec_post = jnp.concatenate([p["g_post"][None], mod[2:3], jnp.zeros((6, D_MODEL), F32)], axis=0)
    (x_new, mix, ya, yb, b_gated, *loss_acc), got_out = _out_fwd(
        a_gated, hl, rest, x, vec_post, p["w_pa"], p["w_pb"], p["w_o"], target=target, hosted=host_out)
    saved = dict(x=x, h_t=h_t, qkv=qkv, rest=rest, o_att=o_att, a_gated=a_gated, lse=lse, hl=hl, a_dec=a_dec,
                 gates=gates, mix=mix, ya=ya, yb=yb, b_gated=b_gated, vec_pre=vec_pre, vec_post=vec_post)
    return (x_new, *loss_acc), saved, got_proj, got_lru, got_out


def _mats_sources(g_pa, g_pb, g_o):
    return [jnp.transpose(g_pa.astype(BF16).reshape(ATT_WIDTH, N_DEV, 128), (1, 0, 2)),
            g_pb.astype(BF16).reshape(N_DEV, 128, D_MODEL), g_o.astype(BF16).reshape(N_DEV, 128, D_MODEL)]


LRU_SMALL = ("conv_b", "w_rg", "b_rg", "w_ig", "b_ig", "lru_lambda")
MOD_SMALL = ("b_mod", "g_pre", "g_post")


def _flat_rows(arrs):
    return jnp.concatenate([a.reshape(-1) for a in arrs]).reshape(-1, 128)


def _layer_bwd(dxn, sv, p, upper_small=None):
    a2a3 = ["a2a"] * 3
    (dout, dya, dyb, dhl, drest, acc_out, *do_dvec), _ = _out_bwd(
        dxn, sv["mix"], sv["ya"], sv["yb"], sv["hl"], sv["rest"], sv["o_att"], sv["vec_post"],
        p["w_pa"], p["w_pb"], p["w_o"])
    mats = _mats_sources(_wgrad_tn(sv["a_gated"], dya, "wgrad_pa"), _wgrad_tn(sv["b_gated"], dyb, "wgrad_pb"),
                         _wgrad_tn(sv["mix"], dout, "wgrad_o"))
    (drest, acc_lru, gwr2, gwi2), _ = _lru_bwd(
        dhl, sv["hl"], sv["a_dec"], sv["gates"], sv["rest"], p["cw"], p["wr2"], p["wi2"], drest)
    small = dict(
        conv_w=acc_lru[0:4], conv_b=acc_lru[4], b_rg=acc_lru[5], b_ig=acc_lru[6],
        lru_lambda=acc_lru[7] * (-jax.nn.sigmoid(-p["lru_lambda"])),
        w_rg=_unpair_blocks(gwr2), w_ig=_unpair_blocks(gwi2))
    hosts = [(mats, a2a3), None, None]
    if upper_small is not None:
        for half, names in enumerate((LRU_SMALL[:3], LRU_SMALL[3:] + ("conv_w",))):
            vec = _flat_rows([jnp.stack([small[n], upper_small[n]]) for n in names])
            hosts[1 + half] = ([vec], ["ag"])
    dqkv, got_attn = [], []
    for g, r in enumerate(DILATIONS):
        d_g, got = _attn_bwd(sv["qkv"][g], do_dvec[g], sv["lse"][g], do_dvec[3 + g], g, r, hosted=hosts[g])
        dqkv.append(d_g)
        got_attn.append(got)
    parts = [_proj_wgrad_group(sv["h_t"][g], dqkv[g], r, f"proj_wgrad_g{g}") for g, r in enumerate(DILATIONS)]
    parts.append(_proj_wgrad_part(sv["h_t"][0], drest, 2, 4, "proj_wgrad_u"))
    parts.append(_proj_wgrad_part(sv["h_t"][0], drest, 7, 0, "proj_wgrad_rest"))
    (dx, acc_pre), got_in = _proj_dgrad_norm_bwd(
        dqkv, drest, p["w_int"], sv["x"], dxn, sv["vec_pre"], hosted=([_grads_to_shards(parts)], ["a2a"]))
    small.update(dmod=jnp.concatenate([acc_pre[0], acc_pre[1], acc_out[0]]), g_pre=acc_pre[2],
                 g_post=acc_out[1])
    return dx, small, dict(mats=got_attn[0], w_in=got_in[0], lru_small=got_attn[1] + got_attn[2])


SMALL_NAMES = ("b_mod", "g_pre", "conv_b", "w_rg", "b_rg", "w_ig", "b_ig", "lru_lambda", "g_post")


def kernel(x, c, w_mod, b_mod, g_pre, w_in, conv_w, conv_b, w_rg, b_rg, w_ig, b_ig, lru_lambda, w_pa, w_pb, w_o, g_post, loss_target, m_w_mod, m_b_mod, m_g_pre, m_w_in, m_conv_w, m_conv_b, m_w_rg, m_b_rg, m_w_ig, m_b_ig, m_lru_lambda, m_w_pa, m_w_pb, m_w_o, m_g_post, v_w_mod, v_b_mod, v_g_pre, v_w_in, v_conv_w, v_conv_b, v_w_rg, v_b_rg, v_w_ig, v_b_ig, v_lru_lambda, v_w_pa, v_w_pb, v_w_o, v_g_post):
    W = dict(w_mod=w_mod, b_mod=b_mod, g_pre=g_pre, w_in=w_in, conv_w=conv_w, conv_b=conv_b, w_rg=w_rg,
             b_rg=b_rg, w_ig=w_ig, b_ig=b_ig, lru_lambda=lru_lambda, w_pa=w_pa, w_pb=w_pb, w_o=w_o,
             g_post=g_post)
    M = dict(w_mod=m_w_mod, b_mod=m_b_mod, g_pre=m_g_pre, w_in=m_w_in, conv_w=m_conv_w, conv_b=m_conv_b,
             w_rg=m_w_rg, b_rg=m_b_rg, w_ig=m_w_ig, b_ig=m_b_ig, lru_lambda=m_lru_lambda, w_pa=m_w_pa,
             w_pb=m_w_pb, w_o=m_w_o, g_post=m_g_post)
    V = dict(w_mod=v_w_mod, b_mod=v_b_mod, g_pre=v_g_pre, w_in=v_w_in, conv_w=v_conv_w, conv_b=v_conv_b,
             w_rg=v_w_rg, b_rg=v_b_rg, w_ig=v_w_ig, b_ig=v_b_ig, lru_lambda=v_lru_lambda, w_pa=v_w_pa,
             w_pb=v_w_pb, w_o=v_w_o, g_post=v_g_post)
    S = x.shape[1]
    me = 4 * lax.axis_index("x") + 2 * lax.axis_index("y") + lax.axis_index("c")
    n_mod = w_mod.shape[2]
    n_in = w_in.shape[2]
    n_conv = conv_w.shape[2]

    c_rows = jnp.broadcast_to(c, (8, D_MODEL))
    w_in_b, w_pa_b, w_pb_b, w_o_b = (t.astype(BF16) for t in (w_in, w_pa, w_pb, w_o))
    g_c, g_win0, g_wpa0, g_wpb0, g_wo0, g_cw = _gather_two_level(
        [c_rows, w_in_b[0], w_pa_b[0], w_pb_b[0], w_o_b[0], conv_w], "gather_weights")
    c_all = g_c[:, 0, :]
    c_pad = jnp.concatenate([c_all, jnp.zeros((8, D_MODEL), F32)], axis=0)
    conv_w_full = jnp.transpose(g_cw, (1, 2, 0, 3)).reshape(2, CONV_WIDTH, D_MODEL)

    mod_cols = _mod_fwd(c_pad, w_mod)
    mod_src = jnp.transpose(mod_cols[:, :8], (1, 0, 2))
    mod_src = jnp.concatenate([mod_src, jnp.zeros((8, 6, n_mod), F32)], axis=1)
    (mod_got,) = _exchange([mod_src], ["a2a"], "scatter_mod")
    mod = jnp.transpose(mod_got[:, :2], (1, 0, 2)).reshape(2, 3 * D_MODEL) + b_mod

    def layer_params(layer, w_in_g, w_pa_g, w_pb_g, w_o_g):
        return _layer_params(layer, w_in_g, w_pa_g, w_pb_g, w_o_g, conv_w_full[layer], conv_b[layer],
                             b_rg[layer], b_ig[layer], lru_lambda[layer], w_rg[layer], w_ig[layer],
                             g_pre[layer], g_post[layer])

    layers = [layer_params(0, g_win0, g_wpa0, g_wpb0, g_wo0), None]
    (act,), sv0, (g_win1,), _, (g_wpa1, g_wpb1, g_wo1) = _layer_fwd(
        x[0], mod[0].reshape(3, D_MODEL), layers[0],
        host_proj=([w_in_b[1]], ["ag"]), host_out=([w_pa_b[1], w_pb_b[1], w_o_b[1]], ["ag"] * 3))
    layers[1] = layer_params(1, g_win1, g_wpa1, g_wpb1, g_wo1)
    (dy, loss_acc), sv1, _, _, _ = _layer_fwd(act, mod[1].reshape(3, D_MODEL), layers[1],
                                              target=loss_target[0])

    dy, small1, got1 = _layer_bwd(dy, sv1, layers[1])
    dy, small0, got0 = _layer_bwd(dy, sv0, layers[0], upper_small=small1)
    grad_x = dy[None]
    r_pa, r_pb, r_o = ([got0["mats"][k], got1["mats"][k]] for k in range(3))
    r_in = [got0["w_in"], got1["w_in"]]
    grads = [small0, small1]

    def stack2(name):
        return jnp.stack([grads[0][name], grads[1][name]], axis=0)

    dmod = stack2("dmod")
    dmod_src = jnp.transpose(dmod.reshape(2, N_DEV, n_mod), (1, 0, 2))
    dmod_src = jnp.concatenate([dmod_src, jnp.zeros((8, 6, n_mod), F32)], axis=1)
    mod_vec = _flat_rows([dmod, stack2("g_pre"), stack2("g_post"), loss_acc])
    r_dmod, r_mod_small = _exchange([dmod_src, mod_vec], ["a2a", "ag"], "exchange_grads")
    mod_rows = mod_vec.shape[0] - loss_acc.shape[0]
    loss = 0.5 * jnp.sum(r_mod_small[:, mod_rows, 0]) / D_MODEL
    r_lru_small = got0["lru_small"]

    res = {}
    res["w_in"] = _adamw(r_in, w_in, m_w_in, v_w_in, 8, "adamw_w_in")
    res["w_pa"] = _adamw(r_pa, w_pa, m_w_pa, v_w_pa, 8, "adamw_w_pa")
    res["w_pb"] = _adamw(r_pb, w_pb, m_w_pb, v_w_pb, 8, "adamw_w_pb")
    res["w_o"] = _adamw(r_o, w_o, m_w_o, v_w_o, 8, "adamw_w_o")
    dmod_all = jnp.transpose(r_dmod[:, :2], (1, 0, 2))
    gw_mod = _mod_wgrad(jnp.transpose(c_all), dmod_all)
    res["w_mod"] = _adamw([gw_mod[0], gw_mod[1]], w_mod, m_w_mod, v_w_mod, 0, "adamw_w_mod")
    for names_rep, stack, tag in ((LRU_SMALL[:3], r_lru_small[0], "lru_a"), (LRU_SMALL[3:], r_lru_small[1], "lru_b"),
                                  (MOD_SMALL, r_mod_small, "mod")):
        rows = sum(W[n].size for n in names_rep) // 128
        rep = _adamw([stack], *(_flat_rows([src[n] for n in names_rep])[None] for src in (W, M, V)), 8,
                     f"adamw_small_{tag}", rows=rows // 2 if rows % 16 == 0 else rows)
        off = 0
        for name in names_rep:
            size = W[name].size
            res[name] = tuple(t.reshape(-1)[off:off + size].reshape(W[name].shape) for t in rep)
            off += size
    lru_rows = sum(W[n].size for n in LRU_SMALL[3:]) // 128
    conv_stack = r_lru_small[1][:, lru_rows:].reshape(8, 2, CONV_WIDTH, D_MODEL)
    conv_stack = lax.dynamic_slice_in_dim(conv_stack, me * n_conv, n_conv, axis=3).reshape(8, 8, n_conv)
    res["conv_w"] = _adamw([conv_stack], conv_w.reshape(1, 8, n_conv), m_conv_w.reshape(1, 8, n_conv),
                           v_conv_w.reshape(1, 8, n_conv), 8, "adamw_conv_w")

    names = ("w_mod", "b_mod", "g_pre", "w_in", "conv_w", "conv_b", "w_rg", "b_rg", "w_ig", "b_ig",
             "lru_lambda", "w_pa", "w_pb", "w_o", "g_post")
    outs = [loss, grad_x]
    for k in range(4):
        outs.extend(res[n][k].reshape(W[n].shape) for n in names)
    return tuple(outs)
```

```python
import functools

import jax
import jax.numpy as jnp
import numpy as np
from jax import lax
from jax.experimental import pallas as pl
from jax.experimental.pallas import tpu as pltpu

F32 = jnp.float32
BF16 = jnp.bfloat16

N_DEV = 8
D_MODEL = 1024
HEAD_DIM = 128
HEADS = 4
ATT_WIDTH = HEADS * HEAD_DIM
DILATIONS = (1, 4, 16)
BAND = 128
N_CHUNKS = 18
QKV_CHUNKS = 9
CONV_WIDTH = 4
LRU_C = 8.0
NORM_EPS = 1e-6
NEG_INF = -1e30
ADAM_LR = 0.001
ADAM_B1 = 0.9
ADAM_B2 = 0.999
ADAM_EPS = 1e-08
ADAM_WD = 0.01
ADAM_STEP = 10
CHUNK_PERM = (0, 3, 6, 1, 4, 7, 2, 5, 8, 10, 11, 12, 13, 14, 15, 16, 17, 9)
DREST_CHUNKS = 10
VMEM_LIMIT = 56 * 1024 * 1024
MESH_AXES = ("x", "y", "c")


def _params(sem=None):
    return pltpu.CompilerParams(dimension_semantics=sem, vmem_limit_bytes=VMEM_LIMIT)


def _silu(x):
    return x * jax.nn.sigmoid(x)


def _dsilu(x):
    s = jax.nn.sigmoid(x)
    return s * (1.0 + x * (1.0 - s))


def _neg_expm1(x):
    series = -x * (1.0 + x * (0.5 + x * (1.0 / 6.0 + x * (1.0 / 24.0 + x * (1.0 / 120.0)))))
    return jnp.where(x > -0.05, series, 1.0 - jnp.exp(x))


def _softplus_neg(lam):
    z = jnp.exp(-jnp.abs(lam))
    small = z * (1.0 - z * (0.5 - z * (1.0 / 3.0 - z * 0.25)))
    log1p_z = jnp.where(z < 1e-2, small, jnp.log(1.0 + z))
    return jnp.maximum(-lam, 0.0) + log1p_z


def _dot(a, b):
    return jnp.dot(a, b, preferred_element_type=F32)


def _dot_nt(a, b):
    return lax.dot_general(a, b, (((1,), (1,)), ((), ())), preferred_element_type=F32)


def _dot_tn(a, b):
    return lax.dot_general(a, b, (((0,), (0,)), ((), ())), preferred_element_type=F32)


def _perm_matrix(r, rows, transpose=False):
    n = rows // r
    p = np.zeros((rows, rows), np.float32)
    dst = np.arange(rows)
    p[dst, (dst % n) * r + dst // n] = 1.0
    return jnp.asarray(p.T if transpose else p, dtype=BF16)


def _permute_f32(p, x, pieces):
    part = x.astype(BF16)
    acc = _dot(p, part)
    for _ in range(pieces - 1):
        x = x - part.astype(F32)
        part = x.astype(BF16)
        acc = acc + _dot(p, part)
    return acc


def _rows_of_residues(ref, lead=()):
    r = ref.shape[len(lead)]
    if r == 1:
        return ref[lead + (0,)]
    return jnp.concatenate([ref[lead + (rho,)] for rho in range(r)], axis=0)


def _store_residues(ref, x, lead=()):
    r = ref.shape[len(lead)]
    n = x.shape[0] // r
    for rho in range(r):
        ref[lead + (rho,)] = x[rho * n:(rho + 1) * n]


ANY_SPEC = pl.BlockSpec(memory_space=pl.ANY)


def _exchange_shapes(arrs, modes):
    return [jax.ShapeDtypeStruct(((N_DEV,) + a.shape) if mode == "ag" else a.shape, a.dtype)
            for a, mode in zip(arrs, modes)]


def _exchange_sems(n):
    return [pltpu.SemaphoreType.DMA((7 * n,)), pltpu.SemaphoreType.DMA((7 * n,)),
            pltpu.SemaphoreType.DMA((n,))]


def _exchange_copies(ins, outs, modes, sems):
    n = len(ins)
    send_sems, recv_sems, local_sems = sems
    x, y, c = lax.axis_index("x"), lax.axis_index("y"), lax.axis_index("c")
    me = 4 * x + 2 * y + c

    def src_for(a, dev):
        return ins[a] if modes[a] == "ag" else ins[a].at[dev]

    local = [pltpu.make_async_copy(src_for(a, me), outs[a].at[me], local_sems.at[a]) for a in range(n)]
    sends, arrivals = [], []
    for k in range(1, N_DEV):
        px = 1 - x if (k >> 2) & 1 else x
        py = 1 - y if (k >> 1) & 1 else y
        pc = 1 - c if k & 1 else c
        peer = 4 * px + 2 * py + pc
        for a in range(n):
            s = (k - 1) * n + a
            for dst, group in ((me, sends), (peer, arrivals)):
                group.append(pltpu.make_async_remote_copy(
                    src_ref=src_for(a, peer), dst_ref=outs[a].at[dst],
                    send_sem=send_sems.at[s], recv_sem=recv_sems.at[s],
                    device_id=(px, py, pc), device_id_type=pl.DeviceIdType.MESH))
    return local, sends, arrivals


def _exchange_start(copies):
    local, sends, _ = copies
    for cp in local + sends:
        cp.start()


def _exchange_wait(copies):
    local, sends, arrivals = copies
    for cp in arrivals:
        cp.wait_recv()
    for cp in sends:
        cp.wait_send()
    for cp in local:
        cp.wait()


def _exchange(arrs, modes, name):
    n = len(arrs)

    def body(*refs):
        copies = _exchange_copies(refs[:n], refs[n:2 * n], modes, refs[2 * n:])
        _exchange_start(copies)
        _exchange_wait(copies)

    outs = pl.pallas_call(
        body, name=name, out_shape=tuple(_exchange_shapes(arrs, modes)),
        in_specs=[ANY_SPEC] * n, out_specs=tuple([ANY_SPEC] * n),
        scratch_shapes=_exchange_sems(n),
    )(*arrs)
    return list(outs)


def _gather_two_level(arrs, name):
    n = len(arrs)

    def body(*refs):
        ins, outs = refs[:n], refs[n:2 * n]
        send_sems, recv_sems, local_sems = refs[2 * n:]
        x, y, c = lax.axis_index("x"), lax.axis_index("y"), lax.axis_index("c")
        me, sibling = (x, y, c), (x, y, 1 - c)
        chips = [(1 - x, y), (x, 1 - y), (1 - x, 1 - y)]

        def copy(a, k, block, to, src=None):
            slot = 4 * block[0] + 2 * block[1] + block[2]
            return pltpu.make_async_remote_copy(
                src_ref=outs[a].at[slot] if src is None else src, dst_ref=outs[a].at[slot],
                send_sem=send_sems.at[7 * a + k], recv_sem=recv_sems.at[7 * a + k],
                device_id=to, device_id_type=pl.DeviceIdType.MESH)

        mine = [pltpu.make_async_copy(ins[a], outs[a].at[4 * x + 2 * y + c], local_sems.at[a])
                for a in range(n)]
        first = []
        for a in range(n):
            first.append(copy(a, 0, me, sibling, src=ins[a]))
            first += [copy(a, 1 + j, me, (*chip, c), src=ins[a]) for j, chip in enumerate(chips)]
        for cp in mine + first:
            cp.start()
        passed = []
        for j, chip in enumerate(chips):
            for a in range(n):
                copy(a, 1 + j, (*chip, c), me).wait_recv()
                passed.append(copy(a, 4 + j, (*chip, c), sibling))
                passed[-1].start()
        for a in range(n):
            copy(a, 0, sibling, me).wait_recv()
            for j, chip in enumerate(chips):
                copy(a, 4 + j, (*chip, 1 - c), me).wait_recv()
        for cp in first + passed:
            cp.wait_send()
        for cp in mine:
            cp.wait()

    outs = pl.pallas_call(
        body, name=name, out_shape=tuple(_exchange_shapes(arrs, ["ag"] * n)),
        in_specs=[ANY_SPEC] * n, out_specs=tuple([ANY_SPEC] * n),
        scratch_shapes=_exchange_sems(n),
    )(*arrs)
    return list(outs)


SHARD_COLS = N_CHUNKS * ATT_WIDTH // N_DEV


def _weights_full(shards):
    return jnp.transpose(shards, (1, 0, 2)).reshape(D_MODEL, N_CHUNKS * ATT_WIDTH)


def _wcols(w_ref, k):
    orig = CHUNK_PERM[k]
    return w_ref[:, orig * ATT_WIDTH:(orig + 1) * ATT_WIDTH]


def _grads_to_shards(parts):
    offsets = np.cumsum([0] + [p.shape[1] for p in parts])
    chunks = []
    for orig in range(N_CHUNKS):
        col = ATT_WIDTH * CHUNK_PERM.index(orig)
        part = int(np.searchsorted(offsets, col, side="right")) - 1
        chunks.append(parts[part][:, col - int(offsets[part]):col - int(offsets[part]) + ATT_WIDTH])
    full = jnp.concatenate(chunks, axis=1)
    return jnp.transpose(full.reshape(D_MODEL, N_DEV, SHARD_COLS), (1, 0, 2))


def _call(body, args, *, name, grid, in_specs, out_specs, out_shape, scratch_shapes=(), aliases=None,
          hosted=None):
    sem = ("arbitrary",) * len(grid)
    if hosted is None:
        outs = pl.pallas_call(
            body, name=name, grid=grid, in_specs=in_specs, out_specs=tuple(out_specs),
            out_shape=tuple(out_shape), scratch_shapes=list(scratch_shapes),
            input_output_aliases=aliases or {}, compiler_params=_params(sem))(*args)
        return list(outs), []
    x_arrs, modes = hosted
    n_in, n_out, n_scr, nx = len(args), len(out_shape), len(scratch_shapes), len(x_arrs)

    def wrapped(*refs):
        ins, x_ins = refs[:n_in], refs[n_in:n_in + nx]
        outs = refs[n_in + nx:n_in + nx + n_out]
        x_outs = refs[n_in + nx + n_out:n_in + 2 * nx + n_out]
        scr = refs[n_in + 2 * nx + n_out:n_in + 2 * nx + n_out + n_scr]
        sems = refs[n_in + 2 * nx + n_out + n_scr:]
        first = pl.program_id(0) == 0
        last = pl.program_id(0) == grid[0] - 1
        for axis in range(1, len(grid)):
            first = jnp.logical_and(first, pl.program_id(axis) == 0)
            last = jnp.logical_and(last, pl.program_id(axis) == grid[axis] - 1)

        @pl.when(first)
        def _():
            _exchange_start(_exchange_copies(x_ins, x_outs, modes, sems))

        body(*ins, *outs, *scr)

        @pl.when(last)
        def _():
            _exchange_wait(_exchange_copies(x_ins, x_outs, modes, sems))

    outs = pl.pallas_call(
        wrapped, name=name, grid=grid, in_specs=list(in_specs) + [ANY_SPEC] * nx,
        out_specs=tuple(out_specs) + tuple([ANY_SPEC] * nx),
        out_shape=tuple(out_shape) + tuple(_exchange_shapes(x_arrs, modes)),
        scratch_shapes=list(scratch_shapes) + _exchange_sems(nx),
        input_output_aliases=aliases or {}, compiler_params=_params(sem))(*args, *x_arrs)
    return list(outs[:n_out]), list(outs[n_out:])


def _mod_fwd(c_pad, w_mod):
    def body(c_ref, w_ref, o_ref):
        sc = _silu(c_ref[...]).astype(BF16)
        for layer in range(2):
            o_ref[layer] = _dot(sc, w_ref[layer].astype(BF16))

    return pl.pallas_call(
        body, name="mod_fwd", out_shape=jax.ShapeDtypeStruct((2, 16, w_mod.shape[2]), F32),
        compiler_params=_params())(c_pad, w_mod)


def _mod_wgrad(c_t, dmod):
    n_cols = dmod.shape[2]

    def body(c_ref, d_ref, o_ref):
        sc = _silu(c_ref[...]).astype(BF16).astype(F32)
        for layer in range(2):
            dm = d_ref[layer].astype(BF16).astype(F32)
            acc = sc[:, 0:1] * dm[0:1, :]
            for b in range(1, N_DEV):
                acc = acc + sc[:, b:b + 1] * dm[b:b + 1, :]
            o_ref[layer] = acc

    return pl.pallas_call(
        body, name="mod_wgrad", out_shape=jax.ShapeDtypeStruct((2, D_MODEL, n_cols), F32),
        compiler_params=_params())(c_t, dmod)


GROUP_COLS = 3 * ATT_WIDTH
PROJ_TM = 256


def _norm_proj_fwd(x, vec, w_int, hosted=None):
    S = x.shape[0]
    tm = PROJ_TM
    perms = [_perm_matrix(r, tm) for r in DILATIONS[1:]]

    def body(x_ref, v_ref, w_ref, p1_ref, p2_ref, ht0_ref, ht1_ref, ht2_ref, q0_ref, q1_ref, q2_ref,
             rest_ref):
        xv = x_ref[...]
        rstd = lax.rsqrt(jnp.mean(xv * xv, axis=-1, keepdims=True) + NORM_EPS)
        hf = ((xv * rstd) * v_ref[0:1, :]) * (1.0 + v_ref[1:2, :]) + v_ref[2:3, :]
        h = hf.astype(BF16)
        for g, (q_ref, ht_ref, p_ref) in enumerate(((q0_ref, ht0_ref, None), (q1_ref, ht1_ref, p1_ref),
                                                    (q2_ref, ht2_ref, p2_ref))):
            rows_f = hf if p_ref is None else _dot(p_ref[...], h)
            ht_ref[...] = rows_f.T.astype(BF16)
            rows = rows_f.astype(BF16)
            n = tm // q_ref.shape[0]
            for t in range(3):
                res = _dot(rows, _wcols(w_ref, 3 * g + t)).astype(BF16)
                for rho in range(q_ref.shape[0]):
                    q_ref[rho, :, t * ATT_WIDTH:(t + 1) * ATT_WIDTH] = res[rho * n:(rho + 1) * n]
        split = (QKV_CHUNKS - 1) * ATT_WIDTH
        rest_ref[:, :split] = _dot(h, w_ref[:, (QKV_CHUNKS + 1) * ATT_WIDTH:])
        rest_ref[:, split:] = _dot(h, w_ref[:, QKV_CHUNKS * ATT_WIDTH:(QKV_CHUNKS + 1) * ATT_WIDTH])

    n_rest = QKV_CHUNKS * ATT_WIDTH
    whole = pl.BlockSpec(memory_space=pltpu.VMEM)
    ht_spec = pl.BlockSpec((D_MODEL, tm), lambda i: (0, i))
    ht_shape = jax.ShapeDtypeStruct((D_MODEL, S), BF16)
    return _call(
        body, (x, vec, w_int, *perms), name="norm_proj_fwd", grid=(S // tm,),
        in_specs=[pl.BlockSpec((tm, D_MODEL), lambda i: (i, 0)),
                  pl.BlockSpec((8, D_MODEL), lambda i: (0, 0)), whole, whole, whole],
        out_specs=(ht_spec, ht_spec, ht_spec,
                   *[pl.BlockSpec((r, tm // r, GROUP_COLS), lambda i: (0, i, 0)) for r in DILATIONS],
                   pl.BlockSpec((tm, n_rest), lambda i: (i, 0))),
        out_shape=(ht_shape, ht_shape, ht_shape,
                   *[jax.ShapeDtypeStruct((r, S // r, GROUP_COLS), BF16) for r in DILATIONS],
                   jax.ShapeDtypeStruct((S, n_rest), F32)),
        hosted=hosted)


def _proj_dgrad_norm_bwd(dqkv, drest, w_int, x, dxn, vec, hosted=None):
    S = x.shape[0]
    tm = PROJ_TM
    perms = [_perm_matrix(r, tm, transpose=True) for r in DILATIONS[1:]]

    def body(d0_ref, d1_ref, d2_ref, dr_ref, w_ref, p1_ref, p2_ref, x_ref, dxn_ref, v_ref, dx_ref, acc_ref):
        i = pl.program_id(0)

        @pl.when(i == 0)
        def _():
            acc_ref[...] = jnp.zeros_like(acc_ref)

        wcols = functools.partial(_wcols, w_ref)
        dhv = None
        for g, (d_ref, p_ref) in enumerate(((d0_ref, None), (d1_ref, p1_ref), (d2_ref, p2_ref))):
            for t in range(3):
                d = _rows_of_residues(d_ref, (t,))
                if p_ref is not None:
                    d = _dot(p_ref[...], d).astype(BF16)
                part = _dot_nt(d, wcols(3 * g + t))
                dhv = part if dhv is None else dhv + part
        for t in range(7):
            dhv = dhv + _dot_nt(dr_ref[t], wcols(11 + t))
        for t in range(2):
            dhv = dhv + _dot_nt(dr_ref[8 + t], wcols(9 + t))

        xv = x_ref[...]
        g = v_ref[0:1, :]
        sc1 = 1.0 + v_ref[1:2, :]
        rstd = lax.rsqrt(jnp.mean(xv * xv, axis=-1, keepdims=True) + NORM_EPS)
        xhat = xv * rstd
        acc_ref[0:1, :] += jnp.sum(dhv, axis=0, keepdims=True)
        acc_ref[1:2, :] += jnp.sum(dhv * (xhat * g), axis=0, keepdims=True)
        acc_ref[2:3, :] += jnp.sum(dhv * xhat * sc1, axis=0, keepdims=True)
        dxhat = dhv * (g * sc1)
        dx = rstd * (dxhat - xhat * jnp.mean(dxhat * xhat, axis=-1, keepdims=True))
        dx_ref[...] = dx + dxn_ref[...]

    row = pl.BlockSpec((tm, D_MODEL), lambda i: (i, 0))
    vec_spec = pl.BlockSpec((8, D_MODEL), lambda i: (0, 0))
    whole = pl.BlockSpec(memory_space=pltpu.VMEM)
    return _call(
        body, (*dqkv, drest, w_int, *perms, x, dxn, vec), name="proj_dgrad_norm_bwd", grid=(S // tm,),
        in_specs=[*[pl.BlockSpec((3, r, tm // r, ATT_WIDTH), lambda i: (0, 0, i, 0)) for r in DILATIONS],
                  pl.BlockSpec((DREST_CHUNKS, tm, ATT_WIDTH), lambda i: (0, i, 0)),
                  whole, whole, whole, row, row, vec_spec],
        out_specs=(row, vec_spec),
        out_shape=(jax.ShapeDtypeStruct((S, D_MODEL), F32), jax.ShapeDtypeStruct((8, D_MODEL), F32)),
        hosted=hosted)


def _proj_wgrad_part(h_t, d, n_chunks, chunk_block, name):
    S = h_t.shape[1]
    ts = 512
    n_steps = S // ts

    def body(h_ref, d_ref, o_ref, acc_ref):
        i = pl.program_id(0)

        @pl.when(i == 0)
        def _():
            acc_ref[...] = jnp.zeros_like(acc_ref)

        ht = h_ref[...]
        for t in range(n_chunks):
            acc_ref[:, t * ATT_WIDTH:(t + 1) * ATT_WIDTH] += _dot(ht, d_ref[t])

        @pl.when(i == n_steps - 1)
        def _():
            o_ref[...] = acc_ref[...].astype(BF16)

    return pl.pallas_call(
        body, name=name, grid=(n_steps,),
        in_specs=[pl.BlockSpec((D_MODEL, ts), lambda i: (0, i)),
                  pl.BlockSpec((n_chunks, ts, ATT_WIDTH), lambda i: (chunk_block, i, 0))],
        out_specs=pl.BlockSpec((D_MODEL, n_chunks * ATT_WIDTH), lambda i: (0, 0)),
        out_shape=jax.ShapeDtypeStruct((D_MODEL, n_chunks * ATT_WIDTH), BF16),
        scratch_shapes=[pltpu.VMEM((D_MODEL, n_chunks * ATT_WIDTH), F32)],
        compiler_params=_params(("arbitrary",)))(h_t, d)


def _proj_wgrad_group(h_t, d, r, name):
    S = h_t.shape[1]
    ts = 512
    n_steps = S // ts
    n = PROJ_TM // r

    def body(h_ref, d_ref, o_ref, acc_ref):
        i = pl.program_id(0)

        @pl.when(i == 0)
        def _():
            acc_ref[...] = jnp.zeros_like(acc_ref)

        ht = h_ref[...]
        for t in range(3):
            rows = jnp.concatenate([d_ref[t, rho, s * n:(s + 1) * n] for s in range(ts // PROJ_TM)
                                    for rho in range(r)], axis=0) if r > 1 else d_ref[t, 0]
            acc_ref[:, t * ATT_WIDTH:(t + 1) * ATT_WIDTH] += _dot(ht, rows)

        @pl.when(i == n_steps - 1)
        def _():
            o_ref[...] = acc_ref[...].astype(BF16)

    return pl.pallas_call(
        body, name=name, grid=(n_steps,),
        in_specs=[pl.BlockSpec((D_MODEL, ts), lambda i: (0, i)),
                  pl.BlockSpec((3, r, ts // r, ATT_WIDTH), lambda i: (0, 0, i, 0))],
        out_specs=pl.BlockSpec((D_MODEL, GROUP_COLS), lambda i: (0, 0)),
        out_shape=jax.ShapeDtypeStruct((D_MODEL, GROUP_COLS), BF16),
        scratch_shapes=[pltpu.VMEM((D_MODEL, GROUP_COLS), F32)],
        compiler_params=_params(("arbitrary",)))(h_t, d)


def _wgrad_tn(a, b, name):
    S, M = a.shape
    N = b.shape[1]
    ts = 1024

    def body(a_ref, b_ref, o_ref):
        i = pl.program_id(0)
        part = _dot_tn(a_ref[...], b_ref[...])

        @pl.when(i == 0)
        def _():
            o_ref[...] = part

        @pl.when(i > 0)
        def _():
            o_ref[...] += part

    return pl.pallas_call(
        body, name=name, grid=(S // ts,),
        in_specs=[pl.BlockSpec((ts, M), lambda i: (i, 0)), pl.BlockSpec((ts, N), lambda i: (i, 0))],
        out_specs=pl.BlockSpec((M, N), lambda i: (0, 0)),
        out_shape=jax.ShapeDtypeStruct((M, N), F32),
        compiler_params=_params(("arbitrary",)))(a, b)


def _attn_fwd(qkv, g, r):
    L = qkv.shape[1]
    nb2 = L // (2 * BAND)
    scale = HEAD_DIM ** -0.5

    def body(q_ref, kp_ref, kc_ref, vp_ref, vc_ref, o_ref, st_ref):
        m_step = pl.program_id(1)
        ii = lax.broadcasted_iota(jnp.int32, (BAND, 2 * BAND), 0)
        kk = lax.broadcasted_iota(jnp.int32, (BAND, 2 * BAND), 1)
        band = jnp.logical_and(kk >= ii, kk <= ii + BAND)
        masks = (jnp.logical_and(band, jnp.logical_or(kk >= BAND, m_step > 0)), band)
        lane = lax.broadcasted_iota(jnp.int32, (BAND, 128), 1)
        stats = [jnp.zeros((BAND, 128), F32), jnp.zeros((BAND, 128), F32)]
        for h in range(HEADS):
            sl = slice(h * HEAD_DIM, (h + 1) * HEAD_DIM)
            kc, vc = kc_ref[:, sl], vc_ref[:, sl]
            keys = (jnp.concatenate([kp_ref[:, sl], kc[:BAND]], axis=0), kc)
            vals = (jnp.concatenate([vp_ref[:, sl], vc[:BAND]], axis=0), vc)
            for b in range(2):
                rows = slice(b * BAND, (b + 1) * BAND)
                s = jnp.where(masks[b], _dot_nt(q_ref[rows, sl], keys[b]) * scale, NEG_INF)
                m = jnp.max(s, axis=1, keepdims=True)
                p = jnp.exp(s - m)
                l = jnp.sum(p, axis=1, keepdims=True)
                o_ref[rows, sl] = _dot(p.astype(BF16), vals[b]) / l
                stats[b] = jnp.where(lane == h, m, stats[b])
                stats[b] = jnp.where(lane == HEADS + h, l, stats[b])
        st_ref[0:BAND, :] = stats[0]
        st_ref[BAND:, :] = stats[1]

    two = (None, 2 * BAND, ATT_WIDTH)
    one = (None, BAND, ATT_WIDTH)

    def prev(m):
        return jnp.maximum(2 * m - 1, 0)

    return pl.pallas_call(
        body, name=f"attn_fwd_g{g}", grid=(r, nb2),
        in_specs=[
            pl.BlockSpec(two, lambda rho, m: (rho, m, 0)),
            pl.BlockSpec(one, lambda rho, m: (rho, prev(m), 1)),
            pl.BlockSpec(two, lambda rho, m: (rho, m, 1)),
            pl.BlockSpec(one, lambda rho, m: (rho, prev(m), 2)),
            pl.BlockSpec(two, lambda rho, m: (rho, m, 2)),
        ],
        out_specs=(pl.BlockSpec(two, lambda rho, m: (rho, m, 0)),
                   pl.BlockSpec((None, 2 * BAND, 128), lambda rho, m: (rho, m, 0))),
        out_shape=(jax.ShapeDtypeStruct((r, L, ATT_WIDTH), F32),
                   jax.ShapeDtypeStruct((r, L, 128), F32)),
        compiler_params=_params(("parallel", "parallel")))(qkv, qkv, qkv, qkv, qkv)


def _attn_combine(outs, stats, rest):
    S = rest.shape[0]
    tm = 256
    gatt_blk = 8
    back = [_perm_matrix(r, tm, transpose=True) for r in DILATIONS[1:]]
    forth = [_perm_matrix(r, tm) for r in DILATIONS[1:]]

    def body(o0_ref, o1_ref, o2_ref, s0_ref, s1_ref, s2_ref, g_ref, b1_ref, b2_ref, f1_ref, f2_ref,
             o_ref, a_ref, l0_ref, l1_ref, l2_ref):
        outs_nat = [o0_ref[0]] + [_permute_f32(b_ref[...], _rows_of_residues(o_g), 2)
                                  for o_g, b_ref in ((o1_ref, b1_ref), (o2_ref, b2_ref))]
        st = [s0_ref[0]] + [_permute_f32(b_ref[...], _rows_of_residues(s_g), 3)
                            for s_g, b_ref in ((s1_ref, b1_ref), (s2_ref, b2_ref))]
        lane = lax.broadcasted_iota(jnp.int32, (tm, 128), 1)
        lse_out = jnp.zeros((tm, 128), F32)
        for h in range(HEADS):
            sl = slice(h * HEAD_DIM, (h + 1) * HEAD_DIM)
            ms = [s[:, h:h + 1] for s in st]
            ls = [s[:, HEADS + h:HEADS + h + 1] for s in st]
            m_all = jnp.maximum(jnp.maximum(ms[0], ms[1]), ms[2])
            ws = [l * jnp.exp(m - m_all) for m, l in zip(ms, ls)]
            den = ws[0] + ws[1] + ws[2]
            o = (ws[0] * outs_nat[0][:, sl] + ws[1] * outs_nat[1][:, sl] + ws[2] * outs_nat[2][:, sl]) / den
            o_ref[:, sl] = o
            a_ref[:, sl] = (o * _silu(g_ref[:, sl])).astype(BF16)
            lse_out = jnp.where(lane == h, m_all + jnp.log(den), lse_out)
        l0_ref[0] = lse_out
        for l_ref, f_ref in ((l1_ref, f1_ref), (l2_ref, f2_ref)):
            _store_residues(l_ref, _permute_f32(f_ref[...], lse_out, 3))

    o_spec = pl.BlockSpec((tm, ATT_WIDTH), lambda i: (i, 0))
    whole = pl.BlockSpec(memory_space=pltpu.VMEM)

    def res_spec(r, width):
        return pl.BlockSpec((r, tm // r, width), lambda i: (0, i, 0))

    return pl.pallas_call(
        body, name="attn_combine", grid=(S // tm,),
        in_specs=[*[res_spec(r, ATT_WIDTH) for r in DILATIONS], *[res_spec(r, 128) for r in DILATIONS],
                  pl.BlockSpec((tm, ATT_WIDTH), lambda i: (i, gatt_blk)), whole, whole, whole, whole],
        out_specs=(o_spec, o_spec, *[res_spec(r, 128) for r in DILATIONS]),
        out_shape=(jax.ShapeDtypeStruct((S, ATT_WIDTH), F32), jax.ShapeDtypeStruct((S, ATT_WIDTH), BF16),
                   *[jax.ShapeDtypeStruct((r, S // r, 128), F32) for r in DILATIONS]),
        compiler_params=_params(("parallel",)))(*outs, *stats, rest, *back, *forth)


def _attn_bwd(qkv, do, lse, dvec, g, r, hosted=None):
    L = qkv.shape[1]
    nb = L // BAND
    nb2 = nb // 2
    scale = HEAD_DIM ** -0.5

    def body(qc_ref, qn_ref, k_ref, v_ref, doc_ref, don_ref, lc_ref, ln_ref, dc_ref, dn_ref,
             out_ref, carry_ref):
        j = pl.program_id(1)

        @pl.when(j == 0)
        def _():
            carry_ref[...] = jnp.zeros_like(carry_ref)

        ii = lax.broadcasted_iota(jnp.int32, (3 * BAND, 2 * BAND), 0)
        kk = lax.broadcasted_iota(jnp.int32, (3 * BAND, 2 * BAND), 1)
        mask = jnp.logical_and(jnp.logical_and(kk <= ii, kk >= ii - BAND),
                               jnp.logical_or(ii < 2 * BAND, j < nb2 - 1))
        lse3 = jnp.concatenate([lc_ref[...], ln_ref[...]], axis=0)
        dvec3 = jnp.concatenate([dc_ref[...], dn_ref[...]], axis=0)
        for h in range(HEADS):
            sl = slice(h * HEAD_DIM, (h + 1) * HEAD_DIM)
            k = k_ref[:, sl]
            v = v_ref[:, sl]
            q = jnp.concatenate([qc_ref[:, sl], qn_ref[:, sl]], axis=0)
            do = jnp.concatenate([doc_ref[:, sl], don_ref[:, sl]], axis=0)
            p = jnp.where(mask, jnp.exp(_dot_nt(q, k) * scale - lse3[:, h:h + 1]), 0.0)
            ds = (p * (_dot_nt(do, v) - dvec3[:, h:h + 1])).astype(BF16)
            dq3 = _dot(ds, k) * scale
            out_ref[0, 0:BAND, sl] = (carry_ref[:, sl] + dq3[:BAND]).astype(BF16)
            out_ref[0, BAND:, sl] = dq3[BAND:2 * BAND].astype(BF16)
            out_ref[1, :, sl] = (_dot_tn(ds, q) * scale).astype(BF16)
            out_ref[2, :, sl] = _dot_tn(p.astype(BF16), do).astype(BF16)
            carry_ref[:, sl] = dq3[2 * BAND:]

    two = (None, 2 * BAND, ATT_WIDTH)
    one = (None, BAND, ATT_WIDTH)
    stwo = (None, 2 * BAND, 128)
    sone = (None, BAND, 128)

    def nxt(j):
        return jnp.minimum(2 * j + 2, nb - 1)

    (out,), got = _call(
        body, (qkv, qkv, qkv, qkv, do, do, lse, lse, dvec, dvec), name=f"attn_bwd_g{g}", grid=(r, nb2),
        in_specs=[
            pl.BlockSpec(two, lambda rho, j: (rho, j, 0)),
            pl.BlockSpec(one, lambda rho, j: (rho, nxt(j), 0)),
            pl.BlockSpec(two, lambda rho, j: (rho, j, 1)),
            pl.BlockSpec(two, lambda rho, j: (rho, j, 2)),
            pl.BlockSpec(two, lambda rho, j: (rho, j, 0)),
            pl.BlockSpec(one, lambda rho, j: (rho, nxt(j), 0)),
            pl.BlockSpec(stwo, lambda rho, j: (rho, j, 0)),
            pl.BlockSpec(sone, lambda rho, j: (rho, nxt(j), 0)),
            pl.BlockSpec(stwo, lambda rho, j: (rho, j, 0)),
            pl.BlockSpec(sone, lambda rho, j: (rho, nxt(j), 0)),
        ],
        out_specs=(pl.BlockSpec((3, None, 2 * BAND, ATT_WIDTH), lambda rho, j: (0, rho, j, 0)),),
        out_shape=(jax.ShapeDtypeStruct((3, r, L, ATT_WIDTH), BF16),),
        scratch_shapes=[pltpu.VMEM((BAND, ATT_WIDTH), F32)],
        hosted=hosted)
    return out, got


LRU_T = 256


def _linear_scan(a, b, carry, reverse):
    T, C = a.shape
    a = a.reshape(T // 8, 8, C)
    b = b.reshape(T // 8, 8, C)
    row8 = lax.broadcasted_iota(jnp.int32, a.shape, 1)
    for s in (1, 2, 4):
        keep = (row8 < 8 - s) if reverse else (row8 >= s)
        shift = 8 - s if reverse else s
        b_sh = jnp.where(keep, pltpu.roll(b, shift, 1), 0.0)
        a_sh = jnp.where(keep, pltpu.roll(a, shift, 1), 1.0)
        b = a * b_sh + b
        a = a * a_sh
    tiles = [None] * (T // 8)
    order = range(T // 8 - 1, -1, -1) if reverse else range(T // 8)
    for k in order:
        y = b[k] + a[k] * carry
        tiles[k] = y
        carry = y[0:1] if reverse else y[7:8]
    return jnp.concatenate(tiles, axis=0), carry


def _gate_matmuls(ucb, w_ref, bias):
    parts = [_dot(ucb[:, j * 128:(j + 1) * 128], w_ref[j]) for j in range(8)]
    return jnp.concatenate(parts, axis=1) + bias


def _rows_before(x, prev8, j):
    T, C = x.shape
    rot = pltpu.roll(x.reshape(T // 8, 8, C), j, 1)
    rot_prev = jnp.concatenate([pltpu.roll(prev8, j, 0)[None], rot[:-1]], axis=0)
    row8 = lax.broadcasted_iota(jnp.int32, rot.shape, 1)
    return jnp.where(row8 >= j, rot, rot_prev).reshape(T, C)


def _rows_after(x, next8, j):
    T, C = x.shape
    rot = pltpu.roll(x.reshape(T // 8, 8, C), 8 - j, 1)
    rot_next = jnp.concatenate([rot[1:], pltpu.roll(next8, 8 - j, 0)[None]], axis=0)
    row8 = lax.broadcasted_iota(jnp.int32, rot.shape, 1)
    return jnp.where(row8 < 8 - j, rot, rot_next).reshape(T, C)


def _conv_fwd(u, u_prev8, cw_ref):
    uc = cw_ref[4:5, :] + cw_ref[0:1, :] * u
    for j in range(1, CONV_WIDTH):
        uc = uc + cw_ref[j:j + 1, :] * _rows_before(u, u_prev8, j)
    return uc


def _lru_fwd(rest, cw, wr2, wi2, hosted=None):
    S = rest.shape[0]
    T = LRU_T

    def body(u_ref, cw_ref, wr_ref, wi_ref, h_ref, a_ref, uc_ref, r_ref, ig_ref, mult_ref,
             ucar_ref, hcar_ref):
        c = pl.program_id(0)

        @pl.when(c == 0)
        def _():
            ucar_ref[...] = jnp.zeros_like(ucar_ref)
            hcar_ref[...] = jnp.zeros_like(hcar_ref)

        u = u_ref[...]
        uc = _conv_fwd(u, ucar_ref[...], cw_ref)
        ucar_ref[...] = u[T - 8:, :]
        ucb = uc.astype(BF16)
        r = jax.nn.sigmoid(_gate_matmuls(ucb, wr_ref, cw_ref[5:6, :]))
        ig = jax.nn.sigmoid(_gate_matmuls(ucb, wi_ref, cw_ref[6:7, :]))
        log_a = -LRU_C * r * _softplus_neg(cw_ref[7:8, :])
        a = jnp.exp(log_a)
        mult = jnp.sqrt(_neg_expm1(2.0 * log_a))
        b = mult * (ig * uc)
        a_ref[...] = a
        uc_ref[...] = uc
        r_ref[...] = r
        ig_ref[...] = ig
        mult_ref[...] = mult
        h, last = _linear_scan(a, b, hcar_ref[0:1, :], reverse=False)
        h_ref[...] = h
        hcar_ref[...] = jnp.broadcast_to(last, (8, D_MODEL))

    row_spec = pl.BlockSpec((T, D_MODEL), lambda c: (c, 0))
    row_shape = jax.ShapeDtypeStruct((S, D_MODEL), F32)
    return _call(
        body, (rest, cw, wr2, wi2), name="lru_fwd", grid=(S // T,),
        in_specs=[row_spec, pl.BlockSpec((8, D_MODEL), lambda c: (0, 0)),
                  pl.BlockSpec((8, 128, 128), lambda c: (0, 0, 0)),
                  pl.BlockSpec((8, 128, 128), lambda c: (0, 0, 0))],
        out_specs=(row_spec,) * 6, out_shape=(row_shape,) * 6,
        scratch_shapes=[pltpu.VMEM((8, D_MODEL), F32), pltpu.VMEM((8, D_MODEL), F32)],
        hosted=hosted)


def _lru_bwd(dhl, hl, a_all, gates, rest, cw, wr2, wi2, dproj, hosted=None):
    S = rest.shape[0]
    T = LRU_T
    nc = S // T

    def body(dh_ref, h_ref, hp_ref, a_ref, uc_ref, r_ref, ig_ref, mult_ref, u_ref, up_ref, cw_ref, wr_ref,
             wi_ref, _alias, du_ref, acc_ref, gwr_ref, gwi_ref, gcar_ref, acar_ref, dcar_ref):
        step = pl.program_id(0)
        c = nc - 1 - step

        @pl.when(step == 0)
        def _():
            acc_ref[...] = jnp.zeros_like(acc_ref)
            gwr_ref[...] = jnp.zeros_like(gwr_ref)
            gwi_ref[...] = jnp.zeros_like(gwi_ref)
            gcar_ref[...] = jnp.zeros_like(gcar_ref)
            acar_ref[...] = jnp.zeros_like(acar_ref)
            dcar_ref[...] = jnp.zeros_like(dcar_ref)

        u = u_ref[...]
        u_prev = jnp.where(c > 0, up_ref[...], 0.0)
        h_prev8 = jnp.where(c > 0, hp_ref[...], 0.0)
        a = a_ref[...]
        h = h_ref[...]
        uc, r, ig, mult = uc_ref[...], r_ref[...], ig_ref[...], mult_ref[...]
        ucb = uc.astype(BF16)
        sp = _softplus_neg(cw_ref[7:8, :])
        a_next = _rows_after(a, acar_ref[...], 1)
        G, first = _linear_scan(a_next, dh_ref[...], gcar_ref[0:1, :], reverse=True)
        gcar_ref[...] = jnp.broadcast_to(first, (8, D_MODEL))
        acar_ref[...] = jnp.broadcast_to(a[0:1, :], (8, D_MODEL))
        d_a = G * _rows_before(h, h_prev8, 1)
        d_mult = G * (ig * uc)
        d_ig = G * (mult * uc)
        duc = G * (mult * ig)
        d_log_a = d_a * a - d_mult * (a * a) / mult
        d_r = d_log_a * (-LRU_C * sp)
        d_sp = jnp.sum(d_log_a * (-LRU_C * r), axis=0, keepdims=True)
        dpre_r = d_r * r * (1.0 - r)
        dpre_i = d_ig * ig * (1.0 - ig)
        dprb = dpre_r.astype(BF16)
        dpib = dpre_i.astype(BF16)
        back = []
        for j in range(8):
            sl = slice(j * 128, (j + 1) * 128)
            back.append(_dot_nt(dprb[:, sl], wr_ref[j]) + _dot_nt(dpib[:, sl], wi_ref[j]))
            gwr_ref[j] += _dot_tn(ucb[:, sl], dprb[:, sl])
            gwi_ref[j] += _dot_tn(ucb[:, sl], dpib[:, sl])
        duc = duc + jnp.concatenate(back, axis=1)
        du = cw_ref[0:1, :] * duc
        duc_next8 = dcar_ref[...]
        acc_ref[0:1, :] += jnp.sum(duc * u, axis=0, keepdims=True)
        for j in range(1, CONV_WIDTH):
            du = du + cw_ref[j:j + 1, :] * _rows_after(duc, duc_next8, j)
            acc_ref[j:j + 1, :] += jnp.sum(duc * _rows_before(u, u_prev, j), axis=0, keepdims=True)
        dcar_ref[...] = duc[0:8, :]
        acc_ref[4:5, :] += jnp.sum(duc, axis=0, keepdims=True)
        acc_ref[5:6, :] += jnp.sum(dpre_r, axis=0, keepdims=True)
        acc_ref[6:7, :] += jnp.sum(dpre_i, axis=0, keepdims=True)
        acc_ref[7:8, :] += d_sp
        du_ref[0] = du[:, 0:ATT_WIDTH].astype(BF16)
        du_ref[1] = du[:, ATT_WIDTH:].astype(BF16)

    def rev(step):
        return nc - 1 - step

    def prev8(step):
        return jnp.maximum(rev(step) * (T // 8) - 1, 0)

    row_spec = pl.BlockSpec((T, D_MODEL), lambda s: (rev(s), 0))
    prev_spec = pl.BlockSpec((8, D_MODEL), lambda s: (prev8(s), 0))
    vec_spec = pl.BlockSpec((8, D_MODEL), lambda s: (0, 0))
    w_spec = pl.BlockSpec((8, 128, 128), lambda s: (0, 0, 0))
    return _call(
        body, (dhl, hl, hl, a_all, *gates, rest, rest, cw, wr2, wi2, dproj), name="lru_bwd", grid=(nc,),
        in_specs=[row_spec, row_spec, prev_spec, row_spec, row_spec, row_spec, row_spec, row_spec,
                  row_spec, prev_spec, vec_spec, w_spec, w_spec, ANY_SPEC],
        out_specs=(pl.BlockSpec((2, T, ATT_WIDTH), lambda s: (4, rev(s), 0)), vec_spec, w_spec, w_spec),
        out_shape=(jax.ShapeDtypeStruct(dproj.shape, BF16), jax.ShapeDtypeStruct((8, D_MODEL), F32),
                   jax.ShapeDtypeStruct((8, 128, 128), F32), jax.ShapeDtypeStruct((8, 128, 128), F32)),
        scratch_shapes=[pltpu.VMEM((8, D_MODEL), F32), pltpu.VMEM((8, D_MODEL), F32),
                        pltpu.VMEM((8, D_MODEL), F32)],
        aliases={13: 0}, hosted=hosted)


OUT_TM = 256


def _out_fwd(a_gated, hl, rest, x, vec, w_pa, w_pb, w_o, target=None, hosted=None):
    S = x.shape[0]
    tm = OUT_TM
    with_loss = target is not None

    def body(*refs):
        (ag_ref, hl_ref, gl_ref, ma_ref, mb_ref, x_ref, v_ref, wpa_ref, wpb_ref, wo_ref), refs = refs[:10], refs[10:]
        if with_loss:
            t_ref, refs = refs[0], refs[1:]
        xn_ref, mix_ref, ya_ref, yb_ref, bg_ref = refs[:5]
        bg = (hl_ref[...] * _silu(gl_ref[...])).astype(BF16)
        ya = _dot(ag_ref[...], wpa_ref[...])
        yb = _dot(bg, wpb_ref[...])
        mix = (jax.nn.sigmoid(ma_ref[...]) * ya + jax.nn.sigmoid(mb_ref[...]) * yb).astype(BF16)
        out = _dot(mix, wo_ref[...])
        rstd = lax.rsqrt(jnp.mean(out * out, axis=-1, keepdims=True) + NORM_EPS)
        x_new = x_ref[...] + v_ref[1:2, :] * ((out * rstd) * v_ref[0:1, :])
        if with_loss:
            acc_ref = refs[5]

            @pl.when(pl.program_id(0) == 0)
            def _():
                acc_ref[...] = jnp.zeros_like(acc_ref)

            err = x_new - t_ref[...]
            xn_ref[...] = err * (1.0 / D_MODEL)
            part = jnp.sum(jnp.sum(err * err, axis=1, keepdims=True), axis=0, keepdims=True)
            acc_ref[...] += jnp.broadcast_to(part, acc_ref.shape)
        else:
            xn_ref[...] = x_new
        mix_ref[...] = mix
        ya_ref[...] = ya.astype(BF16)
        yb_ref[...] = yb.astype(BF16)
        bg_ref[...] = bg

    def col(j):
        return pl.BlockSpec((tm, D_MODEL), lambda i: (i, j))

    def whole(shape):
        return pl.BlockSpec(shape, lambda i: (0, 0))

    row = col(0)
    bf = jax.ShapeDtypeStruct((S, D_MODEL), BF16)
    loss_in = ([target], [row]) if with_loss else ([], [])
    loss_out = ([jax.ShapeDtypeStruct((8, 128), F32)], [whole((8, 128))]) if with_loss else ([], [])
    return _call(
        body, (a_gated, hl, rest, rest, rest, x, vec, w_pa, w_pb, w_o, *loss_in[0]), name="out_fwd",
        grid=(S // tm,),
        in_specs=[pl.BlockSpec((tm, ATT_WIDTH), lambda i: (i, 0)), row, col(1), col(2), col(3), row,
                  whole((8, D_MODEL)), whole((ATT_WIDTH, D_MODEL)), whole((D_MODEL, D_MODEL)),
                  whole((D_MODEL, D_MODEL)), *loss_in[1]],
        out_specs=(row, row, row, row, row, *loss_out[1]),
        out_shape=(jax.ShapeDtypeStruct((S, D_MODEL), F32), bf, bf, bf, bf, *loss_out[0]),
        hosted=hosted)


def _out_bwd(dxn, mix, ya, yb, hl, rest, o_att, vec, w_pa, w_pb, w_o, hosted=None):
    S = dxn.shape[0]
    tm = OUT_TM
    forth = [_perm_matrix(r, tm) for r in DILATIONS[1:]]

    def body(dxn_ref, mix_ref, ya_ref, yb_ref, hl_ref, gl_ref, ma_ref, mb_ref, ga_ref, o_ref, v_ref,
             wpa_ref, wpb_ref, wo_ref, f1_ref, f2_ref,
             dout_ref, dya_ref, dyb_ref, dhl_ref, dp_ref, acc_ref,
             do0_ref, do1_ref, do2_ref, dv0_ref, dv1_ref, dv2_ref):
        i = pl.program_id(0)

        @pl.when(i == 0)
        def _():
            acc_ref[...] = jnp.zeros_like(acc_ref)

        g_post = v_ref[0:1, :]
        gate = v_ref[1:2, :]
        dxn_v = dxn_ref[...]
        out = _dot(mix_ref[...], wo_ref[...])
        rstd = lax.rsqrt(jnp.mean(out * out, axis=-1, keepdims=True) + NORM_EPS)
        nrm = out * rstd
        acc_ref[0:1, :] += jnp.sum(dxn_v * (nrm * g_post), axis=0, keepdims=True)
        acc_ref[1:2, :] += jnp.sum(dxn_v * gate * nrm, axis=0, keepdims=True)
        dn = dxn_v * (gate * g_post)
        dout = (rstd * (dn - nrm * jnp.mean(dn * nrm, axis=-1, keepdims=True))).astype(BF16)
        dout_ref[...] = dout
        dmix = _dot_nt(dout, wo_ref[...])
        sa = jax.nn.sigmoid(ma_ref[...])
        sb = jax.nn.sigmoid(mb_ref[...])
        dya = (dmix * sa).astype(BF16)
        dyb = (dmix * sb).astype(BF16)
        dya_ref[...] = dya
        dyb_ref[...] = dyb
        dma = dmix * ya_ref[...].astype(F32) * (sa * (1.0 - sa))
        dmb = dmix * yb_ref[...].astype(F32) * (sb * (1.0 - sb))
        d_ag = _dot_nt(dya, wpa_ref[...])
        d_bg = _dot_nt(dyb, wpb_ref[...])
        gl = gl_ref[...]
        hl_v = hl_ref[...]
        dhl_ref[...] = d_bg * _silu(gl)
        dgl = d_bg * hl_v * _dsilu(gl)
        ga = ga_ref[...]
        o = o_ref[...]
        do = d_ag * _silu(ga)
        dga = d_ag * o * _dsilu(ga)
        lane = lax.broadcasted_iota(jnp.int32, (tm, 128), 1)
        dvec = jnp.zeros((tm, 128), F32)
        for h in range(HEADS):
            sl = slice(h * HEAD_DIM, (h + 1) * HEAD_DIM)
            dvec = jnp.where(lane == h, jnp.sum(do[:, sl] * o[:, sl], axis=1, keepdims=True), dvec)
        do_b = do.astype(BF16)
        do0_ref[0] = do_b
        dv0_ref[0] = dvec
        for do_ref, dv_ref, f_ref in ((do1_ref, dv1_ref, f1_ref), (do2_ref, dv2_ref, f2_ref)):
            _store_residues(do_ref, _dot(f_ref[...], do_b).astype(BF16))
            _store_residues(dv_ref, _permute_f32(f_ref[...], dvec, 3))
        dp_ref[0] = dgl[:, 0:ATT_WIDTH].astype(BF16)
        dp_ref[1] = dgl[:, ATT_WIDTH:].astype(BF16)
        dp_ref[2] = dma[:, 0:ATT_WIDTH].astype(BF16)
        dp_ref[3] = dma[:, ATT_WIDTH:].astype(BF16)
        dp_ref[4] = dmb[:, 0:ATT_WIDTH].astype(BF16)
        dp_ref[5] = dmb[:, ATT_WIDTH:].astype(BF16)
        dp_ref[6] = dga.astype(BF16)

    def col(j):
        return pl.BlockSpec((tm, D_MODEL), lambda i: (i, j))

    def whole(shape):
        return pl.BlockSpec(shape, lambda i: (0, 0))

    def res_spec(r, width):
        return pl.BlockSpec((r, tm // r, width), lambda i: (0, i, 0))

    row = col(0)
    att = pl.BlockSpec((tm, ATT_WIDTH), lambda i: (i, 0))
    bf = jax.ShapeDtypeStruct((S, D_MODEL), BF16)
    vmem = pl.BlockSpec(memory_space=pltpu.VMEM)
    return _call(
        body, (dxn, mix, ya, yb, hl, rest, rest, rest, rest, o_att, vec, w_pa, w_pb, w_o, *forth),
        name="out_bwd", grid=(S // tm,),
        in_specs=[row, row, row, row, row, col(1), col(2), col(3),
                  pl.BlockSpec((tm, ATT_WIDTH), lambda i: (i, 8)), att,
                  whole((8, D_MODEL)), whole((ATT_WIDTH, D_MODEL)), whole((D_MODEL, D_MODEL)),
                  whole((D_MODEL, D_MODEL)), vmem, vmem],
        out_specs=(row, row, row, row,
                   pl.BlockSpec((7, tm, ATT_WIDTH), lambda i: (0, i, 0)), whole((8, D_MODEL)),
                   *[res_spec(r, ATT_WIDTH) for r in DILATIONS], *[res_spec(r, 128) for r in DILATIONS]),
        out_shape=(bf, bf, bf, jax.ShapeDtypeStruct((S, D_MODEL), F32),
                   jax.ShapeDtypeStruct((DREST_CHUNKS, S, ATT_WIDTH), BF16),
                   jax.ShapeDtypeStruct((8, D_MODEL), F32),
                   *[jax.ShapeDtypeStruct((r, S // r, ATT_WIDTH), BF16) for r in DILATIONS],
                   *[jax.ShapeDtypeStruct((r, S // r, 128), F32) for r in DILATIONS]),
        hosted=hosted)


def _adamw(gsrcs, w, m, v, n_stack, name, rows=None):
    n_layers, R, C = w.shape
    assert len(gsrcs) == n_layers
    budget = 96 * 1024
    tr = rows
    if tr is None:
        tr = R
        while tr * C > budget and tr % 16 == 0:
            tr //= 2
    assert R % tr == 0 and (tr % 8 == 0 or tr == R)
    c1 = 1.0 - ADAM_B1 ** ADAM_STEP
    c2 = 1.0 - ADAM_B2 ** ADAM_STEP

    def body(*refs):
        g_refs = refs[:n_layers]
        w_ref, m_ref, v_ref, go_ref, d_ref, mo_ref, vo_ref = refs[n_layers:]

        def update(g_ref):
            if n_stack:
                g = g_ref[0].astype(F32)
                for s in range(1, n_stack):
                    g = g + g_ref[s].astype(F32)
            else:
                g = g_ref[...]
            m_new = ADAM_B1 * m_ref[...] + (1.0 - ADAM_B1) * g
            v_new = ADAM_B2 * v_ref[...] + (1.0 - ADAM_B2) * (g * g)
            m_hat = m_new / c1
            v_hat = v_new / c2
            go_ref[...] = g
            d_ref[...] = -ADAM_LR * (m_hat / (jnp.sqrt(v_hat) + ADAM_EPS) + ADAM_WD * w_ref[...])
            mo_ref[...] = m_new
            vo_ref[...] = v_new

        for layer in range(n_layers):
            pl.when(pl.program_id(0) == layer)(functools.partial(update, g_refs[layer]))

    def g_spec(layer):
        def rows_of(l, i):
            return jnp.where(l == layer, i, 0)
        if n_stack:
            return pl.BlockSpec((n_stack, tr, C), lambda l, i: (0, rows_of(l, i), 0))
        return pl.BlockSpec((tr, C), lambda l, i: (rows_of(l, i), 0))

    blk = pl.BlockSpec((None, tr, C), lambda l, i: (l, i, 0))
    shp = jax.ShapeDtypeStruct((n_layers, R, C), F32)
    return pl.pallas_call(
        body, name=name, grid=(n_layers, R // tr),
        in_specs=[g_spec(layer) for layer in range(n_layers)] + [blk, blk, blk],
        out_specs=(blk, blk, blk, blk), out_shape=(shp, shp, shp, shp),
        compiler_params=_params(("arbitrary", "arbitrary")))(*gsrcs, w, m, v)


def _pair_blocks(w):
    w = w.reshape(8, 2, 64, 64)
    z = jnp.zeros((8, 64, 64), w.dtype)
    top = jnp.concatenate([w[:, 0], z], axis=2)
    bot = jnp.concatenate([z, w[:, 1]], axis=2)
    return jnp.concatenate([top, bot], axis=1).astype(BF16)


def _unpair_blocks(g):
    return jnp.stack([g[:, :64, :64], g[:, 64:, 64:]], axis=1).reshape(16, 64, 64)


def _layer_params(layer, w_in_g, w_pa_g, w_pb_g, w_o_g, conv_w_full, conv_b, b_rg, b_ig, lru_lambda, w_rg,
                  w_ig, g_pre, g_post):
    cw = jnp.concatenate([conv_w_full, conv_b[None], b_rg[None], b_ig[None], lru_lambda[None]], axis=0)
    return dict(
        w_int=_weights_full(w_in_g),
        w_pa=jnp.transpose(w_pa_g, (1, 0, 2)).reshape(ATT_WIDTH, D_MODEL),
        w_pb=w_pb_g.reshape(D_MODEL, D_MODEL), w_o=w_o_g.reshape(D_MODEL, D_MODEL),
        cw=cw, wr2=_pair_blocks(w_rg), wi2=_pair_blocks(w_ig),
        g_pre=g_pre, g_post=g_post, lru_lambda=lru_lambda)


def _layer_fwd(x, mod, p, host_proj=None, host_lru=None, host_out=None, target=None):
    zeros = jnp.zeros((5, D_MODEL), F32)
    vec_pre = jnp.concatenate([p["g_pre"][None], mod[1:2], mod[0:1], zeros], axis=0)
    (ht0, ht1, ht2, q0, q1, q2, rest), got_proj = _norm_proj_fwd(x, vec_pre, p["w_int"], hosted=host_proj)
    h_t = [ht0, ht1, ht2]
    qkv, outs, stats = [q0, q1, q2], [], []
    for g, r in enumerate(DILATIONS):
        o_g, st_g = _attn_fwd(qkv[g], g, r)
        outs.append(o_g)
        stats.append(st_g)
    o_att, a_gated, *lse = _attn_combine(outs, stats, rest)
    (hl, a_dec, *gates), got_lru = _lru_fwd(rest, p["cw"], p["wr2"], p["wi2"], hosted=host_lru)
    vec_post = jnp.concatenate([p["g_post"][None], mod[2:3], jnp.zeros((6, D_MODEL), F32)], axis=0)
    (x_new, mix, ya, yb, b_gated, *loss_acc), got_out = _out_fwd(
        a_gated, hl, rest, x, vec_post, p["w_pa"], p["w_pb"], p["w_o"], target=target, hosted=host_out)
    saved = dict(x=x, h_t=h_t, qkv=qkv, rest=rest, o_att=o_att, a_gated=a_gated, lse=lse, hl=hl, a_dec=a_dec,
                 gates=gates, mix=mix, ya=ya, yb=yb, b_gated=b_gated, vec_pre=vec_pre, vec_post=vec_post)
    return (x_new, *loss_acc), saved, got_proj, got_lru, got_out


def _mats_sources(g_pa, g_pb, g_o):
    return [jnp.transpose(g_pa.astype(BF16).reshape(ATT_WIDTH, N_DEV, 128), (1, 0, 2)),
            g_pb.astype(BF16).reshape(N_DEV, 128, D_MODEL), g_o.astype(BF16).reshape(N_DEV, 128, D_MODEL)]


LRU_SMALL = ("conv_b", "w_rg", "b_rg", "w_ig", "b_ig", "lru_lambda")
MOD_SMALL = ("b_mod", "g_pre", "g_post")


def _flat_rows(arrs):
    return jnp.concatenate([a.reshape(-1) for a in arrs]).reshape(-1, 128)


def _layer_bwd(dxn, sv, p, upper_small=None):
    a2a3 = ["a2a"] * 3
    (dout, dya, dyb, dhl, drest, acc_out, *do_dvec), _ = _out_bwd(
        dxn, sv["mix"], sv["ya"], sv["yb"], sv["hl"], sv["rest"], sv["o_att"], sv["vec_post"],
        p["w_pa"], p["w_pb"], p["w_o"])
    mats = _mats_sources(_wgrad_tn(sv["a_gated"], dya, "wgrad_pa"), _wgrad_tn(sv["b_gated"], dyb, "wgrad_pb"),
                         _wgrad_tn(sv["mix"], dout, "wgrad_o"))
    (drest, acc_lru, gwr2, gwi2), _ = _lru_bwd(
        dhl, sv["hl"], sv["a_dec"], sv["gates"], sv["rest"], p["cw"], p["wr2"], p["wi2"], drest)
    small = dict(
        conv_w=acc_lru[0:4], conv_b=acc_lru[4], b_rg=acc_lru[5], b_ig=acc_lru[6],
        lru_lambda=acc_lru[7] * (-jax.nn.sigmoid(-p["lru_lambda"])),
        w_rg=_unpair_blocks(gwr2), w_ig=_unpair_blocks(gwi2))
    hosts = [(mats, a2a3), None, None]
    if upper_small is not None:
        for half, names in enumerate((LRU_SMALL[:3], LRU_SMALL[3:] + ("conv_w",))):
            vec = _flat_rows([jnp.stack([small[n], upper_small[n]]) for n in names])
            hosts[1 + half] = ([vec], ["ag"])
    dqkv, got_attn = [], []
    for g, r in enumerate(DILATIONS):
        d_g, got = _attn_bwd(sv["qkv"][g], do_dvec[g], sv["lse"][g], do_dvec[3 + g], g, r, hosted=hosts[g])
        dqkv.append(d_g)
        got_attn.append(got)
    parts = [_proj_wgrad_group(sv["h_t"][g], dqkv[g], r, f"proj_wgrad_g{g}") for g, r in enumerate(DILATIONS)]
    parts.append(_proj_wgrad_part(sv["h_t"][0], drest, 2, 4, "proj_wgrad_u"))
    parts.append(_proj_wgrad_part(sv["h_t"][0], drest, 7, 0, "proj_wgrad_rest"))
    (dx, acc_pre), got_in = _proj_dgrad_norm_bwd(
        dqkv, drest, p["w_int"], sv["x"], dxn, sv["vec_pre"], hosted=([_grads_to_shards(parts)], ["a2a"]))
    small.update(dmod=jnp.concatenate([acc_pre[0], acc_pre[1], acc_out[0]]), g_pre=acc_pre[2],
                 g_post=acc_out[1])
    return dx, small, dict(mats=got_attn[0], w_in=got_in[0], lru_small=got_attn[1] + got_attn[2])


SMALL_NAMES = ("b_mod", "g_pre", "conv_b", "w_rg", "b_rg", "w_ig", "b_ig", "lru_lambda", "g_post")


def kernel(x, c, w_mod, b_mod, g_pre, w_in, conv_w, conv_b, w_rg, b_rg, w_ig, b_ig, lru_lambda, w_pa, w_pb, w_o, g_post, loss_target, m_w_mod, m_b_mod, m_g_pre, m_w_in, m_conv_w, m_conv_b, m_w_rg, m_b_rg, m_w_ig, m_b_ig, m_lru_lambda, m_w_pa, m_w_pb, m_w_o, m_g_post, v_w_mod, v_b_mod, v_g_pre, v_w_in, v_conv_w, v_conv_b, v_w_rg, v_b_rg, v_w_ig, v_b_ig, v_lru_lambda, v_w_pa, v_w_pb, v_w_o, v_g_post):
    W = dict(w_mod=w_mod, b_mod=b_mod, g_pre=g_pre, w_in=w_in, conv_w=conv_w, conv_b=conv_b, w_rg=w_rg,
             b_rg=b_rg, w_ig=w_ig, b_ig=b_ig, lru_lambda=lru_lambda, w_pa=w_pa, w_pb=w_pb, w_o=w_o,
             g_post=g_post)
    M = dict(w_mod=m_w_mod, b_mod=m_b_mod, g_pre=m_g_pre, w_in=m_w_in, conv_w=m_conv_w, conv_b=m_conv_b,
             w_rg=m_w_rg, b_rg=m_b_rg, w_ig=m_w_ig, b_ig=m_b_ig, lru_lambda=m_lru_lambda, w_pa=m_w_pa,
             w_pb=m_w_pb, w_o=m_w_o, g_post=m_g_post)
    V = dict(w_mod=v_w_mod, b_mod=v_b_mod, g_pre=v_g_pre, w_in=v_w_in, conv_w=v_conv_w, conv_b=v_conv_b,
             w_rg=v_w_rg, b_rg=v_b_rg, w_ig=v_w_ig, b_ig=v_b_ig, lru_lambda=v_lru_lambda, w_pa=v_w_pa,
             w_pb=v_w_pb, w_o=v_w_o, g_post=v_g_post)
    S = x.shape[1]
    me = 4 * lax.axis_index("x") + 2 * lax.axis_index("y") + lax.axis_index("c")
    n_mod = w_mod.shape[2]
    n_in = w_in.shape[2]
    n_conv = conv_w.shape[2]

    c_rows = jnp.broadcast_to(c, (8, D_MODEL))
    w_in_b, w_pa_b, w_pb_b, w_o_b = (t.astype(BF16) for t in (w_in, w_pa, w_pb, w_o))
    g_c, g_win0, g_wpa0, g_wpb0, g_wo0, g_cw = _gather_two_level(
        [c_rows, w_in_b[0], w_pa_b[0], w_pb_b[0], w_o_b[0], conv_w], "gather_weights")
    c_all = g_c[:, 0, :]
    c_pad = jnp.concatenate([c_all, jnp.zeros((8, D_MODEL), F32)], axis=0)
    conv_w_full = jnp.transpose(g_cw, (1, 2, 0, 3)).reshape(2, CONV_WIDTH, D_MODEL)

    mod_cols = _mod_fwd(c_pad, w_mod)
    mod_src = jnp.transpose(mod_cols[:, :8], (1, 0, 2))
    mod_src = jnp.concatenate([mod_src, jnp.zeros((8, 6, n_mod), F32)], axis=1)
    (mod_got,) = _exchange([mod_src], ["a2a"], "scatter_mod")
    mod = jnp.transpose(mod_got[:, :2], (1, 0, 2)).reshape(2, 3 * D_MODEL) + b_mod

    def layer_params(layer, w_in_g, w_pa_g, w_pb_g, w_o_g):
        return _layer_params(layer, w_in_g, w_pa_g, w_pb_g, w_o_g, conv_w_full[layer], conv_b[layer],
                             b_rg[layer], b_ig[layer], lru_lambda[layer], w_rg[layer], w_ig[layer],
                             g_pre[layer], g_post[layer])

    layers = [layer_params(0, g_win0, g_wpa0, g_wpb0, g_wo0), None]
    (act,), sv0, (g_win1,), _, (g_wpa1, g_wpb1, g_wo1) = _layer_fwd(
        x[0], mod[0].reshape(3, D_MODEL), layers[0],
        host_proj=([w_in_b[1]], ["ag"]), host_out=([w_pa_b[1], w_pb_b[1], w_o_b[1]], ["ag"] * 3))
    layers[1] = layer_params(1, g_win1, g_wpa1, g_wpb1, g_wo1)
    (dy, loss_acc), sv1, _, _, _ = _layer_fwd(act, mod[1].reshape(3, D_MODEL), layers[1],
                                              target=loss_target[0])

    dy, small1, got1 = _layer_bwd(dy, sv1, layers[1])
    dy, small0, got0 = _layer_bwd(dy, sv0, layers[0], upper_small=small1)
    grad_x = dy[None]
    r_pa, r_pb, r_o = ([got0["mats"][k], got1["mats"][k]] for k in range(3))
    r_in = [got0["w_in"], got1["w_in"]]
    grads = [small0, small1]

    def stack2(name):
        return jnp.stack([grads[0][name], grads[1][name]], axis=0)

    dmod = stack2("dmod")
    dmod_src = jnp.transpose(dmod.reshape(2, N_DEV, n_mod), (1, 0, 2))
    dmod_src = jnp.concatenate([dmod_src, jnp.zeros((8, 6, n_mod), F32)], axis=1)
    mod_vec = _flat_rows([dmod, stack2("g_pre"), stack2("g_post"), loss_acc])
    r_dmod, r_mod_small = _exchange([dmod_src, mod_vec], ["a2a", "ag"], "exchange_grads")
    mod_rows = mod_vec.shape[0] - loss_acc.shape[0]
    loss = 0.5 * jnp.sum(r_mod_small[:, mod_rows, 0]) / D_MODEL
    r_lru_small = got0["lru_small"]

    res = {}
    res["w_in"] = _adamw(r_in, w_in, m_w_in, v_w_in, 8, "adamw_w_in")
    res["w_pa"] = _adamw(r_pa, w_pa, m_w_pa, v_w_pa, 8, "adamw_w_pa")
    res["w_pb"] = _adamw(r_pb, w_pb, m_w_pb, v_w_pb, 8, "adamw_w_pb")
    res["w_o"] = _adamw(r_o, w_o, m_w_o, v_w_o, 8, "adamw_w_o")
    dmod_all = jnp.transpose(r_dmod[:, :2], (1, 0, 2))
    gw_mod = _mod_wgrad(jnp.transpose(c_all), dmod_all)
    res["w_mod"] = _adamw([gw_mod[0], gw_mod[1]], w_mod, m_w_mod, v_w_mod, 0, "adamw_w_mod")
    for names_rep, stack, tag in ((LRU_SMALL[:3], r_lru_small[0], "lru_a"), (LRU_SMALL[3:], r_lru_small[1], "lru_b"),
                                  (MOD_SMALL, r_mod_small, "mod")):
        rows = sum(W[n].size for n in names_rep) // 128
        rep = _adamw([stack], *(_flat_rows([src[n] for n in names_rep])[None] for src in (W, M, V)), 8,
                     f"adamw_small_{tag}", rows=rows // 2 if rows % 16 == 0 else rows)
        off = 0
        for name in names_rep:
            size = W[name].size
            res[name] = tuple(t.reshape(-1)[off:off + size].reshape(W[name].shape) for t in rep)
            off += size
    lru_rows = sum(W[n].size for n in LRU_SMALL[3:]) // 128
    conv_stack = r_lru_small[1][:, lru_rows:].reshape(8, 2, CONV_WIDTH, D_MODEL)
    conv_stack = lax.dynamic_slice_in_dim(conv_stack, me * n_conv, n_conv, axis=3).reshape(8, 8, n_conv)
    res["conv_w"] = _adamw([conv_stack], conv_w.reshape(1, 8, n_conv), m_conv_w.reshape(1, 8, n_conv),
                           v_conv_w.reshape(1, 8, n_conv), 8, "adamw_conv_w")

    names = ("w_mod", "b_mod", "g_pre", "w_in", "conv_w", "conv_b", "w_rg", "b_rg", "w_ig", "b_ig",
             "lru_lambda", "w_pa", "w_pb", "w_o", "g_post")
    outs = [loss, grad_x]
    for k in range(4):
        outs.extend(res[n][k].reshape(W[n].shape) for n in names)
    return tuple(outs)
```

```python
import functools

import jax
import jax.numpy as jnp
import numpy as np
from jax import lax
from jax.experimental import pallas as pl
from jax.experimental.pallas import tpu as pltpu

F32 = jnp.float32
BF16 = jnp.bfloat16

N_DEV = 8
D_MODEL = 1024
HEAD_DIM = 128
HEADS = 4
ATT_WIDTH = HEADS * HEAD_DIM
DILATIONS = (1, 4, 16)
BAND = 128
N_CHUNKS = 18
QKV_CHUNKS = 9
CONV_WIDTH = 4
LRU_C = 8.0
NORM_EPS = 1e-6
NEG_INF = -1e30
ADAM_LR = 0.001
ADAM_B1 = 0.9
ADAM_B2 = 0.999
ADAM_EPS = 1e-08
ADAM_WD = 0.01
ADAM_STEP = 10
CHUNK_PERM = (0, 3, 6, 1, 4, 7, 2, 5, 8, 10, 11, 12, 13, 14, 15, 16, 17, 9)
DREST_CHUNKS = 10
VMEM_LIMIT = 56 * 1024 * 1024
MESH_AXES = ("x", "y", "c")


def _params(sem=None):
    return pltpu.CompilerParams(dimension_semantics=sem, vmem_limit_bytes=VMEM_LIMIT)


def _silu(x):
    return x * jax.nn.sigmoid(x)


def _dsilu(x):
    s = jax.nn.sigmoid(x)
    return s * (1.0 + x * (1.0 - s))


def _neg_expm1(x):
    series = -x * (1.0 + x * (0.5 + x * (1.0 / 6.0 + x * (1.0 / 24.0))))
    return jnp.where(x > -0.05, series, 1.0 - jnp.exp(x))


def _softplus_neg(lam):
    z = jnp.exp(-jnp.abs(lam))
    small = z * (1.0 - z * (0.5 - z * (1.0 / 3.0 - z * 0.25)))
    log1p_z = jnp.where(z < 1e-2, small, jnp.log(1.0 + z))
    return jnp.maximum(-lam, 0.0) + log1p_z


def _dot(a, b):
    return jnp.dot(a, b, preferred_element_type=F32)


def _dot_nt(a, b):
    return lax.dot_general(a, b, (((1,), (1,)), ((), ())), preferred_element_type=F32)


def _dot_tn(a, b):
    return lax.dot_general(a, b, (((0,), (0,)), ((), ())), preferred_element_type=F32)


def _perm_matrix(r, rows, transpose=False):
    n = rows // r
    p = np.zeros((rows, rows), np.float32)
    dst = np.arange(rows)
    p[dst, (dst % n) * r + dst // n] = 1.0
    return jnp.asarray(p.T if transpose else p, dtype=BF16)


def _permute_f32(p, x, pieces):
    part = x.astype(BF16)
    acc = _dot(p, part)
    for _ in range(pieces - 1):
        x = x - part.astype(F32)
        part = x.astype(BF16)
        acc = acc + _dot(p, part)
    return acc


def _rows_of_residues(ref, lead=()):
    r = ref.shape[len(lead)]
    if r == 1:
        return ref[lead + (0,)]
    return jnp.concatenate([ref[lead + (rho,)] for rho in range(r)], axis=0)


def _store_residues(ref, x, lead=()):
    r = ref.shape[len(lead)]
    n = x.shape[0] // r
    for rho in range(r):
        ref[lead + (rho,)] = x[rho * n:(rho + 1) * n]


ANY_SPEC = pl.BlockSpec(memory_space=pl.ANY)


def _exchange_shapes(arrs, modes):
    return [jax.ShapeDtypeStruct(((N_DEV,) + a.shape) if mode == "ag" else a.shape, a.dtype)
            for a, mode in zip(arrs, modes)]


def _exchange_sems(n):
    return [pltpu.SemaphoreType.DMA((7 * n,)), pltpu.SemaphoreType.DMA((7 * n,)),
            pltpu.SemaphoreType.DMA((n,))]


def _exchange_copies(ins, outs, modes, sems):
    n = len(ins)
    send_sems, recv_sems, local_sems = sems
    x, y, c = lax.axis_index("x"), lax.axis_index("y"), lax.axis_index("c")
    me = 4 * x + 2 * y + c

    def src_for(a, dev):
        return ins[a] if modes[a] == "ag" else ins[a].at[dev]

    local = [pltpu.make_async_copy(src_for(a, me), outs[a].at[me], local_sems.at[a]) for a in range(n)]
    sends, arrivals = [], []
    for k in range(1, N_DEV):
        px = 1 - x if (k >> 2) & 1 else x
        py = 1 - y if (k >> 1) & 1 else y
        pc = 1 - c if k & 1 else c
        peer = 4 * px + 2 * py + pc
        for a in range(n):
            s = (k - 1) * n + a
            for dst, group in ((me, sends), (peer, arrivals)):
                group.append(pltpu.make_async_remote_copy(
                    src_ref=src_for(a, peer), dst_ref=outs[a].at[dst],
                    send_sem=send_sems.at[s], recv_sem=recv_sems.at[s],
                    device_id=(px, py, pc), device_id_type=pl.DeviceIdType.MESH))
    return local, sends, arrivals


def _exchange_start(copies):
    local, sends, _ = copies
    for cp in local + sends:
        cp.start()


def _exchange_wait(copies):
    local, sends, arrivals = copies
    for cp in arrivals:
        cp.wait_recv()
    for cp in sends:
        cp.wait_send()
    for cp in local:
        cp.wait()


def _exchange(arrs, modes, name):
    n = len(arrs)

    def body(*refs):
        copies = _exchange_copies(refs[:n], refs[n:2 * n], modes, refs[2 * n:])
        _exchange_start(copies)
        _exchange_wait(copies)

    outs = pl.pallas_call(
        body, name=name, out_shape=tuple(_exchange_shapes(arrs, modes)),
        in_specs=[ANY_SPEC] * n, out_specs=tuple([ANY_SPEC] * n),
        scratch_shapes=_exchange_sems(n),
    )(*arrs)
    return list(outs)


def _gather_two_level(arrs, name):
    n = len(arrs)

    def body(*refs):
        ins, outs = refs[:n], refs[n:2 * n]
        send_sems, recv_sems, local_sems = refs[2 * n:]
        x, y, c = lax.axis_index("x"), lax.axis_index("y"), lax.axis_index("c")
        me, sibling = (x, y, c), (x, y, 1 - c)
        chips = [(1 - x, y), (x, 1 - y), (1 - x, 1 - y)]

        def copy(a, k, block, to, src=None):
            slot = 4 * block[0] + 2 * block[1] + block[2]
            return pltpu.make_async_remote_copy(
                src_ref=outs[a].at[slot] if src is None else src, dst_ref=outs[a].at[slot],
                send_sem=send_sems.at[7 * a + k], recv_sem=recv_sems.at[7 * a + k],
                device_id=to, device_id_type=pl.DeviceIdType.MESH)

        mine = [pltpu.make_async_copy(ins[a], outs[a].at[4 * x + 2 * y + c], local_sems.at[a])
                for a in range(n)]
        first = []
        for a in range(n):
            first.append(copy(a, 0, me, sibling, src=ins[a]))
            first += [copy(a, 1 + j, me, (*chip, c), src=ins[a]) for j, chip in enumerate(chips)]
        for cp in mine + first:
            cp.start()
        passed = []
        for j, chip in enumerate(chips):
            for a in range(n):
                copy(a, 1 + j, (*chip, c), me).wait_recv()
                passed.append(copy(a, 4 + j, (*chip, c), sibling))
                passed[-1].start()
        for a in range(n):
            copy(a, 0, sibling, me).wait_recv()
            for j, chip in enumerate(chips):
                copy(a, 4 + j, (*chip, 1 - c), me).wait_recv()
        for cp in first + passed:
            cp.wait_send()
        for cp in mine:
            cp.wait()

    outs = pl.pallas_call(
        body, name=name, out_shape=tuple(_exchange_shapes(arrs, ["ag"] * n)),
        in_specs=[ANY_SPEC] * n, out_specs=tuple([ANY_SPEC] * n),
        scratch_shapes=_exchange_sems(n),
    )(*arrs)
    return list(outs)


SHARD_COLS = N_CHUNKS * ATT_WIDTH // N_DEV


def _weights_full(shards):
    return jnp.transpose(shards, (1, 0, 2)).reshape(D_MODEL, N_CHUNKS * ATT_WIDTH)


def _wcols(w_ref, k):
    orig = CHUNK_PERM[k]
    return w_ref[:, orig * ATT_WIDTH:(orig + 1) * ATT_WIDTH]


def _grads_to_shards(parts):
    offsets = np.cumsum([0] + [p.shape[1] for p in parts])
    chunks = []
    for orig in range(N_CHUNKS):
        col = ATT_WIDTH * CHUNK_PERM.index(orig)
        part = int(np.searchsorted(offsets, col, side="right")) - 1
        chunks.append(parts[part][:, col - int(offsets[part]):col - int(offsets[part]) + ATT_WIDTH])
    full = jnp.concatenate(chunks, axis=1)
    return jnp.transpose(full.reshape(D_MODEL, N_DEV, SHARD_COLS), (1, 0, 2))


def _call(body, args, *, name, grid, in_specs, out_specs, out_shape, scratch_shapes=(), aliases=None,
          hosted=None):
    sem = ("arbitrary",) * len(grid)
    if hosted is None:
        outs = pl.pallas_call(
            body, name=name, grid=grid, in_specs=in_specs, out_specs=tuple(out_specs),
            out_shape=tuple(out_shape), scratch_shapes=list(scratch_shapes),
            input_output_aliases=aliases or {}, compiler_params=_params(sem))(*args)
        return list(outs), []
    x_arrs, modes = hosted
    n_in, n_out, n_scr, nx = len(args), len(out_shape), len(scratch_shapes), len(x_arrs)

    def wrapped(*refs):
        ins, x_ins = refs[:n_in], refs[n_in:n_in + nx]
        outs = refs[n_in + nx:n_in + nx + n_out]
        x_outs = refs[n_in + nx + n_out:n_in + 2 * nx + n_out]
        scr = refs[n_in + 2 * nx + n_out:n_in + 2 * nx + n_out + n_scr]
        sems = refs[n_in + 2 * nx + n_out + n_scr:]
        first = pl.program_id(0) == 0
        last = pl.program_id(0) == grid[0] - 1
        for axis in range(1, len(grid)):
            first = jnp.logical_and(first, pl.program_id(axis) == 0)
            last = jnp.logical_and(last, pl.program_id(axis) == grid[axis] - 1)

        @pl.when(first)
        def _():
            _exchange_start(_exchange_copies(x_ins, x_outs, modes, sems))

        body(*ins, *outs, *scr)

        @pl.when(last)
        def _():
            _exchange_wait(_exchange_copies(x_ins, x_outs, modes, sems))

    outs = pl.pallas_call(
        wrapped, name=name, grid=grid, in_specs=list(in_specs) + [ANY_SPEC] * nx,
        out_specs=tuple(out_specs) + tuple([ANY_SPEC] * nx),
        out_shape=tuple(out_shape) + tuple(_exchange_shapes(x_arrs, modes)),
        scratch_shapes=list(scratch_shapes) + _exchange_sems(nx),
        input_output_aliases=aliases or {}, compiler_params=_params(sem))(*args, *x_arrs)
    return list(outs[:n_out]), list(outs[n_out:])


def _mod_fwd(c_pad, w_mod):
    def body(c_ref, w_ref, o_ref):
        sc = _silu(c_ref[...]).astype(BF16)
        for layer in range(2):
            o_ref[layer] = _dot(sc, w_ref[layer].astype(BF16))

    return pl.pallas_call(
        body, name="mod_fwd", out_shape=jax.ShapeDtypeStruct((2, 16, w_mod.shape[2]), F32),
        compiler_params=_params())(c_pad, w_mod)


def _mod_wgrad(c_t, dmod):
    n_cols = dmod.shape[2]

    def body(c_ref, d_ref, o_ref):
        sc = _silu(c_ref[...]).astype(BF16).astype(F32)
        for layer in range(2):
            dm = d_ref[layer].astype(BF16).astype(F32)
            acc = sc[:, 0:1] * dm[0:1, :]
            for b in range(1, N_DEV):
                acc = acc + sc[:, b:b + 1] * dm[b:b + 1, :]
            o_ref[layer] = acc

    return pl.pallas_call(
        body, name="mod_wgrad", out_shape=jax.ShapeDtypeStruct((2, D_MODEL, n_cols), F32),
        compiler_params=_params())(c_t, dmod)


GROUP_COLS = 3 * ATT_WIDTH
PROJ_TM = 256


def _norm_proj_fwd(x, vec, w_int, hosted=None):
    S = x.shape[0]
    tm = PROJ_TM
    perms = [_perm_matrix(r, tm) for r in DILATIONS[1:]]

    def body(x_ref, v_ref, w_ref, p1_ref, p2_ref, ht0_ref, ht1_ref, ht2_ref, q0_ref, q1_ref, q2_ref,
             rest_ref):
        xv = x_ref[...]
        rstd = lax.rsqrt(jnp.mean(xv * xv, axis=-1, keepdims=True) + NORM_EPS)
        hf = ((xv * rstd) * v_ref[0:1, :]) * (1.0 + v_ref[1:2, :]) + v_ref[2:3, :]
        h = hf.astype(BF16)
        for g, (q_ref, ht_ref, p_ref) in enumerate(((q0_ref, ht0_ref, None), (q1_ref, ht1_ref, p1_ref),
                                                    (q2_ref, ht2_ref, p2_ref))):
            rows_f = hf if p_ref is None else _dot(p_ref[...], h)
            ht_ref[...] = rows_f.T.astype(BF16)
            rows = rows_f.astype(BF16)
            n = tm // q_ref.shape[0]
            for t in range(3):
                res = _dot(rows, _wcols(w_ref, 3 * g + t)).astype(BF16)
                for rho in range(q_ref.shape[0]):
                    q_ref[rho, :, t * ATT_WIDTH:(t + 1) * ATT_WIDTH] = res[rho * n:(rho + 1) * n]
        split = (QKV_CHUNKS - 1) * ATT_WIDTH
        rest_ref[:, :split] = _dot(h, w_ref[:, (QKV_CHUNKS + 1) * ATT_WIDTH:])
        rest_ref[:, split:] = _dot(h, w_ref[:, QKV_CHUNKS * ATT_WIDTH:(QKV_CHUNKS + 1) * ATT_WIDTH])

    n_rest = QKV_CHUNKS * ATT_WIDTH
    whole = pl.BlockSpec(memory_space=pltpu.VMEM)
    ht_spec = pl.BlockSpec((D_MODEL, tm), lambda i: (0, i))
    ht_shape = jax.ShapeDtypeStruct((D_MODEL, S), BF16)
    return _call(
        body, (x, vec, w_int, *perms), name="norm_proj_fwd", grid=(S // tm,),
        in_specs=[pl.BlockSpec((tm, D_MODEL), lambda i: (i, 0)),
                  pl.BlockSpec((8, D_MODEL), lambda i: (0, 0)), whole, whole, whole],
        out_specs=(ht_spec, ht_spec, ht_spec,
                   *[pl.BlockSpec((r, tm // r, GROUP_COLS), lambda i: (0, i, 0)) for r in DILATIONS],
                   pl.BlockSpec((tm, n_rest), lambda i: (i, 0))),
        out_shape=(ht_shape, ht_shape, ht_shape,
                   *[jax.ShapeDtypeStruct((r, S // r, GROUP_COLS), BF16) for r in DILATIONS],
                   jax.ShapeDtypeStruct((S, n_rest), F32)),
        hosted=hosted)


def _proj_dgrad_norm_bwd(dqkv, drest, w_int, x, dxn, vec, hosted=None):
    S = x.shape[0]
    tm = PROJ_TM
    perms = [_perm_matrix(r, tm, transpose=True) for r in DILATIONS[1:]]

    def body(d0_ref, d1_ref, d2_ref, dr_ref, w_ref, p1_ref, p2_ref, x_ref, dxn_ref, v_ref, dx_ref, acc_ref):
        i = pl.program_id(0)

        @pl.when(i == 0)
        def _():
            acc_ref[...] = jnp.zeros_like(acc_ref)

        wcols = functools.partial(_wcols, w_ref)
        dhv = None
        for g, (d_ref, p_ref) in enumerate(((d0_ref, None), (d1_ref, p1_ref), (d2_ref, p2_ref))):
            for t in range(3):
                d = _rows_of_residues(d_ref, (t,))
                if p_ref is not None:
                    d = _dot(p_ref[...], d).astype(BF16)
                part = _dot_nt(d, wcols(3 * g + t))
                dhv = part if dhv is None else dhv + part
        for t in range(7):
            dhv = dhv + _dot_nt(dr_ref[t], wcols(11 + t))
        for t in range(2):
            dhv = dhv + _dot_nt(dr_ref[8 + t], wcols(9 + t))

        xv = x_ref[...]
        g = v_ref[0:1, :]
        sc1 = 1.0 + v_ref[1:2, :]
        rstd = lax.rsqrt(jnp.mean(xv * xv, axis=-1, keepdims=True) + NORM_EPS)
        xhat = xv * rstd
        acc_ref[0:1, :] += jnp.sum(dhv, axis=0, keepdims=True)
        acc_ref[1:2, :] += jnp.sum(dhv * (xhat * g), axis=0, keepdims=True)
        acc_ref[2:3, :] += jnp.sum(dhv * xhat * sc1, axis=0, keepdims=True)
        dxhat = dhv * (g * sc1)
        dx = rstd * (dxhat - xhat * jnp.mean(dxhat * xhat, axis=-1, keepdims=True))
        dx_ref[...] = dx + dxn_ref[...]

    row = pl.BlockSpec((tm, D_MODEL), lambda i: (i, 0))
    vec_spec = pl.BlockSpec((8, D_MODEL), lambda i: (0, 0))
    whole = pl.BlockSpec(memory_space=pltpu.VMEM)
    return _call(
        body, (*dqkv, drest, w_int, *perms, x, dxn, vec), name="proj_dgrad_norm_bwd", grid=(S // tm,),
        in_specs=[*[pl.BlockSpec((3, r, tm // r, ATT_WIDTH), lambda i: (0, 0, i, 0)) for r in DILATIONS],
                  pl.BlockSpec((DREST_CHUNKS, tm, ATT_WIDTH), lambda i: (0, i, 0)),
                  whole, whole, whole, row, row, vec_spec],
        out_specs=(row, vec_spec),
        out_shape=(jax.ShapeDtypeStruct((S, D_MODEL), F32), jax.ShapeDtypeStruct((8, D_MODEL), F32)),
        hosted=hosted)


def _proj_wgrad_part(h_t, d, n_chunks, chunk_block, name):
    S = h_t.shape[1]
    ts = 512
    n_steps = S // ts

    def body(h_ref, d_ref, o_ref, acc_ref):
        i = pl.program_id(0)

        @pl.when(i == 0)
        def _():
            acc_ref[...] = jnp.zeros_like(acc_ref)

        ht = h_ref[...]
        for t in range(n_chunks):
            acc_ref[:, t * ATT_WIDTH:(t + 1) * ATT_WIDTH] += _dot(ht, d_ref[t])

        @pl.when(i == n_steps - 1)
        def _():
            o_ref[...] = acc_ref[...].astype(BF16)

    return pl.pallas_call(
        body, name=name, grid=(n_steps,),
        in_specs=[pl.BlockSpec((D_MODEL, ts), lambda i: (0, i)),
                  pl.BlockSpec((n_chunks, ts, ATT_WIDTH), lambda i: (chunk_block, i, 0))],
        out_specs=pl.BlockSpec((D_MODEL, n_chunks * ATT_WIDTH), lambda i: (0, 0)),
        out_shape=jax.ShapeDtypeStruct((D_MODEL, n_chunks * ATT_WIDTH), BF16),
        scratch_shapes=[pltpu.VMEM((D_MODEL, n_chunks * ATT_WIDTH), F32)],
        compiler_params=_params(("arbitrary",)))(h_t, d)


def _proj_wgrad_group(h_t, d, r, name):
    S = h_t.shape[1]
    ts = 512
    n_steps = S // ts
    n = PROJ_TM // r

    def body(h_ref, d_ref, o_ref, acc_ref):
        i = pl.program_id(0)

        @pl.when(i == 0)
        def _():
            acc_ref[...] = jnp.zeros_like(acc_ref)

        ht = h_ref[...]
        for t in range(3):
            rows = jnp.concatenate([d_ref[t, rho, s * n:(s + 1) * n] for s in range(ts // PROJ_TM)
                                    for rho in range(r)], axis=0) if r > 1 else d_ref[t, 0]
            acc_ref[:, t * ATT_WIDTH:(t + 1) * ATT_WIDTH] += _dot(ht, rows)

        @pl.when(i == n_steps - 1)
        def _():
            o_ref[...] = acc_ref[...].astype(BF16)

    return pl.pallas_call(
        body, name=name, grid=(n_steps,),
        in_specs=[pl.BlockSpec((D_MODEL, ts), lambda i: (0, i)),
                  pl.BlockSpec((3, r, ts // r, ATT_WIDTH), lambda i: (0, 0, i, 0))],
        out_specs=pl.BlockSpec((D_MODEL, GROUP_COLS), lambda i: (0, 0)),
        out_shape=jax.ShapeDtypeStruct((D_MODEL, GROUP_COLS), BF16),
        scratch_shapes=[pltpu.VMEM((D_MODEL, GROUP_COLS), F32)],
        compiler_params=_params(("arbitrary",)))(h_t, d)


def _wgrad_tn(a, b, name):
    S, M = a.shape
    N = b.shape[1]
    ts = 1024

    def body(a_ref, b_ref, o_ref):
        i = pl.program_id(0)
        part = _dot_tn(a_ref[...], b_ref[...])

        @pl.when(i == 0)
        def _():
            o_ref[...] = part

        @pl.when(i > 0)
        def _():
            o_ref[...] += part

    return pl.pallas_call(
        body, name=name, grid=(S // ts,),
        in_specs=[pl.BlockSpec((ts, M), lambda i: (i, 0)), pl.BlockSpec((ts, N), lambda i: (i, 0))],
        out_specs=pl.BlockSpec((M, N), lambda i: (0, 0)),
        out_shape=jax.ShapeDtypeStruct((M, N), F32),
        compiler_params=_params(("arbitrary",)))(a, b)


def _attn_fwd(qkv, g, r, hosted=None):
    L = qkv.shape[1]
    nb2 = L // (2 * BAND)
    scale = HEAD_DIM ** -0.5

    def body(q_ref, kp_ref, kc_ref, vp_ref, vc_ref, o_ref, st_ref):
        m_step = pl.program_id(1)
        ii = lax.broadcasted_iota(jnp.int32, (BAND, 2 * BAND), 0)
        kk = lax.broadcasted_iota(jnp.int32, (BAND, 2 * BAND), 1)
        band = jnp.logical_and(kk >= ii, kk <= ii + BAND)
        masks = (jnp.logical_and(band, jnp.logical_or(kk >= BAND, m_step > 0)), band)
        lane = lax.broadcasted_iota(jnp.int32, (BAND, 128), 1)
        stats = [jnp.zeros((BAND, 128), F32), jnp.zeros((BAND, 128), F32)]
        for h in range(HEADS):
            sl = slice(h * HEAD_DIM, (h + 1) * HEAD_DIM)
            kc, vc = kc_ref[:, sl], vc_ref[:, sl]
            keys = (jnp.concatenate([kp_ref[:, sl], kc[:BAND]], axis=0), kc)
            vals = (jnp.concatenate([vp_ref[:, sl], vc[:BAND]], axis=0), vc)
            for b in range(2):
                rows = slice(b * BAND, (b + 1) * BAND)
                s = jnp.where(masks[b], _dot_nt(q_ref[rows, sl], keys[b]) * scale, NEG_INF)
                m = jnp.max(s, axis=1, keepdims=True)
                p = jnp.exp(s - m)
                l = jnp.sum(p, axis=1, keepdims=True)
                o_ref[rows, sl] = _dot(p.astype(BF16), vals[b]) / l
                stats[b] = jnp.where(lane == h, m, stats[b])
                stats[b] = jnp.where(lane == HEADS + h, l, stats[b])
        st_ref[0:BAND, :] = stats[0]
        st_ref[BAND:, :] = stats[1]

    two = (None, 2 * BAND, ATT_WIDTH)
    one = (None, BAND, ATT_WIDTH)

    def prev(m):
        return jnp.maximum(2 * m - 1, 0)

    return _call(
        body, (qkv, qkv, qkv, qkv, qkv), name=f"attn_fwd_g{g}", grid=(r, nb2),
        in_specs=[
            pl.BlockSpec(two, lambda rho, m: (rho, m, 0)),
            pl.BlockSpec(one, lambda rho, m: (rho, prev(m), 1)),
            pl.BlockSpec(two, lambda rho, m: (rho, m, 1)),
            pl.BlockSpec(one, lambda rho, m: (rho, prev(m), 2)),
            pl.BlockSpec(two, lambda rho, m: (rho, m, 2)),
        ],
        out_specs=(pl.BlockSpec(two, lambda rho, m: (rho, m, 0)),
                   pl.BlockSpec((None, 2 * BAND, 128), lambda rho, m: (rho, m, 0))),
        out_shape=(jax.ShapeDtypeStruct((r, L, ATT_WIDTH), F32),
                   jax.ShapeDtypeStruct((r, L, 128), F32)),
        hosted=hosted)


def _attn_combine(outs, stats, rest):
    S = rest.shape[0]
    tm = 256
    gatt_blk = 8
    back = [_perm_matrix(r, tm, transpose=True) for r in DILATIONS[1:]]
    forth = [_perm_matrix(r, tm) for r in DILATIONS[1:]]

    def body(o0_ref, o1_ref, o2_ref, s0_ref, s1_ref, s2_ref, g_ref, b1_ref, b2_ref, f1_ref, f2_ref,
             o_ref, a_ref, l0_ref, l1_ref, l2_ref):
        outs_nat = [o0_ref[0]] + [_permute_f32(b_ref[...], _rows_of_residues(o_g), 2)
                                  for o_g, b_ref in ((o1_ref, b1_ref), (o2_ref, b2_ref))]
        st = [s0_ref[0]] + [_permute_f32(b_ref[...], _rows_of_residues(s_g), 3)
                            for s_g, b_ref in ((s1_ref, b1_ref), (s2_ref, b2_ref))]
        lane = lax.broadcasted_iota(jnp.int32, (tm, 128), 1)
        lse_out = jnp.zeros((tm, 128), F32)
        for h in range(HEADS):
            sl = slice(h * HEAD_DIM, (h + 1) * HEAD_DIM)
            ms = [s[:, h:h + 1] for s in st]
            ls = [s[:, HEADS + h:HEADS + h + 1] for s in st]
            m_all = jnp.maximum(jnp.maximum(ms[0], ms[1]), ms[2])
            ws = [l * jnp.exp(m - m_all) for m, l in zip(ms, ls)]
            den = ws[0] + ws[1] + ws[2]
            o = (ws[0] * outs_nat[0][:, sl] + ws[1] * outs_nat[1][:, sl] + ws[2] * outs_nat[2][:, sl]) / den
            o_ref[:, sl] = o
            a_ref[:, sl] = (o * _silu(g_ref[:, sl])).astype(BF16)
            lse_out = jnp.where(lane == h, m_all + jnp.log(den), lse_out)
        l0_ref[0] = lse_out
        for l_ref, f_ref in ((l1_ref, f1_ref), (l2_ref, f2_ref)):
            _store_residues(l_ref, _permute_f32(f_ref[...], lse_out, 3))

    o_spec = pl.BlockSpec((tm, ATT_WIDTH), lambda i: (i, 0))
    whole = pl.BlockSpec(memory_space=pltpu.VMEM)

    def res_spec(r, width):
        return pl.BlockSpec((r, tm // r, width), lambda i: (0, i, 0))

    return pl.pallas_call(
        body, name="attn_combine", grid=(S // tm,),
        in_specs=[*[res_spec(r, ATT_WIDTH) for r in DILATIONS], *[res_spec(r, 128) for r in DILATIONS],
                  pl.BlockSpec((tm, ATT_WIDTH), lambda i: (i, gatt_blk)), whole, whole, whole, whole],
        out_specs=(o_spec, o_spec, *[res_spec(r, 128) for r in DILATIONS]),
        out_shape=(jax.ShapeDtypeStruct((S, ATT_WIDTH), F32), jax.ShapeDtypeStruct((S, ATT_WIDTH), BF16),
                   *[jax.ShapeDtypeStruct((r, S // r, 128), F32) for r in DILATIONS]),
        compiler_params=_params(("parallel",)))(*outs, *stats, rest, *back, *forth)


def _attn_bwd(qkv, do, lse, dvec, g, r, hosted=None):
    L = qkv.shape[1]
    nb = L // BAND
    nb2 = nb // 2
    scale = HEAD_DIM ** -0.5

    def body(qc_ref, qn_ref, k_ref, v_ref, doc_ref, don_ref, lc_ref, ln_ref, dc_ref, dn_ref,
             out_ref, carry_ref):
        j = pl.program_id(1)

        @pl.when(j == 0)
        def _():
            carry_ref[...] = jnp.zeros_like(carry_ref)

        ii = lax.broadcasted_iota(jnp.int32, (3 * BAND, 2 * BAND), 0)
        kk = lax.broadcasted_iota(jnp.int32, (3 * BAND, 2 * BAND), 1)
        mask = jnp.logical_and(jnp.logical_and(kk <= ii, kk >= ii - BAND),
                               jnp.logical_or(ii < 2 * BAND, j < nb2 - 1))
        lse3 = jnp.concatenate([lc_ref[...], ln_ref[...]], axis=0)
        dvec3 = jnp.concatenate([dc_ref[...], dn_ref[...]], axis=0)
        for h in range(HEADS):
            sl = slice(h * HEAD_DIM, (h + 1) * HEAD_DIM)
            k = k_ref[:, sl]
            v = v_ref[:, sl]
            q = jnp.concatenate([qc_ref[:, sl], qn_ref[:, sl]], axis=0)
            do = jnp.concatenate([doc_ref[:, sl], don_ref[:, sl]], axis=0)
            p = jnp.where(mask, jnp.exp(_dot_nt(q, k) * scale - lse3[:, h:h + 1]), 0.0)
            ds = (p * (_dot_nt(do, v) - dvec3[:, h:h + 1])).astype(BF16)
            dq3 = _dot(ds, k) * scale
            out_ref[0, 0:BAND, sl] = (carry_ref[:, sl] + dq3[:BAND]).astype(BF16)
            out_ref[0, BAND:, sl] = dq3[BAND:2 * BAND].astype(BF16)
            out_ref[1, :, sl] = (_dot_tn(ds, q) * scale).astype(BF16)
            out_ref[2, :, sl] = _dot_tn(p.astype(BF16), do).astype(BF16)
            carry_ref[:, sl] = dq3[2 * BAND:]

    two = (None, 2 * BAND, ATT_WIDTH)
    one = (None, BAND, ATT_WIDTH)
    stwo = (None, 2 * BAND, 128)
    sone = (None, BAND, 128)

    def nxt(j):
        return jnp.minimum(2 * j + 2, nb - 1)

    (out,), got = _call(
        body, (qkv, qkv, qkv, qkv, do, do, lse, lse, dvec, dvec), name=f"attn_bwd_g{g}", grid=(r, nb2),
        in_specs=[
            pl.BlockSpec(two, lambda rho, j: (rho, j, 0)),
            pl.BlockSpec(one, lambda rho, j: (rho, nxt(j), 0)),
            pl.BlockSpec(two, lambda rho, j: (rho, j, 1)),
            pl.BlockSpec(two, lambda rho, j: (rho, j, 2)),
            pl.BlockSpec(two, lambda rho, j: (rho, j, 0)),
            pl.BlockSpec(one, lambda rho, j: (rho, nxt(j), 0)),
            pl.BlockSpec(stwo, lambda rho, j: (rho, j, 0)),
            pl.BlockSpec(sone, lambda rho, j: (rho, nxt(j), 0)),
            pl.BlockSpec(stwo, lambda rho, j: (rho, j, 0)),
            pl.BlockSpec(sone, lambda rho, j: (rho, nxt(j), 0)),
        ],
        out_specs=(pl.BlockSpec((3, None, 2 * BAND, ATT_WIDTH), lambda rho, j: (0, rho, j, 0)),),
        out_shape=(jax.ShapeDtypeStruct((3, r, L, ATT_WIDTH), BF16),),
        scratch_shapes=[pltpu.VMEM((BAND, ATT_WIDTH), F32)],
        hosted=hosted)
    return out, got


LRU_T = 256


def _linear_scan(a, b, carry, reverse):
    T, C = a.shape
    a = a.reshape(T // 8, 8, C)
    b = b.reshape(T // 8, 8, C)
    row8 = lax.broadcasted_iota(jnp.int32, a.shape, 1)
    for s in (1, 2, 4):
        keep = (row8 < 8 - s) if reverse else (row8 >= s)
        shift = 8 - s if reverse else s
        b_sh = jnp.where(keep, pltpu.roll(b, shift, 1), 0.0)
        a_sh = jnp.where(keep, pltpu.roll(a, shift, 1), 1.0)
        b = a * b_sh + b
        a = a * a_sh
    tiles = [None] * (T // 8)
    order = range(T // 8 - 1, -1, -1) if reverse else range(T // 8)
    for k in order:
        y = b[k] + a[k] * carry
        tiles[k] = y
        carry = y[0:1] if reverse else y[7:8]
    return jnp.concatenate(tiles, axis=0), carry


def _gate_matmuls(ucb, w_ref, bias):
    parts = [_dot(ucb[:, j * 128:(j + 1) * 128], w_ref[j]) for j in range(8)]
    return jnp.concatenate(parts, axis=1) + bias


def _rows_before(x, prev8, j):
    T, C = x.shape
    rot = pltpu.roll(x.reshape(T // 8, 8, C), j, 1)
    rot_prev = jnp.concatenate([pltpu.roll(prev8, j, 0)[None], rot[:-1]], axis=0)
    row8 = lax.broadcasted_iota(jnp.int32, rot.shape, 1)
    return jnp.where(row8 >= j, rot, rot_prev).reshape(T, C)


def _rows_after(x, next8, j):
    T, C = x.shape
    rot = pltpu.roll(x.reshape(T // 8, 8, C), 8 - j, 1)
    rot_next = jnp.concatenate([rot[1:], pltpu.roll(next8, 8 - j, 0)[None]], axis=0)
    row8 = lax.broadcasted_iota(jnp.int32, rot.shape, 1)
    return jnp.where(row8 < 8 - j, rot, rot_next).reshape(T, C)


def _conv_fwd(u, u_prev8, cw_ref):
    uc = cw_ref[4:5, :] + cw_ref[0:1, :] * u
    for j in range(1, CONV_WIDTH):
        uc = uc + cw_ref[j:j + 1, :] * _rows_before(u, u_prev8, j)
    return uc


def _lru_fwd(rest, cw, wr2, wi2, hosted=None):
    S = rest.shape[0]
    T = LRU_T

    def body(u_ref, cw_ref, wr_ref, wi_ref, h_ref, a_ref, uc_ref, r_ref, ig_ref, mult_ref,
             ucar_ref, hcar_ref):
        c = pl.program_id(0)

        @pl.when(c == 0)
        def _():
            ucar_ref[...] = jnp.zeros_like(ucar_ref)
            hcar_ref[...] = jnp.zeros_like(hcar_ref)

        u = u_ref[...]
        uc = _conv_fwd(u, ucar_ref[...], cw_ref)
        ucar_ref[...] = u[T - 8:, :]
        ucb = uc.astype(BF16)
        r = jax.nn.sigmoid(_gate_matmuls(ucb, wr_ref, cw_ref[5:6, :]))
        ig = jax.nn.sigmoid(_gate_matmuls(ucb, wi_ref, cw_ref[6:7, :]))
        log_a = -LRU_C * r * _softplus_neg(cw_ref[7:8, :])
        a = jnp.exp(log_a)
        mult = jnp.sqrt(_neg_expm1(2.0 * log_a))
        b = mult * (ig * uc)
        a_ref[...] = a
        uc_ref[...] = uc
        r_ref[...] = r
        ig_ref[...] = ig
        mult_ref[...] = mult
        h, last = _linear_scan(a, b, hcar_ref[0:1, :], reverse=False)
        h_ref[...] = h
        hcar_ref[...] = jnp.broadcast_to(last, (8, D_MODEL))

    row_spec = pl.BlockSpec((T, D_MODEL), lambda c: (c, 0))
    row_shape = jax.ShapeDtypeStruct((S, D_MODEL), F32)
    return _call(
        body, (rest, cw, wr2, wi2), name="lru_fwd", grid=(S // T,),
        in_specs=[row_spec, pl.BlockSpec((8, D_MODEL), lambda c: (0, 0)),
                  pl.BlockSpec((8, 128, 128), lambda c: (0, 0, 0)),
                  pl.BlockSpec((8, 128, 128), lambda c: (0, 0, 0))],
        out_specs=(row_spec,) * 6, out_shape=(row_shape,) * 6,
        scratch_shapes=[pltpu.VMEM((8, D_MODEL), F32), pltpu.VMEM((8, D_MODEL), F32)],
        hosted=hosted)


def _lru_bwd(dhl, hl, a_all, gates, rest, cw, wr2, wi2, dproj, hosted=None):
    S = rest.shape[0]
    T = LRU_T
    nc = S // T

    def body(dh_ref, h_ref, hp_ref, a_ref, uc_ref, r_ref, ig_ref, mult_ref, u_ref, up_ref, cw_ref, wr_ref,
             wi_ref, _alias, du_ref, acc_ref, gwr_ref, gwi_ref, gcar_ref, acar_ref, dcar_ref):
        step = pl.program_id(0)
        c = nc - 1 - step

        @pl.when(step == 0)
        def _():
            acc_ref[...] = jnp.zeros_like(acc_ref)
            gwr_ref[...] = jnp.zeros_like(gwr_ref)
            gwi_ref[...] = jnp.zeros_like(gwi_ref)
            gcar_ref[...] = jnp.zeros_like(gcar_ref)
            acar_ref[...] = jnp.zeros_like(acar_ref)
            dcar_ref[...] = jnp.zeros_like(dcar_ref)

        u = u_ref[...]
        u_prev = jnp.where(c > 0, up_ref[...], 0.0)
        h_prev8 = jnp.where(c > 0, hp_ref[...], 0.0)
        a = a_ref[...]
        h = h_ref[...]
        uc, r, ig, mult = uc_ref[...], r_ref[...], ig_ref[...], mult_ref[...]
        ucb = uc.astype(BF16)
        sp = _softplus_neg(cw_ref[7:8, :])
        a_next = _rows_after(a, acar_ref[...], 1)
        G, first = _linear_scan(a_next, dh_ref[...], gcar_ref[0:1, :], reverse=True)
        gcar_ref[...] = jnp.broadcast_to(first, (8, D_MODEL))
        acar_ref[...] = jnp.broadcast_to(a[0:1, :], (8, D_MODEL))
        d_a = G * _rows_before(h, h_prev8, 1)
        d_mult = G * (ig * uc)
        d_ig = G * (mult * uc)
        duc = G * (mult * ig)
        d_log_a = d_a * a - d_mult * (a * a) / mult
        d_r = d_log_a * (-LRU_C * sp)
        d_sp = jnp.sum(d_log_a * (-LRU_C * r), axis=0, keepdims=True)
        dpre_r = d_r * r * (1.0 - r)
        dpre_i = d_ig * ig * (1.0 - ig)
        dprb = dpre_r.astype(BF16)
        dpib = dpre_i.astype(BF16)
        back = []
        for j in range(8):
            sl = slice(j * 128, (j + 1) * 128)
            back.append(_dot_nt(dprb[:, sl], wr_ref[j]) + _dot_nt(dpib[:, sl], wi_ref[j]))
            gwr_ref[j] += _dot_tn(ucb[:, sl], dprb[:, sl])
            gwi_ref[j] += _dot_tn(ucb[:, sl], dpib[:, sl])
        duc = duc + jnp.concatenate(back, axis=1)
        du = cw_ref[0:1, :] * duc
        duc_next8 = dcar_ref[...]
        acc_ref[0:1, :] += jnp.sum(duc * u, axis=0, keepdims=True)
        for j in range(1, CONV_WIDTH):
            du = du + cw_ref[j:j + 1, :] * _rows_after(duc, duc_next8, j)
            acc_ref[j:j + 1, :] += jnp.sum(duc * _rows_before(u, u_prev, j), axis=0, keepdims=True)
        dcar_ref[...] = duc[0:8, :]
        acc_ref[4:5, :] += jnp.sum(duc, axis=0, keepdims=True)
        acc_ref[5:6, :] += jnp.sum(dpre_r, axis=0, keepdims=True)
        acc_ref[6:7, :] += jnp.sum(dpre_i, axis=0, keepdims=True)
        acc_ref[7:8, :] += d_sp
        du_ref[0] = du[:, 0:ATT_WIDTH].astype(BF16)
        du_ref[1] = du[:, ATT_WIDTH:].astype(BF16)

    def rev(step):
        return nc - 1 - step

    def prev8(step):
        return jnp.maximum(rev(step) * (T // 8) - 1, 0)

    row_spec = pl.BlockSpec((T, D_MODEL), lambda s: (rev(s), 0))
    prev_spec = pl.BlockSpec((8, D_MODEL), lambda s: (prev8(s), 0))
    vec_spec = pl.BlockSpec((8, D_MODEL), lambda s: (0, 0))
    w_spec = pl.BlockSpec((8, 128, 128), lambda s: (0, 0, 0))
    return _call(
        body, (dhl, hl, hl, a_all, *gates, rest, rest, cw, wr2, wi2, dproj), name="lru_bwd", grid=(nc,),
        in_specs=[row_spec, row_spec, prev_spec, row_spec, row_spec, row_spec, row_spec, row_spec,
                  row_spec, prev_spec, vec_spec, w_spec, w_spec, ANY_SPEC],
        out_specs=(pl.BlockSpec((2, T, ATT_WIDTH), lambda s: (4, rev(s), 0)), vec_spec, w_spec, w_spec),
        out_shape=(jax.ShapeDtypeStruct(dproj.shape, BF16), jax.ShapeDtypeStruct((8, D_MODEL), F32),
                   jax.ShapeDtypeStruct((8, 128, 128), F32), jax.ShapeDtypeStruct((8, 128, 128), F32)),
        scratch_shapes=[pltpu.VMEM((8, D_MODEL), F32), pltpu.VMEM((8, D_MODEL), F32),
                        pltpu.VMEM((8, D_MODEL), F32)],
        aliases={13: 0}, hosted=hosted)


OUT_TM = 256


def _out_fwd(a_gated, hl, rest, x, vec, w_pa, w_pb, w_o, target=None, hosted=None):
    S = x.shape[0]
    tm = OUT_TM
    with_loss = target is not None

    def body(*refs):
        (ag_ref, hl_ref, gl_ref, ma_ref, mb_ref, x_ref, v_ref, wpa_ref, wpb_ref, wo_ref), refs = refs[:10], refs[10:]
        if with_loss:
            t_ref, refs = refs[0], refs[1:]
        xn_ref, mix_ref, ya_ref, yb_ref, bg_ref = refs[:5]
        bg = (hl_ref[...] * _silu(gl_ref[...])).astype(BF16)
        ya = _dot(ag_ref[...], wpa_ref[...])
        yb = _dot(bg, wpb_ref[...])
        mix = (jax.nn.sigmoid(ma_ref[...]) * ya + jax.nn.sigmoid(mb_ref[...]) * yb).astype(BF16)
        out = _dot(mix, wo_ref[...])
        rstd = lax.rsqrt(jnp.mean(out * out, axis=-1, keepdims=True) + NORM_EPS)
        x_new = x_ref[...] + v_ref[1:2, :] * ((out * rstd) * v_ref[0:1, :])
        if with_loss:
            acc_ref = refs[5]

            @pl.when(pl.program_id(0) == 0)
            def _():
                acc_ref[...] = jnp.zeros_like(acc_ref)

            err = x_new - t_ref[...]
            xn_ref[...] = err * (1.0 / D_MODEL)
            part = jnp.sum(jnp.sum(err * err, axis=1, keepdims=True), axis=0, keepdims=True)
            acc_ref[...] += jnp.broadcast_to(part, acc_ref.shape)
        else:
            xn_ref[...] = x_new
        mix_ref[...] = mix
        ya_ref[...] = ya.astype(BF16)
        yb_ref[...] = yb.astype(BF16)
        bg_ref[...] = bg

    def col(j):
        return pl.BlockSpec((tm, D_MODEL), lambda i: (i, j))

    def whole(shape):
        return pl.BlockSpec(shape, lambda i: (0, 0))

    row = col(0)
    bf = jax.ShapeDtypeStruct((S, D_MODEL), BF16)
    loss_in = ([target], [row]) if with_loss else ([], [])
    loss_out = ([jax.ShapeDtypeStruct((8, 128), F32)], [whole((8, 128))]) if with_loss else ([], [])
    return _call(
        body, (a_gated, hl, rest, rest, rest, x, vec, w_pa, w_pb, w_o, *loss_in[0]), name="out_fwd",
        grid=(S // tm,),
        in_specs=[pl.BlockSpec((tm, ATT_WIDTH), lambda i: (i, 0)), row, col(1), col(2), col(3), row,
                  whole((8, D_MODEL)), whole((ATT_WIDTH, D_MODEL)), whole((D_MODEL, D_MODEL)),
                  whole((D_MODEL, D_MODEL)), *loss_in[1]],
        out_specs=(row, row, row, row, row, *loss_out[1]),
        out_shape=(jax.ShapeDtypeStruct((S, D_MODEL), F32), bf, bf, bf, bf, *loss_out[0]),
        hosted=hosted)


def _out_bwd(dxn, mix, ya, yb, hl, rest, o_att, vec, w_pa, w_pb, w_o, hosted=None):
    S = dxn.shape[0]
    tm = OUT_TM
    forth = [_perm_matrix(r, tm) for r in DILATIONS[1:]]

    def body(dxn_ref, mix_ref, ya_ref, yb_ref, hl_ref, gl_ref, ma_ref, mb_ref, ga_ref, o_ref, v_ref,
             wpa_ref, wpb_ref, wo_ref, f1_ref, f2_ref,
             dout_ref, dya_ref, dyb_ref, dhl_ref, dp_ref, acc_ref,
             do0_ref, do1_ref, do2_ref, dv0_ref, dv1_ref, dv2_ref):
        i = pl.program_id(0)

        @pl.when(i == 0)
        def _():
            acc_ref[...] = jnp.zeros_like(acc_ref)

        g_post = v_ref[0:1, :]
        gate = v_ref[1:2, :]
        dxn_v = dxn_ref[...]
        out = _dot(mix_ref[...], wo_ref[...])
        rstd = lax.rsqrt(jnp.mean(out * out, axis=-1, keepdims=True) + NORM_EPS)
        nrm = out * rstd
        acc_ref[0:1, :] += jnp.sum(dxn_v * (nrm * g_post), axis=0, keepdims=True)
        acc_ref[1:2, :] += jnp.sum(dxn_v * gate * nrm, axis=0, keepdims=True)
        dn = dxn_v * (gate * g_post)
        dout = (rstd * (dn - nrm * jnp.mean(dn * nrm, axis=-1, keepdims=True))).astype(BF16)
        dout_ref[...] = dout
        dmix = _dot_nt(dout, wo_ref[...])
        sa = jax.nn.sigmoid(ma_ref[...])
        sb = jax.nn.sigmoid(mb_ref[...])
        dya = (dmix * sa).astype(BF16)
        dyb = (dmix * sb).astype(BF16)
        dya_ref[...] = dya
        dyb_ref[...] = dyb
        dma = dmix * ya_ref[...].astype(F32) * (sa * (1.0 - sa))
        dmb = dmix * yb_ref[...].astype(F32) * (sb * (1.0 - sb))
        d_ag = _dot_nt(dya, wpa_ref[...])
        d_bg = _dot_nt(dyb, wpb_ref[...])
        gl = gl_ref[...]
        hl_v = hl_ref[...]
        dhl_ref[...] = d_bg * _silu(gl)
        dgl = d_bg * hl_v * _dsilu(gl)
        ga = ga_ref[...]
        o = o_ref[...]
        do = d_ag * _silu(ga)
        dga = d_ag * o * _dsilu(ga)
        lane = lax.broadcasted_iota(jnp.int32, (tm, 128), 1)
        dvec = jnp.zeros((tm, 128), F32)
        for h in range(HEADS):
            sl = slice(h * HEAD_DIM, (h + 1) * HEAD_DIM)
            dvec = jnp.where(lane == h, jnp.sum(do[:, sl] * o[:, sl], axis=1, keepdims=True), dvec)
        do_b = do.astype(BF16)
        do0_ref[0] = do_b
        dv0_ref[0] = dvec
        for do_ref, dv_ref, f_ref in ((do1_ref, dv1_ref, f1_ref), (do2_ref, dv2_ref, f2_ref)):
            _store_residues(do_ref, _dot(f_ref[...], do_b).astype(BF16))
            _store_residues(dv_ref, _permute_f32(f_ref[...], dvec, 3))
        dp_ref[0] = dgl[:, 0:ATT_WIDTH].astype(BF16)
        dp_ref[1] = dgl[:, ATT_WIDTH:].astype(BF16)
        dp_ref[2] = dma[:, 0:ATT_WIDTH].astype(BF16)
        dp_ref[3] = dma[:, ATT_WIDTH:].astype(BF16)
        dp_ref[4] = dmb[:, 0:ATT_WIDTH].astype(BF16)
        dp_ref[5] = dmb[:, ATT_WIDTH:].astype(BF16)
        dp_ref[6] = dga.astype(BF16)

    def col(j):
        return pl.BlockSpec((tm, D_MODEL), lambda i: (i, j))

    def whole(shape):
        return pl.BlockSpec(shape, lambda i: (0, 0))

    def res_spec(r, width):
        return pl.BlockSpec((r, tm // r, width), lambda i: (0, i, 0))

    row = col(0)
    att = pl.BlockSpec((tm, ATT_WIDTH), lambda i: (i, 0))
    bf = jax.ShapeDtypeStruct((S, D_MODEL), BF16)
    vmem = pl.BlockSpec(memory_space=pltpu.VMEM)
    return _call(
        body, (dxn, mix, ya, yb, hl, rest, rest, rest, rest, o_att, vec, w_pa, w_pb, w_o, *forth),
        name="out_bwd", grid=(S // tm,),
        in_specs=[row, row, row, row, row, col(1), col(2), col(3),
                  pl.BlockSpec((tm, ATT_WIDTH), lambda i: (i, 8)), att,
                  whole((8, D_MODEL)), whole((ATT_WIDTH, D_MODEL)), whole((D_MODEL, D_MODEL)),
                  whole((D_MODEL, D_MODEL)), vmem, vmem],
        out_specs=(row, row, row, row,
                   pl.BlockSpec((7, tm, ATT_WIDTH), lambda i: (0, i, 0)), whole((8, D_MODEL)),
                   *[res_spec(r, ATT_WIDTH) for r in DILATIONS], *[res_spec(r, 128) for r in DILATIONS]),
        out_shape=(bf, bf, bf, jax.ShapeDtypeStruct((S, D_MODEL), F32),
                   jax.ShapeDtypeStruct((DREST_CHUNKS, S, ATT_WIDTH), BF16),
                   jax.ShapeDtypeStruct((8, D_MODEL), F32),
                   *[jax.ShapeDtypeStruct((r, S // r, ATT_WIDTH), BF16) for r in DILATIONS],
                   *[jax.ShapeDtypeStruct((r, S // r, 128), F32) for r in DILATIONS]),
        hosted=hosted)


def _adamw(gsrcs, w, m, v, n_stack, name, rows=None):
    n_layers, R, C = w.shape
    assert len(gsrcs) == n_layers
    budget = 96 * 1024
    tr = rows
    if tr is None:
        tr = R
        while tr * C > budget and tr % 16 == 0:
            tr //= 2
    assert R % tr == 0 and (tr % 8 == 0 or tr == R)
    c1 = 1.0 - ADAM_B1 ** ADAM_STEP
    c2 = 1.0 - ADAM_B2 ** ADAM_STEP

    def body(*refs):
        g_refs = refs[:n_layers]
        w_ref, m_ref, v_ref, go_ref, d_ref, mo_ref, vo_ref = refs[n_layers:]

        def update(g_ref):
            if n_stack:
                g = g_ref[0].astype(F32)
                for s in range(1, n_stack):
                    g = g + g_ref[s].astype(F32)
            else:
                g = g_ref[...]
            m_new = ADAM_B1 * m_ref[...] + (1.0 - ADAM_B1) * g
            v_new = ADAM_B2 * v_ref[...] + (1.0 - ADAM_B2) * (g * g)
            m_hat = m_new / c1
            v_hat = v_new / c2
            go_ref[...] = g
            d_ref[...] = -ADAM_LR * (m_hat / (jnp.sqrt(v_hat) + ADAM_EPS) + ADAM_WD * w_ref[...])
            mo_ref[...] = m_new
            vo_ref[...] = v_new

        for layer in range(n_layers):
            pl.when(pl.program_id(0) == layer)(functools.partial(update, g_refs[layer]))

    def g_spec(layer):
        def rows_of(l, i):
            return jnp.where(l == layer, i, 0)
        if n_stack:
            return pl.BlockSpec((n_stack, tr, C), lambda l, i: (0, rows_of(l, i), 0))
        return pl.BlockSpec((tr, C), lambda l, i: (rows_of(l, i), 0))

    blk = pl.BlockSpec((None, tr, C), lambda l, i: (l, i, 0))
    shp = jax.ShapeDtypeStruct((n_layers, R, C), F32)
    return pl.pallas_call(
        body, name=name, grid=(n_layers, R // tr),
        in_specs=[g_spec(layer) for layer in range(n_layers)] + [blk, blk, blk],
        out_specs=(blk, blk, blk, blk), out_shape=(shp, shp, shp, shp),
        compiler_params=_params(("arbitrary", "arbitrary")))(*gsrcs, w, m, v)


def _pair_blocks(w):
    w = w.reshape(8, 2, 64, 64)
    z = jnp.zeros((8, 64, 64), w.dtype)
    top = jnp.concatenate([w[:, 0], z], axis=2)
    bot = jnp.concatenate([z, w[:, 1]], axis=2)
    return jnp.concatenate([top, bot], axis=1).astype(BF16)


def _unpair_blocks(g):
    return jnp.stack([g[:, :64, :64], g[:, 64:, 64:]], axis=1).reshape(16, 64, 64)


def _mats_full(w_pa_g, w_pb_g, w_o_g):
    return dict(w_pa=jnp.transpose(w_pa_g, (1, 0, 2)).reshape(ATT_WIDTH, D_MODEL),
                w_pb=w_pb_g.reshape(D_MODEL, D_MODEL), w_o=w_o_g.reshape(D_MODEL, D_MODEL))


def _layer_params(layer, w_in_g, mats_g, conv_w_full, conv_b, b_rg, b_ig, lru_lambda, w_rg, w_ig, g_pre,
                  g_post):
    cw = jnp.concatenate([conv_w_full, conv_b[None], b_rg[None], b_ig[None], lru_lambda[None]], axis=0)
    p = dict(w_int=_weights_full(w_in_g), cw=cw, wr2=_pair_blocks(w_rg), wi2=_pair_blocks(w_ig),
             g_pre=g_pre, g_post=g_post, lru_lambda=lru_lambda)
    if mats_g is not None:
        p.update(_mats_full(*mats_g))
    return p


def _layer_fwd(x, mod, p, host_proj=None, host_lru=None, host_out=None, own_mats=None, target=None):
    zeros = jnp.zeros((5, D_MODEL), F32)
    vec_pre = jnp.concatenate([p["g_pre"][None], mod[1:2], mod[0:1], zeros], axis=0)
    (ht0, ht1, ht2, q0, q1, q2, rest), got_proj = _norm_proj_fwd(x, vec_pre, p["w_int"], hosted=host_proj)
    h_t = [ht0, ht1, ht2]
    hosts = [None, None, None]
    if own_mats is not None:
        hosts[0], hosts[1] = (list(own_mats[:2]), ["ag"] * 2), ([own_mats[2]], ["ag"])
    qkv, outs, stats, got_attn = [q0, q1, q2], [], [], []
    for g, r in enumerate(DILATIONS):
        (o_g, st_g), got = _attn_fwd(qkv[g], g, r, hosted=hosts[g])
        outs.append(o_g)
        stats.append(st_g)
        got_attn += got
    if own_mats is not None:
        p.update(_mats_full(*got_attn))
    o_att, a_gated, *lse = _attn_combine(outs, stats, rest)
    (hl, a_dec, *gates), got_lru = _lru_fwd(rest, p["cw"], p["wr2"], p["wi2"], hosted=host_lru)
    vec_post = jnp.concatenate([p["g_post"][None], mod[2:3], jnp.zeros((6, D_MODEL), F32)], axis=0)
    (x_new, mix, ya, yb, b_gated, *loss_acc), got_out = _out_fwd(
        a_gated, hl, rest, x, vec_post, p["w_pa"], p["w_pb"], p["w_o"], target=target, hosted=host_out)
    saved = dict(x=x, h_t=h_t, qkv=qkv, rest=rest, o_att=o_att, a_gated=a_gated, lse=lse, hl=hl, a_dec=a_dec,
                 gates=gates, mix=mix, ya=ya, yb=yb, b_gated=b_gated, vec_pre=vec_pre, vec_post=vec_post)
    return (x_new, *loss_acc), saved, got_proj, got_lru, got_out


def _mats_sources(g_pa, g_pb, g_o):
    return [jnp.transpose(g_pa.astype(BF16).reshape(ATT_WIDTH, N_DEV, 128), (1, 0, 2)),
            g_pb.astype(BF16).reshape(N_DEV, 128, D_MODEL), g_o.astype(BF16).reshape(N_DEV, 128, D_MODEL)]


LRU_SMALL = ("conv_b", "w_rg", "b_rg", "w_ig", "b_ig", "lru_lambda")
MOD_SMALL = ("b_mod", "g_pre", "g_post")


def _flat_rows(arrs):
    return jnp.concatenate([a.reshape(-1) for a in arrs]).reshape(-1, 128)


def _layer_bwd(dxn, sv, p, upper_small=None):
    a2a3 = ["a2a"] * 3
    (dout, dya, dyb, dhl, drest, acc_out, *do_dvec), _ = _out_bwd(
        dxn, sv["mix"], sv["ya"], sv["yb"], sv["hl"], sv["rest"], sv["o_att"], sv["vec_post"],
        p["w_pa"], p["w_pb"], p["w_o"])
    mats = _mats_sources(_wgrad_tn(sv["a_gated"], dya, "wgrad_pa"), _wgrad_tn(sv["b_gated"], dyb, "wgrad_pb"),
                         _wgrad_tn(sv["mix"], dout, "wgrad_o"))
    (drest, acc_lru, gwr2, gwi2), _ = _lru_bwd(
        dhl, sv["hl"], sv["a_dec"], sv["gates"], sv["rest"], p["cw"], p["wr2"], p["wi2"], drest)
    small = dict(
        conv_w=acc_lru[0:4], conv_b=acc_lru[4], b_rg=acc_lru[5], b_ig=acc_lru[6],
        lru_lambda=acc_lru[7] * (-jax.nn.sigmoid(-p["lru_lambda"])),
        w_rg=_unpair_blocks(gwr2), w_ig=_unpair_blocks(gwi2))
    hosts = [(mats, a2a3), None, None]
    if upper_small is not None:
        for half, names in enumerate((LRU_SMALL[:3], LRU_SMALL[3:] + ("conv_w",))):
            vec = _flat_rows([jnp.stack([small[n], upper_small[n]]) for n in names])
            hosts[1 + half] = ([vec], ["ag"])
    dqkv, got_attn = [], []
    for g, r in enumerate(DILATIONS):
        d_g, got = _attn_bwd(sv["qkv"][g], do_dvec[g], sv["lse"][g], do_dvec[3 + g], g, r, hosted=hosts[g])
        dqkv.append(d_g)
        got_attn.append(got)
    parts = [_proj_wgrad_group(sv["h_t"][g], dqkv[g], r, f"proj_wgrad_g{g}") for g, r in enumerate(DILATIONS)]
    parts.append(_proj_wgrad_part(sv["h_t"][0], drest, 2, 4, "proj_wgrad_u"))
    parts.append(_proj_wgrad_part(sv["h_t"][0], drest, 7, 0, "proj_wgrad_rest"))
    (dx, acc_pre), got_in = _proj_dgrad_norm_bwd(
        dqkv, drest, p["w_int"], sv["x"], dxn, sv["vec_pre"], hosted=([_grads_to_shards(parts)], ["a2a"]))
    small.update(dmod=jnp.concatenate([acc_pre[0], acc_pre[1], acc_out[0]]), g_pre=acc_pre[2],
                 g_post=acc_out[1])
    return dx, small, dict(mats=got_attn[0], w_in=got_in[0], lru_small=got_attn[1] + got_attn[2])


SMALL_NAMES = ("b_mod", "g_pre", "conv_b", "w_rg", "b_rg", "w_ig", "b_ig", "lru_lambda", "g_post")


def kernel(x, c, w_mod, b_mod, g_pre, w_in, conv_w, conv_b, w_rg, b_rg, w_ig, b_ig, lru_lambda, w_pa, w_pb, w_o, g_post, loss_target, m_w_mod, m_b_mod, m_g_pre, m_w_in, m_conv_w, m_conv_b, m_w_rg, m_b_rg, m_w_ig, m_b_ig, m_lru_lambda, m_w_pa, m_w_pb, m_w_o, m_g_post, v_w_mod, v_b_mod, v_g_pre, v_w_in, v_conv_w, v_conv_b, v_w_rg, v_b_rg, v_w_ig, v_b_ig, v_lru_lambda, v_w_pa, v_w_pb, v_w_o, v_g_post):
    W = dict(w_mod=w_mod, b_mod=b_mod, g_pre=g_pre, w_in=w_in, conv_w=conv_w, conv_b=conv_b, w_rg=w_rg,
             b_rg=b_rg, w_ig=w_ig, b_ig=b_ig, lru_lambda=lru_lambda, w_pa=w_pa, w_pb=w_pb, w_o=w_o,
             g_post=g_post)
    M = dict(w_mod=m_w_mod, b_mod=m_b_mod, g_pre=m_g_pre, w_in=m_w_in, conv_w=m_conv_w, conv_b=m_conv_b,
             w_rg=m_w_rg, b_rg=m_b_rg, w_ig=m_w_ig, b_ig=m_b_ig, lru_lambda=m_lru_lambda, w_pa=m_w_pa,
             w_pb=m_w_pb, w_o=m_w_o, g_post=m_g_post)
    V = dict(w_mod=v_w_mod, b_mod=v_b_mod, g_pre=v_g_pre, w_in=v_w_in, conv_w=v_conv_w, conv_b=v_conv_b,
             w_rg=v_w_rg, b_rg=v_b_rg, w_ig=v_w_ig, b_ig=v_b_ig, lru_lambda=v_lru_lambda, w_pa=v_w_pa,
             w_pb=v_w_pb, w_o=v_w_o, g_post=v_g_post)
    S = x.shape[1]
    me = 4 * lax.axis_index("x") + 2 * lax.axis_index("y") + lax.axis_index("c")
    n_mod = w_mod.shape[2]
    n_in = w_in.shape[2]
    n_conv = conv_w.shape[2]

    c_rows = jnp.broadcast_to(c, (8, D_MODEL))
    w_in_b, w_pa_b, w_pb_b, w_o_b = (t.astype(BF16) for t in (w_in, w_pa, w_pb, w_o))
    g_c, g_win0, g_cw = _gather_two_level([c_rows, w_in_b[0], conv_w], "gather_weights")
    c_all = g_c[:, 0, :]
    c_pad = jnp.concatenate([c_all, jnp.zeros((8, D_MODEL), F32)], axis=0)
    conv_w_full = jnp.transpose(g_cw, (1, 2, 0, 3)).reshape(2, CONV_WIDTH, D_MODEL)

    mod_cols = _mod_fwd(c_pad, w_mod)
    mod_src = jnp.transpose(mod_cols[:, :8], (1, 0, 2))
    mod_src = jnp.concatenate([mod_src, jnp.zeros((8, 6, n_mod), F32)], axis=1)
    (mod_got,) = _exchange([mod_src], ["a2a"], "scatter_mod")
    mod = jnp.transpose(mod_got[:, :2], (1, 0, 2)).reshape(2, 3 * D_MODEL) + b_mod

    def layer_params(layer, w_in_g, mats_g):
        return _layer_params(layer, w_in_g, mats_g, conv_w_full[layer], conv_b[layer], b_rg[layer],
                             b_ig[layer], lru_lambda[layer], w_rg[layer], w_ig[layer], g_pre[layer],
                             g_post[layer])

    layers = [layer_params(0, g_win0, None), None]
    (act,), sv0, (g_win1,), _, mats1_g = _layer_fwd(
        x[0], mod[0].reshape(3, D_MODEL), layers[0], own_mats=(w_pa_b[0], w_pb_b[0], w_o_b[0]),
        host_proj=([w_in_b[1]], ["ag"]), host_out=([w_pa_b[1], w_pb_b[1], w_o_b[1]], ["ag"] * 3))
    layers[1] = layer_params(1, g_win1, mats1_g)
    (dy, loss_acc), sv1, _, _, _ = _layer_fwd(act, mod[1].reshape(3, D_MODEL), layers[1],
                                              target=loss_target[0])

    dy, small1, got1 = _layer_bwd(dy, sv1, layers[1])
    dy, small0, got0 = _layer_bwd(dy, sv0, layers[0], upper_small=small1)
    grad_x = dy[None]
    r_pa, r_pb, r_o = ([got0["mats"][k], got1["mats"][k]] for k in range(3))
    r_in = [got0["w_in"], got1["w_in"]]
    grads = [small0, small1]

    def stack2(name):
        return jnp.stack([grads[0][name], grads[1][name]], axis=0)

    dmod = stack2("dmod")
    dmod_src = jnp.transpose(dmod.reshape(2, N_DEV, n_mod), (1, 0, 2))
    dmod_src = jnp.concatenate([dmod_src, jnp.zeros((8, 6, n_mod), F32)], axis=1)
    mod_vec = _flat_rows([dmod, stack2("g_pre"), stack2("g_post"), loss_acc])
    r_dmod, r_mod_small = _exchange([dmod_src, mod_vec], ["a2a", "ag"], "exchange_grads")
    mod_rows = mod_vec.shape[0] - loss_acc.shape[0]
    loss = 0.5 * jnp.sum(r_mod_small[:, mod_rows, 0]) / D_MODEL
    r_lru_small = got0["lru_small"]

    res = {}
    res["w_in"] = _adamw(r_in, w_in, m_w_in, v_w_in, 8, "adamw_w_in")
    res["w_pa"] = _adamw(r_pa, w_pa, m_w_pa, v_w_pa, 8, "adamw_w_pa")
    res["w_pb"] = _adamw(r_pb, w_pb, m_w_pb, v_w_pb, 8, "adamw_w_pb")
    res["w_o"] = _adamw(r_o, w_o, m_w_o, v_w_o, 8, "adamw_w_o")
    dmod_all = jnp.transpose(r_dmod[:, :2], (1, 0, 2))
    gw_mod = _mod_wgrad(jnp.transpose(c_all), dmod_all)
    res["w_mod"] = _adamw([gw_mod[0], gw_mod[1]], w_mod, m_w_mod, v_w_mod, 0, "adamw_w_mod")
    for names_rep, stack, tag in ((LRU_SMALL[:3], r_lru_small[0], "lru_a"), (LRU_SMALL[3:], r_lru_small[1], "lru_b"),
                                  (MOD_SMALL, r_mod_small, "mod")):
        rows = sum(W[n].size for n in names_rep) // 128
        rep = _adamw([stack], *(_flat_rows([src[n] for n in names_rep])[None] for src in (W, M, V)), 8,
                     f"adamw_small_{tag}", rows=rows // 2 if rows % 16 == 0 else rows)
        off = 0
        for name in names_rep:
            size = W[name].size
            res[name] = tuple(t.reshape(-1)[off:off + size].reshape(W[name].shape) for t in rep)
            off += size
    lru_rows = sum(W[n].size for n in LRU_SMALL[3:]) // 128
    conv_stack = r_lru_small[1][:, lru_rows:].reshape(8, 2, CONV_WIDTH, D_MODEL)
    conv_stack = lax.dynamic_slice_in_dim(conv_stack, me * n_conv, n_conv, axis=3).reshape(8, 8, n_conv)
    res["conv_w"] = _adamw([conv_stack], conv_w.reshape(1, 8, n_conv), m_conv_w.reshape(1, 8, n_conv),
                           v_conv_w.reshape(1, 8, n_conv), 8, "adamw_conv_w")

    names = ("w_mod", "b_mod", "g_pre", "w_in", "conv_w", "conv_b", "w_rg", "b_rg", "w_ig", "b_ig",
             "lru_lambda", "w_pa", "w_pb", "w_o", "g_post")
    outs = [loss, grad_x]
    for k in range(4):
        outs.extend(res[n][k].reshape(W[n].shape) for n in names)
    return tuple(outs)
```

```python
import functools

import jax
import jax.numpy as jnp
import numpy as np
from jax import lax
from jax.experimental import pallas as pl
from jax.experimental.pallas import tpu as pltpu

F32 = jnp.float32
BF16 = jnp.bfloat16

N_DEV = 8
D_MODEL = 1024
HEAD_DIM = 128
HEADS = 4
ATT_WIDTH = HEADS * HEAD_DIM
DILATIONS = (1, 4, 16)
BAND = 128
N_CHUNKS = 18
QKV_CHUNKS = 9
CONV_WIDTH = 4
LRU_C = 8.0
NORM_EPS = 1e-6
NEG_INF = -1e30
ADAM_LR = 0.001
ADAM_B1 = 0.9
ADAM_B2 = 0.999
ADAM_EPS = 1e-08
ADAM_WD = 0.01
ADAM_STEP = 10
CHUNK_PERM = (0, 3, 6, 1, 4, 7, 2, 5, 8, 10, 11, 12, 13, 14, 15, 16, 17, 9)
DREST_CHUNKS = 10
VMEM_LIMIT = 56 * 1024 * 1024
MESH_AXES = ("x", "y", "c")


def _params(sem=None):
    return pltpu.CompilerParams(dimension_semantics=sem, vmem_limit_bytes=VMEM_LIMIT)


def _silu(x):
    return x * jax.nn.sigmoid(x)


def _dsilu(x):
    s = jax.nn.sigmoid(x)
    return s * (1.0 + x * (1.0 - s))


def _neg_expm1(x):
    series = -x * (1.0 + x * (0.5 + x * (1.0 / 6.0 + x * (1.0 / 24.0))))
    return jnp.where(x > -0.05, series, 1.0 - jnp.exp(x))


def _softplus_neg(lam):
    z = jnp.exp(-jnp.abs(lam))
    small = z * (1.0 - z * (0.5 - z * (1.0 / 3.0 - z * 0.25)))
    log1p_z = jnp.where(z < 1e-2, small, jnp.log(1.0 + z))
    return jnp.maximum(-lam, 0.0) + log1p_z


def _dot(a, b):
    return jnp.dot(a, b, preferred_element_type=F32)


def _dot_nt(a, b):
    return lax.dot_general(a, b, (((1,), (1,)), ((), ())), preferred_element_type=F32)


def _dot_tn(a, b):
    return lax.dot_general(a, b, (((0,), (0,)), ((), ())), preferred_element_type=F32)


def _perm_matrix(r, rows, transpose=False):
    n = rows // r
    p = np.zeros((rows, rows), np.float32)
    dst = np.arange(rows)
    p[dst, (dst % n) * r + dst // n] = 1.0
    return jnp.asarray(p.T if transpose else p, dtype=BF16)


def _permute_f32(p, x, pieces):
    part = x.astype(BF16)
    acc = _dot(p, part)
    for _ in range(pieces - 1):
        x = x - part.astype(F32)
        part = x.astype(BF16)
        acc = acc + _dot(p, part)
    return acc


def _rows_of_residues(ref, lead=()):
    r = ref.shape[len(lead)]
    if r == 1:
        return ref[lead + (0,)]
    return jnp.concatenate([ref[lead + (rho,)] for rho in range(r)], axis=0)


def _store_residues(ref, x, lead=()):
    r = ref.shape[len(lead)]
    n = x.shape[0] // r
    for rho in range(r):
        ref[lead + (rho,)] = x[rho * n:(rho + 1) * n]


ANY_SPEC = pl.BlockSpec(memory_space=pl.ANY)


def _exchange_shapes(arrs, modes):
    return [jax.ShapeDtypeStruct(((N_DEV,) + a.shape) if mode == "ag" else a.shape, a.dtype)
            for a, mode in zip(arrs, modes)]


def _exchange_sems(n):
    return [pltpu.SemaphoreType.DMA((7 * n,)), pltpu.SemaphoreType.DMA((7 * n,)),
            pltpu.SemaphoreType.DMA((n,))]


def _exchange_copies(ins, outs, modes, sems):
    n = len(ins)
    send_sems, recv_sems, local_sems = sems
    x, y, c = lax.axis_index("x"), lax.axis_index("y"), lax.axis_index("c")
    me = 4 * x + 2 * y + c

    def src_for(a, dev):
        return ins[a] if modes[a] == "ag" else ins[a].at[dev]

    local = [pltpu.make_async_copy(src_for(a, me), outs[a].at[me], local_sems.at[a]) for a in range(n)]
    sends, arrivals = [], []
    for k in range(1, N_DEV):
        px = 1 - x if (k >> 2) & 1 else x
        py = 1 - y if (k >> 1) & 1 else y
        pc = 1 - c if k & 1 else c
        peer = 4 * px + 2 * py + pc
        for a in range(n):
            s = (k - 1) * n + a
            for dst, group in ((me, sends), (peer, arrivals)):
                group.append(pltpu.make_async_remote_copy(
                    src_ref=src_for(a, peer), dst_ref=outs[a].at[dst],
                    send_sem=send_sems.at[s], recv_sem=recv_sems.at[s],
                    device_id=(px, py, pc), device_id_type=pl.DeviceIdType.MESH))
    return local, sends, arrivals


def _exchange_start(copies):
    local, sends, _ = copies
    for cp in local + sends:
        cp.start()


def _exchange_wait(copies):
    local, sends, arrivals = copies
    for cp in arrivals:
        cp.wait_recv()
    for cp in sends:
        cp.wait_send()
    for cp in local:
        cp.wait()


def _exchange(arrs, modes, name):
    n = len(arrs)

    def body(*refs):
        copies = _exchange_copies(refs[:n], refs[n:2 * n], modes, refs[2 * n:])
        _exchange_start(copies)
        _exchange_wait(copies)

    outs = pl.pallas_call(
        body, name=name, out_shape=tuple(_exchange_shapes(arrs, modes)),
        in_specs=[ANY_SPEC] * n, out_specs=tuple([ANY_SPEC] * n),
        scratch_shapes=_exchange_sems(n),
    )(*arrs)
    return list(outs)


def _gather_two_level(arrs, name):
    n = len(arrs)

    def body(*refs):
        ins, outs = refs[:n], refs[n:2 * n]
        send_sems, recv_sems, local_sems = refs[2 * n:]
        x, y, c = lax.axis_index("x"), lax.axis_index("y"), lax.axis_index("c")
        me, sibling = (x, y, c), (x, y, 1 - c)
        chips = [(1 - x, y), (x, 1 - y), (1 - x, 1 - y)]

        def copy(a, k, block, to, src=None):
            slot = 4 * block[0] + 2 * block[1] + block[2]
            return pltpu.make_async_remote_copy(
                src_ref=outs[a].at[slot] if src is None else src, dst_ref=outs[a].at[slot],
                send_sem=send_sems.at[7 * a + k], recv_sem=recv_sems.at[7 * a + k],
                device_id=to, device_id_type=pl.DeviceIdType.MESH)

        mine = [pltpu.make_async_copy(ins[a], outs[a].at[4 * x + 2 * y + c], local_sems.at[a])
                for a in range(n)]
        first = []
        for a in range(n):
            first.append(copy(a, 0, me, sibling, src=ins[a]))
            first += [copy(a, 1 + j, me, (*chip, c), src=ins[a]) for j, chip in enumerate(chips)]
        for cp in mine + first:
            cp.start()
        passed = []
        for j, chip in enumerate(chips):
            for a in range(n):
                copy(a, 1 + j, (*chip, c), me).wait_recv()
                passed.append(copy(a, 4 + j, (*chip, c), sibling))
                passed[-1].start()
        for a in range(n):
            copy(a, 0, sibling, me).wait_recv()
            for j, chip in enumerate(chips):
                copy(a, 4 + j, (*chip, 1 - c), me).wait_recv()
        for cp in first + passed:
            cp.wait_send()
        for cp in mine:
            cp.wait()

    outs = pl.pallas_call(
        body, name=name, out_shape=tuple(_exchange_shapes(arrs, ["ag"] * n)),
        in_specs=[ANY_SPEC] * n, out_specs=tuple([ANY_SPEC] * n),
        scratch_shapes=_exchange_sems(n),
    )(*arrs)
    return list(outs)


SHARD_COLS = N_CHUNKS * ATT_WIDTH // N_DEV


def _weights_full(shards):
    return jnp.transpose(shards, (1, 0, 2)).reshape(D_MODEL, N_CHUNKS * ATT_WIDTH)


def _wcols(w_ref, k):
    orig = CHUNK_PERM[k]
    return w_ref[:, orig * ATT_WIDTH:(orig + 1) * ATT_WIDTH]


def _store_shard_columns(stage_ref, shards_ref, sems, first_chunk):
    copies = []
    for t in range(stage_ref.shape[1] // ATT_WIDTH):
        col = ATT_WIDTH * CHUNK_PERM[first_chunk + t]
        end = col + ATT_WIDTH
        while col < end:
            dev = col // SHARD_COLS
            upto = min(end, (dev + 1) * SHARD_COLS)
            src = stage_ref.at[:, pl.ds(t * ATT_WIDTH + ATT_WIDTH - (end - col), upto - col)]
            dst = shards_ref.at[dev, :, pl.ds(col - dev * SHARD_COLS, upto - col)]
            copies.append(pltpu.make_async_copy(src, dst, sems.at[len(copies)]))
            col = upto
    for cp in copies:
        cp.start()
    for cp in copies:
        cp.wait()


SHARD_COPIES_PER_CHUNK = 2


def _call(body, args, *, name, grid, in_specs, out_specs, out_shape, scratch_shapes=(), aliases=None,
          hosted=None):
    sem = ("arbitrary",) * len(grid)
    if hosted is None:
        outs = pl.pallas_call(
            body, name=name, grid=grid, in_specs=in_specs, out_specs=tuple(out_specs),
            out_shape=tuple(out_shape), scratch_shapes=list(scratch_shapes),
            input_output_aliases=aliases or {}, compiler_params=_params(sem))(*args)
        return list(outs), []
    x_arrs, modes = hosted
    n_in, n_out, n_scr, nx = len(args), len(out_shape), len(scratch_shapes), len(x_arrs)

    def wrapped(*refs):
        ins, x_ins = refs[:n_in], refs[n_in:n_in + nx]
        outs = refs[n_in + nx:n_in + nx + n_out]
        x_outs = refs[n_in + nx + n_out:n_in + 2 * nx + n_out]
        scr = refs[n_in + 2 * nx + n_out:n_in + 2 * nx + n_out + n_scr]
        sems = refs[n_in + 2 * nx + n_out + n_scr:]
        first = pl.program_id(0) == 0
        last = pl.program_id(0) == grid[0] - 1
        for axis in range(1, len(grid)):
            first = jnp.logical_and(first, pl.program_id(axis) == 0)
            last = jnp.logical_and(last, pl.program_id(axis) == grid[axis] - 1)

        @pl.when(first)
        def _():
            _exchange_start(_exchange_copies(x_ins, x_outs, modes, sems))

        body(*ins, *outs, *scr)

        @pl.when(last)
        def _():
            _exchange_wait(_exchange_copies(x_ins, x_outs, modes, sems))

    outs = pl.pallas_call(
        wrapped, name=name, grid=grid, in_specs=list(in_specs) + [ANY_SPEC] * nx,
        out_specs=tuple(out_specs) + tuple([ANY_SPEC] * nx),
        out_shape=tuple(out_shape) + tuple(_exchange_shapes(x_arrs, modes)),
        scratch_shapes=list(scratch_shapes) + _exchange_sems(nx),
        input_output_aliases=aliases or {}, compiler_params=_params(sem))(*args, *x_arrs)
    return list(outs[:n_out]), list(outs[n_out:])


def _mod_fwd(c_pad, w_mod):
    def body(c_ref, w_ref, o_ref):
        sc = _silu(c_ref[...]).astype(BF16)
        for layer in range(2):
            o_ref[layer] = _dot(sc, w_ref[layer].astype(BF16))

    return pl.pallas_call(
        body, name="mod_fwd", out_shape=jax.ShapeDtypeStruct((2, 16, w_mod.shape[2]), F32),
        compiler_params=_params())(c_pad, w_mod)


def _mod_wgrad(c_t, dmod):
    n_cols = dmod.shape[2]

    def body(c_ref, d_ref, o_ref):
        sc = _silu(c_ref[...]).astype(BF16).astype(F32)
        for layer in range(2):
            dm = d_ref[layer].astype(BF16).astype(F32)
            acc = sc[:, 0:1] * dm[0:1, :]
            for b in range(1, N_DEV):
                acc = acc + sc[:, b:b + 1] * dm[b:b + 1, :]
            o_ref[layer] = acc

    return pl.pallas_call(
        body, name="mod_wgrad", out_shape=jax.ShapeDtypeStruct((2, D_MODEL, n_cols), F32),
        compiler_params=_params())(c_t, dmod)


GROUP_COLS = 3 * ATT_WIDTH
PROJ_TM = 256


def _norm_proj_fwd(x, vec, w_int, hosted=None):
    S = x.shape[0]
    tm = PROJ_TM
    perms = [_perm_matrix(r, tm) for r in DILATIONS[1:]]

    def body(x_ref, v_ref, w_ref, p1_ref, p2_ref, ht0_ref, ht1_ref, ht2_ref, q0_ref, q1_ref, q2_ref,
             rest_ref):
        xv = x_ref[...]
        rstd = lax.rsqrt(jnp.mean(xv * xv, axis=-1, keepdims=True) + NORM_EPS)
        hf = ((xv * rstd) * v_ref[0:1, :]) * (1.0 + v_ref[1:2, :]) + v_ref[2:3, :]
        h = hf.astype(BF16)
        for g, (q_ref, ht_ref, p_ref) in enumerate(((q0_ref, ht0_ref, None), (q1_ref, ht1_ref, p1_ref),
                                                    (q2_ref, ht2_ref, p2_ref))):
            rows_f = hf if p_ref is None else _dot(p_ref[...], h)
            ht_ref[...] = rows_f.T.astype(BF16)
            rows = rows_f.astype(BF16)
            n = tm // q_ref.shape[0]
            for t in range(3):
                res = _dot(rows, _wcols(w_ref, 3 * g + t)).astype(BF16)
                for rho in range(q_ref.shape[0]):
                    q_ref[rho, :, t * ATT_WIDTH:(t + 1) * ATT_WIDTH] = res[rho * n:(rho + 1) * n]
        split = (QKV_CHUNKS - 1) * ATT_WIDTH
        rest_ref[:, :split] = _dot(h, w_ref[:, (QKV_CHUNKS + 1) * ATT_WIDTH:])
        rest_ref[:, split:] = _dot(h, w_ref[:, QKV_CHUNKS * ATT_WIDTH:(QKV_CHUNKS + 1) * ATT_WIDTH])

    n_rest = QKV_CHUNKS * ATT_WIDTH
    whole = pl.BlockSpec(memory_space=pltpu.VMEM)
    ht_spec = pl.BlockSpec((D_MODEL, tm), lambda i: (0, i))
    ht_shape = jax.ShapeDtypeStruct((D_MODEL, S), BF16)
    return _call(
        body, (x, vec, w_int, *perms), name="norm_proj_fwd", grid=(S // tm,),
        in_specs=[pl.BlockSpec((tm, D_MODEL), lambda i: (i, 0)),
                  pl.BlockSpec((8, D_MODEL), lambda i: (0, 0)), whole, whole, whole],
        out_specs=(ht_spec, ht_spec, ht_spec,
                   *[pl.BlockSpec((r, tm // r, GROUP_COLS), lambda i: (0, i, 0)) for r in DILATIONS],
                   pl.BlockSpec((tm, n_rest), lambda i: (i, 0))),
        out_shape=(ht_shape, ht_shape, ht_shape,
                   *[jax.ShapeDtypeStruct((r, S // r, GROUP_COLS), BF16) for r in DILATIONS],
                   jax.ShapeDtypeStruct((S, n_rest), F32)),
        hosted=hosted)


def _proj_dgrad_norm_bwd(dqkv, drest, w_int, x, dxn, vec, hosted=None):
    S = x.shape[0]
    tm = PROJ_TM
    perms = [_perm_matrix(r, tm, transpose=True) for r in DILATIONS[1:]]

    def body(d0_ref, d1_ref, d2_ref, dr_ref, w_ref, p1_ref, p2_ref, x_ref, dxn_ref, v_ref, dx_ref, acc_ref):
        i = pl.program_id(0)

        @pl.when(i == 0)
        def _():
            acc_ref[...] = jnp.zeros_like(acc_ref)

        wcols = functools.partial(_wcols, w_ref)
        dhv = None
        for g, (d_ref, p_ref) in enumerate(((d0_ref, None), (d1_ref, p1_ref), (d2_ref, p2_ref))):
            for t in range(3):
                d = _rows_of_residues(d_ref, (t,))
                if p_ref is not None:
                    d = _dot(p_ref[...], d).astype(BF16)
                part = _dot_nt(d, wcols(3 * g + t))
                dhv = part if dhv is None else dhv + part
        for t in range(7):
            dhv = dhv + _dot_nt(dr_ref[t], wcols(11 + t))
        for t in range(2):
            dhv = dhv + _dot_nt(dr_ref[8 + t], wcols(9 + t))

        xv = x_ref[...]
        g = v_ref[0:1, :]
        sc1 = 1.0 + v_ref[1:2, :]
        rstd = lax.rsqrt(jnp.mean(xv * xv, axis=-1, keepdims=True) + NORM_EPS)
        xhat = xv * rstd
        acc_ref[0:1, :] += jnp.sum(dhv, axis=0, keepdims=True)
        acc_ref[1:2, :] += jnp.sum(dhv * (xhat * g), axis=0, keepdims=True)
        acc_ref[2:3, :] += jnp.sum(dhv * xhat * sc1, axis=0, keepdims=True)
        dxhat = dhv * (g * sc1)
        dx = rstd * (dxhat - xhat * jnp.mean(dxhat * xhat, axis=-1, keepdims=True))
        dx_ref[...] = dx + dxn_ref[...]

    row = pl.BlockSpec((tm, D_MODEL), lambda i: (i, 0))
    vec_spec = pl.BlockSpec((8, D_MODEL), lambda i: (0, 0))
    whole = pl.BlockSpec(memory_space=pltpu.VMEM)
    return _call(
        body, (*dqkv, drest, w_int, *perms, x, dxn, vec), name="proj_dgrad_norm_bwd", grid=(S // tm,),
        in_specs=[*[pl.BlockSpec((3, r, tm // r, ATT_WIDTH), lambda i: (0, 0, i, 0)) for r in DILATIONS],
                  pl.BlockSpec((DREST_CHUNKS, tm, ATT_WIDTH), lambda i: (0, i, 0)),
                  whole, whole, whole, row, row, vec_spec],
        out_specs=(row, vec_spec),
        out_shape=(jax.ShapeDtypeStruct((S, D_MODEL), F32), jax.ShapeDtypeStruct((8, D_MODEL), F32)),
        hosted=hosted)


WGRAD_TS = 512


def _proj_wgrad_call(h_t, d, d_spec, rows_of, n_chunks, first_chunk, shards, name):
    S = h_t.shape[1]
    n_steps = S // WGRAD_TS
    width = n_chunks * ATT_WIDTH

    def body(*refs):
        h_ref, d_ref = refs[:2]
        shards_ref, acc_ref, stage_ref, sems = refs[-4:]
        i = pl.program_id(0)

        @pl.when(i == 0)
        def _():
            acc_ref[...] = jnp.zeros_like(acc_ref)

        ht = h_ref[...]
        for t in range(n_chunks):
            acc_ref[:, t * ATT_WIDTH:(t + 1) * ATT_WIDTH] += _dot(ht, rows_of(d_ref, t))

        @pl.when(i == n_steps - 1)
        def _():
            stage_ref[...] = acc_ref[...].astype(BF16)
            _store_shard_columns(stage_ref, shards_ref, sems, first_chunk)

    chained = shards is not None
    return pl.pallas_call(
        body, name=name, grid=(n_steps,),
        in_specs=[pl.BlockSpec((D_MODEL, WGRAD_TS), lambda i: (0, i)), d_spec] + [ANY_SPEC] * chained,
        out_specs=ANY_SPEC,
        out_shape=jax.ShapeDtypeStruct((N_DEV, D_MODEL, SHARD_COLS), BF16),
        scratch_shapes=[pltpu.VMEM((D_MODEL, width), F32), pltpu.VMEM((D_MODEL, width), BF16),
                        pltpu.SemaphoreType.DMA((SHARD_COPIES_PER_CHUNK * n_chunks,))],
        input_output_aliases={2: 0} if chained else {},
        compiler_params=_params(("arbitrary",)))(h_t, d, *([shards] * chained))


def _proj_wgrad_part(h_t, d, n_chunks, chunk_block, first_chunk, shards, name):
    spec = pl.BlockSpec((n_chunks, WGRAD_TS, ATT_WIDTH), lambda i: (chunk_block, i, 0))
    return _proj_wgrad_call(h_t, d, spec, lambda d_ref, t: d_ref[t], n_chunks, first_chunk, shards, name)


def _proj_wgrad_group(h_t, d, g, r, shards, name):
    n = PROJ_TM // r

    def rows_of(d_ref, t):
        if r == 1:
            return d_ref[t, 0]
        return jnp.concatenate([d_ref[t, rho, s * n:(s + 1) * n] for s in range(WGRAD_TS // PROJ_TM)
                                for rho in range(r)], axis=0)

    spec = pl.BlockSpec((3, r, WGRAD_TS // r, ATT_WIDTH), lambda i: (0, 0, i, 0))
    return _proj_wgrad_call(h_t, d, spec, rows_of, 3, 3 * g, shards, name)


def _wgrad_tn(a, b, name):
    S, M = a.shape
    N = b.shape[1]
    ts = 1024

    def body(a_ref, b_ref, o_ref):
        i = pl.program_id(0)
        part = _dot_tn(a_ref[...], b_ref[...])

        @pl.when(i == 0)
        def _():
            o_ref[...] = part

        @pl.when(i > 0)
        def _():
            o_ref[...] += part

    return pl.pallas_call(
        body, name=name, grid=(S // ts,),
        in_specs=[pl.BlockSpec((ts, M), lambda i: (i, 0)), pl.BlockSpec((ts, N), lambda i: (i, 0))],
        out_specs=pl.BlockSpec((M, N), lambda i: (0, 0)),
        out_shape=jax.ShapeDtypeStruct((M, N), F32),
        compiler_params=_params(("arbitrary",)))(a, b)


def _attn_fwd(qkv, g, r, hosted=None):
    L = qkv.shape[1]
    nb2 = L // (2 * BAND)
    scale = HEAD_DIM ** -0.5

    def body(q_ref, kp_ref, kc_ref, vp_ref, vc_ref, o_ref, st_ref):
        m_step = pl.program_id(1)
        ii = lax.broadcasted_iota(jnp.int32, (BAND, 2 * BAND), 0)
        kk = lax.broadcasted_iota(jnp.int32, (BAND, 2 * BAND), 1)
        band = jnp.logical_and(kk >= ii, kk <= ii + BAND)
        masks = (jnp.logical_and(band, jnp.logical_or(kk >= BAND, m_step > 0)), band)
        lane = lax.broadcasted_iota(jnp.int32, (BAND, 128), 1)
        stats = [jnp.zeros((BAND, 128), F32), jnp.zeros((BAND, 128), F32)]
        for h in range(HEADS):
            sl = slice(h * HEAD_DIM, (h + 1) * HEAD_DIM)
            kc, vc = kc_ref[:, sl], vc_ref[:, sl]
            keys = (jnp.concatenate([kp_ref[:, sl], kc[:BAND]], axis=0), kc)
            vals = (jnp.concatenate([vp_ref[:, sl], vc[:BAND]], axis=0), vc)
            for b in range(2):
                rows = slice(b * BAND, (b + 1) * BAND)
                s = jnp.where(masks[b], _dot_nt(q_ref[rows, sl], keys[b]) * scale, NEG_INF)
                m = jnp.max(s, axis=1, keepdims=True)
                p = jnp.exp(s - m)
                l = jnp.sum(p, axis=1, keepdims=True)
                o_ref[rows, sl] = _dot(p.astype(BF16), vals[b]) / l
                stats[b] = jnp.where(lane == h, m, stats[b])
                stats[b] = jnp.where(lane == HEADS + h, l, stats[b])
        st_ref[0:BAND, :] = stats[0]
        st_ref[BAND:, :] = stats[1]

    two = (None, 2 * BAND, ATT_WIDTH)
    one = (None, BAND, ATT_WIDTH)

    def prev(m):
        return jnp.maximum(2 * m - 1, 0)

    return _call(
        body, (qkv, qkv, qkv, qkv, qkv), name=f"attn_fwd_g{g}", grid=(r, nb2),
        in_specs=[
            pl.BlockSpec(two, lambda rho, m: (rho, m, 0)),
            pl.BlockSpec(one, lambda rho, m: (rho, prev(m), 1)),
            pl.BlockSpec(two, lambda rho, m: (rho, m, 1)),
            pl.BlockSpec(one, lambda rho, m: (rho, prev(m), 2)),
            pl.BlockSpec(two, lambda rho, m: (rho, m, 2)),
        ],
        out_specs=(pl.BlockSpec(two, lambda rho, m: (rho, m, 0)),
                   pl.BlockSpec((None, 2 * BAND, 128), lambda rho, m: (rho, m, 0))),
        out_shape=(jax.ShapeDtypeStruct((r, L, ATT_WIDTH), F32),
                   jax.ShapeDtypeStruct((r, L, 128), F32)),
        hosted=hosted)


def _attn_combine(outs, stats, rest):
    S = rest.shape[0]
    tm = 256
    gatt_blk = 8
    back = [_perm_matrix(r, tm, transpose=True) for r in DILATIONS[1:]]
    forth = [_perm_matrix(r, tm) for r in DILATIONS[1:]]

    def body(o0_ref, o1_ref, o2_ref, s0_ref, s1_ref, s2_ref, g_ref, b1_ref, b2_ref, f1_ref, f2_ref,
             o_ref, a_ref, l0_ref, l1_ref, l2_ref):
        outs_nat = [o0_ref[0]] + [_permute_f32(b_ref[...], _rows_of_residues(o_g), 2)
                                  for o_g, b_ref in ((o1_ref, b1_ref), (o2_ref, b2_ref))]
        st = [s0_ref[0]] + [_permute_f32(b_ref[...], _rows_of_residues(s_g), 3)
                            for s_g, b_ref in ((s1_ref, b1_ref), (s2_ref, b2_ref))]
        lane = lax.broadcasted_iota(jnp.int32, (tm, 128), 1)
        lse_out = jnp.zeros((tm, 128), F32)
        for h in range(HEADS):
            sl = slice(h * HEAD_DIM, (h + 1) * HEAD_DIM)
            ms = [s[:, h:h + 1] for s in st]
            ls = [s[:, HEADS + h:HEADS + h + 1] for s in st]
            m_all = jnp.maximum(jnp.maximum(ms[0], ms[1]), ms[2])
            ws = [l * jnp.exp(m - m_all) for m, l in zip(ms, ls)]
            den = ws[0] + ws[1] + ws[2]
            o = (ws[0] * outs_nat[0][:, sl] + ws[1] * outs_nat[1][:, sl] + ws[2] * outs_nat[2][:, sl]) / den
            o_ref[:, sl] = o
            a_ref[:, sl] = (o * _silu(g_ref[:, sl])).astype(BF16)
            lse_out = jnp.where(lane == h, m_all + jnp.log(den), lse_out)
        l0_ref[0] = lse_out
        for l_ref, f_ref in ((l1_ref, f1_ref), (l2_ref, f2_ref)):
            _store_residues(l_ref, _permute_f32(f_ref[...], lse_out, 3))

    o_spec = pl.BlockSpec((tm, ATT_WIDTH), lambda i: (i, 0))
    whole = pl.BlockSpec(memory_space=pltpu.VMEM)

    def res_spec(r, width):
        return pl.BlockSpec((r, tm // r, width), lambda i: (0, i, 0))

    return pl.pallas_call(
        body, name="attn_combine", grid=(S // tm,),
        in_specs=[*[res_spec(r, ATT_WIDTH) for r in DILATIONS], *[res_spec(r, 128) for r in DILATIONS],
                  pl.BlockSpec((tm, ATT_WIDTH), lambda i: (i, gatt_blk)), whole, whole, whole, whole],
        out_specs=(o_spec, o_spec, *[res_spec(r, 128) for r in DILATIONS]),
        out_shape=(jax.ShapeDtypeStruct((S, ATT_WIDTH), F32), jax.ShapeDtypeStruct((S, ATT_WIDTH), BF16),
                   *[jax.ShapeDtypeStruct((r, S // r, 128), F32) for r in DILATIONS]),
        compiler_params=_params(("parallel",)))(*outs, *stats, rest, *back, *forth)


def _attn_bwd(qkv, do, lse, dvec, g, r, hosted=None):
    L = qkv.shape[1]
    nb = L // BAND
    nb2 = nb // 2
    scale = HEAD_DIM ** -0.5

    def body(qc_ref, qn_ref, k_ref, v_ref, doc_ref, don_ref, lc_ref, ln_ref, dc_ref, dn_ref,
             out_ref, carry_ref):
        j = pl.program_id(1)

        @pl.when(j == 0)
        def _():
            carry_ref[...] = jnp.zeros_like(carry_ref)

        ii = lax.broadcasted_iota(jnp.int32, (3 * BAND, 2 * BAND), 0)
        kk = lax.broadcasted_iota(jnp.int32, (3 * BAND, 2 * BAND), 1)
        mask = jnp.logical_and(jnp.logical_and(kk <= ii, kk >= ii - BAND),
                               jnp.logical_or(ii < 2 * BAND, j < nb2 - 1))
        lse3 = jnp.concatenate([lc_ref[...], ln_ref[...]], axis=0)
        dvec3 = jnp.concatenate([dc_ref[...], dn_ref[...]], axis=0)
        for h in range(HEADS):
            sl = slice(h * HEAD_DIM, (h + 1) * HEAD_DIM)
            k = k_ref[:, sl]
            v = v_ref[:, sl]
            q = jnp.concatenate([qc_ref[:, sl], qn_ref[:, sl]], axis=0)
            do = jnp.concatenate([doc_ref[:, sl], don_ref[:, sl]], axis=0)
            p = jnp.where(mask, jnp.exp(_dot_nt(q, k) * scale - lse3[:, h:h + 1]), 0.0)
            ds = (p * (_dot_nt(do, v) - dvec3[:, h:h + 1])).astype(BF16)
            dq3 = _dot(ds, k) * scale
            out_ref[0, 0:BAND, sl] = (carry_ref[:, sl] + dq3[:BAND]).astype(BF16)
            out_ref[0, BAND:, sl] = dq3[BAND:2 * BAND].astype(BF16)
            out_ref[1, :, sl] = (_dot_tn(ds, q) * scale).astype(BF16)
            out_ref[2, :, sl] = _dot_tn(p.astype(BF16), do).astype(BF16)
            carry_ref[:, sl] = dq3[2 * BAND:]

    two = (None, 2 * BAND, ATT_WIDTH)
    one = (None, BAND, ATT_WIDTH)
    stwo = (None, 2 * BAND, 128)
    sone = (None, BAND, 128)

    def nxt(j):
        return jnp.minimum(2 * j + 2, nb - 1)

    (out,), got = _call(
        body, (qkv, qkv, qkv, qkv, do, do, lse, lse, dvec, dvec), name=f"attn_bwd_g{g}", grid=(r, nb2),
        in_specs=[
            pl.BlockSpec(two, lambda rho, j: (rho, j, 0)),
            pl.BlockSpec(one, lambda rho, j: (rho, nxt(j), 0)),
            pl.BlockSpec(two, lambda rho, j: (rho, j, 1)),
            pl.BlockSpec(two, lambda rho, j: (rho, j, 2)),
            pl.BlockSpec(two, lambda rho, j: (rho, j, 0)),
            pl.BlockSpec(one, lambda rho, j: (rho, nxt(j), 0)),
            pl.BlockSpec(stwo, lambda rho, j: (rho, j, 0)),
            pl.BlockSpec(sone, lambda rho, j: (rho, nxt(j), 0)),
            pl.BlockSpec(stwo, lambda rho, j: (rho, j, 0)),
            pl.BlockSpec(sone, lambda rho, j: (rho, nxt(j), 0)),
        ],
        out_specs=(pl.BlockSpec((3, None, 2 * BAND, ATT_WIDTH), lambda rho, j: (0, rho, j, 0)),),
        out_shape=(jax.ShapeDtypeStruct((3, r, L, ATT_WIDTH), BF16),),
        scratch_shapes=[pltpu.VMEM((BAND, ATT_WIDTH), F32)],
        hosted=hosted)
    return out, got


LRU_T = 256


def _linear_scan(a, b, carry, reverse):
    T, C = a.shape
    a = a.reshape(T // 8, 8, C)
    b = b.reshape(T // 8, 8, C)
    row8 = lax.broadcasted_iota(jnp.int32, a.shape, 1)
    for s in (1, 2, 4):
        keep = (row8 < 8 - s) if reverse else (row8 >= s)
        shift = 8 - s if reverse else s
        b_sh = jnp.where(keep, pltpu.roll(b, shift, 1), 0.0)
        a_sh = jnp.where(keep, pltpu.roll(a, shift, 1), 1.0)
        b = a * b_sh + b
        a = a * a_sh
    tiles = [None] * (T // 8)
    order = range(T // 8 - 1, -1, -1) if reverse else range(T // 8)
    for k in order:
        y = b[k] + a[k] * carry
        tiles[k] = y
        carry = y[0:1] if reverse else y[7:8]
    return jnp.concatenate(tiles, axis=0), carry


def _gate_matmuls(ucb, w_ref, bias):
    parts = [_dot(ucb[:, j * 128:(j + 1) * 128], w_ref[j]) for j in range(8)]
    return jnp.concatenate(parts, axis=1) + bias


def _rows_before(x, prev8, j):
    T, C = x.shape
    rot = pltpu.roll(x.reshape(T // 8, 8, C), j, 1)
    rot_prev = jnp.concatenate([pltpu.roll(prev8, j, 0)[None], rot[:-1]], axis=0)
    row8 = lax.broadcasted_iota(jnp.int32, rot.shape, 1)
    return jnp.where(row8 >= j, rot, rot_prev).reshape(T, C)


def _rows_after(x, next8, j):
    T, C = x.shape
    rot = pltpu.roll(x.reshape(T // 8, 8, C), 8 - j, 1)
    rot_next = jnp.concatenate([rot[1:], pltpu.roll(next8, 8 - j, 0)[None]], axis=0)
    row8 = lax.broadcasted_iota(jnp.int32, rot.shape, 1)
    return jnp.where(row8 < 8 - j, rot, rot_next).reshape(T, C)


def _conv_fwd(u, u_prev8, cw_ref):
    uc = cw_ref[4:5, :] + cw_ref[0:1, :] * u
    for j in range(1, CONV_WIDTH):
        uc = uc + cw_ref[j:j + 1, :] * _rows_before(u, u_prev8, j)
    return uc


def _lru_fwd(rest, cw, wr2, wi2, hosted=None):
    S = rest.shape[0]
    T = LRU_T

    def body(u_ref, cw_ref, wr_ref, wi_ref, h_ref, a_ref, uc_ref, r_ref, ig_ref, mult_ref,
             ucar_ref, hcar_ref):
        c = pl.program_id(0)

        @pl.when(c == 0)
        def _():
            ucar_ref[...] = jnp.zeros_like(ucar_ref)
            hcar_ref[...] = jnp.zeros_like(hcar_ref)

        u = u_ref[...]
        uc = _conv_fwd(u, ucar_ref[...], cw_ref)
        ucar_ref[...] = u[T - 8:, :]
        ucb = uc.astype(BF16)
        r = jax.nn.sigmoid(_gate_matmuls(ucb, wr_ref, cw_ref[5:6, :]))
        ig = jax.nn.sigmoid(_gate_matmuls(ucb, wi_ref, cw_ref[6:7, :]))
        log_a = -LRU_C * r * _softplus_neg(cw_ref[7:8, :])
        a = jnp.exp(log_a)
        mult = jnp.sqrt(_neg_expm1(2.0 * log_a))
        b = mult * (ig * uc)
        a_ref[...] = a
        uc_ref[...] = uc
        r_ref[...] = r
        ig_ref[...] = ig
        mult_ref[...] = mult
        h, last = _linear_scan(a, b, hcar_ref[0:1, :], reverse=False)
        h_ref[...] = h
        hcar_ref[...] = jnp.broadcast_to(last, (8, D_MODEL))

    row_spec = pl.BlockSpec((T, D_MODEL), lambda c: (c, 0))
    row_shape = jax.ShapeDtypeStruct((S, D_MODEL), F32)
    return _call(
        body, (rest, cw, wr2, wi2), name="lru_fwd", grid=(S // T,),
        in_specs=[row_spec, pl.BlockSpec((8, D_MODEL), lambda c: (0, 0)),
                  pl.BlockSpec((8, 128, 128), lambda c: (0, 0, 0)),
                  pl.BlockSpec((8, 128, 128), lambda c: (0, 0, 0))],
        out_specs=(row_spec,) * 6, out_shape=(row_shape,) * 6,
        scratch_shapes=[pltpu.VMEM((8, D_MODEL), F32), pltpu.VMEM((8, D_MODEL), F32)],
        hosted=hosted)


def _lru_bwd(dhl, hl, a_all, gates, rest, cw, wr2, wi2, dproj, hosted=None):
    S = rest.shape[0]
    T = LRU_T
    nc = S // T

    def body(dh_ref, h_ref, hp_ref, a_ref, uc_ref, r_ref, ig_ref, mult_ref, u_ref, up_ref, cw_ref, wr_ref,
             wi_ref, _alias, du_ref, acc_ref, gwr_ref, gwi_ref, gcar_ref, acar_ref, dcar_ref):
        step = pl.program_id(0)
        c = nc - 1 - step

        @pl.when(step == 0)
        def _():
            acc_ref[...] = jnp.zeros_like(acc_ref)
            gwr_ref[...] = jnp.zeros_like(gwr_ref)
            gwi_ref[...] = jnp.zeros_like(gwi_ref)
            gcar_ref[...] = jnp.zeros_like(gcar_ref)
            acar_ref[...] = jnp.zeros_like(acar_ref)
            dcar_ref[...] = jnp.zeros_like(dcar_ref)

        u = u_ref[...]
        u_prev = jnp.where(c > 0, up_ref[...], 0.0)
        h_prev8 = jnp.where(c > 0, hp_ref[...], 0.0)
        a = a_ref[...]
        h = h_ref[...]
        uc, r, ig, mult = uc_ref[...], r_ref[...], ig_ref[...], mult_ref[...]
        ucb = uc.astype(BF16)
        sp = _softplus_neg(cw_ref[7:8, :])
        a_next = _rows_after(a, acar_ref[...], 1)
        G, first = _linear_scan(a_next, dh_ref[...], gcar_ref[0:1, :], reverse=True)
        gcar_ref[...] = jnp.broadcast_to(first, (8, D_MODEL))
        acar_ref[...] = jnp.broadcast_to(a[0:1, :], (8, D_MODEL))
        d_a = G * _rows_before(h, h_prev8, 1)
        d_mult = G * (ig * uc)
        d_ig = G * (mult * uc)
        duc = G * (mult * ig)
        d_log_a = d_a * a - d_mult * (a * a) / mult
        d_r = d_log_a * (-LRU_C * sp)
        d_sp = jnp.sum(d_log_a * (-LRU_C * r), axis=0, keepdims=True)
        dpre_r = d_r * r * (1.0 - r)
        dpre_i = d_ig * ig * (1.0 - ig)
        dprb = dpre_r.astype(BF16)
        dpib = dpre_i.astype(BF16)
        back = []
        for j in range(8):
            sl = slice(j * 128, (j + 1) * 128)
            back.append(_dot_nt(dprb[:, sl], wr_ref[j]) + _dot_nt(dpib[:, sl], wi_ref[j]))
            gwr_ref[j] += _dot_tn(ucb[:, sl], dprb[:, sl])
            gwi_ref[j] += _dot_tn(ucb[:, sl], dpib[:, sl])
        duc = duc + jnp.concatenate(back, axis=1)
        du = cw_ref[0:1, :] * duc
        duc_next8 = dcar_ref[...]
        acc_ref[0:1, :] += jnp.sum(duc * u, axis=0, keepdims=True)
        for j in range(1, CONV_WIDTH):
            du = du + cw_ref[j:j + 1, :] * _rows_after(duc, duc_next8, j)
            acc_ref[j:j + 1, :] += jnp.sum(duc * _rows_before(u, u_prev, j), axis=0, keepdims=True)
        dcar_ref[...] = duc[0:8, :]
        acc_ref[4:5, :] += jnp.sum(duc, axis=0, keepdims=True)
        acc_ref[5:6, :] += jnp.sum(dpre_r, axis=0, keepdims=True)
        acc_ref[6:7, :] += jnp.sum(dpre_i, axis=0, keepdims=True)
        acc_ref[7:8, :] += d_sp
        du_ref[0] = du[:, 0:ATT_WIDTH].astype(BF16)
        du_ref[1] = du[:, ATT_WIDTH:].astype(BF16)

    def rev(step):
        return nc - 1 - step

    def prev8(step):
        return jnp.maximum(rev(step) * (T // 8) - 1, 0)

    row_spec = pl.BlockSpec((T, D_MODEL), lambda s: (rev(s), 0))
    prev_spec = pl.BlockSpec((8, D_MODEL), lambda s: (prev8(s), 0))
    vec_spec = pl.BlockSpec((8, D_MODEL), lambda s: (0, 0))
    w_spec = pl.BlockSpec((8, 128, 128), lambda s: (0, 0, 0))
    return _call(
        body, (dhl, hl, hl, a_all, *gates, rest, rest, cw, wr2, wi2, dproj), name="lru_bwd", grid=(nc,),
        in_specs=[row_spec, row_spec, prev_spec, row_spec, row_spec, row_spec, row_spec, row_spec,
                  row_spec, prev_spec, vec_spec, w_spec, w_spec, ANY_SPEC],
        out_specs=(pl.BlockSpec((2, T, ATT_WIDTH), lambda s: (4, rev(s), 0)), vec_spec, w_spec, w_spec),
        out_shape=(jax.ShapeDtypeStruct(dproj.shape, BF16), jax.ShapeDtypeStruct((8, D_MODEL), F32),
                   jax.ShapeDtypeStruct((8, 128, 128), F32), jax.ShapeDtypeStruct((8, 128, 128), F32)),
        scratch_shapes=[pltpu.VMEM((8, D_MODEL), F32), pltpu.VMEM((8, D_MODEL), F32),
                        pltpu.VMEM((8, D_MODEL), F32)],
        aliases={13: 0}, hosted=hosted)


OUT_TM = 256


def _out_fwd(a_gated, hl, rest, x, vec, w_pa, w_pb, w_o, target=None, hosted=None):
    S = x.shape[0]
    tm = OUT_TM
    with_loss = target is not None

    def body(*refs):
        (ag_ref, hl_ref, gl_ref, ma_ref, mb_ref, x_ref, v_ref, wpa_ref, wpb_ref, wo_ref), refs = refs[:10], refs[10:]
        if with_loss:
            t_ref, refs = refs[0], refs[1:]
        xn_ref, mix_ref, ya_ref, yb_ref, bg_ref = refs[:5]
        bg = (hl_ref[...] * _silu(gl_ref[...])).astype(BF16)
        ya = _dot(ag_ref[...], wpa_ref[...])
        yb = _dot(bg, wpb_ref[...])
        mix = (jax.nn.sigmoid(ma_ref[...]) * ya + jax.nn.sigmoid(mb_ref[...]) * yb).astype(BF16)
        out = _dot(mix, wo_ref[...])
        rstd = lax.rsqrt(jnp.mean(out * out, axis=-1, keepdims=True) + NORM_EPS)
        x_new = x_ref[...] + v_ref[1:2, :] * ((out * rstd) * v_ref[0:1, :])
        if with_loss:
            acc_ref = refs[5]

            @pl.when(pl.program_id(0) == 0)
            def _():
                acc_ref[...] = jnp.zeros_like(acc_ref)

            err = x_new - t_ref[...]
            xn_ref[...] = err * (1.0 / D_MODEL)
            part = jnp.sum(jnp.sum(err * err, axis=1, keepdims=True), axis=0, keepdims=True)
            acc_ref[...] += jnp.broadcast_to(part, acc_ref.shape)
        else:
            xn_ref[...] = x_new
        mix_ref[...] = mix
        ya_ref[...] = ya.astype(BF16)
        yb_ref[...] = yb.astype(BF16)
        bg_ref[...] = bg

    def col(j):
        return pl.BlockSpec((tm, D_MODEL), lambda i: (i, j))

    def whole(shape):
        return pl.BlockSpec(shape, lambda i: (0, 0))

    row = col(0)
    bf = jax.ShapeDtypeStruct((S, D_MODEL), BF16)
    loss_in = ([target], [row]) if with_loss else ([], [])
    loss_out = ([jax.ShapeDtypeStruct((8, 128), F32)], [whole((8, 128))]) if with_loss else ([], [])
    return _call(
        body, (a_gated, hl, rest, rest, rest, x, vec, w_pa, w_pb, w_o, *loss_in[0]), name="out_fwd",
        grid=(S // tm,),
        in_specs=[pl.BlockSpec((tm, ATT_WIDTH), lambda i: (i, 0)), row, col(1), col(2), col(3), row,
                  whole((8, D_MODEL)), whole((ATT_WIDTH, D_MODEL)), whole((D_MODEL, D_MODEL)),
                  whole((D_MODEL, D_MODEL)), *loss_in[1]],
        out_specs=(row, row, row, row, row, *loss_out[1]),
        out_shape=(jax.ShapeDtypeStruct((S, D_MODEL), F32), bf, bf, bf, bf, *loss_out[0]),
        hosted=hosted)


def _out_bwd(dxn, mix, ya, yb, hl, rest, o_att, vec, w_pa, w_pb, w_o, hosted=None):
    S = dxn.shape[0]
    tm = OUT_TM
    forth = [_perm_matrix(r, tm) for r in DILATIONS[1:]]

    def body(dxn_ref, mix_ref, ya_ref, yb_ref, hl_ref, gl_ref, ma_ref, mb_ref, ga_ref, o_ref, v_ref,
             wpa_ref, wpb_ref, wo_ref, f1_ref, f2_ref,
             dout_ref, dya_ref, dyb_ref, dhl_ref, dp_ref, acc_ref,
             do0_ref, do1_ref, do2_ref, dv0_ref, dv1_ref, dv2_ref):
        i = pl.program_id(0)

        @pl.when(i == 0)
        def _():
            acc_ref[...] = jnp.zeros_like(acc_ref)

        g_post = v_ref[0:1, :]
        gate = v_ref[1:2, :]
        dxn_v = dxn_ref[...]
        out = _dot(mix_ref[...], wo_ref[...])
        rstd = lax.rsqrt(jnp.mean(out * out, axis=-1, keepdims=True) + NORM_EPS)
        nrm = out * rstd
        acc_ref[0:1, :] += jnp.sum(dxn_v * (nrm * g_post), axis=0, keepdims=True)
        acc_ref[1:2, :] += jnp.sum(dxn_v * gate * nrm, axis=0, keepdims=True)
        dn = dxn_v * (gate * g_post)
        dout = (rstd * (dn - nrm * jnp.mean(dn * nrm, axis=-1, keepdims=True))).astype(BF16)
        dout_ref[...] = dout
        dmix = _dot_nt(dout, wo_ref[...])
        sa = jax.nn.sigmoid(ma_ref[...])
        sb = jax.nn.sigmoid(mb_ref[...])
        dya = (dmix * sa).astype(BF16)
        dyb = (dmix * sb).astype(BF16)
        dya_ref[...] = dya
        dyb_ref[...] = dyb
        dma = dmix * ya_ref[...].astype(F32) * (sa * (1.0 - sa))
        dmb = dmix * yb_ref[...].astype(F32) * (sb * (1.0 - sb))
        d_ag = _dot_nt(dya, wpa_ref[...])
        d_bg = _dot_nt(dyb, wpb_ref[...])
        gl = gl_ref[...]
        hl_v = hl_ref[...]
        dhl_ref[...] = d_bg * _silu(gl)
        dgl = d_bg * hl_v * _dsilu(gl)
        ga = ga_ref[...]
        o = o_ref[...]
        do = d_ag * _silu(ga)
        dga = d_ag * o * _dsilu(ga)
        lane = lax.broadcasted_iota(jnp.int32, (tm, 128), 1)
        dvec = jnp.zeros((tm, 128), F32)
        for h in range(HEADS):
            sl = slice(h * HEAD_DIM, (h + 1) * HEAD_DIM)
            dvec = jnp.where(lane == h, jnp.sum(do[:, sl] * o[:, sl], axis=1, keepdims=True), dvec)
        do_b = do.astype(BF16)
        do0_ref[0] = do_b
        dv0_ref[0] = dvec
        for do_ref, dv_ref, f_ref in ((do1_ref, dv1_ref, f1_ref), (do2_ref, dv2_ref, f2_ref)):
            _store_residues(do_ref, _dot(f_ref[...], do_b).astype(BF16))
            _store_residues(dv_ref, _permute_f32(f_ref[...], dvec, 3))
        dp_ref[0] = dgl[:, 0:ATT_WIDTH].astype(BF16)
        dp_ref[1] = dgl[:, ATT_WIDTH:].astype(BF16)
        dp_ref[2] = dma[:, 0:ATT_WIDTH].astype(BF16)
        dp_ref[3] = dma[:, ATT_WIDTH:].astype(BF16)
        dp_ref[4] = dmb[:, 0:ATT_WIDTH].astype(BF16)
        dp_ref[5] = dmb[:, ATT_WIDTH:].astype(BF16)
        dp_ref[6] = dga.astype(BF16)

    def col(j):
        return pl.BlockSpec((tm, D_MODEL), lambda i: (i, j))

    def whole(shape):
        return pl.BlockSpec(shape, lambda i: (0, 0))

    def res_spec(r, width):
        return pl.BlockSpec((r, tm // r, width), lambda i: (0, i, 0))

    row = col(0)
    att = pl.BlockSpec((tm, ATT_WIDTH), lambda i: (i, 0))
    bf = jax.ShapeDtypeStruct((S, D_MODEL), BF16)
    vmem = pl.BlockSpec(memory_space=pltpu.VMEM)
    return _call(
        body, (dxn, mix, ya, yb, hl, rest, rest, rest, rest, o_att, vec, w_pa, w_pb, w_o, *forth),
        name="out_bwd", grid=(S // tm,),
        in_specs=[row, row, row, row, row, col(1), col(2), col(3),
                  pl.BlockSpec((tm, ATT_WIDTH), lambda i: (i, 8)), att,
                  whole((8, D_MODEL)), whole((ATT_WIDTH, D_MODEL)), whole((D_MODEL, D_MODEL)),
                  whole((D_MODEL, D_MODEL)), vmem, vmem],
        out_specs=(row, row, row, row,
                   pl.BlockSpec((7, tm, ATT_WIDTH), lambda i: (0, i, 0)), whole((8, D_MODEL)),
                   *[res_spec(r, ATT_WIDTH) for r in DILATIONS], *[res_spec(r, 128) for r in DILATIONS]),
        out_shape=(bf, bf, bf, jax.ShapeDtypeStruct((S, D_MODEL), F32),
                   jax.ShapeDtypeStruct((DREST_CHUNKS, S, ATT_WIDTH), BF16),
                   jax.ShapeDtypeStruct((8, D_MODEL), F32),
                   *[jax.ShapeDtypeStruct((r, S // r, ATT_WIDTH), BF16) for r in DILATIONS],
                   *[jax.ShapeDtypeStruct((r, S // r, 128), F32) for r in DILATIONS]),
        hosted=hosted)


def _adamw(gsrcs, w, m, v, n_stack, name, rows=None):
    n_layers, R, C = w.shape
    assert len(gsrcs) == n_layers
    budget = 96 * 1024
    tr = rows
    if tr is None:
        tr = R
        while tr * C > budget and tr % 16 == 0:
            tr //= 2
    assert R % tr == 0 and (tr % 8 == 0 or tr == R)
    c1 = 1.0 - ADAM_B1 ** ADAM_STEP
    c2 = 1.0 - ADAM_B2 ** ADAM_STEP

    def body(*refs):
        g_refs = refs[:n_layers]
        w_ref, m_ref, v_ref, go_ref, d_ref, mo_ref, vo_ref = refs[n_layers:]

        def update(g_ref):
            if n_stack:
                g = g_ref[0].astype(F32)
                for s in range(1, n_stack):
                    g = g + g_ref[s].astype(F32)
            else:
                g = g_ref[...]
            m_new = ADAM_B1 * m_ref[...] + (1.0 - ADAM_B1) * g
            v_new = ADAM_B2 * v_ref[...] + (1.0 - ADAM_B2) * (g * g)
            m_hat = m_new / c1
            v_hat = v_new / c2
            go_ref[...] = g
            d_ref[...] = -ADAM_LR * (m_hat / (jnp.sqrt(v_hat) + ADAM_EPS) + ADAM_WD * w_ref[...])
            mo_ref[...] = m_new
            vo_ref[...] = v_new

        for layer in range(n_layers):
            pl.when(pl.program_id(0) == layer)(functools.partial(update, g_refs[layer]))

    def g_spec(layer):
        def rows_of(l, i):
            return jnp.where(l == layer, i, 0)
        if n_stack:
            return pl.BlockSpec((n_stack, tr, C), lambda l, i: (0, rows_of(l, i), 0))
        return pl.BlockSpec((tr, C), lambda l, i: (rows_of(l, i), 0))

    blk = pl.BlockSpec((None, tr, C), lambda l, i: (l, i, 0))
    shp = jax.ShapeDtypeStruct((n_layers, R, C), F32)
    return pl.pallas_call(
        body, name=name, grid=(n_layers, R // tr),
        in_specs=[g_spec(layer) for layer in range(n_layers)] + [blk, blk, blk],
        out_specs=(blk, blk, blk, blk), out_shape=(shp, shp, shp, shp),
        compiler_params=_params(("arbitrary", "arbitrary")))(*gsrcs, w, m, v)


def _pair_blocks(w):
    w = w.reshape(8, 2, 64, 64)
    z = jnp.zeros((8, 64, 64), w.dtype)
    top = jnp.concatenate([w[:, 0], z], axis=2)
    bot = jnp.concatenate([z, w[:, 1]], axis=2)
    return jnp.concatenate([top, bot], axis=1).astype(BF16)


def _unpair_blocks(g):
    return jnp.stack([g[:, :64, :64], g[:, 64:, 64:]], axis=1).reshape(16, 64, 64)


def _mats_full(w_pa_g, w_pb_g, w_o_g):
    return dict(w_pa=jnp.transpose(w_pa_g, (1, 0, 2)).reshape(ATT_WIDTH, D_MODEL),
                w_pb=w_pb_g.reshape(D_MODEL, D_MODEL), w_o=w_o_g.reshape(D_MODEL, D_MODEL))


def _layer_params(layer, w_in_g, mats_g, conv_w_full, conv_b, b_rg, b_ig, lru_lambda, w_rg, w_ig, g_pre,
                  g_post):
    cw = jnp.concatenate([conv_w_full, conv_b[None], b_rg[None], b_ig[None], lru_lambda[None]], axis=0)
    p = dict(w_int=_weights_full(w_in_g), cw=cw, wr2=_pair_blocks(w_rg), wi2=_pair_blocks(w_ig),
             g_pre=g_pre, g_post=g_post, lru_lambda=lru_lambda)
    if mats_g is not None:
        p.update(_mats_full(*mats_g))
    return p


def _layer_fwd(x, mod, p, host_proj=None, host_lru=None, host_out=None, own_mats=None, target=None):
    zeros = jnp.zeros((5, D_MODEL), F32)
    vec_pre = jnp.concatenate([p["g_pre"][None], mod[1:2], mod[0:1], zeros], axis=0)
    (ht0, ht1, ht2, q0, q1, q2, rest), got_proj = _norm_proj_fwd(x, vec_pre, p["w_int"], hosted=host_proj)
    h_t = [ht0, ht1, ht2]
    hosts = [None, None, None]
    if own_mats is not None:
        hosts[0], hosts[1] = (list(own_mats[:2]), ["ag"] * 2), ([own_mats[2]], ["ag"])
    qkv, outs, stats, got_attn = [q0, q1, q2], [], [], []
    for g, r in enumerate(DILATIONS):
        (o_g, st_g), got = _attn_fwd(qkv[g], g, r, hosted=hosts[g])
        outs.append(o_g)
        stats.append(st_g)
        got_attn += got
    if own_mats is not None:
        p.update(_mats_full(*got_attn))
    o_att, a_gated, *lse = _attn_combine(outs, stats, rest)
    (hl, a_dec, *gates), got_lru = _lru_fwd(rest, p["cw"], p["wr2"], p["wi2"], hosted=host_lru)
    vec_post = jnp.concatenate([p["g_post"][None], mod[2:3], jnp.zeros((6, D_MODEL), F32)], axis=0)
    (x_new, mix, ya, yb, b_gated, *loss_acc), got_out = _out_fwd(
        a_gated, hl, rest, x, vec_post, p["w_pa"], p["w_pb"], p["w_o"], target=target, hosted=host_out)
    saved = dict(x=x, h_t=h_t, qkv=qkv, rest=rest, o_att=o_att, a_gated=a_gated, lse=lse, hl=hl, a_dec=a_dec,
                 gates=gates, mix=mix, ya=ya, yb=yb, b_gated=b_gated, vec_pre=vec_pre, vec_post=vec_post)
    return (x_new, *loss_acc), saved, got_proj, got_lru, got_out


def _mats_sources(g_pa, g_pb, g_o):
    return [jnp.transpose(g_pa.astype(BF16).reshape(ATT_WIDTH, N_DEV, 128), (1, 0, 2)),
            g_pb.astype(BF16).reshape(N_DEV, 128, D_MODEL), g_o.astype(BF16).reshape(N_DEV, 128, D_MODEL)]


LRU_SMALL = ("conv_b", "w_rg", "b_rg", "w_ig", "b_ig", "lru_lambda")
MOD_SMALL = ("b_mod", "g_pre", "g_post")


def _flat_rows(arrs):
    return jnp.concatenate([a.reshape(-1) for a in arrs]).reshape(-1, 128)


def _layer_bwd(dxn, sv, p, upper_small=None):
    a2a3 = ["a2a"] * 3
    (dout, dya, dyb, dhl, drest, acc_out, *do_dvec), _ = _out_bwd(
        dxn, sv["mix"], sv["ya"], sv["yb"], sv["hl"], sv["rest"], sv["o_att"], sv["vec_post"],
        p["w_pa"], p["w_pb"], p["w_o"])
    mats = _mats_sources(_wgrad_tn(sv["a_gated"], dya, "wgrad_pa"), _wgrad_tn(sv["b_gated"], dyb, "wgrad_pb"),
                         _wgrad_tn(sv["mix"], dout, "wgrad_o"))
    (drest, acc_lru, gwr2, gwi2), _ = _lru_bwd(
        dhl, sv["hl"], sv["a_dec"], sv["gates"], sv["rest"], p["cw"], p["wr2"], p["wi2"], drest)
    small = dict(
        conv_w=acc_lru[0:4], conv_b=acc_lru[4], b_rg=acc_lru[5], b_ig=acc_lru[6],
        lru_lambda=acc_lru[7] * (-jax.nn.sigmoid(-p["lru_lambda"])),
        w_rg=_unpair_blocks(gwr2), w_ig=_unpair_blocks(gwi2))
    hosts = [(mats, a2a3), None, None]
    if upper_small is not None:
        for half, names in enumerate((LRU_SMALL[:3], LRU_SMALL[3:] + ("conv_w",))):
            vec = _flat_rows([jnp.stack([small[n], upper_small[n]]) for n in names])
            hosts[1 + half] = ([vec], ["ag"])
    dqkv, got_attn = [], []
    for g, r in enumerate(DILATIONS):
        d_g, got = _attn_bwd(sv["qkv"][g], do_dvec[g], sv["lse"][g], do_dvec[3 + g], g, r, hosted=hosts[g])
        dqkv.append(d_g)
        got_attn.append(got)
    shards = None
    for g, r in enumerate(DILATIONS):
        shards = _proj_wgrad_group(sv["h_t"][g], dqkv[g], g, r, shards, f"proj_wgrad_g{g}")
    shards = _proj_wgrad_part(sv["h_t"][0], drest, 2, 4, QKV_CHUNKS, shards, "proj_wgrad_u")
    shards = _proj_wgrad_part(sv["h_t"][0], drest, 7, 0, QKV_CHUNKS + 2, shards, "proj_wgrad_rest")
    (dx, acc_pre), got_in = _proj_dgrad_norm_bwd(
        dqkv, drest, p["w_int"], sv["x"], dxn, sv["vec_pre"], hosted=([shards], ["a2a"]))
    small.update(dmod=jnp.concatenate([acc_pre[0], acc_pre[1], acc_out[0]]), g_pre=acc_pre[2],
                 g_post=acc_out[1])
    return dx, small, dict(mats=got_attn[0], w_in=got_in[0], lru_small=got_attn[1] + got_attn[2])


SMALL_NAMES = ("b_mod", "g_pre", "conv_b", "w_rg", "b_rg", "w_ig", "b_ig", "lru_lambda", "g_post")


def kernel(x, c, w_mod, b_mod, g_pre, w_in, conv_w, conv_b, w_rg, b_rg, w_ig, b_ig, lru_lambda, w_pa, w_pb, w_o, g_post, loss_target, m_w_mod, m_b_mod, m_g_pre, m_w_in, m_conv_w, m_conv_b, m_w_rg, m_b_rg, m_w_ig, m_b_ig, m_lru_lambda, m_w_pa, m_w_pb, m_w_o, m_g_post, v_w_mod, v_b_mod, v_g_pre, v_w_in, v_conv_w, v_conv_b, v_w_rg, v_b_rg, v_w_ig, v_b_ig, v_lru_lambda, v_w_pa, v_w_pb, v_w_o, v_g_post):
    W = dict(w_mod=w_mod, b_mod=b_mod, g_pre=g_pre, w_in=w_in, conv_w=conv_w, conv_b=conv_b, w_rg=w_rg,
             b_rg=b_rg, w_ig=w_ig, b_ig=b_ig, lru_lambda=lru_lambda, w_pa=w_pa, w_pb=w_pb, w_o=w_o,
             g_post=g_post)
    M = dict(w_mod=m_w_mod, b_mod=m_b_mod, g_pre=m_g_pre, w_in=m_w_in, conv_w=m_conv_w, conv_b=m_conv_b,
             w_rg=m_w_rg, b_rg=m_b_rg, w_ig=m_w_ig, b_ig=m_b_ig, lru_lambda=m_lru_lambda, w_pa=m_w_pa,
             w_pb=m_w_pb, w_o=m_w_o, g_post=m_g_post)
    V = dict(w_mod=v_w_mod, b_mod=v_b_mod, g_pre=v_g_pre, w_in=v_w_in, conv_w=v_conv_w, conv_b=v_conv_b,
             w_rg=v_w_rg, b_rg=v_b_rg, w_ig=v_w_ig, b_ig=v_b_ig, lru_lambda=v_lru_lambda, w_pa=v_w_pa,
             w_pb=v_w_pb, w_o=v_w_o, g_post=v_g_post)
    S = x.shape[1]
    me = 4 * lax.axis_index("x") + 2 * lax.axis_index("y") + lax.axis_index("c")
    n_mod = w_mod.shape[2]
    n_in = w_in.shape[2]
    n_conv = conv_w.shape[2]

    c_rows = jnp.broadcast_to(c, (8, D_MODEL))
    w_in_b, w_pa_b, w_pb_b, w_o_b = (t.astype(BF16) for t in (w_in, w_pa, w_pb, w_o))
    g_c, g_win0, g_cw = _gather_two_level([c_rows, w_in_b[0], conv_w], "gather_weights")
    c_all = g_c[:, 0, :]
    c_pad = jnp.concatenate([c_all, jnp.zeros((8, D_MODEL), F32)], axis=0)
    conv_w_full = jnp.transpose(g_cw, (1, 2, 0, 3)).reshape(2, CONV_WIDTH, D_MODEL)

    mod_cols = _mod_fwd(c_pad, w_mod)
    mod_src = jnp.transpose(mod_cols[:, :8], (1, 0, 2))
    mod_src = jnp.concatenate([mod_src, jnp.zeros((8, 6, n_mod), F32)], axis=1)
    (mod_got,) = _exchange([mod_src], ["a2a"], "scatter_mod")
    mod = jnp.transpose(mod_got[:, :2], (1, 0, 2)).reshape(2, 3 * D_MODEL) + b_mod

    def layer_params(layer, w_in_g, mats_g):
        return _layer_params(layer, w_in_g, mats_g, conv_w_full[layer], conv_b[layer], b_rg[layer],
                             b_ig[layer], lru_lambda[layer], w_rg[layer], w_ig[layer], g_pre[layer],
                             g_post[layer])

    layers = [layer_params(0, g_win0, None), None]
    (act,), sv0, (g_win1,), _, mats1_g = _layer_fwd(
        x[0], mod[0].reshape(3, D_MODEL), layers[0], own_mats=(w_pa_b[0], w_pb_b[0], w_o_b[0]),
        host_proj=([w_in_b[1]], ["ag"]), host_out=([w_pa_b[1], w_pb_b[1], w_o_b[1]], ["ag"] * 3))
    layers[1] = layer_params(1, g_win1, mats1_g)
    (dy, loss_acc), sv1, _, _, _ = _layer_fwd(act, mod[1].reshape(3, D_MODEL), layers[1],
                                              target=loss_target[0])

    dy, small1, got1 = _layer_bwd(dy, sv1, layers[1])
    dy, small0, got0 = _layer_bwd(dy, sv0, layers[0], upper_small=small1)
    grad_x = dy[None]
    r_pa, r_pb, r_o = ([got0["mats"][k], got1["mats"][k]] for k in range(3))
    r_in = [got0["w_in"], got1["w_in"]]
    grads = [small0, small1]

    def stack2(name):
        return jnp.stack([grads[0][name], grads[1][name]], axis=0)

    dmod = stack2("dmod")
    dmod_src = jnp.transpose(dmod.reshape(2, N_DEV, n_mod), (1, 0, 2))
    dmod_src = jnp.concatenate([dmod_src, jnp.zeros((8, 6, n_mod), F32)], axis=1)
    mod_vec = _flat_rows([dmod, stack2("g_pre"), stack2("g_post"), loss_acc])
    r_dmod, r_mod_small = _exchange([dmod_src, mod_vec], ["a2a", "ag"], "exchange_grads")
    mod_rows = mod_vec.shape[0] - loss_acc.shape[0]
    loss = 0.5 * jnp.sum(r_mod_small[:, mod_rows, 0]) / D_MODEL
    r_lru_small = got0["lru_small"]

    res = {}
    res["w_in"] = _adamw(r_in, w_in, m_w_in, v_w_in, 8, "adamw_w_in")
    res["w_pa"] = _adamw(r_pa, w_pa, m_w_pa, v_w_pa, 8, "adamw_w_pa")
    res["w_pb"] = _adamw(r_pb, w_pb, m_w_pb, v_w_pb, 8, "adamw_w_pb")
    res["w_o"] = _adamw(r_o, w_o, m_w_o, v_w_o, 8, "adamw_w_o")
    dmod_all = jnp.transpose(r_dmod[:, :2], (1, 0, 2))
    gw_mod = _mod_wgrad(jnp.transpose(c_all), dmod_all)
    res["w_mod"] = _adamw([gw_mod[0], gw_mod[1]], w_mod, m_w_mod, v_w_mod, 0, "adamw_w_mod")
    for names_rep, stack, tag in ((LRU_SMALL[:3], r_lru_small[0], "lru_a"), (LRU_SMALL[3:], r_lru_small[1], "lru_b"),
                                  (MOD_SMALL, r_mod_small, "mod")):
        rows = sum(W[n].size for n in names_rep) // 128
        rep = _adamw([stack], *(_flat_rows([src[n] for n in names_rep])[None] for src in (W, M, V)), 8,
                     f"adamw_small_{tag}", rows=rows // 2 if rows % 16 == 0 else rows)
        off = 0
        for name in names_rep:
            size = W[name].size
            res[name] = tuple(t.reshape(-1)[off:off + size].reshape(W[name].shape) for t in rep)
            off += size
    lru_rows = sum(W[n].size for n in LRU_SMALL[3:]) // 128
    conv_stack = r_lru_small[1][:, lru_rows:].reshape(8, 2, CONV_WIDTH, D_MODEL)
    conv_stack = lax.dynamic_slice_in_dim(conv_stack, me * n_conv, n_conv, axis=3).reshape(8, 8, n_conv)
    res["conv_w"] = _adamw([conv_stack], conv_w.reshape(1, 8, n_conv), m_conv_w.reshape(1, 8, n_conv),
                           v_conv_w.reshape(1, 8, n_conv), 8, "adamw_conv_w")

    names = ("w_mod", "b_mod", "g_pre", "w_in", "conv_w", "conv_b", "w_rg", "b_rg", "w_ig", "b_ig",
             "lru_lambda", "w_pa", "w_pb", "w_o", "g_post")
    outs = [loss, grad_x]
    for k in range(4):
        outs.extend(res[n][k].reshape(W[n].shape) for n in names)
    return tuple(outs)
```

```python
import functools

import jax
import jax.numpy as jnp
import numpy as np
from jax import lax
from jax.experimental import pallas as pl
from jax.experimental.pallas import tpu as pltpu

F32 = jnp.float32
BF16 = jnp.bfloat16

N_DEV = 8
D_MODEL = 1024
HEAD_DIM = 128
HEADS = 4
ATT_WIDTH = HEADS * HEAD_DIM
DILATIONS = (1, 4, 16)
BAND = 128
N_CHUNKS = 18
QKV_CHUNKS = 9
CONV_WIDTH = 4
LRU_C = 8.0
NORM_EPS = 1e-6
NEG_INF = -1e30
ADAM_LR = 0.001
ADAM_B1 = 0.9
ADAM_B2 = 0.999
ADAM_EPS = 1e-08
ADAM_WD = 0.01
ADAM_STEP = 10
CHUNK_PERM = (0, 3, 6, 1, 4, 7, 2, 5, 8, 10, 11, 12, 13, 14, 15, 16, 17, 9)
DREST_CHUNKS = 10
VMEM_LIMIT = 56 * 1024 * 1024
MESH_AXES = ("x", "y", "c")


def _params(sem=None):
    return pltpu.CompilerParams(dimension_semantics=sem, vmem_limit_bytes=VMEM_LIMIT)


def _silu(x):
    return x * jax.nn.sigmoid(x)


def _dsilu(x):
    s = jax.nn.sigmoid(x)
    return s * (1.0 + x * (1.0 - s))


def _neg_expm1(x):
    series = -x * (1.0 + x * (0.5 + x * (1.0 / 6.0 + x * (1.0 / 24.0))))
    return jnp.where(x > -0.05, series, 1.0 - jnp.exp(x))


def _softplus_neg(lam):
    z = jnp.exp(-jnp.abs(lam))
    small = z * (1.0 - z * (0.5 - z * (1.0 / 3.0 - z * 0.25)))
    log1p_z = jnp.where(z < 1e-2, small, jnp.log(1.0 + z))
    return jnp.maximum(-lam, 0.0) + log1p_z


def _dot(a, b):
    return jnp.dot(a, b, preferred_element_type=F32)


def _dot_nt(a, b):
    return lax.dot_general(a, b, (((1,), (1,)), ((), ())), preferred_element_type=F32)


def _dot_tn(a, b):
    return lax.dot_general(a, b, (((0,), (0,)), ((), ())), preferred_element_type=F32)


def _perm_matrix(r, rows, transpose=False):
    n = rows // r
    p = np.zeros((rows, rows), np.float32)
    dst = np.arange(rows)
    p[dst, (dst % n) * r + dst // n] = 1.0
    return jnp.asarray(p.T if transpose else p, dtype=BF16)


def _permute_f32(p, x, pieces):
    part = x.astype(BF16)
    acc = _dot(p, part)
    for _ in range(pieces - 1):
        x = x - part.astype(F32)
        part = x.astype(BF16)
        acc = acc + _dot(p, part)
    return acc


def _rows_of_residues(ref, lead=()):
    r = ref.shape[len(lead)]
    if r == 1:
        return ref[lead + (0,)]
    return jnp.concatenate([ref[lead + (rho,)] for rho in range(r)], axis=0)


def _store_residues(ref, x, lead=()):
    r = ref.shape[len(lead)]
    n = x.shape[0] // r
    for rho in range(r):
        ref[lead + (rho,)] = x[rho * n:(rho + 1) * n]


ANY_SPEC = pl.BlockSpec(memory_space=pl.ANY)


def _exchange_shapes(arrs, modes):
    return [jax.ShapeDtypeStruct(((N_DEV,) + a.shape) if mode == "ag" else a.shape, a.dtype)
            for a, mode in zip(arrs, modes)]


def _exchange_sems(n):
    return [pltpu.SemaphoreType.DMA((7 * n,)), pltpu.SemaphoreType.DMA((7 * n,)),
            pltpu.SemaphoreType.DMA((n,))]


def _exchange_copies(ins, outs, modes, sems):
    n = len(ins)
    send_sems, recv_sems, local_sems = sems
    x, y, c = lax.axis_index("x"), lax.axis_index("y"), lax.axis_index("c")
    me = 4 * x + 2 * y + c

    def src_for(a, dev):
        return ins[a] if modes[a] == "ag" else ins[a].at[dev]

    local = [pltpu.make_async_copy(src_for(a, me), outs[a].at[me], local_sems.at[a]) for a in range(n)]
    sends, arrivals = [], []
    for k in range(1, N_DEV):
        px = 1 - x if (k >> 2) & 1 else x
        py = 1 - y if (k >> 1) & 1 else y
        pc = 1 - c if k & 1 else c
        peer = 4 * px + 2 * py + pc
        for a in range(n):
            s = (k - 1) * n + a
            for dst, group in ((me, sends), (peer, arrivals)):
                group.append(pltpu.make_async_remote_copy(
                    src_ref=src_for(a, peer), dst_ref=outs[a].at[dst],
                    send_sem=send_sems.at[s], recv_sem=recv_sems.at[s],
                    device_id=(px, py, pc), device_id_type=pl.DeviceIdType.MESH))
    return local, sends, arrivals


def _exchange_start(copies):
    local, sends, _ = copies
    for cp in local + sends:
        cp.start()


def _exchange_wait(copies):
    local, sends, arrivals = copies
    for cp in arrivals:
        cp.wait_recv()
    for cp in sends:
        cp.wait_send()
    for cp in local:
        cp.wait()


def _exchange(arrs, modes, name):
    n = len(arrs)

    def body(*refs):
        copies = _exchange_copies(refs[:n], refs[n:2 * n], modes, refs[2 * n:])
        _exchange_start(copies)
        _exchange_wait(copies)

    outs = pl.pallas_call(
        body, name=name, out_shape=tuple(_exchange_shapes(arrs, modes)),
        in_specs=[ANY_SPEC] * n, out_specs=tuple([ANY_SPEC] * n),
        scratch_shapes=_exchange_sems(n),
    )(*arrs)
    return list(outs)


def _gather_two_level(arrs, name):
    n = len(arrs)

    def body(*refs):
        ins, outs = refs[:n], refs[n:2 * n]
        send_sems, recv_sems, local_sems = refs[2 * n:]
        x, y, c = lax.axis_index("x"), lax.axis_index("y"), lax.axis_index("c")
        me, sibling = (x, y, c), (x, y, 1 - c)
        chips = [(1 - x, y), (x, 1 - y), (1 - x, 1 - y)]

        def copy(a, k, block, to, src=None):
            slot = 4 * block[0] + 2 * block[1] + block[2]
            return pltpu.make_async_remote_copy(
                src_ref=outs[a].at[slot] if src is None else src, dst_ref=outs[a].at[slot],
                send_sem=send_sems.at[7 * a + k], recv_sem=recv_sems.at[7 * a + k],
                device_id=to, device_id_type=pl.DeviceIdType.MESH)

        mine = [pltpu.make_async_copy(ins[a], outs[a].at[4 * x + 2 * y + c], local_sems.at[a])
                for a in range(n)]
        first = []
        for a in range(n):
            first.append(copy(a, 0, me, sibling, src=ins[a]))
            first += [copy(a, 1 + j, me, (*chip, c), src=ins[a]) for j, chip in enumerate(chips)]
        for cp in mine + first:
            cp.start()
        passed = []
        for j, chip in enumerate(chips):
            for a in range(n):
                copy(a, 1 + j, (*chip, c), me).wait_recv()
                passed.append(copy(a, 4 + j, (*chip, c), sibling))
                passed[-1].start()
        for a in range(n):
            copy(a, 0, sibling, me).wait_recv()
            for j, chip in enumerate(chips):
                copy(a, 4 + j, (*chip, 1 - c), me).wait_recv()
        for cp in first + passed:
            cp.wait_send()
        for cp in mine:
            cp.wait()

    outs = pl.pallas_call(
        body, name=name, out_shape=tuple(_exchange_shapes(arrs, ["ag"] * n)),
        in_specs=[ANY_SPEC] * n, out_specs=tuple([ANY_SPEC] * n),
        scratch_shapes=_exchange_sems(n),
    )(*arrs)
    return list(outs)


SHARD_COLS = N_CHUNKS * ATT_WIDTH // N_DEV


def _weights_full(shards):
    return jnp.transpose(shards, (1, 0, 2)).reshape(D_MODEL, N_CHUNKS * ATT_WIDTH)


def _wcols(w_ref, k):
    orig = CHUNK_PERM[k]
    return w_ref[:, orig * ATT_WIDTH:(orig + 1) * ATT_WIDTH]


def _store_shard_columns(stage_ref, shards_ref, sems, first_chunk):
    copies = []
    for t in range(stage_ref.shape[1] // ATT_WIDTH):
        col = ATT_WIDTH * CHUNK_PERM[first_chunk + t]
        end = col + ATT_WIDTH
        while col < end:
            dev = col // SHARD_COLS
            upto = min(end, (dev + 1) * SHARD_COLS)
            src = stage_ref.at[:, pl.ds(t * ATT_WIDTH + ATT_WIDTH - (end - col), upto - col)]
            dst = shards_ref.at[dev, :, pl.ds(col - dev * SHARD_COLS, upto - col)]
            copies.append(pltpu.make_async_copy(src, dst, sems.at[len(copies)]))
            col = upto
    for cp in copies:
        cp.start()
    for cp in copies:
        cp.wait()


SHARD_COPIES_PER_CHUNK = 2


def _call(body, args, *, name, grid, in_specs, out_specs, out_shape, scratch_shapes=(), aliases=None,
          hosted=None):
    sem = ("arbitrary",) * len(grid)
    if hosted is None:
        outs = pl.pallas_call(
            body, name=name, grid=grid, in_specs=in_specs, out_specs=tuple(out_specs),
            out_shape=tuple(out_shape), scratch_shapes=list(scratch_shapes),
            input_output_aliases=aliases or {}, compiler_params=_params(sem))(*args)
        return list(outs), []
    x_arrs, modes = hosted
    n_in, n_out, n_scr, nx = len(args), len(out_shape), len(scratch_shapes), len(x_arrs)

    def wrapped(*refs):
        ins, x_ins = refs[:n_in], refs[n_in:n_in + nx]
        outs = refs[n_in + nx:n_in + nx + n_out]
        x_outs = refs[n_in + nx + n_out:n_in + 2 * nx + n_out]
        scr = refs[n_in + 2 * nx + n_out:n_in + 2 * nx + n_out + n_scr]
        sems = refs[n_in + 2 * nx + n_out + n_scr:]
        first = pl.program_id(0) == 0
        last = pl.program_id(0) == grid[0] - 1
        for axis in range(1, len(grid)):
            first = jnp.logical_and(first, pl.program_id(axis) == 0)
            last = jnp.logical_and(last, pl.program_id(axis) == grid[axis] - 1)

        @pl.when(first)
        def _():
            _exchange_start(_exchange_copies(x_ins, x_outs, modes, sems))

        body(*ins, *outs, *scr)

        @pl.when(last)
        def _():
            _exchange_wait(_exchange_copies(x_ins, x_outs, modes, sems))

    outs = pl.pallas_call(
        wrapped, name=name, grid=grid, in_specs=list(in_specs) + [ANY_SPEC] * nx,
        out_specs=tuple(out_specs) + tuple([ANY_SPEC] * nx),
        out_shape=tuple(out_shape) + tuple(_exchange_shapes(x_arrs, modes)),
        scratch_shapes=list(scratch_shapes) + _exchange_sems(nx),
        input_output_aliases=aliases or {}, compiler_params=_params(sem))(*args, *x_arrs)
    return list(outs[:n_out]), list(outs[n_out:])


def _mod_fwd(c_pad, w_mod):
    def body(c_ref, w_ref, o_ref):
        sc = _silu(c_ref[...]).astype(BF16)
        for layer in range(2):
            o_ref[layer] = _dot(sc, w_ref[layer].astype(BF16))

    return pl.pallas_call(
        body, name="mod_fwd", out_shape=jax.ShapeDtypeStruct((2, 16, w_mod.shape[2]), F32),
        compiler_params=_params())(c_pad, w_mod)


def _mod_wgrad(c_t, dmod):
    n_cols = dmod.shape[2]

    def body(c_ref, d_ref, o_ref):
        sc = _silu(c_ref[...]).astype(BF16).astype(F32)
        for layer in range(2):
            dm = d_ref[layer].astype(BF16).astype(F32)
            acc = sc[:, 0:1] * dm[0:1, :]
            for b in range(1, N_DEV):
                acc = acc + sc[:, b:b + 1] * dm[b:b + 1, :]
            o_ref[layer] = acc

    return pl.pallas_call(
        body, name="mod_wgrad", out_shape=jax.ShapeDtypeStruct((2, D_MODEL, n_cols), F32),
        compiler_params=_params())(c_t, dmod)


GROUP_COLS = 3 * ATT_WIDTH
PROJ_TM = 256


def _norm_proj_fwd(x, vec, w_int, hosted=None):
    S = x.shape[0]
    tm = PROJ_TM
    perms = [_perm_matrix(r, tm) for r in DILATIONS[1:]]

    def body(x_ref, v_ref, w_ref, p1_ref, p2_ref, ht0_ref, ht1_ref, ht2_ref, q0_ref, q1_ref, q2_ref,
             rest_ref):
        xv = x_ref[...]
        rstd = lax.rsqrt(jnp.mean(xv * xv, axis=-1, keepdims=True) + NORM_EPS)
        hf = ((xv * rstd) * v_ref[0:1, :]) * (1.0 + v_ref[1:2, :]) + v_ref[2:3, :]
        h = hf.astype(BF16)
        for g, (q_ref, ht_ref, p_ref) in enumerate(((q0_ref, ht0_ref, None), (q1_ref, ht1_ref, p1_ref),
                                                    (q2_ref, ht2_ref, p2_ref))):
            rows_f = hf if p_ref is None else _dot(p_ref[...], h)
            ht_ref[...] = rows_f.T.astype(BF16)
            rows = rows_f.astype(BF16)
            n = tm // q_ref.shape[0]
            for t in range(3):
                res = _dot(rows, _wcols(w_ref, 3 * g + t)).astype(BF16)
                for rho in range(q_ref.shape[0]):
                    q_ref[rho, :, t * ATT_WIDTH:(t + 1) * ATT_WIDTH] = res[rho * n:(rho + 1) * n]
        split = (QKV_CHUNKS - 1) * ATT_WIDTH
        rest_ref[:, :split] = _dot(h, w_ref[:, (QKV_CHUNKS + 1) * ATT_WIDTH:])
        rest_ref[:, split:] = _dot(h, w_ref[:, QKV_CHUNKS * ATT_WIDTH:(QKV_CHUNKS + 1) * ATT_WIDTH])

    n_rest = QKV_CHUNKS * ATT_WIDTH
    whole = pl.BlockSpec(memory_space=pltpu.VMEM)
    ht_spec = pl.BlockSpec((D_MODEL, tm), lambda i: (0, i))
    ht_shape = jax.ShapeDtypeStruct((D_MODEL, S), BF16)
    return _call(
        body, (x, vec, w_int, *perms), name="norm_proj_fwd", grid=(S // tm,),
        in_specs=[pl.BlockSpec((tm, D_MODEL), lambda i: (i, 0)),
                  pl.BlockSpec((8, D_MODEL), lambda i: (0, 0)), whole, whole, whole],
        out_specs=(ht_spec, ht_spec, ht_spec,
                   *[pl.BlockSpec((r, tm // r, GROUP_COLS), lambda i: (0, i, 0)) for r in DILATIONS],
                   pl.BlockSpec((tm, n_rest), lambda i: (i, 0))),
        out_shape=(ht_shape, ht_shape, ht_shape,
                   *[jax.ShapeDtypeStruct((r, S // r, GROUP_COLS), BF16) for r in DILATIONS],
                   jax.ShapeDtypeStruct((S, n_rest), F32)),
        hosted=hosted)


def _proj_dgrad_norm_bwd(dqkv, drest, w_int, x, dxn, vec, hosted=None):
    S = x.shape[0]
    tm = PROJ_TM
    perms = [_perm_matrix(r, tm, transpose=True) for r in DILATIONS[1:]]

    def body(d0_ref, d1_ref, d2_ref, dr_ref, w_ref, p1_ref, p2_ref, x_ref, dxn_ref, v_ref, dx_ref, acc_ref):
        i = pl.program_id(0)

        @pl.when(i == 0)
        def _():
            acc_ref[...] = jnp.zeros_like(acc_ref)

        wcols = functools.partial(_wcols, w_ref)
        dhv = None
        for g, (d_ref, p_ref) in enumerate(((d0_ref, None), (d1_ref, p1_ref), (d2_ref, p2_ref))):
            for t in range(3):
                d = _rows_of_residues(d_ref, (t,))
                if p_ref is not None:
                    d = _dot(p_ref[...], d).astype(BF16)
                part = _dot_nt(d, wcols(3 * g + t))
                dhv = part if dhv is None else dhv + part
        for t in range(7):
            dhv = dhv + _dot_nt(dr_ref[t], wcols(11 + t))
        for t in range(2):
            dhv = dhv + _dot_nt(dr_ref[8 + t], wcols(9 + t))

        xv = x_ref[...]
        g = v_ref[0:1, :]
        sc1 = 1.0 + v_ref[1:2, :]
        rstd = lax.rsqrt(jnp.mean(xv * xv, axis=-1, keepdims=True) + NORM_EPS)
        xhat = xv * rstd
        acc_ref[0:1, :] += jnp.sum(dhv, axis=0, keepdims=True)
        acc_ref[1:2, :] += jnp.sum(dhv * (xhat * g), axis=0, keepdims=True)
        acc_ref[2:3, :] += jnp.sum(dhv * xhat * sc1, axis=0, keepdims=True)
        dxhat = dhv * (g * sc1)
        dx = rstd * (dxhat - xhat * jnp.mean(dxhat * xhat, axis=-1, keepdims=True))
        dx_ref[...] = dx + dxn_ref[...]

    row = pl.BlockSpec((tm, D_MODEL), lambda i: (i, 0))
    vec_spec = pl.BlockSpec((8, D_MODEL), lambda i: (0, 0))
    whole = pl.BlockSpec(memory_space=pltpu.VMEM)
    return _call(
        body, (*dqkv, drest, w_int, *perms, x, dxn, vec), name="proj_dgrad_norm_bwd", grid=(S // tm,),
        in_specs=[*[pl.BlockSpec((3, r, tm // r, ATT_WIDTH), lambda i: (0, 0, i, 0)) for r in DILATIONS],
                  pl.BlockSpec((DREST_CHUNKS, tm, ATT_WIDTH), lambda i: (0, i, 0)),
                  whole, whole, whole, row, row, vec_spec],
        out_specs=(row, vec_spec),
        out_shape=(jax.ShapeDtypeStruct((S, D_MODEL), F32), jax.ShapeDtypeStruct((8, D_MODEL), F32)),
        hosted=hosted)


WGRAD_TS = 512


def _proj_wgrad_call(h_t, d, d_spec, rows_of, n_chunks, first_chunk, shards, name):
    S = h_t.shape[1]
    n_steps = S // WGRAD_TS
    width = n_chunks * ATT_WIDTH

    def body(*refs):
        h_ref, d_ref = refs[:2]
        shards_ref, acc_ref, stage_ref, sems = refs[-4:]
        i = pl.program_id(0)

        @pl.when(i == 0)
        def _():
            acc_ref[...] = jnp.zeros_like(acc_ref)

        ht = h_ref[...]
        for t in range(n_chunks):
            acc_ref[:, t * ATT_WIDTH:(t + 1) * ATT_WIDTH] += _dot(ht, rows_of(d_ref, t))

        @pl.when(i == n_steps - 1)
        def _():
            stage_ref[...] = acc_ref[...].astype(BF16)
            _store_shard_columns(stage_ref, shards_ref, sems, first_chunk)

    chained = shards is not None
    return pl.pallas_call(
        body, name=name, grid=(n_steps,),
        in_specs=[pl.BlockSpec((D_MODEL, WGRAD_TS), lambda i: (0, i)), d_spec] + [ANY_SPEC] * chained,
        out_specs=ANY_SPEC,
        out_shape=jax.ShapeDtypeStruct((N_DEV, D_MODEL, SHARD_COLS), BF16),
        scratch_shapes=[pltpu.VMEM((D_MODEL, width), F32), pltpu.VMEM((D_MODEL, width), BF16),
                        pltpu.SemaphoreType.DMA((SHARD_COPIES_PER_CHUNK * n_chunks,))],
        input_output_aliases={2: 0} if chained else {},
        compiler_params=_params(("arbitrary",)))(h_t, d, *([shards] * chained))


def _proj_wgrad_part(h_t, d, n_chunks, chunk_block, first_chunk, shards, name):
    spec = pl.BlockSpec((n_chunks, WGRAD_TS, ATT_WIDTH), lambda i: (chunk_block, i, 0))
    return _proj_wgrad_call(h_t, d, spec, lambda d_ref, t: d_ref[t], n_chunks, first_chunk, shards, name)


def _proj_wgrad_group(h_t, d, g, r, shards, name):
    n = PROJ_TM // r

    def rows_of(d_ref, t):
        if r == 1:
            return d_ref[t, 0]
        return jnp.concatenate([d_ref[t, rho, s * n:(s + 1) * n] for s in range(WGRAD_TS // PROJ_TM)
                                for rho in range(r)], axis=0)

    spec = pl.BlockSpec((3, r, WGRAD_TS // r, ATT_WIDTH), lambda i: (0, 0, i, 0))
    return _proj_wgrad_call(h_t, d, spec, rows_of, 3, 3 * g, shards, name)


def _wgrad_tn(a, b, name):
    S, M = a.shape
    N = b.shape[1]
    ts = 1024

    def body(a_ref, b_ref, o_ref):
        i = pl.program_id(0)
        part = _dot_tn(a_ref[...], b_ref[...])

        @pl.when(i == 0)
        def _():
            o_ref[...] = part

        @pl.when(i > 0)
        def _():
            o_ref[...] += part

    return pl.pallas_call(
        body, name=name, grid=(S // ts,),
        in_specs=[pl.BlockSpec((ts, M), lambda i: (i, 0)), pl.BlockSpec((ts, N), lambda i: (i, 0))],
        out_specs=pl.BlockSpec((M, N), lambda i: (0, 0)),
        out_shape=jax.ShapeDtypeStruct((M, N), F32),
        compiler_params=_params(("arbitrary",)))(a, b)


def _attn_fwd(qkv, g, r, hosted=None):
    L = qkv.shape[1]
    nb2 = L // (2 * BAND)
    scale = HEAD_DIM ** -0.5

    def body(q_ref, kp_ref, kc_ref, vp_ref, vc_ref, o_ref, st_ref):
        m_step = pl.program_id(1)
        ii = lax.broadcasted_iota(jnp.int32, (BAND, 2 * BAND), 0)
        kk = lax.broadcasted_iota(jnp.int32, (BAND, 2 * BAND), 1)
        band = jnp.logical_and(kk >= ii, kk <= ii + BAND)
        masks = (jnp.logical_and(band, jnp.logical_or(kk >= BAND, m_step > 0)), band)
        lane = lax.broadcasted_iota(jnp.int32, (BAND, 128), 1)
        stats = [jnp.zeros((BAND, 128), F32), jnp.zeros((BAND, 128), F32)]
        for h in range(HEADS):
            sl = slice(h * HEAD_DIM, (h + 1) * HEAD_DIM)
            kc, vc = kc_ref[:, sl], vc_ref[:, sl]
            keys = (jnp.concatenate([kp_ref[:, sl], kc[:BAND]], axis=0), kc)
            vals = (jnp.concatenate([vp_ref[:, sl], vc[:BAND]], axis=0), vc)
            for b in range(2):
                rows = slice(b * BAND, (b + 1) * BAND)
                s = jnp.where(masks[b], _dot_nt(q_ref[rows, sl], keys[b]) * scale, NEG_INF)
                m = jnp.max(s, axis=1, keepdims=True)
                p = jnp.exp(s - m)
                l = jnp.sum(p, axis=1, keepdims=True)
                o_ref[rows, sl] = _dot(p.astype(BF16), vals[b]) / l
                stats[b] = jnp.where(lane == h, m, stats[b])
                stats[b] = jnp.where(lane == HEADS + h, l, stats[b])
        st_ref[0:BAND, :] = stats[0]
        st_ref[BAND:, :] = stats[1]

    two = (None, 2 * BAND, ATT_WIDTH)
    one = (None, BAND, ATT_WIDTH)

    def prev(m):
        return jnp.maximum(2 * m - 1, 0)

    return _call(
        body, (qkv, qkv, qkv, qkv, qkv), name=f"attn_fwd_g{g}", grid=(r, nb2),
        in_specs=[
            pl.BlockSpec(two, lambda rho, m: (rho, m, 0)),
            pl.BlockSpec(one, lambda rho, m: (rho, prev(m), 1)),
            pl.BlockSpec(two, lambda rho, m: (rho, m, 1)),
            pl.BlockSpec(one, lambda rho, m: (rho, prev(m), 2)),
            pl.BlockSpec(two, lambda rho, m: (rho, m, 2)),
        ],
        out_specs=(pl.BlockSpec(two, lambda rho, m: (rho, m, 0)),
                   pl.BlockSpec((None, 2 * BAND, 128), lambda rho, m: (rho, m, 0))),
        out_shape=(jax.ShapeDtypeStruct((r, L, ATT_WIDTH), F32),
                   jax.ShapeDtypeStruct((r, L, 128), F32)),
        hosted=hosted)


def _attn_combine(outs, stats, rest):
    S = rest.shape[0]
    tm = 256
    gatt_blk = 8
    back = [_perm_matrix(r, tm, transpose=True) for r in DILATIONS[1:]]
    forth = [_perm_matrix(r, tm) for r in DILATIONS[1:]]

    def body(o0_ref, o1_ref, o2_ref, s0_ref, s1_ref, s2_ref, g_ref, b1_ref, b2_ref, f1_ref, f2_ref,
             o_ref, a_ref, l0_ref, l1_ref, l2_ref):
        outs_nat = [o0_ref[0]] + [_permute_f32(b_ref[...], _rows_of_residues(o_g), 2)
                                  for o_g, b_ref in ((o1_ref, b1_ref), (o2_ref, b2_ref))]
        st = [s0_ref[0]] + [_permute_f32(b_ref[...], _rows_of_residues(s_g), 3)
                            for s_g, b_ref in ((s1_ref, b1_ref), (s2_ref, b2_ref))]
        lane = lax.broadcasted_iota(jnp.int32, (tm, 128), 1)
        lse_out = jnp.zeros((tm, 128), F32)
        for h in range(HEADS):
            sl = slice(h * HEAD_DIM, (h + 1) * HEAD_DIM)
            ms = [s[:, h:h + 1] for s in st]
            ls = [s[:, HEADS + h:HEADS + h + 1] for s in st]
            m_all = jnp.maximum(jnp.maximum(ms[0], ms[1]), ms[2])
            ws = [l * jnp.exp(m - m_all) for m, l in zip(ms, ls)]
            den = ws[0] + ws[1] + ws[2]
            o = (ws[0] * outs_nat[0][:, sl] + ws[1] * outs_nat[1][:, sl] + ws[2] * outs_nat[2][:, sl]) / den
            o_ref[:, sl] = o
            a_ref[:, sl] = (o * _silu(g_ref[:, sl])).astype(BF16)
            lse_out = jnp.where(lane == h, m_all + jnp.log(den), lse_out)
        l0_ref[0] = lse_out
        for l_ref, f_ref in ((l1_ref, f1_ref), (l2_ref, f2_ref)):
            _store_residues(l_ref, _permute_f32(f_ref[...], lse_out, 3))

    o_spec = pl.BlockSpec((tm, ATT_WIDTH), lambda i: (i, 0))
    whole = pl.BlockSpec(memory_space=pltpu.VMEM)

    def res_spec(r, width):
        return pl.BlockSpec((r, tm // r, width), lambda i: (0, i, 0))

    return pl.pallas_call(
        body, name="attn_combine", grid=(S // tm,),
        in_specs=[*[res_spec(r, ATT_WIDTH) for r in DILATIONS], *[res_spec(r, 128) for r in DILATIONS],
                  pl.BlockSpec((tm, ATT_WIDTH), lambda i: (i, gatt_blk)), whole, whole, whole, whole],
        out_specs=(o_spec, o_spec, *[res_spec(r, 128) for r in DILATIONS]),
        out_shape=(jax.ShapeDtypeStruct((S, ATT_WIDTH), F32), jax.ShapeDtypeStruct((S, ATT_WIDTH), BF16),
                   *[jax.ShapeDtypeStruct((r, S // r, 128), F32) for r in DILATIONS]),
        compiler_params=_params(("parallel",)))(*outs, *stats, rest, *back, *forth)


def _attn_bwd(qkv, do, lse, dvec, g, r, hosted=None):
    L = qkv.shape[1]
    nb = L // BAND
    nb2 = nb // 2
    scale = HEAD_DIM ** -0.5

    def body(qc_ref, qn_ref, k_ref, v_ref, doc_ref, don_ref, lc_ref, ln_ref, dc_ref, dn_ref,
             out_ref, carry_ref):
        j = pl.program_id(1)

        @pl.when(j == 0)
        def _():
            carry_ref[...] = jnp.zeros_like(carry_ref)

        ii = lax.broadcasted_iota(jnp.int32, (3 * BAND, 2 * BAND), 0)
        kk = lax.broadcasted_iota(jnp.int32, (3 * BAND, 2 * BAND), 1)
        mask = jnp.logical_and(jnp.logical_and(kk <= ii, kk >= ii - BAND),
                               jnp.logical_or(ii < 2 * BAND, j < nb2 - 1))
        lse3 = jnp.concatenate([lc_ref[...], ln_ref[...]], axis=0)
        dvec3 = jnp.concatenate([dc_ref[...], dn_ref[...]], axis=0)
        for h in range(HEADS):
            sl = slice(h * HEAD_DIM, (h + 1) * HEAD_DIM)
            k = k_ref[:, sl]
            v = v_ref[:, sl]
            q = jnp.concatenate([qc_ref[:, sl], qn_ref[:, sl]], axis=0)
            do = jnp.concatenate([doc_ref[:, sl], don_ref[:, sl]], axis=0)
            p = jnp.where(mask, jnp.exp(_dot_nt(q, k) * scale - lse3[:, h:h + 1]), 0.0)
            ds = (p * (_dot_nt(do, v) - dvec3[:, h:h + 1])).astype(BF16)
            dq3 = _dot(ds, k) * scale
            out_ref[0, 0:BAND, sl] = (carry_ref[:, sl] + dq3[:BAND]).astype(BF16)
            out_ref[0, BAND:, sl] = dq3[BAND:2 * BAND].astype(BF16)
            out_ref[1, :, sl] = (_dot_tn(ds, q) * scale).astype(BF16)
            out_ref[2, :, sl] = _dot_tn(p.astype(BF16), do).astype(BF16)
            carry_ref[:, sl] = dq3[2 * BAND:]

    two = (None, 2 * BAND, ATT_WIDTH)
    one = (None, BAND, ATT_WIDTH)
    stwo = (None, 2 * BAND, 128)
    sone = (None, BAND, 128)

    def nxt(j):
        return jnp.minimum(2 * j + 2, nb - 1)

    (out,), got = _call(
        body, (qkv, qkv, qkv, qkv, do, do, lse, lse, dvec, dvec), name=f"attn_bwd_g{g}", grid=(r, nb2),
        in_specs=[
            pl.BlockSpec(two, lambda rho, j: (rho, j, 0)),
            pl.BlockSpec(one, lambda rho, j: (rho, nxt(j), 0)),
            pl.BlockSpec(two, lambda rho, j: (rho, j, 1)),
            pl.BlockSpec(two, lambda rho, j: (rho, j, 2)),
            pl.BlockSpec(two, lambda rho, j: (rho, j, 0)),
            pl.BlockSpec(one, lambda rho, j: (rho, nxt(j), 0)),
            pl.BlockSpec(stwo, lambda rho, j: (rho, j, 0)),
            pl.BlockSpec(sone, lambda rho, j: (rho, nxt(j), 0)),
            pl.BlockSpec(stwo, lambda rho, j: (rho, j, 0)),
            pl.BlockSpec(sone, lambda rho, j: (rho, nxt(j), 0)),
        ],
        out_specs=(pl.BlockSpec((3, None, 2 * BAND, ATT_WIDTH), lambda rho, j: (0, rho, j, 0)),),
        out_shape=(jax.ShapeDtypeStruct((3, r, L, ATT_WIDTH), BF16),),
        scratch_shapes=[pltpu.VMEM((BAND, ATT_WIDTH), F32)],
        hosted=hosted)
    return out, got


LRU_T = 512


def _linear_scan(a, b, carry, reverse):
    T, C = a.shape
    a = a.reshape(T // 8, 8, C)
    b = b.reshape(T // 8, 8, C)
    row8 = lax.broadcasted_iota(jnp.int32, a.shape, 1)
    for s in (1, 2, 4):
        keep = (row8 < 8 - s) if reverse else (row8 >= s)
        shift = 8 - s if reverse else s
        b_sh = jnp.where(keep, pltpu.roll(b, shift, 1), 0.0)
        a_sh = jnp.where(keep, pltpu.roll(a, shift, 1), 1.0)
        b = a * b_sh + b
        a = a * a_sh
    tiles = [None] * (T // 8)
    order = range(T // 8 - 1, -1, -1) if reverse else range(T // 8)
    for k in order:
        y = b[k] + a[k] * carry
        tiles[k] = y
        carry = y[0:1] if reverse else y[7:8]
    return jnp.concatenate(tiles, axis=0), carry


def _gate_matmuls(ucb, w_ref, bias):
    parts = [_dot(ucb[:, j * 128:(j + 1) * 128], w_ref[j]) for j in range(8)]
    return jnp.concatenate(parts, axis=1) + bias


def _rows_before(x, prev8, j):
    T, C = x.shape
    rot = pltpu.roll(x.reshape(T // 8, 8, C), j, 1)
    rot_prev = jnp.concatenate([pltpu.roll(prev8, j, 0)[None], rot[:-1]], axis=0)
    row8 = lax.broadcasted_iota(jnp.int32, rot.shape, 1)
    return jnp.where(row8 >= j, rot, rot_prev).reshape(T, C)


def _rows_after(x, next8, j):
    T, C = x.shape
    rot = pltpu.roll(x.reshape(T // 8, 8, C), 8 - j, 1)
    rot_next = jnp.concatenate([rot[1:], pltpu.roll(next8, 8 - j, 0)[None]], axis=0)
    row8 = lax.broadcasted_iota(jnp.int32, rot.shape, 1)
    return jnp.where(row8 < 8 - j, rot, rot_next).reshape(T, C)


def _conv_fwd(u, u_prev8, cw_ref):
    uc = cw_ref[4:5, :] + cw_ref[0:1, :] * u
    for j in range(1, CONV_WIDTH):
        uc = uc + cw_ref[j:j + 1, :] * _rows_before(u, u_prev8, j)
    return uc


def _lru_fwd(rest, cw, wr2, wi2, hosted=None):
    S = rest.shape[0]
    T = LRU_T

    def body(u_ref, cw_ref, wr_ref, wi_ref, h_ref, a_ref, uc_ref, r_ref, ig_ref, mult_ref,
             ucar_ref, hcar_ref):
        c = pl.program_id(0)

        @pl.when(c == 0)
        def _():
            ucar_ref[...] = jnp.zeros_like(ucar_ref)
            hcar_ref[...] = jnp.zeros_like(hcar_ref)

        u = u_ref[...]
        uc = _conv_fwd(u, ucar_ref[...], cw_ref)
        ucar_ref[...] = u[T - 8:, :]
        ucb = uc.astype(BF16)
        r = jax.nn.sigmoid(_gate_matmuls(ucb, wr_ref, cw_ref[5:6, :]))
        ig = jax.nn.sigmoid(_gate_matmuls(ucb, wi_ref, cw_ref[6:7, :]))
        log_a = -LRU_C * r * _softplus_neg(cw_ref[7:8, :])
        a = jnp.exp(log_a)
        mult = jnp.sqrt(_neg_expm1(2.0 * log_a))
        b = mult * (ig * uc)
        a_ref[...] = a
        uc_ref[...] = uc
        r_ref[...] = r
        ig_ref[...] = ig
        mult_ref[...] = mult
        h, last = _linear_scan(a, b, hcar_ref[0:1, :], reverse=False)
        h_ref[...] = h
        hcar_ref[...] = jnp.broadcast_to(last, (8, D_MODEL))

    row_spec = pl.BlockSpec((T, D_MODEL), lambda c: (c, 0))
    row_shape = jax.ShapeDtypeStruct((S, D_MODEL), F32)
    return _call(
        body, (rest, cw, wr2, wi2), name="lru_fwd", grid=(S // T,),
        in_specs=[row_spec, pl.BlockSpec((8, D_MODEL), lambda c: (0, 0)),
                  pl.BlockSpec((8, 128, 128), lambda c: (0, 0, 0)),
                  pl.BlockSpec((8, 128, 128), lambda c: (0, 0, 0))],
        out_specs=(row_spec,) * 6, out_shape=(row_shape,) * 6,
        scratch_shapes=[pltpu.VMEM((8, D_MODEL), F32), pltpu.VMEM((8, D_MODEL), F32)],
        hosted=hosted)


def _lru_bwd(dhl, hl, a_all, gates, rest, cw, wr2, wi2, dproj, hosted=None):
    S = rest.shape[0]
    T = LRU_T
    nc = S // T

    def body(dh_ref, h_ref, hp_ref, a_ref, uc_ref, r_ref, ig_ref, mult_ref, u_ref, up_ref, cw_ref, wr_ref,
             wi_ref, _alias, du_ref, acc_ref, gwr_ref, gwi_ref, gcar_ref, acar_ref, dcar_ref):
        step = pl.program_id(0)
        c = nc - 1 - step

        @pl.when(step == 0)
        def _():
            acc_ref[...] = jnp.zeros_like(acc_ref)
            gwr_ref[...] = jnp.zeros_like(gwr_ref)
            gwi_ref[...] = jnp.zeros_like(gwi_ref)
            gcar_ref[...] = jnp.zeros_like(gcar_ref)
            acar_ref[...] = jnp.zeros_like(acar_ref)
            dcar_ref[...] = jnp.zeros_like(dcar_ref)

        u = u_ref[...]
        u_prev = jnp.where(c > 0, up_ref[...], 0.0)
        h_prev8 = jnp.where(c > 0, hp_ref[...], 0.0)
        a = a_ref[...]
        h = h_ref[...]
        uc, r, ig, mult = uc_ref[...], r_ref[...], ig_ref[...], mult_ref[...]
        ucb = uc.astype(BF16)
        sp = _softplus_neg(cw_ref[7:8, :])
        a_next = _rows_after(a, acar_ref[...], 1)
        G, first = _linear_scan(a_next, dh_ref[...], gcar_ref[0:1, :], reverse=True)
        gcar_ref[...] = jnp.broadcast_to(first, (8, D_MODEL))
        acar_ref[...] = jnp.broadcast_to(a[0:1, :], (8, D_MODEL))
        d_a = G * _rows_before(h, h_prev8, 1)
        d_mult = G * (ig * uc)
        d_ig = G * (mult * uc)
        duc = G * (mult * ig)
        d_log_a = d_a * a - d_mult * (a * a) / mult
        d_r = d_log_a * (-LRU_C * sp)
        d_sp = jnp.sum(d_log_a * (-LRU_C * r), axis=0, keepdims=True)
        dpre_r = d_r * r * (1.0 - r)
        dpre_i = d_ig * ig * (1.0 - ig)
        dprb = dpre_r.astype(BF16)
        dpib = dpre_i.astype(BF16)
        back = []
        for j in range(8):
            sl = slice(j * 128, (j + 1) * 128)
            back.append(_dot_nt(dprb[:, sl], wr_ref[j]) + _dot_nt(dpib[:, sl], wi_ref[j]))
            gwr_ref[j] += _dot_tn(ucb[:, sl], dprb[:, sl])
            gwi_ref[j] += _dot_tn(ucb[:, sl], dpib[:, sl])
        duc = duc + jnp.concatenate(back, axis=1)
        du = cw_ref[0:1, :] * duc
        duc_next8 = dcar_ref[...]
        acc_ref[0:1, :] += jnp.sum(duc * u, axis=0, keepdims=True)
        for j in range(1, CONV_WIDTH):
            du = du + cw_ref[j:j + 1, :] * _rows_after(duc, duc_next8, j)
            acc_ref[j:j + 1, :] += jnp.sum(duc * _rows_before(u, u_prev, j), axis=0, keepdims=True)
        dcar_ref[...] = duc[0:8, :]
        acc_ref[4:5, :] += jnp.sum(duc, axis=0, keepdims=True)
        acc_ref[5:6, :] += jnp.sum(dpre_r, axis=0, keepdims=True)
        acc_ref[6:7, :] += jnp.sum(dpre_i, axis=0, keepdims=True)
        acc_ref[7:8, :] += d_sp
        du_ref[0] = du[:, 0:ATT_WIDTH].astype(BF16)
        du_ref[1] = du[:, ATT_WIDTH:].astype(BF16)

    def rev(step):
        return nc - 1 - step

    def prev8(step):
        return jnp.maximum(rev(step) * (T // 8) - 1, 0)

    row_spec = pl.BlockSpec((T, D_MODEL), lambda s: (rev(s), 0))
    prev_spec = pl.BlockSpec((8, D_MODEL), lambda s: (prev8(s), 0))
    vec_spec = pl.BlockSpec((8, D_MODEL), lambda s: (0, 0))
    w_spec = pl.BlockSpec((8, 128, 128), lambda s: (0, 0, 0))
    return _call(
        body, (dhl, hl, hl, a_all, *gates, rest, rest, cw, wr2, wi2, dproj), name="lru_bwd", grid=(nc,),
        in_specs=[row_spec, row_spec, prev_spec, row_spec, row_spec, row_spec, row_spec, row_spec,
                  row_spec, prev_spec, vec_spec, w_spec, w_spec, ANY_SPEC],
        out_specs=(pl.BlockSpec((2, T, ATT_WIDTH), lambda s: (4, rev(s), 0)), vec_spec, w_spec, w_spec),
        out_shape=(jax.ShapeDtypeStruct(dproj.shape, BF16), jax.ShapeDtypeStruct((8, D_MODEL), F32),
                   jax.ShapeDtypeStruct((8, 128, 128), F32), jax.ShapeDtypeStruct((8, 128, 128), F32)),
        scratch_shapes=[pltpu.VMEM((8, D_MODEL), F32), pltpu.VMEM((8, D_MODEL), F32),
                        pltpu.VMEM((8, D_MODEL), F32)],
        aliases={13: 0}, hosted=hosted)


OUT_TM = 256


def _out_fwd(a_gated, hl, rest, x, vec, w_pa, w_pb, w_o, target=None, hosted=None):
    S = x.shape[0]
    tm = OUT_TM
    with_loss = target is not None

    def body(*refs):
        (ag_ref, hl_ref, gl_ref, ma_ref, mb_ref, x_ref, v_ref, wpa_ref, wpb_ref, wo_ref), refs = refs[:10], refs[10:]
        if with_loss:
            t_ref, refs = refs[0], refs[1:]
        xn_ref, mix_ref, ya_ref, yb_ref, bg_ref = refs[:5]
        bg = (hl_ref[...] * _silu(gl_ref[...])).astype(BF16)
        ya = _dot(ag_ref[...], wpa_ref[...])
        yb = _dot(bg, wpb_ref[...])
        mix = (jax.nn.sigmoid(ma_ref[...]) * ya + jax.nn.sigmoid(mb_ref[...]) * yb).astype(BF16)
        out = _dot(mix, wo_ref[...])
        rstd = lax.rsqrt(jnp.mean(out * out, axis=-1, keepdims=True) + NORM_EPS)
        x_new = x_ref[...] + v_ref[1:2, :] * ((out * rstd) * v_ref[0:1, :])
        if with_loss:
            acc_ref = refs[5]

            @pl.when(pl.program_id(0) == 0)
            def _():
                acc_ref[...] = jnp.zeros_like(acc_ref)

            err = x_new - t_ref[...]
            xn_ref[...] = err * (1.0 / D_MODEL)
            part = jnp.sum(jnp.sum(err * err, axis=1, keepdims=True), axis=0, keepdims=True)
            acc_ref[...] += jnp.broadcast_to(part, acc_ref.shape)
        else:
            xn_ref[...] = x_new
        mix_ref[...] = mix
        ya_ref[...] = ya.astype(BF16)
        yb_ref[...] = yb.astype(BF16)
        bg_ref[...] = bg

    def col(j):
        return pl.BlockSpec((tm, D_MODEL), lambda i: (i, j))

    def whole(shape):
        return pl.BlockSpec(shape, lambda i: (0, 0))

    row = col(0)
    bf = jax.ShapeDtypeStruct((S, D_MODEL), BF16)
    loss_in = ([target], [row]) if with_loss else ([], [])
    loss_out = ([jax.ShapeDtypeStruct((8, 128), F32)], [whole((8, 128))]) if with_loss else ([], [])
    return _call(
        body, (a_gated, hl, rest, rest, rest, x, vec, w_pa, w_pb, w_o, *loss_in[0]), name="out_fwd",
        grid=(S // tm,),
        in_specs=[pl.BlockSpec((tm, ATT_WIDTH), lambda i: (i, 0)), row, col(1), col(2), col(3), row,
                  whole((8, D_MODEL)), whole((ATT_WIDTH, D_MODEL)), whole((D_MODEL, D_MODEL)),
                  whole((D_MODEL, D_MODEL)), *loss_in[1]],
        out_specs=(row, row, row, row, row, *loss_out[1]),
        out_shape=(jax.ShapeDtypeStruct((S, D_MODEL), F32), bf, bf, bf, bf, *loss_out[0]),
        hosted=hosted)


def _out_bwd(dxn, mix, ya, yb, hl, rest, o_att, vec, w_pa, w_pb, w_o, hosted=None):
    S = dxn.shape[0]
    tm = OUT_TM
    forth = [_perm_matrix(r, tm) for r in DILATIONS[1:]]

    def body(dxn_ref, mix_ref, ya_ref, yb_ref, hl_ref, gl_ref, ma_ref, mb_ref, ga_ref, o_ref, v_ref,
             wpa_ref, wpb_ref, wo_ref, f1_ref, f2_ref,
             dout_ref, dya_ref, dyb_ref, dhl_ref, dp_ref, acc_ref,
             do0_ref, do1_ref, do2_ref, dv0_ref, dv1_ref, dv2_ref):
        i = pl.program_id(0)

        @pl.when(i == 0)
        def _():
            acc_ref[...] = jnp.zeros_like(acc_ref)

        g_post = v_ref[0:1, :]
        gate = v_ref[1:2, :]
        dxn_v = dxn_ref[...]
        out = _dot(mix_ref[...], wo_ref[...])
        rstd = lax.rsqrt(jnp.mean(out * out, axis=-1, keepdims=True) + NORM_EPS)
        nrm = out * rstd
        acc_ref[0:1, :] += jnp.sum(dxn_v * (nrm * g_post), axis=0, keepdims=True)
        acc_ref[1:2, :] += jnp.sum(dxn_v * gate * nrm, axis=0, keepdims=True)
        dn = dxn_v * (gate * g_post)
        dout = (rstd * (dn - nrm * jnp.mean(dn * nrm, axis=-1, keepdims=True))).astype(BF16)
        dout_ref[...] = dout
        dmix = _dot_nt(dout, wo_ref[...])
        sa = jax.nn.sigmoid(ma_ref[...])
        sb = jax.nn.sigmoid(mb_ref[...])
        dya = (dmix * sa).astype(BF16)
        dyb = (dmix * sb).astype(BF16)
        dya_ref[...] = dya
        dyb_ref[...] = dyb
        dma = dmix * ya_ref[...].astype(F32) * (sa * (1.0 - sa))
        dmb = dmix * yb_ref[...].astype(F32) * (sb * (1.0 - sb))
        d_ag = _dot_nt(dya, wpa_ref[...])
        d_bg = _dot_nt(dyb, wpb_ref[...])
        gl = gl_ref[...]
        hl_v = hl_ref[...]
        dhl_ref[...] = d_bg * _silu(gl)
        dgl = d_bg * hl_v * _dsilu(gl)
        ga = ga_ref[...]
        o = o_ref[...]
        do = d_ag * _silu(ga)
        dga = d_ag * o * _dsilu(ga)
        lane = lax.broadcasted_iota(jnp.int32, (tm, 128), 1)
        dvec = jnp.zeros((tm, 128), F32)
        for h in range(HEADS):
            sl = slice(h * HEAD_DIM, (h + 1) * HEAD_DIM)
            dvec = jnp.where(lane == h, jnp.sum(do[:, sl] * o[:, sl], axis=1, keepdims=True), dvec)
        do_b = do.astype(BF16)
        do0_ref[0] = do_b
        dv0_ref[0] = dvec
        for do_ref, dv_ref, f_ref in ((do1_ref, dv1_ref, f1_ref), (do2_ref, dv2_ref, f2_ref)):
            _store_residues(do_ref, _dot(f_ref[...], do_b).astype(BF16))
            _store_residues(dv_ref, _permute_f32(f_ref[...], dvec, 3))
        dp_ref[0] = dgl[:, 0:ATT_WIDTH].astype(BF16)
        dp_ref[1] = dgl[:, ATT_WIDTH:].astype(BF16)
        dp_ref[2] = dma[:, 0:ATT_WIDTH].astype(BF16)
        dp_ref[3] = dma[:, ATT_WIDTH:].astype(BF16)
        dp_ref[4] = dmb[:, 0:ATT_WIDTH].astype(BF16)
        dp_ref[5] = dmb[:, ATT_WIDTH:].astype(BF16)
        dp_ref[6] = dga.astype(BF16)

    def col(j):
        return pl.BlockSpec((tm, D_MODEL), lambda i: (i, j))

    def whole(shape):
        return pl.BlockSpec(shape, lambda i: (0, 0))

    def res_spec(r, width):
        return pl.BlockSpec((r, tm // r, width), lambda i: (0, i, 0))

    row = col(0)
    att = pl.BlockSpec((tm, ATT_WIDTH), lambda i: (i, 0))
    bf = jax.ShapeDtypeStruct((S, D_MODEL), BF16)
    vmem = pl.BlockSpec(memory_space=pltpu.VMEM)
    return _call(
        body, (dxn, mix, ya, yb, hl, rest, rest, rest, rest, o_att, vec, w_pa, w_pb, w_o, *forth),
        name="out_bwd", grid=(S // tm,),
        in_specs=[row, row, row, row, row, col(1), col(2), col(3),
                  pl.BlockSpec((tm, ATT_WIDTH), lambda i: (i, 8)), att,
                  whole((8, D_MODEL)), whole((ATT_WIDTH, D_MODEL)), whole((D_MODEL, D_MODEL)),
                  whole((D_MODEL, D_MODEL)), vmem, vmem],
        out_specs=(row, row, row, row,
                   pl.BlockSpec((7, tm, ATT_WIDTH), lambda i: (0, i, 0)), whole((8, D_MODEL)),
                   *[res_spec(r, ATT_WIDTH) for r in DILATIONS], *[res_spec(r, 128) for r in DILATIONS]),
        out_shape=(bf, bf, bf, jax.ShapeDtypeStruct((S, D_MODEL), F32),
                   jax.ShapeDtypeStruct((DREST_CHUNKS, S, ATT_WIDTH), BF16),
                   jax.ShapeDtypeStruct((8, D_MODEL), F32),
                   *[jax.ShapeDtypeStruct((r, S // r, ATT_WIDTH), BF16) for r in DILATIONS],
                   *[jax.ShapeDtypeStruct((r, S // r, 128), F32) for r in DILATIONS]),
        hosted=hosted)


def _adamw(gsrcs, w, m, v, n_stack, name, rows=None):
    n_layers, R, C = w.shape
    assert len(gsrcs) == n_layers
    budget = 96 * 1024
    tr = rows
    if tr is None:
        tr = R
        while tr * C > budget and tr % 16 == 0:
            tr //= 2
    assert R % tr == 0 and (tr % 8 == 0 or tr == R)
    c1 = 1.0 - ADAM_B1 ** ADAM_STEP
    c2 = 1.0 - ADAM_B2 ** ADAM_STEP

    def body(*refs):
        g_refs = refs[:n_layers]
        w_ref, m_ref, v_ref, go_ref, d_ref, mo_ref, vo_ref = refs[n_layers:]

        def update(g_ref):
            if n_stack:
                g = g_ref[0].astype(F32)
                for s in range(1, n_stack):
                    g = g + g_ref[s].astype(F32)
            else:
                g = g_ref[...]
            m_new = ADAM_B1 * m_ref[...] + (1.0 - ADAM_B1) * g
            v_new = ADAM_B2 * v_ref[...] + (1.0 - ADAM_B2) * (g * g)
            m_hat = m_new / c1
            v_hat = v_new / c2
            go_ref[...] = g
            d_ref[...] = -ADAM_LR * (m_hat / (jnp.sqrt(v_hat) + ADAM_EPS) + ADAM_WD * w_ref[...])
            mo_ref[...] = m_new
            vo_ref[...] = v_new

        for layer in range(n_layers):
            pl.when(pl.program_id(0) == layer)(functools.partial(update, g_refs[layer]))

    def g_spec(layer):
        def rows_of(l, i):
            return jnp.where(l == layer, i, 0)
        if n_stack:
            return pl.BlockSpec((n_stack, tr, C), lambda l, i: (0, rows_of(l, i), 0))
        return pl.BlockSpec((tr, C), lambda l, i: (rows_of(l, i), 0))

    blk = pl.BlockSpec((None, tr, C), lambda l, i: (l, i, 0))
    shp = jax.ShapeDtypeStruct((n_layers, R, C), F32)
    return pl.pallas_call(
        body, name=name, grid=(n_layers, R // tr),
        in_specs=[g_spec(layer) for layer in range(n_layers)] + [blk, blk, blk],
        out_specs=(blk, blk, blk, blk), out_shape=(shp, shp, shp, shp),
        compiler_params=_params(("arbitrary", "arbitrary")))(*gsrcs, w, m, v)


def _pair_blocks(w):
    w = w.reshape(8, 2, 64, 64)
    z = jnp.zeros((8, 64, 64), w.dtype)
    top = jnp.concatenate([w[:, 0], z], axis=2)
    bot = jnp.concatenate([z, w[:, 1]], axis=2)
    return jnp.concatenate([top, bot], axis=1).astype(BF16)


def _unpair_blocks(g):
    return jnp.stack([g[:, :64, :64], g[:, 64:, 64:]], axis=1).reshape(16, 64, 64)


def _mats_full(w_pa_g, w_pb_g, w_o_g):
    return dict(w_pa=jnp.transpose(w_pa_g, (1, 0, 2)).reshape(ATT_WIDTH, D_MODEL),
                w_pb=w_pb_g.reshape(D_MODEL, D_MODEL), w_o=w_o_g.reshape(D_MODEL, D_MODEL))


def _layer_params(layer, w_in_g, mats_g, conv_w_full, conv_b, b_rg, b_ig, lru_lambda, w_rg, w_ig, g_pre,
                  g_post):
    cw = jnp.concatenate([conv_w_full, conv_b[None], b_rg[None], b_ig[None], lru_lambda[None]], axis=0)
    p = dict(w_int=_weights_full(w_in_g), cw=cw, wr2=_pair_blocks(w_rg), wi2=_pair_blocks(w_ig),
             g_pre=g_pre, g_post=g_post, lru_lambda=lru_lambda)
    if mats_g is not None:
        p.update(_mats_full(*mats_g))
    return p


def _layer_fwd(x, mod, p, host_proj=None, host_lru=None, host_out=None, own_mats=None, target=None):
    zeros = jnp.zeros((5, D_MODEL), F32)
    vec_pre = jnp.concatenate([p["g_pre"][None], mod[1:2], mod[0:1], zeros], axis=0)
    (ht0, ht1, ht2, q0, q1, q2, rest), got_proj = _norm_proj_fwd(x, vec_pre, p["w_int"], hosted=host_proj)
    h_t = [ht0, ht1, ht2]
    hosts = [None, None, None]
    if own_mats is not None:
        hosts[0], hosts[1] = (list(own_mats[:2]), ["ag"] * 2), ([own_mats[2]], ["ag"])
    qkv, outs, stats, got_attn = [q0, q1, q2], [], [], []
    for g, r in enumerate(DILATIONS):
        (o_g, st_g), got = _attn_fwd(qkv[g], g, r, hosted=hosts[g])
        outs.append(o_g)
        stats.append(st_g)
        got_attn += got
    if own_mats is not None:
        p.update(_mats_full(*got_attn))
    o_att, a_gated, *lse = _attn_combine(outs, stats, rest)
    (hl, a_dec, *gates), got_lru = _lru_fwd(rest, p["cw"], p["wr2"], p["wi2"], hosted=host_lru)
    vec_post = jnp.concatenate([p["g_post"][None], mod[2:3], jnp.zeros((6, D_MODEL), F32)], axis=0)
    (x_new, mix, ya, yb, b_gated, *loss_acc), got_out = _out_fwd(
        a_gated, hl, rest, x, vec_post, p["w_pa"], p["w_pb"], p["w_o"], target=target, hosted=host_out)
    saved = dict(x=x, h_t=h_t, qkv=qkv, rest=rest, o_att=o_att, a_gated=a_gated, lse=lse, hl=hl, a_dec=a_dec,
                 gates=gates, mix=mix, ya=ya, yb=yb, b_gated=b_gated, vec_pre=vec_pre, vec_post=vec_post)
    return (x_new, *loss_acc), saved, got_proj, got_lru, got_out


def _mats_sources(g_pa, g_pb, g_o):
    return [jnp.transpose(g_pa.astype(BF16).reshape(ATT_WIDTH, N_DEV, 128), (1, 0, 2)),
            g_pb.astype(BF16).reshape(N_DEV, 128, D_MODEL), g_o.astype(BF16).reshape(N_DEV, 128, D_MODEL)]


LRU_SMALL = ("conv_b", "w_rg", "b_rg", "w_ig", "b_ig", "lru_lambda")
MOD_SMALL = ("b_mod", "g_pre", "g_post")


def _flat_rows(arrs):
    return jnp.concatenate([a.reshape(-1) for a in arrs]).reshape(-1, 128)


def _layer_bwd(dxn, sv, p, upper_small=None):
    a2a3 = ["a2a"] * 3
    (dout, dya, dyb, dhl, drest, acc_out, *do_dvec), _ = _out_bwd(
        dxn, sv["mix"], sv["ya"], sv["yb"], sv["hl"], sv["rest"], sv["o_att"], sv["vec_post"],
        p["w_pa"], p["w_pb"], p["w_o"])
    mats = _mats_sources(_wgrad_tn(sv["a_gated"], dya, "wgrad_pa"), _wgrad_tn(sv["b_gated"], dyb, "wgrad_pb"),
                         _wgrad_tn(sv["mix"], dout, "wgrad_o"))
    (drest, acc_lru, gwr2, gwi2), _ = _lru_bwd(
        dhl, sv["hl"], sv["a_dec"], sv["gates"], sv["rest"], p["cw"], p["wr2"], p["wi2"], drest)
    small = dict(
        conv_w=acc_lru[0:4], conv_b=acc_lru[4], b_rg=acc_lru[5], b_ig=acc_lru[6],
        lru_lambda=acc_lru[7] * (-jax.nn.sigmoid(-p["lru_lambda"])),
        w_rg=_unpair_blocks(gwr2), w_ig=_unpair_blocks(gwi2))
    hosts = [(mats, a2a3), None, None]
    if upper_small is not None:
        for half, names in enumerate((LRU_SMALL[:3], LRU_SMALL[3:] + ("conv_w",))):
            vec = _flat_rows([jnp.stack([small[n], upper_small[n]]) for n in names])
            hosts[1 + half] = ([vec], ["ag"])
    dqkv, got_attn = [], []
    for g, r in enumerate(DILATIONS):
        d_g, got = _attn_bwd(sv["qkv"][g], do_dvec[g], sv["lse"][g], do_dvec[3 + g], g, r, hosted=hosts[g])
        dqkv.append(d_g)
        got_attn.append(got)
    shards = None
    for g, r in enumerate(DILATIONS):
        shards = _proj_wgrad_group(sv["h_t"][g], dqkv[g], g, r, shards, f"proj_wgrad_g{g}")
    shards = _proj_wgrad_part(sv["h_t"][0], drest, 2, 4, QKV_CHUNKS, shards, "proj_wgrad_u")
    shards = _proj_wgrad_part(sv["h_t"][0], drest, 7, 0, QKV_CHUNKS + 2, shards, "proj_wgrad_rest")
    (dx, acc_pre), got_in = _proj_dgrad_norm_bwd(
        dqkv, drest, p["w_int"], sv["x"], dxn, sv["vec_pre"], hosted=([shards], ["a2a"]))
    small.update(dmod=jnp.concatenate([acc_pre[0], acc_pre[1], acc_out[0]]), g_pre=acc_pre[2],
                 g_post=acc_out[1])
    return dx, small, dict(mats=got_attn[0], w_in=got_in[0], lru_small=got_attn[1] + got_attn[2])


SMALL_NAMES = ("b_mod", "g_pre", "conv_b", "w_rg", "b_rg", "w_ig", "b_ig", "lru_lambda", "g_post")


def kernel(x, c, w_mod, b_mod, g_pre, w_in, conv_w, conv_b, w_rg, b_rg, w_ig, b_ig, lru_lambda, w_pa, w_pb, w_o, g_post, loss_target, m_w_mod, m_b_mod, m_g_pre, m_w_in, m_conv_w, m_conv_b, m_w_rg, m_b_rg, m_w_ig, m_b_ig, m_lru_lambda, m_w_pa, m_w_pb, m_w_o, m_g_post, v_w_mod, v_b_mod, v_g_pre, v_w_in, v_conv_w, v_conv_b, v_w_rg, v_b_rg, v_w_ig, v_b_ig, v_lru_lambda, v_w_pa, v_w_pb, v_w_o, v_g_post):
    W = dict(w_mod=w_mod, b_mod=b_mod, g_pre=g_pre, w_in=w_in, conv_w=conv_w, conv_b=conv_b, w_rg=w_rg,
             b_rg=b_rg, w_ig=w_ig, b_ig=b_ig, lru_lambda=lru_lambda, w_pa=w_pa, w_pb=w_pb, w_o=w_o,
             g_post=g_post)
    M = dict(w_mod=m_w_mod, b_mod=m_b_mod, g_pre=m_g_pre, w_in=m_w_in, conv_w=m_conv_w, conv_b=m_conv_b,
             w_rg=m_w_rg, b_rg=m_b_rg, w_ig=m_w_ig, b_ig=m_b_ig, lru_lambda=m_lru_lambda, w_pa=m_w_pa,
             w_pb=m_w_pb, w_o=m_w_o, g_post=m_g_post)
    V = dict(w_mod=v_w_mod, b_mod=v_b_mod, g_pre=v_g_pre, w_in=v_w_in, conv_w=v_conv_w, conv_b=v_conv_b,
             w_rg=v_w_rg, b_rg=v_b_rg, w_ig=v_w_ig, b_ig=v_b_ig, lru_lambda=v_lru_lambda, w_pa=v_w_pa,
             w_pb=v_w_pb, w_o=v_w_o, g_post=v_g_post)
    S = x.shape[1]
    me = 4 * lax.axis_index("x") + 2 * lax.axis_index("y") + lax.axis_index("c")
    n_mod = w_mod.shape[2]
    n_in = w_in.shape[2]
    n_conv = conv_w.shape[2]

    c_rows = jnp.broadcast_to(c, (8, D_MODEL))
    w_in_b, w_pa_b, w_pb_b, w_o_b = (t.astype(BF16) for t in (w_in, w_pa, w_pb, w_o))
    g_c, g_win0, g_cw = _gather_two_level([c_rows, w_in_b[0], conv_w], "gather_weights")
    c_all = g_c[:, 0, :]
    c_pad = jnp.concatenate([c_all, jnp.zeros((8, D_MODEL), F32)], axis=0)
    conv_w_full = jnp.transpose(g_cw, (1, 2, 0, 3)).reshape(2, CONV_WIDTH, D_MODEL)

    mod_cols = _mod_fwd(c_pad, w_mod)
    mod_src = jnp.transpose(mod_cols[:, :8], (1, 0, 2))
    mod_src = jnp.concatenate([mod_src, jnp.zeros((8, 6, n_mod), F32)], axis=1)
    (mod_got,) = _exchange([mod_src], ["a2a"], "scatter_mod")
    mod = jnp.transpose(mod_got[:, :2], (1, 0, 2)).reshape(2, 3 * D_MODEL) + b_mod

    def layer_params(layer, w_in_g, mats_g):
        return _layer_params(layer, w_in_g, mats_g, conv_w_full[layer], conv_b[layer], b_rg[layer],
                             b_ig[layer], lru_lambda[layer], w_rg[layer], w_ig[layer], g_pre[layer],
                             g_post[layer])

    layers = [layer_params(0, g_win0, None), None]
    (act,), sv0, (g_win1,), _, mats1_g = _layer_fwd(
        x[0], mod[0].reshape(3, D_MODEL), layers[0], own_mats=(w_pa_b[0], w_pb_b[0], w_o_b[0]),
        host_proj=([w_in_b[1]], ["ag"]), host_out=([w_pa_b[1], w_pb_b[1], w_o_b[1]], ["ag"] * 3))
    layers[1] = layer_params(1, g_win1, mats1_g)
    (dy, loss_acc), sv1, _, _, _ = _layer_fwd(act, mod[1].reshape(3, D_MODEL), layers[1],
                                              target=loss_target[0])

    dy, small1, got1 = _layer_bwd(dy, sv1, layers[1])
    dy, small0, got0 = _layer_bwd(dy, sv0, layers[0], upper_small=small1)
    grad_x = dy[None]
    r_pa, r_pb, r_o = ([got0["mats"][k], got1["mats"][k]] for k in range(3))
    r_in = [got0["w_in"], got1["w_in"]]
    grads = [small0, small1]

    def stack2(name):
        return jnp.stack([grads[0][name], grads[1][name]], axis=0)

    dmod = stack2("dmod")
    dmod_src = jnp.transpose(dmod.reshape(2, N_DEV, n_mod), (1, 0, 2))
    dmod_src = jnp.concatenate([dmod_src, jnp.zeros((8, 6, n_mod), F32)], axis=1)
    mod_vec = _flat_rows([dmod, stack2("g_pre"), stack2("g_post"), loss_acc])
    r_dmod, r_mod_small = _exchange([dmod_src, mod_vec], ["a2a", "ag"], "exchange_grads")
    mod_rows = mod_vec.shape[0] - loss_acc.shape[0]
    loss = 0.5 * jnp.sum(r_mod_small[:, mod_rows, 0]) / D_MODEL
    r_lru_small = got0["lru_small"]

    res = {}
    res["w_in"] = _adamw(r_in, w_in, m_w_in, v_w_in, 8, "adamw_w_in")
    res["w_pa"] = _adamw(r_pa, w_pa, m_w_pa, v_w_pa, 8, "adamw_w_pa")
    res["w_pb"] = _adamw(r_pb, w_pb, m_w_pb, v_w_pb, 8, "adamw_w_pb")
    res["w_o"] = _adamw(r_o, w_o, m_w_o, v_w_o, 8, "adamw_w_o")
    dmod_all = jnp.transpose(r_dmod[:, :2], (1, 0, 2))
    gw_mod = _mod_wgrad(jnp.transpose(c_all), dmod_all)
    res["w_mod"] = _adamw([gw_mod[0], gw_mod[1]], w_mod, m_w_mod, v_w_mod, 0, "adamw_w_mod")
    for names_rep, stack, tag in ((LRU_SMALL[:3], r_lru_small[0], "lru_a"), (LRU_SMALL[3:], r_lru_small[1], "lru_b"),
                                  (MOD_SMALL, r_mod_small, "mod")):
        rows = sum(W[n].size for n in names_rep) // 128
        rep = _adamw([stack], *(_flat_rows([src[n] for n in names_rep])[None] for src in (W, M, V)), 8,
                     f"adamw_small_{tag}", rows=rows // 2 if rows % 16 == 0 else rows)
        off = 0
        for name in names_rep:
            size = W[name].size
            res[name] = tuple(t.reshape(-1)[off:off + size].reshape(W[name].shape) for t in rep)
            off += size
    lru_rows = sum(W[n].size for n in LRU_SMALL[3:]) // 128
    conv_stack = r_lru_small[1][:, lru_rows:].reshape(8, 2, CONV_WIDTH, D_MODEL)
    conv_stack = lax.dynamic_slice_in_dim(conv_stack, me * n_conv, n_conv, axis=3).reshape(8, 8, n_conv)
    res["conv_w"] = _adamw([conv_stack], conv_w.reshape(1, 8, n_conv), m_conv_w.reshape(1, 8, n_conv),
                           v_conv_w.reshape(1, 8, n_conv), 8, "adamw_conv_w")

    names = ("w_mod", "b_mod", "g_pre", "w_in", "conv_w", "conv_b", "w_rg", "b_rg", "w_ig", "b_ig",
             "lru_lambda", "w_pa", "w_pb", "w_o", "g_post")
    outs = [loss, grad_x]
    for k in range(4):
        outs.extend(res[n][k].reshape(W[n].shape) for n in names)
    return tuple(outs)
```

```python
import functools

import jax
import jax.numpy as jnp
import numpy as np
from jax import lax
from jax.experimental import pallas as pl
from jax.experimental.pallas import tpu as pltpu

F32 = jnp.float32
BF16 = jnp.bfloat16

N_DEV = 8
D_MODEL = 1024
HEAD_DIM = 128
HEADS = 4
ATT_WIDTH = HEADS * HEAD_DIM
DILATIONS = (1, 4, 16)
BAND = 128
N_CHUNKS = 18
QKV_CHUNKS = 9
CONV_WIDTH = 4
LRU_C = 8.0
NORM_EPS = 1e-6
NEG_INF = -1e30
ADAM_LR = 0.001
ADAM_B1 = 0.9
ADAM_B2 = 0.999
ADAM_EPS = 1e-08
ADAM_WD = 0.01
ADAM_STEP = 10
CHUNK_PERM = (0, 3, 6, 1, 4, 7, 2, 5, 8, 10, 11, 12, 13, 14, 15, 16, 17, 9)
DREST_CHUNKS = 10
VMEM_LIMIT = 56 * 1024 * 1024


def _params(sem=None):
    return pltpu.CompilerParams(dimension_semantics=sem, vmem_limit_bytes=VMEM_LIMIT)


def _silu(x):
    return x * jax.nn.sigmoid(x)


def _dsilu(x):
    s = jax.nn.sigmoid(x)
    return s * (1.0 + x * (1.0 - s))


def _neg_expm1(x):
    series = -x * (1.0 + x * (0.5 + x * (1.0 / 6.0 + x * (1.0 / 24.0))))
    return jnp.where(x > -0.05, series, 1.0 - jnp.exp(x))


def _softplus_neg(lam):
    z = jnp.exp(-jnp.abs(lam))
    small = z * (1.0 - z * (0.5 - z * (1.0 / 3.0 - z * 0.25)))
    log1p_z = jnp.where(z < 1e-2, small, jnp.log(1.0 + z))
    return jnp.maximum(-lam, 0.0) + log1p_z


def _dot(a, b):
    return jnp.dot(a, b, preferred_element_type=F32)


def _dot_nt(a, b):
    return lax.dot_general(a, b, (((1,), (1,)), ((), ())), preferred_element_type=F32)


def _dot_tn(a, b):
    return lax.dot_general(a, b, (((0,), (0,)), ((), ())), preferred_element_type=F32)


def _perm_matrix(r, rows, transpose=False):
    n = rows // r
    p = np.zeros((rows, rows), np.float32)
    dst = np.arange(rows)
    p[dst, (dst % n) * r + dst // n] = 1.0
    return jnp.asarray(p.T if transpose else p, dtype=BF16)


def _permute_f32(p, x, pieces):
    part = x.astype(BF16)
    acc = _dot(p, part)
    for _ in range(pieces - 1):
        x = x - part.astype(F32)
        part = x.astype(BF16)
        acc = acc + _dot(p, part)
    return acc


def _rows_of_residues(ref, lead=()):
    r = ref.shape[len(lead)]
    if r == 1:
        return ref[lead + (0,)]
    return jnp.concatenate([ref[lead + (rho,)] for rho in range(r)], axis=0)


def _store_residues(ref, x, lead=()):
    r = ref.shape[len(lead)]
    n = x.shape[0] // r
    for rho in range(r):
        ref[lead + (rho,)] = x[rho * n:(rho + 1) * n]


ANY_SPEC = pl.BlockSpec(memory_space=pl.ANY)


def _exchange_shapes(arrs, modes):
    return [jax.ShapeDtypeStruct(((N_DEV,) + a.shape) if mode == "ag" else a.shape, a.dtype)
            for a, mode in zip(arrs, modes)]


def _exchange_sems(n):
    return [pltpu.SemaphoreType.DMA((7 * n,)), pltpu.SemaphoreType.DMA((7 * n,)),
            pltpu.SemaphoreType.DMA((n,))]


def _exchange_copies(ins, outs, modes, sems):
    n = len(ins)
    send_sems, recv_sems, local_sems = sems
    x, y, c = lax.axis_index("x"), lax.axis_index("y"), lax.axis_index("c")
    me = 4 * x + 2 * y + c

    def src_for(a, dev):
        return ins[a] if modes[a] == "ag" else ins[a].at[dev]

    local = [pltpu.make_async_copy(src_for(a, me), outs[a].at[me], local_sems.at[a]) for a in range(n)]
    sends, arrivals = [], []
    for k in range(1, N_DEV):
        px = 1 - x if (k >> 2) & 1 else x
        py = 1 - y if (k >> 1) & 1 else y
        pc = 1 - c if k & 1 else c
        peer = 4 * px + 2 * py + pc
        for a in range(n):
            s = (k - 1) * n + a
            for dst, group in ((me, sends), (peer, arrivals)):
                group.append(pltpu.make_async_remote_copy(
                    src_ref=src_for(a, peer), dst_ref=outs[a].at[dst],
                    send_sem=send_sems.at[s], recv_sem=recv_sems.at[s],
                    device_id=(px, py, pc), device_id_type=pl.DeviceIdType.MESH))
    return local, sends, arrivals


def _exchange_start(copies):
    local, sends, _ = copies
    for cp in local + sends:
        cp.start()


def _exchange_wait(copies):
    local, sends, arrivals = copies
    for cp in arrivals:
        cp.wait_recv()
    for cp in sends:
        cp.wait_send()
    for cp in local:
        cp.wait()


def _exchange(arrs, modes, name):
    n = len(arrs)

    def body(*refs):
        copies = _exchange_copies(refs[:n], refs[n:2 * n], modes, refs[2 * n:])
        _exchange_start(copies)
        _exchange_wait(copies)

    outs = pl.pallas_call(
        body, name=name, out_shape=tuple(_exchange_shapes(arrs, modes)),
        in_specs=[ANY_SPEC] * n, out_specs=tuple([ANY_SPEC] * n),
        scratch_shapes=_exchange_sems(n),
    )(*arrs)
    return list(outs)


def _gather_two_level(arrs, name):
    n = len(arrs)

    def body(*refs):
        ins, outs = refs[:n], refs[n:2 * n]
        send_sems, recv_sems, local_sems = refs[2 * n:]
        x, y, c = lax.axis_index("x"), lax.axis_index("y"), lax.axis_index("c")
        me, sibling = (x, y, c), (x, y, 1 - c)
        chips = [(1 - x, y), (x, 1 - y), (1 - x, 1 - y)]

        def copy(a, k, block, to, src=None):
            slot = 4 * block[0] + 2 * block[1] + block[2]
            return pltpu.make_async_remote_copy(
                src_ref=outs[a].at[slot] if src is None else src, dst_ref=outs[a].at[slot],
                send_sem=send_sems.at[7 * a + k], recv_sem=recv_sems.at[7 * a + k],
                device_id=to, device_id_type=pl.DeviceIdType.MESH)

        mine = [pltpu.make_async_copy(ins[a], outs[a].at[4 * x + 2 * y + c], local_sems.at[a])
                for a in range(n)]
        first = []
        for a in range(n):
            first.append(copy(a, 0, me, sibling, src=ins[a]))
            first += [copy(a, 1 + j, me, (*chip, c), src=ins[a]) for j, chip in enumerate(chips)]
        for cp in mine + first:
            cp.start()
        passed = []
        for j, chip in enumerate(chips):
            for a in range(n):
                copy(a, 1 + j, (*chip, c), me).wait_recv()
                passed.append(copy(a, 4 + j, (*chip, c), sibling))
                passed[-1].start()
        for a in range(n):
            copy(a, 0, sibling, me).wait_recv()
            for j, chip in enumerate(chips):
                copy(a, 4 + j, (*chip, 1 - c), me).wait_recv()
        for cp in first + passed:
            cp.wait_send()
        for cp in mine:
            cp.wait()

    outs = pl.pallas_call(
        body, name=name, out_shape=tuple(_exchange_shapes(arrs, ["ag"] * n)),
        in_specs=[ANY_SPEC] * n, out_specs=tuple([ANY_SPEC] * n),
        scratch_shapes=_exchange_sems(n),
    )(*arrs)
    return list(outs)


SHARD_COLS = N_CHUNKS * ATT_WIDTH // N_DEV


def _weights_full(shards):
    return jnp.transpose(shards, (1, 0, 2)).reshape(D_MODEL, N_CHUNKS * ATT_WIDTH)


def _wcols(w_ref, k):
    orig = CHUNK_PERM[k]
    return w_ref[:, orig * ATT_WIDTH:(orig + 1) * ATT_WIDTH]


def _store_shard_columns(stage_ref, shards_ref, sems, first_chunk):
    copies = []
    for t in range(stage_ref.shape[1] // ATT_WIDTH):
        col = ATT_WIDTH * CHUNK_PERM[first_chunk + t]
        end = col + ATT_WIDTH
        while col < end:
            dev = col // SHARD_COLS
            upto = min(end, (dev + 1) * SHARD_COLS)
            src = stage_ref.at[:, pl.ds(t * ATT_WIDTH + ATT_WIDTH - (end - col), upto - col)]
            dst = shards_ref.at[dev, :, pl.ds(col - dev * SHARD_COLS, upto - col)]
            copies.append(pltpu.make_async_copy(src, dst, sems.at[len(copies)]))
            col = upto
    for cp in copies:
        cp.start()
    for cp in copies:
        cp.wait()


SHARD_COPIES_PER_CHUNK = 2


def _call(body, args, *, name, grid, in_specs, out_specs, out_shape, scratch_shapes=(), aliases=None,
          hosted=None):
    sem = ("arbitrary",) * len(grid)
    if hosted is None:
        outs = pl.pallas_call(
            body, name=name, grid=grid, in_specs=in_specs, out_specs=tuple(out_specs),
            out_shape=tuple(out_shape), scratch_shapes=list(scratch_shapes),
            input_output_aliases=aliases or {}, compiler_params=_params(sem))(*args)
        return list(outs), []
    x_arrs, modes = hosted
    n_in, n_out, n_scr, nx = len(args), len(out_shape), len(scratch_shapes), len(x_arrs)

    def wrapped(*refs):
        ins, x_ins = refs[:n_in], refs[n_in:n_in + nx]
        outs = refs[n_in + nx:n_in + nx + n_out]
        x_outs = refs[n_in + nx + n_out:n_in + 2 * nx + n_out]
        scr = refs[n_in + 2 * nx + n_out:n_in + 2 * nx + n_out + n_scr]
        sems = refs[n_in + 2 * nx + n_out + n_scr:]
        first = pl.program_id(0) == 0
        last = pl.program_id(0) == grid[0] - 1
        for axis in range(1, len(grid)):
            first = jnp.logical_and(first, pl.program_id(axis) == 0)
            last = jnp.logical_and(last, pl.program_id(axis) == grid[axis] - 1)

        @pl.when(first)
        def _():
            _exchange_start(_exchange_copies(x_ins, x_outs, modes, sems))

        body(*ins, *outs, *scr)

        @pl.when(last)
        def _():
            _exchange_wait(_exchange_copies(x_ins, x_outs, modes, sems))

    outs = pl.pallas_call(
        wrapped, name=name, grid=grid, in_specs=list(in_specs) + [ANY_SPEC] * nx,
        out_specs=tuple(out_specs) + tuple([ANY_SPEC] * nx),
        out_shape=tuple(out_shape) + tuple(_exchange_shapes(x_arrs, modes)),
        scratch_shapes=list(scratch_shapes) + _exchange_sems(nx),
        input_output_aliases=aliases or {}, compiler_params=_params(sem))(*args, *x_arrs)
    return list(outs[:n_out]), list(outs[n_out:])


def _mod_fwd(c_pad, w_mod):
    def body(c_ref, w_ref, o_ref):
        sc = _silu(c_ref[...]).astype(BF16)
        for layer in range(2):
            o_ref[layer] = _dot(sc, w_ref[layer].astype(BF16))

    return pl.pallas_call(
        body, name="mod_fwd", out_shape=jax.ShapeDtypeStruct((2, 16, w_mod.shape[2]), F32),
        compiler_params=_params())(c_pad, w_mod)


def _mod_wgrad(c_t, dmod):
    n_cols = dmod.shape[2]

    def body(c_ref, d_ref, o_ref):
        sc = _silu(c_ref[...]).astype(BF16).astype(F32)
        for layer in range(2):
            dm = d_ref[layer].astype(BF16).astype(F32)
            acc = sc[:, 0:1] * dm[0:1, :]
            for b in range(1, N_DEV):
                acc = acc + sc[:, b:b + 1] * dm[b:b + 1, :]
            o_ref[layer] = acc

    return pl.pallas_call(
        body, name="mod_wgrad", out_shape=jax.ShapeDtypeStruct((2, D_MODEL, n_cols), F32),
        compiler_params=_params())(c_t, dmod)


GROUP_COLS = 3 * ATT_WIDTH
PROJ_TM = 256


def _norm_proj_fwd(x, vec, w_int, hosted=None):
    S = x.shape[0]
    tm = PROJ_TM
    perms = [_perm_matrix(r, tm) for r in DILATIONS[1:]]

    def body(x_ref, v_ref, w_ref, p1_ref, p2_ref, ht0_ref, ht1_ref, ht2_ref, q0_ref, q1_ref, q2_ref,
             rest_ref):
        xv = x_ref[...]
        rstd = lax.rsqrt(jnp.mean(xv * xv, axis=-1, keepdims=True) + NORM_EPS)
        hf = ((xv * rstd) * v_ref[0:1, :]) * (1.0 + v_ref[1:2, :]) + v_ref[2:3, :]
        h = hf.astype(BF16)
        for g, (q_ref, ht_ref, p_ref) in enumerate(((q0_ref, ht0_ref, None), (q1_ref, ht1_ref, p1_ref),
                                                    (q2_ref, ht2_ref, p2_ref))):
            rows_f = hf if p_ref is None else _dot(p_ref[...], h)
            ht_ref[...] = rows_f.T.astype(BF16)
            rows = rows_f.astype(BF16)
            n = tm // q_ref.shape[0]
            for t in range(3):
                res = _dot(rows, _wcols(w_ref, 3 * g + t)).astype(BF16)
                for rho in range(q_ref.shape[0]):
                    q_ref[rho, :, t * ATT_WIDTH:(t + 1) * ATT_WIDTH] = res[rho * n:(rho + 1) * n]
        split = (QKV_CHUNKS - 1) * ATT_WIDTH
        rest_ref[:, :split] = _dot(h, w_ref[:, (QKV_CHUNKS + 1) * ATT_WIDTH:])
        rest_ref[:, split:] = _dot(h, w_ref[:, QKV_CHUNKS * ATT_WIDTH:(QKV_CHUNKS + 1) * ATT_WIDTH])

    n_rest = QKV_CHUNKS * ATT_WIDTH
    whole = pl.BlockSpec(memory_space=pltpu.VMEM)
    ht_spec = pl.BlockSpec((D_MODEL, tm), lambda i: (0, i))
    ht_shape = jax.ShapeDtypeStruct((D_MODEL, S), BF16)
    return _call(
        body, (x, vec, w_int, *perms), name="norm_proj_fwd", grid=(S // tm,),
        in_specs=[pl.BlockSpec((tm, D_MODEL), lambda i: (i, 0)),
                  pl.BlockSpec((8, D_MODEL), lambda i: (0, 0)), whole, whole, whole],
        out_specs=(ht_spec, ht_spec, ht_spec,
                   *[pl.BlockSpec((r, tm // r, GROUP_COLS), lambda i: (0, i, 0)) for r in DILATIONS],
                   pl.BlockSpec((tm, n_rest), lambda i: (i, 0))),
        out_shape=(ht_shape, ht_shape, ht_shape,
                   *[jax.ShapeDtypeStruct((r, S // r, GROUP_COLS), BF16) for r in DILATIONS],
                   jax.ShapeDtypeStruct((S, n_rest), F32)),
        hosted=hosted)


def _proj_dgrad_norm_bwd(dqkv, drest, w_int, x, dxn, vec, hosted=None):
    S = x.shape[0]
    tm = PROJ_TM
    perms = [_perm_matrix(r, tm, transpose=True) for r in DILATIONS[1:]]

    def body(d0_ref, d1_ref, d2_ref, dr_ref, w_ref, p1_ref, p2_ref, x_ref, dxn_ref, v_ref, dx_ref, acc_ref):
        i = pl.program_id(0)

        @pl.when(i == 0)
        def _():
            acc_ref[...] = jnp.zeros_like(acc_ref)

        wcols = functools.partial(_wcols, w_ref)
        dhv = None
        for g, (d_ref, p_ref) in enumerate(((d0_ref, None), (d1_ref, p1_ref), (d2_ref, p2_ref))):
            for t in range(3):
                d = _rows_of_residues(d_ref, (t,))
                if p_ref is not None:
                    d = _dot(p_ref[...], d).astype(BF16)
                part = _dot_nt(d, wcols(3 * g + t))
                dhv = part if dhv is None else dhv + part
        for t in range(7):
            dhv = dhv + _dot_nt(dr_ref[t], wcols(11 + t))
        for t in range(2):
            dhv = dhv + _dot_nt(dr_ref[8 + t], wcols(9 + t))

        xv = x_ref[...]
        g = v_ref[0:1, :]
        sc1 = 1.0 + v_ref[1:2, :]
        rstd = lax.rsqrt(jnp.mean(xv * xv, axis=-1, keepdims=True) + NORM_EPS)
        xhat = xv * rstd
        acc_ref[0:1, :] += jnp.sum(dhv, axis=0, keepdims=True)
        acc_ref[1:2, :] += jnp.sum(dhv * (xhat * g), axis=0, keepdims=True)
        acc_ref[2:3, :] += jnp.sum(dhv * xhat * sc1, axis=0, keepdims=True)
        dxhat = dhv * (g * sc1)
        dx = rstd * (dxhat - xhat * jnp.mean(dxhat * xhat, axis=-1, keepdims=True))
        dx_ref[...] = dx + dxn_ref[...]

    row = pl.BlockSpec((tm, D_MODEL), lambda i: (i, 0))
    vec_spec = pl.BlockSpec((8, D_MODEL), lambda i: (0, 0))
    whole = pl.BlockSpec(memory_space=pltpu.VMEM)
    return _call(
        body, (*dqkv, drest, w_int, *perms, x, dxn, vec), name="proj_dgrad_norm_bwd", grid=(S // tm,),
        in_specs=[*[pl.BlockSpec((3, r, tm // r, ATT_WIDTH), lambda i: (0, 0, i, 0)) for r in DILATIONS],
                  pl.BlockSpec((DREST_CHUNKS, tm, ATT_WIDTH), lambda i: (0, i, 0)),
                  whole, whole, whole, row, row, vec_spec],
        out_specs=(row, vec_spec),
        out_shape=(jax.ShapeDtypeStruct((S, D_MODEL), F32), jax.ShapeDtypeStruct((8, D_MODEL), F32)),
        hosted=hosted)


WGRAD_TS = 512


def _proj_wgrad_call(h_t, d, d_spec, rows_of, n_chunks, first_chunk, shards, name):
    S = h_t.shape[1]
    n_steps = S // WGRAD_TS
    width = n_chunks * ATT_WIDTH

    def body(*refs):
        h_ref, d_ref = refs[:2]
        shards_ref, acc_ref, stage_ref, sems = refs[-4:]
        i = pl.program_id(0)

        @pl.when(i == 0)
        def _():
            acc_ref[...] = jnp.zeros_like(acc_ref)

        ht = h_ref[...]
        for t in range(n_chunks):
            acc_ref[:, t * ATT_WIDTH:(t + 1) * ATT_WIDTH] += _dot(ht, rows_of(d_ref, t))

        @pl.when(i == n_steps - 1)
        def _():
            stage_ref[...] = acc_ref[...].astype(BF16)
            _store_shard_columns(stage_ref, shards_ref, sems, first_chunk)

    chained = shards is not None
    return pl.pallas_call(
        body, name=name, grid=(n_steps,),
        in_specs=[pl.BlockSpec((D_MODEL, WGRAD_TS), lambda i: (0, i)), d_spec] + [ANY_SPEC] * chained,
        out_specs=ANY_SPEC,
        out_shape=jax.ShapeDtypeStruct((N_DEV, D_MODEL, SHARD_COLS), BF16),
        scratch_shapes=[pltpu.VMEM((D_MODEL, width), F32), pltpu.VMEM((D_MODEL, width), BF16),
                        pltpu.SemaphoreType.DMA((SHARD_COPIES_PER_CHUNK * n_chunks,))],
        input_output_aliases={2: 0} if chained else {},
        compiler_params=_params(("arbitrary",)))(h_t, d, *([shards] * chained))


def _proj_wgrad_part(h_t, d, n_chunks, chunk_block, first_chunk, shards, name):
    spec = pl.BlockSpec((n_chunks, WGRAD_TS, ATT_WIDTH), lambda i: (chunk_block, i, 0))
    return _proj_wgrad_call(h_t, d, spec, lambda d_ref, t: d_ref[t], n_chunks, first_chunk, shards, name)


def _proj_wgrad_group(h_t, d, g, r, shards, name):
    n = PROJ_TM // r

    def rows_of(d_ref, t):
        if r == 1:
            return d_ref[t, 0]
        return jnp.concatenate([d_ref[t, rho, s * n:(s + 1) * n] for s in range(WGRAD_TS // PROJ_TM)
                                for rho in range(r)], axis=0)

    spec = pl.BlockSpec((3, r, WGRAD_TS // r, ATT_WIDTH), lambda i: (0, 0, i, 0))
    return _proj_wgrad_call(h_t, d, spec, rows_of, 3, 3 * g, shards, name)


def _wgrad_tn(a, b, name):
    S, M = a.shape
    N = b.shape[1]
    ts = 1024

    def body(a_ref, b_ref, o_ref):
        i = pl.program_id(0)
        part = _dot_tn(a_ref[...], b_ref[...])

        @pl.when(i == 0)
        def _():
            o_ref[...] = part

        @pl.when(i > 0)
        def _():
            o_ref[...] += part

    return pl.pallas_call(
        body, name=name, grid=(S // ts,),
        in_specs=[pl.BlockSpec((ts, M), lambda i: (i, 0)), pl.BlockSpec((ts, N), lambda i: (i, 0))],
        out_specs=pl.BlockSpec((M, N), lambda i: (0, 0)),
        out_shape=jax.ShapeDtypeStruct((M, N), F32),
        compiler_params=_params(("arbitrary",)))(a, b)


ATTN_FWD_BLOCKS = 8


def _attn_fwd(qkv, g, r, hosted=None):
    L = qkv.shape[1]
    qb = min(ATTN_FWD_BLOCKS, L // BAND)
    n_steps = L // (qb * BAND)
    scale = HEAD_DIM ** -0.5

    def body(q_ref, kp_ref, kc_ref, vp_ref, vc_ref, o_ref, st_ref):
        m_step = pl.program_id(1)
        ii = lax.broadcasted_iota(jnp.int32, (BAND, 2 * BAND), 0)
        kk = lax.broadcasted_iota(jnp.int32, (BAND, 2 * BAND), 1)
        band = jnp.logical_and(kk >= ii, kk <= ii + BAND)
        first_mask = jnp.logical_and(band, jnp.logical_or(kk >= BAND, m_step > 0))
        lane = lax.broadcasted_iota(jnp.int32, (BAND, 128), 1)
        stats = [jnp.zeros((BAND, 128), F32) for _ in range(qb)]
        for h in range(HEADS):
            sl = slice(h * HEAD_DIM, (h + 1) * HEAD_DIM)
            kc, vc = kc_ref[:, sl], vc_ref[:, sl]
            for b in range(qb):
                rows = slice(b * BAND, (b + 1) * BAND)
                if b == 0:
                    keys = jnp.concatenate([kp_ref[:, sl], kc[:BAND]], axis=0)
                    vals = jnp.concatenate([vp_ref[:, sl], vc[:BAND]], axis=0)
                else:
                    keys, vals = kc[(b - 1) * BAND:(b + 1) * BAND], vc[(b - 1) * BAND:(b + 1) * BAND]
                s = jnp.where(first_mask if b == 0 else band, _dot_nt(q_ref[rows, sl], keys) * scale, NEG_INF)
                m = jnp.max(s, axis=1, keepdims=True)
                p = jnp.exp(s - m)
                l = jnp.sum(p, axis=1, keepdims=True)
                o_ref[rows, sl] = _dot(p.astype(BF16), vals) / l
                stats[b] = jnp.where(lane == h, m, stats[b])
                stats[b] = jnp.where(lane == HEADS + h, l, stats[b])
        for b in range(qb):
            st_ref[b * BAND:(b + 1) * BAND, :] = stats[b]

    many = (None, qb * BAND, ATT_WIDTH)
    one = (None, BAND, ATT_WIDTH)

    def prev(m):
        return jnp.maximum(qb * m - 1, 0)

    return _call(
        body, (qkv, qkv, qkv, qkv, qkv), name=f"attn_fwd_g{g}", grid=(r, n_steps),
        in_specs=[
            pl.BlockSpec(many, lambda rho, m: (rho, m, 0)),
            pl.BlockSpec(one, lambda rho, m: (rho, prev(m), 1)),
            pl.BlockSpec(many, lambda rho, m: (rho, m, 1)),
            pl.BlockSpec(one, lambda rho, m: (rho, prev(m), 2)),
            pl.BlockSpec(many, lambda rho, m: (rho, m, 2)),
        ],
        out_specs=(pl.BlockSpec(many, lambda rho, m: (rho, m, 0)),
                   pl.BlockSpec((None, qb * BAND, 128), lambda rho, m: (rho, m, 0))),
        out_shape=(jax.ShapeDtypeStruct((r, L, ATT_WIDTH), F32),
                   jax.ShapeDtypeStruct((r, L, 128), F32)),
        hosted=hosted)


def _attn_combine(outs, stats, rest):
    S = rest.shape[0]
    tm = 256
    gatt_blk = 8
    back = [_perm_matrix(r, tm, transpose=True) for r in DILATIONS[1:]]
    forth = [_perm_matrix(r, tm) for r in DILATIONS[1:]]

    def body(o0_ref, o1_ref, o2_ref, s0_ref, s1_ref, s2_ref, g_ref, b1_ref, b2_ref, f1_ref, f2_ref,
             o_ref, a_ref, l0_ref, l1_ref, l2_ref):
        outs_nat = [o0_ref[0]] + [_permute_f32(b_ref[...], _rows_of_residues(o_g), 2)
                                  for o_g, b_ref in ((o1_ref, b1_ref), (o2_ref, b2_ref))]
        st = [s0_ref[0]] + [_permute_f32(b_ref[...], _rows_of_residues(s_g), 3)
                            for s_g, b_ref in ((s1_ref, b1_ref), (s2_ref, b2_ref))]
        lane = lax.broadcasted_iota(jnp.int32, (tm, 128), 1)
        lse_out = jnp.zeros((tm, 128), F32)
        for h in range(HEADS):
            sl = slice(h * HEAD_DIM, (h + 1) * HEAD_DIM)
            ms = [s[:, h:h + 1] for s in st]
            ls = [s[:, HEADS + h:HEADS + h + 1] for s in st]
            m_all = jnp.maximum(jnp.maximum(ms[0], ms[1]), ms[2])
            ws = [l * jnp.exp(m - m_all) for m, l in zip(ms, ls)]
            den = ws[0] + ws[1] + ws[2]
            o = (ws[0] * outs_nat[0][:, sl] + ws[1] * outs_nat[1][:, sl] + ws[2] * outs_nat[2][:, sl]) / den
            o_ref[:, sl] = o
            a_ref[:, sl] = (o * _silu(g_ref[:, sl])).astype(BF16)
            lse_out = jnp.where(lane == h, m_all + jnp.log(den), lse_out)
        l0_ref[0] = lse_out
        for l_ref, f_ref in ((l1_ref, f1_ref), (l2_ref, f2_ref)):
            _store_residues(l_ref, _permute_f32(f_ref[...], lse_out, 3))

    o_spec = pl.BlockSpec((tm, ATT_WIDTH), lambda i: (i, 0))
    whole = pl.BlockSpec(memory_space=pltpu.VMEM)

    def res_spec(r, width):
        return pl.BlockSpec((r, tm // r, width), lambda i: (0, i, 0))

    return pl.pallas_call(
        body, name="attn_combine", grid=(S // tm,),
        in_specs=[*[res_spec(r, ATT_WIDTH) for r in DILATIONS], *[res_spec(r, 128) for r in DILATIONS],
                  pl.BlockSpec((tm, ATT_WIDTH), lambda i: (i, gatt_blk)), whole, whole, whole, whole],
        out_specs=(o_spec, o_spec, *[res_spec(r, 128) for r in DILATIONS]),
        out_shape=(jax.ShapeDtypeStruct((S, ATT_WIDTH), F32), jax.ShapeDtypeStruct((S, ATT_WIDTH), BF16),
                   *[jax.ShapeDtypeStruct((r, S // r, 128), F32) for r in DILATIONS]),
        compiler_params=_params(("parallel",)))(*outs, *stats, rest, *back, *forth)


def _attn_bwd(qkv, do, lse, dvec, g, r, hosted=None):
    L = qkv.shape[1]
    nb = L // BAND
    nb2 = nb // 2
    scale = HEAD_DIM ** -0.5

    def body(qc_ref, qn_ref, k_ref, v_ref, doc_ref, don_ref, lc_ref, ln_ref, dc_ref, dn_ref,
             out_ref, carry_ref):
        j = pl.program_id(1)

        @pl.when(j == 0)
        def _():
            carry_ref[...] = jnp.zeros_like(carry_ref)

        ii = lax.broadcasted_iota(jnp.int32, (3 * BAND, 2 * BAND), 0)
        kk = lax.broadcasted_iota(jnp.int32, (3 * BAND, 2 * BAND), 1)
        mask = jnp.logical_and(jnp.logical_and(kk <= ii, kk >= ii - BAND),
                               jnp.logical_or(ii < 2 * BAND, j < nb2 - 1))
        lse3 = jnp.concatenate([lc_ref[...], ln_ref[...]], axis=0)
        dvec3 = jnp.concatenate([dc_ref[...], dn_ref[...]], axis=0)
        for h in range(HEADS):
            sl = slice(h * HEAD_DIM, (h + 1) * HEAD_DIM)
            k = k_ref[:, sl]
            v = v_ref[:, sl]
            q = jnp.concatenate([qc_ref[:, sl], qn_ref[:, sl]], axis=0)
            do = jnp.concatenate([doc_ref[:, sl], don_ref[:, sl]], axis=0)
            p = jnp.where(mask, jnp.exp(_dot_nt(q, k) * scale - lse3[:, h:h + 1]), 0.0)
            ds = (p * (_dot_nt(do, v) - dvec3[:, h:h + 1])).astype(BF16)
            dq3 = _dot(ds, k) * scale
            out_ref[0, 0:BAND, sl] = (carry_ref[:, sl] + dq3[:BAND]).astype(BF16)
            out_ref[0, BAND:, sl] = dq3[BAND:2 * BAND].astype(BF16)
            out_ref[1, :, sl] = (_dot_tn(ds, q) * scale).astype(BF16)
            out_ref[2, :, sl] = _dot_tn(p.astype(BF16), do).astype(BF16)
            carry_ref[:, sl] = dq3[2 * BAND:]

    two = (None, 2 * BAND, ATT_WIDTH)
    one = (None, BAND, ATT_WIDTH)
    stwo = (None, 2 * BAND, 128)
    sone = (None, BAND, 128)

    def nxt(j):
        return jnp.minimum(2 * j + 2, nb - 1)

    (out,), got = _call(
        body, (qkv, qkv, qkv, qkv, do, do, lse, lse, dvec, dvec), name=f"attn_bwd_g{g}", grid=(r, nb2),
        in_specs=[
            pl.BlockSpec(two, lambda rho, j: (rho, j, 0)),
            pl.BlockSpec(one, lambda rho, j: (rho, nxt(j), 0)),
            pl.BlockSpec(two, lambda rho, j: (rho, j, 1)),
            pl.BlockSpec(two, lambda rho, j: (rho, j, 2)),
            pl.BlockSpec(two, lambda rho, j: (rho, j, 0)),
            pl.BlockSpec(one, lambda rho, j: (rho, nxt(j), 0)),
            pl.BlockSpec(stwo, lambda rho, j: (rho, j, 0)),
            pl.BlockSpec(sone, lambda rho, j: (rho, nxt(j), 0)),
            pl.BlockSpec(stwo, lambda rho, j: (rho, j, 0)),
            pl.BlockSpec(sone, lambda rho, j: (rho, nxt(j), 0)),
        ],
        out_specs=(pl.BlockSpec((3, None, 2 * BAND, ATT_WIDTH), lambda rho, j: (0, rho, j, 0)),),
        out_shape=(jax.ShapeDtypeStruct((3, r, L, ATT_WIDTH), BF16),),
        scratch_shapes=[pltpu.VMEM((BAND, ATT_WIDTH), F32)],
        hosted=hosted)
    return out, got


LRU_T = 256


def _linear_scan(a, b, carry, reverse):
    T, C = a.shape
    a = a.reshape(T // 8, 8, C)
    b = b.reshape(T // 8, 8, C)
    row8 = lax.broadcasted_iota(jnp.int32, a.shape, 1)
    for s in (1, 2, 4):
        keep = (row8 < 8 - s) if reverse else (row8 >= s)
        shift = 8 - s if reverse else s
        b_sh = jnp.where(keep, pltpu.roll(b, shift, 1), 0.0)
        a_sh = jnp.where(keep, pltpu.roll(a, shift, 1), 1.0)
        b = a * b_sh + b
        a = a * a_sh
    tiles = [None] * (T // 8)
    order = range(T // 8 - 1, -1, -1) if reverse else range(T // 8)
    for k in order:
        y = b[k] + a[k] * carry
        tiles[k] = y
        carry = y[0:1] if reverse else y[7:8]
    return jnp.concatenate(tiles, axis=0), carry


def _gate_matmuls(ucb, w_ref, bias):
    parts = [_dot(ucb[:, j * 128:(j + 1) * 128], w_ref[j]) for j in range(8)]
    return jnp.concatenate(parts, axis=1) + bias


def _rows_before(x, prev8, j):
    T, C = x.shape
    rot = pltpu.roll(x.reshape(T // 8, 8, C), j, 1)
    rot_prev = jnp.concatenate([pltpu.roll(prev8, j, 0)[None], rot[:-1]], axis=0)
    row8 = lax.broadcasted_iota(jnp.int32, rot.shape, 1)
    return jnp.where(row8 >= j, rot, rot_prev).reshape(T, C)


def _rows_after(x, next8, j):
    T, C = x.shape
    rot = pltpu.roll(x.reshape(T // 8, 8, C), 8 - j, 1)
    rot_next = jnp.concatenate([rot[1:], pltpu.roll(next8, 8 - j, 0)[None]], axis=0)
    row8 = lax.broadcasted_iota(jnp.int32, rot.shape, 1)
    return jnp.where(row8 < 8 - j, rot, rot_next).reshape(T, C)


def _conv_fwd(u, u_prev8, cw_ref):
    uc = cw_ref[4:5, :] + cw_ref[0:1, :] * u
    for j in range(1, CONV_WIDTH):
        uc = uc + cw_ref[j:j + 1, :] * _rows_before(u, u_prev8, j)
    return uc


def _lru_fwd(rest, cw, wr2, wi2, hosted=None):
    S = rest.shape[0]
    T = LRU_T

    def body(u_ref, cw_ref, wr_ref, wi_ref, h_ref, a_ref, uc_ref, r_ref, ig_ref, mult_ref,
             ucar_ref, hcar_ref):
        c = pl.program_id(0)

        @pl.when(c == 0)
        def _():
            ucar_ref[...] = jnp.zeros_like(ucar_ref)
            hcar_ref[...] = jnp.zeros_like(hcar_ref)

        u = u_ref[...]
        uc = _conv_fwd(u, ucar_ref[...], cw_ref)
        ucar_ref[...] = u[T - 8:, :]
        ucb = uc.astype(BF16)
        r = jax.nn.sigmoid(_gate_matmuls(ucb, wr_ref, cw_ref[5:6, :]))
        ig = jax.nn.sigmoid(_gate_matmuls(ucb, wi_ref, cw_ref[6:7, :]))
        log_a = -LRU_C * r * _softplus_neg(cw_ref[7:8, :])
        a = jnp.exp(log_a)
        mult = jnp.sqrt(_neg_expm1(2.0 * log_a))
        b = mult * (ig * uc)
        a_ref[...] = a
        uc_ref[...] = uc
        r_ref[...] = r
        ig_ref[...] = ig
        mult_ref[...] = mult
        h, last = _linear_scan(a, b, hcar_ref[0:1, :], reverse=False)
        h_ref[...] = h
        hcar_ref[...] = jnp.broadcast_to(last, (8, D_MODEL))

    row_spec = pl.BlockSpec((T, D_MODEL), lambda c: (c, 0))
    row_shape = jax.ShapeDtypeStruct((S, D_MODEL), F32)
    return _call(
        body, (rest, cw, wr2, wi2), name="lru_fwd", grid=(S // T,),
        in_specs=[row_spec, pl.BlockSpec((8, D_MODEL), lambda c: (0, 0)),
                  pl.BlockSpec((8, 128, 128), lambda c: (0, 0, 0)),
                  pl.BlockSpec((8, 128, 128), lambda c: (0, 0, 0))],
        out_specs=(row_spec,) * 6, out_shape=(row_shape,) * 6,
        scratch_shapes=[pltpu.VMEM((8, D_MODEL), F32), pltpu.VMEM((8, D_MODEL), F32)],
        hosted=hosted)


def _lru_bwd(dhl, hl, a_all, gates, rest, cw, wr2, wi2, dproj, hosted=None):
    S = rest.shape[0]
    T = LRU_T
    nc = S // T

    def body(dh_ref, h_ref, hp_ref, a_ref, uc_ref, r_ref, ig_ref, mult_ref, u_ref, up_ref, cw_ref, wr_ref,
             wi_ref, _alias, du_ref, acc_ref, gwr_ref, gwi_ref, gcar_ref, acar_ref, dcar_ref):
        step = pl.program_id(0)
        c = nc - 1 - step

        @pl.when(step == 0)
        def _():
            acc_ref[...] = jnp.zeros_like(acc_ref)
            gwr_ref[...] = jnp.zeros_like(gwr_ref)
            gwi_ref[...] = jnp.zeros_like(gwi_ref)
            gcar_ref[...] = jnp.zeros_like(gcar_ref)
            acar_ref[...] = jnp.zeros_like(acar_ref)
            dcar_ref[...] = jnp.zeros_like(dcar_ref)

        u = u_ref[...]
        u_prev = jnp.where(c > 0, up_ref[...], 0.0)
        h_prev8 = jnp.where(c > 0, hp_ref[...], 0.0)
        a = a_ref[...]
        h = h_ref[...]
        uc, r, ig, mult = uc_ref[...], r_ref[...], ig_ref[...], mult_ref[...]
        ucb = uc.astype(BF16)
        sp = _softplus_neg(cw_ref[7:8, :])
        a_next = _rows_after(a, acar_ref[...], 1)
        G, first = _linear_scan(a_next, dh_ref[...], gcar_ref[0:1, :], reverse=True)
        gcar_ref[...] = jnp.broadcast_to(first, (8, D_MODEL))
        acar_ref[...] = jnp.broadcast_to(a[0:1, :], (8, D_MODEL))
        d_a = G * _rows_before(h, h_prev8, 1)
        d_mult = G * (ig * uc)
        d_ig = G * (mult * uc)
        duc = G * (mult * ig)
        d_log_a = d_a * a - d_mult * (a * a) / mult
        d_r = d_log_a * (-LRU_C * sp)
        d_sp = jnp.sum(d_log_a * (-LRU_C * r), axis=0, keepdims=True)
        dpre_r = d_r * r * (1.0 - r)
        dpre_i = d_ig * ig * (1.0 - ig)
        dprb = dpre_r.astype(BF16)
        dpib = dpre_i.astype(BF16)
        back = []
        for j in range(8):
            sl = slice(j * 128, (j + 1) * 128)
            back.append(_dot_nt(dprb[:, sl], wr_ref[j]) + _dot_nt(dpib[:, sl], wi_ref[j]))
            gwr_ref[j] += _dot_tn(ucb[:, sl], dprb[:, sl])
            gwi_ref[j] += _dot_tn(ucb[:, sl], dpib[:, sl])
        duc = duc + jnp.concatenate(back, axis=1)
        du = cw_ref[0:1, :] * duc
        duc_next8 = dcar_ref[...]
        acc_ref[0:1, :] += jnp.sum(duc * u, axis=0, keepdims=True)
        for j in range(1, CONV_WIDTH):
            du = du + cw_ref[j:j + 1, :] * _rows_after(duc, duc_next8, j)
            acc_ref[j:j + 1, :] += jnp.sum(duc * _rows_before(u, u_prev, j), axis=0, keepdims=True)
        dcar_ref[...] = duc[0:8, :]
        acc_ref[4:5, :] += jnp.sum(duc, axis=0, keepdims=True)
        acc_ref[5:6, :] += jnp.sum(dpre_r, axis=0, keepdims=True)
        acc_ref[6:7, :] += jnp.sum(dpre_i, axis=0, keepdims=True)
        acc_ref[7:8, :] += d_sp
        du_ref[0] = du[:, 0:ATT_WIDTH].astype(BF16)
        du_ref[1] = du[:, ATT_WIDTH:].astype(BF16)

    def rev(step):
        return nc - 1 - step

    def prev8(step):
        return jnp.maximum(rev(step) * (T // 8) - 1, 0)

    row_spec = pl.BlockSpec((T, D_MODEL), lambda s: (rev(s), 0))
    prev_spec = pl.BlockSpec((8, D_MODEL), lambda s: (prev8(s), 0))
    vec_spec = pl.BlockSpec((8, D_MODEL), lambda s: (0, 0))
    w_spec = pl.BlockSpec((8, 128, 128), lambda s: (0, 0, 0))
    return _call(
        body, (dhl, hl, hl, a_all, *gates, rest, rest, cw, wr2, wi2, dproj), name="lru_bwd", grid=(nc,),
        in_specs=[row_spec, row_spec, prev_spec, row_spec, row_spec, row_spec, row_spec, row_spec,
                  row_spec, prev_spec, vec_spec, w_spec, w_spec, ANY_SPEC],
        out_specs=(pl.BlockSpec((2, T, ATT_WIDTH), lambda s: (4, rev(s), 0)), vec_spec, w_spec, w_spec),
        out_shape=(jax.ShapeDtypeStruct(dproj.shape, BF16), jax.ShapeDtypeStruct((8, D_MODEL), F32),
                   jax.ShapeDtypeStruct((8, 128, 128), F32), jax.ShapeDtypeStruct((8, 128, 128), F32)),
        scratch_shapes=[pltpu.VMEM((8, D_MODEL), F32), pltpu.VMEM((8, D_MODEL), F32),
                        pltpu.VMEM((8, D_MODEL), F32)],
        aliases={13: 0}, hosted=hosted)


OUT_TM = 256


def _out_fwd(a_gated, hl, rest, x, vec, w_pa, w_pb, w_o, target=None, hosted=None):
    S = x.shape[0]
    tm = OUT_TM
    with_loss = target is not None

    def body(*refs):
        (ag_ref, hl_ref, gl_ref, ma_ref, mb_ref, x_ref, v_ref, wpa_ref, wpb_ref, wo_ref), refs = refs[:10], refs[10:]
        if with_loss:
            t_ref, refs = refs[0], refs[1:]
        xn_ref, mix_ref, ya_ref, yb_ref, bg_ref = refs[:5]
        bg = (hl_ref[...] * _silu(gl_ref[...])).astype(BF16)
        ya = _dot(ag_ref[...], wpa_ref[...])
        yb = _dot(bg, wpb_ref[...])
        mix = (jax.nn.sigmoid(ma_ref[...]) * ya + jax.nn.sigmoid(mb_ref[...]) * yb).astype(BF16)
        out = _dot(mix, wo_ref[...])
        rstd = lax.rsqrt(jnp.mean(out * out, axis=-1, keepdims=True) + NORM_EPS)
        x_new = x_ref[...] + v_ref[1:2, :] * ((out * rstd) * v_ref[0:1, :])
        if with_loss:
            acc_ref = refs[5]

            @pl.when(pl.program_id(0) == 0)
            def _():
                acc_ref[...] = jnp.zeros_like(acc_ref)

            err = x_new - t_ref[...]
            xn_ref[...] = err * (1.0 / D_MODEL)
            part = jnp.sum(jnp.sum(err * err, axis=1, keepdims=True), axis=0, keepdims=True)
            acc_ref[...] += jnp.broadcast_to(part, acc_ref.shape)
        else:
            xn_ref[...] = x_new
        mix_ref[...] = mix
        ya_ref[...] = ya.astype(BF16)
        yb_ref[...] = yb.astype(BF16)
        bg_ref[...] = bg

    def col(j):
        return pl.BlockSpec((tm, D_MODEL), lambda i: (i, j))

    def whole(shape):
        return pl.BlockSpec(shape, lambda i: (0, 0))

    row = col(0)
    bf = jax.ShapeDtypeStruct((S, D_MODEL), BF16)
    loss_in = ([target], [row]) if with_loss else ([], [])
    loss_out = ([jax.ShapeDtypeStruct((8, 128), F32)], [whole((8, 128))]) if with_loss else ([], [])
    return _call(
        body, (a_gated, hl, rest, rest, rest, x, vec, w_pa, w_pb, w_o, *loss_in[0]), name="out_fwd",
        grid=(S // tm,),
        in_specs=[pl.BlockSpec((tm, ATT_WIDTH), lambda i: (i, 0)), row, col(1), col(2), col(3), row,
                  whole((8, D_MODEL)), whole((ATT_WIDTH, D_MODEL)), whole((D_MODEL, D_MODEL)),
                  whole((D_MODEL, D_MODEL)), *loss_in[1]],
        out_specs=(row, row, row, row, row, *loss_out[1]),
        out_shape=(jax.ShapeDtypeStruct((S, D_MODEL), F32), bf, bf, bf, bf, *loss_out[0]),
        hosted=hosted)


def _out_bwd(dxn, mix, ya, yb, hl, rest, o_att, vec, w_pa, w_pb, w_o, hosted=None):
    S = dxn.shape[0]
    tm = OUT_TM
    forth = [_perm_matrix(r, tm) for r in DILATIONS[1:]]

    def body(dxn_ref, mix_ref, ya_ref, yb_ref, hl_ref, gl_ref, ma_ref, mb_ref, ga_ref, o_ref, v_ref,
             wpa_ref, wpb_ref, wo_ref, f1_ref, f2_ref,
             dout_ref, dya_ref, dyb_ref, dhl_ref, dp_ref, acc_ref,
             do0_ref, do1_ref, do2_ref, dv0_ref, dv1_ref, dv2_ref):
        i = pl.program_id(0)

        @pl.when(i == 0)
        def _():
            acc_ref[...] = jnp.zeros_like(acc_ref)

        g_post = v_ref[0:1, :]
        gate = v_ref[1:2, :]
        dxn_v = dxn_ref[...]
        out = _dot(mix_ref[...], wo_ref[...])
        rstd = lax.rsqrt(jnp.mean(out * out, axis=-1, keepdims=True) + NORM_EPS)
        nrm = out * rstd
        acc_ref[0:1, :] += jnp.sum(dxn_v * (nrm * g_post), axis=0, keepdims=True)
        acc_ref[1:2, :] += jnp.sum(dxn_v * gate * nrm, axis=0, keepdims=True)
        dn = dxn_v * (gate * g_post)
        dout = (rstd * (dn - nrm * jnp.mean(dn * nrm, axis=-1, keepdims=True))).astype(BF16)
        dout_ref[...] = dout
        dmix = _dot_nt(dout, wo_ref[...])
        sa = jax.nn.sigmoid(ma_ref[...])
        sb = jax.nn.sigmoid(mb_ref[...])
        dya = (dmix * sa).astype(BF16)
        dyb = (dmix * sb).astype(BF16)
        dya_ref[...] = dya
        dyb_ref[...] = dyb
        dma = dmix * ya_ref[...].astype(F32) * (sa * (1.0 - sa))
        dmb = dmix * yb_ref[...].astype(F32) * (sb * (1.0 - sb))
        d_ag = _dot_nt(dya, wpa_ref[...])
        d_bg = _dot_nt(dyb, wpb_ref[...])
        gl = gl_ref[...]
        hl_v = hl_ref[...]
        dhl_ref[...] = d_bg * _silu(gl)
        dgl = d_bg * hl_v * _dsilu(gl)
        ga = ga_ref[...]
        o = o_ref[...]
        do = d_ag * _silu(ga)
        dga = d_ag * o * _dsilu(ga)
        lane = lax.broadcasted_iota(jnp.int32, (tm, 128), 1)
        dvec = jnp.zeros((tm, 128), F32)
        for h in range(HEADS):
            sl = slice(h * HEAD_DIM, (h + 1) * HEAD_DIM)
            dvec = jnp.where(lane == h, jnp.sum(do[:, sl] * o[:, sl], axis=1, keepdims=True), dvec)
        do_b = do.astype(BF16)
        do0_ref[0] = do_b
        dv0_ref[0] = dvec
        for do_ref, dv_ref, f_ref in ((do1_ref, dv1_ref, f1_ref), (do2_ref, dv2_ref, f2_ref)):
            _store_residues(do_ref, _dot(f_ref[...], do_b).astype(BF16))
            _store_residues(dv_ref, _permute_f32(f_ref[...], dvec, 3))
        dp_ref[0] = dgl[:, 0:ATT_WIDTH].astype(BF16)
        dp_ref[1] = dgl[:, ATT_WIDTH:].astype(BF16)
        dp_ref[2] = dma[:, 0:ATT_WIDTH].astype(BF16)
        dp_ref[3] = dma[:, ATT_WIDTH:].astype(BF16)
        dp_ref[4] = dmb[:, 0:ATT_WIDTH].astype(BF16)
        dp_ref[5] = dmb[:, ATT_WIDTH:].astype(BF16)
        dp_ref[6] = dga.astype(BF16)

    def col(j):
        return pl.BlockSpec((tm, D_MODEL), lambda i: (i, j))

    def whole(shape):
        return pl.BlockSpec(shape, lambda i: (0, 0))

    def res_spec(r, width):
        return pl.BlockSpec((r, tm // r, width), lambda i: (0, i, 0))

    row = col(0)
    att = pl.BlockSpec((tm, ATT_WIDTH), lambda i: (i, 0))
    bf = jax.ShapeDtypeStruct((S, D_MODEL), BF16)
    vmem = pl.BlockSpec(memory_space=pltpu.VMEM)
    return _call(
        body, (dxn, mix, ya, yb, hl, rest, rest, rest, rest, o_att, vec, w_pa, w_pb, w_o, *forth),
        name="out_bwd", grid=(S // tm,),
        in_specs=[row, row, row, row, row, col(1), col(2), col(3),
                  pl.BlockSpec((tm, ATT_WIDTH), lambda i: (i, 8)), att,
                  whole((8, D_MODEL)), whole((ATT_WIDTH, D_MODEL)), whole((D_MODEL, D_MODEL)),
                  whole((D_MODEL, D_MODEL)), vmem, vmem],
        out_specs=(row, row, row, row,
                   pl.BlockSpec((7, tm, ATT_WIDTH), lambda i: (0, i, 0)), whole((8, D_MODEL)),
                   *[res_spec(r, ATT_WIDTH) for r in DILATIONS], *[res_spec(r, 128) for r in DILATIONS]),
        out_shape=(bf, bf, bf, jax.ShapeDtypeStruct((S, D_MODEL), F32),
                   jax.ShapeDtypeStruct((DREST_CHUNKS, S, ATT_WIDTH), BF16),
                   jax.ShapeDtypeStruct((8, D_MODEL), F32),
                   *[jax.ShapeDtypeStruct((r, S // r, ATT_WIDTH), BF16) for r in DILATIONS],
                   *[jax.ShapeDtypeStruct((r, S // r, 128), F32) for r in DILATIONS]),
        hosted=hosted)


def _adamw(gsrcs, w, m, v, n_stack, name, rows=None):
    n_layers, R, C = w.shape
    assert len(gsrcs) == n_layers
    budget = 96 * 1024
    tr = rows
    if tr is None:
        tr = R
        while tr * C > budget and tr % 16 == 0:
            tr //= 2
    assert R % tr == 0 and (tr % 8 == 0 or tr == R)
    c1 = 1.0 - ADAM_B1 ** ADAM_STEP
    c2 = 1.0 - ADAM_B2 ** ADAM_STEP

    def body(*refs):
        g_refs = refs[:n_layers]
        w_ref, m_ref, v_ref, go_ref, d_ref, mo_ref, vo_ref = refs[n_layers:]

        def update(g_ref):
            if n_stack:
                g = g_ref[0].astype(F32)
                for s in range(1, n_stack):
                    g = g + g_ref[s].astype(F32)
            else:
                g = g_ref[...]
            m_new = ADAM_B1 * m_ref[...] + (1.0 - ADAM_B1) * g
            v_new = ADAM_B2 * v_ref[...] + (1.0 - ADAM_B2) * (g * g)
            m_hat = m_new / c1
            v_hat = v_new / c2
            go_ref[...] = g
            d_ref[...] = -ADAM_LR * (m_hat / (jnp.sqrt(v_hat) + ADAM_EPS) + ADAM_WD * w_ref[...])
            mo_ref[...] = m_new
            vo_ref[...] = v_new

        for layer in range(n_layers):
            pl.when(pl.program_id(0) == layer)(functools.partial(update, g_refs[layer]))

    def g_spec(layer):
        def rows_of(l, i):
            return jnp.where(l == layer, i, 0)
        if n_stack:
            return pl.BlockSpec((n_stack, tr, C), lambda l, i: (0, rows_of(l, i), 0))
        return pl.BlockSpec((tr, C), lambda l, i: (rows_of(l, i), 0))

    blk = pl.BlockSpec((None, tr, C), lambda l, i: (l, i, 0))
    shp = jax.ShapeDtypeStruct((n_layers, R, C), F32)
    return pl.pallas_call(
        body, name=name, grid=(n_layers, R // tr),
        in_specs=[g_spec(layer) for layer in range(n_layers)] + [blk, blk, blk],
        out_specs=(blk, blk, blk, blk), out_shape=(shp, shp, shp, shp),
        compiler_params=_params(("arbitrary", "arbitrary")))(*gsrcs, w, m, v)


def _pair_blocks(w):
    w = w.reshape(8, 2, 64, 64)
    z = jnp.zeros((8, 64, 64), w.dtype)
    top = jnp.concatenate([w[:, 0], z], axis=2)
    bot = jnp.concatenate([z, w[:, 1]], axis=2)
    return jnp.concatenate([top, bot], axis=1).astype(BF16)


def _unpair_blocks(g):
    return jnp.stack([g[:, :64, :64], g[:, 64:, 64:]], axis=1).reshape(16, 64, 64)


def _mats_full(w_pa_g, w_pb_g, w_o_g):
    return dict(w_pa=jnp.transpose(w_pa_g, (1, 0, 2)).reshape(ATT_WIDTH, D_MODEL),
                w_pb=w_pb_g.reshape(D_MODEL, D_MODEL), w_o=w_o_g.reshape(D_MODEL, D_MODEL))


def _layer_params(layer, w_in_g, mats_g, conv_w_full, conv_b, b_rg, b_ig, lru_lambda, w_rg, w_ig, g_pre,
                  g_post):
    cw = jnp.concatenate([conv_w_full, conv_b[None], b_rg[None], b_ig[None], lru_lambda[None]], axis=0)
    p = dict(w_int=_weights_full(w_in_g), cw=cw, wr2=_pair_blocks(w_rg), wi2=_pair_blocks(w_ig),
             g_pre=g_pre, g_post=g_post, lru_lambda=lru_lambda)
    if mats_g is not None:
        p.update(_mats_full(*mats_g))
    return p


def _layer_fwd(x, mod, p, host_proj=None, host_lru=None, host_out=None, own_mats=None, target=None):
    zeros = jnp.zeros((5, D_MODEL), F32)
    vec_pre = jnp.concatenate([p["g_pre"][None], mod[1:2], mod[0:1], zeros], axis=0)
    (ht0, ht1, ht2, q0, q1, q2, rest), got_proj = _norm_proj_fwd(x, vec_pre, p["w_int"], hosted=host_proj)
    h_t = [ht0, ht1, ht2]
    hosts = [None, None, None]
    if own_mats is not None:
        hosts[0], hosts[1] = (list(own_mats[:2]), ["ag"] * 2), ([own_mats[2]], ["ag"])
    qkv, outs, stats, got_attn = [q0, q1, q2], [], [], []
    for g, r in enumerate(DILATIONS):
        (o_g, st_g), got = _attn_fwd(qkv[g], g, r, hosted=hosts[g])
        outs.append(o_g)
        stats.append(st_g)
        got_attn += got
    if own_mats is not None:
        p.update(_mats_full(*got_attn))
    o_att, a_gated, *lse = _attn_combine(outs, stats, rest)
    (hl, a_dec, *gates), got_lru = _lru_fwd(rest, p["cw"], p["wr2"], p["wi2"], hosted=host_lru)
    vec_post = jnp.concatenate([p["g_post"][None], mod[2:3], jnp.zeros((6, D_MODEL), F32)], axis=0)
    (x_new, mix, ya, yb, b_gated, *loss_acc), got_out = _out_fwd(
        a_gated, hl, rest, x, vec_post, p["w_pa"], p["w_pb"], p["w_o"], target=target, hosted=host_out)
    saved = dict(x=x, h_t=h_t, qkv=qkv, rest=rest, o_att=o_att, a_gated=a_gated, lse=lse, hl=hl, a_dec=a_dec,
                 gates=gates, mix=mix, ya=ya, yb=yb, b_gated=b_gated, vec_pre=vec_pre, vec_post=vec_post)
    return (x_new, *loss_acc), saved, got_proj, got_lru, got_out


def _mats_sources(g_pa, g_pb, g_o):
    return [jnp.transpose(g_pa.astype(BF16).reshape(ATT_WIDTH, N_DEV, 128), (1, 0, 2)),
            g_pb.astype(BF16).reshape(N_DEV, 128, D_MODEL), g_o.astype(BF16).reshape(N_DEV, 128, D_MODEL)]


LRU_SMALL = ("conv_b", "w_rg", "b_rg", "w_ig", "b_ig", "lru_lambda")
MOD_SMALL = ("b_mod", "g_pre", "g_post")


def _flat_rows(arrs):
    return jnp.concatenate([a.reshape(-1) for a in arrs]).reshape(-1, 128)


def _layer_bwd(dxn, sv, p, upper_small=None):
    a2a3 = ["a2a"] * 3
    (dout, dya, dyb, dhl, drest, acc_out, *do_dvec), _ = _out_bwd(
        dxn, sv["mix"], sv["ya"], sv["yb"], sv["hl"], sv["rest"], sv["o_att"], sv["vec_post"],
        p["w_pa"], p["w_pb"], p["w_o"])
    mats = _mats_sources(_wgrad_tn(sv["a_gated"], dya, "wgrad_pa"), _wgrad_tn(sv["b_gated"], dyb, "wgrad_pb"),
                         _wgrad_tn(sv["mix"], dout, "wgrad_o"))
    (drest, acc_lru, gwr2, gwi2), _ = _lru_bwd(
        dhl, sv["hl"], sv["a_dec"], sv["gates"], sv["rest"], p["cw"], p["wr2"], p["wi2"], drest)
    small = dict(
        conv_w=acc_lru[0:4], conv_b=acc_lru[4], b_rg=acc_lru[5], b_ig=acc_lru[6],
        lru_lambda=acc_lru[7] * (-jax.nn.sigmoid(-p["lru_lambda"])),
        w_rg=_unpair_blocks(gwr2), w_ig=_unpair_blocks(gwi2))
    hosts = [(mats, a2a3), None, None]
    if upper_small is not None:
        for half, names in enumerate((LRU_SMALL[:3], LRU_SMALL[3:] + ("conv_w",))):
            vec = _flat_rows([jnp.stack([small[n], upper_small[n]]) for n in names])
            hosts[1 + half] = ([vec], ["ag"])
    dqkv, got_attn = [], []
    for g, r in enumerate(DILATIONS):
        d_g, got = _attn_bwd(sv["qkv"][g], do_dvec[g], sv["lse"][g], do_dvec[3 + g], g, r, hosted=hosts[g])
        dqkv.append(d_g)
        got_attn.append(got)
    shards = None
    for g, r in enumerate(DILATIONS):
        shards = _proj_wgrad_group(sv["h_t"][g], dqkv[g], g, r, shards, f"proj_wgrad_g{g}")
    shards = _proj_wgrad_part(sv["h_t"][0], drest, 2, 4, QKV_CHUNKS, shards, "proj_wgrad_u")
    shards = _proj_wgrad_part(sv["h_t"][0], drest, 7, 0, QKV_CHUNKS + 2, shards, "proj_wgrad_rest")
    (dx, acc_pre), got_in = _proj_dgrad_norm_bwd(
        dqkv, drest, p["w_int"], sv["x"], dxn, sv["vec_pre"], hosted=([shards], ["a2a"]))
    small.update(dmod=jnp.concatenate([acc_pre[0], acc_pre[1], acc_out[0]]), g_pre=acc_pre[2],
                 g_post=acc_out[1])
    return dx, small, dict(mats=got_attn[0], w_in=got_in[0], lru_small=got_attn[1] + got_attn[2])


def kernel(x, c, w_mod, b_mod, g_pre, w_in, conv_w, conv_b, w_rg, b_rg, w_ig, b_ig, lru_lambda, w_pa, w_pb, w_o, g_post, loss_target, m_w_mod, m_b_mod, m_g_pre, m_w_in, m_conv_w, m_conv_b, m_w_rg, m_b_rg, m_w_ig, m_b_ig, m_lru_lambda, m_w_pa, m_w_pb, m_w_o, m_g_post, v_w_mod, v_b_mod, v_g_pre, v_w_in, v_conv_w, v_conv_b, v_w_rg, v_b_rg, v_w_ig, v_b_ig, v_lru_lambda, v_w_pa, v_w_pb, v_w_o, v_g_post):
    W = dict(w_mod=w_mod, b_mod=b_mod, g_pre=g_pre, w_in=w_in, conv_w=conv_w, conv_b=conv_b, w_rg=w_rg,
             b_rg=b_rg, w_ig=w_ig, b_ig=b_ig, lru_lambda=lru_lambda, w_pa=w_pa, w_pb=w_pb, w_o=w_o,
             g_post=g_post)
    M = dict(w_mod=m_w_mod, b_mod=m_b_mod, g_pre=m_g_pre, w_in=m_w_in, conv_w=m_conv_w, conv_b=m_conv_b,
             w_rg=m_w_rg, b_rg=m_b_rg, w_ig=m_w_ig, b_ig=m_b_ig, lru_lambda=m_lru_lambda, w_pa=m_w_pa,
             w_pb=m_w_pb, w_o=m_w_o, g_post=m_g_post)
    V = dict(w_mod=v_w_mod, b_mod=v_b_mod, g_pre=v_g_pre, w_in=v_w_in, conv_w=v_conv_w, conv_b=v_conv_b,
             w_rg=v_w_rg, b_rg=v_b_rg, w_ig=v_w_ig, b_ig=v_b_ig, lru_lambda=v_lru_lambda, w_pa=v_w_pa,
             w_pb=v_w_pb, w_o=v_w_o, g_post=v_g_post)
    S = x.shape[1]
    me = 4 * lax.axis_index("x") + 2 * lax.axis_index("y") + lax.axis_index("c")
    n_mod = w_mod.shape[2]
    n_in = w_in.shape[2]
    n_conv = conv_w.shape[2]

    c_rows = jnp.broadcast_to(c, (8, D_MODEL))
    w_in_b, w_pa_b, w_pb_b, w_o_b = (t.astype(BF16) for t in (w_in, w_pa, w_pb, w_o))
    g_c, g_win0, g_cw = _gather_two_level([c_rows, w_in_b[0], conv_w], "gather_weights")
    c_all = g_c[:, 0, :]
    c_pad = jnp.concatenate([c_all, jnp.zeros((8, D_MODEL), F32)], axis=0)
    conv_w_full = jnp.transpose(g_cw, (1, 2, 0, 3)).reshape(2, CONV_WIDTH, D_MODEL)

    mod_cols = _mod_fwd(c_pad, w_mod)
    mod_src = jnp.transpose(mod_cols[:, :8], (1, 0, 2))
    mod_src = jnp.concatenate([mod_src, jnp.zeros((8, 6, n_mod), F32)], axis=1)
    (mod_got,) = _exchange([mod_src], ["a2a"], "scatter_mod")
    mod = jnp.transpose(mod_got[:, :2], (1, 0, 2)).reshape(2, 3 * D_MODEL) + b_mod

    def layer_params(layer, w_in_g, mats_g):
        return _layer_params(layer, w_in_g, mats_g, conv_w_full[layer], conv_b[layer], b_rg[layer],
                             b_ig[layer], lru_lambda[layer], w_rg[layer], w_ig[layer], g_pre[layer],
                             g_post[layer])

    layers = [layer_params(0, g_win0, None), None]
    (act,), sv0, (g_win1,), _, mats1_g = _layer_fwd(
        x[0], mod[0].reshape(3, D_MODEL), layers[0], own_mats=(w_pa_b[0], w_pb_b[0], w_o_b[0]),
        host_proj=([w_in_b[1]], ["ag"]), host_out=([w_pa_b[1], w_pb_b[1], w_o_b[1]], ["ag"] * 3))
    layers[1] = layer_params(1, g_win1, mats1_g)
    (dy, loss_acc), sv1, _, _, _ = _layer_fwd(act, mod[1].reshape(3, D_MODEL), layers[1],
                                              target=loss_target[0])

    dy, small1, got1 = _layer_bwd(dy, sv1, layers[1])
    dy, small0, got0 = _layer_bwd(dy, sv0, layers[0], upper_small=small1)
    grad_x = dy[None]
    r_pa, r_pb, r_o = ([got0["mats"][k], got1["mats"][k]] for k in range(3))
    r_in = [got0["w_in"], got1["w_in"]]
    grads = [small0, small1]

    def stack2(name):
        return jnp.stack([grads[0][name], grads[1][name]], axis=0)

    dmod = stack2("dmod")
    dmod_src = jnp.transpose(dmod.reshape(2, N_DEV, n_mod), (1, 0, 2))
    dmod_src = jnp.concatenate([dmod_src, jnp.zeros((8, 6, n_mod), F32)], axis=1)
    mod_vec = _flat_rows([dmod, stack2("g_pre"), stack2("g_post"), loss_acc])
    r_dmod, r_mod_small = _exchange([dmod_src, mod_vec], ["a2a", "ag"], "exchange_grads")
    mod_rows = mod_vec.shape[0] - loss_acc.shape[0]
    loss = 0.5 * jnp.sum(r_mod_small[:, mod_rows, 0]) / D_MODEL
    r_lru_small = got0["lru_small"]

    res = {}
    res["w_in"] = _adamw(r_in, w_in, m_w_in, v_w_in, 8, "adamw_w_in")
    res["w_pa"] = _adamw(r_pa, w_pa, m_w_pa, v_w_pa, 8, "adamw_w_pa")
    res["w_pb"] = _adamw(r_pb, w_pb, m_w_pb, v_w_pb, 8, "adamw_w_pb")
    res["w_o"] = _adamw(r_o, w_o, m_w_o, v_w_o, 8, "adamw_w_o")
    dmod_all = jnp.transpose(r_dmod[:, :2], (1, 0, 2))
    gw_mod = _mod_wgrad(jnp.transpose(c_all), dmod_all)
    res["w_mod"] = _adamw([gw_mod[0], gw_mod[1]], w_mod, m_w_mod, v_w_mod, 0, "adamw_w_mod")
    for names_rep, stack, tag in ((LRU_SMALL[:3], r_lru_small[0], "lru_a"), (LRU_SMALL[3:], r_lru_small[1], "lru_b"),
                                  (MOD_SMALL, r_mod_small, "mod")):
        rows = sum(W[n].size for n in names_rep) // 128
        rep = _adamw([stack], *(_flat_rows([src[n] for n in names_rep])[None] for src in (W, M, V)), 8,
                     f"adamw_small_{tag}", rows=rows // 2 if rows % 16 == 0 else rows)
        off = 0
        for name in names_rep:
            size = W[name].size
            res[name] = tuple(t.reshape(-1)[off:off + size].reshape(W[name].shape) for t in rep)
            off += size
    lru_rows = sum(W[n].size for n in LRU_SMALL[3:]) // 128
    conv_stack = r_lru_small[1][:, lru_rows:].reshape(8, 2, CONV_WIDTH, D_MODEL)
    conv_stack = lax.dynamic_slice_in_dim(conv_stack, me * n_conv, n_conv, axis=3).reshape(8, 8, n_conv)
    res["conv_w"] = _adamw([conv_stack], conv_w.reshape(1, 8, n_conv), m_conv_w.reshape(1, 8, n_conv),
                           v_conv_w.reshape(1, 8, n_conv), 8, "adamw_conv_w")

    names = ("w_mod", "b_mod", "g_pre", "w_in", "conv_w", "conv_b", "w_rg", "b_rg", "w_ig", "b_ig",
             "lru_lambda", "w_pa", "w_pb", "w_o", "g_post")
    outs = [loss, grad_x]
    for k in range(4):
        outs.extend(res[n][k].reshape(W[n].shape) for n in names)
    return tuple(outs)
```

```python
import functools

import jax
import jax.numpy as jnp
import numpy as np
from jax import lax
from jax.experimental import pallas as pl
from jax.experimental.pallas import tpu as pltpu

F32 = jnp.float32
BF16 = jnp.bfloat16

N_DEV = 8
D_MODEL = 1024
HEAD_DIM = 128
HEADS = 4
ATT_WIDTH = HEADS * HEAD_DIM
DILATIONS = (1, 4, 16)
BAND = 128
N_CHUNKS = 18
QKV_CHUNKS = 9
CONV_WIDTH = 4
LRU_C = 8.0
NORM_EPS = 1e-6
NEG_INF = -1e30
ADAM_LR = 0.001
ADAM_B1 = 0.9
ADAM_B2 = 0.999
ADAM_EPS = 1e-08
ADAM_WD = 0.01
ADAM_STEP = 10
CHUNK_PERM = (0, 3, 6, 1, 4, 7, 2, 5, 8, 10, 11, 12, 13, 14, 15, 16, 17, 9)
DREST_CHUNKS = 10
VMEM_LIMIT = 56 * 1024 * 1024


def _params(sem=None):
    return pltpu.CompilerParams(dimension_semantics=sem, vmem_limit_bytes=VMEM_LIMIT)


def _silu(x):
    return x * jax.nn.sigmoid(x)


def _dsilu(x):
    s = jax.nn.sigmoid(x)
    return s * (1.0 + x * (1.0 - s))


def _neg_expm1(x):
    series = -x * (1.0 + x * (0.5 + x * (1.0 / 6.0 + x * (1.0 / 24.0))))
    return jnp.where(x > -0.05, series, 1.0 - jnp.exp(x))


def _softplus_neg(lam):
    z = jnp.exp(-jnp.abs(lam))
    small = z * (1.0 - z * (0.5 - z * (1.0 / 3.0 - z * 0.25)))
    log1p_z = jnp.where(z < 1e-2, small, jnp.log(1.0 + z))
    return jnp.maximum(-lam, 0.0) + log1p_z


def _dot(a, b):
    return jnp.dot(a, b, preferred_element_type=F32)


def _dot_nt(a, b):
    return lax.dot_general(a, b, (((1,), (1,)), ((), ())), preferred_element_type=F32)


def _dot_tn(a, b):
    return lax.dot_general(a, b, (((0,), (0,)), ((), ())), preferred_element_type=F32)


def _perm_matrix(r, rows, transpose=False):
    n = rows // r
    p = np.zeros((rows, rows), np.float32)
    dst = np.arange(rows)
    p[dst, (dst % n) * r + dst // n] = 1.0
    return jnp.asarray(p.T if transpose else p, dtype=BF16)


def _permute_f32(p, x, pieces):
    part = x.astype(BF16)
    acc = _dot(p, part)
    for _ in range(pieces - 1):
        x = x - part.astype(F32)
        part = x.astype(BF16)
        acc = acc + _dot(p, part)
    return acc


def _rows_of_residues(ref, lead=()):
    r = ref.shape[len(lead)]
    if r == 1:
        return ref[lead + (0,)]
    return jnp.concatenate([ref[lead + (rho,)] for rho in range(r)], axis=0)


def _store_residues(ref, x, lead=()):
    r = ref.shape[len(lead)]
    n = x.shape[0] // r
    for rho in range(r):
        ref[lead + (rho,)] = x[rho * n:(rho + 1) * n]


ANY_SPEC = pl.BlockSpec(memory_space=pl.ANY)


def _exchange_shapes(arrs, modes):
    return [jax.ShapeDtypeStruct(((N_DEV,) + a.shape) if mode == "ag" else a.shape, a.dtype)
            for a, mode in zip(arrs, modes)]


def _exchange_sems(n):
    return [pltpu.SemaphoreType.DMA((7 * n,)), pltpu.SemaphoreType.DMA((7 * n,)),
            pltpu.SemaphoreType.DMA((n,))]


def _exchange_copies(ins, outs, modes, sems):
    n = len(ins)
    send_sems, recv_sems, local_sems = sems
    x, y, c = lax.axis_index("x"), lax.axis_index("y"), lax.axis_index("c")
    me = 4 * x + 2 * y + c

    def src_for(a, dev):
        return ins[a] if modes[a] == "ag" else ins[a].at[dev]

    local = [pltpu.make_async_copy(src_for(a, me), outs[a].at[me], local_sems.at[a]) for a in range(n)]
    sends, arrivals = [], []
    for k in range(1, N_DEV):
        px = 1 - x if (k >> 2) & 1 else x
        py = 1 - y if (k >> 1) & 1 else y
        pc = 1 - c if k & 1 else c
        peer = 4 * px + 2 * py + pc
        for a in range(n):
            s = (k - 1) * n + a
            for dst, group in ((me, sends), (peer, arrivals)):
                group.append(pltpu.make_async_remote_copy(
                    src_ref=src_for(a, peer), dst_ref=outs[a].at[dst],
                    send_sem=send_sems.at[s], recv_sem=recv_sems.at[s],
                    device_id=(px, py, pc), device_id_type=pl.DeviceIdType.MESH))
    return local, sends, arrivals


def _exchange_start(copies):
    local, sends, _ = copies
    for cp in local + sends:
        cp.start()


def _exchange_wait(copies):
    local, sends, arrivals = copies
    for cp in arrivals:
        cp.wait_recv()
    for cp in sends:
        cp.wait_send()
    for cp in local:
        cp.wait()


def _exchange(arrs, modes, name):
    n = len(arrs)

    def body(*refs):
        copies = _exchange_copies(refs[:n], refs[n:2 * n], modes, refs[2 * n:])
        _exchange_start(copies)
        _exchange_wait(copies)

    outs = pl.pallas_call(
        body, name=name, out_shape=tuple(_exchange_shapes(arrs, modes)),
        in_specs=[ANY_SPEC] * n, out_specs=tuple([ANY_SPEC] * n),
        scratch_shapes=_exchange_sems(n),
    )(*arrs)
    return list(outs)


def _gather_two_level(arrs, name):
    n = len(arrs)

    def body(*refs):
        ins, outs = refs[:n], refs[n:2 * n]
        send_sems, recv_sems, local_sems = refs[2 * n:]
        x, y, c = lax.axis_index("x"), lax.axis_index("y"), lax.axis_index("c")
        me, sibling = (x, y, c), (x, y, 1 - c)
        chips = [(1 - x, y), (x, 1 - y), (1 - x, 1 - y)]

        def copy(a, k, block, to, src=None):
            slot = 4 * block[0] + 2 * block[1] + block[2]
            return pltpu.make_async_remote_copy(
                src_ref=outs[a].at[slot] if src is None else src, dst_ref=outs[a].at[slot],
                send_sem=send_sems.at[7 * a + k], recv_sem=recv_sems.at[7 * a + k],
                device_id=to, device_id_type=pl.DeviceIdType.MESH)

        mine = [pltpu.make_async_copy(ins[a], outs[a].at[4 * x + 2 * y + c], local_sems.at[a])
                for a in range(n)]
        first = []
        for a in range(n):
            first.append(copy(a, 0, me, sibling, src=ins[a]))
            first += [copy(a, 1 + j, me, (*chip, c), src=ins[a]) for j, chip in enumerate(chips)]
        for cp in mine + first:
            cp.start()
        passed = []
        for j, chip in enumerate(chips):
            for a in range(n):
                copy(a, 1 + j, (*chip, c), me).wait_recv()
                passed.append(copy(a, 4 + j, (*chip, c), sibling))
                passed[-1].start()
        for a in range(n):
            copy(a, 0, sibling, me).wait_recv()
            for j, chip in enumerate(chips):
                copy(a, 4 + j, (*chip, 1 - c), me).wait_recv()
        for cp in first + passed:
            cp.wait_send()
        for cp in mine:
            cp.wait()

    outs = pl.pallas_call(
        body, name=name, out_shape=tuple(_exchange_shapes(arrs, ["ag"] * n)),
        in_specs=[ANY_SPEC] * n, out_specs=tuple([ANY_SPEC] * n),
        scratch_shapes=_exchange_sems(n),
    )(*arrs)
    return list(outs)


SHARD_COLS = N_CHUNKS * ATT_WIDTH // N_DEV


def _weights_full(shards):
    return jnp.transpose(shards, (1, 0, 2)).reshape(D_MODEL, N_CHUNKS * ATT_WIDTH)


def _wcols(w_ref, k):
    orig = CHUNK_PERM[k]
    return w_ref[:, orig * ATT_WIDTH:(orig + 1) * ATT_WIDTH]


def _store_shard_columns(stage_ref, shards_ref, sems, first_chunk):
    copies = []
    for t in range(stage_ref.shape[1] // ATT_WIDTH):
        col = ATT_WIDTH * CHUNK_PERM[first_chunk + t]
        end = col + ATT_WIDTH
        while col < end:
            dev = col // SHARD_COLS
            upto = min(end, (dev + 1) * SHARD_COLS)
            src = stage_ref.at[:, pl.ds(t * ATT_WIDTH + ATT_WIDTH - (end - col), upto - col)]
            dst = shards_ref.at[dev, :, pl.ds(col - dev * SHARD_COLS, upto - col)]
            copies.append(pltpu.make_async_copy(src, dst, sems.at[len(copies)]))
            col = upto
    for cp in copies:
        cp.start()
    for cp in copies:
        cp.wait()


SHARD_COPIES_PER_CHUNK = 2


def _call(body, args, *, name, grid, in_specs, out_specs, out_shape, scratch_shapes=(), aliases=None,
          hosted=None):
    sem = ("arbitrary",) * len(grid)
    if hosted is None:
        outs = pl.pallas_call(
            body, name=name, grid=grid, in_specs=in_specs, out_specs=tuple(out_specs),
            out_shape=tuple(out_shape), scratch_shapes=list(scratch_shapes),
            input_output_aliases=aliases or {}, compiler_params=_params(sem))(*args)
        return list(outs), []
    x_arrs, modes = hosted
    n_in, n_out, n_scr, nx = len(args), len(out_shape), len(scratch_shapes), len(x_arrs)

    def wrapped(*refs):
        ins, x_ins = refs[:n_in], refs[n_in:n_in + nx]
        outs = refs[n_in + nx:n_in + nx + n_out]
        x_outs = refs[n_in + nx + n_out:n_in + 2 * nx + n_out]
        scr = refs[n_in + 2 * nx + n_out:n_in + 2 * nx + n_out + n_scr]
        sems = refs[n_in + 2 * nx + n_out + n_scr:]
        first = pl.program_id(0) == 0
        last = pl.program_id(0) == grid[0] - 1
        for axis in range(1, len(grid)):
            first = jnp.logical_and(first, pl.program_id(axis) == 0)
            last = jnp.logical_and(last, pl.program_id(axis) == grid[axis] - 1)

        @pl.when(first)
        def _():
            _exchange_start(_exchange_copies(x_ins, x_outs, modes, sems))

        body(*ins, *outs, *scr)

        @pl.when(last)
        def _():
            _exchange_wait(_exchange_copies(x_ins, x_outs, modes, sems))

    outs = pl.pallas_call(
        wrapped, name=name, grid=grid, in_specs=list(in_specs) + [ANY_SPEC] * nx,
        out_specs=tuple(out_specs) + tuple([ANY_SPEC] * nx),
        out_shape=tuple(out_shape) + tuple(_exchange_shapes(x_arrs, modes)),
        scratch_shapes=list(scratch_shapes) + _exchange_sems(nx),
        input_output_aliases=aliases or {}, compiler_params=_params(sem))(*args, *x_arrs)
    return list(outs[:n_out]), list(outs[n_out:])


def _mod_fwd(c_pad, w_mod):
    def body(c_ref, w_ref, o_ref):
        sc = _silu(c_ref[...]).astype(BF16)
        for layer in range(2):
            o_ref[layer] = _dot(sc, w_ref[layer].astype(BF16))

    return pl.pallas_call(
        body, name="mod_fwd", out_shape=jax.ShapeDtypeStruct((2, 16, w_mod.shape[2]), F32),
        compiler_params=_params())(c_pad, w_mod)


def _mod_wgrad(c_t, dmod):
    n_cols = dmod.shape[2]

    def body(c_ref, d_ref, o_ref):
        sc = _silu(c_ref[...]).astype(BF16).astype(F32)
        for layer in range(2):
            dm = d_ref[layer].astype(BF16).astype(F32)
            acc = sc[:, 0:1] * dm[0:1, :]
            for b in range(1, N_DEV):
                acc = acc + sc[:, b:b + 1] * dm[b:b + 1, :]
            o_ref[layer] = acc

    return pl.pallas_call(
        body, name="mod_wgrad", out_shape=jax.ShapeDtypeStruct((2, D_MODEL, n_cols), F32),
        compiler_params=_params())(c_t, dmod)


GROUP_COLS = 3 * ATT_WIDTH
PROJ_TM = 256


def _norm_proj_fwd(x, vec, w_int, hosted=None):
    S = x.shape[0]
    tm = PROJ_TM
    perms = [_perm_matrix(r, tm) for r in DILATIONS[1:]]

    def body(x_ref, v_ref, w_ref, p1_ref, p2_ref, ht0_ref, ht1_ref, ht2_ref, q0_ref, q1_ref, q2_ref,
             rest_ref):
        xv = x_ref[...]
        rstd = lax.rsqrt(jnp.mean(xv * xv, axis=-1, keepdims=True) + NORM_EPS)
        hf = ((xv * rstd) * v_ref[0:1, :]) * (1.0 + v_ref[1:2, :]) + v_ref[2:3, :]
        h = hf.astype(BF16)
        for g, (q_ref, ht_ref, p_ref) in enumerate(((q0_ref, ht0_ref, None), (q1_ref, ht1_ref, p1_ref),
                                                    (q2_ref, ht2_ref, p2_ref))):
            rows_f = hf if p_ref is None else _dot(p_ref[...], h)
            ht_ref[...] = rows_f.T.astype(BF16)
            rows = rows_f.astype(BF16)
            n = tm // q_ref.shape[0]
            for t in range(3):
                res = _dot(rows, _wcols(w_ref, 3 * g + t)).astype(BF16)
                for rho in range(q_ref.shape[0]):
                    q_ref[rho, :, t * ATT_WIDTH:(t + 1) * ATT_WIDTH] = res[rho * n:(rho + 1) * n]
        split = (QKV_CHUNKS - 1) * ATT_WIDTH
        rest_ref[:, :split] = _dot(h, w_ref[:, (QKV_CHUNKS + 1) * ATT_WIDTH:])
        rest_ref[:, split:] = _dot(h, w_ref[:, QKV_CHUNKS * ATT_WIDTH:(QKV_CHUNKS + 1) * ATT_WIDTH])

    n_rest = QKV_CHUNKS * ATT_WIDTH
    whole = pl.BlockSpec(memory_space=pltpu.VMEM)
    ht_spec = pl.BlockSpec((D_MODEL, tm), lambda i: (0, i))
    ht_shape = jax.ShapeDtypeStruct((D_MODEL, S), BF16)
    return _call(
        body, (x, vec, w_int, *perms), name="norm_proj_fwd", grid=(S // tm,),
        in_specs=[pl.BlockSpec((tm, D_MODEL), lambda i: (i, 0)),
                  pl.BlockSpec((8, D_MODEL), lambda i: (0, 0)), whole, whole, whole],
        out_specs=(ht_spec, ht_spec, ht_spec,
                   *[pl.BlockSpec((r, tm // r, GROUP_COLS), lambda i: (0, i, 0)) for r in DILATIONS],
                   pl.BlockSpec((tm, n_rest), lambda i: (i, 0))),
        out_shape=(ht_shape, ht_shape, ht_shape,
                   *[jax.ShapeDtypeStruct((r, S // r, GROUP_COLS), BF16) for r in DILATIONS],
                   jax.ShapeDtypeStruct((S, n_rest), F32)),
        hosted=hosted)


def _proj_dgrad_norm_bwd(dqkv, drest, w_int, x, dxn, vec, hosted=None):
    S = x.shape[0]
    tm = PROJ_TM
    perms = [_perm_matrix(r, tm, transpose=True) for r in DILATIONS[1:]]

    def body(d0_ref, d1_ref, d2_ref, dr_ref, w_ref, p1_ref, p2_ref, x_ref, dxn_ref, v_ref, dx_ref, acc_ref):
        i = pl.program_id(0)

        @pl.when(i == 0)
        def _():
            acc_ref[...] = jnp.zeros_like(acc_ref)

        wcols = functools.partial(_wcols, w_ref)
        dhv = None
        for g, (d_ref, p_ref) in enumerate(((d0_ref, None), (d1_ref, p1_ref), (d2_ref, p2_ref))):
            for t in range(3):
                d = _rows_of_residues(d_ref, (t,))
                if p_ref is not None:
                    d = _dot(p_ref[...], d).astype(BF16)
                part = _dot_nt(d, wcols(3 * g + t))
                dhv = part if dhv is None else dhv + part
        for t in range(7):
            dhv = dhv + _dot_nt(dr_ref[t], wcols(11 + t))
        for t in range(2):
            dhv = dhv + _dot_nt(dr_ref[8 + t], wcols(9 + t))

        xv = x_ref[...]
        g = v_ref[0:1, :]
        sc1 = 1.0 + v_ref[1:2, :]
        rstd = lax.rsqrt(jnp.mean(xv * xv, axis=-1, keepdims=True) + NORM_EPS)
        xhat = xv * rstd
        acc_ref[0:1, :] += jnp.sum(dhv, axis=0, keepdims=True)
        acc_ref[1:2, :] += jnp.sum(dhv * (xhat * g), axis=0, keepdims=True)
        acc_ref[2:3, :] += jnp.sum(dhv * xhat * sc1, axis=0, keepdims=True)
        dxhat = dhv * (g * sc1)
        dx = rstd * (dxhat - xhat * jnp.mean(dxhat * xhat, axis=-1, keepdims=True))
        dx_ref[...] = dx + dxn_ref[...]

    row = pl.BlockSpec((tm, D_MODEL), lambda i: (i, 0))
    vec_spec = pl.BlockSpec((8, D_MODEL), lambda i: (0, 0))
    whole = pl.BlockSpec(memory_space=pltpu.VMEM)
    return _call(
        body, (*dqkv, drest, w_int, *perms, x, dxn, vec), name="proj_dgrad_norm_bwd", grid=(S // tm,),
        in_specs=[*[pl.BlockSpec((3, r, tm // r, ATT_WIDTH), lambda i: (0, 0, i, 0)) for r in DILATIONS],
                  pl.BlockSpec((DREST_CHUNKS, tm, ATT_WIDTH), lambda i: (0, i, 0)),
                  whole, whole, whole, row, row, vec_spec],
        out_specs=(row, vec_spec),
        out_shape=(jax.ShapeDtypeStruct((S, D_MODEL), F32), jax.ShapeDtypeStruct((8, D_MODEL), F32)),
        hosted=hosted)


WGRAD_TS = 512


def _proj_wgrad_call(h_t, d, d_spec, rows_of, n_chunks, first_chunk, shards, name):
    S = h_t.shape[1]
    n_steps = S // WGRAD_TS
    width = n_chunks * ATT_WIDTH

    def body(*refs):
        h_ref, d_ref = refs[:2]
        shards_ref, acc_ref, stage_ref, sems = refs[-4:]
        i = pl.program_id(0)

        @pl.when(i == 0)
        def _():
            acc_ref[...] = jnp.zeros_like(acc_ref)

        ht = h_ref[...]
        for t in range(n_chunks):
            acc_ref[:, t * ATT_WIDTH:(t + 1) * ATT_WIDTH] += _dot(ht, rows_of(d_ref, t))

        @pl.when(i == n_steps - 1)
        def _():
            stage_ref[...] = acc_ref[...].astype(BF16)
            _store_shard_columns(stage_ref, shards_ref, sems, first_chunk)

    chained = shards is not None
    return pl.pallas_call(
        body, name=name, grid=(n_steps,),
        in_specs=[pl.BlockSpec((D_MODEL, WGRAD_TS), lambda i: (0, i)), d_spec] + [ANY_SPEC] * chained,
        out_specs=ANY_SPEC,
        out_shape=jax.ShapeDtypeStruct((N_DEV, D_MODEL, SHARD_COLS), BF16),
        scratch_shapes=[pltpu.VMEM((D_MODEL, width), F32), pltpu.VMEM((D_MODEL, width), BF16),
                        pltpu.SemaphoreType.DMA((SHARD_COPIES_PER_CHUNK * n_chunks,))],
        input_output_aliases={2: 0} if chained else {},
        compiler_params=_params(("arbitrary",)))(h_t, d, *([shards] * chained))


def _proj_wgrad_part(h_t, d, n_chunks, chunk_block, first_chunk, shards, name):
    spec = pl.BlockSpec((n_chunks, WGRAD_TS, ATT_WIDTH), lambda i: (chunk_block, i, 0))
    return _proj_wgrad_call(h_t, d, spec, lambda d_ref, t: d_ref[t], n_chunks, first_chunk, shards, name)


def _proj_wgrad_group(h_t, d, g, r, shards, name):
    n = PROJ_TM // r

    def rows_of(d_ref, t):
        if r == 1:
            return d_ref[t, 0]
        return jnp.concatenate([d_ref[t, rho, s * n:(s + 1) * n] for s in range(WGRAD_TS // PROJ_TM)
                                for rho in range(r)], axis=0)

    spec = pl.BlockSpec((3, r, WGRAD_TS // r, ATT_WIDTH), lambda i: (0, 0, i, 0))
    return _proj_wgrad_call(h_t, d, spec, rows_of, 3, 3 * g, shards, name)


def _wgrad_tn(a, b, name):
    S, M = a.shape
    N = b.shape[1]
    ts = 1024

    def body(a_ref, b_ref, o_ref):
        i = pl.program_id(0)
        part = _dot_tn(a_ref[...], b_ref[...])

        @pl.when(i == 0)
        def _():
            o_ref[...] = part

        @pl.when(i > 0)
        def _():
            o_ref[...] += part

    return pl.pallas_call(
        body, name=name, grid=(S // ts,),
        in_specs=[pl.BlockSpec((ts, M), lambda i: (i, 0)), pl.BlockSpec((ts, N), lambda i: (i, 0))],
        out_specs=pl.BlockSpec((M, N), lambda i: (0, 0)),
        out_shape=jax.ShapeDtypeStruct((M, N), F32),
        compiler_params=_params(("arbitrary",)))(a, b)


ATTN_FWD_BLOCKS = 8


def _attn_fwd(qkv, g, r, hosted=None):
    L = qkv.shape[1]
    qb = min(ATTN_FWD_BLOCKS, L // BAND)
    n_steps = L // (qb * BAND)
    scale = HEAD_DIM ** -0.5

    def body(q_ref, kp_ref, kc_ref, vp_ref, vc_ref, o_ref, st_ref):
        m_step = pl.program_id(1)
        ii = lax.broadcasted_iota(jnp.int32, (BAND, 2 * BAND), 0)
        kk = lax.broadcasted_iota(jnp.int32, (BAND, 2 * BAND), 1)
        band = jnp.logical_and(kk >= ii, kk <= ii + BAND)
        first_mask = jnp.logical_and(band, jnp.logical_or(kk >= BAND, m_step > 0))
        lane = lax.broadcasted_iota(jnp.int32, (BAND, 128), 1)
        stats = [jnp.zeros((BAND, 128), F32) for _ in range(qb)]
        for h in range(HEADS):
            sl = slice(h * HEAD_DIM, (h + 1) * HEAD_DIM)
            kc, vc = kc_ref[:, sl], vc_ref[:, sl]
            for b in range(qb):
                rows = slice(b * BAND, (b + 1) * BAND)
                if b == 0:
                    keys = jnp.concatenate([kp_ref[:, sl], kc[:BAND]], axis=0)
                    vals = jnp.concatenate([vp_ref[:, sl], vc[:BAND]], axis=0)
                else:
                    keys, vals = kc[(b - 1) * BAND:(b + 1) * BAND], vc[(b - 1) * BAND:(b + 1) * BAND]
                s = jnp.where(first_mask if b == 0 else band, _dot_nt(q_ref[rows, sl], keys) * scale, NEG_INF)
                m = jnp.max(s, axis=1, keepdims=True)
                p = jnp.exp(s - m)
                l = jnp.sum(p, axis=1, keepdims=True)
                o_ref[rows, sl] = _dot(p.astype(BF16), vals) / l
                stats[b] = jnp.where(lane == h, m, stats[b])
                stats[b] = jnp.where(lane == HEADS + h, l, stats[b])
        for b in range(qb):
            st_ref[b * BAND:(b + 1) * BAND, :] = stats[b]

    many = (None, qb * BAND, ATT_WIDTH)
    one = (None, BAND, ATT_WIDTH)

    def prev(m):
        return jnp.maximum(qb * m - 1, 0)

    return _call(
        body, (qkv, qkv, qkv, qkv, qkv), name=f"attn_fwd_g{g}", grid=(r, n_steps),
        in_specs=[
            pl.BlockSpec(many, lambda rho, m: (rho, m, 0)),
            pl.BlockSpec(one, lambda rho, m: (rho, prev(m), 1)),
            pl.BlockSpec(many, lambda rho, m: (rho, m, 1)),
            pl.BlockSpec(one, lambda rho, m: (rho, prev(m), 2)),
            pl.BlockSpec(many, lambda rho, m: (rho, m, 2)),
        ],
        out_specs=(pl.BlockSpec(many, lambda rho, m: (rho, m, 0)),
                   pl.BlockSpec((None, qb * BAND, 128), lambda rho, m: (rho, m, 0))),
        out_shape=(jax.ShapeDtypeStruct((r, L, ATT_WIDTH), F32),
                   jax.ShapeDtypeStruct((r, L, 128), F32)),
        hosted=hosted)


def _attn_combine(outs, stats, rest):
    S = rest.shape[0]
    tm = 256
    gatt_blk = 8
    back = [_perm_matrix(r, tm, transpose=True) for r in DILATIONS[1:]]
    forth = [_perm_matrix(r, tm) for r in DILATIONS[1:]]

    def body(o0_ref, o1_ref, o2_ref, s0_ref, s1_ref, s2_ref, g_ref, b1_ref, b2_ref, f1_ref, f2_ref,
             o_ref, a_ref, l0_ref, l1_ref, l2_ref):
        outs_nat = [o0_ref[0]] + [_permute_f32(b_ref[...], _rows_of_residues(o_g), 2)
                                  for o_g, b_ref in ((o1_ref, b1_ref), (o2_ref, b2_ref))]
        st = [s0_ref[0]] + [_permute_f32(b_ref[...], _rows_of_residues(s_g), 3)
                            for s_g, b_ref in ((s1_ref, b1_ref), (s2_ref, b2_ref))]
        lane = lax.broadcasted_iota(jnp.int32, (tm, 128), 1)
        lse_out = jnp.zeros((tm, 128), F32)
        for h in range(HEADS):
            sl = slice(h * HEAD_DIM, (h + 1) * HEAD_DIM)
            ms = [s[:, h:h + 1] for s in st]
            ls = [s[:, HEADS + h:HEADS + h + 1] for s in st]
            m_all = jnp.maximum(jnp.maximum(ms[0], ms[1]), ms[2])
            ws = [l * jnp.exp(m - m_all) for m, l in zip(ms, ls)]
            den = ws[0] + ws[1] + ws[2]
            o = (ws[0] * outs_nat[0][:, sl] + ws[1] * outs_nat[1][:, sl] + ws[2] * outs_nat[2][:, sl]) / den
            o_ref[:, sl] = o
            a_ref[:, sl] = (o * _silu(g_ref[:, sl])).astype(BF16)
            lse_out = jnp.where(lane == h, m_all + jnp.log(den), lse_out)
        l0_ref[0] = lse_out
        for l_ref, f_ref in ((l1_ref, f1_ref), (l2_ref, f2_ref)):
            _store_residues(l_ref, _permute_f32(f_ref[...], lse_out, 3))

    o_spec = pl.BlockSpec((tm, ATT_WIDTH), lambda i: (i, 0))
    whole = pl.BlockSpec(memory_space=pltpu.VMEM)

    def res_spec(r, width):
        return pl.BlockSpec((r, tm // r, width), lambda i: (0, i, 0))

    return pl.pallas_call(
        body, name="attn_combine", grid=(S // tm,),
        in_specs=[*[res_spec(r, ATT_WIDTH) for r in DILATIONS], *[res_spec(r, 128) for r in DILATIONS],
                  pl.BlockSpec((tm, ATT_WIDTH), lambda i: (i, gatt_blk)), whole, whole, whole, whole],
        out_specs=(o_spec, o_spec, *[res_spec(r, 128) for r in DILATIONS]),
        out_shape=(jax.ShapeDtypeStruct((S, ATT_WIDTH), F32), jax.ShapeDtypeStruct((S, ATT_WIDTH), BF16),
                   *[jax.ShapeDtypeStruct((r, S // r, 128), F32) for r in DILATIONS]),
        compiler_params=_params(("parallel",)))(*outs, *stats, rest, *back, *forth)


ATTN_BWD_PAIRS = 4


def _attn_bwd(qkv, do, lse, dvec, g, r, hosted=None):
    L = qkv.shape[1]
    nb = L // BAND
    pairs = min(ATTN_BWD_PAIRS, nb // 2)
    kb = 2 * pairs
    n_steps = nb // kb
    scale = HEAD_DIM ** -0.5

    def body(qc_ref, qn_ref, k_ref, v_ref, doc_ref, don_ref, lc_ref, ln_ref, dc_ref, dn_ref,
             out_ref, carry_ref):
        j = pl.program_id(1)

        @pl.when(j == 0)
        def _():
            carry_ref[...] = jnp.zeros_like(carry_ref)

        ii = lax.broadcasted_iota(jnp.int32, (3 * BAND, 2 * BAND), 0)
        kk = lax.broadcasted_iota(jnp.int32, (3 * BAND, 2 * BAND), 1)
        band = jnp.logical_and(kk <= ii, kk >= ii - BAND)
        last_band = jnp.logical_and(band, jnp.logical_or(ii < 2 * BAND, j < n_steps - 1))
        lse_all = jnp.concatenate([lc_ref[...], ln_ref[...]], axis=0)
        dvec_all = jnp.concatenate([dc_ref[...], dn_ref[...]], axis=0)
        for h in range(HEADS):
            sl = slice(h * HEAD_DIM, (h + 1) * HEAD_DIM)
            q_all = jnp.concatenate([qc_ref[:, sl], qn_ref[:, sl]], axis=0)
            do_all = jnp.concatenate([doc_ref[:, sl], don_ref[:, sl]], axis=0)
            dq_prev = carry_ref[:, sl]
            for a in range(pairs):
                keys = slice(2 * a * BAND, (2 * a + 2) * BAND)
                rows = slice(2 * a * BAND, (2 * a + 3) * BAND)
                k, v, q, do = k_ref[keys, sl], v_ref[keys, sl], q_all[rows], do_all[rows]
                mask = last_band if a == pairs - 1 else band
                p = jnp.where(mask, jnp.exp(_dot_nt(q, k) * scale - lse_all[rows, h:h + 1]), 0.0)
                ds = (p * (_dot_nt(do, v) - dvec_all[rows, h:h + 1])).astype(BF16)
                dq3 = _dot(ds, k) * scale
                out_ref[0, 2 * a * BAND:(2 * a + 1) * BAND, sl] = (dq_prev + dq3[:BAND]).astype(BF16)
                out_ref[0, (2 * a + 1) * BAND:(2 * a + 2) * BAND, sl] = dq3[BAND:2 * BAND].astype(BF16)
                out_ref[1, keys, sl] = (_dot_tn(ds, q) * scale).astype(BF16)
                out_ref[2, keys, sl] = _dot_tn(p.astype(BF16), do).astype(BF16)
                dq_prev = dq3[2 * BAND:]
            carry_ref[:, sl] = dq_prev

    many = (None, kb * BAND, ATT_WIDTH)
    one = (None, BAND, ATT_WIDTH)
    smany = (None, kb * BAND, 128)
    sone = (None, BAND, 128)

    def nxt(j):
        return jnp.minimum(kb * j + kb, nb - 1)

    (out,), got = _call(
        body, (qkv, qkv, qkv, qkv, do, do, lse, lse, dvec, dvec), name=f"attn_bwd_g{g}", grid=(r, n_steps),
        in_specs=[
            pl.BlockSpec(many, lambda rho, j: (rho, j, 0)),
            pl.BlockSpec(one, lambda rho, j: (rho, nxt(j), 0)),
            pl.BlockSpec(many, lambda rho, j: (rho, j, 1)),
            pl.BlockSpec(many, lambda rho, j: (rho, j, 2)),
            pl.BlockSpec(many, lambda rho, j: (rho, j, 0)),
            pl.BlockSpec(one, lambda rho, j: (rho, nxt(j), 0)),
            pl.BlockSpec(smany, lambda rho, j: (rho, j, 0)),
            pl.BlockSpec(sone, lambda rho, j: (rho, nxt(j), 0)),
            pl.BlockSpec(smany, lambda rho, j: (rho, j, 0)),
            pl.BlockSpec(sone, lambda rho, j: (rho, nxt(j), 0)),
        ],
        out_specs=(pl.BlockSpec((3, None, kb * BAND, ATT_WIDTH), lambda rho, j: (0, rho, j, 0)),),
        out_shape=(jax.ShapeDtypeStruct((3, r, L, ATT_WIDTH), BF16),),
        scratch_shapes=[pltpu.VMEM((BAND, ATT_WIDTH), F32)],
        hosted=hosted)
    return out, got


LRU_T = 256


def _linear_scan(a, b, carry, reverse):
    T, C = a.shape
    a = a.reshape(T // 8, 8, C)
    b = b.reshape(T // 8, 8, C)
    row8 = lax.broadcasted_iota(jnp.int32, a.shape, 1)
    for s in (1, 2, 4):
        keep = (row8 < 8 - s) if reverse else (row8 >= s)
        shift = 8 - s if reverse else s
        b_sh = jnp.where(keep, pltpu.roll(b, shift, 1), 0.0)
        a_sh = jnp.where(keep, pltpu.roll(a, shift, 1), 1.0)
        b = a * b_sh + b
        a = a * a_sh
    tiles = [None] * (T // 8)
    order = range(T // 8 - 1, -1, -1) if reverse else range(T // 8)
    for k in order:
        y = b[k] + a[k] * carry
        tiles[k] = y
        carry = y[0:1] if reverse else y[7:8]
    return jnp.concatenate(tiles, axis=0), carry


def _gate_matmuls(ucb, w_ref, bias):
    parts = [_dot(ucb[:, j * 128:(j + 1) * 128], w_ref[j]) for j in range(8)]
    return jnp.concatenate(parts, axis=1) + bias


def _rows_before(x, prev8, j):
    T, C = x.shape
    rot = pltpu.roll(x.reshape(T // 8, 8, C), j, 1)
    rot_prev = jnp.concatenate([pltpu.roll(prev8, j, 0)[None], rot[:-1]], axis=0)
    row8 = lax.broadcasted_iota(jnp.int32, rot.shape, 1)
    return jnp.where(row8 >= j, rot, rot_prev).reshape(T, C)


def _rows_after(x, next8, j):
    T, C = x.shape
    rot = pltpu.roll(x.reshape(T // 8, 8, C), 8 - j, 1)
    rot_next = jnp.concatenate([rot[1:], pltpu.roll(next8, 8 - j, 0)[None]], axis=0)
    row8 = lax.broadcasted_iota(jnp.int32, rot.shape, 1)
    return jnp.where(row8 < 8 - j, rot, rot_next).reshape(T, C)


def _conv_fwd(u, u_prev8, cw_ref):
    uc = cw_ref[4:5, :] + cw_ref[0:1, :] * u
    for j in range(1, CONV_WIDTH):
        uc = uc + cw_ref[j:j + 1, :] * _rows_before(u, u_prev8, j)
    return uc


def _lru_fwd(rest, cw, wr2, wi2, hosted=None):
    S = rest.shape[0]
    T = LRU_T

    def body(u_ref, cw_ref, wr_ref, wi_ref, h_ref, a_ref, uc_ref, r_ref, ig_ref, mult_ref,
             ucar_ref, hcar_ref):
        c = pl.program_id(0)

        @pl.when(c == 0)
        def _():
            ucar_ref[...] = jnp.zeros_like(ucar_ref)
            hcar_ref[...] = jnp.zeros_like(hcar_ref)

        u = u_ref[...]
        uc = _conv_fwd(u, ucar_ref[...], cw_ref)
        ucar_ref[...] = u[T - 8:, :]
        ucb = uc.astype(BF16)
        r = jax.nn.sigmoid(_gate_matmuls(ucb, wr_ref, cw_ref[5:6, :]))
        ig = jax.nn.sigmoid(_gate_matmuls(ucb, wi_ref, cw_ref[6:7, :]))
        log_a = -LRU_C * r * _softplus_neg(cw_ref[7:8, :])
        a = jnp.exp(log_a)
        mult = jnp.sqrt(_neg_expm1(2.0 * log_a))
        b = mult * (ig * uc)
        a_ref[...] = a
        uc_ref[...] = uc
        r_ref[...] = r
        ig_ref[...] = ig
        mult_ref[...] = mult
        h, last = _linear_scan(a, b, hcar_ref[0:1, :], reverse=False)
        h_ref[...] = h
        hcar_ref[...] = jnp.broadcast_to(last, (8, D_MODEL))

    row_spec = pl.BlockSpec((T, D_MODEL), lambda c: (c, 0))
    row_shape = jax.ShapeDtypeStruct((S, D_MODEL), F32)
    return _call(
        body, (rest, cw, wr2, wi2), name="lru_fwd", grid=(S // T,),
        in_specs=[row_spec, pl.BlockSpec((8, D_MODEL), lambda c: (0, 0)),
                  pl.BlockSpec((8, 128, 128), lambda c: (0, 0, 0)),
                  pl.BlockSpec((8, 128, 128), lambda c: (0, 0, 0))],
        out_specs=(row_spec,) * 6, out_shape=(row_shape,) * 6,
        scratch_shapes=[pltpu.VMEM((8, D_MODEL), F32), pltpu.VMEM((8, D_MODEL), F32)],
        hosted=hosted)


def _lru_bwd(dhl, hl, a_all, gates, rest, cw, wr2, wi2, dproj, hosted=None):
    S = rest.shape[0]
    T = LRU_T
    nc = S // T

    def body(dh_ref, h_ref, hp_ref, a_ref, uc_ref, r_ref, ig_ref, mult_ref, u_ref, up_ref, cw_ref, wr_ref,
             wi_ref, _alias, du_ref, acc_ref, gwr_ref, gwi_ref, gcar_ref, acar_ref, dcar_ref):
        step = pl.program_id(0)
        c = nc - 1 - step

        @pl.when(step == 0)
        def _():
            acc_ref[...] = jnp.zeros_like(acc_ref)
            gwr_ref[...] = jnp.zeros_like(gwr_ref)
            gwi_ref[...] = jnp.zeros_like(gwi_ref)
            gcar_ref[...] = jnp.zeros_like(gcar_ref)
            acar_ref[...] = jnp.zeros_like(acar_ref)
            dcar_ref[...] = jnp.zeros_like(dcar_ref)

        u = u_ref[...]
        u_prev = jnp.where(c > 0, up_ref[...], 0.0)
        h_prev8 = jnp.where(c > 0, hp_ref[...], 0.0)
        a = a_ref[...]
        h = h_ref[...]
        uc, r, ig, mult = uc_ref[...], r_ref[...], ig_ref[...], mult_ref[...]
        ucb = uc.astype(BF16)
        sp = _softplus_neg(cw_ref[7:8, :])
        a_next = _rows_after(a, acar_ref[...], 1)
        G, first = _linear_scan(a_next, dh_ref[...], gcar_ref[0:1, :], reverse=True)
        gcar_ref[...] = jnp.broadcast_to(first, (8, D_MODEL))
        acar_ref[...] = jnp.broadcast_to(a[0:1, :], (8, D_MODEL))
        d_a = G * _rows_before(h, h_prev8, 1)
        d_mult = G * (ig * uc)
        d_ig = G * (mult * uc)
        duc = G * (mult * ig)
        d_log_a = d_a * a - d_mult * (a * a) / mult
        d_r = d_log_a * (-LRU_C * sp)
        d_sp = jnp.sum(d_log_a * (-LRU_C * r), axis=0, keepdims=True)
        dpre_r = d_r * r * (1.0 - r)
        dpre_i = d_ig * ig * (1.0 - ig)
        dprb = dpre_r.astype(BF16)
        dpib = dpre_i.astype(BF16)
        back = []
        for j in range(8):
            sl = slice(j * 128, (j + 1) * 128)
            back.append(_dot_nt(dprb[:, sl], wr_ref[j]) + _dot_nt(dpib[:, sl], wi_ref[j]))
            gwr_ref[j] += _dot_tn(ucb[:, sl], dprb[:, sl])
            gwi_ref[j] += _dot_tn(ucb[:, sl], dpib[:, sl])
        duc = duc + jnp.concatenate(back, axis=1)
        du = cw_ref[0:1, :] * duc
        duc_next8 = dcar_ref[...]
        acc_ref[0:1, :] += jnp.sum(duc * u, axis=0, keepdims=True)
        for j in range(1, CONV_WIDTH):
            du = du + cw_ref[j:j + 1, :] * _rows_after(duc, duc_next8, j)
            acc_ref[j:j + 1, :] += jnp.sum(duc * _rows_before(u, u_prev, j), axis=0, keepdims=True)
        dcar_ref[...] = duc[0:8, :]
        acc_ref[4:5, :] += jnp.sum(duc, axis=0, keepdims=True)
        acc_ref[5:6, :] += jnp.sum(dpre_r, axis=0, keepdims=True)
        acc_ref[6:7, :] += jnp.sum(dpre_i, axis=0, keepdims=True)
        acc_ref[7:8, :] += d_sp
        du_ref[0] = du[:, 0:ATT_WIDTH].astype(BF16)
        du_ref[1] = du[:, ATT_WIDTH:].astype(BF16)

    def rev(step):
        return nc - 1 - step

    def prev8(step):
        return jnp.maximum(rev(step) * (T // 8) - 1, 0)

    row_spec = pl.BlockSpec((T, D_MODEL), lambda s: (rev(s), 0))
    prev_spec = pl.BlockSpec((8, D_MODEL), lambda s: (prev8(s), 0))
    vec_spec = pl.BlockSpec((8, D_MODEL), lambda s: (0, 0))
    w_spec = pl.BlockSpec((8, 128, 128), lambda s: (0, 0, 0))
    return _call(
        body, (dhl, hl, hl, a_all, *gates, rest, rest, cw, wr2, wi2, dproj), name="lru_bwd", grid=(nc,),
        in_specs=[row_spec, row_spec, prev_spec, row_spec, row_spec, row_spec, row_spec, row_spec,
                  row_spec, prev_spec, vec_spec, w_spec, w_spec, ANY_SPEC],
        out_specs=(pl.BlockSpec((2, T, ATT_WIDTH), lambda s: (4, rev(s), 0)), vec_spec, w_spec, w_spec),
        out_shape=(jax.ShapeDtypeStruct(dproj.shape, BF16), jax.ShapeDtypeStruct((8, D_MODEL), F32),
                   jax.ShapeDtypeStruct((8, 128, 128), F32), jax.ShapeDtypeStruct((8, 128, 128), F32)),
        scratch_shapes=[pltpu.VMEM((8, D_MODEL), F32), pltpu.VMEM((8, D_MODEL), F32),
                        pltpu.VMEM((8, D_MODEL), F32)],
        aliases={13: 0}, hosted=hosted)


OUT_TM = 256


def _out_fwd(a_gated, hl, rest, x, vec, w_pa, w_pb, w_o, target=None, hosted=None):
    S = x.shape[0]
    tm = OUT_TM
    with_loss = target is not None

    def body(*refs):
        (ag_ref, hl_ref, gl_ref, ma_ref, mb_ref, x_ref, v_ref, wpa_ref, wpb_ref, wo_ref), refs = refs[:10], refs[10:]
        if with_loss:
            t_ref, refs = refs[0], refs[1:]
        xn_ref, mix_ref, ya_ref, yb_ref, bg_ref = refs[:5]
        bg = (hl_ref[...] * _silu(gl_ref[...])).astype(BF16)
        ya = _dot(ag_ref[...], wpa_ref[...])
        yb = _dot(bg, wpb_ref[...])
        mix = (jax.nn.sigmoid(ma_ref[...]) * ya + jax.nn.sigmoid(mb_ref[...]) * yb).astype(BF16)
        out = _dot(mix, wo_ref[...])
        rstd = lax.rsqrt(jnp.mean(out * out, axis=-1, keepdims=True) + NORM_EPS)
        x_new = x_ref[...] + v_ref[1:2, :] * ((out * rstd) * v_ref[0:1, :])
        if with_loss:
            acc_ref = refs[5]

            @pl.when(pl.program_id(0) == 0)
            def _():
                acc_ref[...] = jnp.zeros_like(acc_ref)

            err = x_new - t_ref[...]
            xn_ref[...] = err * (1.0 / D_MODEL)
            part = jnp.sum(jnp.sum(err * err, axis=1, keepdims=True), axis=0, keepdims=True)
            acc_ref[...] += jnp.broadcast_to(part, acc_ref.shape)
        else:
            xn_ref[...] = x_new
        mix_ref[...] = mix
        ya_ref[...] = ya.astype(BF16)
        yb_ref[...] = yb.astype(BF16)
        bg_ref[...] = bg

    def col(j):
        return pl.BlockSpec((tm, D_MODEL), lambda i: (i, j))

    def whole(shape):
        return pl.BlockSpec(shape, lambda i: (0, 0))

    row = col(0)
    bf = jax.ShapeDtypeStruct((S, D_MODEL), BF16)
    loss_in = ([target], [row]) if with_loss else ([], [])
    loss_out = ([jax.ShapeDtypeStruct((8, 128), F32)], [whole((8, 128))]) if with_loss else ([], [])
    return _call(
        body, (a_gated, hl, rest, rest, rest, x, vec, w_pa, w_pb, w_o, *loss_in[0]), name="out_fwd",
        grid=(S // tm,),
        in_specs=[pl.BlockSpec((tm, ATT_WIDTH), lambda i: (i, 0)), row, col(1), col(2), col(3), row,
                  whole((8, D_MODEL)), whole((ATT_WIDTH, D_MODEL)), whole((D_MODEL, D_MODEL)),
                  whole((D_MODEL, D_MODEL)), *loss_in[1]],
        out_specs=(row, row, row, row, row, *loss_out[1]),
        out_shape=(jax.ShapeDtypeStruct((S, D_MODEL), F32), bf, bf, bf, bf, *loss_out[0]),
        hosted=hosted)


def _out_bwd(dxn, mix, ya, yb, hl, rest, o_att, vec, w_pa, w_pb, w_o, hosted=None):
    S = dxn.shape[0]
    tm = OUT_TM
    forth = [_perm_matrix(r, tm) for r in DILATIONS[1:]]

    def body(dxn_ref, mix_ref, ya_ref, yb_ref, hl_ref, gl_ref, ma_ref, mb_ref, ga_ref, o_ref, v_ref,
             wpa_ref, wpb_ref, wo_ref, f1_ref, f2_ref,
             dout_ref, dya_ref, dyb_ref, dhl_ref, dp_ref, acc_ref,
             do0_ref, do1_ref, do2_ref, dv0_ref, dv1_ref, dv2_ref):
        i = pl.program_id(0)

        @pl.when(i == 0)
        def _():
            acc_ref[...] = jnp.zeros_like(acc_ref)

        g_post = v_ref[0:1, :]
        gate = v_ref[1:2, :]
        dxn_v = dxn_ref[...]
        out = _dot(mix_ref[...], wo_ref[...])
        rstd = lax.rsqrt(jnp.mean(out * out, axis=-1, keepdims=True) + NORM_EPS)
        nrm = out * rstd
        acc_ref[0:1, :] += jnp.sum(dxn_v * (nrm * g_post), axis=0, keepdims=True)
        acc_ref[1:2, :] += jnp.sum(dxn_v * gate * nrm, axis=0, keepdims=True)
        dn = dxn_v * (gate * g_post)
        dout = (rstd * (dn - nrm * jnp.mean(dn * nrm, axis=-1, keepdims=True))).astype(BF16)
        dout_ref[...] = dout
        dmix = _dot_nt(dout, wo_ref[...])
        sa = jax.nn.sigmoid(ma_ref[...])
        sb = jax.nn.sigmoid(mb_ref[...])
        dya = (dmix * sa).astype(BF16)
        dyb = (dmix * sb).astype(BF16)
        dya_ref[...] = dya
        dyb_ref[...] = dyb
        dma = dmix * ya_ref[...].astype(F32) * (sa * (1.0 - sa))
        dmb = dmix * yb_ref[...].astype(F32) * (sb * (1.0 - sb))
        d_ag = _dot_nt(dya, wpa_ref[...])
        d_bg = _dot_nt(dyb, wpb_ref[...])
        gl = gl_ref[...]
        hl_v = hl_ref[...]
        dhl_ref[...] = d_bg * _silu(gl)
        dgl = d_bg * hl_v * _dsilu(gl)
        ga = ga_ref[...]
        o = o_ref[...]
        do = d_ag * _silu(ga)
        dga = d_ag * o * _dsilu(ga)
        lane = lax.broadcasted_iota(jnp.int32, (tm, 128), 1)
        dvec = jnp.zeros((tm, 128), F32)
        for h in range(HEADS):
            sl = slice(h * HEAD_DIM, (h + 1) * HEAD_DIM)
            dvec = jnp.where(lane == h, jnp.sum(do[:, sl] * o[:, sl], axis=1, keepdims=True), dvec)
        do_b = do.astype(BF16)
        do0_ref[0] = do_b
        dv0_ref[0] = dvec
        for do_ref, dv_ref, f_ref in ((do1_ref, dv1_ref, f1_ref), (do2_ref, dv2_ref, f2_ref)):
            _store_residues(do_ref, _dot(f_ref[...], do_b).astype(BF16))
            _store_residues(dv_ref, _permute_f32(f_ref[...], dvec, 3))
        dp_ref[0] = dgl[:, 0:ATT_WIDTH].astype(BF16)
        dp_ref[1] = dgl[:, ATT_WIDTH:].astype(BF16)
        dp_ref[2] = dma[:, 0:ATT_WIDTH].astype(BF16)
        dp_ref[3] = dma[:, ATT_WIDTH:].astype(BF16)
        dp_ref[4] = dmb[:, 0:ATT_WIDTH].astype(BF16)
        dp_ref[5] = dmb[:, ATT_WIDTH:].astype(BF16)
        dp_ref[6] = dga.astype(BF16)

    def col(j):
        return pl.BlockSpec((tm, D_MODEL), lambda i: (i, j))

    def whole(shape):
        return pl.BlockSpec(shape, lambda i: (0, 0))

    def res_spec(r, width):
        return pl.BlockSpec((r, tm // r, width), lambda i: (0, i, 0))

    row = col(0)
    att = pl.BlockSpec((tm, ATT_WIDTH), lambda i: (i, 0))
    bf = jax.ShapeDtypeStruct((S, D_MODEL), BF16)
    vmem = pl.BlockSpec(memory_space=pltpu.VMEM)
    return _call(
        body, (dxn, mix, ya, yb, hl, rest, rest, rest, rest, o_att, vec, w_pa, w_pb, w_o, *forth),
        name="out_bwd", grid=(S // tm,),
        in_specs=[row, row, row, row, row, col(1), col(2), col(3),
                  pl.BlockSpec((tm, ATT_WIDTH), lambda i: (i, 8)), att,
                  whole((8, D_MODEL)), whole((ATT_WIDTH, D_MODEL)), whole((D_MODEL, D_MODEL)),
                  whole((D_MODEL, D_MODEL)), vmem, vmem],
        out_specs=(row, row, row, row,
                   pl.BlockSpec((7, tm, ATT_WIDTH), lambda i: (0, i, 0)), whole((8, D_MODEL)),
                   *[res_spec(r, ATT_WIDTH) for r in DILATIONS], *[res_spec(r, 128) for r in DILATIONS]),
        out_shape=(bf, bf, bf, jax.ShapeDtypeStruct((S, D_MODEL), F32),
                   jax.ShapeDtypeStruct((DREST_CHUNKS, S, ATT_WIDTH), BF16),
                   jax.ShapeDtypeStruct((8, D_MODEL), F32),
                   *[jax.ShapeDtypeStruct((r, S // r, ATT_WIDTH), BF16) for r in DILATIONS],
                   *[jax.ShapeDtypeStruct((r, S // r, 128), F32) for r in DILATIONS]),
        hosted=hosted)


def _adamw(gsrcs, w, m, v, n_stack, name, rows=None):
    n_layers, R, C = w.shape
    assert len(gsrcs) == n_layers
    budget = 96 * 1024
    tr = rows
    if tr is None:
        tr = R
        while tr * C > budget and tr % 16 == 0:
            tr //= 2
    assert R % tr == 0 and (tr % 8 == 0 or tr == R)
    c1 = 1.0 - ADAM_B1 ** ADAM_STEP
    c2 = 1.0 - ADAM_B2 ** ADAM_STEP

    def body(*refs):
        g_refs = refs[:n_layers]
        w_ref, m_ref, v_ref, go_ref, d_ref, mo_ref, vo_ref = refs[n_layers:]

        def update(g_ref):
            if n_stack:
                g = g_ref[0].astype(F32)
                for s in range(1, n_stack):
                    g = g + g_ref[s].astype(F32)
            else:
                g = g_ref[...]
            m_new = ADAM_B1 * m_ref[...] + (1.0 - ADAM_B1) * g
            v_new = ADAM_B2 * v_ref[...] + (1.0 - ADAM_B2) * (g * g)
            m_hat = m_new / c1
            v_hat = v_new / c2
            go_ref[...] = g
            d_ref[...] = -ADAM_LR * (m_hat / (jnp.sqrt(v_hat) + ADAM_EPS) + ADAM_WD * w_ref[...])
            mo_ref[...] = m_new
            vo_ref[...] = v_new

        for layer in range(n_layers):
            pl.when(pl.program_id(0) == layer)(functools.partial(update, g_refs[layer]))

    def g_spec(layer):
        def rows_of(l, i):
            return jnp.where(l == layer, i, 0)
        if n_stack:
            return pl.BlockSpec((n_stack, tr, C), lambda l, i: (0, rows_of(l, i), 0))
        return pl.BlockSpec((tr, C), lambda l, i: (rows_of(l, i), 0))

    blk = pl.BlockSpec((None, tr, C), lambda l, i: (l, i, 0))
    shp = jax.ShapeDtypeStruct((n_layers, R, C), F32)
    return pl.pallas_call(
        body, name=name, grid=(n_layers, R // tr),
        in_specs=[g_spec(layer) for layer in range(n_layers)] + [blk, blk, blk],
        out_specs=(blk, blk, blk, blk), out_shape=(shp, shp, shp, shp),
        compiler_params=_params(("arbitrary", "arbitrary")))(*gsrcs, w, m, v)


def _pair_blocks(w):
    w = w.reshape(8, 2, 64, 64)
    z = jnp.zeros((8, 64, 64), w.dtype)
    top = jnp.concatenate([w[:, 0], z], axis=2)
    bot = jnp.concatenate([z, w[:, 1]], axis=2)
    return jnp.concatenate([top, bot], axis=1).astype(BF16)


def _unpair_blocks(g):
    return jnp.stack([g[:, :64, :64], g[:, 64:, 64:]], axis=1).reshape(16, 64, 64)


def _mats_full(w_pa_g, w_pb_g, w_o_g):
    return dict(w_pa=jnp.transpose(w_pa_g, (1, 0, 2)).reshape(ATT_WIDTH, D_MODEL),
                w_pb=w_pb_g.reshape(D_MODEL, D_MODEL), w_o=w_o_g.reshape(D_MODEL, D_MODEL))


def _layer_params(layer, w_in_g, mats_g, conv_w_full, conv_b, b_rg, b_ig, lru_lambda, w_rg, w_ig, g_pre,
                  g_post):
    cw = jnp.concatenate([conv_w_full, conv_b[None], b_rg[None], b_ig[None], lru_lambda[None]], axis=0)
    p = dict(w_int=_weights_full(w_in_g), cw=cw, wr2=_pair_blocks(w_rg), wi2=_pair_blocks(w_ig),
             g_pre=g_pre, g_post=g_post, lru_lambda=lru_lambda)
    if mats_g is not None:
        p.update(_mats_full(*mats_g))
    return p


def _layer_fwd(x, mod, p, host_proj=None, host_lru=None, host_out=None, own_mats=None, target=None):
    zeros = jnp.zeros((5, D_MODEL), F32)
    vec_pre = jnp.concatenate([p["g_pre"][None], mod[1:2], mod[0:1], zeros], axis=0)
    (ht0, ht1, ht2, q0, q1, q2, rest), got_proj = _norm_proj_fwd(x, vec_pre, p["w_int"], hosted=host_proj)
    h_t = [ht0, ht1, ht2]
    hosts = [None, None, None]
    if own_mats is not None:
        hosts[0], hosts[1] = (list(own_mats[:2]), ["ag"] * 2), ([own_mats[2]], ["ag"])
    qkv, outs, stats, got_attn = [q0, q1, q2], [], [], []
    for g, r in enumerate(DILATIONS):
        (o_g, st_g), got = _attn_fwd(qkv[g], g, r, hosted=hosts[g])
        outs.append(o_g)
        stats.append(st_g)
        got_attn += got
    if own_mats is not None:
        p.update(_mats_full(*got_attn))
    o_att, a_gated, *lse = _attn_combine(outs, stats, rest)
    (hl, a_dec, *gates), got_lru = _lru_fwd(rest, p["cw"], p["wr2"], p["wi2"], hosted=host_lru)
    vec_post = jnp.concatenate([p["g_post"][None], mod[2:3], jnp.zeros((6, D_MODEL), F32)], axis=0)
    (x_new, mix, ya, yb, b_gated, *loss_acc), got_out = _out_fwd(
        a_gated, hl, rest, x, vec_post, p["w_pa"], p["w_pb"], p["w_o"], target=target, hosted=host_out)
    saved = dict(x=x, h_t=h_t, qkv=qkv, rest=rest, o_att=o_att, a_gated=a_gated, lse=lse, hl=hl, a_dec=a_dec,
                 gates=gates, mix=mix, ya=ya, yb=yb, b_gated=b_gated, vec_pre=vec_pre, vec_post=vec_post)
    return (x_new, *loss_acc), saved, got_proj, got_lru, got_out


def _mats_sources(g_pa, g_pb, g_o):
    return [jnp.transpose(g_pa.astype(BF16).reshape(ATT_WIDTH, N_DEV, 128), (1, 0, 2)),
            g_pb.astype(BF16).reshape(N_DEV, 128, D_MODEL), g_o.astype(BF16).reshape(N_DEV, 128, D_MODEL)]


LRU_SMALL = ("conv_b", "w_rg", "b_rg", "w_ig", "b_ig", "lru_lambda")
MOD_SMALL = ("b_mod", "g_pre", "g_post")


def _flat_rows(arrs):
    return jnp.concatenate([a.reshape(-1) for a in arrs]).reshape(-1, 128)


def _layer_bwd(dxn, sv, p, upper_small=None):
    a2a3 = ["a2a"] * 3
    (dout, dya, dyb, dhl, drest, acc_out, *do_dvec), _ = _out_bwd(
        dxn, sv["mix"], sv["ya"], sv["yb"], sv["hl"], sv["rest"], sv["o_att"], sv["vec_post"],
        p["w_pa"], p["w_pb"], p["w_o"])
    mats = _mats_sources(_wgrad_tn(sv["a_gated"], dya, "wgrad_pa"), _wgrad_tn(sv["b_gated"], dyb, "wgrad_pb"),
                         _wgrad_tn(sv["mix"], dout, "wgrad_o"))
    (drest, acc_lru, gwr2, gwi2), _ = _lru_bwd(
        dhl, sv["hl"], sv["a_dec"], sv["gates"], sv["rest"], p["cw"], p["wr2"], p["wi2"], drest)
    small = dict(
        conv_w=acc_lru[0:4], conv_b=acc_lru[4], b_rg=acc_lru[5], b_ig=acc_lru[6],
        lru_lambda=acc_lru[7] * (-jax.nn.sigmoid(-p["lru_lambda"])),
        w_rg=_unpair_blocks(gwr2), w_ig=_unpair_blocks(gwi2))
    hosts = [(mats, a2a3), None, None]
    if upper_small is not None:
        for half, names in enumerate((LRU_SMALL[:3], LRU_SMALL[3:] + ("conv_w",))):
            vec = _flat_rows([jnp.stack([small[n], upper_small[n]]) for n in names])
            hosts[1 + half] = ([vec], ["ag"])
    dqkv, got_attn = [], []
    for g, r in enumerate(DILATIONS):
        d_g, got = _attn_bwd(sv["qkv"][g], do_dvec[g], sv["lse"][g], do_dvec[3 + g], g, r, hosted=hosts[g])
        dqkv.append(d_g)
        got_attn.append(got)
    shards = None
    for g, r in enumerate(DILATIONS):
        shards = _proj_wgrad_group(sv["h_t"][g], dqkv[g], g, r, shards, f"proj_wgrad_g{g}")
    shards = _proj_wgrad_part(sv["h_t"][0], drest, 2, 4, QKV_CHUNKS, shards, "proj_wgrad_u")
    shards = _proj_wgrad_part(sv["h_t"][0], drest, 7, 0, QKV_CHUNKS + 2, shards, "proj_wgrad_rest")
    (dx, acc_pre), got_in = _proj_dgrad_norm_bwd(
        dqkv, drest, p["w_int"], sv["x"], dxn, sv["vec_pre"], hosted=([shards], ["a2a"]))
    small.update(dmod=jnp.concatenate([acc_pre[0], acc_pre[1], acc_out[0]]), g_pre=acc_pre[2],
                 g_post=acc_out[1])
    return dx, small, dict(mats=got_attn[0], w_in=got_in[0], lru_small=got_attn[1] + got_attn[2])


def kernel(x, c, w_mod, b_mod, g_pre, w_in, conv_w, conv_b, w_rg, b_rg, w_ig, b_ig, lru_lambda, w_pa, w_pb, w_o, g_post, loss_target, m_w_mod, m_b_mod, m_g_pre, m_w_in, m_conv_w, m_conv_b, m_w_rg, m_b_rg, m_w_ig, m_b_ig, m_lru_lambda, m_w_pa, m_w_pb, m_w_o, m_g_post, v_w_mod, v_b_mod, v_g_pre, v_w_in, v_conv_w, v_conv_b, v_w_rg, v_b_rg, v_w_ig, v_b_ig, v_lru_lambda, v_w_pa, v_w_pb, v_w_o, v_g_post):
    W = dict(w_mod=w_mod, b_mod=b_mod, g_pre=g_pre, w_in=w_in, conv_w=conv_w, conv_b=conv_b, w_rg=w_rg,
             b_rg=b_rg, w_ig=w_ig, b_ig=b_ig, lru_lambda=lru_lambda, w_pa=w_pa, w_pb=w_pb, w_o=w_o,
             g_post=g_post)
    M = dict(w_mod=m_w_mod, b_mod=m_b_mod, g_pre=m_g_pre, w_in=m_w_in, conv_w=m_conv_w, conv_b=m_conv_b,
             w_rg=m_w_rg, b_rg=m_b_rg, w_ig=m_w_ig, b_ig=m_b_ig, lru_lambda=m_lru_lambda, w_pa=m_w_pa,
             w_pb=m_w_pb, w_o=m_w_o, g_post=m_g_post)
    V = dict(w_mod=v_w_mod, b_mod=v_b_mod, g_pre=v_g_pre, w_in=v_w_in, conv_w=v_conv_w, conv_b=v_conv_b,
             w_rg=v_w_rg, b_rg=v_b_rg, w_ig=v_w_ig, b_ig=v_b_ig, lru_lambda=v_lru_lambda, w_pa=v_w_pa,
             w_pb=v_w_pb, w_o=v_w_o, g_post=v_g_post)
    S = x.shape[1]
    me = 4 * lax.axis_index("x") + 2 * lax.axis_index("y") + lax.axis_index("c")
    n_mod = w_mod.shape[2]
    n_in = w_in.shape[2]
    n_conv = conv_w.shape[2]

    c_rows = jnp.broadcast_to(c, (8, D_MODEL))
    w_in_b, w_pa_b, w_pb_b, w_o_b = (t.astype(BF16) for t in (w_in, w_pa, w_pb, w_o))
    g_c, g_win0, g_cw = _gather_two_level([c_rows, w_in_b[0], conv_w], "gather_weights")
    c_all = g_c[:, 0, :]
    c_pad = jnp.concatenate([c_all, jnp.zeros((8, D_MODEL), F32)], axis=0)
    conv_w_full = jnp.transpose(g_cw, (1, 2, 0, 3)).reshape(2, CONV_WIDTH, D_MODEL)

    mod_cols = _mod_fwd(c_pad, w_mod)
    mod_src = jnp.transpose(mod_cols[:, :8], (1, 0, 2))
    mod_src = jnp.concatenate([mod_src, jnp.zeros((8, 6, n_mod), F32)], axis=1)
    (mod_got,) = _exchange([mod_src], ["a2a"], "scatter_mod")
    mod = jnp.transpose(mod_got[:, :2], (1, 0, 2)).reshape(2, 3 * D_MODEL) + b_mod

    def layer_params(layer, w_in_g, mats_g):
        return _layer_params(layer, w_in_g, mats_g, conv_w_full[layer], conv_b[layer], b_rg[layer],
                             b_ig[layer], lru_lambda[layer], w_rg[layer], w_ig[layer], g_pre[layer],
                             g_post[layer])

    layers = [layer_params(0, g_win0, None), None]
    (act,), sv0, (g_win1,), _, mats1_g = _layer_fwd(
        x[0], mod[0].reshape(3, D_MODEL), layers[0], own_mats=(w_pa_b[0], w_pb_b[0], w_o_b[0]),
        host_proj=([w_in_b[1]], ["ag"]), host_out=([w_pa_b[1], w_pb_b[1], w_o_b[1]], ["ag"] * 3))
    layers[1] = layer_params(1, g_win1, mats1_g)
    (dy, loss_acc), sv1, _, _, _ = _layer_fwd(act, mod[1].reshape(3, D_MODEL), layers[1],
                                              target=loss_target[0])

    dy, small1, got1 = _layer_bwd(dy, sv1, layers[1])
    dy, small0, got0 = _layer_bwd(dy, sv0, layers[0], upper_small=small1)
    grad_x = dy[None]
    r_pa, r_pb, r_o = ([got0["mats"][k], got1["mats"][k]] for k in range(3))
    r_in = [got0["w_in"], got1["w_in"]]
    grads = [small0, small1]

    def stack2(name):
        return jnp.stack([grads[0][name], grads[1][name]], axis=0)

    dmod = stack2("dmod")
    dmod_src = jnp.transpose(dmod.reshape(2, N_DEV, n_mod), (1, 0, 2))
    dmod_src = jnp.concatenate([dmod_src, jnp.zeros((8, 6, n_mod), F32)], axis=1)
    mod_vec = _flat_rows([dmod, stack2("g_pre"), stack2("g_post"), loss_acc])
    r_dmod, r_mod_small = _exchange([dmod_src, mod_vec], ["a2a", "ag"], "exchange_grads")
    mod_rows = mod_vec.shape[0] - loss_acc.shape[0]
    loss = 0.5 * jnp.sum(r_mod_small[:, mod_rows, 0]) / D_MODEL
    r_lru_small = got0["lru_small"]

    res = {}
    res["w_in"] = _adamw(r_in, w_in, m_w_in, v_w_in, 8, "adamw_w_in")
    res["w_pa"] = _adamw(r_pa, w_pa, m_w_pa, v_w_pa, 8, "adamw_w_pa")
    res["w_pb"] = _adamw(r_pb, w_pb, m_w_pb, v_w_pb, 8, "adamw_w_pb")
    res["w_o"] = _adamw(r_o, w_o, m_w_o, v_w_o, 8, "adamw_w_o")
    dmod_all = jnp.transpose(r_dmod[:, :2], (1, 0, 2))
    gw_mod = _mod_wgrad(jnp.transpose(c_all), dmod_all)
    res["w_mod"] = _adamw([gw_mod[0], gw_mod[1]], w_mod, m_w_mod, v_w_mod, 0, "adamw_w_mod")
    for names_rep, stack, tag in ((LRU_SMALL[:3], r_lru_small[0], "lru_a"), (LRU_SMALL[3:], r_lru_small[1], "lru_b"),
                                  (MOD_SMALL, r_mod_small, "mod")):
        rows = sum(W[n].size for n in names_rep) // 128
        rep = _adamw([stack], *(_flat_rows([src[n] for n in names_rep])[None] for src in (W, M, V)), 8,
                     f"adamw_small_{tag}", rows=rows // 2 if rows % 16 == 0 else rows)
        off = 0
        for name in names_rep:
            size = W[name].size
            res[name] = tuple(t.reshape(-1)[off:off + size].reshape(W[name].shape) for t in rep)
            off += size
    lru_rows = sum(W[n].size for n in LRU_SMALL[3:]) // 128
    conv_stack = r_lru_small[1][:, lru_rows:].reshape(8, 2, CONV_WIDTH, D_MODEL)
    conv_stack = lax.dynamic_slice_in_dim(conv_stack, me * n_conv, n_conv, axis=3).reshape(8, 8, n_conv)
    res["conv_w"] = _adamw([conv_stack], conv_w.reshape(1, 8, n_conv), m_conv_w.reshape(1, 8, n_conv),
                           v_conv_w.reshape(1, 8, n_conv), 8, "adamw_conv_w")

    names = ("w_mod", "b_mod", "g_pre", "w_in", "conv_w", "conv_b", "w_rg", "b_rg", "w_ig", "b_ig",
             "lru_lambda", "w_pa", "w_pb", "w_o", "g_post")
    outs = [loss, grad_x]
    for k in range(4):
        outs.extend(res[n][k].reshape(W[n].shape) for n in names)
    return tuple(outs)
```

```python
import functools

import jax
import jax.numpy as jnp
import numpy as np
from jax import lax
from jax.experimental import pallas as pl
from jax.experimental.pallas import tpu as pltpu

F32 = jnp.float32
BF16 = jnp.bfloat16

N_DEV = 8
D_MODEL = 1024
HEAD_DIM = 128
HEADS = 4
ATT_WIDTH = HEADS * HEAD_DIM
DILATIONS = (1, 4, 16)
BAND = 128
N_CHUNKS = 18
QKV_CHUNKS = 9
CONV_WIDTH = 4
LRU_C = 8.0
NORM_EPS = 1e-6
NEG_INF = -1e30
ADAM_LR = 0.001
ADAM_B1 = 0.9
ADAM_B2 = 0.999
ADAM_EPS = 1e-08
ADAM_WD = 0.01
ADAM_STEP = 10
CHUNK_PERM = (0, 3, 6, 1, 4, 7, 2, 5, 8, 10, 11, 12, 13, 14, 15, 16, 17, 9)
DREST_CHUNKS = 10
VMEM_LIMIT = 56 * 1024 * 1024


def _params(sem=None):
    return pltpu.CompilerParams(dimension_semantics=sem, vmem_limit_bytes=VMEM_LIMIT)


def _silu(x):
    return x * jax.nn.sigmoid(x)


def _dsilu(x):
    s = jax.nn.sigmoid(x)
    return s * (1.0 + x * (1.0 - s))


def _neg_expm1(x):
    series = -x * (1.0 + x * (0.5 + x * (1.0 / 6.0 + x * (1.0 / 24.0))))
    return jnp.where(x > -0.05, series, 1.0 - jnp.exp(x))


def _softplus_neg(lam):
    z = jnp.exp(-jnp.abs(lam))
    small = z * (1.0 - z * (0.5 - z * (1.0 / 3.0 - z * 0.25)))
    log1p_z = jnp.where(z < 1e-2, small, jnp.log(1.0 + z))
    return jnp.maximum(-lam, 0.0) + log1p_z


def _dot(a, b):
    return jnp.dot(a, b, preferred_element_type=F32)


def _dot_nt(a, b):
    return lax.dot_general(a, b, (((1,), (1,)), ((), ())), preferred_element_type=F32)


def _dot_tn(a, b):
    return lax.dot_general(a, b, (((0,), (0,)), ((), ())), preferred_element_type=F32)


def _perm_matrix(r, rows, transpose=False):
    n = rows // r
    p = np.zeros((rows, rows), np.float32)
    dst = np.arange(rows)
    p[dst, (dst % n) * r + dst // n] = 1.0
    return jnp.asarray(p.T if transpose else p, dtype=BF16)


def _permute_f32(p, x, pieces):
    part = x.astype(BF16)
    acc = _dot(p, part)
    for _ in range(pieces - 1):
        x = x - part.astype(F32)
        part = x.astype(BF16)
        acc = acc + _dot(p, part)
    return acc


def _rows_of_residues(ref, lead=()):
    r = ref.shape[len(lead)]
    if r == 1:
        return ref[lead + (0,)]
    return jnp.concatenate([ref[lead + (rho,)] for rho in range(r)], axis=0)


def _store_residues(ref, x, lead=()):
    r = ref.shape[len(lead)]
    n = x.shape[0] // r
    for rho in range(r):
        ref[lead + (rho,)] = x[rho * n:(rho + 1) * n]


ANY_SPEC = pl.BlockSpec(memory_space=pl.ANY)


def _exchange_shapes(arrs, modes):
    return [jax.ShapeDtypeStruct(((N_DEV,) + a.shape) if mode == "ag" else a.shape, a.dtype)
            for a, mode in zip(arrs, modes)]


def _exchange_sems(n):
    return [pltpu.SemaphoreType.DMA((7 * n,)), pltpu.SemaphoreType.DMA((7 * n,)),
            pltpu.SemaphoreType.DMA((n,))]


def _exchange_copies(ins, outs, modes, sems):
    n = len(ins)
    send_sems, recv_sems, local_sems = sems
    x, y, c = lax.axis_index("x"), lax.axis_index("y"), lax.axis_index("c")
    me = 4 * x + 2 * y + c

    def src_for(a, dev):
        return ins[a] if modes[a] == "ag" else ins[a].at[dev]

    local = [pltpu.make_async_copy(src_for(a, me), outs[a].at[me], local_sems.at[a]) for a in range(n)]
    sends, arrivals = [], []
    for k in range(1, N_DEV):
        px = 1 - x if (k >> 2) & 1 else x
        py = 1 - y if (k >> 1) & 1 else y
        pc = 1 - c if k & 1 else c
        peer = 4 * px + 2 * py + pc
        for a in range(n):
            s = (k - 1) * n + a
            for dst, group in ((me, sends), (peer, arrivals)):
                group.append(pltpu.make_async_remote_copy(
                    src_ref=src_for(a, peer), dst_ref=outs[a].at[dst],
                    send_sem=send_sems.at[s], recv_sem=recv_sems.at[s],
                    device_id=(px, py, pc), device_id_type=pl.DeviceIdType.MESH))
    return local, sends, arrivals


def _exchange_start(copies):
    local, sends, _ = copies
    for cp in local + sends:
        cp.start()


def _exchange_wait(copies):
    local, sends, arrivals = copies
    for cp in arrivals:
        cp.wait_recv()
    for cp in sends:
        cp.wait_send()
    for cp in local:
        cp.wait()


def _exchange(arrs, modes, name):
    n = len(arrs)

    def body(*refs):
        copies = _exchange_copies(refs[:n], refs[n:2 * n], modes, refs[2 * n:])
        _exchange_start(copies)
        _exchange_wait(copies)

    outs = pl.pallas_call(
        body, name=name, out_shape=tuple(_exchange_shapes(arrs, modes)),
        in_specs=[ANY_SPEC] * n, out_specs=tuple([ANY_SPEC] * n),
        scratch_shapes=_exchange_sems(n),
    )(*arrs)
    return list(outs)


def _gather_two_level(arrs, name):
    n = len(arrs)

    def body(*refs):
        ins, outs = refs[:n], refs[n:2 * n]
        send_sems, recv_sems, local_sems = refs[2 * n:]
        x, y, c = lax.axis_index("x"), lax.axis_index("y"), lax.axis_index("c")
        me, sibling = (x, y, c), (x, y, 1 - c)
        chips = [(1 - x, y), (x, 1 - y), (1 - x, 1 - y)]

        def copy(a, k, block, to, src=None):
            slot = 4 * block[0] + 2 * block[1] + block[2]
            return pltpu.make_async_remote_copy(
                src_ref=outs[a].at[slot] if src is None else src, dst_ref=outs[a].at[slot],
                send_sem=send_sems.at[7 * a + k], recv_sem=recv_sems.at[7 * a + k],
                device_id=to, device_id_type=pl.DeviceIdType.MESH)

        mine = [pltpu.make_async_copy(ins[a], outs[a].at[4 * x + 2 * y + c], local_sems.at[a])
                for a in range(n)]
        first = []
        for a in range(n):
            first.append(copy(a, 0, me, sibling, src=ins[a]))
            first += [copy(a, 1 + j, me, (*chip, c), src=ins[a]) for j, chip in enumerate(chips)]
        for cp in mine + first:
            cp.start()
        passed = []
        for j, chip in enumerate(chips):
            for a in range(n):
                copy(a, 1 + j, (*chip, c), me).wait_recv()
                passed.append(copy(a, 4 + j, (*chip, c), sibling))
                passed[-1].start()
        for a in range(n):
            copy(a, 0, sibling, me).wait_recv()
            for j, chip in enumerate(chips):
                copy(a, 4 + j, (*chip, 1 - c), me).wait_recv()
        for cp in first + passed:
            cp.wait_send()
        for cp in mine:
            cp.wait()

    outs = pl.pallas_call(
        body, name=name, out_shape=tuple(_exchange_shapes(arrs, ["ag"] * n)),
        in_specs=[ANY_SPEC] * n, out_specs=tuple([ANY_SPEC] * n),
        scratch_shapes=_exchange_sems(n),
    )(*arrs)
    return list(outs)


SHARD_COLS = N_CHUNKS * ATT_WIDTH // N_DEV


def _weights_full(shards):
    return jnp.transpose(shards, (1, 0, 2)).reshape(D_MODEL, N_CHUNKS * ATT_WIDTH)


def _wcols(w_ref, k):
    orig = CHUNK_PERM[k]
    return w_ref[:, orig * ATT_WIDTH:(orig + 1) * ATT_WIDTH]


def _store_shard_columns(stage_ref, shards_ref, sems, first_chunk):
    copies = []
    for t in range(stage_ref.shape[1] // ATT_WIDTH):
        col = ATT_WIDTH * CHUNK_PERM[first_chunk + t]
        end = col + ATT_WIDTH
        while col < end:
            dev = col // SHARD_COLS
            upto = min(end, (dev + 1) * SHARD_COLS)
            src = stage_ref.at[:, pl.ds(t * ATT_WIDTH + ATT_WIDTH - (end - col), upto - col)]
            dst = shards_ref.at[dev, :, pl.ds(col - dev * SHARD_COLS, upto - col)]
            copies.append(pltpu.make_async_copy(src, dst, sems.at[len(copies)]))
            col = upto
    for cp in copies:
        cp.start()
    for cp in copies:
        cp.wait()


SHARD_COPIES_PER_CHUNK = 2


def _call(body, args, *, name, grid, in_specs, out_specs, out_shape, scratch_shapes=(), aliases=None,
          hosted=None):
    sem = ("arbitrary",) * len(grid)
    if hosted is None:
        outs = pl.pallas_call(
            body, name=name, grid=grid, in_specs=in_specs, out_specs=tuple(out_specs),
            out_shape=tuple(out_shape), scratch_shapes=list(scratch_shapes),
            input_output_aliases=aliases or {}, compiler_params=_params(sem))(*args)
        return list(outs), []
    x_arrs, modes = hosted
    n_in, n_out, n_scr, nx = len(args), len(out_shape), len(scratch_shapes), len(x_arrs)

    def wrapped(*refs):
        ins, x_ins = refs[:n_in], refs[n_in:n_in + nx]
        outs = refs[n_in + nx:n_in + nx + n_out]
        x_outs = refs[n_in + nx + n_out:n_in + 2 * nx + n_out]
        scr = refs[n_in + 2 * nx + n_out:n_in + 2 * nx + n_out + n_scr]
        sems = refs[n_in + 2 * nx + n_out + n_scr:]
        first = pl.program_id(0) == 0
        last = pl.program_id(0) == grid[0] - 1
        for axis in range(1, len(grid)):
            first = jnp.logical_and(first, pl.program_id(axis) == 0)
            last = jnp.logical_and(last, pl.program_id(axis) == grid[axis] - 1)

        @pl.when(first)
        def _():
            _exchange_start(_exchange_copies(x_ins, x_outs, modes, sems))

        body(*ins, *outs, *scr)

        @pl.when(last)
        def _():
            _exchange_wait(_exchange_copies(x_ins, x_outs, modes, sems))

    outs = pl.pallas_call(
        wrapped, name=name, grid=grid, in_specs=list(in_specs) + [ANY_SPEC] * nx,
        out_specs=tuple(out_specs) + tuple([ANY_SPEC] * nx),
        out_shape=tuple(out_shape) + tuple(_exchange_shapes(x_arrs, modes)),
        scratch_shapes=list(scratch_shapes) + _exchange_sems(nx),
        input_output_aliases=aliases or {}, compiler_params=_params(sem))(*args, *x_arrs)
    return list(outs[:n_out]), list(outs[n_out:])


def _mod_fwd(c_pad, w_mod):
    def body(c_ref, w_ref, o_ref):
        sc = _silu(c_ref[...]).astype(BF16)
        for layer in range(2):
            o_ref[layer] = _dot(sc, w_ref[layer].astype(BF16))

    return pl.pallas_call(
        body, name="mod_fwd", out_shape=jax.ShapeDtypeStruct((2, 16, w_mod.shape[2]), F32),
        compiler_params=_params())(c_pad, w_mod)


def _mod_wgrad(c_t, dmod):
    n_cols = dmod.shape[2]

    def body(c_ref, d_ref, o_ref):
        sc = _silu(c_ref[...]).astype(BF16).astype(F32)
        for layer in range(2):
            dm = d_ref[layer].astype(BF16).astype(F32)
            acc = sc[:, 0:1] * dm[0:1, :]
            for b in range(1, N_DEV):
                acc = acc + sc[:, b:b + 1] * dm[b:b + 1, :]
            o_ref[layer] = acc

    return pl.pallas_call(
        body, name="mod_wgrad", out_shape=jax.ShapeDtypeStruct((2, D_MODEL, n_cols), F32),
        compiler_params=_params())(c_t, dmod)


GROUP_COLS = 3 * ATT_WIDTH
PROJ_TM = 256


def _norm_proj_fwd(x, vec, w_int, hosted=None):
    S = x.shape[0]
    tm = PROJ_TM
    perms = [_perm_matrix(r, tm) for r in DILATIONS[1:]]

    def body(x_ref, v_ref, w_ref, p1_ref, p2_ref, ht0_ref, ht1_ref, ht2_ref, q0_ref, q1_ref, q2_ref,
             rest_ref):
        xv = x_ref[...]
        rstd = lax.rsqrt(jnp.mean(xv * xv, axis=-1, keepdims=True) + NORM_EPS)
        hf = ((xv * rstd) * v_ref[0:1, :]) * (1.0 + v_ref[1:2, :]) + v_ref[2:3, :]
        h = hf.astype(BF16)
        for g, (q_ref, ht_ref, p_ref) in enumerate(((q0_ref, ht0_ref, None), (q1_ref, ht1_ref, p1_ref),
                                                    (q2_ref, ht2_ref, p2_ref))):
            rows_f = hf if p_ref is None else _dot(p_ref[...], h)
            ht_ref[...] = rows_f.T.astype(BF16)
            rows = rows_f.astype(BF16)
            n = tm // q_ref.shape[0]
            for t in range(3):
                res = _dot(rows, _wcols(w_ref, 3 * g + t)).astype(BF16)
                for rho in range(q_ref.shape[0]):
                    q_ref[rho, :, t * ATT_WIDTH:(t + 1) * ATT_WIDTH] = res[rho * n:(rho + 1) * n]
        split = (QKV_CHUNKS - 1) * ATT_WIDTH
        rest_ref[:, :split] = _dot(h, w_ref[:, (QKV_CHUNKS + 1) * ATT_WIDTH:])
        rest_ref[:, split:] = _dot(h, w_ref[:, QKV_CHUNKS * ATT_WIDTH:(QKV_CHUNKS + 1) * ATT_WIDTH])

    n_rest = QKV_CHUNKS * ATT_WIDTH
    whole = pl.BlockSpec(memory_space=pltpu.VMEM)
    ht_spec = pl.BlockSpec((D_MODEL, tm), lambda i: (0, i))
    ht_shape = jax.ShapeDtypeStruct((D_MODEL, S), BF16)
    return _call(
        body, (x, vec, w_int, *perms), name="norm_proj_fwd", grid=(S // tm,),
        in_specs=[pl.BlockSpec((tm, D_MODEL), lambda i: (i, 0)),
                  pl.BlockSpec((8, D_MODEL), lambda i: (0, 0)), whole, whole, whole],
        out_specs=(ht_spec, ht_spec, ht_spec,
                   *[pl.BlockSpec((r, tm // r, GROUP_COLS), lambda i: (0, i, 0)) for r in DILATIONS],
                   pl.BlockSpec((tm, n_rest), lambda i: (i, 0))),
        out_shape=(ht_shape, ht_shape, ht_shape,
                   *[jax.ShapeDtypeStruct((r, S // r, GROUP_COLS), BF16) for r in DILATIONS],
                   jax.ShapeDtypeStruct((S, n_rest), F32)),
        hosted=hosted)


def _proj_dgrad_norm_bwd(dqkv, drest, w_int, x, dxn, vec, hosted=None):
    S = x.shape[0]
    tm = PROJ_TM
    perms = [_perm_matrix(r, tm, transpose=True) for r in DILATIONS[1:]]

    def body(d0_ref, d1_ref, d2_ref, dr_ref, w_ref, p1_ref, p2_ref, x_ref, dxn_ref, v_ref, dx_ref, acc_ref):
        i = pl.program_id(0)

        @pl.when(i == 0)
        def _():
            acc_ref[...] = jnp.zeros_like(acc_ref)

        wcols = functools.partial(_wcols, w_ref)
        dhv = None
        for g, (d_ref, p_ref) in enumerate(((d0_ref, None), (d1_ref, p1_ref), (d2_ref, p2_ref))):
            for t in range(3):
                d = _rows_of_residues(d_ref, (t,))
                if p_ref is not None:
                    d = _dot(p_ref[...], d).astype(BF16)
                part = _dot_nt(d, wcols(3 * g + t))
                dhv = part if dhv is None else dhv + part
        for t in range(7):
            dhv = dhv + _dot_nt(dr_ref[t], wcols(11 + t))
        for t in range(2):
            dhv = dhv + _dot_nt(dr_ref[8 + t], wcols(9 + t))

        xv = x_ref[...]
        g = v_ref[0:1, :]
        sc1 = 1.0 + v_ref[1:2, :]
        rstd = lax.rsqrt(jnp.mean(xv * xv, axis=-1, keepdims=True) + NORM_EPS)
        xhat = xv * rstd
        acc_ref[0:1, :] += jnp.sum(dhv, axis=0, keepdims=True)
        acc_ref[1:2, :] += jnp.sum(dhv * (xhat * g), axis=0, keepdims=True)
        acc_ref[2:3, :] += jnp.sum(dhv * xhat * sc1, axis=0, keepdims=True)
        dxhat = dhv * (g * sc1)
        dx = rstd * (dxhat - xhat * jnp.mean(dxhat * xhat, axis=-1, keepdims=True))
        dx_ref[...] = dx + dxn_ref[...]

    row = pl.BlockSpec((tm, D_MODEL), lambda i: (i, 0))
    vec_spec = pl.BlockSpec((8, D_MODEL), lambda i: (0, 0))
    whole = pl.BlockSpec(memory_space=pltpu.VMEM)
    return _call(
        body, (*dqkv, drest, w_int, *perms, x, dxn, vec), name="proj_dgrad_norm_bwd", grid=(S // tm,),
        in_specs=[*[pl.BlockSpec((3, r, tm // r, ATT_WIDTH), lambda i: (0, 0, i, 0)) for r in DILATIONS],
                  pl.BlockSpec((DREST_CHUNKS, tm, ATT_WIDTH), lambda i: (0, i, 0)),
                  whole, whole, whole, row, row, vec_spec],
        out_specs=(row, vec_spec),
        out_shape=(jax.ShapeDtypeStruct((S, D_MODEL), F32), jax.ShapeDtypeStruct((8, D_MODEL), F32)),
        hosted=hosted)


WGRAD_TS = 512


def _proj_wgrad_call(h_t, d, d_spec, rows_of, n_chunks, first_chunk, shards, name):
    S = h_t.shape[1]
    n_steps = S // WGRAD_TS
    width = n_chunks * ATT_WIDTH

    def body(*refs):
        h_ref, d_ref = refs[:2]
        shards_ref, acc_ref, stage_ref, sems = refs[-4:]
        i = pl.program_id(0)

        @pl.when(i == 0)
        def _():
            acc_ref[...] = jnp.zeros_like(acc_ref)

        ht = h_ref[...]
        for t in range(n_chunks):
            acc_ref[:, t * ATT_WIDTH:(t + 1) * ATT_WIDTH] += _dot(ht, rows_of(d_ref, t))

        @pl.when(i == n_steps - 1)
        def _():
            stage_ref[...] = acc_ref[...].astype(BF16)
            _store_shard_columns(stage_ref, shards_ref, sems, first_chunk)

    chained = shards is not None
    return pl.pallas_call(
        body, name=name, grid=(n_steps,),
        in_specs=[pl.BlockSpec((D_MODEL, WGRAD_TS), lambda i: (0, i)), d_spec] + [ANY_SPEC] * chained,
        out_specs=ANY_SPEC,
        out_shape=jax.ShapeDtypeStruct((N_DEV, D_MODEL, SHARD_COLS), BF16),
        scratch_shapes=[pltpu.VMEM((D_MODEL, width), F32), pltpu.VMEM((D_MODEL, width), BF16),
                        pltpu.SemaphoreType.DMA((SHARD_COPIES_PER_CHUNK * n_chunks,))],
        input_output_aliases={2: 0} if chained else {},
        compiler_params=_params(("arbitrary",)))(h_t, d, *([shards] * chained))


def _proj_wgrad_part(h_t, d, n_chunks, chunk_block, first_chunk, shards, name):
    spec = pl.BlockSpec((n_chunks, WGRAD_TS, ATT_WIDTH), lambda i: (chunk_block, i, 0))
    return _proj_wgrad_call(h_t, d, spec, lambda d_ref, t: d_ref[t], n_chunks, first_chunk, shards, name)


def _proj_wgrad_group(h_t, d, g, r, shards, name):
    n = PROJ_TM // r

    def rows_of(d_ref, t):
        if r == 1:
            return d_ref[t, 0]
        return jnp.concatenate([d_ref[t, rho, s * n:(s + 1) * n] for s in range(WGRAD_TS // PROJ_TM)
                                for rho in range(r)], axis=0)

    spec = pl.BlockSpec((3, r, WGRAD_TS // r, ATT_WIDTH), lambda i: (0, 0, i, 0))
    return _proj_wgrad_call(h_t, d, spec, rows_of, 3, 3 * g, shards, name)


def _wgrad_tn(a, b, name):
    S, M = a.shape
    N = b.shape[1]
    ts = 1024

    def body(a_ref, b_ref, o_ref):
        i = pl.program_id(0)
        part = _dot_tn(a_ref[...], b_ref[...])

        @pl.when(i == 0)
        def _():
            o_ref[...] = part

        @pl.when(i > 0)
        def _():
            o_ref[...] += part

    return pl.pallas_call(
        body, name=name, grid=(S // ts,),
        in_specs=[pl.BlockSpec((ts, M), lambda i: (i, 0)), pl.BlockSpec((ts, N), lambda i: (i, 0))],
        out_specs=pl.BlockSpec((M, N), lambda i: (0, 0)),
        out_shape=jax.ShapeDtypeStruct((M, N), F32),
        compiler_params=_params(("arbitrary",)))(a, b)


ATTN_FWD_BLOCKS = 8


def _attn_fwd(qkv, g, r, hosted=None):
    L = qkv.shape[1]
    qb = min(ATTN_FWD_BLOCKS, L // BAND)
    n_steps = L // (qb * BAND)
    scale = HEAD_DIM ** -0.5

    def body(q_ref, kp_ref, kc_ref, vp_ref, vc_ref, o_ref, st_ref):
        m_step = pl.program_id(1)
        ii = lax.broadcasted_iota(jnp.int32, (BAND, 2 * BAND), 0)
        kk = lax.broadcasted_iota(jnp.int32, (BAND, 2 * BAND), 1)
        band = jnp.logical_and(kk >= ii, kk <= ii + BAND)
        first_mask = jnp.logical_and(band, jnp.logical_or(kk >= BAND, m_step > 0))
        lane = lax.broadcasted_iota(jnp.int32, (BAND, 128), 1)
        stats = [jnp.zeros((BAND, 128), F32) for _ in range(qb)]
        for h in range(HEADS):
            sl = slice(h * HEAD_DIM, (h + 1) * HEAD_DIM)
            kc, vc = kc_ref[:, sl], vc_ref[:, sl]
            for b in range(qb):
                rows = slice(b * BAND, (b + 1) * BAND)
                if b == 0:
                    keys = jnp.concatenate([kp_ref[:, sl], kc[:BAND]], axis=0)
                    vals = jnp.concatenate([vp_ref[:, sl], vc[:BAND]], axis=0)
                else:
                    keys, vals = kc[(b - 1) * BAND:(b + 1) * BAND], vc[(b - 1) * BAND:(b + 1) * BAND]
                s = jnp.where(first_mask if b == 0 else band, _dot_nt(q_ref[rows, sl], keys) * scale, NEG_INF)
                m = jnp.max(s, axis=1, keepdims=True)
                p = jnp.exp(s - m)
                l = jnp.sum(p, axis=1, keepdims=True)
                o_ref[rows, sl] = _dot(p.astype(BF16), vals) / l
                stats[b] = jnp.where(lane == h, m, stats[b])
                stats[b] = jnp.where(lane == HEADS + h, l, stats[b])
        for b in range(qb):
            st_ref[b * BAND:(b + 1) * BAND, :] = stats[b]

    many = (None, qb * BAND, ATT_WIDTH)
    one = (None, BAND, ATT_WIDTH)

    def prev(m):
        return jnp.maximum(qb * m - 1, 0)

    return _call(
        body, (qkv, qkv, qkv, qkv, qkv), name=f"attn_fwd_g{g}", grid=(r, n_steps),
        in_specs=[
            pl.BlockSpec(many, lambda rho, m: (rho, m, 0)),
            pl.BlockSpec(one, lambda rho, m: (rho, prev(m), 1)),
            pl.BlockSpec(many, lambda rho, m: (rho, m, 1)),
            pl.BlockSpec(one, lambda rho, m: (rho, prev(m), 2)),
            pl.BlockSpec(many, lambda rho, m: (rho, m, 2)),
        ],
        out_specs=(pl.BlockSpec(many, lambda rho, m: (rho, m, 0)),
                   pl.BlockSpec((None, qb * BAND, 128), lambda rho, m: (rho, m, 0))),
        out_shape=(jax.ShapeDtypeStruct((r, L, ATT_WIDTH), F32),
                   jax.ShapeDtypeStruct((r, L, 128), F32)),
        hosted=hosted)


def _attn_combine(outs, stats, rest):
    S = rest.shape[0]
    tm = 256
    gatt_blk = 8
    back = [_perm_matrix(r, tm, transpose=True) for r in DILATIONS[1:]]
    forth = [_perm_matrix(r, tm) for r in DILATIONS[1:]]

    def body(o0_ref, o1_ref, o2_ref, s0_ref, s1_ref, s2_ref, g_ref, b1_ref, b2_ref, f1_ref, f2_ref,
             o_ref, a_ref, l0_ref, l1_ref, l2_ref):
        outs_nat = [o0_ref[0]] + [_permute_f32(b_ref[...], _rows_of_residues(o_g), 2)
                                  for o_g, b_ref in ((o1_ref, b1_ref), (o2_ref, b2_ref))]
        st = [s0_ref[0]] + [_permute_f32(b_ref[...], _rows_of_residues(s_g), 3)
                            for s_g, b_ref in ((s1_ref, b1_ref), (s2_ref, b2_ref))]
        lane = lax.broadcasted_iota(jnp.int32, (tm, 128), 1)
        lse_out = jnp.zeros((tm, 128), F32)
        for h in range(HEADS):
            sl = slice(h * HEAD_DIM, (h + 1) * HEAD_DIM)
            ms = [s[:, h:h + 1] for s in st]
            ls = [s[:, HEADS + h:HEADS + h + 1] for s in st]
            m_all = jnp.maximum(jnp.maximum(ms[0], ms[1]), ms[2])
            ws = [l * jnp.exp(m - m_all) for m, l in zip(ms, ls)]
            den = ws[0] + ws[1] + ws[2]
            o = (ws[0] * outs_nat[0][:, sl] + ws[1] * outs_nat[1][:, sl] + ws[2] * outs_nat[2][:, sl]) / den
            o_ref[:, sl] = o
            a_ref[:, sl] = (o * _silu(g_ref[:, sl])).astype(BF16)
            lse_out = jnp.where(lane == h, m_all + jnp.log(den), lse_out)
        l0_ref[0] = lse_out
        for l_ref, f_ref in ((l1_ref, f1_ref), (l2_ref, f2_ref)):
            _store_residues(l_ref, _permute_f32(f_ref[...], lse_out, 3))

    o_spec = pl.BlockSpec((tm, ATT_WIDTH), lambda i: (i, 0))
    whole = pl.BlockSpec(memory_space=pltpu.VMEM)

    def res_spec(r, width):
        return pl.BlockSpec((r, tm // r, width), lambda i: (0, i, 0))

    return pl.pallas_call(
        body, name="attn_combine", grid=(S // tm,),
        in_specs=[*[res_spec(r, ATT_WIDTH) for r in DILATIONS], *[res_spec(r, 128) for r in DILATIONS],
                  pl.BlockSpec((tm, ATT_WIDTH), lambda i: (i, gatt_blk)), whole, whole, whole, whole],
        out_specs=(o_spec, o_spec, *[res_spec(r, 128) for r in DILATIONS]),
        out_shape=(jax.ShapeDtypeStruct((S, ATT_WIDTH), F32), jax.ShapeDtypeStruct((S, ATT_WIDTH), BF16),
                   *[jax.ShapeDtypeStruct((r, S // r, 128), F32) for r in DILATIONS]),
        compiler_params=_params(("parallel",)))(*outs, *stats, rest, *back, *forth)


ATTN_BWD_PAIRS = 4


def _attn_bwd(qkv, do, lse, dvec, g, r, hosted=None):
    L = qkv.shape[1]
    nb = L // BAND
    pairs = min(ATTN_BWD_PAIRS, nb // 2)
    kb = 2 * pairs
    n_steps = nb // kb
    scale = HEAD_DIM ** -0.5

    def body(qc_ref, qn_ref, k_ref, v_ref, doc_ref, don_ref, lc_ref, ln_ref, dc_ref, dn_ref,
             out_ref, carry_ref):
        j = pl.program_id(1)

        @pl.when(j == 0)
        def _():
            carry_ref[...] = jnp.zeros_like(carry_ref)

        ii = lax.broadcasted_iota(jnp.int32, (3 * BAND, 2 * BAND), 0)
        kk = lax.broadcasted_iota(jnp.int32, (3 * BAND, 2 * BAND), 1)
        band = jnp.logical_and(kk <= ii, kk >= ii - BAND)
        last_band = jnp.logical_and(band, jnp.logical_or(ii < 2 * BAND, j < n_steps - 1))
        lse_all = jnp.concatenate([lc_ref[...], ln_ref[...]], axis=0)
        dvec_all = jnp.concatenate([dc_ref[...], dn_ref[...]], axis=0)
        for h in range(HEADS):
            sl = slice(h * HEAD_DIM, (h + 1) * HEAD_DIM)
            q_all = jnp.concatenate([qc_ref[:, sl], qn_ref[:, sl]], axis=0)
            do_all = jnp.concatenate([doc_ref[:, sl], don_ref[:, sl]], axis=0)
            dq_prev = carry_ref[:, sl]
            for a in range(pairs):
                keys = slice(2 * a * BAND, (2 * a + 2) * BAND)
                rows = slice(2 * a * BAND, (2 * a + 3) * BAND)
                k, v, q, do = k_ref[keys, sl], v_ref[keys, sl], q_all[rows], do_all[rows]
                mask = last_band if a == pairs - 1 else band
                p = jnp.where(mask, jnp.exp(_dot_nt(q, k) * scale - lse_all[rows, h:h + 1]), 0.0)
                ds = (p * (_dot_nt(do, v) - dvec_all[rows, h:h + 1])).astype(BF16)
                dq3 = _dot(ds, k) * scale
                out_ref[0, 2 * a * BAND:(2 * a + 1) * BAND, sl] = (dq_prev + dq3[:BAND]).astype(BF16)
                out_ref[0, (2 * a + 1) * BAND:(2 * a + 2) * BAND, sl] = dq3[BAND:2 * BAND].astype(BF16)
                out_ref[1, keys, sl] = (_dot_tn(ds, q) * scale).astype(BF16)
                out_ref[2, keys, sl] = _dot_tn(p.astype(BF16), do).astype(BF16)
                dq_prev = dq3[2 * BAND:]
            carry_ref[:, sl] = dq_prev

    many = (None, kb * BAND, ATT_WIDTH)
    one = (None, BAND, ATT_WIDTH)
    smany = (None, kb * BAND, 128)
    sone = (None, BAND, 128)

    def nxt(j):
        return jnp.minimum(kb * j + kb, nb - 1)

    (out,), got = _call(
        body, (qkv, qkv, qkv, qkv, do, do, lse, lse, dvec, dvec), name=f"attn_bwd_g{g}", grid=(r, n_steps),
        in_specs=[
            pl.BlockSpec(many, lambda rho, j: (rho, j, 0)),
            pl.BlockSpec(one, lambda rho, j: (rho, nxt(j), 0)),
            pl.BlockSpec(many, lambda rho, j: (rho, j, 1)),
            pl.BlockSpec(many, lambda rho, j: (rho, j, 2)),
            pl.BlockSpec(many, lambda rho, j: (rho, j, 0)),
            pl.BlockSpec(one, lambda rho, j: (rho, nxt(j), 0)),
            pl.BlockSpec(smany, lambda rho, j: (rho, j, 0)),
            pl.BlockSpec(sone, lambda rho, j: (rho, nxt(j), 0)),
            pl.BlockSpec(smany, lambda rho, j: (rho, j, 0)),
            pl.BlockSpec(sone, lambda rho, j: (rho, nxt(j), 0)),
        ],
        out_specs=(pl.BlockSpec((3, None, kb * BAND, ATT_WIDTH), lambda rho, j: (0, rho, j, 0)),),
        out_shape=(jax.ShapeDtypeStruct((3, r, L, ATT_WIDTH), BF16),),
        scratch_shapes=[pltpu.VMEM((BAND, ATT_WIDTH), F32)],
        hosted=hosted)
    return out, got


LRU_T = 256


def _linear_scan(a, b, carry, reverse):
    T, C = a.shape
    a = a.reshape(T // 8, 8, C)
    b = b.reshape(T // 8, 8, C)
    row8 = lax.broadcasted_iota(jnp.int32, a.shape, 1)
    for s in (1, 2, 4):
        keep = (row8 < 8 - s) if reverse else (row8 >= s)
        shift = 8 - s if reverse else s
        b_sh = jnp.where(keep, pltpu.roll(b, shift, 1), 0.0)
        a_sh = jnp.where(keep, pltpu.roll(a, shift, 1), 1.0)
        b = a * b_sh + b
        a = a * a_sh
    tiles = [None] * (T // 8)
    order = range(T // 8 - 1, -1, -1) if reverse else range(T // 8)
    for k in order:
        y = b[k] + a[k] * carry
        tiles[k] = y
        carry = y[0:1] if reverse else y[7:8]
    return jnp.concatenate(tiles, axis=0), carry


def _gate_matmuls(ucb, w_ref, bias):
    parts = [_dot(ucb[:, j * 128:(j + 1) * 128], w_ref[j]) for j in range(8)]
    return jnp.concatenate(parts, axis=1) + bias


def _rows_before(x, prev8, j):
    T, C = x.shape
    rot = pltpu.roll(x.reshape(T // 8, 8, C), j, 1)
    rot_prev = jnp.concatenate([pltpu.roll(prev8, j, 0)[None], rot[:-1]], axis=0)
    row8 = lax.broadcasted_iota(jnp.int32, rot.shape, 1)
    return jnp.where(row8 >= j, rot, rot_prev).reshape(T, C)


def _rows_after(x, next8, j):
    T, C = x.shape
    rot = pltpu.roll(x.reshape(T // 8, 8, C), 8 - j, 1)
    rot_next = jnp.concatenate([rot[1:], pltpu.roll(next8, 8 - j, 0)[None]], axis=0)
    row8 = lax.broadcasted_iota(jnp.int32, rot.shape, 1)
    return jnp.where(row8 < 8 - j, rot, rot_next).reshape(T, C)


def _conv_fwd(u, u_prev8, cw_ref):
    uc = cw_ref[4:5, :] + cw_ref[0:1, :] * u
    for j in range(1, CONV_WIDTH):
        uc = uc + cw_ref[j:j + 1, :] * _rows_before(u, u_prev8, j)
    return uc


def _lru_fwd(rest, cw, wr2, wi2, hosted=None):
    S = rest.shape[0]
    T = LRU_T

    def body(u_ref, cw_ref, wr_ref, wi_ref, h_ref, a_ref, uc_ref, r_ref, ig_ref, mult_ref,
             ucar_ref, hcar_ref):
        c = pl.program_id(0)

        @pl.when(c == 0)
        def _():
            ucar_ref[...] = jnp.zeros_like(ucar_ref)
            hcar_ref[...] = jnp.zeros_like(hcar_ref)

        u = u_ref[...]
        uc = _conv_fwd(u, ucar_ref[...], cw_ref)
        ucar_ref[...] = u[T - 8:, :]
        ucb = uc.astype(BF16)
        r = jax.nn.sigmoid(_gate_matmuls(ucb, wr_ref, cw_ref[5:6, :]))
        ig = jax.nn.sigmoid(_gate_matmuls(ucb, wi_ref, cw_ref[6:7, :]))
        log_a = -LRU_C * r * _softplus_neg(cw_ref[7:8, :])
        a = jnp.exp(log_a)
        mult = jnp.sqrt(_neg_expm1(2.0 * log_a))
        b = mult * (ig * uc)
        a_ref[...] = a
        uc_ref[...] = uc
        r_ref[...] = r
        ig_ref[...] = ig
        mult_ref[...] = mult
        h, last = _linear_scan(a, b, hcar_ref[0:1, :], reverse=False)
        h_ref[...] = h
        hcar_ref[...] = jnp.broadcast_to(last, (8, D_MODEL))

    row_spec = pl.BlockSpec((T, D_MODEL), lambda c: (c, 0))
    row_shape = jax.ShapeDtypeStruct((S, D_MODEL), F32)
    return _call(
        body, (rest, cw, wr2, wi2), name="lru_fwd", grid=(S // T,),
        in_specs=[row_spec, pl.BlockSpec((8, D_MODEL), lambda c: (0, 0)),
                  pl.BlockSpec((8, 128, 128), lambda c: (0, 0, 0)),
                  pl.BlockSpec((8, 128, 128), lambda c: (0, 0, 0))],
        out_specs=(row_spec,) * 6, out_shape=(row_shape,) * 6,
        scratch_shapes=[pltpu.VMEM((8, D_MODEL), F32), pltpu.VMEM((8, D_MODEL), F32)],
        hosted=hosted)


def _lru_bwd(dhl, hl, a_all, gates, rest, cw, wr2, wi2, dproj, hosted=None):
    S = rest.shape[0]
    T = LRU_T
    nc = S // T

    def body(dh_ref, h_ref, hp_ref, a_ref, uc_ref, r_ref, ig_ref, mult_ref, u_ref, up_ref, cw_ref, wr_ref,
             wi_ref, _alias, du_ref, acc_ref, gwr_ref, gwi_ref, gcar_ref, acar_ref, dcar_ref):
        step = pl.program_id(0)
        c = nc - 1 - step

        @pl.when(step == 0)
        def _():
            acc_ref[...] = jnp.zeros_like(acc_ref)
            gwr_ref[...] = jnp.zeros_like(gwr_ref)
            gwi_ref[...] = jnp.zeros_like(gwi_ref)
            gcar_ref[...] = jnp.zeros_like(gcar_ref)
            acar_ref[...] = jnp.zeros_like(acar_ref)
            dcar_ref[...] = jnp.zeros_like(dcar_ref)

        u = u_ref[...]
        u_prev = jnp.where(c > 0, up_ref[...], 0.0)
        h_prev8 = jnp.where(c > 0, hp_ref[...], 0.0)
        a = a_ref[...]
        h = h_ref[...]
        uc, r, ig, mult = uc_ref[...], r_ref[...], ig_ref[...], mult_ref[...]
        ucb = uc.astype(BF16)
        sp = _softplus_neg(cw_ref[7:8, :])
        a_next = _rows_after(a, acar_ref[...], 1)
        G, first = _linear_scan(a_next, dh_ref[...], gcar_ref[0:1, :], reverse=True)
        gcar_ref[...] = jnp.broadcast_to(first, (8, D_MODEL))
        acar_ref[...] = jnp.broadcast_to(a[0:1, :], (8, D_MODEL))
        d_a = G * _rows_before(h, h_prev8, 1)
        d_mult = G * (ig * uc)
        d_ig = G * (mult * uc)
        duc = G * (mult * ig)
        d_log_a = d_a * a - d_mult * (a * a) / mult
        d_r = d_log_a * (-LRU_C * sp)
        d_sp = jnp.sum(d_log_a * (-LRU_C * r), axis=0, keepdims=True)
        dpre_r = d_r * r * (1.0 - r)
        dpre_i = d_ig * ig * (1.0 - ig)
        dprb = dpre_r.astype(BF16)
        dpib = dpre_i.astype(BF16)
        back = []
        for j in range(8):
            sl = slice(j * 128, (j + 1) * 128)
            back.append(_dot_nt(dprb[:, sl], wr_ref[j]) + _dot_nt(dpib[:, sl], wi_ref[j]))
            gwr_ref[j] += _dot_tn(ucb[:, sl], dprb[:, sl])
            gwi_ref[j] += _dot_tn(ucb[:, sl], dpib[:, sl])
        duc = duc + jnp.concatenate(back, axis=1)
        du = cw_ref[0:1, :] * duc
        duc_next8 = dcar_ref[...]
        acc_ref[0:1, :] += jnp.sum(duc * u, axis=0, keepdims=True)
        for j in range(1, CONV_WIDTH):
            du = du + cw_ref[j:j + 1, :] * _rows_after(duc, duc_next8, j)
            acc_ref[j:j + 1, :] += jnp.sum(duc * _rows_before(u, u_prev, j), axis=0, keepdims=True)
        dcar_ref[...] = duc[0:8, :]
        acc_ref[4:5, :] += jnp.sum(duc, axis=0, keepdims=True)
        acc_ref[5:6, :] += jnp.sum(dpre_r, axis=0, keepdims=True)
        acc_ref[6:7, :] += jnp.sum(dpre_i, axis=0, keepdims=True)
        acc_ref[7:8, :] += d_sp
        du_ref[0] = du[:, 0:ATT_WIDTH].astype(BF16)
        du_ref[1] = du[:, ATT_WIDTH:].astype(BF16)

    def rev(step):
        return nc - 1 - step

    def prev8(step):
        return jnp.maximum(rev(step) * (T // 8) - 1, 0)

    row_spec = pl.BlockSpec((T, D_MODEL), lambda s: (rev(s), 0))
    prev_spec = pl.BlockSpec((8, D_MODEL), lambda s: (prev8(s), 0))
    vec_spec = pl.BlockSpec((8, D_MODEL), lambda s: (0, 0))
    w_spec = pl.BlockSpec((8, 128, 128), lambda s: (0, 0, 0))
    return _call(
        body, (dhl, hl, hl, a_all, *gates, rest, rest, cw, wr2, wi2, dproj), name="lru_bwd", grid=(nc,),
        in_specs=[row_spec, row_spec, prev_spec, row_spec, row_spec, row_spec, row_spec, row_spec,
                  row_spec, prev_spec, vec_spec, w_spec, w_spec, ANY_SPEC],
        out_specs=(pl.BlockSpec((2, T, ATT_WIDTH), lambda s: (4, rev(s), 0)), vec_spec, w_spec, w_spec),
        out_shape=(jax.ShapeDtypeStruct(dproj.shape, BF16), jax.ShapeDtypeStruct((8, D_MODEL), F32),
                   jax.ShapeDtypeStruct((8, 128, 128), F32), jax.ShapeDtypeStruct((8, 128, 128), F32)),
        scratch_shapes=[pltpu.VMEM((8, D_MODEL), F32), pltpu.VMEM((8, D_MODEL), F32),
                        pltpu.VMEM((8, D_MODEL), F32)],
        aliases={13: 0}, hosted=hosted)


OUT_TM = 256


def _out_fwd(a_gated, hl, rest, x, vec, w_pa, w_pb, w_o, target=None, hosted=None):
    S = x.shape[0]
    tm = OUT_TM
    with_loss = target is not None

    def body(*refs):
        (ag_ref, hl_ref, gl_ref, ma_ref, mb_ref, x_ref, v_ref, wpa_ref, wpb_ref, wo_ref), refs = refs[:10], refs[10:]
        if with_loss:
            t_ref, refs = refs[0], refs[1:]
        xn_ref, mix_ref, ya_ref, yb_ref, bg_ref = refs[:5]
        bg = (hl_ref[...] * _silu(gl_ref[...])).astype(BF16)
        ya = _dot(ag_ref[...], wpa_ref[...])
        yb = _dot(bg, wpb_ref[...])
        mix = (jax.nn.sigmoid(ma_ref[...]) * ya + jax.nn.sigmoid(mb_ref[...]) * yb).astype(BF16)
        out = _dot(mix, wo_ref[...])
        rstd = lax.rsqrt(jnp.mean(out * out, axis=-1, keepdims=True) + NORM_EPS)
        x_new = x_ref[...] + v_ref[1:2, :] * ((out * rstd) * v_ref[0:1, :])
        if with_loss:
            acc_ref = refs[5]

            @pl.when(pl.program_id(0) == 0)
            def _():
                acc_ref[...] = jnp.zeros_like(acc_ref)

            err = x_new - t_ref[...]
            xn_ref[...] = err * (1.0 / D_MODEL)
            part = jnp.sum(jnp.sum(err * err, axis=1, keepdims=True), axis=0, keepdims=True)
            acc_ref[...] += jnp.broadcast_to(part, acc_ref.shape)
        else:
            xn_ref[...] = x_new
        mix_ref[...] = mix
        ya_ref[...] = ya.astype(BF16)
        yb_ref[...] = yb.astype(BF16)
        bg_ref[...] = bg

    def col(j):
        return pl.BlockSpec((tm, D_MODEL), lambda i: (i, j))

    def whole(shape):
        return pl.BlockSpec(shape, lambda i: (0, 0))

    row = col(0)
    bf = jax.ShapeDtypeStruct((S, D_MODEL), BF16)
    loss_in = ([target], [row]) if with_loss else ([], [])
    loss_out = ([jax.ShapeDtypeStruct((8, 128), F32)], [whole((8, 128))]) if with_loss else ([], [])
    return _call(
        body, (a_gated, hl, rest, rest, rest, x, vec, w_pa, w_pb, w_o, *loss_in[0]), name="out_fwd",
        grid=(S // tm,),
        in_specs=[pl.BlockSpec((tm, ATT_WIDTH), lambda i: (i, 0)), row, col(1), col(2), col(3), row,
                  whole((8, D_MODEL)), whole((ATT_WIDTH, D_MODEL)), whole((D_MODEL, D_MODEL)),
                  whole((D_MODEL, D_MODEL)), *loss_in[1]],
        out_specs=(row, row, row, row, row, *loss_out[1]),
        out_shape=(jax.ShapeDtypeStruct((S, D_MODEL), F32), bf, bf, bf, bf, *loss_out[0]),
        hosted=hosted)


def _out_bwd(dxn, mix, ya, yb, hl, rest, o_att, vec, w_pa, w_pb, w_o, hosted=None):
    S = dxn.shape[0]
    tm = OUT_TM
    forth = [_perm_matrix(r, tm) for r in DILATIONS[1:]]

    def body(dxn_ref, mix_ref, ya_ref, yb_ref, hl_ref, gl_ref, ma_ref, mb_ref, ga_ref, o_ref, v_ref,
             wpa_ref, wpb_ref, wo_ref, f1_ref, f2_ref,
             dout_ref, dya_ref, dyb_ref, dhl_ref, dp_ref, acc_ref,
             do0_ref, do1_ref, do2_ref, dv0_ref, dv1_ref, dv2_ref):
        i = pl.program_id(0)

        @pl.when(i == 0)
        def _():
            acc_ref[...] = jnp.zeros_like(acc_ref)

        g_post = v_ref[0:1, :]
        gate = v_ref[1:2, :]
        dxn_v = dxn_ref[...]
        out = _dot(mix_ref[...], wo_ref[...])
        rstd = lax.rsqrt(jnp.mean(out * out, axis=-1, keepdims=True) + NORM_EPS)
        nrm = out * rstd
        acc_ref[0:1, :] += jnp.sum(dxn_v * (nrm * g_post), axis=0, keepdims=True)
        acc_ref[1:2, :] += jnp.sum(dxn_v * gate * nrm, axis=0, keepdims=True)
        dn = dxn_v * (gate * g_post)
        dout = (rstd * (dn - nrm * jnp.mean(dn * nrm, axis=-1, keepdims=True))).astype(BF16)
        dout_ref[...] = dout
        dmix = _dot_nt(dout, wo_ref[...])
        sa = jax.nn.sigmoid(ma_ref[...])
        sb = jax.nn.sigmoid(mb_ref[...])
        dya = (dmix * sa).astype(BF16)
        dyb = (dmix * sb).astype(BF16)
        dya_ref[...] = dya
        dyb_ref[...] = dyb
        dma = dmix * ya_ref[...].astype(F32) * (sa * (1.0 - sa))
        dmb = dmix * yb_ref[...].astype(F32) * (sb * (1.0 - sb))
        d_ag = _dot_nt(dya, wpa_ref[...])
        d_bg = _dot_nt(dyb, wpb_ref[...])
        gl = gl_ref[...]
        hl_v = hl_ref[...]
        dhl_ref[...] = d_bg * _silu(gl)
        dgl = d_bg * hl_v * _dsilu(gl)
        ga = ga_ref[...]
        o = o_ref[...]
        do = d_ag * _silu(ga)
        dga = d_ag * o * _dsilu(ga)
        lane = lax.broadcasted_iota(jnp.int32, (tm, 128), 1)
        dvec = jnp.zeros((tm, 128), F32)
        for h in range(HEADS):
            sl = slice(h * HEAD_DIM, (h + 1) * HEAD_DIM)
            dvec = jnp.where(lane == h, jnp.sum(do[:, sl] * o[:, sl], axis=1, keepdims=True), dvec)
        do_b = do.astype(BF16)
        do0_ref[0] = do_b
        dv0_ref[0] = dvec
        for do_ref, dv_ref, f_ref in ((do1_ref, dv1_ref, f1_ref), (do2_ref, dv2_ref, f2_ref)):
            _store_residues(do_ref, _dot(f_ref[...], do_b).astype(BF16))
            _store_residues(dv_ref, _permute_f32(f_ref[...], dvec, 3))
        dp_ref[0] = dgl[:, 0:ATT_WIDTH].astype(BF16)
        dp_ref[1] = dgl[:, ATT_WIDTH:].astype(BF16)
        dp_ref[2] = dma[:, 0:ATT_WIDTH].astype(BF16)
        dp_ref[3] = dma[:, ATT_WIDTH:].astype(BF16)
        dp_ref[4] = dmb[:, 0:ATT_WIDTH].astype(BF16)
        dp_ref[5] = dmb[:, ATT_WIDTH:].astype(BF16)
        dp_ref[6] = dga.astype(BF16)

    def col(j):
        return pl.BlockSpec((tm, D_MODEL), lambda i: (i, j))

    def whole(shape):
        return pl.BlockSpec(shape, lambda i: (0, 0))

    def res_spec(r, width):
        return pl.BlockSpec((r, tm // r, width), lambda i: (0, i, 0))

    row = col(0)
    att = pl.BlockSpec((tm, ATT_WIDTH), lambda i: (i, 0))
    bf = jax.ShapeDtypeStruct((S, D_MODEL), BF16)
    vmem = pl.BlockSpec(memory_space=pltpu.VMEM)
    return _call(
        body, (dxn, mix, ya, yb, hl, rest, rest, rest, rest, o_att, vec, w_pa, w_pb, w_o, *forth),
        name="out_bwd", grid=(S // tm,),
        in_specs=[row, row, row, row, row, col(1), col(2), col(3),
                  pl.BlockSpec((tm, ATT_WIDTH), lambda i: (i, 8)), att,
                  whole((8, D_MODEL)), whole((ATT_WIDTH, D_MODEL)), whole((D_MODEL, D_MODEL)),
                  whole((D_MODEL, D_MODEL)), vmem, vmem],
        out_specs=(row, row, row, row,
                   pl.BlockSpec((7, tm, ATT_WIDTH), lambda i: (0, i, 0)), whole((8, D_MODEL)),
                   *[res_spec(r, ATT_WIDTH) for r in DILATIONS], *[res_spec(r, 128) for r in DILATIONS]),
        out_shape=(bf, bf, bf, jax.ShapeDtypeStruct((S, D_MODEL), F32),
                   jax.ShapeDtypeStruct((DREST_CHUNKS, S, ATT_WIDTH), BF16),
                   jax.ShapeDtypeStruct((8, D_MODEL), F32),
                   *[jax.ShapeDtypeStruct((r, S // r, ATT_WIDTH), BF16) for r in DILATIONS],
                   *[jax.ShapeDtypeStruct((r, S // r, 128), F32) for r in DILATIONS]),
        hosted=hosted)


def _adamw(gsrcs, w, m, v, n_stack, name, rows=None):
    n_layers, R, C = w.shape
    assert len(gsrcs) == n_layers
    budget = 96 * 1024
    tr = rows
    if tr is None:
        tr = R
        while tr * C > budget and tr % 16 == 0:
            tr //= 2
    assert R % tr == 0 and (tr % 8 == 0 or tr == R)
    c1 = 1.0 - ADAM_B1 ** ADAM_STEP
    c2 = 1.0 - ADAM_B2 ** ADAM_STEP

    def body(*refs):
        g_refs = refs[:n_layers]
        w_ref, m_ref, v_ref, go_ref, d_ref, mo_ref, vo_ref = refs[n_layers:]

        def update(g_ref):
            if n_stack:
                g = g_ref[0].astype(F32)
                for s in range(1, n_stack):
                    g = g + g_ref[s].astype(F32)
            else:
                g = g_ref[...]
            m_new = ADAM_B1 * m_ref[...] + (1.0 - ADAM_B1) * g
            v_new = ADAM_B2 * v_ref[...] + (1.0 - ADAM_B2) * (g * g)
            m_hat = m_new / c1
            v_hat = v_new / c2
            go_ref[...] = g
            d_ref[...] = -ADAM_LR * (m_hat / (jnp.sqrt(v_hat) + ADAM_EPS) + ADAM_WD * w_ref[...])
            mo_ref[...] = m_new
            vo_ref[...] = v_new

        for layer in range(n_layers):
            pl.when(pl.program_id(0) == layer)(functools.partial(update, g_refs[layer]))

    def g_spec(layer):
        def rows_of(l, i):
            return jnp.where(l == layer, i, 0)
        if n_stack:
            return pl.BlockSpec((n_stack, tr, C), lambda l, i: (0, rows_of(l, i), 0))
        return pl.BlockSpec((tr, C), lambda l, i: (rows_of(l, i), 0))

    blk = pl.BlockSpec((None, tr, C), lambda l, i: (l, i, 0))
    shp = jax.ShapeDtypeStruct((n_layers, R, C), F32)
    return pl.pallas_call(
        body, name=name, grid=(n_layers, R // tr),
        in_specs=[g_spec(layer) for layer in range(n_layers)] + [blk, blk, blk],
        out_specs=(blk, blk, blk, blk), out_shape=(shp, shp, shp, shp),
        compiler_params=_params(("arbitrary", "arbitrary")))(*gsrcs, w, m, v)


def _pair_blocks(w):
    w = w.reshape(8, 2, 64, 64)
    z = jnp.zeros((8, 64, 64), w.dtype)
    top = jnp.concatenate([w[:, 0], z], axis=2)
    bot = jnp.concatenate([z, w[:, 1]], axis=2)
    return jnp.concatenate([top, bot], axis=1).astype(BF16)


def _unpair_blocks(g):
    return jnp.stack([g[:, :64, :64], g[:, 64:, 64:]], axis=1).reshape(16, 64, 64)


def _mats_full(w_pa_g, w_pb_g, w_o_g):
    return dict(w_pa=jnp.transpose(w_pa_g, (1, 0, 2)).reshape(ATT_WIDTH, D_MODEL),
                w_pb=w_pb_g.reshape(D_MODEL, D_MODEL), w_o=w_o_g.reshape(D_MODEL, D_MODEL))


def _layer_params(layer, w_in_g, mats_g, conv_w_full, conv_b, b_rg, b_ig, lru_lambda, w_rg, w_ig, g_pre,
                  g_post):
    cw = jnp.concatenate([conv_w_full, conv_b[None], b_rg[None], b_ig[None], lru_lambda[None]], axis=0)
    p = dict(w_int=_weights_full(w_in_g), cw=cw, wr2=_pair_blocks(w_rg), wi2=_pair_blocks(w_ig),
             g_pre=g_pre, g_post=g_post, lru_lambda=lru_lambda)
    if mats_g is not None:
        p.update(_mats_full(*mats_g))
    return p


def _layer_fwd(x, mod, p, host_proj=None, host_lru=None, host_out=None, own_mats=None, target=None):
    zeros = jnp.zeros((5, D_MODEL), F32)
    vec_pre = jnp.concatenate([p["g_pre"][None], mod[1:2], mod[0:1], zeros], axis=0)
    (ht0, ht1, ht2, q0, q1, q2, rest), got_proj = _norm_proj_fwd(x, vec_pre, p["w_int"], hosted=host_proj)
    h_t = [ht0, ht1, ht2]
    qkv, outs, stats = [q0, q1, q2], [], []
    for g, r in enumerate(DILATIONS):
        (o_g, st_g), _ = _attn_fwd(qkv[g], g, r)
        outs.append(o_g)
        stats.append(st_g)
    o_att, a_gated, *lse = _attn_combine(outs, stats, rest)
    if own_mats is not None:
        host_lru = (list(own_mats), ["ag"] * 3)
    (hl, a_dec, *gates), got_lru = _lru_fwd(rest, p["cw"], p["wr2"], p["wi2"], hosted=host_lru)
    if own_mats is not None:
        p.update(_mats_full(*got_lru))
    vec_post = jnp.concatenate([p["g_post"][None], mod[2:3], jnp.zeros((6, D_MODEL), F32)], axis=0)
    (x_new, mix, ya, yb, b_gated, *loss_acc), got_out = _out_fwd(
        a_gated, hl, rest, x, vec_post, p["w_pa"], p["w_pb"], p["w_o"], target=target, hosted=host_out)
    saved = dict(x=x, h_t=h_t, qkv=qkv, rest=rest, o_att=o_att, a_gated=a_gated, lse=lse, hl=hl, a_dec=a_dec,
                 gates=gates, mix=mix, ya=ya, yb=yb, b_gated=b_gated, vec_pre=vec_pre, vec_post=vec_post)
    return (x_new, *loss_acc), saved, got_proj, got_lru, got_out


def _mats_sources(g_pa, g_pb, g_o):
    return [jnp.transpose(g_pa.astype(BF16).reshape(ATT_WIDTH, N_DEV, 128), (1, 0, 2)),
            g_pb.astype(BF16).reshape(N_DEV, 128, D_MODEL), g_o.astype(BF16).reshape(N_DEV, 128, D_MODEL)]


LRU_SMALL = ("conv_b", "w_rg", "b_rg", "w_ig", "b_ig", "lru_lambda")
MOD_SMALL = ("b_mod", "g_pre", "g_post")


def _flat_rows(arrs):
    return jnp.concatenate([a.reshape(-1) for a in arrs]).reshape(-1, 128)


def _layer_bwd(dxn, sv, p, upper_small=None):
    a2a3 = ["a2a"] * 3
    (dout, dya, dyb, dhl, drest, acc_out, *do_dvec), _ = _out_bwd(
        dxn, sv["mix"], sv["ya"], sv["yb"], sv["hl"], sv["rest"], sv["o_att"], sv["vec_post"],
        p["w_pa"], p["w_pb"], p["w_o"])
    mats = _mats_sources(_wgrad_tn(sv["a_gated"], dya, "wgrad_pa"), _wgrad_tn(sv["b_gated"], dyb, "wgrad_pb"),
                         _wgrad_tn(sv["mix"], dout, "wgrad_o"))
    (drest, acc_lru, gwr2, gwi2), _ = _lru_bwd(
        dhl, sv["hl"], sv["a_dec"], sv["gates"], sv["rest"], p["cw"], p["wr2"], p["wi2"], drest)
    small = dict(
        conv_w=acc_lru[0:4], conv_b=acc_lru[4], b_rg=acc_lru[5], b_ig=acc_lru[6],
        lru_lambda=acc_lru[7] * (-jax.nn.sigmoid(-p["lru_lambda"])),
        w_rg=_unpair_blocks(gwr2), w_ig=_unpair_blocks(gwi2))
    hosts = [(mats, a2a3), None, None]
    if upper_small is not None:
        for half, names in enumerate((LRU_SMALL[:3], LRU_SMALL[3:] + ("conv_w",))):
            vec = _flat_rows([jnp.stack([small[n], upper_small[n]]) for n in names])
            hosts[1 + half] = ([vec], ["ag"])
    dqkv, got_attn = [], []
    for g, r in enumerate(DILATIONS):
        d_g, got = _attn_bwd(sv["qkv"][g], do_dvec[g], sv["lse"][g], do_dvec[3 + g], g, r, hosted=hosts[g])
        dqkv.append(d_g)
        got_attn.append(got)
    shards = None
    for g, r in enumerate(DILATIONS):
        shards = _proj_wgrad_group(sv["h_t"][g], dqkv[g], g, r, shards, f"proj_wgrad_g{g}")
    shards = _proj_wgrad_part(sv["h_t"][0], drest, 2, 4, QKV_CHUNKS, shards, "proj_wgrad_u")
    shards = _proj_wgrad_part(sv["h_t"][0], drest, 7, 0, QKV_CHUNKS + 2, shards, "proj_wgrad_rest")
    (dx, acc_pre), got_in = _proj_dgrad_norm_bwd(
        dqkv, drest, p["w_int"], sv["x"], dxn, sv["vec_pre"], hosted=([shards], ["a2a"]))
    small.update(dmod=jnp.concatenate([acc_pre[0], acc_pre[1], acc_out[0]]), g_pre=acc_pre[2],
                 g_post=acc_out[1])
    return dx, small, dict(mats=got_attn[0], w_in=got_in[0], lru_small=got_attn[1] + got_attn[2])


def kernel(x, c, w_mod, b_mod, g_pre, w_in, conv_w, conv_b, w_rg, b_rg, w_ig, b_ig, lru_lambda, w_pa, w_pb, w_o, g_post, loss_target, m_w_mod, m_b_mod, m_g_pre, m_w_in, m_conv_w, m_conv_b, m_w_rg, m_b_rg, m_w_ig, m_b_ig, m_lru_lambda, m_w_pa, m_w_pb, m_w_o, m_g_post, v_w_mod, v_b_mod, v_g_pre, v_w_in, v_conv_w, v_conv_b, v_w_rg, v_b_rg, v_w_ig, v_b_ig, v_lru_lambda, v_w_pa, v_w_pb, v_w_o, v_g_post):
    W = dict(w_mod=w_mod, b_mod=b_mod, g_pre=g_pre, w_in=w_in, conv_w=conv_w, conv_b=conv_b, w_rg=w_rg,
             b_rg=b_rg, w_ig=w_ig, b_ig=b_ig, lru_lambda=lru_lambda, w_pa=w_pa, w_pb=w_pb, w_o=w_o,
             g_post=g_post)
    M = dict(w_mod=m_w_mod, b_mod=m_b_mod, g_pre=m_g_pre, w_in=m_w_in, conv_w=m_conv_w, conv_b=m_conv_b,
             w_rg=m_w_rg, b_rg=m_b_rg, w_ig=m_w_ig, b_ig=m_b_ig, lru_lambda=m_lru_lambda, w_pa=m_w_pa,
             w_pb=m_w_pb, w_o=m_w_o, g_post=m_g_post)
    V = dict(w_mod=v_w_mod, b_mod=v_b_mod, g_pre=v_g_pre, w_in=v_w_in, conv_w=v_conv_w, conv_b=v_conv_b,
             w_rg=v_w_rg, b_rg=v_b_rg, w_ig=v_w_ig, b_ig=v_b_ig, lru_lambda=v_lru_lambda, w_pa=v_w_pa,
             w_pb=v_w_pb, w_o=v_w_o, g_post=v_g_post)
    S = x.shape[1]
    me = 4 * lax.axis_index("x") + 2 * lax.axis_index("y") + lax.axis_index("c")
    n_mod = w_mod.shape[2]
    n_in = w_in.shape[2]
    n_conv = conv_w.shape[2]

    c_rows = jnp.broadcast_to(c, (8, D_MODEL))
    w_in_b, w_pa_b, w_pb_b, w_o_b = (t.astype(BF16) for t in (w_in, w_pa, w_pb, w_o))
    g_c, g_win0, g_cw = _gather_two_level([c_rows, w_in_b[0], conv_w], "gather_weights")
    c_all = g_c[:, 0, :]
    c_pad = jnp.concatenate([c_all, jnp.zeros((8, D_MODEL), F32)], axis=0)
    conv_w_full = jnp.transpose(g_cw, (1, 2, 0, 3)).reshape(2, CONV_WIDTH, D_MODEL)

    mod_cols = _mod_fwd(c_pad, w_mod)
    mod_src = jnp.transpose(mod_cols[:, :8], (1, 0, 2))
    mod_src = jnp.concatenate([mod_src, jnp.zeros((8, 6, n_mod), F32)], axis=1)
    (mod_got,) = _exchange([mod_src], ["a2a"], "scatter_mod")
    mod = jnp.transpose(mod_got[:, :2], (1, 0, 2)).reshape(2, 3 * D_MODEL) + b_mod

    def layer_params(layer, w_in_g, mats_g):
        return _layer_params(layer, w_in_g, mats_g, conv_w_full[layer], conv_b[layer], b_rg[layer],
                             b_ig[layer], lru_lambda[layer], w_rg[layer], w_ig[layer], g_pre[layer],
                             g_post[layer])

    layers = [layer_params(0, g_win0, None), None]
    (act,), sv0, (g_win1,), _, mats1_g = _layer_fwd(
        x[0], mod[0].reshape(3, D_MODEL), layers[0], own_mats=(w_pa_b[0], w_pb_b[0], w_o_b[0]),
        host_proj=([w_in_b[1]], ["ag"]), host_out=([w_pa_b[1], w_pb_b[1], w_o_b[1]], ["ag"] * 3))
    layers[1] = layer_params(1, g_win1, mats1_g)
    (dy, loss_acc), sv1, _, _, _ = _layer_fwd(act, mod[1].reshape(3, D_MODEL), layers[1],
                                              target=loss_target[0])

    dy, small1, got1 = _layer_bwd(dy, sv1, layers[1])
    dy, small0, got0 = _layer_bwd(dy, sv0, layers[0], upper_small=small1)
    grad_x = dy[None]
    r_pa, r_pb, r_o = ([got0["mats"][k], got1["mats"][k]] for k in range(3))
    r_in = [got0["w_in"], got1["w_in"]]
    grads = [small0, small1]

    def stack2(name):
        return jnp.stack([grads[0][name], grads[1][name]], axis=0)

    dmod = stack2("dmod")
    dmod_src = jnp.transpose(dmod.reshape(2, N_DEV, n_mod), (1, 0, 2))
    dmod_src = jnp.concatenate([dmod_src, jnp.zeros((8, 6, n_mod), F32)], axis=1)
    mod_vec = _flat_rows([dmod, stack2("g_pre"), stack2("g_post"), loss_acc])
    r_dmod, r_mod_small = _exchange([dmod_src, mod_vec], ["a2a", "ag"], "exchange_grads")
    mod_rows = mod_vec.shape[0] - loss_acc.shape[0]
    loss = 0.5 * jnp.sum(r_mod_small[:, mod_rows, 0]) / D_MODEL
    r_lru_small = got0["lru_small"]

    res = {}
    res["w_in"] = _adamw(r_in, w_in, m_w_in, v_w_in, 8, "adamw_w_in")
    res["w_pa"] = _adamw(r_pa, w_pa, m_w_pa, v_w_pa, 8, "adamw_w_pa")
    res["w_pb"] = _adamw(r_pb, w_pb, m_w_pb, v_w_pb, 8, "adamw_w_pb")
    res["w_o"] = _adamw(r_o, w_o, m_w_o, v_w_o, 8, "adamw_w_o")
    dmod_all = jnp.transpose(r_dmod[:, :2], (1, 0, 2))
    gw_mod = _mod_wgrad(jnp.transpose(c_all), dmod_all)
    res["w_mod"] = _adamw([gw_mod[0], gw_mod[1]], w_mod, m_w_mod, v_w_mod, 0, "adamw_w_mod")
    for names_rep, stack, tag in ((LRU_SMALL[:3], r_lru_small[0], "lru_a"), (LRU_SMALL[3:], r_lru_small[1], "lru_b"),
                                  (MOD_SMALL, r_mod_small, "mod")):
        rows = sum(W[n].size for n in names_rep) // 128
        rep = _adamw([stack], *(_flat_rows([src[n] for n in names_rep])[None] for src in (W, M, V)), 8,
                     f"adamw_small_{tag}", rows=rows // 2 if rows % 16 == 0 else rows)
        off = 0
        for name in names_rep:
            size = W[name].size
            res[name] = tuple(t.reshape(-1)[off:off + size].reshape(W[name].shape) for t in rep)
            off += size
    lru_rows = sum(W[n].size for n in LRU_SMALL[3:]) // 128
    conv_stack = r_lru_small[1][:, lru_rows:].reshape(8, 2, CONV_WIDTH, D_MODEL)
    conv_stack = lax.dynamic_slice_in_dim(conv_stack, me * n_conv, n_conv, axis=3).reshape(8, 8, n_conv)
    res["conv_w"] = _adamw([conv_stack], conv_w.reshape(1, 8, n_conv), m_conv_w.reshape(1, 8, n_conv),
                           v_conv_w.reshape(1, 8, n_conv), 8, "adamw_conv_w")

    names = ("w_mod", "b_mod", "g_pre", "w_in", "conv_w", "conv_b", "w_rg", "b_rg", "w_ig", "b_ig",
             "lru_lambda", "w_pa", "w_pb", "w_o", "g_post")
    outs = [loss, grad_x]
    for k in range(4):
        outs.extend(res[n][k].reshape(W[n].shape) for n in names)
    return tuple(outs)
```

```python
import functools

import jax
import jax.numpy as jnp
import numpy as np
from jax import lax
from jax.experimental import pallas as pl
from jax.experimental.pallas import tpu as pltpu

F32 = jnp.float32
BF16 = jnp.bfloat16

N_DEV = 8
D_MODEL = 1024
HEAD_DIM = 128
HEADS = 4
ATT_WIDTH = HEADS * HEAD_DIM
DILATIONS = (1, 4, 16)
BAND = 128
N_CHUNKS = 18
QKV_CHUNKS = 9
CONV_WIDTH = 4
LRU_C = 8.0
NORM_EPS = 1e-6
NEG_INF = -1e30
ADAM_LR = 0.001
ADAM_B1 = 0.9
ADAM_B2 = 0.999
ADAM_EPS = 1e-08
ADAM_WD = 0.01
ADAM_STEP = 10
CHUNK_PERM = (0, 3, 6, 1, 4, 7, 2, 5, 8, 10, 11, 12, 13, 14, 15, 16, 17, 9)
DREST_CHUNKS = 10
VMEM_LIMIT = 56 * 1024 * 1024


def _params(sem=None):
    return pltpu.CompilerParams(dimension_semantics=sem, vmem_limit_bytes=VMEM_LIMIT)


def _silu(x):
    return x * jax.nn.sigmoid(x)


def _dsilu(x):
    s = jax.nn.sigmoid(x)
    return s * (1.0 + x * (1.0 - s))


def _neg_expm1(x):
    series = -x * (1.0 + x * (0.5 + x * (1.0 / 6.0 + x * (1.0 / 24.0))))
    return jnp.where(x > -0.05, series, 1.0 - jnp.exp(x))


def _softplus_neg(lam):
    z = jnp.exp(-jnp.abs(lam))
    small = z * (1.0 - z * (0.5 - z * (1.0 / 3.0 - z * 0.25)))
    log1p_z = jnp.where(z < 1e-2, small, jnp.log(1.0 + z))
    return jnp.maximum(-lam, 0.0) + log1p_z


def _dot(a, b):
    return jnp.dot(a, b, preferred_element_type=F32)


def _dot_nt(a, b):
    return lax.dot_general(a, b, (((1,), (1,)), ((), ())), preferred_element_type=F32)


def _dot_tn(a, b):
    return lax.dot_general(a, b, (((0,), (0,)), ((), ())), preferred_element_type=F32)


def _perm_matrix(r, rows, transpose=False):
    n = rows // r
    p = np.zeros((rows, rows), np.float32)
    dst = np.arange(rows)
    p[dst, (dst % n) * r + dst // n] = 1.0
    return jnp.asarray(p.T if transpose else p, dtype=BF16)


def _permute_f32(p, x, pieces):
    part = x.astype(BF16)
    acc = _dot(p, part)
    for _ in range(pieces - 1):
        x = x - part.astype(F32)
        part = x.astype(BF16)
        acc = acc + _dot(p, part)
    return acc


def _rows_of_residues(ref, lead=()):
    r = ref.shape[len(lead)]
    if r == 1:
        return ref[lead + (0,)]
    return jnp.concatenate([ref[lead + (rho,)] for rho in range(r)], axis=0)


def _store_residues(ref, x, lead=()):
    r = ref.shape[len(lead)]
    n = x.shape[0] // r
    for rho in range(r):
        ref[lead + (rho,)] = x[rho * n:(rho + 1) * n]


ANY_SPEC = pl.BlockSpec(memory_space=pl.ANY)


def _exchange_shapes(arrs, modes):
    return [jax.ShapeDtypeStruct(((N_DEV,) + a.shape) if mode == "ag" else a.shape, a.dtype)
            for a, mode in zip(arrs, modes)]


def _exchange_sems(n):
    return [pltpu.SemaphoreType.DMA((7 * n,)), pltpu.SemaphoreType.DMA((7 * n,)),
            pltpu.SemaphoreType.DMA((n,))]


def _exchange_copies(ins, outs, modes, sems):
    n = len(ins)
    send_sems, recv_sems, local_sems = sems
    x, y, c = lax.axis_index("x"), lax.axis_index("y"), lax.axis_index("c")
    me = 4 * x + 2 * y + c

    def src_for(a, dev):
        return ins[a] if modes[a] == "ag" else ins[a].at[dev]

    local = [pltpu.make_async_copy(src_for(a, me), outs[a].at[me], local_sems.at[a]) for a in range(n)]
    sends, arrivals = [], []
    for k in range(1, N_DEV):
        px = 1 - x if (k >> 2) & 1 else x
        py = 1 - y if (k >> 1) & 1 else y
        pc = 1 - c if k & 1 else c
        peer = 4 * px + 2 * py + pc
        for a in range(n):
            s = (k - 1) * n + a
            for dst, group in ((me, sends), (peer, arrivals)):
                group.append(pltpu.make_async_remote_copy(
                    src_ref=src_for(a, peer), dst_ref=outs[a].at[dst],
                    send_sem=send_sems.at[s], recv_sem=recv_sems.at[s],
                    device_id=(px, py, pc), device_id_type=pl.DeviceIdType.MESH))
    return local, sends, arrivals


def _exchange_start(copies):
    local, sends, _ = copies
    for cp in local + sends:
        cp.start()


def _exchange_wait(copies):
    local, sends, arrivals = copies
    for cp in arrivals:
        cp.wait_recv()
    for cp in sends:
        cp.wait_send()
    for cp in local:
        cp.wait()


def _exchange(arrs, modes, name):
    n = len(arrs)

    def body(*refs):
        copies = _exchange_copies(refs[:n], refs[n:2 * n], modes, refs[2 * n:])
        _exchange_start(copies)
        _exchange_wait(copies)

    outs = pl.pallas_call(
        body, name=name, out_shape=tuple(_exchange_shapes(arrs, modes)),
        in_specs=[ANY_SPEC] * n, out_specs=tuple([ANY_SPEC] * n),
        scratch_shapes=_exchange_sems(n),
    )(*arrs)
    return list(outs)


def _gather_two_level(arrs, name):
    n = len(arrs)

    def body(*refs):
        ins, outs = refs[:n], refs[n:2 * n]
        send_sems, recv_sems, local_sems = refs[2 * n:]
        x, y, c = lax.axis_index("x"), lax.axis_index("y"), lax.axis_index("c")
        me, sibling = (x, y, c), (x, y, 1 - c)
        chips = [(1 - x, y), (x, 1 - y), (1 - x, 1 - y)]

        def copy(a, k, block, to, src=None):
            slot = 4 * block[0] + 2 * block[1] + block[2]
            return pltpu.make_async_remote_copy(
                src_ref=outs[a].at[slot] if src is None else src, dst_ref=outs[a].at[slot],
                send_sem=send_sems.at[7 * a + k], recv_sem=recv_sems.at[7 * a + k],
                device_id=to, device_id_type=pl.DeviceIdType.MESH)

        mine = [pltpu.make_async_copy(ins[a], outs[a].at[4 * x + 2 * y + c], local_sems.at[a])
                for a in range(n)]
        first = []
        for a in range(n):
            first.append(copy(a, 0, me, sibling, src=ins[a]))
            first += [copy(a, 1 + j, me, (*chip, c), src=ins[a]) for j, chip in enumerate(chips)]
        for cp in mine + first:
            cp.start()
        passed = []
        for j, chip in enumerate(chips):
            for a in range(n):
                copy(a, 1 + j, (*chip, c), me).wait_recv()
                passed.append(copy(a, 4 + j, (*chip, c), sibling))
                passed[-1].start()
        for a in range(n):
            copy(a, 0, sibling, me).wait_recv()
            for j, chip in enumerate(chips):
                copy(a, 4 + j, (*chip, 1 - c), me).wait_recv()
        for cp in first + passed:
            cp.wait_send()
        for cp in mine:
            cp.wait()

    outs = pl.pallas_call(
        body, name=name, out_shape=tuple(_exchange_shapes(arrs, ["ag"] * n)),
        in_specs=[ANY_SPEC] * n, out_specs=tuple([ANY_SPEC] * n),
        scratch_shapes=_exchange_sems(n),
    )(*arrs)
    return list(outs)


SHARD_COLS = N_CHUNKS * ATT_WIDTH // N_DEV


def _weights_full(shards):
    return jnp.transpose(shards, (1, 0, 2)).reshape(D_MODEL, N_CHUNKS * ATT_WIDTH)


def _wcols(w_ref, k):
    orig = CHUNK_PERM[k]
    return w_ref[:, orig * ATT_WIDTH:(orig + 1) * ATT_WIDTH]


def _store_shard_columns(stage_ref, shards_ref, sems, first_chunk):
    copies = []
    for t in range(stage_ref.shape[1] // ATT_WIDTH):
        col = ATT_WIDTH * CHUNK_PERM[first_chunk + t]
        end = col + ATT_WIDTH
        while col < end:
            dev = col // SHARD_COLS
            upto = min(end, (dev + 1) * SHARD_COLS)
            src = stage_ref.at[:, pl.ds(t * ATT_WIDTH + ATT_WIDTH - (end - col), upto - col)]
            dst = shards_ref.at[dev, :, pl.ds(col - dev * SHARD_COLS, upto - col)]
            copies.append(pltpu.make_async_copy(src, dst, sems.at[len(copies)]))
            col = upto
    for cp in copies:
        cp.start()
    for cp in copies:
        cp.wait()


SHARD_COPIES_PER_CHUNK = 2


def _call(body, args, *, name, grid, in_specs, out_specs, out_shape, scratch_shapes=(), aliases=None,
          hosted=None):
    sem = ("arbitrary",) * len(grid)
    if hosted is None:
        outs = pl.pallas_call(
            body, name=name, grid=grid, in_specs=in_specs, out_specs=tuple(out_specs),
            out_shape=tuple(out_shape), scratch_shapes=list(scratch_shapes),
            input_output_aliases=aliases or {}, compiler_params=_params(sem))(*args)
        return list(outs), []
    x_arrs, modes = hosted
    n_in, n_out, n_scr, nx = len(args), len(out_shape), len(scratch_shapes), len(x_arrs)

    def wrapped(*refs):
        ins, x_ins = refs[:n_in], refs[n_in:n_in + nx]
        outs = refs[n_in + nx:n_in + nx + n_out]
        x_outs = refs[n_in + nx + n_out:n_in + 2 * nx + n_out]
        scr = refs[n_in + 2 * nx + n_out:n_in + 2 * nx + n_out + n_scr]
        sems = refs[n_in + 2 * nx + n_out + n_scr:]
        first = pl.program_id(0) == 0
        last = pl.program_id(0) == grid[0] - 1
        for axis in range(1, len(grid)):
            first = jnp.logical_and(first, pl.program_id(axis) == 0)
            last = jnp.logical_and(last, pl.program_id(axis) == grid[axis] - 1)

        @pl.when(first)
        def _():
            _exchange_start(_exchange_copies(x_ins, x_outs, modes, sems))

        body(*ins, *outs, *scr)

        @pl.when(last)
        def _():
            _exchange_wait(_exchange_copies(x_ins, x_outs, modes, sems))

    outs = pl.pallas_call(
        wrapped, name=name, grid=grid, in_specs=list(in_specs) + [ANY_SPEC] * nx,
        out_specs=tuple(out_specs) + tuple([ANY_SPEC] * nx),
        out_shape=tuple(out_shape) + tuple(_exchange_shapes(x_arrs, modes)),
        scratch_shapes=list(scratch_shapes) + _exchange_sems(nx),
        input_output_aliases=aliases or {}, compiler_params=_params(sem))(*args, *x_arrs)
    return list(outs[:n_out]), list(outs[n_out:])


def _mod_fwd(c_pad, w_mod):
    def body(c_ref, w_ref, o_ref):
        sc = _silu(c_ref[...]).astype(BF16)
        for layer in range(2):
            o_ref[layer] = _dot(sc, w_ref[layer].astype(BF16))

    return pl.pallas_call(
        body, name="mod_fwd", out_shape=jax.ShapeDtypeStruct((2, 16, w_mod.shape[2]), F32),
        compiler_params=_params())(c_pad, w_mod)


def _mod_wgrad(c_t, dmod):
    n_cols = dmod.shape[2]

    def body(c_ref, d_ref, o_ref):
        sc = _silu(c_ref[...]).astype(BF16).astype(F32)
        for layer in range(2):
            dm = d_ref[layer].astype(BF16).astype(F32)
            acc = sc[:, 0:1] * dm[0:1, :]
            for b in range(1, N_DEV):
                acc = acc + sc[:, b:b + 1] * dm[b:b + 1, :]
            o_ref[layer] = acc

    return pl.pallas_call(
        body, name="mod_wgrad", out_shape=jax.ShapeDtypeStruct((2, D_MODEL, n_cols), F32),
        compiler_params=_params())(c_t, dmod)


GROUP_COLS = 3 * ATT_WIDTH
PROJ_TM = 256


def _norm_proj_fwd(x, vec, w_int, hosted=None):
    S = x.shape[0]
    tm = PROJ_TM
    perms = [_perm_matrix(r, tm) for r in DILATIONS[1:]]

    def body(x_ref, v_ref, w_ref, p1_ref, p2_ref, ht0_ref, ht1_ref, ht2_ref, q0_ref, q1_ref, q2_ref,
             rest_ref):
        xv = x_ref[...]
        rstd = lax.rsqrt(jnp.mean(xv * xv, axis=-1, keepdims=True) + NORM_EPS)
        hf = ((xv * rstd) * v_ref[0:1, :]) * (1.0 + v_ref[1:2, :]) + v_ref[2:3, :]
        h = hf.astype(BF16)
        for g, (q_ref, ht_ref, p_ref) in enumerate(((q0_ref, ht0_ref, None), (q1_ref, ht1_ref, p1_ref),
                                                    (q2_ref, ht2_ref, p2_ref))):
            rows_f = hf if p_ref is None else _dot(p_ref[...], h)
            ht_ref[...] = rows_f.T.astype(BF16)
            rows = rows_f.astype(BF16)
            n = tm // q_ref.shape[0]
            for t in range(3):
                res = _dot(rows, _wcols(w_ref, 3 * g + t)).astype(BF16)
                for rho in range(q_ref.shape[0]):
                    q_ref[rho, :, t * ATT_WIDTH:(t + 1) * ATT_WIDTH] = res[rho * n:(rho + 1) * n]
        split = (QKV_CHUNKS - 1) * ATT_WIDTH
        rest_ref[:, :split] = _dot(h, w_ref[:, (QKV_CHUNKS + 1) * ATT_WIDTH:])
        rest_ref[:, split:] = _dot(h, w_ref[:, QKV_CHUNKS * ATT_WIDTH:(QKV_CHUNKS + 1) * ATT_WIDTH])

    n_rest = QKV_CHUNKS * ATT_WIDTH
    whole = pl.BlockSpec(memory_space=pltpu.VMEM)
    ht_spec = pl.BlockSpec((D_MODEL, tm), lambda i: (0, i))
    ht_shape = jax.ShapeDtypeStruct((D_MODEL, S), BF16)
    return _call(
        body, (x, vec, w_int, *perms), name="norm_proj_fwd", grid=(S // tm,),
        in_specs=[pl.BlockSpec((tm, D_MODEL), lambda i: (i, 0)),
                  pl.BlockSpec((8, D_MODEL), lambda i: (0, 0)), whole, whole, whole],
        out_specs=(ht_spec, ht_spec, ht_spec,
                   *[pl.BlockSpec((r, tm // r, GROUP_COLS), lambda i: (0, i, 0)) for r in DILATIONS],
                   pl.BlockSpec((tm, n_rest), lambda i: (i, 0))),
        out_shape=(ht_shape, ht_shape, ht_shape,
                   *[jax.ShapeDtypeStruct((r, S // r, GROUP_COLS), BF16) for r in DILATIONS],
                   jax.ShapeDtypeStruct((S, n_rest), F32)),
        hosted=hosted)


def _proj_dgrad_norm_bwd(dqkv, drest, w_int, x, dxn, vec, hosted=None):
    S = x.shape[0]
    tm = PROJ_TM
    perms = [_perm_matrix(r, tm, transpose=True) for r in DILATIONS[1:]]

    def body(d0_ref, d1_ref, d2_ref, dr_ref, w_ref, p1_ref, p2_ref, x_ref, dxn_ref, v_ref, dx_ref, acc_ref):
        i = pl.program_id(0)

        @pl.when(i == 0)
        def _():
            acc_ref[...] = jnp.zeros_like(acc_ref)

        wcols = functools.partial(_wcols, w_ref)
        dhv = None
        for g, (d_ref, p_ref) in enumerate(((d0_ref, None), (d1_ref, p1_ref), (d2_ref, p2_ref))):
            for t in range(3):
                d = _rows_of_residues(d_ref, (t,))
                if p_ref is not None:
                    d = _dot(p_ref[...], d).astype(BF16)
                part = _dot_nt(d, wcols(3 * g + t))
                dhv = part if dhv is None else dhv + part
        for t in range(7):
            dhv = dhv + _dot_nt(dr_ref[t], wcols(11 + t))
        for t in range(2):
            dhv = dhv + _dot_nt(dr_ref[8 + t], wcols(9 + t))

        xv = x_ref[...]
        g = v_ref[0:1, :]
        sc1 = 1.0 + v_ref[1:2, :]
        rstd = lax.rsqrt(jnp.mean(xv * xv, axis=-1, keepdims=True) + NORM_EPS)
        xhat = xv * rstd
        acc_ref[0:1, :] += jnp.sum(dhv, axis=0, keepdims=True)
        acc_ref[1:2, :] += jnp.sum(dhv * (xhat * g), axis=0, keepdims=True)
        acc_ref[2:3, :] += jnp.sum(dhv * xhat * sc1, axis=0, keepdims=True)
        dxhat = dhv * (g * sc1)
        dx = rstd * (dxhat - xhat * jnp.mean(dxhat * xhat, axis=-1, keepdims=True))
        dx_ref[...] = dx + dxn_ref[...]

    row = pl.BlockSpec((tm, D_MODEL), lambda i: (i, 0))
    vec_spec = pl.BlockSpec((8, D_MODEL), lambda i: (0, 0))
    whole = pl.BlockSpec(memory_space=pltpu.VMEM)
    return _call(
        body, (*dqkv, drest, w_int, *perms, x, dxn, vec), name="proj_dgrad_norm_bwd", grid=(S // tm,),
        in_specs=[*[pl.BlockSpec((3, r, tm // r, ATT_WIDTH), lambda i: (0, 0, i, 0)) for r in DILATIONS],
                  pl.BlockSpec((DREST_CHUNKS, tm, ATT_WIDTH), lambda i: (0, i, 0)),
                  whole, whole, whole, row, row, vec_spec],
        out_specs=(row, vec_spec),
        out_shape=(jax.ShapeDtypeStruct((S, D_MODEL), F32), jax.ShapeDtypeStruct((8, D_MODEL), F32)),
        hosted=hosted)


WGRAD_TS = 1024


def _proj_wgrad_call(h_t, d, d_spec, rows_of, n_chunks, first_chunk, shards, name):
    S = h_t.shape[1]
    n_steps = S // WGRAD_TS
    width = n_chunks * ATT_WIDTH

    def body(*refs):
        h_ref, d_ref = refs[:2]
        shards_ref, acc_ref, stage_ref, sems = refs[-4:]
        i = pl.program_id(0)

        @pl.when(i == 0)
        def _():
            acc_ref[...] = jnp.zeros_like(acc_ref)

        ht = h_ref[...]
        for t in range(n_chunks):
            acc_ref[:, t * ATT_WIDTH:(t + 1) * ATT_WIDTH] += _dot(ht, rows_of(d_ref, t))

        @pl.when(i == n_steps - 1)
        def _():
            stage_ref[...] = acc_ref[...].astype(BF16)
            _store_shard_columns(stage_ref, shards_ref, sems, first_chunk)

    chained = shards is not None
    return pl.pallas_call(
        body, name=name, grid=(n_steps,),
        in_specs=[pl.BlockSpec((D_MODEL, WGRAD_TS), lambda i: (0, i)), d_spec] + [ANY_SPEC] * chained,
        out_specs=ANY_SPEC,
        out_shape=jax.ShapeDtypeStruct((N_DEV, D_MODEL, SHARD_COLS), BF16),
        scratch_shapes=[pltpu.VMEM((D_MODEL, width), F32), pltpu.VMEM((D_MODEL, width), BF16),
                        pltpu.SemaphoreType.DMA((SHARD_COPIES_PER_CHUNK * n_chunks,))],
        input_output_aliases={2: 0} if chained else {},
        compiler_params=_params(("arbitrary",)))(h_t, d, *([shards] * chained))


def _proj_wgrad_part(h_t, d, n_chunks, chunk_block, first_chunk, shards, name):
    spec = pl.BlockSpec((n_chunks, WGRAD_TS, ATT_WIDTH), lambda i: (chunk_block, i, 0))
    return _proj_wgrad_call(h_t, d, spec, lambda d_ref, t: d_ref[t], n_chunks, first_chunk, shards, name)


def _proj_wgrad_group(h_t, d, g, r, shards, name):
    n = PROJ_TM // r

    def rows_of(d_ref, t):
        if r == 1:
            return d_ref[t, 0]
        return jnp.concatenate([d_ref[t, rho, s * n:(s + 1) * n] for s in range(WGRAD_TS // PROJ_TM)
                                for rho in range(r)], axis=0)

    spec = pl.BlockSpec((3, r, WGRAD_TS // r, ATT_WIDTH), lambda i: (0, 0, i, 0))
    return _proj_wgrad_call(h_t, d, spec, rows_of, 3, 3 * g, shards, name)


def _wgrad_tn(a, b, name):
    S, M = a.shape
    N = b.shape[1]
    ts = 1024

    def body(a_ref, b_ref, o_ref):
        i = pl.program_id(0)
        part = _dot_tn(a_ref[...], b_ref[...])

        @pl.when(i == 0)
        def _():
            o_ref[...] = part

        @pl.when(i > 0)
        def _():
            o_ref[...] += part

    return pl.pallas_call(
        body, name=name, grid=(S // ts,),
        in_specs=[pl.BlockSpec((ts, M), lambda i: (i, 0)), pl.BlockSpec((ts, N), lambda i: (i, 0))],
        out_specs=pl.BlockSpec((M, N), lambda i: (0, 0)),
        out_shape=jax.ShapeDtypeStruct((M, N), F32),
        compiler_params=_params(("arbitrary",)))(a, b)


ATTN_FWD_BLOCKS = 8


def _attn_fwd(qkv, g, r, hosted=None):
    L = qkv.shape[1]
    qb = min(ATTN_FWD_BLOCKS, L // BAND)
    n_steps = L // (qb * BAND)
    scale = HEAD_DIM ** -0.5

    def body(q_ref, kp_ref, kc_ref, vp_ref, vc_ref, o_ref, st_ref):
        m_step = pl.program_id(1)
        ii = lax.broadcasted_iota(jnp.int32, (BAND, 2 * BAND), 0)
        kk = lax.broadcasted_iota(jnp.int32, (BAND, 2 * BAND), 1)
        band = jnp.logical_and(kk >= ii, kk <= ii + BAND)
        first_mask = jnp.logical_and(band, jnp.logical_or(kk >= BAND, m_step > 0))
        lane = lax.broadcasted_iota(jnp.int32, (BAND, 128), 1)
        stats = [jnp.zeros((BAND, 128), F32) for _ in range(qb)]
        for h in range(HEADS):
            sl = slice(h * HEAD_DIM, (h + 1) * HEAD_DIM)
            kc, vc = kc_ref[:, sl], vc_ref[:, sl]
            for b in range(qb):
                rows = slice(b * BAND, (b + 1) * BAND)
                if b == 0:
                    keys = jnp.concatenate([kp_ref[:, sl], kc[:BAND]], axis=0)
                    vals = jnp.concatenate([vp_ref[:, sl], vc[:BAND]], axis=0)
                else:
                    keys, vals = kc[(b - 1) * BAND:(b + 1) * BAND], vc[(b - 1) * BAND:(b + 1) * BAND]
                s = jnp.where(first_mask if b == 0 else band, _dot_nt(q_ref[rows, sl], keys) * scale, NEG_INF)
                m = jnp.max(s, axis=1, keepdims=True)
                p = jnp.exp(s - m)
                l = jnp.sum(p, axis=1, keepdims=True)
                o_ref[rows, sl] = _dot(p.astype(BF16), vals) / l
                stats[b] = jnp.where(lane == h, m, stats[b])
                stats[b] = jnp.where(lane == HEADS + h, l, stats[b])
        for b in range(qb):
            st_ref[b * BAND:(b + 1) * BAND, :] = stats[b]

    many = (None, qb * BAND, ATT_WIDTH)
    one = (None, BAND, ATT_WIDTH)

    def prev(m):
        return jnp.maximum(qb * m - 1, 0)

    return _call(
        body, (qkv, qkv, qkv, qkv, qkv), name=f"attn_fwd_g{g}", grid=(r, n_steps),
        in_specs=[
            pl.BlockSpec(many, lambda rho, m: (rho, m, 0)),
            pl.BlockSpec(one, lambda rho, m: (rho, prev(m), 1)),
            pl.BlockSpec(many, lambda rho, m: (rho, m, 1)),
            pl.BlockSpec(one, lambda rho, m: (rho, prev(m), 2)),
            pl.BlockSpec(many, lambda rho, m: (rho, m, 2)),
        ],
        out_specs=(pl.BlockSpec(many, lambda rho, m: (rho, m, 0)),
                   pl.BlockSpec((None, qb * BAND, 128), lambda rho, m: (rho, m, 0))),
        out_shape=(jax.ShapeDtypeStruct((r, L, ATT_WIDTH), F32),
                   jax.ShapeDtypeStruct((r, L, 128), F32)),
        hosted=hosted)


def _attn_combine(outs, stats, rest):
    S = rest.shape[0]
    tm = 256
    gatt_blk = 8
    back = [_perm_matrix(r, tm, transpose=True) for r in DILATIONS[1:]]
    forth = [_perm_matrix(r, tm) for r in DILATIONS[1:]]

    def body(o0_ref, o1_ref, o2_ref, s0_ref, s1_ref, s2_ref, g_ref, b1_ref, b2_ref, f1_ref, f2_ref,
             o_ref, a_ref, l0_ref, l1_ref, l2_ref):
        outs_nat = [o0_ref[0]] + [_permute_f32(b_ref[...], _rows_of_residues(o_g), 2)
                                  for o_g, b_ref in ((o1_ref, b1_ref), (o2_ref, b2_ref))]
        st = [s0_ref[0]] + [_permute_f32(b_ref[...], _rows_of_residues(s_g), 3)
                            for s_g, b_ref in ((s1_ref, b1_ref), (s2_ref, b2_ref))]
        lane = lax.broadcasted_iota(jnp.int32, (tm, 128), 1)
        lse_out = jnp.zeros((tm, 128), F32)
        for h in range(HEADS):
            sl = slice(h * HEAD_DIM, (h + 1) * HEAD_DIM)
            ms = [s[:, h:h + 1] for s in st]
            ls = [s[:, HEADS + h:HEADS + h + 1] for s in st]
            m_all = jnp.maximum(jnp.maximum(ms[0], ms[1]), ms[2])
            ws = [l * jnp.exp(m - m_all) for m, l in zip(ms, ls)]
            den = ws[0] + ws[1] + ws[2]
            o = (ws[0] * outs_nat[0][:, sl] + ws[1] * outs_nat[1][:, sl] + ws[2] * outs_nat[2][:, sl]) / den
            o_ref[:, sl] = o
            a_ref[:, sl] = (o * _silu(g_ref[:, sl])).astype(BF16)
            lse_out = jnp.where(lane == h, m_all + jnp.log(den), lse_out)
        l0_ref[0] = lse_out
        for l_ref, f_ref in ((l1_ref, f1_ref), (l2_ref, f2_ref)):
            _store_residues(l_ref, _permute_f32(f_ref[...], lse_out, 3))

    o_spec = pl.BlockSpec((tm, ATT_WIDTH), lambda i: (i, 0))
    whole = pl.BlockSpec(memory_space=pltpu.VMEM)

    def res_spec(r, width):
        return pl.BlockSpec((r, tm // r, width), lambda i: (0, i, 0))

    return pl.pallas_call(
        body, name="attn_combine", grid=(S // tm,),
        in_specs=[*[res_spec(r, ATT_WIDTH) for r in DILATIONS], *[res_spec(r, 128) for r in DILATIONS],
                  pl.BlockSpec((tm, ATT_WIDTH), lambda i: (i, gatt_blk)), whole, whole, whole, whole],
        out_specs=(o_spec, o_spec, *[res_spec(r, 128) for r in DILATIONS]),
        out_shape=(jax.ShapeDtypeStruct((S, ATT_WIDTH), F32), jax.ShapeDtypeStruct((S, ATT_WIDTH), BF16),
                   *[jax.ShapeDtypeStruct((r, S // r, 128), F32) for r in DILATIONS]),
        compiler_params=_params(("parallel",)))(*outs, *stats, rest, *back, *forth)


ATTN_BWD_PAIRS = 4


def _attn_bwd(qkv, do, lse, dvec, g, r, hosted=None):
    L = qkv.shape[1]
    nb = L // BAND
    pairs = min(ATTN_BWD_PAIRS, nb // 2)
    kb = 2 * pairs
    n_steps = nb // kb
    scale = HEAD_DIM ** -0.5

    def body(qc_ref, qn_ref, k_ref, v_ref, doc_ref, don_ref, lc_ref, ln_ref, dc_ref, dn_ref,
             out_ref, carry_ref):
        j = pl.program_id(1)

        @pl.when(j == 0)
        def _():
            carry_ref[...] = jnp.zeros_like(carry_ref)

        ii = lax.broadcasted_iota(jnp.int32, (3 * BAND, 2 * BAND), 0)
        kk = lax.broadcasted_iota(jnp.int32, (3 * BAND, 2 * BAND), 1)
        band = jnp.logical_and(kk <= ii, kk >= ii - BAND)
        last_band = jnp.logical_and(band, jnp.logical_or(ii < 2 * BAND, j < n_steps - 1))
        lse_all = jnp.concatenate([lc_ref[...], ln_ref[...]], axis=0)
        dvec_all = jnp.concatenate([dc_ref[...], dn_ref[...]], axis=0)
        for h in range(HEADS):
            sl = slice(h * HEAD_DIM, (h + 1) * HEAD_DIM)
            q_all = jnp.concatenate([qc_ref[:, sl], qn_ref[:, sl]], axis=0)
            do_all = jnp.concatenate([doc_ref[:, sl], don_ref[:, sl]], axis=0)
            dq_prev = carry_ref[:, sl]
            for a in range(pairs):
                keys = slice(2 * a * BAND, (2 * a + 2) * BAND)
                rows = slice(2 * a * BAND, (2 * a + 3) * BAND)
                k, v, q, do = k_ref[keys, sl], v_ref[keys, sl], q_all[rows], do_all[rows]
                mask = last_band if a == pairs - 1 else band
                p = jnp.where(mask, jnp.exp(_dot_nt(q, k) * scale - lse_all[rows, h:h + 1]), 0.0)
                ds = (p * (_dot_nt(do, v) - dvec_all[rows, h:h + 1])).astype(BF16)
                dq3 = _dot(ds, k) * scale
                out_ref[0, 2 * a * BAND:(2 * a + 1) * BAND, sl] = (dq_prev + dq3[:BAND]).astype(BF16)
                out_ref[0, (2 * a + 1) * BAND:(2 * a + 2) * BAND, sl] = dq3[BAND:2 * BAND].astype(BF16)
                out_ref[1, keys, sl] = (_dot_tn(ds, q) * scale).astype(BF16)
                out_ref[2, keys, sl] = _dot_tn(p.astype(BF16), do).astype(BF16)
                dq_prev = dq3[2 * BAND:]
            carry_ref[:, sl] = dq_prev

    many = (None, kb * BAND, ATT_WIDTH)
    one = (None, BAND, ATT_WIDTH)
    smany = (None, kb * BAND, 128)
    sone = (None, BAND, 128)

    def nxt(j):
        return jnp.minimum(kb * j + kb, nb - 1)

    (out,), got = _call(
        body, (qkv, qkv, qkv, qkv, do, do, lse, lse, dvec, dvec), name=f"attn_bwd_g{g}", grid=(r, n_steps),
        in_specs=[
            pl.BlockSpec(many, lambda rho, j: (rho, j, 0)),
            pl.BlockSpec(one, lambda rho, j: (rho, nxt(j), 0)),
            pl.BlockSpec(many, lambda rho, j: (rho, j, 1)),
            pl.BlockSpec(many, lambda rho, j: (rho, j, 2)),
            pl.BlockSpec(many, lambda rho, j: (rho, j, 0)),
            pl.BlockSpec(one, lambda rho, j: (rho, nxt(j), 0)),
            pl.BlockSpec(smany, lambda rho, j: (rho, j, 0)),
            pl.BlockSpec(sone, lambda rho, j: (rho, nxt(j), 0)),
            pl.BlockSpec(smany, lambda rho, j: (rho, j, 0)),
            pl.BlockSpec(sone, lambda rho, j: (rho, nxt(j), 0)),
        ],
        out_specs=(pl.BlockSpec((3, None, kb * BAND, ATT_WIDTH), lambda rho, j: (0, rho, j, 0)),),
        out_shape=(jax.ShapeDtypeStruct((3, r, L, ATT_WIDTH), BF16),),
        scratch_shapes=[pltpu.VMEM((BAND, ATT_WIDTH), F32)],
        hosted=hosted)
    return out, got


LRU_T = 256


def _linear_scan(a, b, carry, reverse):
    T, C = a.shape
    a = a.reshape(T // 8, 8, C)
    b = b.reshape(T // 8, 8, C)
    row8 = lax.broadcasted_iota(jnp.int32, a.shape, 1)
    for s in (1, 2, 4):
        keep = (row8 < 8 - s) if reverse else (row8 >= s)
        shift = 8 - s if reverse else s
        b_sh = jnp.where(keep, pltpu.roll(b, shift, 1), 0.0)
        a_sh = jnp.where(keep, pltpu.roll(a, shift, 1), 1.0)
        b = a * b_sh + b
        a = a * a_sh
    tiles = [None] * (T // 8)
    order = range(T // 8 - 1, -1, -1) if reverse else range(T // 8)
    for k in order:
        y = b[k] + a[k] * carry
        tiles[k] = y
        carry = y[0:1] if reverse else y[7:8]
    return jnp.concatenate(tiles, axis=0), carry


def _gate_matmuls(ucb, w_ref, bias):
    parts = [_dot(ucb[:, j * 128:(j + 1) * 128], w_ref[j]) for j in range(8)]
    return jnp.concatenate(parts, axis=1) + bias


def _rows_before(x, prev8, j):
    T, C = x.shape
    rot = pltpu.roll(x.reshape(T // 8, 8, C), j, 1)
    rot_prev = jnp.concatenate([pltpu.roll(prev8, j, 0)[None], rot[:-1]], axis=0)
    row8 = lax.broadcasted_iota(jnp.int32, rot.shape, 1)
    return jnp.where(row8 >= j, rot, rot_prev).reshape(T, C)


def _rows_after(x, next8, j):
    T, C = x.shape
    rot = pltpu.roll(x.reshape(T // 8, 8, C), 8 - j, 1)
    rot_next = jnp.concatenate([rot[1:], pltpu.roll(next8, 8 - j, 0)[None]], axis=0)
    row8 = lax.broadcasted_iota(jnp.int32, rot.shape, 1)
    return jnp.where(row8 < 8 - j, rot, rot_next).reshape(T, C)


def _conv_fwd(u, u_prev8, cw_ref):
    uc = cw_ref[4:5, :] + cw_ref[0:1, :] * u
    for j in range(1, CONV_WIDTH):
        uc = uc + cw_ref[j:j + 1, :] * _rows_before(u, u_prev8, j)
    return uc


def _lru_fwd(rest, cw, wr2, wi2, hosted=None):
    S = rest.shape[0]
    T = LRU_T

    def body(u_ref, cw_ref, wr_ref, wi_ref, h_ref, a_ref, uc_ref, r_ref, ig_ref, mult_ref,
             ucar_ref, hcar_ref):
        c = pl.program_id(0)

        @pl.when(c == 0)
        def _():
            ucar_ref[...] = jnp.zeros_like(ucar_ref)
            hcar_ref[...] = jnp.zeros_like(hcar_ref)

        u = u_ref[...]
        uc = _conv_fwd(u, ucar_ref[...], cw_ref)
        ucar_ref[...] = u[T - 8:, :]
        ucb = uc.astype(BF16)
        r = jax.nn.sigmoid(_gate_matmuls(ucb, wr_ref, cw_ref[5:6, :]))
        ig = jax.nn.sigmoid(_gate_matmuls(ucb, wi_ref, cw_ref[6:7, :]))
        log_a = -LRU_C * r * _softplus_neg(cw_ref[7:8, :])
        a = jnp.exp(log_a)
        mult = jnp.sqrt(_neg_expm1(2.0 * log_a))
        b = mult * (ig * uc)
        a_ref[...] = a
        uc_ref[...] = uc
        r_ref[...] = r
        ig_ref[...] = ig
        mult_ref[...] = mult
        h, last = _linear_scan(a, b, hcar_ref[0:1, :], reverse=False)
        h_ref[...] = h
        hcar_ref[...] = jnp.broadcast_to(last, (8, D_MODEL))

    row_spec = pl.BlockSpec((T, D_MODEL), lambda c: (c, 0))
    row_shape = jax.ShapeDtypeStruct((S, D_MODEL), F32)
    return _call(
        body, (rest, cw, wr2, wi2), name="lru_fwd", grid=(S // T,),
        in_specs=[row_spec, pl.BlockSpec((8, D_MODEL), lambda c: (0, 0)),
                  pl.BlockSpec((8, 128, 128), lambda c: (0, 0, 0)),
                  pl.BlockSpec((8, 128, 128), lambda c: (0, 0, 0))],
        out_specs=(row_spec,) * 6, out_shape=(row_shape,) * 6,
        scratch_shapes=[pltpu.VMEM((8, D_MODEL), F32), pltpu.VMEM((8, D_MODEL), F32)],
        hosted=hosted)


def _lru_bwd(dhl, hl, a_all, gates, rest, cw, wr2, wi2, dproj, hosted=None):
    S = rest.shape[0]
    T = LRU_T
    nc = S // T

    def body(dh_ref, h_ref, hp_ref, a_ref, uc_ref, r_ref, ig_ref, mult_ref, u_ref, up_ref, cw_ref, wr_ref,
             wi_ref, _alias, du_ref, acc_ref, gwr_ref, gwi_ref, gcar_ref, acar_ref, dcar_ref):
        step = pl.program_id(0)
        c = nc - 1 - step

        @pl.when(step == 0)
        def _():
            acc_ref[...] = jnp.zeros_like(acc_ref)
            gwr_ref[...] = jnp.zeros_like(gwr_ref)
            gwi_ref[...] = jnp.zeros_like(gwi_ref)
            gcar_ref[...] = jnp.zeros_like(gcar_ref)
            acar_ref[...] = jnp.zeros_like(acar_ref)
            dcar_ref[...] = jnp.zeros_like(dcar_ref)

        u = u_ref[...]
        u_prev = jnp.where(c > 0, up_ref[...], 0.0)
        h_prev8 = jnp.where(c > 0, hp_ref[...], 0.0)
        a = a_ref[...]
        h = h_ref[...]
        uc, r, ig, mult = uc_ref[...], r_ref[...], ig_ref[...], mult_ref[...]
        ucb = uc.astype(BF16)
        sp = _softplus_neg(cw_ref[7:8, :])
        a_next = _rows_after(a, acar_ref[...], 1)
        G, first = _linear_scan(a_next, dh_ref[...], gcar_ref[0:1, :], reverse=True)
        gcar_ref[...] = jnp.broadcast_to(first, (8, D_MODEL))
        acar_ref[...] = jnp.broadcast_to(a[0:1, :], (8, D_MODEL))
        d_a = G * _rows_before(h, h_prev8, 1)
        d_mult = G * (ig * uc)
        d_ig = G * (mult * uc)
        duc = G * (mult * ig)
        d_log_a = d_a * a - d_mult * (a * a) / mult
        d_r = d_log_a * (-LRU_C * sp)
        d_sp = jnp.sum(d_log_a * (-LRU_C * r), axis=0, keepdims=True)
        dpre_r = d_r * r * (1.0 - r)
        dpre_i = d_ig * ig * (1.0 - ig)
        dprb = dpre_r.astype(BF16)
        dpib = dpre_i.astype(BF16)
        back = []
        for j in range(8):
            sl = slice(j * 128, (j + 1) * 128)
            back.append(_dot_nt(dprb[:, sl], wr_ref[j]) + _dot_nt(dpib[:, sl], wi_ref[j]))
            gwr_ref[j] += _dot_tn(ucb[:, sl], dprb[:, sl])
            gwi_ref[j] += _dot_tn(ucb[:, sl], dpib[:, sl])
        duc = duc + jnp.concatenate(back, axis=1)
        du = cw_ref[0:1, :] * duc
        duc_next8 = dcar_ref[...]
        acc_ref[0:1, :] += jnp.sum(duc * u, axis=0, keepdims=True)
        for j in range(1, CONV_WIDTH):
            du = du + cw_ref[j:j + 1, :] * _rows_after(duc, duc_next8, j)
            acc_ref[j:j + 1, :] += jnp.sum(duc * _rows_before(u, u_prev, j), axis=0, keepdims=True)
        dcar_ref[...] = duc[0:8, :]
        acc_ref[4:5, :] += jnp.sum(duc, axis=0, keepdims=True)
        acc_ref[5:6, :] += jnp.sum(dpre_r, axis=0, keepdims=True)
        acc_ref[6:7, :] += jnp.sum(dpre_i, axis=0, keepdims=True)
        acc_ref[7:8, :] += d_sp
        du_ref[0] = du[:, 0:ATT_WIDTH].astype(BF16)
        du_ref[1] = du[:, ATT_WIDTH:].astype(BF16)

    def rev(step):
        return nc - 1 - step

    def prev8(step):
        return jnp.maximum(rev(step) * (T // 8) - 1, 0)

    row_spec = pl.BlockSpec((T, D_MODEL), lambda s: (rev(s), 0))
    prev_spec = pl.BlockSpec((8, D_MODEL), lambda s: (prev8(s), 0))
    vec_spec = pl.BlockSpec((8, D_MODEL), lambda s: (0, 0))
    w_spec = pl.BlockSpec((8, 128, 128), lambda s: (0, 0, 0))
    return _call(
        body, (dhl, hl, hl, a_all, *gates, rest, rest, cw, wr2, wi2, dproj), name="lru_bwd", grid=(nc,),
        in_specs=[row_spec, row_spec, prev_spec, row_spec, row_spec, row_spec, row_spec, row_spec,
                  row_spec, prev_spec, vec_spec, w_spec, w_spec, ANY_SPEC],
        out_specs=(pl.BlockSpec((2, T, ATT_WIDTH), lambda s: (4, rev(s), 0)), vec_spec, w_spec, w_spec),
        out_shape=(jax.ShapeDtypeStruct(dproj.shape, BF16), jax.ShapeDtypeStruct((8, D_MODEL), F32),
                   jax.ShapeDtypeStruct((8, 128, 128), F32), jax.ShapeDtypeStruct((8, 128, 128), F32)),
        scratch_shapes=[pltpu.VMEM((8, D_MODEL), F32), pltpu.VMEM((8, D_MODEL), F32),
                        pltpu.VMEM((8, D_MODEL), F32)],
        aliases={13: 0}, hosted=hosted)


OUT_TM = 256


def _out_fwd(a_gated, hl, rest, x, vec, w_pa, w_pb, w_o, target=None, hosted=None):
    S = x.shape[0]
    tm = OUT_TM
    with_loss = target is not None

    def body(*refs):
        (ag_ref, hl_ref, gl_ref, ma_ref, mb_ref, x_ref, v_ref, wpa_ref, wpb_ref, wo_ref), refs = refs[:10], refs[10:]
        if with_loss:
            t_ref, refs = refs[0], refs[1:]
        xn_ref, mix_ref, ya_ref, yb_ref, bg_ref = refs[:5]
        bg = (hl_ref[...] * _silu(gl_ref[...])).astype(BF16)
        ya = _dot(ag_ref[...], wpa_ref[...])
        yb = _dot(bg, wpb_ref[...])
        mix = (jax.nn.sigmoid(ma_ref[...]) * ya + jax.nn.sigmoid(mb_ref[...]) * yb).astype(BF16)
        out = _dot(mix, wo_ref[...])
        rstd = lax.rsqrt(jnp.mean(out * out, axis=-1, keepdims=True) + NORM_EPS)
        x_new = x_ref[...] + v_ref[1:2, :] * ((out * rstd) * v_ref[0:1, :])
        if with_loss:
            acc_ref = refs[5]

            @pl.when(pl.program_id(0) == 0)
            def _():
                acc_ref[...] = jnp.zeros_like(acc_ref)

            err = x_new - t_ref[...]
            xn_ref[...] = err * (1.0 / D_MODEL)
            part = jnp.sum(jnp.sum(err * err, axis=1, keepdims=True), axis=0, keepdims=True)
            acc_ref[...] += jnp.broadcast_to(part, acc_ref.shape)
        else:
            xn_ref[...] = x_new
        mix_ref[...] = mix
        ya_ref[...] = ya.astype(BF16)
        yb_ref[...] = yb.astype(BF16)
        bg_ref[...] = bg

    def col(j):
        return pl.BlockSpec((tm, D_MODEL), lambda i: (i, j))

    def whole(shape):
        return pl.BlockSpec(shape, lambda i: (0, 0))

    row = col(0)
    bf = jax.ShapeDtypeStruct((S, D_MODEL), BF16)
    loss_in = ([target], [row]) if with_loss else ([], [])
    loss_out = ([jax.ShapeDtypeStruct((8, 128), F32)], [whole((8, 128))]) if with_loss else ([], [])
    return _call(
        body, (a_gated, hl, rest, rest, rest, x, vec, w_pa, w_pb, w_o, *loss_in[0]), name="out_fwd",
        grid=(S // tm,),
        in_specs=[pl.BlockSpec((tm, ATT_WIDTH), lambda i: (i, 0)), row, col(1), col(2), col(3), row,
                  whole((8, D_MODEL)), whole((ATT_WIDTH, D_MODEL)), whole((D_MODEL, D_MODEL)),
                  whole((D_MODEL, D_MODEL)), *loss_in[1]],
        out_specs=(row, row, row, row, row, *loss_out[1]),
        out_shape=(jax.ShapeDtypeStruct((S, D_MODEL), F32), bf, bf, bf, bf, *loss_out[0]),
        hosted=hosted)


def _out_bwd(dxn, mix, ya, yb, hl, rest, o_att, vec, w_pa, w_pb, w_o, hosted=None):
    S = dxn.shape[0]
    tm = OUT_TM
    forth = [_perm_matrix(r, tm) for r in DILATIONS[1:]]

    def body(dxn_ref, mix_ref, ya_ref, yb_ref, hl_ref, gl_ref, ma_ref, mb_ref, ga_ref, o_ref, v_ref,
             wpa_ref, wpb_ref, wo_ref, f1_ref, f2_ref,
             dout_ref, dya_ref, dyb_ref, dhl_ref, dp_ref, acc_ref,
             do0_ref, do1_ref, do2_ref, dv0_ref, dv1_ref, dv2_ref):
        i = pl.program_id(0)

        @pl.when(i == 0)
        def _():
            acc_ref[...] = jnp.zeros_like(acc_ref)

        g_post = v_ref[0:1, :]
        gate = v_ref[1:2, :]
        dxn_v = dxn_ref[...]
        out = _dot(mix_ref[...], wo_ref[...])
        rstd = lax.rsqrt(jnp.mean(out * out, axis=-1, keepdims=True) + NORM_EPS)
        nrm = out * rstd
        acc_ref[0:1, :] += jnp.sum(dxn_v * (nrm * g_post), axis=0, keepdims=True)
        acc_ref[1:2, :] += jnp.sum(dxn_v * gate * nrm, axis=0, keepdims=True)
        dn = dxn_v * (gate * g_post)
        dout = (rstd * (dn - nrm * jnp.mean(dn * nrm, axis=-1, keepdims=True))).astype(BF16)
        dout_ref[...] = dout
        dmix = _dot_nt(dout, wo_ref[...])
        sa = jax.nn.sigmoid(ma_ref[...])
        sb = jax.nn.sigmoid(mb_ref[...])
        dya = (dmix * sa).astype(BF16)
        dyb = (dmix * sb).astype(BF16)
        dya_ref[...] = dya
        dyb_ref[...] = dyb
        dma = dmix * ya_ref[...].astype(F32) * (sa * (1.0 - sa))
        dmb = dmix * yb_ref[...].astype(F32) * (sb * (1.0 - sb))
        d_ag = _dot_nt(dya, wpa_ref[...])
        d_bg = _dot_nt(dyb, wpb_ref[...])
        gl = gl_ref[...]
        hl_v = hl_ref[...]
        dhl_ref[...] = d_bg * _silu(gl)
        dgl = d_bg * hl_v * _dsilu(gl)
        ga = ga_ref[...]
        o = o_ref[...]
        do = d_ag * _silu(ga)
        dga = d_ag * o * _dsilu(ga)
        lane = lax.broadcasted_iota(jnp.int32, (tm, 128), 1)
        dvec = jnp.zeros((tm, 128), F32)
        for h in range(HEADS):
            sl = slice(h * HEAD_DIM, (h + 1) * HEAD_DIM)
            dvec = jnp.where(lane == h, jnp.sum(do[:, sl] * o[:, sl], axis=1, keepdims=True), dvec)
        do_b = do.astype(BF16)
        do0_ref[0] = do_b
        dv0_ref[0] = dvec
        for do_ref, dv_ref, f_ref in ((do1_ref, dv1_ref, f1_ref), (do2_ref, dv2_ref, f2_ref)):
            _store_residues(do_ref, _dot(f_ref[...], do_b).astype(BF16))
            _store_residues(dv_ref, _permute_f32(f_ref[...], dvec, 3))
        dp_ref[0] = dgl[:, 0:ATT_WIDTH].astype(BF16)
        dp_ref[1] = dgl[:, ATT_WIDTH:].astype(BF16)
        dp_ref[2] = dma[:, 0:ATT_WIDTH].astype(BF16)
        dp_ref[3] = dma[:, ATT_WIDTH:].astype(BF16)
        dp_ref[4] = dmb[:, 0:ATT_WIDTH].astype(BF16)
        dp_ref[5] = dmb[:, ATT_WIDTH:].astype(BF16)
        dp_ref[6] = dga.astype(BF16)

    def col(j):
        return pl.BlockSpec((tm, D_MODEL), lambda i: (i, j))

    def whole(shape):
        return pl.BlockSpec(shape, lambda i: (0, 0))

    def res_spec(r, width):
        return pl.BlockSpec((r, tm // r, width), lambda i: (0, i, 0))

    row = col(0)
    att = pl.BlockSpec((tm, ATT_WIDTH), lambda i: (i, 0))
    bf = jax.ShapeDtypeStruct((S, D_MODEL), BF16)
    vmem = pl.BlockSpec(memory_space=pltpu.VMEM)
    return _call(
        body, (dxn, mix, ya, yb, hl, rest, rest, rest, rest, o_att, vec, w_pa, w_pb, w_o, *forth),
        name="out_bwd", grid=(S // tm,),
        in_specs=[row, row, row, row, row, col(1), col(2), col(3),
                  pl.BlockSpec((tm, ATT_WIDTH), lambda i: (i, 8)), att,
                  whole((8, D_MODEL)), whole((ATT_WIDTH, D_MODEL)), whole((D_MODEL, D_MODEL)),
                  whole((D_MODEL, D_MODEL)), vmem, vmem],
        out_specs=(row, row, row, row,
                   pl.BlockSpec((7, tm, ATT_WIDTH), lambda i: (0, i, 0)), whole((8, D_MODEL)),
                   *[res_spec(r, ATT_WIDTH) for r in DILATIONS], *[res_spec(r, 128) for r in DILATIONS]),
        out_shape=(bf, bf, bf, jax.ShapeDtypeStruct((S, D_MODEL), F32),
                   jax.ShapeDtypeStruct((DREST_CHUNKS, S, ATT_WIDTH), BF16),
                   jax.ShapeDtypeStruct((8, D_MODEL), F32),
                   *[jax.ShapeDtypeStruct((r, S // r, ATT_WIDTH), BF16) for r in DILATIONS],
                   *[jax.ShapeDtypeStruct((r, S // r, 128), F32) for r in DILATIONS]),
        hosted=hosted)


def _adamw(gsrcs, w, m, v, n_stack, name, rows=None):
    n_layers, R, C = w.shape
    assert len(gsrcs) == n_layers
    budget = 96 * 1024
    tr = rows
    if tr is None:
        tr = R
        while tr * C > budget and tr % 16 == 0:
            tr //= 2
    assert R % tr == 0 and (tr % 8 == 0 or tr == R)
    c1 = 1.0 - ADAM_B1 ** ADAM_STEP
    c2 = 1.0 - ADAM_B2 ** ADAM_STEP

    def body(*refs):
        g_refs = refs[:n_layers]
        w_ref, m_ref, v_ref, go_ref, d_ref, mo_ref, vo_ref = refs[n_layers:]

        def update(g_ref):
            if n_stack:
                g = g_ref[0].astype(F32)
                for s in range(1, n_stack):
                    g = g + g_ref[s].astype(F32)
            else:
                g = g_ref[...]
            m_new = ADAM_B1 * m_ref[...] + (1.0 - ADAM_B1) * g
            v_new = ADAM_B2 * v_ref[...] + (1.0 - ADAM_B2) * (g * g)
            m_hat = m_new / c1
            v_hat = v_new / c2
            go_ref[...] = g
            d_ref[...] = -ADAM_LR * (m_hat / (jnp.sqrt(v_hat) + ADAM_EPS) + ADAM_WD * w_ref[...])
            mo_ref[...] = m_new
            vo_ref[...] = v_new

        for layer in range(n_layers):
            pl.when(pl.program_id(0) == layer)(functools.partial(update, g_refs[layer]))

    def g_spec(layer):
        def rows_of(l, i):
            return jnp.where(l == layer, i, 0)
        if n_stack:
            return pl.BlockSpec((n_stack, tr, C), lambda l, i: (0, rows_of(l, i), 0))
        return pl.BlockSpec((tr, C), lambda l, i: (rows_of(l, i), 0))

    blk = pl.BlockSpec((None, tr, C), lambda l, i: (l, i, 0))
    shp = jax.ShapeDtypeStruct((n_layers, R, C), F32)
    return pl.pallas_call(
        body, name=name, grid=(n_layers, R // tr),
        in_specs=[g_spec(layer) for layer in range(n_layers)] + [blk, blk, blk],
        out_specs=(blk, blk, blk, blk), out_shape=(shp, shp, shp, shp),
        compiler_params=_params(("arbitrary", "arbitrary")))(*gsrcs, w, m, v)


def _pair_blocks(w):
    w = w.reshape(8, 2, 64, 64)
    z = jnp.zeros((8, 64, 64), w.dtype)
    top = jnp.concatenate([w[:, 0], z], axis=2)
    bot = jnp.concatenate([z, w[:, 1]], axis=2)
    return jnp.concatenate([top, bot], axis=1).astype(BF16)


def _unpair_blocks(g):
    return jnp.stack([g[:, :64, :64], g[:, 64:, 64:]], axis=1).reshape(16, 64, 64)


def _mats_full(w_pa_g, w_pb_g, w_o_g):
    return dict(w_pa=jnp.transpose(w_pa_g, (1, 0, 2)).reshape(ATT_WIDTH, D_MODEL),
                w_pb=w_pb_g.reshape(D_MODEL, D_MODEL), w_o=w_o_g.reshape(D_MODEL, D_MODEL))


def _layer_params(layer, w_in_g, mats_g, conv_w_full, conv_b, b_rg, b_ig, lru_lambda, w_rg, w_ig, g_pre,
                  g_post):
    cw = jnp.concatenate([conv_w_full, conv_b[None], b_rg[None], b_ig[None], lru_lambda[None]], axis=0)
    p = dict(w_int=_weights_full(w_in_g), cw=cw, wr2=_pair_blocks(w_rg), wi2=_pair_blocks(w_ig),
             g_pre=g_pre, g_post=g_post, lru_lambda=lru_lambda)
    if mats_g is not None:
        p.update(_mats_full(*mats_g))
    return p


def _layer_fwd(x, mod, p, host_proj=None, host_lru=None, host_out=None, own_mats=None, target=None):
    zeros = jnp.zeros((5, D_MODEL), F32)
    vec_pre = jnp.concatenate([p["g_pre"][None], mod[1:2], mod[0:1], zeros], axis=0)
    (ht0, ht1, ht2, q0, q1, q2, rest), got_proj = _norm_proj_fwd(x, vec_pre, p["w_int"], hosted=host_proj)
    h_t = [ht0, ht1, ht2]
    hosts = [None, None, None]
    if own_mats is not None:
        hosts[0], hosts[1] = (list(own_mats[:2]), ["ag"] * 2), ([own_mats[2]], ["ag"])
    qkv, outs, stats, got_attn = [q0, q1, q2], [], [], []
    for g, r in enumerate(DILATIONS):
        (o_g, st_g), got = _attn_fwd(qkv[g], g, r, hosted=hosts[g])
        outs.append(o_g)
        stats.append(st_g)
        got_attn += got
    if own_mats is not None:
        p.update(_mats_full(*got_attn))
    o_att, a_gated, *lse = _attn_combine(outs, stats, rest)
    (hl, a_dec, *gates), got_lru = _lru_fwd(rest, p["cw"], p["wr2"], p["wi2"], hosted=host_lru)
    vec_post = jnp.concatenate([p["g_post"][None], mod[2:3], jnp.zeros((6, D_MODEL), F32)], axis=0)
    (x_new, mix, ya, yb, b_gated, *loss_acc), got_out = _out_fwd(
        a_gated, hl, rest, x, vec_post, p["w_pa"], p["w_pb"], p["w_o"], target=target, hosted=host_out)
    saved = dict(x=x, h_t=h_t, qkv=qkv, rest=rest, o_att=o_att, a_gated=a_gated, lse=lse, hl=hl, a_dec=a_dec,
                 gates=gates, mix=mix, ya=ya, yb=yb, b_gated=b_gated, vec_pre=vec_pre, vec_post=vec_post)
    return (x_new, *loss_acc), saved, got_proj, got_lru, got_out


def _mats_sources(g_pa, g_pb, g_o):
    return [jnp.transpose(g_pa.astype(BF16).reshape(ATT_WIDTH, N_DEV, 128), (1, 0, 2)),
            g_pb.astype(BF16).reshape(N_DEV, 128, D_MODEL), g_o.astype(BF16).reshape(N_DEV, 128, D_MODEL)]


LRU_SMALL = ("conv_b", "w_rg", "b_rg", "w_ig", "b_ig", "lru_lambda")
MOD_SMALL = ("b_mod", "g_pre", "g_post")


def _flat_rows(arrs):
    return jnp.concatenate([a.reshape(-1) for a in arrs]).reshape(-1, 128)


def _layer_bwd(dxn, sv, p, upper_small=None):
    a2a3 = ["a2a"] * 3
    (dout, dya, dyb, dhl, drest, acc_out, *do_dvec), _ = _out_bwd(
        dxn, sv["mix"], sv["ya"], sv["yb"], sv["hl"], sv["rest"], sv["o_att"], sv["vec_post"],
        p["w_pa"], p["w_pb"], p["w_o"])
    mats = _mats_sources(_wgrad_tn(sv["a_gated"], dya, "wgrad_pa"), _wgrad_tn(sv["b_gated"], dyb, "wgrad_pb"),
                         _wgrad_tn(sv["mix"], dout, "wgrad_o"))
    (drest, acc_lru, gwr2, gwi2), _ = _lru_bwd(
        dhl, sv["hl"], sv["a_dec"], sv["gates"], sv["rest"], p["cw"], p["wr2"], p["wi2"], drest)
    small = dict(
        conv_w=acc_lru[0:4], conv_b=acc_lru[4], b_rg=acc_lru[5], b_ig=acc_lru[6],
        lru_lambda=acc_lru[7] * (-jax.nn.sigmoid(-p["lru_lambda"])),
        w_rg=_unpair_blocks(gwr2), w_ig=_unpair_blocks(gwi2))
    hosts = [(mats, a2a3), None, None]
    if upper_small is not None:
        for half, names in enumerate((LRU_SMALL[:3], LRU_SMALL[3:] + ("conv_w",))):
            vec = _flat_rows([jnp.stack([small[n], upper_small[n]]) for n in names])
            hosts[1 + half] = ([vec], ["ag"])
    dqkv, got_attn = [], []
    for g, r in enumerate(DILATIONS):
        d_g, got = _attn_bwd(sv["qkv"][g], do_dvec[g], sv["lse"][g], do_dvec[3 + g], g, r, hosted=hosts[g])
        dqkv.append(d_g)
        got_attn.append(got)
    shards = None
    for g, r in enumerate(DILATIONS):
        shards = _proj_wgrad_group(sv["h_t"][g], dqkv[g], g, r, shards, f"proj_wgrad_g{g}")
    shards = _proj_wgrad_part(sv["h_t"][0], drest, 2, 4, QKV_CHUNKS, shards, "proj_wgrad_u")
    shards = _proj_wgrad_part(sv["h_t"][0], drest, 7, 0, QKV_CHUNKS + 2, shards, "proj_wgrad_rest")
    (dx, acc_pre), got_in = _proj_dgrad_norm_bwd(
        dqkv, drest, p["w_int"], sv["x"], dxn, sv["vec_pre"], hosted=([shards], ["a2a"]))
    small.update(dmod=jnp.concatenate([acc_pre[0], acc_pre[1], acc_out[0]]), g_pre=acc_pre[2],
                 g_post=acc_out[1])
    return dx, small, dict(mats=got_attn[0], w_in=got_in[0], lru_small=got_attn[1] + got_attn[2])


def kernel(x, c, w_mod, b_mod, g_pre, w_in, conv_w, conv_b, w_rg, b_rg, w_ig, b_ig, lru_lambda, w_pa, w_pb, w_o, g_post, loss_target, m_w_mod, m_b_mod, m_g_pre, m_w_in, m_conv_w, m_conv_b, m_w_rg, m_b_rg, m_w_ig, m_b_ig, m_lru_lambda, m_w_pa, m_w_pb, m_w_o, m_g_post, v_w_mod, v_b_mod, v_g_pre, v_w_in, v_conv_w, v_conv_b, v_w_rg, v_b_rg, v_w_ig, v_b_ig, v_lru_lambda, v_w_pa, v_w_pb, v_w_o, v_g_post):
    W = dict(w_mod=w_mod, b_mod=b_mod, g_pre=g_pre, w_in=w_in, conv_w=conv_w, conv_b=conv_b, w_rg=w_rg,
             b_rg=b_rg, w_ig=w_ig, b_ig=b_ig, lru_lambda=lru_lambda, w_pa=w_pa, w_pb=w_pb, w_o=w_o,
             g_post=g_post)
    M = dict(w_mod=m_w_mod, b_mod=m_b_mod, g_pre=m_g_pre, w_in=m_w_in, conv_w=m_conv_w, conv_b=m_conv_b,
             w_rg=m_w_rg, b_rg=m_b_rg, w_ig=m_w_ig, b_ig=m_b_ig, lru_lambda=m_lru_lambda, w_pa=m_w_pa,
             w_pb=m_w_pb, w_o=m_w_o, g_post=m_g_post)
    V = dict(w_mod=v_w_mod, b_mod=v_b_mod, g_pre=v_g_pre, w_in=v_w_in, conv_w=v_conv_w, conv_b=v_conv_b,
             w_rg=v_w_rg, b_rg=v_b_rg, w_ig=v_w_ig, b_ig=v_b_ig, lru_lambda=v_lru_lambda, w_pa=v_w_pa,
             w_pb=v_w_pb, w_o=v_w_o, g_post=v_g_post)
    S = x.shape[1]
    me = 4 * lax.axis_index("x") + 2 * lax.axis_index("y") + lax.axis_index("c")
    n_mod = w_mod.shape[2]
    n_in = w_in.shape[2]
    n_conv = conv_w.shape[2]

    c_rows = jnp.broadcast_to(c, (8, D_MODEL))
    w_in_b, w_pa_b, w_pb_b, w_o_b = (t.astype(BF16) for t in (w_in, w_pa, w_pb, w_o))
    g_c, g_win0, g_cw = _gather_two_level([c_rows, w_in_b[0], conv_w], "gather_weights")
    c_all = g_c[:, 0, :]
    c_pad = jnp.concatenate([c_all, jnp.zeros((8, D_MODEL), F32)], axis=0)
    conv_w_full = jnp.transpose(g_cw, (1, 2, 0, 3)).reshape(2, CONV_WIDTH, D_MODEL)

    mod_cols = _mod_fwd(c_pad, w_mod)
    mod_src = jnp.transpose(mod_cols[:, :8], (1, 0, 2))
    mod_src = jnp.concatenate([mod_src, jnp.zeros((8, 6, n_mod), F32)], axis=1)
    (mod_got,) = _exchange([mod_src], ["a2a"], "scatter_mod")
    mod = jnp.transpose(mod_got[:, :2], (1, 0, 2)).reshape(2, 3 * D_MODEL) + b_mod

    def layer_params(layer, w_in_g, mats_g):
        return _layer_params(layer, w_in_g, mats_g, conv_w_full[layer], conv_b[layer], b_rg[layer],
                             b_ig[layer], lru_lambda[layer], w_rg[layer], w_ig[layer], g_pre[layer],
                             g_post[layer])

    layers = [layer_params(0, g_win0, None), None]
    (act,), sv0, (g_win1,), _, mats1_g = _layer_fwd(
        x[0], mod[0].reshape(3, D_MODEL), layers[0], own_mats=(w_pa_b[0], w_pb_b[0], w_o_b[0]),
        host_proj=([w_in_b[1]], ["ag"]), host_out=([w_pa_b[1], w_pb_b[1], w_o_b[1]], ["ag"] * 3))
    layers[1] = layer_params(1, g_win1, mats1_g)
    (dy, loss_acc), sv1, _, _, _ = _layer_fwd(act, mod[1].reshape(3, D_MODEL), layers[1],
                                              target=loss_target[0])

    dy, small1, got1 = _layer_bwd(dy, sv1, layers[1])
    dy, small0, got0 = _layer_bwd(dy, sv0, layers[0], upper_small=small1)
    grad_x = dy[None]
    r_pa, r_pb, r_o = ([got0["mats"][k], got1["mats"][k]] for k in range(3))
    r_in = [got0["w_in"], got1["w_in"]]
    grads = [small0, small1]

    def stack2(name):
        return jnp.stack([grads[0][name], grads[1][name]], axis=0)

    dmod = stack2("dmod")
    dmod_src = jnp.transpose(dmod.reshape(2, N_DEV, n_mod), (1, 0, 2))
    dmod_src = jnp.concatenate([dmod_src, jnp.zeros((8, 6, n_mod), F32)], axis=1)
    mod_vec = _flat_rows([dmod, stack2("g_pre"), stack2("g_post"), loss_acc])
    r_dmod, r_mod_small = _exchange([dmod_src, mod_vec], ["a2a", "ag"], "exchange_grads")
    mod_rows = mod_vec.shape[0] - loss_acc.shape[0]
    loss = 0.5 * jnp.sum(r_mod_small[:, mod_rows, 0]) / D_MODEL
    r_lru_small = got0["lru_small"]

    res = {}
    res["w_in"] = _adamw(r_in, w_in, m_w_in, v_w_in, 8, "adamw_w_in")
    res["w_pa"] = _adamw(r_pa, w_pa, m_w_pa, v_w_pa, 8, "adamw_w_pa")
    res["w_pb"] = _adamw(r_pb, w_pb, m_w_pb, v_w_pb, 8, "adamw_w_pb")
    res["w_o"] = _adamw(r_o, w_o, m_w_o, v_w_o, 8, "adamw_w_o")
    dmod_all = jnp.transpose(r_dmod[:, :2], (1, 0, 2))
    gw_mod = _mod_wgrad(jnp.transpose(c_all), dmod_all)
    res["w_mod"] = _adamw([gw_mod[0], gw_mod[1]], w_mod, m_w_mod, v_w_mod, 0, "adamw_w_mod")
    for names_rep, stack, tag in ((LRU_SMALL[:3], r_lru_small[0], "lru_a"), (LRU_SMALL[3:], r_lru_small[1], "lru_b"),
                                  (MOD_SMALL, r_mod_small, "mod")):
        rows = sum(W[n].size for n in names_rep) // 128
        rep = _adamw([stack], *(_flat_rows([src[n] for n in names_rep])[None] for src in (W, M, V)), 8,
                     f"adamw_small_{tag}", rows=rows // 2 if rows % 16 == 0 else rows)
        off = 0
        for name in names_rep:
            size = W[name].size
            res[name] = tuple(t.reshape(-1)[off:off + size].reshape(W[name].shape) for t in rep)
            off += size
    lru_rows = sum(W[n].size for n in LRU_SMALL[3:]) // 128
    conv_stack = r_lru_small[1][:, lru_rows:].reshape(8, 2, CONV_WIDTH, D_MODEL)
    conv_stack = lax.dynamic_slice_in_dim(conv_stack, me * n_conv, n_conv, axis=3).reshape(8, 8, n_conv)
    res["conv_w"] = _adamw([conv_stack], conv_w.reshape(1, 8, n_conv), m_conv_w.reshape(1, 8, n_conv),
                           v_conv_w.reshape(1, 8, n_conv), 8, "adamw_conv_w")

    names = ("w_mod", "b_mod", "g_pre", "w_in", "conv_w", "conv_b", "w_rg", "b_rg", "w_ig", "b_ig",
             "lru_lambda", "w_pa", "w_pb", "w_o", "g_post")
    outs = [loss, grad_x]
    for k in range(4):
        outs.extend(res[n][k].reshape(W[n].shape) for n in names)
    return tuple(outs)
```

```python
import functools

import jax
import jax.numpy as jnp
import numpy as np
from jax import lax
from jax.experimental import pallas as pl
from jax.experimental.pallas import tpu as pltpu

F32 = jnp.float32
BF16 = jnp.bfloat16

N_DEV = 8
D_MODEL = 1024
HEAD_DIM = 128
HEADS = 4
ATT_WIDTH = HEADS * HEAD_DIM
DILATIONS = (1, 4, 16)
BAND = 128
N_CHUNKS = 18
QKV_CHUNKS = 9
CONV_WIDTH = 4
LRU_C = 8.0
NORM_EPS = 1e-6
NEG_INF = -1e30
ADAM_LR = 0.001
ADAM_B1 = 0.9
ADAM_B2 = 0.999
ADAM_EPS = 1e-08
ADAM_WD = 0.01
ADAM_STEP = 10
CHUNK_PERM = (0, 3, 6, 1, 4, 7, 2, 5, 8, 10, 11, 12, 13, 14, 15, 16, 17, 9)
DREST_CHUNKS = 10
VMEM_LIMIT = 56 * 1024 * 1024


def _params(sem=None):
    return pltpu.CompilerParams(dimension_semantics=sem, vmem_limit_bytes=VMEM_LIMIT)


def _silu(x):
    return x * jax.nn.sigmoid(x)


def _dsilu(x):
    s = jax.nn.sigmoid(x)
    return s * (1.0 + x * (1.0 - s))


def _neg_expm1(x):
    series = -x * (1.0 + x * (0.5 + x * (1.0 / 6.0 + x * (1.0 / 24.0))))
    return jnp.where(x > -0.05, series, 1.0 - jnp.exp(x))


def _softplus_neg(lam):
    z = jnp.exp(-jnp.abs(lam))
    small = z * (1.0 - z * (0.5 - z * (1.0 / 3.0 - z * 0.25)))
    log1p_z = jnp.where(z < 1e-2, small, jnp.log(1.0 + z))
    return jnp.maximum(-lam, 0.0) + log1p_z


def _dot(a, b):
    return jnp.dot(a, b, preferred_element_type=F32)


def _dot_nt(a, b):
    return lax.dot_general(a, b, (((1,), (1,)), ((), ())), preferred_element_type=F32)


def _dot_tn(a, b):
    return lax.dot_general(a, b, (((0,), (0,)), ((), ())), preferred_element_type=F32)


def _perm_matrix(r, rows, transpose=False):
    n = rows // r
    p = np.zeros((rows, rows), np.float32)
    dst = np.arange(rows)
    p[dst, (dst % n) * r + dst // n] = 1.0
    return jnp.asarray(p.T if transpose else p, dtype=BF16)


def _permute_f32(p, x, pieces):
    part = x.astype(BF16)
    acc = _dot(p, part)
    for _ in range(pieces - 1):
        x = x - part.astype(F32)
        part = x.astype(BF16)
        acc = acc + _dot(p, part)
    return acc


def _rows_of_residues(ref, lead=()):
    r = ref.shape[len(lead)]
    if r == 1:
        return ref[lead + (0,)]
    return jnp.concatenate([ref[lead + (rho,)] for rho in range(r)], axis=0)


def _store_residues(ref, x, lead=()):
    r = ref.shape[len(lead)]
    n = x.shape[0] // r
    for rho in range(r):
        ref[lead + (rho,)] = x[rho * n:(rho + 1) * n]


ANY_SPEC = pl.BlockSpec(memory_space=pl.ANY)


def _exchange_shapes(arrs, modes):
    return [jax.ShapeDtypeStruct(((N_DEV,) + a.shape) if mode == "ag" else a.shape, a.dtype)
            for a, mode in zip(arrs, modes)]


def _exchange_sems(n):
    return [pltpu.SemaphoreType.DMA((7 * n,)), pltpu.SemaphoreType.DMA((7 * n,)),
            pltpu.SemaphoreType.DMA((n,))]


def _exchange_copies(ins, outs, modes, sems):
    n = len(ins)
    send_sems, recv_sems, local_sems = sems
    x, y, c = lax.axis_index("x"), lax.axis_index("y"), lax.axis_index("c")
    me = 4 * x + 2 * y + c

    def src_for(a, dev):
        return ins[a] if modes[a] == "ag" else ins[a].at[dev]

    local = [pltpu.make_async_copy(src_for(a, me), outs[a].at[me], local_sems.at[a]) for a in range(n)]
    sends, arrivals = [], []
    for k in range(1, N_DEV):
        px = 1 - x if (k >> 2) & 1 else x
        py = 1 - y if (k >> 1) & 1 else y
        pc = 1 - c if k & 1 else c
        peer = 4 * px + 2 * py + pc
        for a in range(n):
            s = (k - 1) * n + a
            for dst, group in ((me, sends), (peer, arrivals)):
                group.append(pltpu.make_async_remote_copy(
                    src_ref=src_for(a, peer), dst_ref=outs[a].at[dst],
                    send_sem=send_sems.at[s], recv_sem=recv_sems.at[s],
                    device_id=(px, py, pc), device_id_type=pl.DeviceIdType.MESH))
    return local, sends, arrivals


def _exchange_start(copies):
    local, sends, _ = copies
    for cp in local + sends:
        cp.start()


def _exchange_wait(copies):
    local, sends, arrivals = copies
    for cp in arrivals:
        cp.wait_recv()
    for cp in sends:
        cp.wait_send()
    for cp in local:
        cp.wait()


def _exchange(arrs, modes, name):
    n = len(arrs)

    def body(*refs):
        copies = _exchange_copies(refs[:n], refs[n:2 * n], modes, refs[2 * n:])
        _exchange_start(copies)
        _exchange_wait(copies)

    outs = pl.pallas_call(
        body, name=name, out_shape=tuple(_exchange_shapes(arrs, modes)),
        in_specs=[ANY_SPEC] * n, out_specs=tuple([ANY_SPEC] * n),
        scratch_shapes=_exchange_sems(n),
    )(*arrs)
    return list(outs)


def _gather_two_level(arrs, name):
    n = len(arrs)

    def body(*refs):
        ins, outs = refs[:n], refs[n:2 * n]
        send_sems, recv_sems, local_sems = refs[2 * n:]
        x, y, c = lax.axis_index("x"), lax.axis_index("y"), lax.axis_index("c")
        me, sibling = (x, y, c), (x, y, 1 - c)
        chips = [(1 - x, y), (x, 1 - y), (1 - x, 1 - y)]

        def copy(a, k, block, to, src=None):
            slot = 4 * block[0] + 2 * block[1] + block[2]
            return pltpu.make_async_remote_copy(
                src_ref=outs[a].at[slot] if src is None else src, dst_ref=outs[a].at[slot],
                send_sem=send_sems.at[7 * a + k], recv_sem=recv_sems.at[7 * a + k],
                device_id=to, device_id_type=pl.DeviceIdType.MESH)

        mine = [pltpu.make_async_copy(ins[a], outs[a].at[4 * x + 2 * y + c], local_sems.at[a])
                for a in range(n)]
        first = []
        for a in range(n):
            first.append(copy(a, 0, me, sibling, src=ins[a]))
            first += [copy(a, 1 + j, me, (*chip, c), src=ins[a]) for j, chip in enumerate(chips)]
        for cp in mine + first:
            cp.start()
        passed = []
        for j, chip in enumerate(chips):
            for a in range(n):
                copy(a, 1 + j, (*chip, c), me).wait_recv()
                passed.append(copy(a, 4 + j, (*chip, c), sibling))
                passed[-1].start()
        for a in range(n):
            copy(a, 0, sibling, me).wait_recv()
            for j, chip in enumerate(chips):
                copy(a, 4 + j, (*chip, 1 - c), me).wait_recv()
        for cp in first + passed:
            cp.wait_send()
        for cp in mine:
            cp.wait()

    outs = pl.pallas_call(
        body, name=name, out_shape=tuple(_exchange_shapes(arrs, ["ag"] * n)),
        in_specs=[ANY_SPEC] * n, out_specs=tuple([ANY_SPEC] * n),
        scratch_shapes=_exchange_sems(n),
    )(*arrs)
    return list(outs)


SHARD_COLS = N_CHUNKS * ATT_WIDTH // N_DEV


def _weights_full(shards):
    return jnp.transpose(shards, (1, 0, 2)).reshape(D_MODEL, N_CHUNKS * ATT_WIDTH)


def _wcols(w_ref, k):
    orig = CHUNK_PERM[k]
    return w_ref[:, orig * ATT_WIDTH:(orig + 1) * ATT_WIDTH]


def _store_shard_columns(stage_ref, shards_ref, sems, first_chunk):
    copies = []
    for t in range(stage_ref.shape[1] // ATT_WIDTH):
        col = ATT_WIDTH * CHUNK_PERM[first_chunk + t]
        end = col + ATT_WIDTH
        while col < end:
            dev = col // SHARD_COLS
            upto = min(end, (dev + 1) * SHARD_COLS)
            src = stage_ref.at[:, pl.ds(t * ATT_WIDTH + ATT_WIDTH - (end - col), upto - col)]
            dst = shards_ref.at[dev, :, pl.ds(col - dev * SHARD_COLS, upto - col)]
            copies.append(pltpu.make_async_copy(src, dst, sems.at[len(copies)]))
            col = upto
    for cp in copies:
        cp.start()
    for cp in copies:
        cp.wait()


SHARD_COPIES_PER_CHUNK = 2


def _call(body, args, *, name, grid, in_specs, out_specs, out_shape, scratch_shapes=(), aliases=None,
          hosted=None):
    sem = ("arbitrary",) * len(grid)
    if hosted is None:
        outs = pl.pallas_call(
            body, name=name, grid=grid, in_specs=in_specs, out_specs=tuple(out_specs),
            out_shape=tuple(out_shape), scratch_shapes=list(scratch_shapes),
            input_output_aliases=aliases or {}, compiler_params=_params(sem))(*args)
        return list(outs), []
    x_arrs, modes = hosted
    n_in, n_out, n_scr, nx = len(args), len(out_shape), len(scratch_shapes), len(x_arrs)

    def wrapped(*refs):
        ins, x_ins = refs[:n_in], refs[n_in:n_in + nx]
        outs = refs[n_in + nx:n_in + nx + n_out]
        x_outs = refs[n_in + nx + n_out:n_in + 2 * nx + n_out]
        scr = refs[n_in + 2 * nx + n_out:n_in + 2 * nx + n_out + n_scr]
        sems = refs[n_in + 2 * nx + n_out + n_scr:]
        first = pl.program_id(0) == 0
        last = pl.program_id(0) == grid[0] - 1
        for axis in range(1, len(grid)):
            first = jnp.logical_and(first, pl.program_id(axis) == 0)
            last = jnp.logical_and(last, pl.program_id(axis) == grid[axis] - 1)

        @pl.when(first)
        def _():
            _exchange_start(_exchange_copies(x_ins, x_outs, modes, sems))

        body(*ins, *outs, *scr)

        @pl.when(last)
        def _():
            _exchange_wait(_exchange_copies(x_ins, x_outs, modes, sems))

    outs = pl.pallas_call(
        wrapped, name=name, grid=grid, in_specs=list(in_specs) + [ANY_SPEC] * nx,
        out_specs=tuple(out_specs) + tuple([ANY_SPEC] * nx),
        out_shape=tuple(out_shape) + tuple(_exchange_shapes(x_arrs, modes)),
        scratch_shapes=list(scratch_shapes) + _exchange_sems(nx),
        input_output_aliases=aliases or {}, compiler_params=_params(sem))(*args, *x_arrs)
    return list(outs[:n_out]), list(outs[n_out:])


def _mod_fwd(c_pad, w_mod):
    def body(c_ref, w_ref, o_ref):
        sc = _silu(c_ref[...]).astype(BF16)
        for layer in range(2):
            o_ref[layer] = _dot(sc, w_ref[layer].astype(BF16))

    return pl.pallas_call(
        body, name="mod_fwd", out_shape=jax.ShapeDtypeStruct((2, 16, w_mod.shape[2]), F32),
        compiler_params=_params())(c_pad, w_mod)


def _mod_wgrad(c_t, dmod):
    n_cols = dmod.shape[2]

    def body(c_ref, d_ref, o_ref):
        sc = _silu(c_ref[...]).astype(BF16).astype(F32)
        for layer in range(2):
            dm = d_ref[layer].astype(BF16).astype(F32)
            acc = sc[:, 0:1] * dm[0:1, :]
            for b in range(1, N_DEV):
                acc = acc + sc[:, b:b + 1] * dm[b:b + 1, :]
            o_ref[layer] = acc

    return pl.pallas_call(
        body, name="mod_wgrad", out_shape=jax.ShapeDtypeStruct((2, D_MODEL, n_cols), F32),
        compiler_params=_params())(c_t, dmod)


GROUP_COLS = 3 * ATT_WIDTH
PROJ_TM = 256


def _norm_proj_fwd(x, vec, w_int, hosted=None):
    S = x.shape[0]
    tm = PROJ_TM
    perms = [_perm_matrix(r, tm) for r in DILATIONS[1:]]

    def body(x_ref, v_ref, w_ref, p1_ref, p2_ref, ht0_ref, ht1_ref, ht2_ref, q0_ref, q1_ref, q2_ref,
             rest_ref):
        xv = x_ref[...]
        rstd = lax.rsqrt(jnp.mean(xv * xv, axis=-1, keepdims=True) + NORM_EPS)
        hf = ((xv * rstd) * v_ref[0:1, :]) * (1.0 + v_ref[1:2, :]) + v_ref[2:3, :]
        h = hf.astype(BF16)
        for g, (q_ref, ht_ref, p_ref) in enumerate(((q0_ref, ht0_ref, None), (q1_ref, ht1_ref, p1_ref),
                                                    (q2_ref, ht2_ref, p2_ref))):
            rows_f = hf if p_ref is None else _dot(p_ref[...], h)
            ht_ref[...] = rows_f.T.astype(BF16)
            rows = rows_f.astype(BF16)
            n = tm // q_ref.shape[0]
            for t in range(3):
                res = _dot(rows, _wcols(w_ref, 3 * g + t)).astype(BF16)
                for rho in range(q_ref.shape[0]):
                    q_ref[rho, :, t * ATT_WIDTH:(t + 1) * ATT_WIDTH] = res[rho * n:(rho + 1) * n]
        split = (QKV_CHUNKS - 1) * ATT_WIDTH
        rest_ref[:, :split] = _dot(h, w_ref[:, (QKV_CHUNKS + 1) * ATT_WIDTH:])
        rest_ref[:, split:] = _dot(h, w_ref[:, QKV_CHUNKS * ATT_WIDTH:(QKV_CHUNKS + 1) * ATT_WIDTH])

    n_rest = QKV_CHUNKS * ATT_WIDTH
    whole = pl.BlockSpec(memory_space=pltpu.VMEM)
    ht_spec = pl.BlockSpec((D_MODEL, tm), lambda i: (0, i))
    ht_shape = jax.ShapeDtypeStruct((D_MODEL, S), BF16)
    return _call(
        body, (x, vec, w_int, *perms), name="norm_proj_fwd", grid=(S // tm,),
        in_specs=[pl.BlockSpec((tm, D_MODEL), lambda i: (i, 0)),
                  pl.BlockSpec((8, D_MODEL), lambda i: (0, 0)), whole, whole, whole],
        out_specs=(ht_spec, ht_spec, ht_spec,
                   *[pl.BlockSpec((r, tm // r, GROUP_COLS), lambda i: (0, i, 0)) for r in DILATIONS],
                   pl.BlockSpec((tm, n_rest), lambda i: (i, 0))),
        out_shape=(ht_shape, ht_shape, ht_shape,
                   *[jax.ShapeDtypeStruct((r, S // r, GROUP_COLS), BF16) for r in DILATIONS],
                   jax.ShapeDtypeStruct((S, n_rest), F32)),
        hosted=hosted)


def _proj_dgrad_norm_bwd(dqkv, drest, w_int, x, dxn, vec, hosted=None):
    S = x.shape[0]
    tm = PROJ_TM
    perms = [_perm_matrix(r, tm, transpose=True) for r in DILATIONS[1:]]

    def body(d0_ref, d1_ref, d2_ref, dr_ref, w_ref, p1_ref, p2_ref, x_ref, dxn_ref, v_ref, dx_ref, acc_ref):
        i = pl.program_id(0)

        @pl.when(i == 0)
        def _():
            acc_ref[...] = jnp.zeros_like(acc_ref)

        wcols = functools.partial(_wcols, w_ref)
        dhv = None
        for g, (d_ref, p_ref) in enumerate(((d0_ref, None), (d1_ref, p1_ref), (d2_ref, p2_ref))):
            for t in range(3):
                d = _rows_of_residues(d_ref, (t,))
                if p_ref is not None:
                    d = _dot(p_ref[...], d).astype(BF16)
                part = _dot_nt(d, wcols(3 * g + t))
                dhv = part if dhv is None else dhv + part
        for t in range(7):
            dhv = dhv + _dot_nt(dr_ref[t], wcols(11 + t))
        for t in range(2):
            dhv = dhv + _dot_nt(dr_ref[8 + t], wcols(9 + t))

        xv = x_ref[...]
        g = v_ref[0:1, :]
        sc1 = 1.0 + v_ref[1:2, :]
        rstd = lax.rsqrt(jnp.mean(xv * xv, axis=-1, keepdims=True) + NORM_EPS)
        xhat = xv * rstd
        acc_ref[0:1, :] += jnp.sum(dhv, axis=0, keepdims=True)
        acc_ref[1:2, :] += jnp.sum(dhv * (xhat * g), axis=0, keepdims=True)
        acc_ref[2:3, :] += jnp.sum(dhv * xhat * sc1, axis=0, keepdims=True)
        dxhat = dhv * (g * sc1)
        dx = rstd * (dxhat - xhat * jnp.mean(dxhat * xhat, axis=-1, keepdims=True))
        dx_ref[...] = dx + dxn_ref[...]

    row = pl.BlockSpec((tm, D_MODEL), lambda i: (i, 0))
    vec_spec = pl.BlockSpec((8, D_MODEL), lambda i: (0, 0))
    whole = pl.BlockSpec(memory_space=pltpu.VMEM)
    return _call(
        body, (*dqkv, drest, w_int, *perms, x, dxn, vec), name="proj_dgrad_norm_bwd", grid=(S // tm,),
        in_specs=[*[pl.BlockSpec((3, r, tm // r, ATT_WIDTH), lambda i: (0, 0, i, 0)) for r in DILATIONS],
                  pl.BlockSpec((DREST_CHUNKS, tm, ATT_WIDTH), lambda i: (0, i, 0)),
                  whole, whole, whole, row, row, vec_spec],
        out_specs=(row, vec_spec),
        out_shape=(jax.ShapeDtypeStruct((S, D_MODEL), F32), jax.ShapeDtypeStruct((8, D_MODEL), F32)),
        hosted=hosted)


WGRAD_TS = 512


def _proj_wgrad_call(h_t, d, d_spec, rows_of, n_chunks, first_chunk, shards, name):
    S = h_t.shape[1]
    n_steps = S // WGRAD_TS
    width = n_chunks * ATT_WIDTH

    def body(*refs):
        h_ref, d_ref = refs[:2]
        shards_ref, acc_ref, stage_ref, sems = refs[-4:]
        i = pl.program_id(0)

        @pl.when(i == 0)
        def _():
            acc_ref[...] = jnp.zeros_like(acc_ref)

        ht = h_ref[...]
        for t in range(n_chunks):
            acc_ref[:, t * ATT_WIDTH:(t + 1) * ATT_WIDTH] += _dot(ht, rows_of(d_ref, t))

        @pl.when(i == n_steps - 1)
        def _():
            stage_ref[...] = acc_ref[...].astype(BF16)
            _store_shard_columns(stage_ref, shards_ref, sems, first_chunk)

    chained = shards is not None
    return pl.pallas_call(
        body, name=name, grid=(n_steps,),
        in_specs=[pl.BlockSpec((D_MODEL, WGRAD_TS), lambda i: (0, i)), d_spec] + [ANY_SPEC] * chained,
        out_specs=ANY_SPEC,
        out_shape=jax.ShapeDtypeStruct((N_DEV, D_MODEL, SHARD_COLS), BF16),
        scratch_shapes=[pltpu.VMEM((D_MODEL, width), F32), pltpu.VMEM((D_MODEL, width), BF16),
                        pltpu.SemaphoreType.DMA((SHARD_COPIES_PER_CHUNK * n_chunks,))],
        input_output_aliases={2: 0} if chained else {},
        compiler_params=_params(("arbitrary",)))(h_t, d, *([shards] * chained))


def _proj_wgrad_part(h_t, d, n_chunks, chunk_block, first_chunk, shards, name):
    spec = pl.BlockSpec((n_chunks, WGRAD_TS, ATT_WIDTH), lambda i: (chunk_block, i, 0))
    return _proj_wgrad_call(h_t, d, spec, lambda d_ref, t: d_ref[t], n_chunks, first_chunk, shards, name)


def _proj_wgrad_group(h_t, d, g, r, shards, name):
    n = PROJ_TM // r

    def rows_of(d_ref, t):
        if r == 1:
            return d_ref[t, 0]
        return jnp.concatenate([d_ref[t, rho, s * n:(s + 1) * n] for s in range(WGRAD_TS // PROJ_TM)
                                for rho in range(r)], axis=0)

    spec = pl.BlockSpec((3, r, WGRAD_TS // r, ATT_WIDTH), lambda i: (0, 0, i, 0))
    return _proj_wgrad_call(h_t, d, spec, rows_of, 3, 3 * g, shards, name)


def _wgrad_tn(a, b, name):
    S, M = a.shape
    N = b.shape[1]
    ts = 1024

    def body(a_ref, b_ref, o_ref):
        i = pl.program_id(0)
        part = _dot_tn(a_ref[...], b_ref[...])

        @pl.when(i == 0)
        def _():
            o_ref[...] = part

        @pl.when(i > 0)
        def _():
            o_ref[...] += part

    return pl.pallas_call(
        body, name=name, grid=(S // ts,),
        in_specs=[pl.BlockSpec((ts, M), lambda i: (i, 0)), pl.BlockSpec((ts, N), lambda i: (i, 0))],
        out_specs=pl.BlockSpec((M, N), lambda i: (0, 0)),
        out_shape=jax.ShapeDtypeStruct((M, N), F32),
        compiler_params=_params(("arbitrary",)))(a, b)


ATTN_FWD_BLOCKS = 8


def _attn_fwd(qkv, g, r, hosted=None):
    L = qkv.shape[1]
    qb = min(ATTN_FWD_BLOCKS, L // BAND)
    n_steps = L // (qb * BAND)
    scale = HEAD_DIM ** -0.5

    def body(q_ref, kp_ref, kc_ref, vp_ref, vc_ref, o_ref, st_ref):
        m_step = pl.program_id(1)
        ii = lax.broadcasted_iota(jnp.int32, (BAND, 2 * BAND), 0)
        kk = lax.broadcasted_iota(jnp.int32, (BAND, 2 * BAND), 1)
        band = jnp.logical_and(kk >= ii, kk <= ii + BAND)
        first_mask = jnp.logical_and(band, jnp.logical_or(kk >= BAND, m_step > 0))
        lane = lax.broadcasted_iota(jnp.int32, (BAND, 128), 1)
        stats = [jnp.zeros((BAND, 128), F32) for _ in range(qb)]
        for h in range(HEADS):
            sl = slice(h * HEAD_DIM, (h + 1) * HEAD_DIM)
            kc, vc = kc_ref[:, sl], vc_ref[:, sl]
            for b in range(qb):
                rows = slice(b * BAND, (b + 1) * BAND)
                if b == 0:
                    keys = jnp.concatenate([kp_ref[:, sl], kc[:BAND]], axis=0)
                    vals = jnp.concatenate([vp_ref[:, sl], vc[:BAND]], axis=0)
                else:
                    keys, vals = kc[(b - 1) * BAND:(b + 1) * BAND], vc[(b - 1) * BAND:(b + 1) * BAND]
                s = jnp.where(first_mask if b == 0 else band, _dot_nt(q_ref[rows, sl], keys) * scale, NEG_INF)
                m = jnp.max(s, axis=1, keepdims=True)
                p = jnp.exp(s - m)
                l = jnp.sum(p, axis=1, keepdims=True)
                o_ref[rows, sl] = _dot(p.astype(BF16), vals) / l
                stats[b] = jnp.where(lane == h, m, stats[b])
                stats[b] = jnp.where(lane == HEADS + h, l, stats[b])
        for b in range(qb):
            st_ref[b * BAND:(b + 1) * BAND, :] = stats[b]

    many = (None, qb * BAND, ATT_WIDTH)
    one = (None, BAND, ATT_WIDTH)

    def prev(m):
        return jnp.maximum(qb * m - 1, 0)

    return _call(
        body, (qkv, qkv, qkv, qkv, qkv), name=f"attn_fwd_g{g}", grid=(r, n_steps),
        in_specs=[
            pl.BlockSpec(many, lambda rho, m: (rho, m, 0)),
            pl.BlockSpec(one, lambda rho, m: (rho, prev(m), 1)),
            pl.BlockSpec(many, lambda rho, m: (rho, m, 1)),
            pl.BlockSpec(one, lambda rho, m: (rho, prev(m), 2)),
            pl.BlockSpec(many, lambda rho, m: (rho, m, 2)),
        ],
        out_specs=(pl.BlockSpec(many, lambda rho, m: (rho, m, 0)),
                   pl.BlockSpec((None, qb * BAND, 128), lambda rho, m: (rho, m, 0))),
        out_shape=(jax.ShapeDtypeStruct((r, L, ATT_WIDTH), F32),
                   jax.ShapeDtypeStruct((r, L, 128), F32)),
        hosted=hosted)


def _attn_combine(outs, stats, rest):
    S = rest.shape[0]
    tm = 256
    gatt_blk = 8
    back = [_perm_matrix(r, tm, transpose=True) for r in DILATIONS[1:]]
    forth = [_perm_matrix(r, tm) for r in DILATIONS[1:]]

    def body(o0_ref, o1_ref, o2_ref, s0_ref, s1_ref, s2_ref, g_ref, b1_ref, b2_ref, f1_ref, f2_ref,
             o_ref, a_ref, l0_ref, l1_ref, l2_ref):
        outs_nat = [o0_ref[0]] + [_permute_f32(b_ref[...], _rows_of_residues(o_g), 2)
                                  for o_g, b_ref in ((o1_ref, b1_ref), (o2_ref, b2_ref))]
        st = [s0_ref[0]] + [_permute_f32(b_ref[...], _rows_of_residues(s_g), 3)
                            for s_g, b_ref in ((s1_ref, b1_ref), (s2_ref, b2_ref))]
        lane = lax.broadcasted_iota(jnp.int32, (tm, 128), 1)
        lse_out = jnp.zeros((tm, 128), F32)
        for h in range(HEADS):
            sl = slice(h * HEAD_DIM, (h + 1) * HEAD_DIM)
            ms = [s[:, h:h + 1] for s in st]
            ls = [s[:, HEADS + h:HEADS + h + 1] for s in st]
            m_all = jnp.maximum(jnp.maximum(ms[0], ms[1]), ms[2])
            ws = [l * jnp.exp(m - m_all) for m, l in zip(ms, ls)]
            den = ws[0] + ws[1] + ws[2]
            o = (ws[0] * outs_nat[0][:, sl] + ws[1] * outs_nat[1][:, sl] + ws[2] * outs_nat[2][:, sl]) / den
            o_ref[:, sl] = o
            a_ref[:, sl] = (o * _silu(g_ref[:, sl])).astype(BF16)
            lse_out = jnp.where(lane == h, m_all + jnp.log(den), lse_out)
        l0_ref[0] = lse_out
        for l_ref, f_ref in ((l1_ref, f1_ref), (l2_ref, f2_ref)):
            _store_residues(l_ref, _permute_f32(f_ref[...], lse_out, 3))

    o_spec = pl.BlockSpec((tm, ATT_WIDTH), lambda i: (i, 0))
    whole = pl.BlockSpec(memory_space=pltpu.VMEM)

    def res_spec(r, width):
        return pl.BlockSpec((r, tm // r, width), lambda i: (0, i, 0))

    return pl.pallas_call(
        body, name="attn_combine", grid=(S // tm,),
        in_specs=[*[res_spec(r, ATT_WIDTH) for r in DILATIONS], *[res_spec(r, 128) for r in DILATIONS],
                  pl.BlockSpec((tm, ATT_WIDTH), lambda i: (i, gatt_blk)), whole, whole, whole, whole],
        out_specs=(o_spec, o_spec, *[res_spec(r, 128) for r in DILATIONS]),
        out_shape=(jax.ShapeDtypeStruct((S, ATT_WIDTH), F32), jax.ShapeDtypeStruct((S, ATT_WIDTH), BF16),
                   *[jax.ShapeDtypeStruct((r, S // r, 128), F32) for r in DILATIONS]),
        compiler_params=_params(("parallel",)))(*outs, *stats, rest, *back, *forth)


ATTN_BWD_PAIRS = 4


def _attn_bwd(qkv, do, lse, dvec, g, r, hosted=None):
    L = qkv.shape[1]
    nb = L // BAND
    pairs = min(ATTN_BWD_PAIRS, nb // 2)
    kb = 2 * pairs
    n_steps = nb // kb
    scale = HEAD_DIM ** -0.5

    def body(qc_ref, qn_ref, k_ref, v_ref, doc_ref, don_ref, lc_ref, ln_ref, dc_ref, dn_ref,
             out_ref, carry_ref):
        j = pl.program_id(1)

        @pl.when(j == 0)
        def _():
            carry_ref[...] = jnp.zeros_like(carry_ref)

        ii = lax.broadcasted_iota(jnp.int32, (3 * BAND, 2 * BAND), 0)
        kk = lax.broadcasted_iota(jnp.int32, (3 * BAND, 2 * BAND), 1)
        band = jnp.logical_and(kk <= ii, kk >= ii - BAND)
        last_band = jnp.logical_and(band, jnp.logical_or(ii < 2 * BAND, j < n_steps - 1))
        lse_all = jnp.concatenate([lc_ref[...], ln_ref[...]], axis=0)
        dvec_all = jnp.concatenate([dc_ref[...], dn_ref[...]], axis=0)
        for h in range(HEADS):
            sl = slice(h * HEAD_DIM, (h + 1) * HEAD_DIM)
            q_all = jnp.concatenate([qc_ref[:, sl], qn_ref[:, sl]], axis=0)
            do_all = jnp.concatenate([doc_ref[:, sl], don_ref[:, sl]], axis=0)
            dq_prev = carry_ref[:, sl]
            for a in range(pairs):
                keys = slice(2 * a * BAND, (2 * a + 2) * BAND)
                rows = slice(2 * a * BAND, (2 * a + 3) * BAND)
                k, v, q, do = k_ref[keys, sl], v_ref[keys, sl], q_all[rows], do_all[rows]
                mask = last_band if a == pairs - 1 else band
                p = jnp.where(mask, jnp.exp(_dot_nt(q, k) * scale - lse_all[rows, h:h + 1]), 0.0)
                ds = (p * (_dot_nt(do, v) - dvec_all[rows, h:h + 1])).astype(BF16)
                dq3 = _dot(ds, k) * scale
                out_ref[0, 2 * a * BAND:(2 * a + 1) * BAND, sl] = (dq_prev + dq3[:BAND]).astype(BF16)
                out_ref[0, (2 * a + 1) * BAND:(2 * a + 2) * BAND, sl] = dq3[BAND:2 * BAND].astype(BF16)
                out_ref[1, keys, sl] = (_dot_tn(ds, q) * scale).astype(BF16)
                out_ref[2, keys, sl] = _dot_tn(p.astype(BF16), do).astype(BF16)
                dq_prev = dq3[2 * BAND:]
            carry_ref[:, sl] = dq_prev

    many = (None, kb * BAND, ATT_WIDTH)
    one = (None, BAND, ATT_WIDTH)
    smany = (None, kb * BAND, 128)
    sone = (None, BAND, 128)

    def nxt(j):
        return jnp.minimum(kb * j + kb, nb - 1)

    (out,), got = _call(
        body, (qkv, qkv, qkv, qkv, do, do, lse, lse, dvec, dvec), name=f"attn_bwd_g{g}", grid=(r, n_steps),
        in_specs=[
            pl.BlockSpec(many, lambda rho, j: (rho, j, 0)),
            pl.BlockSpec(one, lambda rho, j: (rho, nxt(j), 0)),
            pl.BlockSpec(many, lambda rho, j: (rho, j, 1)),
            pl.BlockSpec(many, lambda rho, j: (rho, j, 2)),
            pl.BlockSpec(many, lambda rho, j: (rho, j, 0)),
            pl.BlockSpec(one, lambda rho, j: (rho, nxt(j), 0)),
            pl.BlockSpec(smany, lambda rho, j: (rho, j, 0)),
            pl.BlockSpec(sone, lambda rho, j: (rho, nxt(j), 0)),
            pl.BlockSpec(smany, lambda rho, j: (rho, j, 0)),
            pl.BlockSpec(sone, lambda rho, j: (rho, nxt(j), 0)),
        ],
        out_specs=(pl.BlockSpec((3, None, kb * BAND, ATT_WIDTH), lambda rho, j: (0, rho, j, 0)),),
        out_shape=(jax.ShapeDtypeStruct((3, r, L, ATT_WIDTH), BF16),),
        scratch_shapes=[pltpu.VMEM((BAND, ATT_WIDTH), F32)],
        hosted=hosted)
    return out, got


LRU_T = 256


def _linear_scan(a, b, carry, reverse):
    T, C = a.shape
    a = a.reshape(T // 8, 8, C)
    b = b.reshape(T // 8, 8, C)
    row8 = lax.broadcasted_iota(jnp.int32, a.shape, 1)
    for s in (1, 2, 4):
        keep = (row8 < 8 - s) if reverse else (row8 >= s)
        shift = 8 - s if reverse else s
        b_sh = jnp.where(keep, pltpu.roll(b, shift, 1), 0.0)
        a_sh = jnp.where(keep, pltpu.roll(a, shift, 1), 1.0)
        b = a * b_sh + b
        a = a * a_sh
    tiles = [None] * (T // 8)
    order = range(T // 8 - 1, -1, -1) if reverse else range(T // 8)
    for k in order:
        y = b[k] + a[k] * carry
        tiles[k] = y
        carry = y[0:1] if reverse else y[7:8]
    return jnp.concatenate(tiles, axis=0), carry


def _gate_matmuls(ucb, w_ref, bias):
    parts = [_dot(ucb[:, j * 128:(j + 1) * 128], w_ref[j]) for j in range(8)]
    return jnp.concatenate(parts, axis=1) + bias


def _rows_before(x, prev8, j):
    T, C = x.shape
    rot = pltpu.roll(x.reshape(T // 8, 8, C), j, 1)
    rot_prev = jnp.concatenate([pltpu.roll(prev8, j, 0)[None], rot[:-1]], axis=0)
    row8 = lax.broadcasted_iota(jnp.int32, rot.shape, 1)
    return jnp.where(row8 >= j, rot, rot_prev).reshape(T, C)


def _rows_after(x, next8, j):
    T, C = x.shape
    rot = pltpu.roll(x.reshape(T // 8, 8, C), 8 - j, 1)
    rot_next = jnp.concatenate([rot[1:], pltpu.roll(next8, 8 - j, 0)[None]], axis=0)
    row8 = lax.broadcasted_iota(jnp.int32, rot.shape, 1)
    return jnp.where(row8 < 8 - j, rot, rot_next).reshape(T, C)


def _conv_fwd(u, u_prev8, cw_ref):
    uc = cw_ref[4:5, :] + cw_ref[0:1, :] * u
    for j in range(1, CONV_WIDTH):
        uc = uc + cw_ref[j:j + 1, :] * _rows_before(u, u_prev8, j)
    return uc


def _lru_fwd(rest, cw, wr2, wi2, hosted=None):
    S = rest.shape[0]
    T = LRU_T

    def body(u_ref, cw_ref, wr_ref, wi_ref, h_ref, a_ref, uc_ref, r_ref, ig_ref, mult_ref,
             ucar_ref, hcar_ref):
        c = pl.program_id(0)

        @pl.when(c == 0)
        def _():
            ucar_ref[...] = jnp.zeros_like(ucar_ref)
            hcar_ref[...] = jnp.zeros_like(hcar_ref)

        u = u_ref[...]
        uc = _conv_fwd(u, ucar_ref[...], cw_ref)
        ucar_ref[...] = u[T - 8:, :]
        ucb = uc.astype(BF16)
        r = jax.nn.sigmoid(_gate_matmuls(ucb, wr_ref, cw_ref[5:6, :]))
        ig = jax.nn.sigmoid(_gate_matmuls(ucb, wi_ref, cw_ref[6:7, :]))
        log_a = -LRU_C * r * _softplus_neg(cw_ref[7:8, :])
        a = jnp.exp(log_a)
        mult = jnp.sqrt(_neg_expm1(2.0 * log_a))
        b = mult * (ig * uc)
        a_ref[...] = a
        uc_ref[...] = uc
        r_ref[...] = r
        ig_ref[...] = ig
        mult_ref[...] = mult
        h, last = _linear_scan(a, b, hcar_ref[0:1, :], reverse=False)
        h_ref[...] = h
        hcar_ref[...] = jnp.broadcast_to(last, (8, D_MODEL))

    row_spec = pl.BlockSpec((T, D_MODEL), lambda c: (c, 0))
    row_shape = jax.ShapeDtypeStruct((S, D_MODEL), F32)
    return _call(
        body, (rest, cw, wr2, wi2), name="lru_fwd", grid=(S // T,),
        in_specs=[row_spec, pl.BlockSpec((8, D_MODEL), lambda c: (0, 0)),
                  pl.BlockSpec((8, 128, 128), lambda c: (0, 0, 0)),
                  pl.BlockSpec((8, 128, 128), lambda c: (0, 0, 0))],
        out_specs=(row_spec,) * 6, out_shape=(row_shape,) * 6,
        scratch_shapes=[pltpu.VMEM((8, D_MODEL), F32), pltpu.VMEM((8, D_MODEL), F32)],
        hosted=hosted)


def _lru_bwd(dhl, hl, a_all, gates, rest, cw, wr2, wi2, dproj, hosted=None):
    S = rest.shape[0]
    T = LRU_T
    nc = S // T

    def body(dh_ref, h_ref, hp_ref, a_ref, uc_ref, r_ref, ig_ref, mult_ref, u_ref, up_ref, cw_ref, wr_ref,
             wi_ref, _alias, du_ref, acc_ref, gwr_ref, gwi_ref, gcar_ref, acar_ref, dcar_ref):
        step = pl.program_id(0)
        c = nc - 1 - step

        @pl.when(step == 0)
        def _():
            acc_ref[...] = jnp.zeros_like(acc_ref)
            gwr_ref[...] = jnp.zeros_like(gwr_ref)
            gwi_ref[...] = jnp.zeros_like(gwi_ref)
            gcar_ref[...] = jnp.zeros_like(gcar_ref)
            acar_ref[...] = jnp.zeros_like(acar_ref)
            dcar_ref[...] = jnp.zeros_like(dcar_ref)

        u = u_ref[...]
        u_prev = jnp.where(c > 0, up_ref[...], 0.0)
        h_prev8 = jnp.where(c > 0, hp_ref[...], 0.0)
        a = a_ref[...]
        h = h_ref[...]
        uc, r, ig, mult = uc_ref[...], r_ref[...], ig_ref[...], mult_ref[...]
        ucb = uc.astype(BF16)
        sp = _softplus_neg(cw_ref[7:8, :])
        a_next = _rows_after(a, acar_ref[...], 1)
        G, first = _linear_scan(a_next, dh_ref[...], gcar_ref[0:1, :], reverse=True)
        gcar_ref[...] = jnp.broadcast_to(first, (8, D_MODEL))
        acar_ref[...] = jnp.broadcast_to(a[0:1, :], (8, D_MODEL))
        d_a = G * _rows_before(h, h_prev8, 1)
        d_mult = G * (ig * uc)
        d_ig = G * (mult * uc)
        duc = G * (mult * ig)
        d_log_a = d_a * a - d_mult * (a * a) / mult
        d_r = d_log_a * (-LRU_C * sp)
        d_sp = jnp.sum(d_log_a * (-LRU_C * r), axis=0, keepdims=True)
        dpre_r = d_r * r * (1.0 - r)
        dpre_i = d_ig * ig * (1.0 - ig)
        dprb = dpre_r.astype(BF16)
        dpib = dpre_i.astype(BF16)
        back = []
        for j in range(8):
            sl = slice(j * 128, (j + 1) * 128)
            back.append(_dot_nt(dprb[:, sl], wr_ref[j]) + _dot_nt(dpib[:, sl], wi_ref[j]))
            gwr_ref[j] += _dot_tn(ucb[:, sl], dprb[:, sl])
            gwi_ref[j] += _dot_tn(ucb[:, sl], dpib[:, sl])
        duc = duc + jnp.concatenate(back, axis=1)
        du = cw_ref[0:1, :] * duc
        duc_next8 = dcar_ref[...]
        acc_ref[0:1, :] += jnp.sum(duc * u, axis=0, keepdims=True)
        for j in range(1, CONV_WIDTH):
            du = du + cw_ref[j:j + 1, :] * _rows_after(duc, duc_next8, j)
            acc_ref[j:j + 1, :] += jnp.sum(duc * _rows_before(u, u_prev, j), axis=0, keepdims=True)
        dcar_ref[...] = duc[0:8, :]
        acc_ref[4:5, :] += jnp.sum(duc, axis=0, keepdims=True)
        acc_ref[5:6, :] += jnp.sum(dpre_r, axis=0, keepdims=True)
        acc_ref[6:7, :] += jnp.sum(dpre_i, axis=0, keepdims=True)
        acc_ref[7:8, :] += d_sp
        du_ref[0] = du[:, 0:ATT_WIDTH].astype(BF16)
        du_ref[1] = du[:, ATT_WIDTH:].astype(BF16)

    def rev(step):
        return nc - 1 - step

    def prev8(step):
        return jnp.maximum(rev(step) * (T // 8) - 1, 0)

    row_spec = pl.BlockSpec((T, D_MODEL), lambda s: (rev(s), 0))
    prev_spec = pl.BlockSpec((8, D_MODEL), lambda s: (prev8(s), 0))
    vec_spec = pl.BlockSpec((8, D_MODEL), lambda s: (0, 0))
    w_spec = pl.BlockSpec((8, 128, 128), lambda s: (0, 0, 0))
    return _call(
        body, (dhl, hl, hl, a_all, *gates, rest, rest, cw, wr2, wi2, dproj), name="lru_bwd", grid=(nc,),
        in_specs=[row_spec, row_spec, prev_spec, row_spec, row_spec, row_spec, row_spec, row_spec,
                  row_spec, prev_spec, vec_spec, w_spec, w_spec, ANY_SPEC],
        out_specs=(pl.BlockSpec((2, T, ATT_WIDTH), lambda s: (4, rev(s), 0)), vec_spec, w_spec, w_spec),
        out_shape=(jax.ShapeDtypeStruct(dproj.shape, BF16), jax.ShapeDtypeStruct((8, D_MODEL), F32),
                   jax.ShapeDtypeStruct((8, 128, 128), F32), jax.ShapeDtypeStruct((8, 128, 128), F32)),
        scratch_shapes=[pltpu.VMEM((8, D_MODEL), F32), pltpu.VMEM((8, D_MODEL), F32),
                        pltpu.VMEM((8, D_MODEL), F32)],
        aliases={13: 0}, hosted=hosted)


OUT_TM = 256


def _out_fwd(a_gated, hl, rest, x, vec, w_pa, w_pb, w_o, target=None, hosted=None):
    S = x.shape[0]
    tm = OUT_TM
    with_loss = target is not None

    def body(*refs):
        (ag_ref, hl_ref, gl_ref, ma_ref, mb_ref, x_ref, v_ref, wpa_ref, wpb_ref, wo_ref), refs = refs[:10], refs[10:]
        if with_loss:
            t_ref, refs = refs[0], refs[1:]
        xn_ref, mix_ref, ya_ref, yb_ref, bg_ref = refs[:5]
        bg = (hl_ref[...] * _silu(gl_ref[...])).astype(BF16)
        ya = _dot(ag_ref[...], wpa_ref[...])
        yb = _dot(bg, wpb_ref[...])
        mix = (jax.nn.sigmoid(ma_ref[...]) * ya + jax.nn.sigmoid(mb_ref[...]) * yb).astype(BF16)
        out = _dot(mix, wo_ref[...])
        rstd = lax.rsqrt(jnp.mean(out * out, axis=-1, keepdims=True) + NORM_EPS)
        x_new = x_ref[...] + v_ref[1:2, :] * ((out * rstd) * v_ref[0:1, :])
        if with_loss:
            acc_ref = refs[5]

            @pl.when(pl.program_id(0) == 0)
            def _():
                acc_ref[...] = jnp.zeros_like(acc_ref)

            err = x_new - t_ref[...]
            xn_ref[...] = err * (1.0 / D_MODEL)
            part = jnp.sum(jnp.sum(err * err, axis=1, keepdims=True), axis=0, keepdims=True)
            acc_ref[...] += jnp.broadcast_to(part, acc_ref.shape)
        else:
            xn_ref[...] = x_new
        mix_ref[...] = mix
        ya_ref[...] = ya.astype(BF16)
        yb_ref[...] = yb.astype(BF16)
        bg_ref[...] = bg

    def col(j):
        return pl.BlockSpec((tm, D_MODEL), lambda i: (i, j))

    def whole(shape):
        return pl.BlockSpec(shape, lambda i: (0, 0))

    row = col(0)
    bf = jax.ShapeDtypeStruct((S, D_MODEL), BF16)
    loss_in = ([target], [row]) if with_loss else ([], [])
    loss_out = ([jax.ShapeDtypeStruct((8, 128), F32)], [whole((8, 128))]) if with_loss else ([], [])
    return _call(
        body, (a_gated, hl, rest, rest, rest, x, vec, w_pa, w_pb, w_o, *loss_in[0]), name="out_fwd",
        grid=(S // tm,),
        in_specs=[pl.BlockSpec((tm, ATT_WIDTH), lambda i: (i, 0)), row, col(1), col(2), col(3), row,
                  whole((8, D_MODEL)), whole((ATT_WIDTH, D_MODEL)), whole((D_MODEL, D_MODEL)),
                  whole((D_MODEL, D_MODEL)), *loss_in[1]],
        out_specs=(row, row, row, row, row, *loss_out[1]),
        out_shape=(jax.ShapeDtypeStruct((S, D_MODEL), F32), bf, bf, bf, bf, *loss_out[0]),
        hosted=hosted)


def _out_bwd(dxn, mix, ya, yb, hl, rest, o_att, vec, w_pa, w_pb, w_o, hosted=None):
    S = dxn.shape[0]
    tm = OUT_TM
    forth = [_perm_matrix(r, tm) for r in DILATIONS[1:]]

    def body(dxn_ref, mix_ref, ya_ref, yb_ref, hl_ref, gl_ref, ma_ref, mb_ref, ga_ref, o_ref, v_ref,
             wpa_ref, wpb_ref, wo_ref, f1_ref, f2_ref,
             dout_ref, dya_ref, dyb_ref, dhl_ref, dp_ref, acc_ref,
             do0_ref, do1_ref, do2_ref, dv0_ref, dv1_ref, dv2_ref):
        i = pl.program_id(0)

        @pl.when(i == 0)
        def _():
            acc_ref[...] = jnp.zeros_like(acc_ref)

        g_post = v_ref[0:1, :]
        gate = v_ref[1:2, :]
        dxn_v = dxn_ref[...]
        out = _dot(mix_ref[...], wo_ref[...])
        rstd = lax.rsqrt(jnp.mean(out * out, axis=-1, keepdims=True) + NORM_EPS)
        nrm = out * rstd
        acc_ref[0:1, :] += jnp.sum(dxn_v * (nrm * g_post), axis=0, keepdims=True)
        acc_ref[1:2, :] += jnp.sum(dxn_v * gate * nrm, axis=0, keepdims=True)
        dn = dxn_v * (gate * g_post)
        dout = (rstd * (dn - nrm * jnp.mean(dn * nrm, axis=-1, keepdims=True))).astype(BF16)
        dout_ref[...] = dout
        dmix = _dot_nt(dout, wo_ref[...])
        for c in range(2):
            cs = slice(c * ATT_WIDTH, (c + 1) * ATT_WIDTH)
            dm = dmix[:, cs]
            sa = jax.nn.sigmoid(ma_ref[:, cs])
            sb = jax.nn.sigmoid(mb_ref[:, cs])
            dya_ref[:, cs] = (dm * sa).astype(BF16)
            dyb_ref[:, cs] = (dm * sb).astype(BF16)
            dp_ref[2 + c] = (dm * ya_ref[:, cs].astype(F32) * (sa * (1.0 - sa))).astype(BF16)
            dp_ref[4 + c] = (dm * yb_ref[:, cs].astype(F32) * (sb * (1.0 - sb))).astype(BF16)
        d_ag = _dot_nt(dya_ref[...], wpa_ref[...])
        d_bg = _dot_nt(dyb_ref[...], wpb_ref[...])
        for c in range(2):
            cs = slice(c * ATT_WIDTH, (c + 1) * ATT_WIDTH)
            gl = gl_ref[:, cs]
            dhl_ref[:, cs] = d_bg[:, cs] * _silu(gl)
            dp_ref[c] = (d_bg[:, cs] * hl_ref[:, cs] * _dsilu(gl)).astype(BF16)
        ga = ga_ref[...]
        o = o_ref[...]
        do = d_ag * _silu(ga)
        dga = d_ag * o * _dsilu(ga)
        lane = lax.broadcasted_iota(jnp.int32, (tm, 128), 1)
        dvec = jnp.zeros((tm, 128), F32)
        for h in range(HEADS):
            sl = slice(h * HEAD_DIM, (h + 1) * HEAD_DIM)
            dvec = jnp.where(lane == h, jnp.sum(do[:, sl] * o[:, sl], axis=1, keepdims=True), dvec)
        do_b = do.astype(BF16)
        do0_ref[0] = do_b
        dv0_ref[0] = dvec
        for do_ref, dv_ref, f_ref in ((do1_ref, dv1_ref, f1_ref), (do2_ref, dv2_ref, f2_ref)):
            _store_residues(do_ref, _dot(f_ref[...], do_b).astype(BF16))
            _store_residues(dv_ref, _permute_f32(f_ref[...], dvec, 3))
        dp_ref[6] = dga.astype(BF16)

    def col(j):
        return pl.BlockSpec((tm, D_MODEL), lambda i: (i, j))

    def whole(shape):
        return pl.BlockSpec(shape, lambda i: (0, 0))

    def res_spec(r, width):
        return pl.BlockSpec((r, tm // r, width), lambda i: (0, i, 0))

    row = col(0)
    att = pl.BlockSpec((tm, ATT_WIDTH), lambda i: (i, 0))
    bf = jax.ShapeDtypeStruct((S, D_MODEL), BF16)
    vmem = pl.BlockSpec(memory_space=pltpu.VMEM)
    return _call(
        body, (dxn, mix, ya, yb, hl, rest, rest, rest, rest, o_att, vec, w_pa, w_pb, w_o, *forth),
        name="out_bwd", grid=(S // tm,),
        in_specs=[row, row, row, row, row, col(1), col(2), col(3),
                  pl.BlockSpec((tm, ATT_WIDTH), lambda i: (i, 8)), att,
                  whole((8, D_MODEL)), whole((ATT_WIDTH, D_MODEL)), whole((D_MODEL, D_MODEL)),
                  whole((D_MODEL, D_MODEL)), vmem, vmem],
        out_specs=(row, row, row, row,
                   pl.BlockSpec((7, tm, ATT_WIDTH), lambda i: (0, i, 0)), whole((8, D_MODEL)),
                   *[res_spec(r, ATT_WIDTH) for r in DILATIONS], *[res_spec(r, 128) for r in DILATIONS]),
        out_shape=(bf, bf, bf, jax.ShapeDtypeStruct((S, D_MODEL), F32),
                   jax.ShapeDtypeStruct((DREST_CHUNKS, S, ATT_WIDTH), BF16),
                   jax.ShapeDtypeStruct((8, D_MODEL), F32),
                   *[jax.ShapeDtypeStruct((r, S // r, ATT_WIDTH), BF16) for r in DILATIONS],
                   *[jax.ShapeDtypeStruct((r, S // r, 128), F32) for r in DILATIONS]),
        hosted=hosted)


def _adamw(gsrcs, w, m, v, n_stack, name, rows=None):
    n_layers, R, C = w.shape
    assert len(gsrcs) == n_layers
    budget = 96 * 1024
    tr = rows
    if tr is None:
        tr = R
        while tr * C > budget and tr % 16 == 0:
            tr //= 2
    assert R % tr == 0 and (tr % 8 == 0 or tr == R)
    c1 = 1.0 - ADAM_B1 ** ADAM_STEP
    c2 = 1.0 - ADAM_B2 ** ADAM_STEP

    def body(*refs):
        g_refs = refs[:n_layers]
        w_ref, m_ref, v_ref, go_ref, d_ref, mo_ref, vo_ref = refs[n_layers:]

        def update(g_ref):
            if n_stack:
                g = g_ref[0].astype(F32)
                for s in range(1, n_stack):
                    g = g + g_ref[s].astype(F32)
            else:
                g = g_ref[...]
            m_new = ADAM_B1 * m_ref[...] + (1.0 - ADAM_B1) * g
            v_new = ADAM_B2 * v_ref[...] + (1.0 - ADAM_B2) * (g * g)
            m_hat = m_new / c1
            v_hat = v_new / c2
            go_ref[...] = g
            d_ref[...] = -ADAM_LR * (m_hat / (jnp.sqrt(v_hat) + ADAM_EPS) + ADAM_WD * w_ref[...])
            mo_ref[...] = m_new
            vo_ref[...] = v_new

        for layer in range(n_layers):
            pl.when(pl.program_id(0) == layer)(functools.partial(update, g_refs[layer]))

    def g_spec(layer):
        def rows_of(l, i):
            return jnp.where(l == layer, i, 0)
        if n_stack:
            return pl.BlockSpec((n_stack, tr, C), lambda l, i: (0, rows_of(l, i), 0))
        return pl.BlockSpec((tr, C), lambda l, i: (rows_of(l, i), 0))

    blk = pl.BlockSpec((None, tr, C), lambda l, i: (l, i, 0))
    shp = jax.ShapeDtypeStruct((n_layers, R, C), F32)
    return pl.pallas_call(
        body, name=name, grid=(n_layers, R // tr),
        in_specs=[g_spec(layer) for layer in range(n_layers)] + [blk, blk, blk],
        out_specs=(blk, blk, blk, blk), out_shape=(shp, shp, shp, shp),
        compiler_params=_params(("arbitrary", "arbitrary")))(*gsrcs, w, m, v)


def _pair_blocks(w):
    w = w.reshape(8, 2, 64, 64)
    z = jnp.zeros((8, 64, 64), w.dtype)
    top = jnp.concatenate([w[:, 0], z], axis=2)
    bot = jnp.concatenate([z, w[:, 1]], axis=2)
    return jnp.concatenate([top, bot], axis=1).astype(BF16)


def _unpair_blocks(g):
    return jnp.stack([g[:, :64, :64], g[:, 64:, 64:]], axis=1).reshape(16, 64, 64)


def _mats_full(w_pa_g, w_pb_g, w_o_g):
    return dict(w_pa=jnp.transpose(w_pa_g, (1, 0, 2)).reshape(ATT_WIDTH, D_MODEL),
                w_pb=w_pb_g.reshape(D_MODEL, D_MODEL), w_o=w_o_g.reshape(D_MODEL, D_MODEL))


def _layer_params(layer, w_in_g, mats_g, conv_w_full, conv_b, b_rg, b_ig, lru_lambda, w_rg, w_ig, g_pre,
                  g_post):
    cw = jnp.concatenate([conv_w_full, conv_b[None], b_rg[None], b_ig[None], lru_lambda[None]], axis=0)
    p = dict(w_int=_weights_full(w_in_g), cw=cw, wr2=_pair_blocks(w_rg), wi2=_pair_blocks(w_ig),
             g_pre=g_pre, g_post=g_post, lru_lambda=lru_lambda)
    if mats_g is not None:
        p.update(_mats_full(*mats_g))
    return p


def _layer_fwd(x, mod, p, host_proj=None, host_lru=None, host_out=None, own_mats=None, target=None):
    zeros = jnp.zeros((5, D_MODEL), F32)
    vec_pre = jnp.concatenate([p["g_pre"][None], mod[1:2], mod[0:1], zeros], axis=0)
    (ht0, ht1, ht2, q0, q1, q2, rest), got_proj = _norm_proj_fwd(x, vec_pre, p["w_int"], hosted=host_proj)
    h_t = [ht0, ht1, ht2]
    hosts = [None, None, None]
    if own_mats is not None:
        hosts[0], hosts[1] = (list(own_mats[:2]), ["ag"] * 2), ([own_mats[2]], ["ag"])
    qkv, outs, stats, got_attn = [q0, q1, q2], [], [], []
    for g, r in enumerate(DILATIONS):
        (o_g, st_g), got = _attn_fwd(qkv[g], g, r, hosted=hosts[g])
        outs.append(o_g)
        stats.append(st_g)
        got_attn += got
    if own_mats is not None:
        p.update(_mats_full(*got_attn))
    o_att, a_gated, *lse = _attn_combine(outs, stats, rest)
    (hl, a_dec, *gates), got_lru = _lru_fwd(rest, p["cw"], p["wr2"], p["wi2"], hosted=host_lru)
    vec_post = jnp.concatenate([p["g_post"][None], mod[2:3], jnp.zeros((6, D_MODEL), F32)], axis=0)
    (x_new, mix, ya, yb, b_gated, *loss_acc), got_out = _out_fwd(
        a_gated, hl, rest, x, vec_post, p["w_pa"], p["w_pb"], p["w_o"], target=target, hosted=host_out)
    saved = dict(x=x, h_t=h_t, qkv=qkv, rest=rest, o_att=o_att, a_gated=a_gated, lse=lse, hl=hl, a_dec=a_dec,
                 gates=gates, mix=mix, ya=ya, yb=yb, b_gated=b_gated, vec_pre=vec_pre, vec_post=vec_post)
    return (x_new, *loss_acc), saved, got_proj, got_lru, got_out


def _mats_sources(g_pa, g_pb, g_o):
    return [jnp.transpose(g_pa.astype(BF16).reshape(ATT_WIDTH, N_DEV, 128), (1, 0, 2)),
            g_pb.astype(BF16).reshape(N_DEV, 128, D_MODEL), g_o.astype(BF16).reshape(N_DEV, 128, D_MODEL)]


LRU_SMALL = ("conv_b", "w_rg", "b_rg", "w_ig", "b_ig", "lru_lambda")
MOD_SMALL = ("b_mod", "g_pre", "g_post")


def _flat_rows(arrs):
    return jnp.concatenate([a.reshape(-1) for a in arrs]).reshape(-1, 128)


def _layer_bwd(dxn, sv, p, upper_small=None):
    a2a3 = ["a2a"] * 3
    (dout, dya, dyb, dhl, drest, acc_out, *do_dvec), _ = _out_bwd(
        dxn, sv["mix"], sv["ya"], sv["yb"], sv["hl"], sv["rest"], sv["o_att"], sv["vec_post"],
        p["w_pa"], p["w_pb"], p["w_o"])
    mats = _mats_sources(_wgrad_tn(sv["a_gated"], dya, "wgrad_pa"), _wgrad_tn(sv["b_gated"], dyb, "wgrad_pb"),
                         _wgrad_tn(sv["mix"], dout, "wgrad_o"))
    (drest, acc_lru, gwr2, gwi2), _ = _lru_bwd(
        dhl, sv["hl"], sv["a_dec"], sv["gates"], sv["rest"], p["cw"], p["wr2"], p["wi2"], drest)
    small = dict(
        conv_w=acc_lru[0:4], conv_b=acc_lru[4], b_rg=acc_lru[5], b_ig=acc_lru[6],
        lru_lambda=acc_lru[7] * (-jax.nn.sigmoid(-p["lru_lambda"])),
        w_rg=_unpair_blocks(gwr2), w_ig=_unpair_blocks(gwi2))
    hosts = [(mats, a2a3), None, None]
    if upper_small is not None:
        for half, names in enumerate((LRU_SMALL[:3], LRU_SMALL[3:] + ("conv_w",))):
            vec = _flat_rows([jnp.stack([small[n], upper_small[n]]) for n in names])
            hosts[1 + half] = ([vec], ["ag"])
    dqkv, got_attn = [], []
    for g, r in enumerate(DILATIONS):
        d_g, got = _attn_bwd(sv["qkv"][g], do_dvec[g], sv["lse"][g], do_dvec[3 + g], g, r, hosted=hosts[g])
        dqkv.append(d_g)
        got_attn.append(got)
    shards = None
    for g, r in enumerate(DILATIONS):
        shards = _proj_wgrad_group(sv["h_t"][g], dqkv[g], g, r, shards, f"proj_wgrad_g{g}")
    shards = _proj_wgrad_part(sv["h_t"][0], drest, 2, 4, QKV_CHUNKS, shards, "proj_wgrad_u")
    shards = _proj_wgrad_part(sv["h_t"][0], drest, 7, 0, QKV_CHUNKS + 2, shards, "proj_wgrad_rest")
    (dx, acc_pre), got_in = _proj_dgrad_norm_bwd(
        dqkv, drest, p["w_int"], sv["x"], dxn, sv["vec_pre"], hosted=([shards], ["a2a"]))
    small.update(dmod=jnp.concatenate([acc_pre[0], acc_pre[1], acc_out[0]]), g_pre=acc_pre[2],
                 g_post=acc_out[1])
    return dx, small, dict(mats=got_attn[0], w_in=got_in[0], lru_small=got_attn[1] + got_attn[2])


def kernel(x, c, w_mod, b_mod, g_pre, w_in, conv_w, conv_b, w_rg, b_rg, w_ig, b_ig, lru_lambda, w_pa, w_pb, w_o, g_post, loss_target, m_w_mod, m_b_mod, m_g_pre, m_w_in, m_conv_w, m_conv_b, m_w_rg, m_b_rg, m_w_ig, m_b_ig, m_lru_lambda, m_w_pa, m_w_pb, m_w_o, m_g_post, v_w_mod, v_b_mod, v_g_pre, v_w_in, v_conv_w, v_conv_b, v_w_rg, v_b_rg, v_w_ig, v_b_ig, v_lru_lambda, v_w_pa, v_w_pb, v_w_o, v_g_post):
    W = dict(w_mod=w_mod, b_mod=b_mod, g_pre=g_pre, w_in=w_in, conv_w=conv_w, conv_b=conv_b, w_rg=w_rg,
             b_rg=b_rg, w_ig=w_ig, b_ig=b_ig, lru_lambda=lru_lambda, w_pa=w_pa, w_pb=w_pb, w_o=w_o,
             g_post=g_post)
    M = dict(w_mod=m_w_mod, b_mod=m_b_mod, g_pre=m_g_pre, w_in=m_w_in, conv_w=m_conv_w, conv_b=m_conv_b,
             w_rg=m_w_rg, b_rg=m_b_rg, w_ig=m_w_ig, b_ig=m_b_ig, lru_lambda=m_lru_lambda, w_pa=m_w_pa,
             w_pb=m_w_pb, w_o=m_w_o, g_post=m_g_post)
    V = dict(w_mod=v_w_mod, b_mod=v_b_mod, g_pre=v_g_pre, w_in=v_w_in, conv_w=v_conv_w, conv_b=v_conv_b,
             w_rg=v_w_rg, b_rg=v_b_rg, w_ig=v_w_ig, b_ig=v_b_ig, lru_lambda=v_lru_lambda, w_pa=v_w_pa,
             w_pb=v_w_pb, w_o=v_w_o, g_post=v_g_post)
    S = x.shape[1]
    me = 4 * lax.axis_index("x") + 2 * lax.axis_index("y") + lax.axis_index("c")
    n_mod = w_mod.shape[2]
    n_in = w_in.shape[2]
    n_conv = conv_w.shape[2]

    c_rows = jnp.broadcast_to(c, (8, D_MODEL))
    w_in_b, w_pa_b, w_pb_b, w_o_b = (t.astype(BF16) for t in (w_in, w_pa, w_pb, w_o))
    g_c, g_win0, g_cw = _gather_two_level([c_rows, w_in_b[0], conv_w], "gather_weights")
    c_all = g_c[:, 0, :]
    c_pad = jnp.concatenate([c_all, jnp.zeros((8, D_MODEL), F32)], axis=0)
    conv_w_full = jnp.transpose(g_cw, (1, 2, 0, 3)).reshape(2, CONV_WIDTH, D_MODEL)

    mod_cols = _mod_fwd(c_pad, w_mod)
    mod_src = jnp.transpose(mod_cols[:, :8], (1, 0, 2))
    mod_src = jnp.concatenate([mod_src, jnp.zeros((8, 6, n_mod), F32)], axis=1)
    (mod_got,) = _exchange([mod_src], ["a2a"], "scatter_mod")
    mod = jnp.transpose(mod_got[:, :2], (1, 0, 2)).reshape(2, 3 * D_MODEL) + b_mod

    def layer_params(layer, w_in_g, mats_g):
        return _layer_params(layer, w_in_g, mats_g, conv_w_full[layer], conv_b[layer], b_rg[layer],
                             b_ig[layer], lru_lambda[layer], w_rg[layer], w_ig[layer], g_pre[layer],
                             g_post[layer])

    layers = [layer_params(0, g_win0, None), None]
    (act,), sv0, (g_win1,), _, mats1_g = _layer_fwd(
        x[0], mod[0].reshape(3, D_MODEL), layers[0], own_mats=(w_pa_b[0], w_pb_b[0], w_o_b[0]),
        host_proj=([w_in_b[1]], ["ag"]), host_out=([w_pa_b[1], w_pb_b[1], w_o_b[1]], ["ag"] * 3))
    layers[1] = layer_params(1, g_win1, mats1_g)
    (dy, loss_acc), sv1, _, _, _ = _layer_fwd(act, mod[1].reshape(3, D_MODEL), layers[1],
                                              target=loss_target[0])

    dy, small1, got1 = _layer_bwd(dy, sv1, layers[1])
    dy, small0, got0 = _layer_bwd(dy, sv0, layers[0], upper_small=small1)
    grad_x = dy[None]
    r_pa, r_pb, r_o = ([got0["mats"][k], got1["mats"][k]] for k in range(3))
    r_in = [got0["w_in"], got1["w_in"]]
    grads = [small0, small1]

    def stack2(name):
        return jnp.stack([grads[0][name], grads[1][name]], axis=0)

    dmod = stack2("dmod")
    dmod_src = jnp.transpose(dmod.reshape(2, N_DEV, n_mod), (1, 0, 2))
    dmod_src = jnp.concatenate([dmod_src, jnp.zeros((8, 6, n_mod), F32)], axis=1)
    mod_vec = _flat_rows([dmod, stack2("g_pre"), stack2("g_post"), loss_acc])
    r_dmod, r_mod_small = _exchange([dmod_src, mod_vec], ["a2a", "ag"], "exchange_grads")
    mod_rows = mod_vec.shape[0] - loss_acc.shape[0]
    loss = 0.5 * jnp.sum(r_mod_small[:, mod_rows, 0]) / D_MODEL
    r_lru_small = got0["lru_small"]

    res = {}
    res["w_in"] = _adamw(r_in, w_in, m_w_in, v_w_in, 8, "adamw_w_in")
    res["w_pa"] = _adamw(r_pa, w_pa, m_w_pa, v_w_pa, 8, "adamw_w_pa")
    res["w_pb"] = _adamw(r_pb, w_pb, m_w_pb, v_w_pb, 8, "adamw_w_pb")
    res["w_o"] = _adamw(r_o, w_o, m_w_o, v_w_o, 8, "adamw_w_o")
    dmod_all = jnp.transpose(r_dmod[:, :2], (1, 0, 2))
    gw_mod = _mod_wgrad(jnp.transpose(c_all), dmod_all)
    res["w_mod"] = _adamw([gw_mod[0], gw_mod[1]], w_mod, m_w_mod, v_w_mod, 0, "adamw_w_mod")
    for names_rep, stack, tag in ((LRU_SMALL[:3], r_lru_small[0], "lru_a"), (LRU_SMALL[3:], r_lru_small[1], "lru_b"),
                                  (MOD_SMALL, r_mod_small, "mod")):
        rows = sum(W[n].size for n in names_rep) // 128
        rep = _adamw([stack], *(_flat_rows([src[n] for n in names_rep])[None] for src in (W, M, V)), 8,
                     f"adamw_small_{tag}", rows=rows // 2 if rows % 16 == 0 else rows)
        off = 0
        for name in names_rep:
            size = W[name].size
            res[name] = tuple(t.reshape(-1)[off:off + size].reshape(W[name].shape) for t in rep)
            off += size
    lru_rows = sum(W[n].size for n in LRU_SMALL[3:]) // 128
    conv_stack = r_lru_small[1][:, lru_rows:].reshape(8, 2, CONV_WIDTH, D_MODEL)
    conv_stack = lax.dynamic_slice_in_dim(conv_stack, me * n_conv, n_conv, axis=3).reshape(8, 8, n_conv)
    res["conv_w"] = _adamw([conv_stack], conv_w.reshape(1, 8, n_conv), m_conv_w.reshape(1, 8, n_conv),
                           v_conv_w.reshape(1, 8, n_conv), 8, "adamw_conv_w")

    names = ("w_mod", "b_mod", "g_pre", "w_in", "conv_w", "conv_b", "w_rg", "b_rg", "w_ig", "b_ig",
             "lru_lambda", "w_pa", "w_pb", "w_o", "g_post")
    outs = [loss, grad_x]
    for k in range(4):
        outs.extend(res[n][k].reshape(W[n].shape) for n in names)
    return tuple(outs)
```
